```python
import math
import jax
import jax.numpy as jnp
from jax import lax
import numpy as np

D_MODEL = 2048
BATCH = 2
SEQ = 8192
DEPTH = 2

HEAD_DIM = 128
N_SLOTS = D_MODEL // HEAD_DIM
N_EVEN = (DEPTH + 1) // 2
N_ODD = DEPTH // 2
NSA_HEADS = 8
NSA_GROUPS = 2
NSA_HPG = NSA_HEADS // NSA_GROUPS
CMP_BLOCK = 32
CMP_STRIDE = 16
CMP_HIDDEN = 256
SEL_BLOCK = 64
SEL_TOP_N = 16
WINDOW = 512
MLA_HEADS = 8
MLA_Q_RANK = 512
MLA_KV_RANK = 256
MLA_NOPE = 128
MLA_ROPE = 64
MLA_V = 128
DSA_HEADS = 16
DSA_KV_HEADS = 4
IDX_HEADS = 16
IDX_DIM = 64
IDX_ROPE = 32
DSA_TOPK_MAX = 256
REL_BUCKETS = 32
REL_MAX_DIST = 128
FFN_HIDDEN = 4096
CONV_WIDTH = 3
ROPE_THETA = 10000.0
EPS = 1e-6
Q_BLOCK = 128
NEG = -1e30
FORCE = 1e9

EVEN_COLS = NSA_HEADS * HEAD_DIM + 6 * NSA_GROUPS * HEAD_DIM + 3 * NSA_HEADS + MLA_Q_RANK + MLA_KV_RANK + MLA_ROPE
EVEN_OUT = NSA_HEADS * HEAD_DIM + MLA_HEADS * MLA_V
ODD_COLS = DSA_HEADS * HEAD_DIM + 2 * DSA_KV_HEADS * HEAD_DIM + IDX_HEADS * IDX_DIM + IDX_DIM + IDX_HEADS
ODD_OUT = DSA_HEADS * HEAD_DIM

kernel_name = "hybrid_nsa_mla_dsa_convffn"


def _offsets(sizes):
    return [int(v) for v in np.cumsum(sizes)[:-1]]


def rms_norm(x, g):
    xf = x.astype(jnp.float32)
    y = xf * lax.rsqrt(jnp.mean(xf * xf, axis=-1, keepdims=True) + EPS)
    return (y * g.astype(jnp.float32)).astype(x.dtype)


def masked_softmax(logits, mask):
    lf = jnp.where(mask, logits.astype(jnp.float32), NEG)
    p = jax.nn.softmax(lf, axis=-1)
    return p * jnp.any(mask, axis=-1, keepdims=True)


def t5_bucket(dist):
    n = jnp.maximum(dist, 0)
    max_exact = REL_BUCKETS // 2
    nf = jnp.maximum(n, 1).astype(jnp.float32)
    large = max_exact + (jnp.log(nf / max_exact) / math.log(REL_MAX_DIST / max_exact)
                         * (REL_BUCKETS - max_exact)).astype(jnp.int32)
    large = jnp.minimum(large, REL_BUCKETS - 1)
    return jnp.where(n < max_exact, n, large)


def rope_tables(seq_len, dim):
    half = dim // 2
    inv = ROPE_THETA ** (-jnp.arange(half, dtype=jnp.float32) / half)
    ang = jnp.arange(seq_len, dtype=jnp.float32)[:, None] * inv[None, :]
    return jnp.cos(ang), jnp.sin(ang)


def apply_rope(x, cos, sin):
    half = x.shape[-1] // 2
    x1 = x[..., :half].astype(jnp.float32)
    x2 = x[..., half:].astype(jnp.float32)
    return jnp.concatenate([x1 * cos - x2 * sin, x1 * sin + x2 * cos], axis=-1).astype(x.dtype)


def ada_modulate(c, w, b):
    m = jax.nn.silu(c) @ w + b
    shift, scale, gate = jnp.split(m, 3, axis=-1)
    return shift[:, None], scale[:, None], gate[:, None]


def compress(x, pe, w1, b1, w2, b2):
    B_, S_, G_, d = x.shape
    ch = x.reshape(B_, S_ // CMP_STRIDE, CMP_STRIDE, G_, d)
    blk = jnp.concatenate([ch[:, :-1], ch[:, 1:]], axis=2) + pe[None, None, :, None, :]
    nc = blk.shape[1]
    flat = jnp.moveaxis(blk, 3, 2).reshape(B_, nc, G_, CMP_BLOCK * d)
    return jax.nn.gelu(flat @ w1 + b1) @ w2 + b2


def nsa_attention(q, k_cmp, v_cmp, k_slc, v_slc, k_win, v_win, gates, tbl):
    B_, S_, H_, d = q.shape
    G_, hpg = NSA_GROUPS, NSA_HPG
    nc = k_cmp.shape[1]
    ns = S_ // SEL_BLOCK
    n_sel = min(SEL_TOP_N, ns)
    nk_sel = n_sel * SEL_BLOCK
    scale = d ** -0.5
    cmp_start = jnp.arange(nc) * CMP_STRIDE
    cmp_end = cmp_start + CMP_BLOCK - 1
    blk_ids = jnp.arange(ns)
    sel_start = blk_ids * SEL_BLOCK
    overlap = jnp.clip(jnp.minimum(cmp_start[:, None] + CMP_BLOCK, sel_start[None, :] + SEL_BLOCK)
                       - jnp.maximum(cmp_start[:, None], sel_start[None, :]), 0, None
                       ).astype(jnp.float32) / CMP_BLOCK
    ks_blocks = k_slc.reshape(B_, ns, SEL_BLOCK, G_, d).transpose(0, 3, 1, 2, 4)
    vs_blocks = v_slc.reshape(B_, ns, SEL_BLOCK, G_, d).transpose(0, 3, 1, 2, 4)
    kw_pad = jnp.pad(k_win, ((0, 0), (WINDOW, 0), (0, 0), (0, 0)))
    vw_pad = jnp.pad(v_win, ((0, 0), (WINDOW, 0), (0, 0), (0, 0)))
    tbl_heads = tbl.reshape(REL_BUCKETS, G_, hpg)
    tbl_g = tbl_heads.transpose(1, 0, 2)
    b_ix = jnp.arange(B_)[:, None, None, None]
    g_ix = jnp.arange(G_)[None, :, None, None]
    off_blk = jnp.arange(SEL_BLOCK)
    off_win = jnp.arange(Q_BLOCK + WINDOW)

    def block(bi):
        q0 = bi * Q_BLOCK
        t = q0 + jnp.arange(Q_BLOCK)
        qb = lax.dynamic_slice_in_dim(q, q0, Q_BLOCK, axis=1).reshape(B_, Q_BLOCK, G_, hpg, d)
        gb = jax.nn.sigmoid(lax.dynamic_slice_in_dim(gates, q0, Q_BLOCK, axis=1).astype(jnp.float32)
                            ).reshape(B_, Q_BLOCK, G_, hpg, 3)
        dist_c = t[:, None] - cmp_end[None, :]
        bias_c = tbl_heads[t5_bucket(dist_c)].transpose(2, 3, 0, 1)
        s_c = jnp.einsum('bqghd,bkgd->bghqk', qb, k_cmp) * scale + bias_c
        p_c = masked_softmax(s_c, dist_c >= 0)
        o_c = jnp.einsum('bghqk,bkgd->bqghd', p_c.astype(v_cmp.dtype), v_cmp)
        imp = jnp.einsum('bghqk,kn->bgqn', p_c, overlap)
        tb = t // SEL_BLOCK
        forced = (blk_ids[None, :] == 0) | (blk_ids[None, :] == tb[:, None]) | (blk_ids[None, :] == tb[:, None] - 1)
        causal_blk = sel_start[None, :] <= t[:, None]
        score = jnp.where(forced, FORCE, jnp.where(causal_blk, imp, NEG))
        _, idx = lax.top_k(score, n_sel)
        k_g = ks_blocks[b_ix, g_ix, idx].reshape(B_, G_, Q_BLOCK, nk_sel, d)
        v_g = vs_blocks[b_ix, g_ix, idx].reshape(B_, G_, Q_BLOCK, nk_sel, d)
        pos_s = (idx[..., None] * SEL_BLOCK + off_blk).reshape(B_, G_, Q_BLOCK, nk_sel)
        dist_s = t[None, None, :, None] - pos_s
        bias_s = jnp.moveaxis(tbl_g[g_ix, t5_bucket(dist_s)], -1, 2)
        s_s = jnp.einsum('bqghd,bgqkd->bghqk', qb, k_g) * scale + bias_s
        p_s = masked_softmax(s_s, (dist_s >= 0)[:, :, None])
        o_s = jnp.einsum('bghqk,bgqkd->bqghd', p_s.astype(v_g.dtype), v_g)
        kw = lax.dynamic_slice_in_dim(kw_pad, q0, Q_BLOCK + WINDOW, axis=1)
        vw = lax.dynamic_slice_in_dim(vw_pad, q0, Q_BLOCK + WINDOW, axis=1)
        pos_w = q0 - WINDOW + off_win
        dist_w = t[:, None] - pos_w[None, :]
        mask_w = (pos_w[None, :] >= 0) & (dist_w >= 0) & (dist_w < WINDOW)
        bias_w = tbl_heads[t5_bucket(dist_w)].transpose(2, 3, 0, 1)
        s_w = jnp.einsum('bqghd,bkgd->bghqk', qb, kw) * scale + bias_w
        p_w = masked_softmax(s_w, mask_w)
        o_w = jnp.einsum('bghqk,bkgd->bqghd', p_w.astype(vw.dtype), vw)
        o = gb[..., 0:1] * o_c + gb[..., 1:2] * o_s + gb[..., 2:3] * o_w
        return o.reshape(B_, Q_BLOCK, H_ * d).astype(q.dtype)

    out = lax.map(block, jnp.arange(S_ // Q_BLOCK))
    return out.transpose(1, 0, 2, 3).reshape(B_, S_, H_ * d)


def mla_attention(q_nope, q_pe, k_nope, k_pe, v):
    B_, S_, H_, _ = q_nope.shape
    dv = v.shape[-1]
    scale = (MLA_NOPE + MLA_ROPE) ** -0.5
    kpos = jnp.arange(S_)

    def block(bi):
        q0 = bi * Q_BLOCK
        t = q0 + jnp.arange(Q_BLOCK)
        qn = lax.dynamic_slice_in_dim(q_nope, q0, Q_BLOCK, axis=1)
        qr = lax.dynamic_slice_in_dim(q_pe, q0, Q_BLOCK, axis=1)
        s = (jnp.einsum('bqhd,bkhd->bhqk', qn, k_nope) + jnp.einsum('bqhd,bkd->bhqk', qr, k_pe)) * scale
        p = masked_softmax(s, kpos[None, :] <= t[:, None])
        o = jnp.einsum('bhqk,bkhd->bqhd', p.astype(v.dtype), v)
        return o.reshape(B_, Q_BLOCK, H_ * dv)

    out = lax.map(block, jnp.arange(S_ // Q_BLOCK))
    return out.transpose(1, 0, 2, 3).reshape(B_, S_, H_ * dv)


def dsa_attention(q, k, v, iq, ik, iw, tbl):
    B_, S_, H_, d = q.shape
    kvh = k.shape[2]
    hpg = H_ // kvh
    k_sel = min(DSA_TOPK_MAX, S_ // 4)
    scale = d ** -0.5
    kpos = jnp.arange(S_)
    b_ix = jnp.arange(B_)[:, None, None]

    def block(bi):
        q0 = bi * Q_BLOCK
        t = q0 + jnp.arange(Q_BLOCK)
        iqb = lax.dynamic_slice_in_dim(iq, q0, Q_BLOCK, axis=1)
        iwb = lax.dynamic_slice_in_dim(iw, q0, Q_BLOCK, axis=1).astype(jnp.float32)
        rel = jax.nn.relu(jnp.einsum('bqhd,bkd->bqhk', iqb, ik).astype(jnp.float32))
        score = jnp.einsum('bqhk,bqh->bqk', rel, iwb)
        score = jnp.where(kpos[None, None, :] <= t[None, :, None], score, NEG)
        _, idx = lax.top_k(score, k_sel)
        kg = k[b_ix, idx]
        vg = v[b_ix, idx]
        dist = t[None, :, None] - idx
        bias = tbl[t5_bucket(dist)].reshape(B_, Q_BLOCK, k_sel, kvh, hpg).transpose(0, 3, 4, 1, 2)
        qb = lax.dynamic_slice_in_dim(q, q0, Q_BLOCK, axis=1).reshape(B_, Q_BLOCK, kvh, hpg, d)
        s = jnp.einsum('bqghd,bqkgd->bghqk', qb, kg) * scale + bias
        p = masked_softmax(s, (dist >= 0)[:, None, None])
        o = jnp.einsum('bghqk,bqkgd->bqghd', p.astype(vg.dtype), vg)
        return o.reshape(B_, Q_BLOCK, H_ * d)

    out = lax.map(block, jnp.arange(S_ // Q_BLOCK))
    return out.transpose(1, 0, 2, 3).reshape(B_, S_, H_ * d)


def even_mixer(h, rel_bias, w_in, w_out, nsa_qk_g, cmp_pe, cmp_w1, cmp_b1, cmp_w2, cmp_b2,
               mla_q_norm_g, mla_kv_norm_g, mla_w_uq, mla_w_ukv, mla_nope_g, mla_rope_g):
    B_, S_, _ = h.shape
    proj = h @ w_in
    sizes = (NSA_HEADS * HEAD_DIM, 6 * NSA_GROUPS * HEAD_DIM, 3 * NSA_HEADS, MLA_Q_RANK, MLA_KV_RANK, MLA_ROPE)
    q_n, kv_n, gate_n, cq, ckv, kpe = jnp.split(proj, _offsets(sizes), axis=-1)
    q_n = rms_norm(q_n.reshape(B_, S_, NSA_HEADS, HEAD_DIM), nsa_qk_g[0])
    kv_n = kv_n.reshape(B_, S_, 6, NSA_GROUPS, HEAD_DIM)
    k_cmp = rms_norm(compress(kv_n[:, :, 0], cmp_pe[0], cmp_w1[0], cmp_b1[0], cmp_w2[0], cmp_b2[0]), nsa_qk_g[1])
    v_cmp = compress(kv_n[:, :, 1], cmp_pe[1], cmp_w1[1], cmp_b1[1], cmp_w2[1], cmp_b2[1])
    k_slc = rms_norm(kv_n[:, :, 2], nsa_qk_g[1])
    k_win = rms_norm(kv_n[:, :, 4], nsa_qk_g[1])
    gates = gate_n.reshape(B_, S_, NSA_HEADS, 3)
    o_nsa = nsa_attention(q_n, k_cmp, v_cmp, k_slc, kv_n[:, :, 3], k_win, kv_n[:, :, 5], gates,
                          rel_bias[:, :NSA_HEADS])
    qf = (rms_norm(cq, mla_q_norm_g) @ mla_w_uq).reshape(B_, S_, MLA_HEADS, MLA_NOPE + MLA_ROPE)
    kvf = (rms_norm(ckv, mla_kv_norm_g) @ mla_w_ukv).reshape(B_, S_, MLA_HEADS, MLA_NOPE + MLA_V)
    cos, sin = rope_tables(S_, MLA_ROPE)
    q_nope = rms_norm(qf[..., :MLA_NOPE], mla_nope_g[0])
    q_pe = apply_rope(rms_norm(qf[..., MLA_NOPE:], mla_rope_g[0]), cos[:, None], sin[:, None])
    k_nope = rms_norm(kvf[..., :MLA_NOPE], mla_nope_g[1])
    k_pe = apply_rope(rms_norm(kpe, mla_rope_g[1]), cos, sin)
    o_mla = mla_attention(q_nope, q_pe, k_nope, k_pe, kvf[..., MLA_NOPE:])
    return jnp.concatenate([o_nsa, o_mla], axis=-1) @ w_out


def odd_mixer(h, rel_bias, w_in, w_out, qk_g):
    B_, S_, _ = h.shape
    proj = h @ w_in
    sizes = (DSA_HEADS * HEAD_DIM, DSA_KV_HEADS * HEAD_DIM, DSA_KV_HEADS * HEAD_DIM, IDX_HEADS * IDX_DIM, IDX_DIM, IDX_HEADS)
    q, k, v, iq, ik, iw = jnp.split(proj, _offsets(sizes), axis=-1)
    q = rms_norm(q.reshape(B_, S_, DSA_HEADS, HEAD_DIM), qk_g[0])
    k = rms_norm(k.reshape(B_, S_, DSA_KV_HEADS, HEAD_DIM), qk_g[1])
    v = v.reshape(B_, S_, DSA_KV_HEADS, HEAD_DIM)
    cos, sin = rope_tables(S_, IDX_ROPE)
    iq = iq.reshape(B_, S_, IDX_HEADS, IDX_DIM)
    iq = jnp.concatenate([apply_rope(iq[..., :IDX_ROPE], cos[:, None], sin[:, None]), iq[..., IDX_ROPE:]], axis=-1) * (IDX_DIM ** -0.5)
    ik = jnp.concatenate([apply_rope(ik[..., :IDX_ROPE], cos, sin), ik[..., IDX_ROPE:]], axis=-1)
    iw = iw * (IDX_HEADS ** -0.5)
    return dsa_attention(q, k, v, iq, ik, iw, rel_bias) @ w_out


def conv_ffn(h, w_up, conv_w, conv_b, w_down):
    S_ = h.shape[1]
    u = h @ w_up
    up = jnp.pad(u, ((0, 0), (CONV_WIDTH - 1, 0), (0, 0)))
    u = sum(conv_w[j] * up[:, j:j + S_] for j in range(CONV_WIDTH)) + conv_b
    gate, val = jnp.split(u, 2, axis=-1)
    return (jax.nn.silu(gate) * val) @ w_down


def setup_inputs(seed: int = 0) -> dict:
    key = jax.random.key(seed)
    ks = jax.random.split(key, 32)

    def w(k, shape, fan_in, mult=1.0):
        return jax.random.normal(k, shape, jnp.float32) * (mult * fan_in ** -0.5)

    def gain(k, shape):
        return 1.0 + 0.05 * jax.random.normal(k, shape, jnp.float32)

    def bias(k, shape, s=0.02):
        return s * jax.random.normal(k, shape, jnp.float32)

    D = D_MODEL
    return {
        "x": jax.random.normal(ks[0], (BATCH, SEQ, D), jnp.float32),
        "c": jax.random.normal(ks[1], (BATCH, D), jnp.float32),
        "rel_bias": bias(ks[2], (REL_BUCKETS, N_SLOTS), 0.5),
        "ada_w": w(ks[3], (DEPTH, 2, D, 3 * D), D, 0.5),
        "ada_b": bias(ks[4], (DEPTH, 2, 3 * D)),
        "norm_g": gain(ks[5], (DEPTH, 2, D)),
        "ev_w_in": w(ks[6], (N_EVEN, D, EVEN_COLS), D),
        "ev_w_out": w(ks[7], (N_EVEN, EVEN_OUT, D), EVEN_OUT),
        "nsa_qk_g": gain(ks[8], (N_EVEN, 2, HEAD_DIM)),
        "cmp_pe": bias(ks[9], (N_EVEN, 2, CMP_BLOCK, HEAD_DIM), 0.1),
        "cmp_w1": w(ks[10], (N_EVEN, 2, CMP_BLOCK * HEAD_DIM, CMP_HIDDEN), CMP_BLOCK * HEAD_DIM),
        "cmp_b1": bias(ks[11], (N_EVEN, 2, CMP_HIDDEN)),
        "cmp_w2": w(ks[12], (N_EVEN, 2, CMP_HIDDEN, HEAD_DIM), CMP_HIDDEN),
        "cmp_b2": bias(ks[13], (N_EVEN, 2, HEAD_DIM)),
        "mla_q_norm_g": gain(ks[14], (N_EVEN, MLA_Q_RANK)),
        "mla_kv_norm_g": gain(ks[15], (N_EVEN, MLA_KV_RANK)),
        "mla_w_uq": w(ks[16], (N_EVEN, MLA_Q_RANK, MLA_HEADS * (MLA_NOPE + MLA_ROPE)), MLA_Q_RANK),
        "mla_w_ukv": w(ks[17], (N_EVEN, MLA_KV_RANK, MLA_HEADS * (MLA_NOPE + MLA_V)), MLA_KV_RANK),
        "mla_nope_g": gain(ks[18], (N_EVEN, 2, MLA_NOPE)),
        "mla_rope_g": gain(ks[19], (N_EVEN, 2, MLA_ROPE)),
        "od_w_in": w(ks[20], (N_ODD, D, ODD_COLS), D),
        "od_w_out": w(ks[21], (N_ODD, ODD_OUT, D), ODD_OUT),
        "dsa_qk_g": gain(ks[22], (N_ODD, 2, HEAD_DIM)),
        "ffn_w_up": w(ks[23], (DEPTH, D, 2 * FFN_HIDDEN), D),
        "ffn_conv_w": w(ks[24], (DEPTH, CONV_WIDTH, 2 * FFN_HIDDEN), CONV_WIDTH),
        "ffn_conv_b": bias(ks[25], (DEPTH, 2 * FFN_HIDDEN)),
        "ffn_w_down": w(ks[26], (DEPTH, FFN_HIDDEN, D), FFN_HIDDEN),
    }


def reference(x, c, rel_bias, ada_w, ada_b, norm_g, ev_w_in, ev_w_out, nsa_qk_g, cmp_pe, cmp_w1, cmp_b1,
              cmp_w2, cmp_b2, mla_q_norm_g, mla_kv_norm_g, mla_w_uq, mla_w_ukv, mla_nope_g, mla_rope_g,
              od_w_in, od_w_out, dsa_qk_g, ffn_w_up, ffn_conv_w, ffn_conv_b, ffn_w_down):
    for i in range(DEPTH):
        j = i // 2
        shift, scale, gate = ada_modulate(c, ada_w[i, 0], ada_b[i, 0])
        h = rms_norm(x, norm_g[i, 0]) * (1.0 + scale) + shift
        if i % 2 == 0:
            mix = even_mixer(h, rel_bias, ev_w_in[j], ev_w_out[j], nsa_qk_g[j], cmp_pe[j], cmp_w1[j],
                             cmp_b1[j], cmp_w2[j], cmp_b2[j], mla_q_norm_g[j], mla_kv_norm_g[j],
                             mla_w_uq[j], mla_w_ukv[j], mla_nope_g[j], mla_rope_g[j])
        else:
            mix = odd_mixer(h, rel_bias, od_w_in[j], od_w_out[j], dsa_qk_g[j])
        x = x + gate * mix
        shift, scale, gate = ada_modulate(c, ada_w[i, 1], ada_b[i, 1])
        h = rms_norm(x, norm_g[i, 1]) * (1.0 + scale) + shift
        x = x + gate * conv_ffn(h, ffn_w_up[i], ffn_conv_w[i], ffn_conv_b[i], ffn_w_down[i])
    return x
```

```python
import functools
import math

import numpy as np
import jax
import jax.numpy as jnp
from jax import lax
from jax.experimental import pallas as pl
from jax.experimental.pallas import tpu as pltpu

HEAD_DIM = 128
NSA_HEADS = 8
NSA_GROUPS = 2
NSA_HPG = NSA_HEADS // NSA_GROUPS
CMP_BLOCK = 32
CMP_STRIDE = 16
CMP_HIDDEN = 256
SEL_BLOCK = 64
SEL_TOP_N = 16
WINDOW = 512
MLA_HEADS = 8
MLA_Q_RANK = 512
MLA_KV_RANK = 256
MLA_NOPE = 128
MLA_ROPE = 64
MLA_V = 128
DSA_HEADS = 16
DSA_KV_HEADS = 4
DSA_HPG = DSA_HEADS // DSA_KV_HEADS
IDX_HEADS = 16
IDX_DIM = 64
IDX_ROPE = 32
DSA_TOPK_MAX = 256
REL_BUCKETS = 32
REL_MAX_DIST = 128
CONV_WIDTH = 3
ROPE_THETA = 10000.0
EPS = 1e-6
NEG = -1e30
FORCE = 1e9

LANE = 128
QB = 128
VMEM_LIMIT = 56 * 1024 * 1024

F32 = jnp.float32
BF16 = jnp.bfloat16


def _t5_thresholds():
    d = np.arange(0, 4 * REL_MAX_DIST)
    half = REL_BUCKETS // 2
    val = np.log(np.maximum(d, 1) / half) / math.log(REL_MAX_DIST / half) * (REL_BUCKETS - half)
    large = np.minimum(half + np.floor(np.maximum(val, 0.0)).astype(np.int64), REL_BUCKETS - 1)
    bucket = np.where(d < half, d, large)
    return [int(np.argmax(bucket >= b)) for b in range(1, REL_BUCKETS)]


T5_THR = _t5_thresholds()
T5_FAR = T5_THR[-1]
assert T5_FAR <= LANE


def _cparams(sem):
    return pltpu.CompilerParams(dimension_semantics=sem, vmem_limit_bytes=VMEM_LIMIT)


def _dot(a, b):
    return jnp.dot(a, b, preferred_element_type=F32)


def _dot_nt(a, b):
    return lax.dot_general(a, b, (((1,), (1,)), ((), ())), preferred_element_type=F32)


def _ada_kernel(c_ref, w_ref, b_ref, o_ref):
    c = c_ref[...]
    a = c * jax.nn.sigmoid(c)
    o_ref[0] = jnp.dot(a, w_ref[0], preferred_element_type=F32,
                       precision=lax.Precision.HIGHEST) + b_ref[0]


def ada_all(c, ada_w, ada_b):
    depth, two, d, n3 = ada_w.shape
    bsz = c.shape[0]
    rows = 8
    cp = jnp.zeros((rows, d), F32).at[:bsz].set(c)
    w = ada_w.reshape(depth * two, d, n3)
    b = ada_b.reshape(depth * two, 1, n3)
    tn = 512
    out = pl.pallas_call(
        _ada_kernel,
        grid=(depth * two, n3 // tn),
        in_specs=[pl.BlockSpec((rows, d), lambda l, j: (0, 0)),
                  pl.BlockSpec((1, d, tn), lambda l, j: (l, 0, j)),
                  pl.BlockSpec((1, 1, tn), lambda l, j: (l, 0, j))],
        out_specs=pl.BlockSpec((1, rows, tn), lambda l, j: (l, 0, j)),
        out_shape=jax.ShapeDtypeStruct((depth * two, rows, n3), F32),
        compiler_params=_cparams(("arbitrary", "arbitrary")),
        name="ada_mod",
    )(cp, w, b)
    return out[:, :bsz].reshape(depth, two, bsz, n3)


def _modnorm_kernel(x_ref, g_ref, sc_ref, sh_ref, o_ref):
    x = x_ref[...]
    y = x * lax.rsqrt(jnp.mean(x * x, axis=-1, keepdims=True) + EPS)
    h = (y * g_ref[...]) * (1.0 + sc_ref[0]) + sh_ref[0]
    o_ref[...] = h.astype(o_ref.dtype)


def modnorm(x2, g, scale, shift, seq):
    m, d = x2.shape
    tm = 512
    tpb = seq // tm
    return pl.pallas_call(
        _modnorm_kernel,
        grid=(m // tm,),
        in_specs=[pl.BlockSpec((tm, d), lambda i: (i, 0)),
                  pl.BlockSpec((1, d), lambda i: (0, 0)),
                  pl.BlockSpec((1, 1, d), lambda i: (i // tpb, 0, 0)),
                  pl.BlockSpec((1, 1, d), lambda i: (i // tpb, 0, 0))],
        out_specs=pl.BlockSpec((tm, d), lambda i: (i, 0)),
        out_shape=jax.ShapeDtypeStruct((m, d), BF16),
        compiler_params=_cparams(("arbitrary",)),
        name="modnorm",
    )(x2, g.reshape(1, d), scale.reshape(-1, 1, d), shift.reshape(-1, 1, d))


def _proj_kernel(x_ref, w_ref, o_ref, *, nslab):
    acc = _dot(x_ref[...], w_ref[...])
    for s in range(nslab):
        o_ref[s] = acc[:, s * LANE:(s + 1) * LANE]


def proj_slabs(x, w, tm=512, tn=384):
    m, k = x.shape
    n = w.shape[1]
    assert n % tn == 0 and m % tm == 0
    nslab = tn // LANE
    return pl.pallas_call(
        functools.partial(_proj_kernel, nslab=nslab),
        grid=(m // tm, n // tn),
        in_specs=[pl.BlockSpec((tm, k), lambda i, j: (i, 0)),
                  pl.BlockSpec((k, tn), lambda i, j: (0, j))],
        out_specs=pl.BlockSpec((nslab, tm, LANE), lambda i, j: (j, i, 0)),
        out_shape=jax.ShapeDtypeStruct((n // LANE, m, LANE), F32),
        compiler_params=_cparams(("arbitrary", "arbitrary")),
        name="proj_slabs",
    )(x, w)


def _normproj_kernel(x_ref, g_ref, w_ref, o_ref, *, kslab, nslab):
    x = jnp.concatenate([x_ref[s] for s in range(kslab)], axis=1)
    y = x * lax.rsqrt(jnp.mean(x * x, axis=-1, keepdims=True) + EPS) * g_ref[...]
    acc = _dot(y.astype(BF16), w_ref[...])
    for s in range(nslab):
        o_ref[s] = acc[:, s * LANE:(s + 1) * LANE]


def normproj_slabs(x_slabs, g, w, tm=512, tn=512):
    kslab, m, _ = x_slabs.shape
    k = kslab * LANE
    n = w.shape[1]
    assert n % tn == 0
    nslab = tn // LANE
    return pl.pallas_call(
        functools.partial(_normproj_kernel, kslab=kslab, nslab=nslab),
        grid=(m // tm, n // tn),
        in_specs=[pl.BlockSpec((kslab, tm, LANE), lambda i, j: (0, i, 0)),
                  pl.BlockSpec((1, k), lambda i, j: (0, 0)),
                  pl.BlockSpec((k, tn), lambda i, j: (0, j))],
        out_specs=pl.BlockSpec((nslab, tm, LANE), lambda i, j: (j, i, 0)),
        out_shape=jax.ShapeDtypeStruct((n // LANE, m, LANE), F32),
        compiler_params=_cparams(("arbitrary", "arbitrary")),
        name="normproj_slabs",
    )(x_slabs, g.reshape(1, k), w)


def _resproj_kernel(*refs, npair):
    xres_ref, gate_ref = refs[2 * npair], refs[2 * npair + 1]
    o_ref = refs[2 * npair + 2]
    acc = _dot(refs[0][...], refs[1][...])
    for p in range(1, npair):
        acc = acc + _dot(refs[2 * p][...], refs[2 * p + 1][...])
    o_ref[...] = xres_ref[...] + gate_ref[0] * acc


def resproj(pairs, xres, gate, seq, tm=512, tn=512):
    m, n = xres.shape
    tpb = seq // tm
    in_specs, args = [], []
    for x, w in pairs:
        k = x.shape[1]
        in_specs += [pl.BlockSpec((tm, k), lambda i, j: (i, 0)),
                     pl.BlockSpec((k, tn), lambda i, j: (0, j))]
        args += [x, w]
    in_specs += [pl.BlockSpec((tm, tn), lambda i, j: (i, j)),
                 pl.BlockSpec((1, 1, tn), lambda i, j: (i // tpb, 0, j))]
    args += [xres, gate.reshape(-1, 1, n)]
    return pl.pallas_call(
        functools.partial(_resproj_kernel, npair=len(pairs)),
        grid=(m // tm, n // tn),
        in_specs=in_specs,
        out_specs=pl.BlockSpec((tm, tn), lambda i, j: (i, j)),
        out_shape=jax.ShapeDtypeStruct((m, n), F32),
        compiler_params=_cparams(("arbitrary", "arbitrary")),
        name="resproj",
    )(*args)


HALO = 8


def _ffn_up_kernel(h_ref, wg_ref, wv_ref, cwg_ref, cwv_ref, cbg_ref, cbv_ref, o_ref,
                   ug_ref, uv_ref, *, tm, tiles_per_seq):
    i = pl.program_id(1)
    first = (i % tiles_per_seq) == 0

    @pl.when(first)
    def _():
        ug_ref[0:HALO, :] = jnp.zeros((HALO, ug_ref.shape[1]), F32)
        uv_ref[0:HALO, :] = jnp.zeros((HALO, uv_ref.shape[1]), F32)

    @pl.when(jnp.logical_not(first))
    def _():
        ug_ref[0:HALO, :] = ug_ref[tm:tm + HALO, :]
        uv_ref[0:HALO, :] = uv_ref[tm:tm + HALO, :]

    h = h_ref[...]
    ug_ref[HALO:HALO + tm, :] = _dot(h, wg_ref[...])
    uv_ref[HALO:HALO + tm, :] = _dot(h, wv_ref[...])

    def conv(u_ref, cw_ref, cb_ref):
        out = cb_ref[...]
        for j in range(CONV_WIDTH):
            off = HALO - (CONV_WIDTH - 1) + j
            out = out + cw_ref[j:j + 1, :] * u_ref[off:off + tm, :]
        return out

    g = conv(ug_ref, cwg_ref, cbg_ref)
    v = conv(uv_ref, cwv_ref, cbv_ref)
    o_ref[...] = (g * jax.nn.sigmoid(g) * v).astype(o_ref.dtype)


def ffn_up(h, w_up, conv_w, conv_b, seq, tm=512, tn=512):
    m, d = h.shape
    f = w_up.shape[1] // 2
    nj = f // tn
    tps = seq // tm
    cb = conv_b.reshape(1, 2 * f)
    return pl.pallas_call(
        functools.partial(_ffn_up_kernel, tm=tm, tiles_per_seq=tps),
        grid=(nj, m // tm),
        in_specs=[pl.BlockSpec((tm, d), lambda j, i: (i, 0)),
                  pl.BlockSpec((d, tn), lambda j, i: (0, j)),
                  pl.BlockSpec((d, tn), lambda j, i: (0, nj + j)),
                  pl.BlockSpec((CONV_WIDTH, tn), lambda j, i: (0, j)),
                  pl.BlockSpec((CONV_WIDTH, tn), lambda j, i: (0, nj + j)),
                  pl.BlockSpec((1, tn), lambda j, i: (0, j)),
                  pl.BlockSpec((1, tn), lambda j, i: (0, nj + j))],
        out_specs=pl.BlockSpec((tm, tn), lambda j, i: (i, j)),
        out_shape=jax.ShapeDtypeStruct((m, f), BF16),
        scratch_shapes=[pltpu.VMEM((tm + HALO, tn), F32), pltpu.VMEM((tm + HALO, tn), F32)],
        compiler_params=_cparams(("arbitrary", "arbitrary")),
        name="ffn_up_conv",
    )(h, w_up, w_up, conv_w, conv_w, cb, cb)


def _t5_select(dist, tbl_ref, heads):
    vals = [jnp.full(dist.shape, tbl_ref[0, h], F32) for h in heads]
    for b in range(1, REL_BUCKETS):
        m = dist >= T5_THR[b - 1]
        vals = [jnp.where(m, tbl_ref[b, h], v) for h, v in zip(heads, vals)]
    return vals


def _bias_tiles_kernel(tbl_ref, o_ref):
    h = pl.program_id(0)
    r = lax.broadcasted_iota(jnp.int32, (LANE, LANE), 0)
    c = lax.broadcasted_iota(jnp.int32, (LANE, LANE), 1)
    for rel in range(2):
        o_ref[0, rel] = _t5_select(rel * LANE + r - c, tbl_ref, [h])[0]
    o_ref[0, 2] = jnp.full((LANE, LANE), tbl_ref[REL_BUCKETS - 1, h], F32)


def bias_tiles(rel_bias):
    nh = rel_bias.shape[1]
    return pl.pallas_call(
        _bias_tiles_kernel,
        grid=(nh,),
        in_specs=[pl.BlockSpec(memory_space=pltpu.SMEM)],
        out_specs=pl.BlockSpec((1, 3, LANE, LANE), lambda h: (h, 0, 0, 0)),
        out_shape=jax.ShapeDtypeStruct((nh, 3, LANE, LANE), F32),
        compiler_params=_cparams(("arbitrary",)),
        name="t5_bias_tiles",
    )(rel_bias)


def _compress_kernel(x_ref, pe_ref, w1_ref, b1_ref, w2_ref, b2_ref, g_ref, o_ref, *, half):
    kv = pl.program_id(0)
    x = x_ref[0]
    nrow = x.shape[0]
    a = _dot((x + pe_ref[0, :, :half]).astype(BF16), w1_ref[0, :half, :])
    b = _dot((x + pe_ref[0, :, half:]).astype(BF16), w1_ref[0, half:, :])
    b_next = jnp.concatenate([b[1:], jnp.zeros((1, b.shape[1]), F32)], axis=0)
    hid = jax.nn.gelu(a + b_next + b1_ref[0])
    out = _dot(hid.astype(BF16), w2_ref[0]) + b2_ref[0]
    normed = out * lax.rsqrt(jnp.mean(out * out, axis=-1, keepdims=True) + EPS) * g_ref[...]
    out = jnp.where(kv == 0, normed, out)
    o_ref[0, 0] = out.astype(o_ref.dtype)
    del nrow


def compress_kv(proj, slab0, bsz, seq, cmp_pe, cmp_w1, cmp_b1, cmp_w2, cmp_b2, g_k):
    nslab, m, _ = proj.shape
    nchunk = seq // CMP_STRIDE
    half = CMP_STRIDE * HEAD_DIM
    xv = proj.reshape(nslab, m // CMP_STRIDE, half)
    pe = cmp_pe.reshape(2, 1, CMP_BLOCK * HEAD_DIM)
    return pl.pallas_call(
        functools.partial(_compress_kernel, half=half),
        grid=(2, bsz, NSA_GROUPS),
        in_specs=[pl.BlockSpec((1, nchunk, half), lambda kv, b, g: (slab0 + 2 * kv + g, b, 0)),
                  pl.BlockSpec((1, 1, 2 * half), lambda kv, b, g: (kv, 0, 0)),
                  pl.BlockSpec((1, 2 * half, CMP_HIDDEN), lambda kv, b, g: (kv, 0, 0)),
                  pl.BlockSpec((1, 1, CMP_HIDDEN), lambda kv, b, g: (kv, 0, 0)),
                  pl.BlockSpec((1, CMP_HIDDEN, HEAD_DIM), lambda kv, b, g: (kv, 0, 0)),
                  pl.BlockSpec((1, 1, HEAD_DIM), lambda kv, b, g: (kv, 0, 0)),
                  pl.BlockSpec((1, HEAD_DIM), lambda kv, b, g: (0, 0))],
        out_specs=pl.BlockSpec((1, 1, nchunk, HEAD_DIM), lambda kv, b, g: (kv, g, b, 0)),
        out_shape=jax.ShapeDtypeStruct((2, NSA_GROUPS, bsz * nchunk, HEAD_DIM), BF16),
        compiler_params=_cparams(("arbitrary", "arbitrary", "arbitrary")),
        name="nsa_compress",
    )(xv, pe, cmp_w1.astype(BF16), cmp_b1.reshape(2, 1, CMP_HIDDEN), cmp_w2.astype(BF16),
      cmp_b2.reshape(2, 1, HEAD_DIM), g_k.reshape(1, HEAD_DIM))


def _masked_softmax(s, mask):
    sm = jnp.where(mask, s, NEG)
    mx = jnp.max(sm, axis=-1, keepdims=True)
    p = jnp.where(mask, jnp.exp(sm - mx), 0.0)
    den = jnp.sum(p, axis=-1, keepdims=True)
    return p / jnp.where(den > 0.0, den, 1.0)


def _flash_step(s, mask, v, m, l, acc):
    nh, nq, nk = s.shape
    sm = jnp.where(mask, s, NEG)
    m_new = jnp.maximum(m, jnp.max(sm, axis=-1, keepdims=True))
    p = jnp.where(mask, jnp.exp(sm - m_new), 0.0)
    alpha = jnp.exp(m - m_new)
    l_new = alpha * l + jnp.sum(p, axis=-1, keepdims=True)
    pv = _dot(p.reshape(nh * nq, nk).astype(BF16), v).reshape(nh, nq, v.shape[-1])
    return m_new, l_new, alpha * acc + pv


def _flash_finish(l, acc):
    return acc / jnp.where(l > 0.0, l, 1.0)


def _nsa_kernel(tbl_ref, q_ref, gate_ref, kc_ref, vc_ref, ks_ref, vs_ref, kw_ref, vw_ref,
                dt_ref, ov_ref, ex_ref, o_ref, selm_ref, *, seq, nc):
    g = pl.program_id(1)
    qi = pl.program_id(2)
    q0 = qi * QB
    hpg = NSA_HPG
    ncp = kc_ref.shape[2]
    ns = seq // SEL_BLOCK
    q4 = q_ref[...].reshape(hpg * QB, HEAD_DIM)

    s_c = _dot_nt(q4, kc_ref[0, 0]).reshape(hpg, QB, ncp)
    row = lax.broadcasted_iota(jnp.int32, (QB, ncp), 0)
    col = lax.broadcasted_iota(jnp.int32, (QB, ncp), 1)
    dist_c = q0 + row - (col * CMP_STRIDE + CMP_BLOCK - 1)
    valid_c = (dist_c >= 0) & (col < nc)
    heads = [g * hpg + h for h in range(hpg)]
    bias_c = _t5_select(dist_c, tbl_ref, heads)
    vc = vc_ref[0, 0]
    p_sum = jnp.zeros((QB, ncp), F32)
    o_c = []
    for h in range(hpg):
        p = _masked_softmax(s_c[h] + bias_c[h], valid_c)
        p_sum = p_sum + p
        o_c.append(_dot(p.astype(BF16), vc))

    imp = jnp.dot(p_sum, ov_ref[...], preferred_element_type=F32, precision=lax.Precision.HIGHEST)
    t = q0 + lax.broadcasted_iota(jnp.int32, (QB, LANE), 0)
    blk = lax.broadcasted_iota(jnp.int32, (QB, LANE), 1)
    tb = t // SEL_BLOCK
    forced = (blk == 0) | (blk == tb) | (blk == tb - 1)
    score = jnp.where(forced, FORCE, jnp.where(blk * SEL_BLOCK <= t, imp, NEG))
    score = jnp.where(blk < ns, score, -jnp.inf)
    sel = jnp.zeros((QB, LANE), F32)
    blk_f = blk.astype(F32)
    for _ in range(min(SEL_TOP_N, ns)):
        mx = jnp.max(score, axis=-1, keepdims=True)
        first = jnp.min(jnp.where(score == mx, blk_f, float(LANE)), axis=-1, keepdims=True)
        pick = blk_f == first
        sel = jnp.where(pick, 1.0, sel)
        score = jnp.where(pick, -jnp.inf, score)
    sel_b = sel.astype(BF16)
    chunk = 512
    n_chunk = (q0 + QB + chunk - 1) // chunk

    def expand(c, carry):
        c0 = pl.multiple_of(c * chunk, chunk)
        selm_ref[:, pl.ds(c0, chunk)] = _dot(sel_b, ex_ref[:, pl.ds(c0, chunk)])
        return carry

    lax.fori_loop(0, n_chunk, expand, 0)

    r3 = lax.broadcasted_iota(jnp.int32, (QB, LANE), 0)
    c3 = lax.broadcasted_iota(jnp.int32, (QB, LANE), 1)

    def sel_step(kt, carry):
        m, l, acc = carry
        k0 = pl.multiple_of(kt * LANE, LANE)
        kk = ks_ref[0, pl.ds(k0, LANE), :]
        vv = vs_ref[0, pl.ds(k0, LANE), :]
        rel = jnp.minimum(qi - kt, 2)
        bias = jnp.stack([dt_ref[h, rel] for h in range(hpg)], axis=0)
        s = _dot_nt(q4, kk).reshape(hpg, QB, LANE) + bias
        mask = (selm_ref[:, pl.ds(k0, LANE)] > 0.5) & (k0 + c3 <= q0 + r3)
        return _flash_step(s, mask[None], vv, m, l, acc)

    init = (jnp.full((hpg, QB, 1), NEG, F32), jnp.zeros((hpg, QB, 1), F32),
            jnp.zeros((hpg, QB, HEAD_DIM), F32))
    _, l_s, acc_s = lax.fori_loop(0, qi + 1, sel_step, init)
    o_s = _flash_finish(l_s, acc_s)

    wkeys = WINDOW + QB
    start = pl.multiple_of(jnp.maximum(q0 - WINDOW, 0), LANE)
    kw = kw_ref[0, pl.ds(start, wkeys), :]
    vw = vw_ref[0, pl.ds(start, wkeys), :]
    s_w = _dot_nt(q4, kw).reshape(hpg, QB, wkeys)
    rw = lax.broadcasted_iota(jnp.int32, (QB, wkeys), 0)
    cw = lax.broadcasted_iota(jnp.int32, (QB, wkeys), 1)
    dist_w = (q0 + rw) - (start + cw)
    mask_w = (dist_w >= 0) & (dist_w < WINDOW)
    rel0 = (q0 - start) // LANE
    o_w = []
    for h in range(hpg):
        bias_w = jnp.concatenate(
            [dt_ref[h, jnp.clip(rel0 - j, 0, 2)] for j in range(wkeys // LANE)], axis=1)
        p = _masked_softmax(s_w[h] + bias_w, mask_w)
        o_w.append(_dot(p.astype(BF16), vw))

    gates = jax.nn.sigmoid(gate_ref[0])
    for h in range(hpg):
        o = (gates[:, 3 * h:3 * h + 1] * o_c[h] + gates[:, 3 * h + 1:3 * h + 2] * o_s[h]
             + gates[:, 3 * h + 2:3 * h + 3] * o_w[h])
        o_ref[:, h * HEAD_DIM:(h + 1) * HEAD_DIM] = o.astype(o_ref.dtype)


def nsa_attention(tbl, q, gates, kvc, ks, vs, kw, vw, dtiles, bsz, seq):
    nq = seq // QB
    ncp = seq // CMP_STRIDE
    nc = ncp - 1
    ns = seq // SEL_BLOCK
    assert ns <= LANE and seq >= WINDOW + QB
    cs = np.arange(ncp)[:, None] * CMP_STRIDE
    ss = np.arange(LANE)[None, :] * SEL_BLOCK
    ov = np.clip(np.minimum(cs + CMP_BLOCK, ss + SEL_BLOCK) - np.maximum(cs, ss), 0, None) / CMP_BLOCK
    ov[nc:] = 0.0
    expand = (np.arange(LANE)[:, None] == (np.arange(seq)[None, :] // SEL_BLOCK)).astype(np.float32)
    kv_spec = pl.BlockSpec((1, seq, HEAD_DIM), lambda b, g, i: (g, b, 0))
    return pl.pallas_call(
        functools.partial(_nsa_kernel, seq=seq, nc=nc),
        grid=(bsz, NSA_GROUPS, nq),
        in_specs=[pl.BlockSpec(memory_space=pltpu.SMEM),
                  pl.BlockSpec((NSA_HPG, QB, HEAD_DIM), lambda b, g, i: (g, b * nq + i, 0)),
                  pl.BlockSpec((1, QB, LANE), lambda b, g, i: (g, b * nq + i, 0)),
                  pl.BlockSpec((1, 1, ncp, HEAD_DIM), lambda b, g, i: (0, g, b, 0)),
                  pl.BlockSpec((1, 1, ncp, HEAD_DIM), lambda b, g, i: (1, g, b, 0)),
                  kv_spec, kv_spec, kv_spec, kv_spec,
                  pl.BlockSpec((NSA_HPG, 3, LANE, LANE), lambda b, g, i: (g, 0, 0, 0)),
                  pl.BlockSpec((ncp, LANE), lambda b, g, i: (0, 0)),
                  pl.BlockSpec((LANE, seq), lambda b, g, i: (0, 0))],
        out_specs=pl.BlockSpec((QB, NSA_HPG * HEAD_DIM), lambda b, g, i: (b * nq + i, g)),
        out_shape=jax.ShapeDtypeStruct((bsz * seq, NSA_HEADS * HEAD_DIM), BF16),
        scratch_shapes=[pltpu.VMEM((QB, seq), F32)],
        compiler_params=_cparams(("arbitrary", "arbitrary", "arbitrary")),
        name="nsa_attention",
    )(tbl, q, gates, kvc, kvc, ks, vs, kw, vw, dtiles, jnp.asarray(ov, F32), jnp.asarray(expand, BF16))


def _mla_kernel(q_ref, k_ref, v_ref, o_ref, *, tq, tk):
    qi = pl.program_id(2)
    q0 = qi * tq
    q = q_ref[0]
    r = lax.broadcasted_iota(jnp.int32, (tq, tk), 0)
    c = lax.broadcasted_iota(jnp.int32, (tq, tk), 1)

    def step(kt, carry):
        m, l, acc = carry
        k0 = pl.multiple_of(kt * tk, tk)
        s = _dot_nt(q, k_ref[0, pl.ds(k0, tk), :])[None]
        mask = (k0 + c <= q0 + r)[None]
        return _flash_step(s, mask, v_ref[0, pl.ds(k0, tk), :], m, l, acc)

    init = (jnp.full((1, tq, 1), NEG, F32), jnp.zeros((1, tq, 1), F32),
            jnp.zeros((1, tq, v_ref.shape[-1]), F32))
    _, l, acc = lax.fori_loop(0, (q0 + tq + tk - 1) // tk, step, init)
    o_ref[...] = _flash_finish(l, acc)[0].astype(o_ref.dtype)


def mla_attention(q, k, v, bsz, seq, tq=256, tk=256):
    nh, _, dqk = q.shape
    dv = v.shape[-1]
    nq = seq // tq
    return pl.pallas_call(
        functools.partial(_mla_kernel, tq=tq, tk=tk),
        grid=(bsz, nh, nq),
        in_specs=[pl.BlockSpec((1, tq, dqk), lambda b, h, i: (h, b * nq + i, 0)),
                  pl.BlockSpec((1, seq, dqk), lambda b, h, i: (h, b, 0)),
                  pl.BlockSpec((1, seq, dv), lambda b, h, i: (h, b, 0))],
        out_specs=pl.BlockSpec((tq, dv), lambda b, h, i: (b * nq + i, h)),
        out_shape=jax.ShapeDtypeStruct((bsz * seq, nh * dv), BF16),
        compiler_params=_cparams(("arbitrary", "arbitrary", "arbitrary")),
        name="mla_attention",
    )(q, k, v)


INT_MIN = -2 ** 31
NEG_KEY = int(np.array(NEG, np.float32).view(np.int32)) ^ 0x7FFFFFFF
IDX_CHUNK = 512


def _sort_key(x):
    bits = pltpu.bitcast(x + 0.0, jnp.int32)
    return jnp.where(bits < 0, bits ^ 0x7FFFFFFF, bits)


def _dsa_kernel(iq_ref, iw_ref, ik_ref, q_ref, k_ref, v_ref, dt_ref, o_ref, key_ref, *, seq, k_sel):
    qi = pl.program_id(1)
    q0 = qi * QB
    n_chunk = (q0 + QB + IDX_CHUNK - 1) // IDX_CHUNK
    n_rest = seq - n_chunk * IDX_CHUNK
    rowc = lax.broadcasted_iota(jnp.int32, (QB, IDX_CHUNK), 0)
    colc = lax.broadcasted_iota(jnp.int32, (QB, IDX_CHUNK), 1)
    iw = iw_ref[...]

    def score_chunk(c, carry):
        c0 = pl.multiple_of(c * IDX_CHUNK, IDX_CHUNK)
        ikc = ik_ref[pl.ds(c0, IDX_CHUNK), :]
        acc = jnp.zeros((QB, IDX_CHUNK), F32)
        for h in range(IDX_HEADS):
            acc = acc + jnp.maximum(_dot_nt(iq_ref[h], ikc), 0.0) * iw[:, h:h + 1]
        acc = jnp.where(c0 + colc <= q0 + rowc, acc, NEG)
        key_ref[:, pl.ds(c0, IDX_CHUNK)] = _sort_key(acc)
        return carry

    lax.fori_loop(0, n_chunk, score_chunk, 0)

    def count(pred):
        def body(c, acc):
            c0 = pl.multiple_of(c * IDX_CHUNK, IDX_CHUNK)
            hit = jnp.where(pred(key_ref[:, pl.ds(c0, IDX_CHUNK)], c0), 1.0, 0.0)
            part = hit[:, 0:LANE]
            for j in range(1, IDX_CHUNK // LANE):
                part = part + hit[:, j * LANE:(j + 1) * LANE]
            return acc + part
        acc = lax.fori_loop(0, n_chunk, body, jnp.zeros((QB, LANE), F32))
        return jnp.sum(acc, axis=-1, keepdims=True)

    rest = n_rest.astype(F32)
    kf = float(k_sel)

    def bit_step(i, u):
        bit = jnp.left_shift(jnp.int32(1), 31 - i)
        trial = (u | bit) ^ INT_MIN
        cnt = count(lambda keys, c0: keys >= trial) + jnp.where(NEG_KEY >= trial, rest, 0.0)
        return jnp.where(cnt >= kf, u | bit, u)

    u = lax.fori_loop(0, 32, bit_step, jnp.zeros((QB, 1), jnp.int32))
    thr = u ^ INT_MIN
    cnt_gt = count(lambda keys, c0: keys > thr) + jnp.where(NEG_KEY > thr, rest, 0.0)
    cnt_ge = count(lambda keys, c0: keys >= thr) + jnp.where(NEG_KEY >= thr, rest, 0.0)
    need = kf - cnt_gt
    tie_row = (cnt_ge > kf) & (thr != NEG_KEY)
    idx_bits = (seq - 1).bit_length()

    def tie_cut():
        def idx_step(i, x):
            bit = jnp.left_shift(jnp.int32(1), idx_bits - 1 - i)
            trial = x | bit
            f = count(lambda keys, c0: (keys == thr) & (c0 + colc < trial))
            return jnp.where(f <= need - 1.0, trial, x)
        return lax.fori_loop(0, idx_bits, idx_step, jnp.zeros((QB, 1), jnp.int32))

    any_tie = jnp.max(jnp.where(tie_row, 1.0, 0.0)) > 0.0
    x_cut = lax.cond(any_tie, tie_cut, lambda: jnp.full((QB, 1), 2 ** 30, jnp.int32))
    x_cut = jnp.where(tie_row, x_cut, 2 ** 30)

    r3 = lax.broadcasted_iota(jnp.int32, (QB, LANE), 0)
    c3 = lax.broadcasted_iota(jnp.int32, (QB, LANE), 1)
    for gk in range(DSA_KV_HEADS):
        q4 = q_ref[gk * DSA_HPG:(gk + 1) * DSA_HPG].reshape(DSA_HPG * QB, HEAD_DIM)

        def att_step(kt, carry, gk=gk, q4=q4):
            m, l, acc = carry
            k0 = pl.multiple_of(kt * LANE, LANE)
            kk = k_ref[gk, pl.ds(k0, LANE), :]
            vv = v_ref[gk, pl.ds(k0, LANE), :]
            rel = jnp.minimum(qi - kt, 2)
            bias = jnp.stack([dt_ref[gk * DSA_HPG + h, rel] for h in range(DSA_HPG)], axis=0)
            s = _dot_nt(q4, kk).reshape(DSA_HPG, QB, LANE) + bias
            keys = key_ref[:, pl.ds(k0, LANE)]
            pos = k0 + c3
            chosen = (keys > thr) | ((keys == thr) & (pos <= x_cut))
            mask = chosen & (pos <= q0 + r3)
            return _flash_step(s, mask[None], vv, m, l, acc)

        init = (jnp.full((DSA_HPG, QB, 1), NEG, F32), jnp.zeros((DSA_HPG, QB, 1), F32),
                jnp.zeros((DSA_HPG, QB, HEAD_DIM), F32))
        _, l, acc = lax.fori_loop(0, qi + 1, att_step, init)
        o = _flash_finish(l, acc)
        for h in range(DSA_HPG):
            hh = gk * DSA_HPG + h
            o_ref[:, hh * HEAD_DIM:(hh + 1) * HEAD_DIM] = o[h].astype(o_ref.dtype)


def dsa_attention(iq, iw, ik, q, k, v, dtiles, bsz, seq):
    nq = seq // QB
    k_sel = min(DSA_TOPK_MAX, seq // 4)
    assert seq % IDX_CHUNK == 0
    return pl.pallas_call(
        functools.partial(_dsa_kernel, seq=seq, k_sel=k_sel),
        grid=(bsz, nq),
        in_specs=[pl.BlockSpec((IDX_HEADS, QB, LANE), lambda b, i: (0, b * nq + i, 0)),
                  pl.BlockSpec((QB, LANE), lambda b, i: (b * nq + i, 0)),
                  pl.BlockSpec((seq, LANE), lambda b, i: (b, 0)),
                  pl.BlockSpec((DSA_HEADS, QB, HEAD_DIM), lambda b, i: (0, b * nq + i, 0)),
                  pl.BlockSpec((DSA_KV_HEADS, seq, HEAD_DIM), lambda b, i: (0, b, 0),
                               pipeline_mode=pl.Buffered(1)),
                  pl.BlockSpec((DSA_KV_HEADS, seq, HEAD_DIM), lambda b, i: (0, b, 0),
                               pipeline_mode=pl.Buffered(1)),
                  pl.BlockSpec((DSA_HEADS, 3, LANE, LANE), lambda b, i: (0, 0, 0, 0),
                               pipeline_mode=pl.Buffered(1))],
        out_specs=pl.BlockSpec((QB, DSA_HEADS * HEAD_DIM), lambda b, i: (b * nq + i, 0)),
        out_shape=jax.ShapeDtypeStruct((bsz * seq, DSA_HEADS * HEAD_DIM), BF16),
        scratch_shapes=[pltpu.VMEM((QB, seq), jnp.int32)],
        compiler_params=_cparams(("arbitrary", "arbitrary")),
        name="dsa_attention",
    )(iq, iw, ik, q, k, v, dtiles)


def _rms(x, g):
    return x * lax.rsqrt(jnp.mean(x * x, axis=-1, keepdims=True) + EPS) * g


def _rope_tables(seq, dim):
    half = dim // 2
    inv = ROPE_THETA ** (-jnp.arange(half, dtype=F32) / half)
    ang = jnp.arange(seq, dtype=F32)[:, None] * inv[None, :]
    return jnp.cos(ang), jnp.sin(ang)


def _rope(x, cos, sin):
    half = x.shape[-1] // 2
    x1, x2 = x[..., :half], x[..., half:]
    return jnp.concatenate([x1 * cos - x2 * sin, x1 * sin + x2 * cos], axis=-1)


def _pad_cols(w, n):
    return jnp.pad(w, ((0, 0), (0, n - w.shape[1])))


def _even_mixer(h, x2, gate, tbl, dtiles, bsz, seq, w_in, w_out, nsa_qk_g, cmp_pe, cmp_w1, cmp_b1,
                cmp_w2, cmp_b2, q_norm_g, kv_norm_g, w_uq, w_ukv, nope_g, rope_g):
    m = bsz * seq
    nq_cols = NSA_HEADS * HEAD_DIM
    nkv_cols = 6 * NSA_GROUPS * HEAD_DIM
    ngate = 3 * NSA_HEADS
    o_gate = nq_cols + nkv_cols
    o_cq = o_gate + ngate
    o_ckv = o_cq + MLA_Q_RANK
    o_kpe = o_ckv + MLA_KV_RANK
    tail = jnp.concatenate([w_in[:, o_kpe:], w_in[:, o_gate:o_cq]], axis=1)
    w_r = jnp.concatenate([w_in[:, :o_gate], w_in[:, o_cq:o_kpe], _pad_cols(tail, LANE)], axis=1).astype(BF16)
    proj = proj_slabs(h, w_r, tn=384)
    s_kv = NSA_HEADS
    s_cq = s_kv + 6 * NSA_GROUPS
    s_ckv = s_cq + MLA_Q_RANK // LANE
    s_tail = s_ckv + MLA_KV_RANK // LANE

    scale = HEAD_DIM ** -0.5
    q_n = (_rms(proj[:s_kv], nsa_qk_g[0]) * scale).astype(BF16)
    kv = proj[s_kv:s_cq].reshape(6, NSA_GROUPS, m, HEAD_DIM)
    k_slc = _rms(kv[2], nsa_qk_g[1]).astype(BF16)
    v_slc = kv[3].astype(BF16)
    k_win = _rms(kv[4], nsa_qk_g[1]).astype(BF16)
    v_win = kv[5].astype(BF16)
    kvc = compress_kv(proj, s_kv, bsz, seq, cmp_pe, cmp_w1, cmp_b1, cmp_w2, cmp_b2, nsa_qk_g[1])
    tail_v = proj[s_tail]
    gates = tail_v[:, MLA_ROPE:MLA_ROPE + ngate].reshape(m, NSA_GROUPS, 3 * NSA_HPG)
    gates = jnp.pad(jnp.transpose(gates, (1, 0, 2)), ((0, 0), (0, 0), (0, LANE - 3 * NSA_HPG)))
    o_nsa = nsa_attention(tbl, q_n, gates, kvc, k_slc, v_slc, k_win, v_win, dtiles, bsz, seq)

    dq = MLA_NOPE + MLA_ROPE
    wq = w_uq.reshape(MLA_Q_RANK, MLA_HEADS, dq)
    wq_r = jnp.concatenate([wq[:, :, :MLA_NOPE].reshape(MLA_Q_RANK, -1),
                            wq[:, :, MLA_NOPE:].reshape(MLA_Q_RANK, -1)], axis=1).astype(BF16)
    qf = normproj_slabs(proj[s_cq:s_ckv], q_norm_g, wq_r, tn=512)
    kvf = normproj_slabs(proj[s_ckv:s_tail], kv_norm_g, w_ukv.astype(BF16), tn=512)
    cos, sin = _rope_tables(seq, MLA_ROPE)
    cos = jnp.tile(cos, (bsz, 1))
    sin = jnp.tile(sin, (bsz, 1))
    mscale = dq ** -0.5
    q_nope = _rms(qf[:MLA_HEADS], nope_g[0])
    q_pe = qf[MLA_HEADS:].reshape(MLA_HEADS // 2, m, 2, MLA_ROPE)
    q_pe = jnp.transpose(q_pe, (0, 2, 1, 3)).reshape(MLA_HEADS, m, MLA_ROPE)
    q_pe = _rope(_rms(q_pe, rope_g[0]), cos[None], sin[None])
    zpad = jnp.zeros((MLA_HEADS, m, 2 * LANE - dq), F32)
    q_mla = (jnp.concatenate([q_nope, q_pe, zpad], axis=-1) * mscale).astype(BF16)
    kvf = kvf.reshape(MLA_HEADS, 2, m, LANE)
    k_nope = _rms(kvf[:, 0], nope_g[1])
    k_pe = _rope(_rms(tail_v[:, :MLA_ROPE], rope_g[1]), cos, sin)
    k_mla = jnp.concatenate([k_nope, jnp.broadcast_to(k_pe[None], (MLA_HEADS, m, MLA_ROPE)), zpad],
                            axis=-1).astype(BF16)
    v_mla = kvf[:, 1].astype(BF16)
    o_mla = mla_attention(q_mla, k_mla, v_mla, bsz, seq)
    w_o = w_out.astype(BF16)
    return resproj([(o_nsa, w_o[:nq_cols]), (o_mla, w_o[nq_cols:])], x2, gate, seq)


def _odd_mixer(h, x2, gate, dtiles, bsz, seq, w_in, w_out, qk_g):
    m = bsz * seq
    nq = DSA_HEADS * HEAD_DIM
    nkv = DSA_KV_HEADS * HEAD_DIM
    niq = IDX_HEADS * IDX_DIM
    ncols = w_in.shape[1]
    npad = -(-ncols // 384) * 384
    proj = proj_slabs(h, _pad_cols(w_in, npad).astype(BF16), tn=384)
    s_k = nq // LANE
    s_v = s_k + nkv // LANE
    s_iq = s_v + nkv // LANE
    s_tail = s_iq + niq // LANE
    q = (_rms(proj[:s_k], qk_g[0]) * HEAD_DIM ** -0.5).astype(BF16)
    k = _rms(proj[s_k:s_v], qk_g[1]).astype(BF16)
    v = proj[s_v:s_iq].astype(BF16)
    cos, sin = _rope_tables(seq, IDX_ROPE)
    cos = jnp.tile(cos, (bsz, 1))
    sin = jnp.tile(sin, (bsz, 1))
    iq = proj[s_iq:s_tail].reshape(IDX_HEADS // 2, m, 2, IDX_DIM)
    iq = jnp.transpose(iq, (0, 2, 1, 3)).reshape(IDX_HEADS, m, IDX_DIM)
    iq = jnp.concatenate([_rope(iq[..., :IDX_ROPE], cos[None], sin[None]), iq[..., IDX_ROPE:]], axis=-1)
    iq = jnp.pad(iq * IDX_DIM ** -0.5, ((0, 0), (0, 0), (0, LANE - IDX_DIM))).astype(BF16)
    tail = proj[s_tail]
    ik = tail[:, :IDX_DIM]
    ik = jnp.concatenate([_rope(ik[:, :IDX_ROPE], cos, sin), ik[:, IDX_ROPE:]], axis=-1)
    ik = jnp.pad(ik, ((0, 0), (0, LANE - IDX_DIM))).astype(BF16)
    iw = jnp.pad(tail[:, IDX_DIM:IDX_DIM + IDX_HEADS] * IDX_HEADS ** -0.5, ((0, 0), (0, LANE - IDX_HEADS)))
    o = dsa_attention(iq, iw, ik, q, k, v, dtiles, bsz, seq)
    return resproj([(o, w_out.astype(BF16))], x2, gate, seq)


def _conv_ffn(h, x2, gate, seq, w_up, conv_w, conv_b, w_down):
    a = ffn_up(h, w_up.astype(BF16), conv_w, conv_b, seq)
    return resproj([(a, w_down.astype(BF16))], x2, gate, seq)


def kernel(x, c, rel_bias, ada_w, ada_b, norm_g, ev_w_in, ev_w_out, nsa_qk_g, cmp_pe, cmp_w1, cmp_b1, cmp_w2, cmp_b2, mla_q_norm_g, mla_kv_norm_g, mla_w_uq, mla_w_ukv, mla_nope_g, mla_rope_g, od_w_in, od_w_out, dsa_qk_g, ffn_w_up, ffn_conv_w, ffn_conv_b, ffn_w_down):
    bsz, seq, d = x.shape
    depth = ada_w.shape[0]
    x2 = x.reshape(bsz * seq, d)
    mods = ada_all(c, ada_w, ada_b)
    dtiles = bias_tiles(rel_bias)
    for i in range(depth):
        j = i // 2
        shift, scale, gate = jnp.split(mods[i, 0], 3, axis=-1)
        h = modnorm(x2, norm_g[i, 0], scale, shift, seq)
        if i % 2 == 0:
            x2 = _even_mixer(h, x2, gate, rel_bias, dtiles, bsz, seq, ev_w_in[j], ev_w_out[j], nsa_qk_g[j],
                             cmp_pe[j], cmp_w1[j], cmp_b1[j], cmp_w2[j], cmp_b2[j], mla_q_norm_g[j],
                             mla_kv_norm_g[j], mla_w_uq[j], mla_w_ukv[j], mla_nope_g[j], mla_rope_g[j])
        else:
            x2 = _odd_mixer(h, x2, gate, dtiles, bsz, seq, od_w_in[j], od_w_out[j], dsa_qk_g[j])
        shift, scale, gate = jnp.split(mods[i, 1], 3, axis=-1)
        h = modnorm(x2, norm_g[i, 1], scale, shift, seq)
        x2 = _conv_ffn(h, x2, gate, seq, ffn_w_up[i], ffn_conv_w[i], ffn_conv_b[i], ffn_w_down[i])
    return x2.reshape(bsz, seq, d)
```

```python
import functools
import math

import numpy as np
import jax
import jax.numpy as jnp
from jax import lax
from jax.experimental import pallas as pl
from jax.experimental.pallas import tpu as pltpu

HEAD_DIM = 128
NSA_HEADS = 8
NSA_GROUPS = 2
NSA_HPG = NSA_HEADS // NSA_GROUPS
CMP_BLOCK = 32
CMP_STRIDE = 16
CMP_HIDDEN = 256
SEL_BLOCK = 64
SEL_TOP_N = 16
WINDOW = 512
MLA_HEADS = 8
MLA_Q_RANK = 512
MLA_KV_RANK = 256
MLA_NOPE = 128
MLA_ROPE = 64
MLA_V = 128
DSA_HEADS = 16
DSA_KV_HEADS = 4
DSA_HPG = DSA_HEADS // DSA_KV_HEADS
IDX_HEADS = 16
IDX_DIM = 64
IDX_ROPE = 32
DSA_TOPK_MAX = 256
REL_BUCKETS = 32
REL_MAX_DIST = 128
CONV_WIDTH = 3
ROPE_THETA = 10000.0
EPS = 1e-6
NEG = -1e30
FORCE = 1e9

LANE = 128
QB = 128
VMEM_LIMIT = 56 * 1024 * 1024

F32 = jnp.float32
BF16 = jnp.bfloat16


def _t5_thresholds():
    d = np.arange(0, 4 * REL_MAX_DIST)
    half = REL_BUCKETS // 2
    val = np.log(np.maximum(d, 1) / half) / math.log(REL_MAX_DIST / half) * (REL_BUCKETS - half)
    large = np.minimum(half + np.floor(np.maximum(val, 0.0)).astype(np.int64), REL_BUCKETS - 1)
    bucket = np.where(d < half, d, large)
    return [int(np.argmax(bucket >= b)) for b in range(1, REL_BUCKETS)]


T5_THR = _t5_thresholds()
T5_FAR = T5_THR[-1]
assert T5_FAR <= LANE


def _cparams(sem):
    return pltpu.CompilerParams(dimension_semantics=sem, vmem_limit_bytes=VMEM_LIMIT)


def _dot(a, b):
    return jnp.dot(a, b, preferred_element_type=F32)


def _ada_kernel(c_ref, w_ref, b_ref, o_ref):
    c = c_ref[...]
    a = c * jax.nn.sigmoid(c)
    o_ref[0] = jnp.dot(a, w_ref[0], preferred_element_type=F32,
                       precision=lax.Precision.HIGHEST) + b_ref[0]


def ada_all(c, ada_w, ada_b):
    depth, two, d, n3 = ada_w.shape
    bsz = c.shape[0]
    rows = 8
    cp = jnp.zeros((rows, d), F32).at[:bsz].set(c)
    w = ada_w.reshape(depth * two, d, n3)
    b = ada_b.reshape(depth * two, 1, n3)
    tn = 512
    out = pl.pallas_call(
        _ada_kernel,
        grid=(depth * two, n3 // tn),
        in_specs=[pl.BlockSpec((rows, d), lambda l, j: (0, 0)),
                  pl.BlockSpec((1, d, tn), lambda l, j: (l, 0, j)),
                  pl.BlockSpec((1, 1, tn), lambda l, j: (l, 0, j))],
        out_specs=pl.BlockSpec((1, rows, tn), lambda l, j: (l, 0, j)),
        out_shape=jax.ShapeDtypeStruct((depth * two, rows, n3), F32),
        compiler_params=_cparams(("arbitrary", "arbitrary")),
        name="ada_mod",
    )(cp, w, b)
    return out[:, :bsz].reshape(depth, two, bsz, n3)


def _modnorm_kernel(x_ref, g_ref, sc_ref, sh_ref, o_ref):
    x = x_ref[...]
    y = x * lax.rsqrt(jnp.mean(x * x, axis=-1, keepdims=True) + EPS)
    h = (y * g_ref[...]) * (1.0 + sc_ref[0]) + sh_ref[0]
    o_ref[...] = h.astype(o_ref.dtype)


def modnorm(x2, g, scale, shift, seq):
    m, d = x2.shape
    tm = 512
    tpb = seq // tm
    return pl.pallas_call(
        _modnorm_kernel,
        grid=(m // tm,),
        in_specs=[pl.BlockSpec((tm, d), lambda i: (i, 0)),
                  pl.BlockSpec((1, d), lambda i: (0, 0)),
                  pl.BlockSpec((1, 1, d), lambda i: (i // tpb, 0, 0)),
                  pl.BlockSpec((1, 1, d), lambda i: (i // tpb, 0, 0))],
        out_specs=pl.BlockSpec((tm, d), lambda i: (i, 0)),
        out_shape=jax.ShapeDtypeStruct((m, d), BF16),
        compiler_params=_cparams(("arbitrary",)),
        name="modnorm",
    )(x2, g.reshape(1, d), scale.reshape(-1, 1, d), shift.reshape(-1, 1, d))


def _proj_kernel(x_ref, w_ref, o_ref, *, nslab):
    acc = _dot(x_ref[...], w_ref[...])
    for s in range(nslab):
        o_ref[s] = acc[:, s * LANE:(s + 1) * LANE]


def proj_slabs(x, w, tm=512, tn=384):
    m, k = x.shape
    n = w.shape[1]
    assert n % tn == 0 and m % tm == 0
    nslab = tn // LANE
    return pl.pallas_call(
        functools.partial(_proj_kernel, nslab=nslab),
        grid=(m // tm, n // tn),
        in_specs=[pl.BlockSpec((tm, k), lambda i, j: (i, 0)),
                  pl.BlockSpec((k, tn), lambda i, j: (0, j))],
        out_specs=pl.BlockSpec((nslab, tm, LANE), lambda i, j: (j, i, 0)),
        out_shape=jax.ShapeDtypeStruct((n // LANE, m, LANE), F32),
        compiler_params=_cparams(("arbitrary", "arbitrary")),
        name="proj_slabs",
    )(x, w)


def _normproj_kernel(x_ref, g_ref, w_ref, o_ref, *, kslab, nslab):
    x = jnp.concatenate([x_ref[s] for s in range(kslab)], axis=1)
    y = x * lax.rsqrt(jnp.mean(x * x, axis=-1, keepdims=True) + EPS) * g_ref[...]
    acc = _dot(y.astype(BF16), w_ref[...])
    for s in range(nslab):
        o_ref[s] = acc[:, s * LANE:(s + 1) * LANE]


def normproj_slabs(x_slabs, g, w, tm=512, tn=512):
    kslab, m, _ = x_slabs.shape
    k = kslab * LANE
    n = w.shape[1]
    assert n % tn == 0
    nslab = tn // LANE
    return pl.pallas_call(
        functools.partial(_normproj_kernel, kslab=kslab, nslab=nslab),
        grid=(m // tm, n // tn),
        in_specs=[pl.BlockSpec((kslab, tm, LANE), lambda i, j: (0, i, 0)),
                  pl.BlockSpec((1, k), lambda i, j: (0, 0)),
                  pl.BlockSpec((k, tn), lambda i, j: (0, j))],
        out_specs=pl.BlockSpec((nslab, tm, LANE), lambda i, j: (j, i, 0)),
        out_shape=jax.ShapeDtypeStruct((n // LANE, m, LANE), F32),
        compiler_params=_cparams(("arbitrary", "arbitrary")),
        name="normproj_slabs",
    )(x_slabs, g.reshape(1, k), w)


def _resproj_kernel(*refs, npair):
    xres_ref, gate_ref = refs[2 * npair], refs[2 * npair + 1]
    o_ref = refs[2 * npair + 2]
    acc = _dot(refs[0][...], refs[1][...])
    for p in range(1, npair):
        acc = acc + _dot(refs[2 * p][...], refs[2 * p + 1][...])
    o_ref[...] = xres_ref[...] + gate_ref[0] * acc


def resproj(pairs, xres, gate, seq, tm=512, tn=512):
    m, n = xres.shape
    tpb = seq // tm
    in_specs, args = [], []
    for x, w in pairs:
        k = x.shape[1]
        in_specs += [pl.BlockSpec((tm, k), lambda i, j: (i, 0)),
                     pl.BlockSpec((k, tn), lambda i, j: (0, j))]
        args += [x, w]
    in_specs += [pl.BlockSpec((tm, tn), lambda i, j: (i, j)),
                 pl.BlockSpec((1, 1, tn), lambda i, j: (i // tpb, 0, j))]
    args += [xres, gate.reshape(-1, 1, n)]
    return pl.pallas_call(
        functools.partial(_resproj_kernel, npair=len(pairs)),
        grid=(m // tm, n // tn),
        in_specs=in_specs,
        out_specs=pl.BlockSpec((tm, tn), lambda i, j: (i, j)),
        out_shape=jax.ShapeDtypeStruct((m, n), F32),
        compiler_params=_cparams(("arbitrary", "arbitrary")),
        name="resproj",
    )(*args)


HALO = 8


def _ffn_up_kernel(h_ref, wg_ref, wv_ref, cwg_ref, cwv_ref, cbg_ref, cbv_ref, o_ref,
                   ug_ref, uv_ref, *, tm, tiles_per_seq):
    i = pl.program_id(1)
    first = (i % tiles_per_seq) == 0

    @pl.when(first)
    def _():
        ug_ref[0:HALO, :] = jnp.zeros((HALO, ug_ref.shape[1]), F32)
        uv_ref[0:HALO, :] = jnp.zeros((HALO, uv_ref.shape[1]), F32)

    @pl.when(jnp.logical_not(first))
    def _():
        ug_ref[0:HALO, :] = ug_ref[tm:tm + HALO, :]
        uv_ref[0:HALO, :] = uv_ref[tm:tm + HALO, :]

    h = h_ref[...]
    ug_ref[HALO:HALO + tm, :] = _dot(h, wg_ref[...])
    uv_ref[HALO:HALO + tm, :] = _dot(h, wv_ref[...])

    def conv(u_ref, cw_ref, cb_ref):
        out = cb_ref[...]
        for j in range(CONV_WIDTH):
            off = HALO - (CONV_WIDTH - 1) + j
            out = out + cw_ref[j:j + 1, :] * u_ref[off:off + tm, :]
        return out

    g = conv(ug_ref, cwg_ref, cbg_ref)
    v = conv(uv_ref, cwv_ref, cbv_ref)
    o_ref[...] = (g * jax.nn.sigmoid(g) * v).astype(o_ref.dtype)


def ffn_up(h, w_up, conv_w, conv_b, seq, tm=512, tn=512):
    m, d = h.shape
    f = w_up.shape[1] // 2
    nj = f // tn
    tps = seq // tm
    cb = conv_b.reshape(1, 2 * f)
    return pl.pallas_call(
        functools.partial(_ffn_up_kernel, tm=tm, tiles_per_seq=tps),
        grid=(nj, m // tm),
        in_specs=[pl.BlockSpec((tm, d), lambda j, i: (i, 0)),
                  pl.BlockSpec((d, tn), lambda j, i: (0, j)),
                  pl.BlockSpec((d, tn), lambda j, i: (0, nj + j)),
                  pl.BlockSpec((CONV_WIDTH, tn), lambda j, i: (0, j)),
                  pl.BlockSpec((CONV_WIDTH, tn), lambda j, i: (0, nj + j)),
                  pl.BlockSpec((1, tn), lambda j, i: (0, j)),
                  pl.BlockSpec((1, tn), lambda j, i: (0, nj + j))],
        out_specs=pl.BlockSpec((tm, tn), lambda j, i: (i, j)),
        out_shape=jax.ShapeDtypeStruct((m, f), BF16),
        scratch_shapes=[pltpu.VMEM((tm + HALO, tn), F32), pltpu.VMEM((tm + HALO, tn), F32)],
        compiler_params=_cparams(("arbitrary", "arbitrary")),
        name="ffn_up_conv",
    )(h, w_up, w_up, conv_w, conv_w, cb, cb)


LOG2E = 1.4426950408889634
CWIN = 16


def _t5_shifted(dist, tbl_ref, h):
    val = jnp.full(dist.shape, tbl_ref[0, h], F32)
    for b in range(1, REL_BUCKETS):
        val = jnp.where(dist >= T5_THR[b - 1], tbl_ref[b, h], val)
    return (val - tbl_ref[REL_BUCKETS - 1, h]) * LOG2E


def _bias_tiles_kernel(tbl_ref, dt_ref, dc_ref):
    h = pl.program_id(0)
    key = lax.broadcasted_iota(jnp.int32, (LANE, LANE), 0)
    q = lax.broadcasted_iota(jnp.int32, (LANE, LANE), 1)
    for rel in range(2):
        dt_ref[0, rel] = _t5_shifted(rel * LANE + q - key, tbl_ref, h)
    dt_ref[0, 2] = jnp.zeros((LANE, LANE), F32)
    u = lax.broadcasted_iota(jnp.int32, (CWIN, LANE), 0)
    qc = lax.broadcasted_iota(jnp.int32, (CWIN, LANE), 1)
    dc_ref[0] = _t5_shifted(qc - CMP_STRIDE * (u - CWIN // 2) - (CMP_BLOCK - 1), tbl_ref, h)


def bias_tiles(rel_bias):
    nh = rel_bias.shape[1]
    return pl.pallas_call(
        _bias_tiles_kernel,
        grid=(nh,),
        in_specs=[pl.BlockSpec(memory_space=pltpu.SMEM)],
        out_specs=[pl.BlockSpec((1, 3, LANE, LANE), lambda h: (h, 0, 0, 0)),
                   pl.BlockSpec((1, CWIN, LANE), lambda h: (h, 0, 0))],
        out_shape=[jax.ShapeDtypeStruct((nh, 3, LANE, LANE), F32),
                   jax.ShapeDtypeStruct((nh, CWIN, LANE), F32)],
        compiler_params=_cparams(("arbitrary",)),
        name="t5_bias_tiles",
    )(rel_bias)


def _compress_kernel(x_ref, pe_ref, w1_ref, b1_ref, w2_ref, b2_ref, g_ref, o_ref, *, half):
    kv = pl.program_id(0)
    x = x_ref[0]
    a = _dot((x + pe_ref[0, :, :half]).astype(BF16), w1_ref[0, :half, :])
    b = _dot((x + pe_ref[0, :, half:]).astype(BF16), w1_ref[0, half:, :])
    b_next = jnp.concatenate([b[1:], jnp.zeros((1, b.shape[1]), F32)], axis=0)
    hid = jax.nn.gelu(a + b_next + b1_ref[0])
    out = _dot(hid.astype(BF16), w2_ref[0]) + b2_ref[0]
    normed = out * lax.rsqrt(jnp.mean(out * out, axis=-1, keepdims=True) + EPS) * g_ref[...]
    out = jnp.where(kv == 0, normed, out)
    o_ref[0, 0] = out.astype(o_ref.dtype)


def compress_kv(proj, slab0, bsz, seq, cmp_pe, cmp_w1, cmp_b1, cmp_w2, cmp_b2, g_k):
    nslab, m, _ = proj.shape
    nchunk = seq // CMP_STRIDE
    half = CMP_STRIDE * HEAD_DIM
    xv = proj.reshape(nslab, m // CMP_STRIDE, half)
    pe = cmp_pe.reshape(2, 1, CMP_BLOCK * HEAD_DIM)
    return pl.pallas_call(
        functools.partial(_compress_kernel, half=half),
        grid=(2, bsz, NSA_GROUPS),
        in_specs=[pl.BlockSpec((1, nchunk, half), lambda kv, b, g: (slab0 + 2 * kv + g, b, 0)),
                  pl.BlockSpec((1, 1, 2 * half), lambda kv, b, g: (kv, 0, 0)),
                  pl.BlockSpec((1, 2 * half, CMP_HIDDEN), lambda kv, b, g: (kv, 0, 0)),
                  pl.BlockSpec((1, 1, CMP_HIDDEN), lambda kv, b, g: (kv, 0, 0)),
                  pl.BlockSpec((1, CMP_HIDDEN, HEAD_DIM), lambda kv, b, g: (kv, 0, 0)),
                  pl.BlockSpec((1, 1, HEAD_DIM), lambda kv, b, g: (kv, 0, 0)),
                  pl.BlockSpec((1, HEAD_DIM), lambda kv, b, g: (0, 0))],
        out_specs=pl.BlockSpec((1, 1, nchunk, HEAD_DIM), lambda kv, b, g: (kv, g, b, 0)),
        out_shape=jax.ShapeDtypeStruct((2, NSA_GROUPS, bsz * nchunk, HEAD_DIM), BF16),
        compiler_params=_cparams(("arbitrary", "arbitrary", "arbitrary")),
        name="nsa_compress",
    )(xv, pe, cmp_w1.astype(BF16), cmp_b1.reshape(2, 1, CMP_HIDDEN), cmp_w2.astype(BF16),
      cmp_b2.reshape(2, 1, HEAD_DIM), g_k.reshape(1, HEAD_DIM))


KW = 512


def _tile_lanes(x, n):
    return jnp.concatenate([x] * n, axis=1)


def _flash_init(m_ref, l_ref, acc_ref):
    m_ref[...] = jnp.full(m_ref.shape, NEG, F32)
    l_ref[...] = jnp.zeros(l_ref.shape, F32)
    acc_ref[...] = jnp.zeros(acc_ref.shape, F32)


def _flash_update(s, v_t, m_ref, l_ref, acc_ref):
    m_old = m_ref[...]
    m_new = jnp.maximum(m_old, jnp.max(s, axis=0, keepdims=True))
    p = jnp.exp2(s - m_new)
    alpha = jnp.exp2(m_old - m_new)
    l_ref[...] = alpha * l_ref[...] + jnp.sum(p, axis=0, keepdims=True)
    acc_ref[...] = alpha * acc_ref[...] + _dot(v_t, p.astype(BF16))
    m_ref[...] = m_new


def _inv_den(m, den):
    ok = m > 0.5 * NEG
    return jnp.where(ok, 1.0 / jnp.where(ok, den, 1.0), 0.0)


def _flash_result(m_ref, l_ref, acc_ref):
    return acc_ref[...] * _inv_den(m_ref[...], l_ref[...])


def _softmax_cols(s):
    m = jnp.max(s, axis=0, keepdims=True)
    p = jnp.exp2(s - m)
    return p * _inv_den(m, jnp.sum(p, axis=0, keepdims=True))


def _near_bias(dt_ref, heads, qi, kt0, ntile):
    rows = []
    for j in range(ntile):
        rel = jnp.clip(qi - (kt0 + j), 0, 2)
        rows.append(jnp.concatenate([dt_ref[h, rel] for h in heads], axis=1))
    return jnp.concatenate(rows, axis=0)


def _nsa_kernel(qt_ref, gt_ref, kc_ref, vct_ref, ks_ref, vst_ref, kw_ref, vwt_ref,
                dt_ref, dc_ref, ovt_ref, ext_ref, o_ref,
                sc_ref, m_ref, l_ref, acc_ref, *, seq, nc):
    qi = pl.program_id(2)
    q0 = qi * QB
    hpg = NSA_HPG
    heads = list(range(hpg))
    ncp = kc_ref.shape[1]
    ns = seq // SEL_BLOCK
    q_t = jnp.concatenate([qt_ref[h] for h in heads], axis=1)

    pad = CWIN // 2
    sc_ref[0:pad, :] = jnp.zeros((pad, hpg * QB), F32)
    sc_ref[pad + ncp:2 * pad + ncp, :] = jnp.zeros((pad, hpg * QB), F32)
    sc_ref[pad:pad + ncp, :] = _dot(kc_ref[0], q_t)
    r0 = pl.multiple_of(qi * (QB // CMP_STRIDE), 8)
    sc_ref[pl.ds(r0, CWIN), :] = sc_ref[pl.ds(r0, CWIN), :] + jnp.concatenate(
        [dc_ref[h] for h in heads], axis=1)
    ci = lax.broadcasted_iota(jnp.int32, (ncp, QB), 0)
    tc = q0 + lax.broadcasted_iota(jnp.int32, (ncp, QB), 1)
    valid_c = (ci * CMP_STRIDE + CMP_BLOCK - 1 <= tc) & (ci < nc)
    p_c = _softmax_cols(sc_ref[pad:pad + ncp, :] + _tile_lanes(jnp.where(valid_c, 0.0, NEG), hpg))
    oc_t = _dot(vct_ref[0], p_c.astype(BF16))
    p_sum = p_c[:, 0:QB]
    for h in range(1, hpg):
        p_sum = p_sum + p_c[:, h * QB:(h + 1) * QB]

    imp = jnp.dot(ovt_ref[...], p_sum, preferred_element_type=F32, precision=lax.Precision.HIGHEST)
    blk = lax.broadcasted_iota(jnp.int32, (LANE, QB), 0)
    t = q0 + lax.broadcasted_iota(jnp.int32, (LANE, QB), 1)
    tb = t // SEL_BLOCK
    forced = (blk == 0) | (blk == tb) | (blk == tb - 1)
    score = jnp.where(forced, FORCE, jnp.where(blk * SEL_BLOCK <= t, imp, NEG))
    score = jnp.where(blk < ns, score, -jnp.inf)
    blk_f = blk.astype(F32)
    sel = jnp.zeros((LANE, QB), F32)
    for _ in range(min(SEL_TOP_N, ns)):
        mx = jnp.max(score, axis=0, keepdims=True)
        first = jnp.min(jnp.where(score == mx, blk_f, float(LANE)), axis=0, keepdims=True)
        pick = blk_f == first
        sel = jnp.where(pick, 1.0, sel)
        score = jnp.where(pick, -jnp.inf, score)
    sel_b = sel.astype(BF16)

    _flash_init(m_ref, l_ref, acc_ref)
    kpos = lax.broadcasted_iota(jnp.int32, (KW, QB), 0)
    tq = q0 + lax.broadcasted_iota(jnp.int32, (KW, QB), 1)

    def sel_step(c, near):
        c0 = pl.multiple_of(c * KW, KW)
        s = _dot(ks_ref[0, pl.ds(c0, KW), :], q_t)
        chosen = _dot(ext_ref[pl.ds(c0, KW), :], sel_b)
        madd = (chosen - 1.0) * (-NEG)
        if near:
            madd = madd + jnp.where(c0 + kpos <= tq, 0.0, NEG)
            s = s + _near_bias(dt_ref, heads, qi, c * (KW // LANE), KW // LANE)
        _flash_update(s + _tile_lanes(madd, hpg), vst_ref[0, :, pl.ds(c0, KW)], m_ref, l_ref, acc_ref)

    c_near = jnp.maximum(qi - 1, 0) // (KW // LANE)
    lax.fori_loop(0, c_near, lambda c, x: (sel_step(c, False), x)[1], 0)
    lax.fori_loop(c_near, qi // (KW // LANE) + 1, lambda c, x: (sel_step(c, True), x)[1], 0)
    os_t = _flash_result(m_ref, l_ref, acc_ref)

    wkeys = WINDOW + QB
    start = pl.multiple_of(jnp.maximum(q0 - WINDOW, 0), LANE)
    s_w = _dot(kw_ref[0, pl.ds(start, wkeys), :], q_t)
    s_w = s_w + _near_bias(dt_ref, heads, qi, start // LANE, wkeys // LANE)
    dist_w = (q0 + lax.broadcasted_iota(jnp.int32, (wkeys, QB), 1)) - (
        start + lax.broadcasted_iota(jnp.int32, (wkeys, QB), 0))
    mask_w = (dist_w >= 0) & (dist_w < WINDOW)
    p_w = _softmax_cols(s_w + _tile_lanes(jnp.where(mask_w, 0.0, NEG), hpg))
    ow_t = _dot(vwt_ref[0, :, pl.ds(start, wkeys)], p_w.astype(BF16))

    gates = jax.nn.sigmoid(gt_ref[0])
    for h in heads:
        sl = slice(h * QB, (h + 1) * QB)
        o_t = (gates[3 * h:3 * h + 1] * oc_t[:, sl] + gates[3 * h + 1:3 * h + 2] * os_t[:, sl]
               + gates[3 * h + 2:3 * h + 3] * ow_t[:, sl])
        o_ref[:, h * HEAD_DIM:(h + 1) * HEAD_DIM] = o_t.T.astype(o_ref.dtype)


def nsa_attention(q_t, gates_t, kc, vc_t, ks, vs_t, kw, vw_t, dt, dc, bsz, seq):
    nq = seq // QB
    ncp = seq // CMP_STRIDE
    nc = ncp - 1
    ns = seq // SEL_BLOCK
    assert ns <= LANE and seq >= WINDOW + QB and seq % KW == 0
    cs = np.arange(ncp)[None, :] * CMP_STRIDE
    ss = np.arange(LANE)[:, None] * SEL_BLOCK
    ov = np.clip(np.minimum(cs + CMP_BLOCK, ss + SEL_BLOCK) - np.maximum(cs, ss), 0, None) / CMP_BLOCK
    ov[:, nc:] = 0.0
    expand = ((np.arange(seq)[:, None] // SEL_BLOCK) == np.arange(LANE)[None, :]).astype(np.float32)
    k_spec = pl.BlockSpec((1, seq, HEAD_DIM), lambda b, g, i: (g, b, 0))
    vt_spec = pl.BlockSpec((1, HEAD_DIM, seq), lambda b, g, i: (g, 0, b))
    lanes = NSA_HPG * QB
    return pl.pallas_call(
        functools.partial(_nsa_kernel, seq=seq, nc=nc),
        grid=(bsz, NSA_GROUPS, nq),
        in_specs=[pl.BlockSpec((NSA_HPG, HEAD_DIM, QB), lambda b, g, i: (g, 0, b * nq + i)),
                  pl.BlockSpec((1, 16, QB), lambda b, g, i: (g, 0, b * nq + i)),
                  pl.BlockSpec((1, ncp, HEAD_DIM), lambda b, g, i: (g, b, 0)),
                  pl.BlockSpec((1, HEAD_DIM, ncp), lambda b, g, i: (g, 0, b)),
                  k_spec, vt_spec, k_spec, vt_spec,
                  pl.BlockSpec((NSA_HPG, 3, LANE, LANE), lambda b, g, i: (g, 0, 0, 0)),
                  pl.BlockSpec((NSA_HPG, CWIN, LANE), lambda b, g, i: (g, 0, 0)),
                  pl.BlockSpec((LANE, ncp), lambda b, g, i: (0, 0)),
                  pl.BlockSpec((seq, LANE), lambda b, g, i: (0, 0))],
        out_specs=pl.BlockSpec((QB, NSA_HPG * HEAD_DIM), lambda b, g, i: (b * nq + i, g)),
        out_shape=jax.ShapeDtypeStruct((bsz * seq, NSA_HEADS * HEAD_DIM), BF16),
        scratch_shapes=[pltpu.VMEM((ncp + CWIN, lanes), F32),
                        pltpu.VMEM((1, lanes), F32), pltpu.VMEM((1, lanes), F32),
                        pltpu.VMEM((HEAD_DIM, lanes), F32)],
        compiler_params=_cparams(("arbitrary", "arbitrary", "arbitrary")),
        name="nsa_attention",
    )(q_t, gates_t, kc, vc_t, ks, vs_t, kw, vw_t, dt, dc, jnp.asarray(ov, F32), jnp.asarray(expand, BF16))


def _mla_kernel(qt_ref, k_ref, vt_ref, o_ref, m_ref, l_ref, acc_ref):
    qi = pl.program_id(2)
    q_t = qt_ref[0]
    _flash_init(m_ref, l_ref, acc_ref)

    def step(c, x):
        c0 = pl.multiple_of(c * KW, KW)
        _flash_update(_dot(k_ref[0, pl.ds(c0, KW), :], q_t), vt_ref[0, :, pl.ds(c0, KW)],
                      m_ref, l_ref, acc_ref)
        return x

    lax.fori_loop(0, qi, step, 0)
    c0 = pl.multiple_of(qi * KW, KW)
    kpos = lax.broadcasted_iota(jnp.int32, (KW, KW), 0)
    tq = lax.broadcasted_iota(jnp.int32, (KW, KW), 1)
    s = _dot(k_ref[0, pl.ds(c0, KW), :], q_t) + jnp.where(kpos <= tq, 0.0, NEG)
    _flash_update(s, vt_ref[0, :, pl.ds(c0, KW)], m_ref, l_ref, acc_ref)
    o_ref[...] = _flash_result(m_ref, l_ref, acc_ref).T.astype(o_ref.dtype)


def mla_attention(q_t, k, v_t, bsz, seq):
    nh, dqk, _ = q_t.shape
    dv = v_t.shape[1]
    nq = seq // KW
    return pl.pallas_call(
        _mla_kernel,
        grid=(bsz, nh, nq),
        in_specs=[pl.BlockSpec((1, dqk, KW), lambda b, h, i: (h, 0, b * nq + i)),
                  pl.BlockSpec((1, seq, dqk), lambda b, h, i: (h, b, 0)),
                  pl.BlockSpec((1, dv, seq), lambda b, h, i: (h, 0, b))],
        out_specs=pl.BlockSpec((KW, dv), lambda b, h, i: (b * nq + i, h)),
        out_shape=jax.ShapeDtypeStruct((bsz * seq, nh * dv), BF16),
        scratch_shapes=[pltpu.VMEM((1, KW), F32), pltpu.VMEM((1, KW), F32), pltpu.VMEM((dv, KW), F32)],
        compiler_params=_cparams(("arbitrary", "arbitrary", "arbitrary")),
        name="mla_attention",
    )(q_t, k, v_t)


INT_MIN = -2 ** 31
NEG_KEY = int(np.array(NEG, np.float32).view(np.int32)) ^ 0x7FFFFFFF


def _sort_key(x):
    bits = pltpu.bitcast(x + 0.0, jnp.int32)
    return jnp.where(bits < 0, bits ^ 0x7FFFFFFF, bits)


def _dsa_kernel(iqt_ref, iwt_ref, ik_ref, qt_ref, k_ref, vt_ref, dt_ref, o_ref,
                key_ref, madd_ref, m_ref, l_ref, acc_ref, *, seq, k_sel):
    qi = pl.program_id(1)
    q0 = qi * QB
    n_chunk = (q0 + QB + KW - 1) // KW
    n_rest = seq - n_chunk * KW
    kpos = lax.broadcasted_iota(jnp.int32, (KW, QB), 0)
    tq = q0 + lax.broadcasted_iota(jnp.int32, (KW, QB), 1)
    hpp = KW // QB

    def score_chunk(c, x):
        c0 = pl.multiple_of(c * KW, KW)
        ikc = ik_ref[pl.ds(c0, KW), :]
        acc = jnp.zeros((KW, QB), F32)
        for piece in range(IDX_HEADS // hpp):
            sl = slice(piece * KW, (piece + 1) * KW)
            s = jnp.maximum(_dot(ikc, iqt_ref[0, :, sl]), 0.0) * iwt_ref[0, :, sl]
            for j in range(hpp):
                acc = acc + s[:, j * QB:(j + 1) * QB]
        acc = jnp.where(c0 + kpos <= tq, acc, NEG)
        key_ref[pl.ds(c0, KW), :] = _sort_key(acc)
        return x

    lax.fori_loop(0, n_chunk, score_chunk, 0)

    def count(pred):
        def body(c, acc):
            c0 = pl.multiple_of(c * KW, KW)
            hit = jnp.where(pred(key_ref[pl.ds(c0, KW), :], c0), 1.0, 0.0)
            return acc + jnp.sum(hit.reshape(KW // 8, 8, QB), axis=0)
        acc = lax.fori_loop(0, n_chunk, body, jnp.zeros((8, QB), F32))
        return jnp.sum(acc, axis=0, keepdims=True)

    rest = n_rest.astype(F32)
    kf = float(k_sel)

    def bit_step(i, u):
        bit = jnp.left_shift(jnp.int32(1), 31 - i)
        trial = (u | bit) ^ INT_MIN
        cnt = count(lambda keys, c0: keys >= trial) + jnp.where(NEG_KEY >= trial, rest, 0.0)
        return jnp.where(cnt >= kf, u | bit, u)

    u = lax.fori_loop(0, 32, bit_step, jnp.zeros((1, QB), jnp.int32))
    thr = u ^ INT_MIN
    cnt_gt = count(lambda keys, c0: keys > thr) + jnp.where(NEG_KEY > thr, rest, 0.0)
    cnt_ge = count(lambda keys, c0: keys >= thr) + jnp.where(NEG_KEY >= thr, rest, 0.0)
    need = kf - cnt_gt
    tie_q = (cnt_ge > kf) & (thr != NEG_KEY)
    idx_bits = (seq - 1).bit_length()
    no_cut = 2 ** 30

    def tie_cut():
        def idx_step(i, x):
            bit = jnp.left_shift(jnp.int32(1), idx_bits - 1 - i)
            trial = x | bit
            f = count(lambda keys, c0: (keys == thr) & (c0 + kpos < trial))
            return jnp.where(f <= need - 1.0, trial, x)
        return lax.fori_loop(0, idx_bits, idx_step, jnp.zeros((1, QB), jnp.int32))

    any_tie = jnp.max(jnp.where(tie_q, 1.0, 0.0)) > 0.0
    x_cut = lax.cond(any_tie, tie_cut, lambda: jnp.full((1, QB), no_cut, jnp.int32))
    x_cut = jnp.where(tie_q, x_cut, no_cut)

    def mask_chunk(c, x):
        c0 = pl.multiple_of(c * KW, KW)
        keys = key_ref[pl.ds(c0, KW), :]
        pos = c0 + kpos
        chosen = (keys > thr) | ((keys == thr) & (pos <= x_cut))
        madd_ref[pl.ds(c0, KW), :] = jnp.where(chosen & (pos <= tq), 0.0, NEG)
        return x

    lax.fori_loop(0, n_chunk, mask_chunk, 0)

    c_near = jnp.maximum(qi - 1, 0) // (KW // LANE)
    for gk in range(DSA_KV_HEADS):
        heads = [gk * DSA_HPG + h for h in range(DSA_HPG)]
        q_t = jnp.concatenate([qt_ref[h] for h in heads], axis=1)
        _flash_init(m_ref, l_ref, acc_ref)

        def att_step(c, near, gk=gk, heads=heads, q_t=q_t):
            c0 = pl.multiple_of(c * KW, KW)
            s = _dot(k_ref[gk, pl.ds(c0, KW), :], q_t) + _tile_lanes(madd_ref[pl.ds(c0, KW), :], DSA_HPG)
            if near:
                s = s + _near_bias(dt_ref, heads, qi, c * (KW // LANE), KW // LANE)
            _flash_update(s, vt_ref[gk, :, pl.ds(c0, KW)], m_ref, l_ref, acc_ref)

        lax.fori_loop(0, c_near, lambda c, x, f=att_step: (f(c, False), x)[1], 0)
        lax.fori_loop(c_near, n_chunk, lambda c, x, f=att_step: (f(c, True), x)[1], 0)
        o_t = _flash_result(m_ref, l_ref, acc_ref)
        for h in range(DSA_HPG):
            hh = heads[h]
            o_ref[:, hh * HEAD_DIM:(hh + 1) * HEAD_DIM] = o_t[:, h * QB:(h + 1) * QB].T.astype(o_ref.dtype)


def dsa_attention(iq_t, iw_t, ik, q_t, k, v_t, dt, bsz, seq):
    nq = seq // QB
    k_sel = min(DSA_TOPK_MAX, seq // 4)
    assert seq % KW == 0
    lanes = DSA_HPG * QB
    return pl.pallas_call(
        functools.partial(_dsa_kernel, seq=seq, k_sel=k_sel),
        grid=(bsz, nq),
        in_specs=[pl.BlockSpec((1, LANE, IDX_HEADS * QB), lambda b, i: (b * nq + i, 0, 0)),
                  pl.BlockSpec((1, 1, IDX_HEADS * QB), lambda b, i: (b * nq + i, 0, 0)),
                  pl.BlockSpec((seq, LANE), lambda b, i: (b, 0)),
                  pl.BlockSpec((DSA_HEADS, HEAD_DIM, QB), lambda b, i: (0, 0, b * nq + i)),
                  pl.BlockSpec((DSA_KV_HEADS, seq, HEAD_DIM), lambda b, i: (0, b, 0),
                               pipeline_mode=pl.Buffered(1)),
                  pl.BlockSpec((DSA_KV_HEADS, HEAD_DIM, seq), lambda b, i: (0, 0, b),
                               pipeline_mode=pl.Buffered(1)),
                  pl.BlockSpec((DSA_HEADS, 3, LANE, LANE), lambda b, i: (0, 0, 0, 0),
                               pipeline_mode=pl.Buffered(1))],
        out_specs=pl.BlockSpec((QB, DSA_HEADS * HEAD_DIM), lambda b, i: (b * nq + i, 0)),
        out_shape=jax.ShapeDtypeStruct((bsz * seq, DSA_HEADS * HEAD_DIM), BF16),
        scratch_shapes=[pltpu.VMEM((seq, QB), jnp.int32), pltpu.VMEM((seq, QB), F32),
                        pltpu.VMEM((1, lanes), F32), pltpu.VMEM((1, lanes), F32),
                        pltpu.VMEM((HEAD_DIM, lanes), F32)],
        compiler_params=_cparams(("arbitrary", "arbitrary")),
        name="dsa_attention",
    )(iq_t, iw_t, ik, q_t, k, v_t, dt)


def _rms(x, g):
    return x * lax.rsqrt(jnp.mean(x * x, axis=-1, keepdims=True) + EPS) * g


def _rope_tables(seq, dim):
    half = dim // 2
    inv = ROPE_THETA ** (-jnp.arange(half, dtype=F32) / half)
    ang = jnp.arange(seq, dtype=F32)[:, None] * inv[None, :]
    return jnp.cos(ang), jnp.sin(ang)


def _rope(x, cos, sin):
    half = x.shape[-1] // 2
    x1, x2 = x[..., :half], x[..., half:]
    return jnp.concatenate([x1 * cos - x2 * sin, x1 * sin + x2 * cos], axis=-1)


def _pad_cols(w, n):
    return jnp.pad(w, ((0, 0), (0, n - w.shape[1])))


def _t(x):
    return jnp.swapaxes(x, -1, -2)


def _even_mixer(h, x2, gate, dt, dc, bsz, seq, w_in, w_out, nsa_qk_g, cmp_pe, cmp_w1, cmp_b1,
                cmp_w2, cmp_b2, q_norm_g, kv_norm_g, w_uq, w_ukv, nope_g, rope_g):
    m = bsz * seq
    nq_cols = NSA_HEADS * HEAD_DIM
    nkv_cols = 6 * NSA_GROUPS * HEAD_DIM
    ngate = 3 * NSA_HEADS
    o_gate = nq_cols + nkv_cols
    o_cq = o_gate + ngate
    o_ckv = o_cq + MLA_Q_RANK
    o_kpe = o_ckv + MLA_KV_RANK
    tail = jnp.concatenate([w_in[:, o_kpe:], w_in[:, o_gate:o_cq]], axis=1)
    w_r = jnp.concatenate([w_in[:, :o_gate], w_in[:, o_cq:o_kpe], _pad_cols(tail, LANE)], axis=1).astype(BF16)
    proj = proj_slabs(h, w_r, tn=384)
    s_kv = NSA_HEADS
    s_cq = s_kv + 6 * NSA_GROUPS
    s_ckv = s_cq + MLA_Q_RANK // LANE
    s_tail = s_ckv + MLA_KV_RANK // LANE

    scale = HEAD_DIM ** -0.5 * LOG2E
    q_t = _t((_rms(proj[:s_kv], nsa_qk_g[0]) * scale).astype(BF16))
    kv = proj[s_kv:s_cq].reshape(6, NSA_GROUPS, m, HEAD_DIM)
    k_slc = _rms(kv[2], nsa_qk_g[1]).astype(BF16)
    vs_t = _t(kv[3].astype(BF16))
    k_win = _rms(kv[4], nsa_qk_g[1]).astype(BF16)
    vw_t = _t(kv[5].astype(BF16))
    kvc = compress_kv(proj, s_kv, bsz, seq, cmp_pe, cmp_w1, cmp_b1, cmp_w2, cmp_b2, nsa_qk_g[1])
    tail_v = proj[s_tail]
    gates = tail_v[:, MLA_ROPE:MLA_ROPE + ngate].reshape(m, NSA_GROUPS, 3 * NSA_HPG)
    gates_t = jnp.pad(jnp.transpose(gates, (1, 2, 0)), ((0, 0), (0, 16 - 3 * NSA_HPG), (0, 0)))
    o_nsa = nsa_attention(q_t, gates_t, kvc[0], _t(kvc[1]), k_slc, vs_t, k_win, vw_t,
                          dt[:NSA_HEADS], dc[:NSA_HEADS], bsz, seq)

    dq = MLA_NOPE + MLA_ROPE
    wq = w_uq.reshape(MLA_Q_RANK, MLA_HEADS, dq)
    wq_r = jnp.concatenate([wq[:, :, :MLA_NOPE].reshape(MLA_Q_RANK, -1),
                            wq[:, :, MLA_NOPE:].reshape(MLA_Q_RANK, -1)], axis=1).astype(BF16)
    qf = normproj_slabs(proj[s_cq:s_ckv], q_norm_g, wq_r, tn=512)
    kvf = normproj_slabs(proj[s_ckv:s_tail], kv_norm_g, w_ukv.astype(BF16), tn=512)
    cos, sin = _rope_tables(seq, MLA_ROPE)
    cos = jnp.tile(cos, (bsz, 1))
    sin = jnp.tile(sin, (bsz, 1))
    mscale = dq ** -0.5 * LOG2E
    q_nope = _rms(qf[:MLA_HEADS], nope_g[0])
    q_pe = qf[MLA_HEADS:].reshape(MLA_HEADS // 2, m, 2, MLA_ROPE)
    q_pe = jnp.transpose(q_pe, (0, 2, 1, 3)).reshape(MLA_HEADS, m, MLA_ROPE)
    q_pe = _rope(_rms(q_pe, rope_g[0]), cos[None], sin[None])
    zpad = jnp.zeros((MLA_HEADS, m, 2 * LANE - dq), F32)
    q_mla_t = _t((jnp.concatenate([q_nope, q_pe, zpad], axis=-1) * mscale).astype(BF16))
    kvf = kvf.reshape(MLA_HEADS, 2, m, LANE)
    k_nope = _rms(kvf[:, 0], nope_g[1])
    k_pe = _rope(_rms(tail_v[:, :MLA_ROPE], rope_g[1]), cos, sin)
    k_mla = jnp.concatenate([k_nope, jnp.broadcast_to(k_pe[None], (MLA_HEADS, m, MLA_ROPE)), zpad],
                            axis=-1).astype(BF16)
    v_mla_t = _t(kvf[:, 1].astype(BF16))
    o_mla = mla_attention(q_mla_t, k_mla, v_mla_t, bsz, seq)
    w_o = w_out.astype(BF16)
    return resproj([(o_nsa, w_o[:nq_cols]), (o_mla, w_o[nq_cols:])], x2, gate, seq)


def _odd_mixer(h, x2, gate, dt, bsz, seq, w_in, w_out, qk_g):
    m = bsz * seq
    nt = m // QB
    nq = DSA_HEADS * HEAD_DIM
    nkv = DSA_KV_HEADS * HEAD_DIM
    niq = IDX_HEADS * IDX_DIM
    ncols = w_in.shape[1]
    npad = -(-ncols // 384) * 384
    proj = proj_slabs(h, _pad_cols(w_in, npad).astype(BF16), tn=384)
    s_k = nq // LANE
    s_v = s_k + nkv // LANE
    s_iq = s_v + nkv // LANE
    s_tail = s_iq + niq // LANE
    q_t = _t((_rms(proj[:s_k], qk_g[0]) * (HEAD_DIM ** -0.5 * LOG2E)).astype(BF16))
    k = _rms(proj[s_k:s_v], qk_g[1]).astype(BF16)
    v_t = _t(proj[s_v:s_iq].astype(BF16))
    cos, sin = _rope_tables(seq, IDX_ROPE)
    cos = jnp.tile(cos, (bsz, 1))
    sin = jnp.tile(sin, (bsz, 1))
    iq = proj[s_iq:s_tail].reshape(IDX_HEADS // 2, m, 2, IDX_DIM)
    iq = jnp.transpose(iq, (0, 2, 1, 3)).reshape(IDX_HEADS, m, IDX_DIM)
    iq = jnp.concatenate([_rope(iq[..., :IDX_ROPE], cos[None], sin[None]), iq[..., IDX_ROPE:]], axis=-1)
    iq = (iq * IDX_DIM ** -0.5).astype(BF16).reshape(IDX_HEADS, nt, QB, IDX_DIM)
    iq_t = jnp.transpose(iq, (1, 3, 0, 2)).reshape(nt, IDX_DIM, IDX_HEADS * QB)
    iq_t = jnp.pad(iq_t, ((0, 0), (0, LANE - IDX_DIM), (0, 0)))
    tail = proj[s_tail]
    ik = tail[:, :IDX_DIM]
    ik = jnp.concatenate([_rope(ik[:, :IDX_ROPE], cos, sin), ik[:, IDX_ROPE:]], axis=-1)
    ik = jnp.pad(ik, ((0, 0), (0, LANE - IDX_DIM))).astype(BF16)
    iw = (tail[:, IDX_DIM:IDX_DIM + IDX_HEADS] * IDX_HEADS ** -0.5).reshape(nt, QB, IDX_HEADS)
    iw_t = jnp.transpose(iw, (0, 2, 1)).reshape(nt, 1, IDX_HEADS * QB)
    o = dsa_attention(iq_t, iw_t, ik, q_t, k, v_t, dt, bsz, seq)
    return resproj([(o, w_out.astype(BF16))], x2, gate, seq)


def _conv_ffn(h, x2, gate, seq, w_up, conv_w, conv_b, w_down):
    a = ffn_up(h, w_up.astype(BF16), conv_w, conv_b, seq)
    return resproj([(a, w_down.astype(BF16))], x2, gate, seq)


def kernel(x, c, rel_bias, ada_w, ada_b, norm_g, ev_w_in, ev_w_out, nsa_qk_g, cmp_pe, cmp_w1, cmp_b1, cmp_w2, cmp_b2, mla_q_norm_g, mla_kv_norm_g, mla_w_uq, mla_w_ukv, mla_nope_g, mla_rope_g, od_w_in, od_w_out, dsa_qk_g, ffn_w_up, ffn_conv_w, ffn_conv_b, ffn_w_down):
    bsz, seq, d = x.shape
    depth = ada_w.shape[0]
    x2 = x.reshape(bsz * seq, d)
    mods = ada_all(c, ada_w, ada_b)
    dt, dc = bias_tiles(rel_bias)
    for i in range(depth):
        j = i // 2
        shift, scale, gate = jnp.split(mods[i, 0], 3, axis=-1)
        h = modnorm(x2, norm_g[i, 0], scale, shift, seq)
        if i % 2 == 0:
            x2 = _even_mixer(h, x2, gate, dt, dc, bsz, seq, ev_w_in[j], ev_w_out[j], nsa_qk_g[j],
                             cmp_pe[j], cmp_w1[j], cmp_b1[j], cmp_w2[j], cmp_b2[j], mla_q_norm_g[j],
                             mla_kv_norm_g[j], mla_w_uq[j], mla_w_ukv[j], mla_nope_g[j], mla_rope_g[j])
        else:
            x2 = _odd_mixer(h, x2, gate, dt, bsz, seq, od_w_in[j], od_w_out[j], dsa_qk_g[j])
        shift, scale, gate = jnp.split(mods[i, 1], 3, axis=-1)
        h = modnorm(x2, norm_g[i, 1], scale, shift, seq)
        x2 = _conv_ffn(h, x2, gate, seq, ffn_w_up[i], ffn_conv_w[i], ffn_conv_b[i], ffn_w_down[i])
    return x2.reshape(bsz, seq, d)
```

```python
import functools
import math

import numpy as np
import jax
import jax.numpy as jnp
from jax import lax
from jax.experimental import pallas as pl
from jax.experimental.pallas import tpu as pltpu

HEAD_DIM = 128
NSA_HEADS = 8
NSA_GROUPS = 2
NSA_HPG = NSA_HEADS // NSA_GROUPS
CMP_BLOCK = 32
CMP_STRIDE = 16
CMP_HIDDEN = 256
SEL_BLOCK = 64
SEL_TOP_N = 16
WINDOW = 512
MLA_HEADS = 8
MLA_Q_RANK = 512
MLA_KV_RANK = 256
MLA_NOPE = 128
MLA_ROPE = 64
MLA_V = 128
DSA_HEADS = 16
DSA_KV_HEADS = 4
DSA_HPG = DSA_HEADS // DSA_KV_HEADS
IDX_HEADS = 16
IDX_DIM = 64
IDX_ROPE = 32
DSA_TOPK_MAX = 256
REL_BUCKETS = 32
REL_MAX_DIST = 128
CONV_WIDTH = 3
ROPE_THETA = 10000.0
EPS = 1e-6
NEG = -1e30
FORCE = 1e9

LANE = 128
QB = 128
VMEM_LIMIT = 56 * 1024 * 1024

F32 = jnp.float32
BF16 = jnp.bfloat16


def _t5_thresholds():
    d = np.arange(0, 4 * REL_MAX_DIST)
    half = REL_BUCKETS // 2
    val = np.log(np.maximum(d, 1) / half) / math.log(REL_MAX_DIST / half) * (REL_BUCKETS - half)
    large = np.minimum(half + np.floor(np.maximum(val, 0.0)).astype(np.int64), REL_BUCKETS - 1)
    bucket = np.where(d < half, d, large)
    return [int(np.argmax(bucket >= b)) for b in range(1, REL_BUCKETS)]


T5_THR = _t5_thresholds()
T5_FAR = T5_THR[-1]
assert T5_FAR <= LANE


def _cparams(sem):
    return pltpu.CompilerParams(dimension_semantics=sem, vmem_limit_bytes=VMEM_LIMIT)


def _dot(a, b):
    return jnp.dot(a, b, preferred_element_type=F32)


def _ada_kernel(c_ref, w_ref, b_ref, o_ref):
    c = c_ref[...]
    a = c * jax.nn.sigmoid(c)
    o_ref[0] = jnp.dot(a, w_ref[0], preferred_element_type=F32,
                       precision=lax.Precision.HIGHEST) + b_ref[0]


def ada_all(c, ada_w, ada_b):
    depth, two, d, n3 = ada_w.shape
    bsz = c.shape[0]
    rows = 8
    cp = jnp.zeros((rows, d), F32).at[:bsz].set(c)
    w = ada_w.reshape(depth * two, d, n3)
    b = ada_b.reshape(depth * two, 1, n3)
    tn = 512
    out = pl.pallas_call(
        _ada_kernel,
        grid=(depth * two, n3 // tn),
        in_specs=[pl.BlockSpec((rows, d), lambda l, j: (0, 0)),
                  pl.BlockSpec((1, d, tn), lambda l, j: (l, 0, j)),
                  pl.BlockSpec((1, 1, tn), lambda l, j: (l, 0, j))],
        out_specs=pl.BlockSpec((1, rows, tn), lambda l, j: (l, 0, j)),
        out_shape=jax.ShapeDtypeStruct((depth * two, rows, n3), F32),
        compiler_params=_cparams(("arbitrary", "arbitrary")),
        name="ada_mod",
    )(cp, w, b)
    return out[:, :bsz].reshape(depth, two, bsz, n3)


def _modnorm_kernel(x_ref, g_ref, sc_ref, sh_ref, o_ref):
    x = x_ref[...]
    y = x * lax.rsqrt(jnp.mean(x * x, axis=-1, keepdims=True) + EPS)
    h = (y * g_ref[...]) * (1.0 + sc_ref[0]) + sh_ref[0]
    o_ref[...] = h.astype(o_ref.dtype)


def modnorm(x2, g, scale, shift, seq):
    m, d = x2.shape
    tm = 512
    tpb = seq // tm
    return pl.pallas_call(
        _modnorm_kernel,
        grid=(m // tm,),
        in_specs=[pl.BlockSpec((tm, d), lambda i: (i, 0)),
                  pl.BlockSpec((1, d), lambda i: (0, 0)),
                  pl.BlockSpec((1, 1, d), lambda i: (i // tpb, 0, 0)),
                  pl.BlockSpec((1, 1, d), lambda i: (i // tpb, 0, 0))],
        out_specs=pl.BlockSpec((tm, d), lambda i: (i, 0)),
        out_shape=jax.ShapeDtypeStruct((m, d), BF16),
        compiler_params=_cparams(("arbitrary",)),
        name="modnorm",
    )(x2, g.reshape(1, d), scale.reshape(-1, 1, d), shift.reshape(-1, 1, d))


def _proj_kernel(x_ref, w_ref, o_ref, *, nslab):
    acc = _dot(x_ref[...], w_ref[...])
    for s in range(nslab):
        o_ref[s] = acc[:, s * LANE:(s + 1) * LANE]


def proj_slabs(x, w, tm=512, tn=384):
    m, k = x.shape
    n = w.shape[1]
    assert n % tn == 0 and m % tm == 0
    nslab = tn // LANE
    return pl.pallas_call(
        functools.partial(_proj_kernel, nslab=nslab),
        grid=(m // tm, n // tn),
        in_specs=[pl.BlockSpec((tm, k), lambda i, j: (i, 0)),
                  pl.BlockSpec((k, tn), lambda i, j: (0, j))],
        out_specs=pl.BlockSpec((nslab, tm, LANE), lambda i, j: (j, i, 0)),
        out_shape=jax.ShapeDtypeStruct((n // LANE, m, LANE), F32),
        compiler_params=_cparams(("arbitrary", "arbitrary")),
        name="proj_slabs",
    )(x, w)


def _normproj_kernel(x_ref, g_ref, w_ref, o_ref, *, kslab, nslab):
    x = jnp.concatenate([x_ref[s] for s in range(kslab)], axis=1)
    y = x * lax.rsqrt(jnp.mean(x * x, axis=-1, keepdims=True) + EPS) * g_ref[...]
    acc = _dot(y.astype(BF16), w_ref[...])
    for s in range(nslab):
        o_ref[s] = acc[:, s * LANE:(s + 1) * LANE]


def normproj_slabs(x_slabs, g, w, tm=512, tn=512):
    kslab, m, _ = x_slabs.shape
    k = kslab * LANE
    n = w.shape[1]
    assert n % tn == 0
    nslab = tn // LANE
    return pl.pallas_call(
        functools.partial(_normproj_kernel, kslab=kslab, nslab=nslab),
        grid=(m // tm, n // tn),
        in_specs=[pl.BlockSpec((kslab, tm, LANE), lambda i, j: (0, i, 0)),
                  pl.BlockSpec((1, k), lambda i, j: (0, 0)),
                  pl.BlockSpec((k, tn), lambda i, j: (0, j))],
        out_specs=pl.BlockSpec((nslab, tm, LANE), lambda i, j: (j, i, 0)),
        out_shape=jax.ShapeDtypeStruct((n // LANE, m, LANE), F32),
        compiler_params=_cparams(("arbitrary", "arbitrary")),
        name="normproj_slabs",
    )(x_slabs, g.reshape(1, k), w)


def _resproj_kernel(*refs, npair):
    xres_ref, gate_ref = refs[2 * npair], refs[2 * npair + 1]
    o_ref = refs[2 * npair + 2]
    acc = _dot(refs[0][...], refs[1][...])
    for p in range(1, npair):
        acc = acc + _dot(refs[2 * p][...], refs[2 * p + 1][...])
    o_ref[...] = xres_ref[...] + gate_ref[0] * acc


def resproj(pairs, xres, gate, seq, tm=512, tn=512):
    m, n = xres.shape
    tpb = seq // tm
    in_specs, args = [], []
    for x, w in pairs:
        k = x.shape[1]
        in_specs += [pl.BlockSpec((tm, k), lambda i, j: (i, 0)),
                     pl.BlockSpec((k, tn), lambda i, j: (0, j))]
        args += [x, w]
    in_specs += [pl.BlockSpec((tm, tn), lambda i, j: (i, j)),
                 pl.BlockSpec((1, 1, tn), lambda i, j: (i // tpb, 0, j))]
    args += [xres, gate.reshape(-1, 1, n)]
    return pl.pallas_call(
        functools.partial(_resproj_kernel, npair=len(pairs)),
        grid=(m // tm, n // tn),
        in_specs=in_specs,
        out_specs=pl.BlockSpec((tm, tn), lambda i, j: (i, j)),
        out_shape=jax.ShapeDtypeStruct((m, n), F32),
        compiler_params=_cparams(("arbitrary", "arbitrary")),
        name="resproj",
    )(*args)


HALO = 8


def _ffn_up_kernel(h_ref, wg_ref, wv_ref, cwg_ref, cwv_ref, cbg_ref, cbv_ref, o_ref,
                   ug_ref, uv_ref, *, tm, tiles_per_seq):
    i = pl.program_id(1)
    first = (i % tiles_per_seq) == 0

    @pl.when(first)
    def _():
        ug_ref[0:HALO, :] = jnp.zeros((HALO, ug_ref.shape[1]), F32)
        uv_ref[0:HALO, :] = jnp.zeros((HALO, uv_ref.shape[1]), F32)

    @pl.when(jnp.logical_not(first))
    def _():
        ug_ref[0:HALO, :] = ug_ref[tm:tm + HALO, :]
        uv_ref[0:HALO, :] = uv_ref[tm:tm + HALO, :]

    h = h_ref[...]
    ug_ref[HALO:HALO + tm, :] = _dot(h, wg_ref[...])
    uv_ref[HALO:HALO + tm, :] = _dot(h, wv_ref[...])

    def conv(u_ref, cw_ref, cb_ref):
        out = cb_ref[...]
        for j in range(CONV_WIDTH):
            off = HALO - (CONV_WIDTH - 1) + j
            out = out + cw_ref[j:j + 1, :] * u_ref[off:off + tm, :]
        return out

    g = conv(ug_ref, cwg_ref, cbg_ref)
    v = conv(uv_ref, cwv_ref, cbv_ref)
    o_ref[...] = (g * jax.nn.sigmoid(g) * v).astype(o_ref.dtype)


def ffn_up(h, w_up, conv_w, conv_b, seq, tm=512, tn=512):
    m, d = h.shape
    f = w_up.shape[1] // 2
    nj = f // tn
    tps = seq // tm
    cb = conv_b.reshape(1, 2 * f)
    return pl.pallas_call(
        functools.partial(_ffn_up_kernel, tm=tm, tiles_per_seq=tps),
        grid=(nj, m // tm),
        in_specs=[pl.BlockSpec((tm, d), lambda j, i: (i, 0)),
                  pl.BlockSpec((d, tn), lambda j, i: (0, j)),
                  pl.BlockSpec((d, tn), lambda j, i: (0, nj + j)),
                  pl.BlockSpec((CONV_WIDTH, tn), lambda j, i: (0, j)),
                  pl.BlockSpec((CONV_WIDTH, tn), lambda j, i: (0, nj + j)),
                  pl.BlockSpec((1, tn), lambda j, i: (0, j)),
                  pl.BlockSpec((1, tn), lambda j, i: (0, nj + j))],
        out_specs=pl.BlockSpec((tm, tn), lambda j, i: (i, j)),
        out_shape=jax.ShapeDtypeStruct((m, f), BF16),
        scratch_shapes=[pltpu.VMEM((tm + HALO, tn), F32), pltpu.VMEM((tm + HALO, tn), F32)],
        compiler_params=_cparams(("arbitrary", "arbitrary")),
        name="ffn_up_conv",
    )(h, w_up, w_up, conv_w, conv_w, cb, cb)


LOG2E = 1.4426950408889634
CWIN = 16


def _t5_shifted(dist, tbl_ref, h):
    val = jnp.full(dist.shape, tbl_ref[0, h], F32)
    for b in range(1, REL_BUCKETS):
        val = jnp.where(dist >= T5_THR[b - 1], tbl_ref[b, h], val)
    return (val - tbl_ref[REL_BUCKETS - 1, h]) * LOG2E


def _bias_tiles_kernel(tbl_ref, dt_ref, dc_ref):
    h = pl.program_id(0)
    key = lax.broadcasted_iota(jnp.int32, (LANE, LANE), 0)
    q = lax.broadcasted_iota(jnp.int32, (LANE, LANE), 1)
    for rel in range(2):
        dt_ref[0, rel] = _t5_shifted(rel * LANE + q - key, tbl_ref, h)
    dt_ref[0, 2] = jnp.zeros((LANE, LANE), F32)
    u = lax.broadcasted_iota(jnp.int32, (CWIN, LANE), 0)
    qc = lax.broadcasted_iota(jnp.int32, (CWIN, LANE), 1)
    dc_ref[0] = _t5_shifted(qc - CMP_STRIDE * (u - CWIN // 2) - (CMP_BLOCK - 1), tbl_ref, h)


def bias_tiles(rel_bias):
    nh = rel_bias.shape[1]
    return pl.pallas_call(
        _bias_tiles_kernel,
        grid=(nh,),
        in_specs=[pl.BlockSpec(memory_space=pltpu.SMEM)],
        out_specs=[pl.BlockSpec((1, 3, LANE, LANE), lambda h: (h, 0, 0, 0)),
                   pl.BlockSpec((1, CWIN, LANE), lambda h: (h, 0, 0))],
        out_shape=[jax.ShapeDtypeStruct((nh, 3, LANE, LANE), F32),
                   jax.ShapeDtypeStruct((nh, CWIN, LANE), F32)],
        compiler_params=_cparams(("arbitrary",)),
        name="t5_bias_tiles",
    )(rel_bias)


def _compress_kernel(x_ref, pe_ref, w1_ref, b1_ref, w2_ref, b2_ref, g_ref, o_ref, *, half):
    kv = pl.program_id(0)
    x = x_ref[0]
    a = _dot((x + pe_ref[0, :, :half]).astype(BF16), w1_ref[0, :half, :])
    b = _dot((x + pe_ref[0, :, half:]).astype(BF16), w1_ref[0, half:, :])
    b_next = jnp.concatenate([b[1:], jnp.zeros((1, b.shape[1]), F32)], axis=0)
    hid = jax.nn.gelu(a + b_next + b1_ref[0])
    out = _dot(hid.astype(BF16), w2_ref[0]) + b2_ref[0]
    normed = out * lax.rsqrt(jnp.mean(out * out, axis=-1, keepdims=True) + EPS) * g_ref[...]
    out = jnp.where(kv == 0, normed, out)
    o_ref[0, 0] = out.astype(o_ref.dtype)


def compress_kv(proj, slab0, bsz, seq, cmp_pe, cmp_w1, cmp_b1, cmp_w2, cmp_b2, g_k):
    nslab, m, _ = proj.shape
    nchunk = seq // CMP_STRIDE
    half = CMP_STRIDE * HEAD_DIM
    xv = proj.reshape(nslab, m // CMP_STRIDE, half)
    pe = cmp_pe.reshape(2, 1, CMP_BLOCK * HEAD_DIM)
    return pl.pallas_call(
        functools.partial(_compress_kernel, half=half),
        grid=(2, bsz, NSA_GROUPS),
        in_specs=[pl.BlockSpec((1, nchunk, half), lambda kv, b, g: (slab0 + 2 * kv + g, b, 0)),
                  pl.BlockSpec((1, 1, 2 * half), lambda kv, b, g: (kv, 0, 0)),
                  pl.BlockSpec((1, 2 * half, CMP_HIDDEN), lambda kv, b, g: (kv, 0, 0)),
                  pl.BlockSpec((1, 1, CMP_HIDDEN), lambda kv, b, g: (kv, 0, 0)),
                  pl.BlockSpec((1, CMP_HIDDEN, HEAD_DIM), lambda kv, b, g: (kv, 0, 0)),
                  pl.BlockSpec((1, 1, HEAD_DIM), lambda kv, b, g: (kv, 0, 0)),
                  pl.BlockSpec((1, HEAD_DIM), lambda kv, b, g: (0, 0))],
        out_specs=pl.BlockSpec((1, 1, nchunk, HEAD_DIM), lambda kv, b, g: (kv, g, b, 0)),
        out_shape=jax.ShapeDtypeStruct((2, NSA_GROUPS, bsz * nchunk, HEAD_DIM), BF16),
        compiler_params=_cparams(("arbitrary", "arbitrary", "arbitrary")),
        name="nsa_compress",
    )(xv, pe, cmp_w1.astype(BF16), cmp_b1.reshape(2, 1, CMP_HIDDEN), cmp_w2.astype(BF16),
      cmp_b2.reshape(2, 1, HEAD_DIM), g_k.reshape(1, HEAD_DIM))


KW = 512
PV_KEYS = 256


def _tile_lanes(x, n):
    return jnp.concatenate([x] * n, axis=1)


def _flash_init(m_ref, l_ref, acc_ref):
    m_ref[...] = jnp.full(m_ref.shape, NEG, F32)
    l_ref[...] = jnp.zeros(l_ref.shape, F32)
    acc_ref[...] = jnp.zeros(acc_ref.shape, F32)


def _zero_after(x):
    bits = pltpu.bitcast(x, jnp.int32)
    return lax.shift_right_logical(lax.shift_right_logical(bits, 16), 16).astype(F32)


def _flash_update(s, v_t, m_ref, l_ref, acc_ref, col_max=None, after=None):
    m_old = m_ref[...]
    if col_max is None:
        col_max = jnp.max(s, axis=0, keepdims=True)
    m_new = jnp.maximum(m_old, col_max)
    alpha = jnp.exp2(m_old - m_new)
    l_new = alpha * l_ref[...]
    acc = alpha * acc_ref[...]
    nk = s.shape[0]
    for k0 in range(0, nk, PV_KEYS):
        p = jnp.exp2(s[k0:k0 + PV_KEYS] - m_new)
        l_new = l_new + jnp.sum(p, axis=0, keepdims=True)
        acc = acc + _dot(v_t[:, k0:k0 + PV_KEYS], p.astype(BF16))
    l_ref[...] = l_new
    acc_ref[...] = acc
    m_ref[...] = m_new if after is None else m_new + _zero_after(after)


def _inv_den(m, den):
    ok = m > 0.5 * NEG
    return jnp.where(ok, 1.0 / jnp.where(ok, den, 1.0), 0.0)


def _flash_result(m_ref, l_ref, acc_ref):
    return acc_ref[...] * _inv_den(m_ref[...], l_ref[...])


def _softmax_cols(s):
    m = jnp.max(s, axis=0, keepdims=True)
    p = jnp.exp2(s - m)
    return p * _inv_den(m, jnp.sum(p, axis=0, keepdims=True))


def _near_bias(dt_ref, heads, qi, kt0, ntile):
    rows = []
    for j in range(ntile):
        rel = jnp.clip(qi - (kt0 + j), 0, 2)
        rows.append(jnp.concatenate([dt_ref[h, rel] for h in heads], axis=1))
    return jnp.concatenate(rows, axis=0)


def _pipelined_chunks(n, qk_stage, soft_stage):
    @pl.when(n > 0)
    def _():
        qk_stage(0, 0)

    def pair(p, x):
        c = 2 * p
        ahead = qk_stage(c + 1, 1)
        soft_stage(c, 0, ahead)
        ahead = qk_stage(jnp.minimum(c + 2, n - 1), 0)
        soft_stage(c + 1, 1, ahead)
        return x

    lax.fori_loop(0, n // 2, pair, 0)

    @pl.when(n % 2 == 1)
    def _():
        soft_stage(n - 1, 0, None)


def _nsa_kernel(qt_ref, gt_ref, kc_ref, vct_ref, ks_ref, vst_ref, kw_ref, vwt_ref,
                dt_ref, dc_ref, ovt_ref, ext_ref, o_ref,
                sc_ref, m_ref, l_ref, acc_ref, sbuf0, sbuf1, cbuf0, cbuf1, *, seq, nc):
    sbuf, cbuf = (sbuf0, sbuf1), (cbuf0, cbuf1)
    qi = pl.program_id(2)
    q0 = qi * QB
    hpg = NSA_HPG
    heads = list(range(hpg))
    ncp = kc_ref.shape[1]
    ns = seq // SEL_BLOCK
    q_t = jnp.concatenate([qt_ref[h] for h in heads], axis=1)

    pad = CWIN // 2
    sc_ref[0:pad, :] = jnp.zeros((pad, hpg * QB), F32)
    sc_ref[pad + ncp:2 * pad + ncp, :] = jnp.zeros((pad, hpg * QB), F32)
    sc_ref[pad:pad + ncp, :] = _dot(kc_ref[0], q_t)
    r0 = pl.multiple_of(qi * (QB // CMP_STRIDE), 8)
    sc_ref[pl.ds(r0, CWIN), :] = sc_ref[pl.ds(r0, CWIN), :] + jnp.concatenate(
        [dc_ref[h] for h in heads], axis=1)
    ci = lax.broadcasted_iota(jnp.int32, (ncp, QB), 0)
    tc = q0 + lax.broadcasted_iota(jnp.int32, (ncp, QB), 1)
    valid_c = (ci * CMP_STRIDE + CMP_BLOCK - 1 <= tc) & (ci < nc)
    p_c = _softmax_cols(sc_ref[pad:pad + ncp, :] + _tile_lanes(jnp.where(valid_c, 0.0, NEG), hpg))
    oc_t = _dot(vct_ref[0], p_c.astype(BF16))
    p_sum = p_c[:, 0:QB]
    for h in range(1, hpg):
        p_sum = p_sum + p_c[:, h * QB:(h + 1) * QB]

    imp = jnp.dot(ovt_ref[...], p_sum, preferred_element_type=F32, precision=lax.Precision.HIGHEST)
    blk = lax.broadcasted_iota(jnp.int32, (LANE, QB), 0)
    t = q0 + lax.broadcasted_iota(jnp.int32, (LANE, QB), 1)
    tb = t // SEL_BLOCK
    forced = (blk == 0) | (blk == tb) | (blk == tb - 1)
    score = jnp.where(forced, FORCE, jnp.where(blk * SEL_BLOCK <= t, imp, NEG))
    score = jnp.where(blk < ns, score, -jnp.inf)
    blk_f = blk.astype(F32)
    sel = jnp.zeros((LANE, QB), F32)
    for _ in range(min(SEL_TOP_N, ns)):
        mx = jnp.max(score, axis=0, keepdims=True)
        first = jnp.min(jnp.where(score == mx, blk_f, float(LANE)), axis=0, keepdims=True)
        pick = blk_f == first
        sel = jnp.where(pick, 1.0, sel)
        score = jnp.where(pick, -jnp.inf, score)
    sel_b = sel.astype(BF16)

    _flash_init(m_ref, l_ref, acc_ref)
    kpos = lax.broadcasted_iota(jnp.int32, (KW, QB), 0)
    tq = q0 + lax.broadcasted_iota(jnp.int32, (KW, QB), 1)

    def block_mask(c0):
        chosen = _dot(ext_ref[pl.ds(c0, KW), :], sel_b)
        return (chosen - 1.0) * (-NEG)

    def qk_stage(c, buf):
        c0 = pl.multiple_of(c * KW, KW)
        s = _dot(ks_ref[0, pl.ds(c0, KW), :], q_t) + _tile_lanes(block_mask(c0), hpg)
        sbuf[buf][...] = s
        col_max = jnp.max(s, axis=0, keepdims=True)
        cbuf[buf][...] = col_max
        return col_max

    def soft_stage(c, buf, ahead):
        c0 = pl.multiple_of(c * KW, KW)
        _flash_update(sbuf[buf][...], vst_ref[0, :, pl.ds(c0, KW)], m_ref, l_ref, acc_ref,
                      col_max=cbuf[buf][...], after=ahead)

    def near_step(c, x):
        c0 = pl.multiple_of(c * KW, KW)
        madd = block_mask(c0) + jnp.where(c0 + kpos <= tq, 0.0, NEG)
        s = _dot(ks_ref[0, pl.ds(c0, KW), :], q_t) + _near_bias(dt_ref, heads, qi, c * (KW // LANE), KW // LANE)
        _flash_update(s + _tile_lanes(madd, hpg), vst_ref[0, :, pl.ds(c0, KW)], m_ref, l_ref, acc_ref)
        return x

    c_near = jnp.maximum(qi - 1, 0) // (KW // LANE)
    _pipelined_chunks(c_near, qk_stage, soft_stage)
    lax.fori_loop(c_near, qi // (KW // LANE) + 1, near_step, 0)
    os_t = _flash_result(m_ref, l_ref, acc_ref)

    wkeys = WINDOW + QB
    start = pl.multiple_of(jnp.maximum(q0 - WINDOW, 0), LANE)
    s_w = _dot(kw_ref[0, pl.ds(start, wkeys), :], q_t)
    s_w = s_w + _near_bias(dt_ref, heads, qi, start // LANE, wkeys // LANE)
    dist_w = (q0 + lax.broadcasted_iota(jnp.int32, (wkeys, QB), 1)) - (
        start + lax.broadcasted_iota(jnp.int32, (wkeys, QB), 0))
    mask_w = (dist_w >= 0) & (dist_w < WINDOW)
    p_w = _softmax_cols(s_w + _tile_lanes(jnp.where(mask_w, 0.0, NEG), hpg))
    ow_t = _dot(vwt_ref[0, :, pl.ds(start, wkeys)], p_w.astype(BF16))

    gates = jax.nn.sigmoid(gt_ref[0])
    for h in heads:
        sl = slice(h * QB, (h + 1) * QB)
        o_t = (gates[3 * h:3 * h + 1] * oc_t[:, sl] + gates[3 * h + 1:3 * h + 2] * os_t[:, sl]
               + gates[3 * h + 2:3 * h + 3] * ow_t[:, sl])
        o_ref[:, h * HEAD_DIM:(h + 1) * HEAD_DIM] = o_t.T.astype(o_ref.dtype)


def nsa_attention(q_t, gates_t, kc, vc_t, ks, vs_t, kw, vw_t, dt, dc, bsz, seq):
    nq = seq // QB
    ncp = seq // CMP_STRIDE
    nc = ncp - 1
    ns = seq // SEL_BLOCK
    assert ns <= LANE and seq >= WINDOW + QB and seq % KW == 0
    cs = np.arange(ncp)[None, :] * CMP_STRIDE
    ss = np.arange(LANE)[:, None] * SEL_BLOCK
    ov = np.clip(np.minimum(cs + CMP_BLOCK, ss + SEL_BLOCK) - np.maximum(cs, ss), 0, None) / CMP_BLOCK
    ov[:, nc:] = 0.0
    expand = ((np.arange(seq)[:, None] // SEL_BLOCK) == np.arange(LANE)[None, :]).astype(np.float32)
    k_spec = pl.BlockSpec((1, seq, HEAD_DIM), lambda b, g, i: (g, b, 0))
    vt_spec = pl.BlockSpec((1, HEAD_DIM, seq), lambda b, g, i: (g, 0, b))
    lanes = NSA_HPG * QB
    return pl.pallas_call(
        functools.partial(_nsa_kernel, seq=seq, nc=nc),
        grid=(bsz, NSA_GROUPS, nq),
        in_specs=[pl.BlockSpec((NSA_HPG, HEAD_DIM, QB), lambda b, g, i: (g, 0, b * nq + i)),
                  pl.BlockSpec((1, 16, QB), lambda b, g, i: (g, 0, b * nq + i)),
                  pl.BlockSpec((1, ncp, HEAD_DIM), lambda b, g, i: (g, b, 0)),
                  pl.BlockSpec((1, HEAD_DIM, ncp), lambda b, g, i: (g, 0, b)),
                  k_spec, vt_spec, k_spec, vt_spec,
                  pl.BlockSpec((NSA_HPG, 3, LANE, LANE), lambda b, g, i: (g, 0, 0, 0)),
                  pl.BlockSpec((NSA_HPG, CWIN, LANE), lambda b, g, i: (g, 0, 0)),
                  pl.BlockSpec((LANE, ncp), lambda b, g, i: (0, 0)),
                  pl.BlockSpec((seq, LANE), lambda b, g, i: (0, 0))],
        out_specs=pl.BlockSpec((QB, NSA_HPG * HEAD_DIM), lambda b, g, i: (b * nq + i, g)),
        out_shape=jax.ShapeDtypeStruct((bsz * seq, NSA_HEADS * HEAD_DIM), BF16),
        scratch_shapes=[pltpu.VMEM((ncp + CWIN, lanes), F32),
                        pltpu.VMEM((1, lanes), F32), pltpu.VMEM((1, lanes), F32),
                        pltpu.VMEM((HEAD_DIM, lanes), F32),
                        pltpu.VMEM((KW, lanes), F32), pltpu.VMEM((KW, lanes), F32),
                        pltpu.VMEM((1, lanes), F32), pltpu.VMEM((1, lanes), F32)],
        compiler_params=_cparams(("arbitrary", "arbitrary", "arbitrary")),
        name="nsa_attention",
    )(q_t, gates_t, kc, vc_t, ks, vs_t, kw, vw_t, dt, dc, jnp.asarray(ov, F32), jnp.asarray(expand, BF16))


MLA_HPS = 2


def _mla_kernel(qt_ref, k_ref, vt_ref, o_ref, *scratch):
    qi = pl.program_id(2)
    chains = [scratch[3 * h:3 * h + 3] for h in range(MLA_HPS)]
    sbuf = [scratch[(3 + b) * MLA_HPS:(4 + b) * MLA_HPS] for b in range(2)]
    cbuf = [scratch[(5 + b) * MLA_HPS:(6 + b) * MLA_HPS] for b in range(2)]
    for ch in chains:
        _flash_init(*ch)

    def qk_stage(c, buf):
        c0 = pl.multiple_of(c * KW, KW)
        col_max = []
        for h in range(MLA_HPS):
            s = _dot(k_ref[h, pl.ds(c0, KW), :], qt_ref[h])
            sbuf[buf][h][...] = s
            col_max.append(jnp.max(s, axis=0, keepdims=True))
            cbuf[buf][h][...] = col_max[-1]
        return col_max

    def soft_stage(c, buf, ahead):
        c0 = pl.multiple_of(c * KW, KW)
        for h, ch in enumerate(chains):
            _flash_update(sbuf[buf][h][...], vt_ref[h, :, pl.ds(c0, KW)], *ch, col_max=cbuf[buf][h][...],
                          after=None if ahead is None else ahead[h])

    _pipelined_chunks(qi, qk_stage, soft_stage)
    c0 = pl.multiple_of(qi * KW, KW)
    kpos = lax.broadcasted_iota(jnp.int32, (KW, KW), 0)
    tq = lax.broadcasted_iota(jnp.int32, (KW, KW), 1)
    causal = jnp.where(kpos <= tq, 0.0, NEG)
    dv = vt_ref.shape[1]
    scores = [_dot(k_ref[h, pl.ds(c0, KW), :], qt_ref[h]) + causal for h in range(MLA_HPS)]
    for h, ch in enumerate(chains):
        _flash_update(scores[h], vt_ref[h, :, pl.ds(c0, KW)], *ch)
        o_ref[:, h * dv:(h + 1) * dv] = _flash_result(*ch).T.astype(o_ref.dtype)


def mla_attention(q_t, k, v_t, bsz, seq):
    nh, dqk, _ = q_t.shape
    dv = v_t.shape[1]
    nq = seq // KW
    hps = MLA_HPS
    state = [pltpu.VMEM((1, KW), F32), pltpu.VMEM((1, KW), F32), pltpu.VMEM((dv, KW), F32)] * hps
    state += [pltpu.VMEM((KW, KW), F32)] * (2 * hps)
    state += [pltpu.VMEM((1, KW), F32)] * (2 * hps)
    return pl.pallas_call(
        _mla_kernel,
        grid=(bsz, nh // hps, nq),
        in_specs=[pl.BlockSpec((hps, dqk, KW), lambda b, h, i: (h, 0, b * nq + i)),
                  pl.BlockSpec((hps, seq, dqk), lambda b, h, i: (h, b, 0)),
                  pl.BlockSpec((hps, dv, seq), lambda b, h, i: (h, 0, b))],
        out_specs=pl.BlockSpec((KW, hps * dv), lambda b, h, i: (b * nq + i, h)),
        out_shape=jax.ShapeDtypeStruct((bsz * seq, nh * dv), BF16),
        scratch_shapes=state,
        compiler_params=_cparams(("arbitrary", "arbitrary", "arbitrary")),
        name="mla_attention",
    )(q_t, k, v_t)


INT_MIN = -2 ** 31
NEG_KEY = int(np.array(NEG, np.float32).view(np.int32)) ^ 0x7FFFFFFF


def _sort_key(x):
    bits = pltpu.bitcast(x + 0.0, jnp.int32)
    return jnp.where(bits < 0, bits ^ 0x7FFFFFFF, bits)


def _dsa_kernel(iqt_ref, iwt_ref, ik_ref, qt_ref, k_ref, vt_ref, dt_ref, o_ref,
                key_ref, madd_ref, *state, seq, k_sel):
    qi = pl.program_id(1)
    q0 = qi * QB
    n_chunk = (q0 + QB + KW - 1) // KW
    n_rest = seq - n_chunk * KW
    kpos = lax.broadcasted_iota(jnp.int32, (KW, QB), 0)
    tq = q0 + lax.broadcasted_iota(jnp.int32, (KW, QB), 1)
    hpp = KW // QB

    def score_chunk(c, x):
        c0 = pl.multiple_of(c * KW, KW)
        ikc = ik_ref[pl.ds(c0, KW), :]
        acc = jnp.zeros((KW, QB), F32)
        for piece in range(IDX_HEADS // hpp):
            sl = slice(piece * KW, (piece + 1) * KW)
            s = jnp.maximum(_dot(ikc, iqt_ref[0, :, sl]), 0.0) * iwt_ref[0, :, sl]
            for j in range(hpp):
                acc = acc + s[:, j * QB:(j + 1) * QB]
        acc = jnp.where(c0 + kpos <= tq, acc, NEG)
        key_ref[pl.ds(c0, KW), :] = _sort_key(acc)
        return x

    lax.fori_loop(0, n_chunk, score_chunk, 0)

    def count(pred):
        def body(c, acc):
            c0 = pl.multiple_of(c * KW, KW)
            hit = jnp.where(pred(key_ref[pl.ds(c0, KW), :], c0), 1.0, 0.0)
            parts = [hit[8 * i:8 * (i + 1)] for i in range(KW // 8)]
            while len(parts) > 1:
                parts = [parts[i] + parts[i + 1] for i in range(0, len(parts), 2)]
            return acc + parts[0]
        acc = lax.fori_loop(0, n_chunk, body, jnp.zeros((8, QB), F32))
        return jnp.sum(acc, axis=0, keepdims=True)

    rest = n_rest.astype(F32)
    kf = float(k_sel)

    def bit_step(i, u):
        bit = jnp.left_shift(jnp.int32(1), 31 - i)
        trial = (u | bit) ^ INT_MIN
        cnt = count(lambda keys, c0: keys >= trial) + jnp.where(NEG_KEY >= trial, rest, 0.0)
        return jnp.where(cnt >= kf, u | bit, u)

    u = lax.fori_loop(0, 32, bit_step, jnp.zeros((1, QB), jnp.int32))
    thr = u ^ INT_MIN
    cnt_gt = count(lambda keys, c0: keys > thr) + jnp.where(NEG_KEY > thr, rest, 0.0)
    cnt_ge = count(lambda keys, c0: keys >= thr) + jnp.where(NEG_KEY >= thr, rest, 0.0)
    need = kf - cnt_gt
    tie_q = (cnt_ge > kf) & (thr != NEG_KEY)
    idx_bits = (seq - 1).bit_length()
    no_cut = 2 ** 30

    def tie_cut():
        def idx_step(i, x):
            bit = jnp.left_shift(jnp.int32(1), idx_bits - 1 - i)
            trial = x | bit
            f = count(lambda keys, c0: (keys == thr) & (c0 + kpos < trial))
            return jnp.where(f <= need - 1.0, trial, x)
        return lax.fori_loop(0, idx_bits, idx_step, jnp.zeros((1, QB), jnp.int32))

    any_tie = jnp.max(jnp.where(tie_q, 1.0, 0.0)) > 0.0
    x_cut = lax.cond(any_tie, tie_cut, lambda: jnp.full((1, QB), no_cut, jnp.int32))
    x_cut = jnp.where(tie_q, x_cut, no_cut)

    def mask_chunk(c, x):
        c0 = pl.multiple_of(c * KW, KW)
        keys = key_ref[pl.ds(c0, KW), :]
        pos = c0 + kpos
        chosen = (keys > thr) | ((keys == thr) & (pos <= x_cut))
        madd_ref[pl.ds(c0, KW), :] = jnp.where(chosen & (pos <= tq), 0.0, NEG)
        return x

    lax.fori_loop(0, n_chunk, mask_chunk, 0)

    c_near = jnp.maximum(qi - 1, 0) // (KW // LANE)
    ng = DSA_KV_HEADS
    chains = [state[3 * g:3 * g + 3] for g in range(ng)]
    sbuf = [state[(3 + b) * ng:(4 + b) * ng] for b in range(2)]
    cbuf = [state[(5 + b) * ng:(6 + b) * ng] for b in range(2)]
    group_heads = [[g * DSA_HPG + h for h in range(DSA_HPG)] for g in range(ng)]
    for ch in chains:
        _flash_init(*ch)

    def raw_scores(c0, g):
        q_t = jnp.concatenate([qt_ref[h] for h in group_heads[g]], axis=1)
        return _dot(k_ref[g, pl.ds(c0, KW), :], q_t)

    def qk_stage(c, buf):
        c0 = pl.multiple_of(c * KW, KW)
        madd = _tile_lanes(madd_ref[pl.ds(c0, KW), :], DSA_HPG)
        col_max = []
        for g in range(ng):
            s = raw_scores(c0, g) + madd
            sbuf[buf][g][...] = s
            col_max.append(jnp.max(s, axis=0, keepdims=True))
            cbuf[buf][g][...] = col_max[-1]
        return col_max

    def soft_stage(c, buf, ahead):
        c0 = pl.multiple_of(c * KW, KW)
        for g, ch in enumerate(chains):
            _flash_update(sbuf[buf][g][...], vt_ref[g, :, pl.ds(c0, KW)], *ch, col_max=cbuf[buf][g][...],
                          after=None if ahead is None else ahead[g])

    _pipelined_chunks(c_near, qk_stage, soft_stage)

    def near_step(c, x):
        c0 = pl.multiple_of(c * KW, KW)
        madd = _tile_lanes(madd_ref[pl.ds(c0, KW), :], DSA_HPG)
        scores = [raw_scores(c0, g) + madd + _near_bias(dt_ref, group_heads[g], qi, c * (KW // LANE), KW // LANE)
                  for g in range(ng)]
        for g, ch in enumerate(chains):
            _flash_update(scores[g], vt_ref[g, :, pl.ds(c0, KW)], *ch)
        return x

    lax.fori_loop(c_near, n_chunk, near_step, 0)
    for g, ch in enumerate(chains):
        o_t = _flash_result(*ch)
        for h in range(DSA_HPG):
            hh = group_heads[g][h]
            o_ref[:, hh * HEAD_DIM:(hh + 1) * HEAD_DIM] = o_t[:, h * QB:(h + 1) * QB].T.astype(o_ref.dtype)


def dsa_attention(iq_t, iw_t, ik, q_t, k, v_t, dt, bsz, seq):
    nq = seq // QB
    k_sel = min(DSA_TOPK_MAX, seq // 4)
    assert seq % KW == 0
    lanes = DSA_HPG * QB
    return pl.pallas_call(
        functools.partial(_dsa_kernel, seq=seq, k_sel=k_sel),
        grid=(bsz, nq),
        in_specs=[pl.BlockSpec((1, LANE, IDX_HEADS * QB), lambda b, i: (b * nq + i, 0, 0)),
                  pl.BlockSpec((1, 1, IDX_HEADS * QB), lambda b, i: (b * nq + i, 0, 0)),
                  pl.BlockSpec((seq, LANE), lambda b, i: (b, 0)),
                  pl.BlockSpec((DSA_HEADS, HEAD_DIM, QB), lambda b, i: (0, 0, b * nq + i)),
                  pl.BlockSpec((DSA_KV_HEADS, seq, HEAD_DIM), lambda b, i: (0, b, 0),
                               pipeline_mode=pl.Buffered(1)),
                  pl.BlockSpec((DSA_KV_HEADS, HEAD_DIM, seq), lambda b, i: (0, 0, b),
                               pipeline_mode=pl.Buffered(1)),
                  pl.BlockSpec((DSA_HEADS, 3, LANE, LANE), lambda b, i: (0, 0, 0, 0),
                               pipeline_mode=pl.Buffered(1))],
        out_specs=pl.BlockSpec((QB, DSA_HEADS * HEAD_DIM), lambda b, i: (b * nq + i, 0)),
        out_shape=jax.ShapeDtypeStruct((bsz * seq, DSA_HEADS * HEAD_DIM), BF16),
        scratch_shapes=[pltpu.VMEM((seq, QB), jnp.int32), pltpu.VMEM((seq, QB), F32)]
        + [pltpu.VMEM((1, lanes), F32), pltpu.VMEM((1, lanes), F32),
           pltpu.VMEM((HEAD_DIM, lanes), F32)] * DSA_KV_HEADS
        + [pltpu.VMEM((KW, lanes), F32)] * (2 * DSA_KV_HEADS)
        + [pltpu.VMEM((1, lanes), F32)] * (2 * DSA_KV_HEADS),
        compiler_params=_cparams(("arbitrary", "arbitrary")),
        name="dsa_attention",
    )(iq_t, iw_t, ik, q_t, k, v_t, dt)


def _rms(x, g):
    return x * lax.rsqrt(jnp.mean(x * x, axis=-1, keepdims=True) + EPS) * g


def _rope_tables(seq, dim):
    half = dim // 2
    inv = ROPE_THETA ** (-jnp.arange(half, dtype=F32) / half)
    ang = jnp.arange(seq, dtype=F32)[:, None] * inv[None, :]
    return jnp.cos(ang), jnp.sin(ang)


def _rope(x, cos, sin):
    half = x.shape[-1] // 2
    x1, x2 = x[..., :half], x[..., half:]
    return jnp.concatenate([x1 * cos - x2 * sin, x1 * sin + x2 * cos], axis=-1)


def _pad_cols(w, n):
    return jnp.pad(w, ((0, 0), (0, n - w.shape[1])))


def _t(x):
    return jnp.swapaxes(x, -1, -2)


def _even_mixer(h, x2, gate, dt, dc, bsz, seq, w_in, w_out, nsa_qk_g, cmp_pe, cmp_w1, cmp_b1,
                cmp_w2, cmp_b2, q_norm_g, kv_norm_g, w_uq, w_ukv, nope_g, rope_g):
    m = bsz * seq
    nq_cols = NSA_HEADS * HEAD_DIM
    nkv_cols = 6 * NSA_GROUPS * HEAD_DIM
    ngate = 3 * NSA_HEADS
    o_gate = nq_cols + nkv_cols
    o_cq = o_gate + ngate
    o_ckv = o_cq + MLA_Q_RANK
    o_kpe = o_ckv + MLA_KV_RANK
    tail = jnp.concatenate([w_in[:, o_kpe:], w_in[:, o_gate:o_cq]], axis=1)
    w_r = jnp.concatenate([w_in[:, :o_gate], w_in[:, o_cq:o_kpe], _pad_cols(tail, LANE)], axis=1).astype(BF16)
    proj = proj_slabs(h, w_r, tn=384)
    s_kv = NSA_HEADS
    s_cq = s_kv + 6 * NSA_GROUPS
    s_ckv = s_cq + MLA_Q_RANK // LANE
    s_tail = s_ckv + MLA_KV_RANK // LANE

    scale = HEAD_DIM ** -0.5 * LOG2E
    q_t = _t((_rms(proj[:s_kv], nsa_qk_g[0]) * scale).astype(BF16))
    kv = proj[s_kv:s_cq].reshape(6, NSA_GROUPS, m, HEAD_DIM)
    k_slc = _rms(kv[2], nsa_qk_g[1]).astype(BF16)
    vs_t = _t(kv[3].astype(BF16))
    k_win = _rms(kv[4], nsa_qk_g[1]).astype(BF16)
    vw_t = _t(kv[5].astype(BF16))
    kvc = compress_kv(proj, s_kv, bsz, seq, cmp_pe, cmp_w1, cmp_b1, cmp_w2, cmp_b2, nsa_qk_g[1])
    tail_v = proj[s_tail]
    gates = tail_v[:, MLA_ROPE:MLA_ROPE + ngate].reshape(m, NSA_GROUPS, 3 * NSA_HPG)
    gates_t = jnp.pad(jnp.transpose(gates, (1, 2, 0)), ((0, 0), (0, 16 - 3 * NSA_HPG), (0, 0)))
    o_nsa = nsa_attention(q_t, gates_t, kvc[0], _t(kvc[1]), k_slc, vs_t, k_win, vw_t,
                          dt[:NSA_HEADS], dc[:NSA_HEADS], bsz, seq)

    dq = MLA_NOPE + MLA_ROPE
    wq = w_uq.reshape(MLA_Q_RANK, MLA_HEADS, dq)
    wq_r = jnp.concatenate([wq[:, :, :MLA_NOPE].reshape(MLA_Q_RANK, -1),
                            wq[:, :, MLA_NOPE:].reshape(MLA_Q_RANK, -1)], axis=1).astype(BF16)
    qf = normproj_slabs(proj[s_cq:s_ckv], q_norm_g, wq_r, tn=512)
    kvf = normproj_slabs(proj[s_ckv:s_tail], kv_norm_g, w_ukv.astype(BF16), tn=512)
    cos, sin = _rope_tables(seq, MLA_ROPE)
    cos = jnp.tile(cos, (bsz, 1))
    sin = jnp.tile(sin, (bsz, 1))
    mscale = dq ** -0.5 * LOG2E
    q_nope = _rms(qf[:MLA_HEADS], nope_g[0])
    q_pe = qf[MLA_HEADS:].reshape(MLA_HEADS // 2, m, 2, MLA_ROPE)
    q_pe = jnp.transpose(q_pe, (0, 2, 1, 3)).reshape(MLA_HEADS, m, MLA_ROPE)
    q_pe = _rope(_rms(q_pe, rope_g[0]), cos[None], sin[None])
    zpad = jnp.zeros((MLA_HEADS, m, 2 * LANE - dq), F32)
    q_mla_t = _t((jnp.concatenate([q_nope, q_pe, zpad], axis=-1) * mscale).astype(BF16))
    kvf = kvf.reshape(MLA_HEADS, 2, m, LANE)
    k_nope = _rms(kvf[:, 0], nope_g[1])
    k_pe = _rope(_rms(tail_v[:, :MLA_ROPE], rope_g[1]), cos, sin)
    k_mla = jnp.concatenate([k_nope, jnp.broadcast_to(k_pe[None], (MLA_HEADS, m, MLA_ROPE)), zpad],
                            axis=-1).astype(BF16)
    v_mla_t = _t(kvf[:, 1].astype(BF16))
    o_mla = mla_attention(q_mla_t, k_mla, v_mla_t, bsz, seq)
    w_o = w_out.astype(BF16)
    return resproj([(o_nsa, w_o[:nq_cols]), (o_mla, w_o[nq_cols:])], x2, gate, seq)


def _odd_mixer(h, x2, gate, dt, bsz, seq, w_in, w_out, qk_g):
    m = bsz * seq
    nt = m // QB
    nq = DSA_HEADS * HEAD_DIM
    nkv = DSA_KV_HEADS * HEAD_DIM
    niq = IDX_HEADS * IDX_DIM
    ncols = w_in.shape[1]
    npad = -(-ncols // 384) * 384
    proj = proj_slabs(h, _pad_cols(w_in, npad).astype(BF16), tn=384)
    s_k = nq // LANE
    s_v = s_k + nkv // LANE
    s_iq = s_v + nkv // LANE
    s_tail = s_iq + niq // LANE
    q_t = _t((_rms(proj[:s_k], qk_g[0]) * (HEAD_DIM ** -0.5 * LOG2E)).astype(BF16))
    k = _rms(proj[s_k:s_v], qk_g[1]).astype(BF16)
    v_t = _t(proj[s_v:s_iq].astype(BF16))
    cos, sin = _rope_tables(seq, IDX_ROPE)
    cos = jnp.tile(cos, (bsz, 1))
    sin = jnp.tile(sin, (bsz, 1))
    iq = proj[s_iq:s_tail].reshape(IDX_HEADS // 2, m, 2, IDX_DIM)
    iq = jnp.transpose(iq, (0, 2, 1, 3)).reshape(IDX_HEADS, m, IDX_DIM)
    iq = jnp.concatenate([_rope(iq[..., :IDX_ROPE], cos[None], sin[None]), iq[..., IDX_ROPE:]], axis=-1)
    iq = (iq * IDX_DIM ** -0.5).astype(BF16).reshape(IDX_HEADS, nt, QB, IDX_DIM)
    iq_t = jnp.transpose(iq, (1, 3, 0, 2)).reshape(nt, IDX_DIM, IDX_HEADS * QB)
    iq_t = jnp.pad(iq_t, ((0, 0), (0, LANE - IDX_DIM), (0, 0)))
    tail = proj[s_tail]
    ik = tail[:, :IDX_DIM]
    ik = jnp.concatenate([_rope(ik[:, :IDX_ROPE], cos, sin), ik[:, IDX_ROPE:]], axis=-1)
    ik = jnp.pad(ik, ((0, 0), (0, LANE - IDX_DIM))).astype(BF16)
    iw = (tail[:, IDX_DIM:IDX_DIM + IDX_HEADS] * IDX_HEADS ** -0.5).reshape(nt, QB, IDX_HEADS)
    iw_t = jnp.transpose(iw, (0, 2, 1)).reshape(nt, 1, IDX_HEADS * QB)
    o = dsa_attention(iq_t, iw_t, ik, q_t, k, v_t, dt, bsz, seq)
    return resproj([(o, w_out.astype(BF16))], x2, gate, seq)


def _conv_ffn(h, x2, gate, seq, w_up, conv_w, conv_b, w_down):
    a = ffn_up(h, w_up.astype(BF16), conv_w, conv_b, seq)
    return resproj([(a, w_down.astype(BF16))], x2, gate, seq)


def kernel(x, c, rel_bias, ada_w, ada_b, norm_g, ev_w_in, ev_w_out, nsa_qk_g, cmp_pe, cmp_w1, cmp_b1, cmp_w2, cmp_b2, mla_q_norm_g, mla_kv_norm_g, mla_w_uq, mla_w_ukv, mla_nope_g, mla_rope_g, od_w_in, od_w_out, dsa_qk_g, ffn_w_up, ffn_conv_w, ffn_conv_b, ffn_w_down):
    bsz, seq, d = x.shape
    depth = ada_w.shape[0]
    x2 = x.reshape(bsz * seq, d)
    mods = ada_all(c, ada_w, ada_b)
    dt, dc = bias_tiles(rel_bias)
    for i in range(depth):
        j = i // 2
        shift, scale, gate = jnp.split(mods[i, 0], 3, axis=-1)
        h = modnorm(x2, norm_g[i, 0], scale, shift, seq)
        if i % 2 == 0:
            x2 = _even_mixer(h, x2, gate, dt, dc, bsz, seq, ev_w_in[j], ev_w_out[j], nsa_qk_g[j],
                             cmp_pe[j], cmp_w1[j], cmp_b1[j], cmp_w2[j], cmp_b2[j], mla_q_norm_g[j],
                             mla_kv_norm_g[j], mla_w_uq[j], mla_w_ukv[j], mla_nope_g[j], mla_rope_g[j])
        else:
            x2 = _odd_mixer(h, x2, gate, dt, bsz, seq, od_w_in[j], od_w_out[j], dsa_qk_g[j])
        shift, scale, gate = jnp.split(mods[i, 1], 3, axis=-1)
        h = modnorm(x2, norm_g[i, 1], scale, shift, seq)
        x2 = _conv_ffn(h, x2, gate, seq, ffn_w_up[i], ffn_conv_w[i], ffn_conv_b[i], ffn_w_down[i])
    return x2.reshape(bsz, seq, d)
```

```python
import functools
import math

import numpy as np
import jax
import jax.numpy as jnp
from jax import lax
from jax.experimental import pallas as pl
from jax.experimental.pallas import tpu as pltpu

HEAD_DIM = 128
NSA_HEADS = 8
NSA_GROUPS = 2
NSA_HPG = NSA_HEADS // NSA_GROUPS
CMP_BLOCK = 32
CMP_STRIDE = 16
CMP_HIDDEN = 256
SEL_BLOCK = 64
SEL_TOP_N = 16
WINDOW = 512
MLA_HEADS = 8
MLA_Q_RANK = 512
MLA_KV_RANK = 256
MLA_NOPE = 128
MLA_ROPE = 64
MLA_V = 128
DSA_HEADS = 16
DSA_KV_HEADS = 4
DSA_HPG = DSA_HEADS // DSA_KV_HEADS
IDX_HEADS = 16
IDX_DIM = 64
IDX_ROPE = 32
DSA_TOPK_MAX = 256
REL_BUCKETS = 32
REL_MAX_DIST = 128
CONV_WIDTH = 3
ROPE_THETA = 10000.0
EPS = 1e-6
NEG = -1e30
FORCE = 1e9

LANE = 128
QB = 128
VMEM_LIMIT = 56 * 1024 * 1024

F32 = jnp.float32
BF16 = jnp.bfloat16


def _t5_thresholds():
    d = np.arange(0, 4 * REL_MAX_DIST)
    half = REL_BUCKETS // 2
    val = np.log(np.maximum(d, 1) / half) / math.log(REL_MAX_DIST / half) * (REL_BUCKETS - half)
    large = np.minimum(half + np.floor(np.maximum(val, 0.0)).astype(np.int64), REL_BUCKETS - 1)
    bucket = np.where(d < half, d, large)
    return [int(np.argmax(bucket >= b)) for b in range(1, REL_BUCKETS)]


T5_THR = _t5_thresholds()
T5_FAR = T5_THR[-1]
assert T5_FAR <= LANE


def _cparams(sem):
    return pltpu.CompilerParams(dimension_semantics=sem, vmem_limit_bytes=VMEM_LIMIT)


def _dot(a, b):
    return jnp.dot(a, b, preferred_element_type=F32)


def _ada_kernel(c_ref, w_ref, b_ref, o_ref):
    c = c_ref[...]
    a = c * jax.nn.sigmoid(c)
    o_ref[0] = jnp.dot(a, w_ref[0], preferred_element_type=F32,
                       precision=lax.Precision.HIGHEST) + b_ref[0]


def ada_all(c, ada_w, ada_b):
    depth, two, d, n3 = ada_w.shape
    bsz = c.shape[0]
    rows = 8
    cp = jnp.zeros((rows, d), F32).at[:bsz].set(c)
    w = ada_w.reshape(depth * two, d, n3)
    b = ada_b.reshape(depth * two, 1, n3)
    tn = 512
    out = pl.pallas_call(
        _ada_kernel,
        grid=(depth * two, n3 // tn),
        in_specs=[pl.BlockSpec((rows, d), lambda l, j: (0, 0)),
                  pl.BlockSpec((1, d, tn), lambda l, j: (l, 0, j)),
                  pl.BlockSpec((1, 1, tn), lambda l, j: (l, 0, j))],
        out_specs=pl.BlockSpec((1, rows, tn), lambda l, j: (l, 0, j)),
        out_shape=jax.ShapeDtypeStruct((depth * two, rows, n3), F32),
        compiler_params=_cparams(("arbitrary", "arbitrary")),
        name="ada_mod",
    )(cp, w, b)
    return out[:, :bsz].reshape(depth, two, bsz, n3)


def _modnorm_kernel(x_ref, g_ref, sc_ref, sh_ref, o_ref):
    x = x_ref[...]
    y = x * lax.rsqrt(jnp.mean(x * x, axis=-1, keepdims=True) + EPS)
    h = (y * g_ref[...]) * (1.0 + sc_ref[0]) + sh_ref[0]
    o_ref[...] = h.astype(o_ref.dtype)


def modnorm(x2, g, scale, shift, seq):
    m, d = x2.shape
    tm = 512
    tpb = seq // tm
    return pl.pallas_call(
        _modnorm_kernel,
        grid=(m // tm,),
        in_specs=[pl.BlockSpec((tm, d), lambda i: (i, 0)),
                  pl.BlockSpec((1, d), lambda i: (0, 0)),
                  pl.BlockSpec((1, 1, d), lambda i: (i // tpb, 0, 0)),
                  pl.BlockSpec((1, 1, d), lambda i: (i // tpb, 0, 0))],
        out_specs=pl.BlockSpec((tm, d), lambda i: (i, 0)),
        out_shape=jax.ShapeDtypeStruct((m, d), BF16),
        compiler_params=_cparams(("arbitrary",)),
        name="modnorm",
    )(x2, g.reshape(1, d), scale.reshape(-1, 1, d), shift.reshape(-1, 1, d))


def _proj_kernel(x_ref, w_ref, o_ref, *, nslab):
    acc = _dot(x_ref[...], w_ref[...])
    for s in range(nslab):
        o_ref[s] = acc[:, s * LANE:(s + 1) * LANE]


def proj_slabs(x, w, tm=1024, tn=384):
    m, k = x.shape
    n = w.shape[1]
    assert n % tn == 0 and m % tm == 0
    nslab = tn // LANE
    return pl.pallas_call(
        functools.partial(_proj_kernel, nslab=nslab),
        grid=(m // tm, n // tn),
        in_specs=[pl.BlockSpec((tm, k), lambda i, j: (i, 0)),
                  pl.BlockSpec((k, tn), lambda i, j: (0, j))],
        out_specs=pl.BlockSpec((nslab, tm, LANE), lambda i, j: (j, i, 0)),
        out_shape=jax.ShapeDtypeStruct((n // LANE, m, LANE), F32),
        compiler_params=_cparams(("arbitrary", "arbitrary")),
        name="proj_slabs",
    )(x, w)


def _normproj_kernel(x_ref, g_ref, w_ref, o_ref, *, kslab, nslab):
    x = jnp.concatenate([x_ref[s] for s in range(kslab)], axis=1)
    y = x * lax.rsqrt(jnp.mean(x * x, axis=-1, keepdims=True) + EPS) * g_ref[...]
    acc = _dot(y.astype(BF16), w_ref[...])
    for s in range(nslab):
        o_ref[s] = acc[:, s * LANE:(s + 1) * LANE]


def normproj_slabs(x_slabs, g, w, tm=512, tn=512):
    kslab, m, _ = x_slabs.shape
    k = kslab * LANE
    n = w.shape[1]
    assert n % tn == 0
    nslab = tn // LANE
    return pl.pallas_call(
        functools.partial(_normproj_kernel, kslab=kslab, nslab=nslab),
        grid=(m // tm, n // tn),
        in_specs=[pl.BlockSpec((kslab, tm, LANE), lambda i, j: (0, i, 0)),
                  pl.BlockSpec((1, k), lambda i, j: (0, 0)),
                  pl.BlockSpec((k, tn), lambda i, j: (0, j))],
        out_specs=pl.BlockSpec((nslab, tm, LANE), lambda i, j: (j, i, 0)),
        out_shape=jax.ShapeDtypeStruct((n // LANE, m, LANE), F32),
        compiler_params=_cparams(("arbitrary", "arbitrary")),
        name="normproj_slabs",
    )(x_slabs, g.reshape(1, k), w)


def _resproj_kernel(*refs, npair):
    xres_ref, gate_ref = refs[2 * npair], refs[2 * npair + 1]
    o_ref = refs[2 * npair + 2]
    acc = _dot(refs[0][...], refs[1][...])
    for p in range(1, npair):
        acc = acc + _dot(refs[2 * p][...], refs[2 * p + 1][...])
    o_ref[...] = xres_ref[...] + gate_ref[0] * acc


def resproj(pairs, xres, gate, seq, tm=1024, tn=512):
    m, n = xres.shape
    tpb = seq // tm
    in_specs, args = [], []
    for x, w in pairs:
        k = x.shape[1]
        in_specs += [pl.BlockSpec((tm, k), lambda i, j: (i, 0)),
                     pl.BlockSpec((k, tn), lambda i, j: (0, j))]
        args += [x, w]
    in_specs += [pl.BlockSpec((tm, tn), lambda i, j: (i, j)),
                 pl.BlockSpec((1, 1, tn), lambda i, j: (i // tpb, 0, j))]
    args += [xres, gate.reshape(-1, 1, n)]
    return pl.pallas_call(
        functools.partial(_resproj_kernel, npair=len(pairs)),
        grid=(m // tm, n // tn),
        in_specs=in_specs,
        out_specs=pl.BlockSpec((tm, tn), lambda i, j: (i, j)),
        out_shape=jax.ShapeDtypeStruct((m, n), F32),
        compiler_params=_cparams(("arbitrary", "arbitrary")),
        name="resproj",
    )(*args)


HALO = 8


def _ffn_up_kernel(h_ref, wg_ref, wv_ref, cwg_ref, cwv_ref, cbg_ref, cbv_ref, o_ref,
                   ug_ref, uv_ref, *, tm, tiles_per_seq):
    i = pl.program_id(1)
    first = (i % tiles_per_seq) == 0

    @pl.when(first)
    def _():
        ug_ref[0:HALO, :] = jnp.zeros((HALO, ug_ref.shape[1]), F32)
        uv_ref[0:HALO, :] = jnp.zeros((HALO, uv_ref.shape[1]), F32)

    @pl.when(jnp.logical_not(first))
    def _():
        ug_ref[0:HALO, :] = ug_ref[tm:tm + HALO, :]
        uv_ref[0:HALO, :] = uv_ref[tm:tm + HALO, :]

    h = h_ref[...]
    ug_ref[HALO:HALO + tm, :] = _dot(h, wg_ref[...])
    uv_ref[HALO:HALO + tm, :] = _dot(h, wv_ref[...])

    def conv(u_ref, cw_ref, cb_ref):
        out = cb_ref[...]
        for j in range(CONV_WIDTH):
            off = HALO - (CONV_WIDTH - 1) + j
            out = out + cw_ref[j:j + 1, :] * u_ref[off:off + tm, :]
        return out

    g = conv(ug_ref, cwg_ref, cbg_ref)
    v = conv(uv_ref, cwv_ref, cbv_ref)
    o_ref[...] = (g * jax.nn.sigmoid(g) * v).astype(o_ref.dtype)


def ffn_up(h, w_up, conv_w, conv_b, seq, tm=1024, tn=512):
    m, d = h.shape
    f = w_up.shape[1] // 2
    nj = f // tn
    tps = seq // tm
    cb = conv_b.reshape(1, 2 * f)
    return pl.pallas_call(
        functools.partial(_ffn_up_kernel, tm=tm, tiles_per_seq=tps),
        grid=(nj, m // tm),
        in_specs=[pl.BlockSpec((tm, d), lambda j, i: (i, 0)),
                  pl.BlockSpec((d, tn), lambda j, i: (0, j)),
                  pl.BlockSpec((d, tn), lambda j, i: (0, nj + j)),
                  pl.BlockSpec((CONV_WIDTH, tn), lambda j, i: (0, j)),
                  pl.BlockSpec((CONV_WIDTH, tn), lambda j, i: (0, nj + j)),
                  pl.BlockSpec((1, tn), lambda j, i: (0, j)),
                  pl.BlockSpec((1, tn), lambda j, i: (0, nj + j))],
        out_specs=pl.BlockSpec((tm, tn), lambda j, i: (i, j)),
        out_shape=jax.ShapeDtypeStruct((m, f), BF16),
        scratch_shapes=[pltpu.VMEM((tm + HALO, tn), F32), pltpu.VMEM((tm + HALO, tn), F32)],
        compiler_params=_cparams(("arbitrary", "arbitrary")),
        name="ffn_up_conv",
    )(h, w_up, w_up, conv_w, conv_w, cb, cb)


LOG2E = 1.4426950408889634
CWIN = 16


def _t5_shifted(dist, tbl_ref, h):
    val = jnp.full(dist.shape, tbl_ref[0, h], F32)
    for b in range(1, REL_BUCKETS):
        val = jnp.where(dist >= T5_THR[b - 1], tbl_ref[b, h], val)
    return (val - tbl_ref[REL_BUCKETS - 1, h]) * LOG2E


def _bias_tiles_kernel(tbl_ref, dt_ref, dc_ref):
    h = pl.program_id(0)
    key = lax.broadcasted_iota(jnp.int32, (LANE, LANE), 0)
    q = lax.broadcasted_iota(jnp.int32, (LANE, LANE), 1)
    for rel in range(2):
        dt_ref[0, rel] = _t5_shifted(rel * LANE + q - key, tbl_ref, h)
    dt_ref[0, 2] = jnp.zeros((LANE, LANE), F32)
    u = lax.broadcasted_iota(jnp.int32, (CWIN, LANE), 0)
    qc = lax.broadcasted_iota(jnp.int32, (CWIN, LANE), 1)
    dc_ref[0] = _t5_shifted(qc - CMP_STRIDE * (u - CWIN // 2) - (CMP_BLOCK - 1), tbl_ref, h)


def bias_tiles(rel_bias):
    nh = rel_bias.shape[1]
    return pl.pallas_call(
        _bias_tiles_kernel,
        grid=(nh,),
        in_specs=[pl.BlockSpec(memory_space=pltpu.SMEM)],
        out_specs=[pl.BlockSpec((1, 3, LANE, LANE), lambda h: (h, 0, 0, 0)),
                   pl.BlockSpec((1, CWIN, LANE), lambda h: (h, 0, 0))],
        out_shape=[jax.ShapeDtypeStruct((nh, 3, LANE, LANE), F32),
                   jax.ShapeDtypeStruct((nh, CWIN, LANE), F32)],
        compiler_params=_cparams(("arbitrary",)),
        name="t5_bias_tiles",
    )(rel_bias)


def _compress_kernel(x_ref, pe_ref, w1_ref, b1_ref, w2_ref, b2_ref, g_ref, o_ref, *, half):
    kv = pl.program_id(0)
    x = x_ref[0]
    a = _dot((x + pe_ref[0, :, :half]).astype(BF16), w1_ref[0, :half, :])
    b = _dot((x + pe_ref[0, :, half:]).astype(BF16), w1_ref[0, half:, :])
    b_next = jnp.concatenate([b[1:], jnp.zeros((1, b.shape[1]), F32)], axis=0)
    hid = jax.nn.gelu(a + b_next + b1_ref[0])
    out = _dot(hid.astype(BF16), w2_ref[0]) + b2_ref[0]
    normed = out * lax.rsqrt(jnp.mean(out * out, axis=-1, keepdims=True) + EPS) * g_ref[...]
    out = jnp.where(kv == 0, normed, out)
    o_ref[0, 0] = out.astype(o_ref.dtype)


def compress_kv(proj, slab0, bsz, seq, cmp_pe, cmp_w1, cmp_b1, cmp_w2, cmp_b2, g_k):
    nslab, m, _ = proj.shape
    nchunk = seq // CMP_STRIDE
    half = CMP_STRIDE * HEAD_DIM
    xv = proj.reshape(nslab, m // CMP_STRIDE, half)
    pe = cmp_pe.reshape(2, 1, CMP_BLOCK * HEAD_DIM)
    return pl.pallas_call(
        functools.partial(_compress_kernel, half=half),
        grid=(2, bsz, NSA_GROUPS),
        in_specs=[pl.BlockSpec((1, nchunk, half), lambda kv, b, g: (slab0 + 2 * kv + g, b, 0)),
                  pl.BlockSpec((1, 1, 2 * half), lambda kv, b, g: (kv, 0, 0)),
                  pl.BlockSpec((1, 2 * half, CMP_HIDDEN), lambda kv, b, g: (kv, 0, 0)),
                  pl.BlockSpec((1, 1, CMP_HIDDEN), lambda kv, b, g: (kv, 0, 0)),
                  pl.BlockSpec((1, CMP_HIDDEN, HEAD_DIM), lambda kv, b, g: (kv, 0, 0)),
                  pl.BlockSpec((1, 1, HEAD_DIM), lambda kv, b, g: (kv, 0, 0)),
                  pl.BlockSpec((1, HEAD_DIM), lambda kv, b, g: (0, 0))],
        out_specs=pl.BlockSpec((1, 1, nchunk, HEAD_DIM), lambda kv, b, g: (kv, g, b, 0)),
        out_shape=jax.ShapeDtypeStruct((2, NSA_GROUPS, bsz * nchunk, HEAD_DIM), BF16),
        compiler_params=_cparams(("arbitrary", "arbitrary", "arbitrary")),
        name="nsa_compress",
    )(xv, pe, cmp_w1.astype(BF16), cmp_b1.reshape(2, 1, CMP_HIDDEN), cmp_w2.astype(BF16),
      cmp_b2.reshape(2, 1, HEAD_DIM), g_k.reshape(1, HEAD_DIM))


KW = 512
PV_KEYS = 256


def _tile_lanes(x, n):
    return jnp.concatenate([x] * n, axis=1)


def _flash_init(m_ref, l_ref, acc_ref):
    m_ref[...] = jnp.full(m_ref.shape, NEG, F32)
    l_ref[...] = jnp.zeros(l_ref.shape, F32)
    acc_ref[...] = jnp.zeros(acc_ref.shape, F32)


def _zero_after(x):
    bits = pltpu.bitcast(x, jnp.int32)
    return lax.shift_right_logical(lax.shift_right_logical(bits, 16), 16).astype(F32)


def _flash_update(s, v_t, m_ref, l_ref, acc_ref, col_max=None, after=None):
    m_old = m_ref[...]
    if col_max is None:
        col_max = jnp.max(s, axis=0, keepdims=True)
    m_new = jnp.maximum(m_old, col_max)
    alpha = jnp.exp2(m_old - m_new)
    l_new = alpha * l_ref[...]
    acc = alpha * acc_ref[...]
    nk = s.shape[0]
    for k0 in range(0, nk, PV_KEYS):
        p = jnp.exp2(s[k0:k0 + PV_KEYS] - m_new)
        l_new = l_new + jnp.sum(p, axis=0, keepdims=True)
        acc = acc + _dot(v_t[:, k0:k0 + PV_KEYS], p.astype(BF16))
    l_ref[...] = l_new
    acc_ref[...] = acc
    m_ref[...] = m_new if after is None else m_new + _zero_after(after)


def _inv_den(m, den):
    ok = m > 0.5 * NEG
    return jnp.where(ok, 1.0 / jnp.where(ok, den, 1.0), 0.0)


def _flash_result(m_ref, l_ref, acc_ref):
    return acc_ref[...] * _inv_den(m_ref[...], l_ref[...])


def _softmax_cols(s):
    m = jnp.max(s, axis=0, keepdims=True)
    p = jnp.exp2(s - m)
    return p * _inv_den(m, jnp.sum(p, axis=0, keepdims=True))


def _near_bias(dt_ref, heads, qi, kt0, ntile):
    rows = []
    for j in range(ntile):
        rel = jnp.clip(qi - (kt0 + j), 0, 2)
        rows.append(jnp.concatenate([dt_ref[h, rel] for h in heads], axis=1))
    return jnp.concatenate(rows, axis=0)


def _pipelined_chunks(n, qk_stage, soft_stage):
    @pl.when(n > 0)
    def _():
        qk_stage(0, 0)

    def pair(p, x):
        c = 2 * p
        ahead = qk_stage(c + 1, 1)
        soft_stage(c, 0, ahead)
        ahead = qk_stage(jnp.minimum(c + 2, n - 1), 0)
        soft_stage(c + 1, 1, ahead)
        return x

    lax.fori_loop(0, n // 2, pair, 0)

    @pl.when(n % 2 == 1)
    def _():
        soft_stage(n - 1, 0, None)


def _nsa_kernel(qt_ref, gt_ref, kc_ref, vct_ref, ks_ref, vst_ref, kw_ref, vwt_ref,
                dt_ref, dc_ref, ext_ref, o_ref,
                sc_ref, ps_ref, m_ref, l_ref, acc_ref, sbuf0, sbuf1, cbuf0, cbuf1, *, seq, nc):
    sbuf, cbuf = (sbuf0, sbuf1), (cbuf0, cbuf1)
    qi = pl.program_id(2)
    q0 = qi * QB
    hpg = NSA_HPG
    heads = list(range(hpg))
    ncp = kc_ref.shape[1]
    ns = seq // SEL_BLOCK
    q_t = jnp.concatenate([qt_ref[h] for h in heads], axis=1)

    pad = CWIN // 2
    sc_ref[0:pad, :] = jnp.zeros((pad, hpg * QB), F32)
    sc_ref[pad + ncp:2 * pad + ncp, :] = jnp.zeros((pad, hpg * QB), F32)
    sc_ref[pad:pad + ncp, :] = _dot(kc_ref[0], q_t)
    r0 = pl.multiple_of(qi * (QB // CMP_STRIDE), 8)
    sc_ref[pl.ds(r0, CWIN), :] = sc_ref[pl.ds(r0, CWIN), :] + jnp.concatenate(
        [dc_ref[h] for h in heads], axis=1)
    ci = lax.broadcasted_iota(jnp.int32, (ncp, QB), 0)
    tc = q0 + lax.broadcasted_iota(jnp.int32, (ncp, QB), 1)
    valid_c = (ci * CMP_STRIDE + CMP_BLOCK - 1 <= tc) & (ci < nc)
    p_c = _softmax_cols(sc_ref[pad:pad + ncp, :] + _tile_lanes(jnp.where(valid_c, 0.0, NEG), hpg))
    oc_t = _dot(vct_ref[0], p_c.astype(BF16))
    p_sum = p_c[:, 0:QB]
    for h in range(1, hpg):
        p_sum = p_sum + p_c[:, h * QB:(h + 1) * QB]

    wkeys = WINDOW + QB
    start = pl.multiple_of(jnp.maximum(q0 - WINDOW, 0), LANE)
    s_w = _dot(kw_ref[0, pl.ds(start, wkeys), :], q_t)
    s_w = s_w + _near_bias(dt_ref, heads, qi, start // LANE, wkeys // LANE)
    dist_w = (q0 + lax.broadcasted_iota(jnp.int32, (wkeys, QB), 1)) - (
        start + lax.broadcasted_iota(jnp.int32, (wkeys, QB), 0))
    mask_w = (dist_w >= 0) & (dist_w < WINDOW)
    p_w = _softmax_cols(s_w + _tile_lanes(jnp.where(mask_w, 0.0, NEG), hpg))
    ow_t = _dot(vwt_ref[0, :, pl.ds(start, wkeys)], p_w.astype(BF16))

    ps_ref[0:8, :] = jnp.zeros((8, QB), F32)
    ps_ref[8:8 + ncp, :] = p_sum
    per = SEL_BLOCK // CMP_STRIDE
    band = [ps_ref[pl.ds(8 + r, ns, stride=per), :] for r in range(-1, per)]
    imp = 0.5 * band[0] + band[1] + band[2] + band[3] + 0.5 * band[4]
    if ns < LANE:
        imp = jnp.concatenate([imp, jnp.zeros((LANE - ns, QB), F32)], axis=0)
    blk = lax.broadcasted_iota(jnp.int32, (LANE, QB), 0)
    t = q0 + lax.broadcasted_iota(jnp.int32, (LANE, QB), 1)
    tb = t // SEL_BLOCK
    forced = (blk == 0) | (blk == tb) | (blk == tb - 1)
    score = jnp.where(forced, FORCE, jnp.where(blk * SEL_BLOCK <= t, imp, NEG))
    score = jnp.where(blk < ns, score, -jnp.inf)
    blk_f = blk.astype(F32)
    sel = jnp.zeros((LANE, QB), F32)
    for _ in range(min(SEL_TOP_N, ns)):
        mx = jnp.max(score, axis=0, keepdims=True)
        first = jnp.min(jnp.where(score == mx, blk_f, float(LANE)), axis=0, keepdims=True)
        pick = blk_f == first
        sel = jnp.where(pick, 1.0, sel)
        score = jnp.where(pick, -jnp.inf, score)
    sel_b = sel.astype(BF16)

    _flash_init(m_ref, l_ref, acc_ref)
    kpos = lax.broadcasted_iota(jnp.int32, (KW, QB), 0)
    tq = q0 + lax.broadcasted_iota(jnp.int32, (KW, QB), 1)

    def block_mask(c0):
        chosen = _dot(ext_ref[pl.ds(c0, KW), :], sel_b)
        return (chosen - 1.0) * (-NEG)

    def qk_stage(c, buf):
        c0 = pl.multiple_of(c * KW, KW)
        s = _dot(ks_ref[0, pl.ds(c0, KW), :], q_t) + _tile_lanes(block_mask(c0), hpg)
        sbuf[buf][...] = s
        col_max = jnp.max(s, axis=0, keepdims=True)
        cbuf[buf][...] = col_max
        return col_max

    def soft_stage(c, buf, ahead):
        c0 = pl.multiple_of(c * KW, KW)
        _flash_update(sbuf[buf][...], vst_ref[0, :, pl.ds(c0, KW)], m_ref, l_ref, acc_ref,
                      col_max=cbuf[buf][...], after=ahead)

    def near_step(c, x):
        c0 = pl.multiple_of(c * KW, KW)
        madd = block_mask(c0) + jnp.where(c0 + kpos <= tq, 0.0, NEG)
        s = _dot(ks_ref[0, pl.ds(c0, KW), :], q_t) + _near_bias(dt_ref, heads, qi, c * (KW // LANE), KW // LANE)
        _flash_update(s + _tile_lanes(madd, hpg), vst_ref[0, :, pl.ds(c0, KW)], m_ref, l_ref, acc_ref)
        return x

    c_near = jnp.maximum(qi - 1, 0) // (KW // LANE)
    _pipelined_chunks(c_near, qk_stage, soft_stage)
    lax.fori_loop(c_near, qi // (KW // LANE) + 1, near_step, 0)
    os_t = _flash_result(m_ref, l_ref, acc_ref)

    gates = jax.nn.sigmoid(gt_ref[0])
    for h in heads:
        sl = slice(h * QB, (h + 1) * QB)
        o_t = (gates[3 * h:3 * h + 1] * oc_t[:, sl] + gates[3 * h + 1:3 * h + 2] * os_t[:, sl]
               + gates[3 * h + 2:3 * h + 3] * ow_t[:, sl])
        o_ref[:, h * HEAD_DIM:(h + 1) * HEAD_DIM] = o_t.T.astype(o_ref.dtype)


def nsa_attention(q_t, gates_t, kc, vc_t, ks, vs_t, kw, vw_t, dt, dc, bsz, seq):
    nq = seq // QB
    ncp = seq // CMP_STRIDE
    nc = ncp - 1
    ns = seq // SEL_BLOCK
    assert ns <= LANE and seq >= WINDOW + QB and seq % KW == 0
    assert CMP_BLOCK == 2 * CMP_STRIDE and SEL_BLOCK == 4 * CMP_STRIDE
    expand =((np.arange(seq)[:, None] // SEL_BLOCK) == np.arange(LANE)[None, :]).astype(np.float32)
    k_spec = pl.BlockSpec((1, seq, HEAD_DIM), lambda b, g, i: (g, b, 0))
    vt_spec = pl.BlockSpec((1, HEAD_DIM, seq), lambda b, g, i: (g, 0, b))
    lanes = NSA_HPG * QB
    return pl.pallas_call(
        functools.partial(_nsa_kernel, seq=seq, nc=nc),
        grid=(bsz, NSA_GROUPS, nq),
        in_specs=[pl.BlockSpec((NSA_HPG, HEAD_DIM, QB), lambda b, g, i: (g, 0, b * nq + i)),
                  pl.BlockSpec((1, 16, QB), lambda b, g, i: (g, 0, b * nq + i)),
                  pl.BlockSpec((1, ncp, HEAD_DIM), lambda b, g, i: (g, b, 0)),
                  pl.BlockSpec((1, HEAD_DIM, ncp), lambda b, g, i: (g, 0, b)),
                  k_spec, vt_spec, k_spec, vt_spec,
                  pl.BlockSpec((NSA_HPG, 3, LANE, LANE), lambda b, g, i: (g, 0, 0, 0)),
                  pl.BlockSpec((NSA_HPG, CWIN, LANE), lambda b, g, i: (g, 0, 0)),
                  pl.BlockSpec((seq, LANE), lambda b, g, i: (0, 0))],
        out_specs=pl.BlockSpec((QB, NSA_HPG * HEAD_DIM), lambda b, g, i: (b * nq + i, g)),
        out_shape=jax.ShapeDtypeStruct((bsz * seq, NSA_HEADS * HEAD_DIM), BF16),
        scratch_shapes=[pltpu.VMEM((ncp + CWIN, lanes), F32), pltpu.VMEM((ncp + 8, QB), F32),
                        pltpu.VMEM((1, lanes), F32), pltpu.VMEM((1, lanes), F32),
                        pltpu.VMEM((HEAD_DIM, lanes), F32),
                        pltpu.VMEM((KW, lanes), F32), pltpu.VMEM((KW, lanes), F32),
                        pltpu.VMEM((1, lanes), F32), pltpu.VMEM((1, lanes), F32)],
        compiler_params=_cparams(("arbitrary", "arbitrary", "arbitrary")),
        name="nsa_attention",
    )(q_t, gates_t, kc, vc_t, ks, vs_t, kw, vw_t, dt, dc, jnp.asarray(expand, BF16))


MLA_HPS = 2


def _mla_kernel(qt_ref, k_ref, vt_ref, o_ref, *scratch):
    qi = pl.program_id(2)
    chains = [scratch[3 * h:3 * h + 3] for h in range(MLA_HPS)]
    sbuf = [scratch[(3 + b) * MLA_HPS:(4 + b) * MLA_HPS] for b in range(2)]
    cbuf = [scratch[(5 + b) * MLA_HPS:(6 + b) * MLA_HPS] for b in range(2)]
    for ch in chains:
        _flash_init(*ch)

    def qk_stage(c, buf):
        c0 = pl.multiple_of(c * KW, KW)
        col_max = []
        for h in range(MLA_HPS):
            s = _dot(k_ref[h, pl.ds(c0, KW), :], qt_ref[h])
            sbuf[buf][h][...] = s
            col_max.append(jnp.max(s, axis=0, keepdims=True))
            cbuf[buf][h][...] = col_max[-1]
        return col_max

    def soft_stage(c, buf, ahead):
        c0 = pl.multiple_of(c * KW, KW)
        for h, ch in enumerate(chains):
            _flash_update(sbuf[buf][h][...], vt_ref[h, :, pl.ds(c0, KW)], *ch, col_max=cbuf[buf][h][...],
                          after=None if ahead is None else ahead[h])

    _pipelined_chunks(qi, qk_stage, soft_stage)
    c0 = pl.multiple_of(qi * KW, KW)
    kpos = lax.broadcasted_iota(jnp.int32, (KW, KW), 0)
    tq = lax.broadcasted_iota(jnp.int32, (KW, KW), 1)
    causal = jnp.where(kpos <= tq, 0.0, NEG)
    dv = vt_ref.shape[1]
    scores = [_dot(k_ref[h, pl.ds(c0, KW), :], qt_ref[h]) + causal for h in range(MLA_HPS)]
    for h, ch in enumerate(chains):
        _flash_update(scores[h], vt_ref[h, :, pl.ds(c0, KW)], *ch)
        o_ref[:, h * dv:(h + 1) * dv] = _flash_result(*ch).T.astype(o_ref.dtype)


def mla_attention(q_t, k, v_t, bsz, seq):
    nh, dqk, _ = q_t.shape
    dv = v_t.shape[1]
    nq = seq // KW
    hps = MLA_HPS
    state = [pltpu.VMEM((1, KW), F32), pltpu.VMEM((1, KW), F32), pltpu.VMEM((dv, KW), F32)] * hps
    state += [pltpu.VMEM((KW, KW), F32)] * (2 * hps)
    state += [pltpu.VMEM((1, KW), F32)] * (2 * hps)
    return pl.pallas_call(
        _mla_kernel,
        grid=(bsz, nh // hps, nq),
        in_specs=[pl.BlockSpec((hps, dqk, KW), lambda b, h, i: (h, 0, b * nq + i)),
                  pl.BlockSpec((hps, seq, dqk), lambda b, h, i: (h, b, 0)),
                  pl.BlockSpec((hps, dv, seq), lambda b, h, i: (h, 0, b))],
        out_specs=pl.BlockSpec((KW, hps * dv), lambda b, h, i: (b * nq + i, h)),
        out_shape=jax.ShapeDtypeStruct((bsz * seq, nh * dv), BF16),
        scratch_shapes=state,
        compiler_params=_cparams(("arbitrary", "arbitrary", "arbitrary")),
        name="mla_attention",
    )(q_t, k, v_t)


INT_MIN = -2 ** 31
NEG_KEY = int(np.array(NEG, np.float32).view(np.int32)) ^ 0x7FFFFFFF


def _sort_key(x):
    bits = pltpu.bitcast(x + 0.0, jnp.int32)
    return jnp.where(bits < 0, bits ^ 0x7FFFFFFF, bits)


def _dsa_kernel(iqt_ref, iwt_ref, ik_ref, qt_ref, k_ref, vt_ref, dt_ref, o_ref,
                key_ref, madd_ref, *state, seq, k_sel):
    qi = pl.program_id(1)
    q0 = qi * QB
    n_chunk = (q0 + QB + KW - 1) // KW
    n_rest = seq - n_chunk * KW
    kpos = lax.broadcasted_iota(jnp.int32, (KW, QB), 0)
    tq = q0 + lax.broadcasted_iota(jnp.int32, (KW, QB), 1)
    hpp = KW // QB

    def score_chunk(c, x):
        c0 = pl.multiple_of(c * KW, KW)
        ikc = ik_ref[pl.ds(c0, KW), :]
        acc = jnp.zeros((KW, QB), F32)
        for piece in range(IDX_HEADS // hpp):
            sl = slice(piece * KW, (piece + 1) * KW)
            s = jnp.maximum(_dot(ikc, iqt_ref[0, :, sl]), 0.0) * iwt_ref[0, :, sl]
            for j in range(hpp):
                acc = acc + s[:, j * QB:(j + 1) * QB]
        acc = jnp.where(c0 + kpos <= tq, acc, NEG)
        key_ref[pl.ds(c0, KW), :] = _sort_key(acc)
        return x

    lax.fori_loop(0, n_chunk, score_chunk, 0)

    def count(pred):
        def body(c, acc):
            c0 = pl.multiple_of(c * KW, KW)
            hit = jnp.where(pred(key_ref[pl.ds(c0, KW), :], c0), 1.0, 0.0)
            parts = [hit[8 * i:8 * (i + 1)] for i in range(KW // 8)]
            while len(parts) > 1:
                parts = [parts[i] + parts[i + 1] for i in range(0, len(parts), 2)]
            return acc + parts[0]
        acc = lax.fori_loop(0, n_chunk, body, jnp.zeros((8, QB), F32))
        return jnp.sum(acc, axis=0, keepdims=True)

    rest = n_rest.astype(F32)
    kf = float(k_sel)

    def bit_cond(st):
        i, _, _, settled = st
        return (i < 32) & (jnp.min(settled) < 0.5)

    def bit_step(st):
        i, u, thr_s, settled = st
        bit = jnp.left_shift(jnp.int32(1), 31 - i)
        trial = (u | bit) ^ INT_MIN
        cnt = count(lambda keys, c0: keys >= trial) + jnp.where(NEG_KEY >= trial, rest, 0.0)
        new = (cnt == kf) & (settled < 0.5)
        return (i + 1, jnp.where(cnt >= kf, u | bit, u), jnp.where(new, trial, thr_s),
                jnp.where(new, 1.0, settled))

    _, u, thr_s, settled = lax.while_loop(
        bit_cond, bit_step,
        (jnp.int32(0), jnp.zeros((1, QB), jnp.int32), jnp.zeros((1, QB), jnp.int32), jnp.zeros((1, QB), F32)))
    is_settled = settled > 0.5
    thr = jnp.where(is_settled, thr_s, u ^ INT_MIN)
    def edge_counts():
        return (count(lambda keys, c0: keys > thr) + jnp.where(NEG_KEY > thr, rest, 0.0),
                count(lambda keys, c0: keys >= thr) + jnp.where(NEG_KEY >= thr, rest, 0.0))

    zero_cnt = jnp.zeros((1, QB), F32)
    cnt_gt, cnt_ge = lax.cond(jnp.min(settled) > 0.5, lambda: (zero_cnt, zero_cnt), edge_counts)
    need = kf - cnt_gt
    tie_q = (cnt_ge > kf) & (thr != NEG_KEY) & jnp.logical_not(is_settled)
    idx_bits = (seq - 1).bit_length()
    no_cut = 2 ** 30

    def tie_cut():
        def idx_step(i, x):
            bit = jnp.left_shift(jnp.int32(1), idx_bits - 1 - i)
            trial = x | bit
            f = count(lambda keys, c0: (keys == thr) & (c0 + kpos < trial))
            return jnp.where(f <= need - 1.0, trial, x)
        return lax.fori_loop(0, idx_bits, idx_step, jnp.zeros((1, QB), jnp.int32))

    any_tie = jnp.max(jnp.where(tie_q, 1.0, 0.0)) > 0.0
    x_cut = lax.cond(any_tie, tie_cut, lambda: jnp.full((1, QB), no_cut, jnp.int32))
    x_cut = jnp.where(tie_q, x_cut, no_cut)

    def mask_chunk(c, x):
        c0 = pl.multiple_of(c * KW, KW)
        keys = key_ref[pl.ds(c0, KW), :]
        pos = c0 + kpos
        chosen = (keys > thr) | ((keys == thr) & (pos <= x_cut))
        madd_ref[pl.ds(c0, KW), :] = jnp.where(chosen & (pos <= tq), 0.0, NEG)
        return x

    lax.fori_loop(0, n_chunk, mask_chunk, 0)

    c_near = jnp.maximum(qi - 1, 0) // (KW // LANE)
    ng = DSA_KV_HEADS
    chains = [state[3 * g:3 * g + 3] for g in range(ng)]
    sbuf = [state[(3 + b) * ng:(4 + b) * ng] for b in range(2)]
    cbuf = [state[(5 + b) * ng:(6 + b) * ng] for b in range(2)]
    group_heads = [[g * DSA_HPG + h for h in range(DSA_HPG)] for g in range(ng)]
    for ch in chains:
        _flash_init(*ch)

    def raw_scores(c0, g):
        q_t = jnp.concatenate([qt_ref[h] for h in group_heads[g]], axis=1)
        return _dot(k_ref[g, pl.ds(c0, KW), :], q_t)

    def qk_stage(c, buf):
        c0 = pl.multiple_of(c * KW, KW)
        madd = _tile_lanes(madd_ref[pl.ds(c0, KW), :], DSA_HPG)
        col_max = []
        for g in range(ng):
            s = raw_scores(c0, g) + madd
            sbuf[buf][g][...] = s
            col_max.append(jnp.max(s, axis=0, keepdims=True))
            cbuf[buf][g][...] = col_max[-1]
        return col_max

    def soft_stage(c, buf, ahead):
        c0 = pl.multiple_of(c * KW, KW)
        for g, ch in enumerate(chains):
            _flash_update(sbuf[buf][g][...], vt_ref[g, :, pl.ds(c0, KW)], *ch, col_max=cbuf[buf][g][...],
                          after=None if ahead is None else ahead[g])

    _pipelined_chunks(c_near, qk_stage, soft_stage)

    def near_step(c, x):
        c0 = pl.multiple_of(c * KW, KW)
        madd = _tile_lanes(madd_ref[pl.ds(c0, KW), :], DSA_HPG)
        scores = [raw_scores(c0, g) + madd + _near_bias(dt_ref, group_heads[g], qi, c * (KW // LANE), KW // LANE)
                  for g in range(ng)]
        for g, ch in enumerate(chains):
            _flash_update(scores[g], vt_ref[g, :, pl.ds(c0, KW)], *ch)
        return x

    lax.fori_loop(c_near, n_chunk, near_step, 0)
    for g, ch in enumerate(chains):
        o_t = _flash_result(*ch)
        for h in range(DSA_HPG):
            hh = group_heads[g][h]
            o_ref[:, hh * HEAD_DIM:(hh + 1) * HEAD_DIM] = o_t[:, h * QB:(h + 1) * QB].T.astype(o_ref.dtype)


def dsa_attention(iq_t, iw_t, ik, q_t, k, v_t, dt, bsz, seq):
    nq = seq // QB
    k_sel = min(DSA_TOPK_MAX, seq // 4)
    assert seq % KW == 0
    lanes = DSA_HPG * QB
    return pl.pallas_call(
        functools.partial(_dsa_kernel, seq=seq, k_sel=k_sel),
        grid=(bsz, nq),
        in_specs=[pl.BlockSpec((1, LANE, IDX_HEADS * QB), lambda b, i: (b * nq + i, 0, 0)),
                  pl.BlockSpec((1, 1, IDX_HEADS * QB), lambda b, i: (b * nq + i, 0, 0)),
                  pl.BlockSpec((seq, LANE), lambda b, i: (b, 0)),
                  pl.BlockSpec((DSA_HEADS, HEAD_DIM, QB), lambda b, i: (0, 0, b * nq + i)),
                  pl.BlockSpec((DSA_KV_HEADS, seq, HEAD_DIM), lambda b, i: (0, b, 0),
                               pipeline_mode=pl.Buffered(1)),
                  pl.BlockSpec((DSA_KV_HEADS, HEAD_DIM, seq), lambda b, i: (0, 0, b),
                               pipeline_mode=pl.Buffered(1)),
                  pl.BlockSpec((DSA_HEADS, 3, LANE, LANE), lambda b, i: (0, 0, 0, 0),
                               pipeline_mode=pl.Buffered(1))],
        out_specs=pl.BlockSpec((QB, DSA_HEADS * HEAD_DIM), lambda b, i: (b * nq + i, 0)),
        out_shape=jax.ShapeDtypeStruct((bsz * seq, DSA_HEADS * HEAD_DIM), BF16),
        scratch_shapes=[pltpu.VMEM((seq, QB), jnp.int32), pltpu.VMEM((seq, QB), F32)]
        + [pltpu.VMEM((1, lanes), F32), pltpu.VMEM((1, lanes), F32),
           pltpu.VMEM((HEAD_DIM, lanes), F32)] * DSA_KV_HEADS
        + [pltpu.VMEM((KW, lanes), F32)] * (2 * DSA_KV_HEADS)
        + [pltpu.VMEM((1, lanes), F32)] * (2 * DSA_KV_HEADS),
        compiler_params=_cparams(("arbitrary", "arbitrary")),
        name="dsa_attention",
    )(iq_t, iw_t, ik, q_t, k, v_t, dt)


def _rms(x, g):
    return x * lax.rsqrt(jnp.mean(x * x, axis=-1, keepdims=True) + EPS) * g


def _rope_tables(seq, dim):
    half = dim // 2
    inv = ROPE_THETA ** (-jnp.arange(half, dtype=F32) / half)
    ang = jnp.arange(seq, dtype=F32)[:, None] * inv[None, :]
    return jnp.cos(ang), jnp.sin(ang)


def _rope(x, cos, sin):
    half = x.shape[-1] // 2
    x1, x2 = x[..., :half], x[..., half:]
    return jnp.concatenate([x1 * cos - x2 * sin, x1 * sin + x2 * cos], axis=-1)


def _pad_cols(w, n):
    return jnp.pad(w, ((0, 0), (0, n - w.shape[1])))


def _t(x):
    return jnp.swapaxes(x, -1, -2)


def _even_mixer(h, x2, gate, dt, dc, bsz, seq, w_in, w_out, nsa_qk_g, cmp_pe, cmp_w1, cmp_b1,
                cmp_w2, cmp_b2, q_norm_g, kv_norm_g, w_uq, w_ukv, nope_g, rope_g):
    m = bsz * seq
    nq_cols = NSA_HEADS * HEAD_DIM
    nkv_cols = 6 * NSA_GROUPS * HEAD_DIM
    ngate = 3 * NSA_HEADS
    o_gate = nq_cols + nkv_cols
    o_cq = o_gate + ngate
    o_ckv = o_cq + MLA_Q_RANK
    o_kpe = o_ckv + MLA_KV_RANK
    tail = jnp.concatenate([w_in[:, o_kpe:], w_in[:, o_gate:o_cq]], axis=1)
    w_r = jnp.concatenate([w_in[:, :o_gate], w_in[:, o_cq:o_kpe], _pad_cols(tail, LANE)], axis=1).astype(BF16)
    proj = proj_slabs(h, w_r, tn=w_r.shape[1] // 3)
    s_kv = NSA_HEADS
    s_cq = s_kv + 6 * NSA_GROUPS
    s_ckv = s_cq + MLA_Q_RANK // LANE
    s_tail = s_ckv + MLA_KV_RANK // LANE

    scale = HEAD_DIM ** -0.5 * LOG2E
    q_t = _t((_rms(proj[:s_kv], nsa_qk_g[0]) * scale).astype(BF16))
    kv = proj[s_kv:s_cq].reshape(6, NSA_GROUPS, m, HEAD_DIM)
    k_slc = _rms(kv[2], nsa_qk_g[1]).astype(BF16)
    vs_t = _t(kv[3].astype(BF16))
    k_win = _rms(kv[4], nsa_qk_g[1]).astype(BF16)
    vw_t = _t(kv[5].astype(BF16))
    kvc = compress_kv(proj, s_kv, bsz, seq, cmp_pe, cmp_w1, cmp_b1, cmp_w2, cmp_b2, nsa_qk_g[1])
    tail_v = proj[s_tail]
    gates = tail_v[:, MLA_ROPE:MLA_ROPE + ngate].reshape(m, NSA_GROUPS, 3 * NSA_HPG)
    gates_t = jnp.pad(jnp.transpose(gates, (1, 2, 0)), ((0, 0), (0, 16 - 3 * NSA_HPG), (0, 0)))
    o_nsa = nsa_attention(q_t, gates_t, kvc[0], _t(kvc[1]), k_slc, vs_t, k_win, vw_t,
                          dt[:NSA_HEADS], dc[:NSA_HEADS], bsz, seq)

    dq = MLA_NOPE + MLA_ROPE
    wq = w_uq.reshape(MLA_Q_RANK, MLA_HEADS, dq)
    wq_r = jnp.concatenate([wq[:, :, :MLA_NOPE].reshape(MLA_Q_RANK, -1),
                            wq[:, :, MLA_NOPE:].reshape(MLA_Q_RANK, -1)], axis=1).astype(BF16)
    qf = normproj_slabs(proj[s_cq:s_ckv], q_norm_g, wq_r, tn=512)
    kvf = normproj_slabs(proj[s_ckv:s_tail], kv_norm_g, w_ukv.astype(BF16), tn=512)
    cos, sin = _rope_tables(seq, MLA_ROPE)
    cos = jnp.tile(cos, (bsz, 1))
    sin = jnp.tile(sin, (bsz, 1))
    mscale = dq ** -0.5 * LOG2E
    q_nope = _rms(qf[:MLA_HEADS], nope_g[0])
    q_pe = qf[MLA_HEADS:].reshape(MLA_HEADS // 2, m, 2, MLA_ROPE)
    q_pe = jnp.transpose(q_pe, (0, 2, 1, 3)).reshape(MLA_HEADS, m, MLA_ROPE)
    q_pe = _rope(_rms(q_pe, rope_g[0]), cos[None], sin[None])
    zpad = jnp.zeros((MLA_HEADS, m, 2 * LANE - dq), F32)
    q_mla_t = _t((jnp.concatenate([q_nope, q_pe, zpad], axis=-1) * mscale).astype(BF16))
    kvf = kvf.reshape(MLA_HEADS, 2, m, LANE)
    k_nope = _rms(kvf[:, 0], nope_g[1])
    k_pe = _rope(_rms(tail_v[:, :MLA_ROPE], rope_g[1]), cos, sin)
    k_mla = jnp.concatenate([k_nope, jnp.broadcast_to(k_pe[None], (MLA_HEADS, m, MLA_ROPE)), zpad],
                            axis=-1).astype(BF16)
    v_mla_t = _t(kvf[:, 1].astype(BF16))
    o_mla = mla_attention(q_mla_t, k_mla, v_mla_t, bsz, seq)
    w_o = w_out.astype(BF16)
    return resproj([(o_nsa, w_o[:nq_cols]), (o_mla, w_o[nq_cols:])], x2, gate, seq)


def _odd_mixer(h, x2, gate, dt, bsz, seq, w_in, w_out, qk_g):
    m = bsz * seq
    nt = m // QB
    nq = DSA_HEADS * HEAD_DIM
    nkv = DSA_KV_HEADS * HEAD_DIM
    niq = IDX_HEADS * IDX_DIM
    ncols = w_in.shape[1]
    npad = -(-ncols // 384) * 384
    proj = proj_slabs(h, _pad_cols(w_in, npad).astype(BF16), tn=npad // 3)
    s_k = nq // LANE
    s_v = s_k + nkv // LANE
    s_iq = s_v + nkv // LANE
    s_tail = s_iq + niq // LANE
    q_t = _t((_rms(proj[:s_k], qk_g[0]) * (HEAD_DIM ** -0.5 * LOG2E)).astype(BF16))
    k = _rms(proj[s_k:s_v], qk_g[1]).astype(BF16)
    v_t = _t(proj[s_v:s_iq].astype(BF16))
    cos, sin = _rope_tables(seq, IDX_ROPE)
    cos = jnp.tile(cos, (bsz, 1))
    sin = jnp.tile(sin, (bsz, 1))
    iq = proj[s_iq:s_tail].reshape(IDX_HEADS // 2, m, 2, IDX_DIM)
    iq = jnp.transpose(iq, (0, 2, 1, 3)).reshape(IDX_HEADS, m, IDX_DIM)
    iq = jnp.concatenate([_rope(iq[..., :IDX_ROPE], cos[None], sin[None]), iq[..., IDX_ROPE:]], axis=-1)
    iq = (iq * IDX_DIM ** -0.5).astype(BF16).reshape(IDX_HEADS, nt, QB, IDX_DIM)
    iq_t = jnp.transpose(iq, (1, 3, 0, 2)).reshape(nt, IDX_DIM, IDX_HEADS * QB)
    iq_t = jnp.pad(iq_t, ((0, 0), (0, LANE - IDX_DIM), (0, 0)))
    tail = proj[s_tail]
    ik = tail[:, :IDX_DIM]
    ik = jnp.concatenate([_rope(ik[:, :IDX_ROPE], cos, sin), ik[:, IDX_ROPE:]], axis=-1)
    ik = jnp.pad(ik, ((0, 0), (0, LANE - IDX_DIM))).astype(BF16)
    iw = (tail[:, IDX_DIM:IDX_DIM + IDX_HEADS] * IDX_HEADS ** -0.5).reshape(nt, QB, IDX_HEADS)
    iw_t = jnp.transpose(iw, (0, 2, 1)).reshape(nt, 1, IDX_HEADS * QB)
    o = dsa_attention(iq_t, iw_t, ik, q_t, k, v_t, dt, bsz, seq)
    return resproj([(o, w_out.astype(BF16))], x2, gate, seq)


def _conv_ffn(h, x2, gate, seq, w_up, conv_w, conv_b, w_down):
    a = ffn_up(h, w_up.astype(BF16), conv_w, conv_b, seq)
    return resproj([(a, w_down.astype(BF16))], x2, gate, seq)


def kernel(x, c, rel_bias, ada_w, ada_b, norm_g, ev_w_in, ev_w_out, nsa_qk_g, cmp_pe, cmp_w1, cmp_b1, cmp_w2, cmp_b2, mla_q_norm_g, mla_kv_norm_g, mla_w_uq, mla_w_ukv, mla_nope_g, mla_rope_g, od_w_in, od_w_out, dsa_qk_g, ffn_w_up, ffn_conv_w, ffn_conv_b, ffn_w_down):
    bsz, seq, d = x.shape
    depth = ada_w.shape[0]
    x2 = x.reshape(bsz * seq, d)
    mods = ada_all(c, ada_w, ada_b)
    dt, dc = bias_tiles(rel_bias)
    for i in range(depth):
        j = i // 2
        shift, scale, gate = jnp.split(mods[i, 0], 3, axis=-1)
        h = modnorm(x2, norm_g[i, 0], scale, shift, seq)
        if i % 2 == 0:
            x2 = _even_mixer(h, x2, gate, dt, dc, bsz, seq, ev_w_in[j], ev_w_out[j], nsa_qk_g[j],
                             cmp_pe[j], cmp_w1[j], cmp_b1[j], cmp_w2[j], cmp_b2[j], mla_q_norm_g[j],
                             mla_kv_norm_g[j], mla_w_uq[j], mla_w_ukv[j], mla_nope_g[j], mla_rope_g[j])
        else:
            x2 = _odd_mixer(h, x2, gate, dt, bsz, seq, od_w_in[j], od_w_out[j], dsa_qk_g[j])
        shift, scale, gate = jnp.split(mods[i, 1], 3, axis=-1)
        h = modnorm(x2, norm_g[i, 1], scale, shift, seq)
        x2 = _conv_ffn(h, x2, gate, seq, ffn_w_up[i], ffn_conv_w[i], ffn_conv_b[i], ffn_w_down[i])
    return x2.reshape(bsz, seq, d)
```

```python
import functools
import math

import numpy as np
import jax
import jax.numpy as jnp
from jax import lax
from jax.experimental import pallas as pl
from jax.experimental.pallas import tpu as pltpu

HEAD_DIM = 128
NSA_HEADS = 8
NSA_GROUPS = 2
NSA_HPG = NSA_HEADS // NSA_GROUPS
CMP_BLOCK = 32
CMP_STRIDE = 16
CMP_HIDDEN = 256
SEL_BLOCK = 64
SEL_TOP_N = 16
WINDOW = 512
MLA_HEADS = 8
MLA_Q_RANK = 512
MLA_KV_RANK = 256
MLA_NOPE = 128
MLA_ROPE = 64
MLA_V = 128
DSA_HEADS = 16
DSA_KV_HEADS = 4
DSA_HPG = DSA_HEADS // DSA_KV_HEADS
IDX_HEADS = 16
IDX_DIM = 64
IDX_ROPE = 32
DSA_TOPK_MAX = 256
REL_BUCKETS = 32
REL_MAX_DIST = 128
CONV_WIDTH = 3
ROPE_THETA = 10000.0
EPS = 1e-6
NEG = -1e30
FORCE = 1e9

LANE = 128
QB = 128
VMEM_LIMIT = 56 * 1024 * 1024

F32 = jnp.float32
BF16 = jnp.bfloat16


def _t5_thresholds():
    d = np.arange(0, 4 * REL_MAX_DIST)
    half = REL_BUCKETS // 2
    val = np.log(np.maximum(d, 1) / half) / math.log(REL_MAX_DIST / half) * (REL_BUCKETS - half)
    large = np.minimum(half + np.floor(np.maximum(val, 0.0)).astype(np.int64), REL_BUCKETS - 1)
    bucket = np.where(d < half, d, large)
    return [int(np.argmax(bucket >= b)) for b in range(1, REL_BUCKETS)]


T5_THR = _t5_thresholds()
T5_FAR = T5_THR[-1]
assert T5_FAR <= LANE


def _cparams(sem):
    return pltpu.CompilerParams(dimension_semantics=sem, vmem_limit_bytes=VMEM_LIMIT)


def _dot(a, b):
    return jnp.dot(a, b, preferred_element_type=F32)


def _ada_kernel(c_ref, w_ref, b_ref, o_ref):
    c = c_ref[...]
    a = c * jax.nn.sigmoid(c)
    o_ref[0] = jnp.dot(a, w_ref[0], preferred_element_type=F32,
                       precision=lax.Precision.HIGHEST) + b_ref[0]


def ada_all(c, ada_w, ada_b):
    depth, two, d, n3 = ada_w.shape
    bsz = c.shape[0]
    rows = 8
    cp = jnp.zeros((rows, d), F32).at[:bsz].set(c)
    w = ada_w.reshape(depth * two, d, n3)
    b = ada_b.reshape(depth * two, 1, n3)
    tn = 512
    out = pl.pallas_call(
        _ada_kernel,
        grid=(depth * two, n3 // tn),
        in_specs=[pl.BlockSpec((rows, d), lambda l, j: (0, 0)),
                  pl.BlockSpec((1, d, tn), lambda l, j: (l, 0, j)),
                  pl.BlockSpec((1, 1, tn), lambda l, j: (l, 0, j))],
        out_specs=pl.BlockSpec((1, rows, tn), lambda l, j: (l, 0, j)),
        out_shape=jax.ShapeDtypeStruct((depth * two, rows, n3), F32),
        compiler_params=_cparams(("arbitrary", "arbitrary")),
        name="ada_mod",
    )(cp, w, b)
    return out[:, :bsz].reshape(depth, two, bsz, n3)


def _modnorm_kernel(x_ref, g_ref, sc_ref, sh_ref, o_ref):
    x = x_ref[...]
    y = x * lax.rsqrt(jnp.mean(x * x, axis=-1, keepdims=True) + EPS)
    h = (y * g_ref[...]) * (1.0 + sc_ref[0]) + sh_ref[0]
    o_ref[...] = h.astype(o_ref.dtype)


def modnorm(x2, g, scale, shift, seq):
    m, d = x2.shape
    tm = 512
    tpb = seq // tm
    return pl.pallas_call(
        _modnorm_kernel,
        grid=(m // tm,),
        in_specs=[pl.BlockSpec((tm, d), lambda i: (i, 0)),
                  pl.BlockSpec((1, d), lambda i: (0, 0)),
                  pl.BlockSpec((1, 1, d), lambda i: (i // tpb, 0, 0)),
                  pl.BlockSpec((1, 1, d), lambda i: (i // tpb, 0, 0))],
        out_specs=pl.BlockSpec((tm, d), lambda i: (i, 0)),
        out_shape=jax.ShapeDtypeStruct((m, d), BF16),
        compiler_params=_cparams(("arbitrary",)),
        name="modnorm",
    )(x2, g.reshape(1, d), scale.reshape(-1, 1, d), shift.reshape(-1, 1, d))


def _proj_kernel(x_ref, w_ref, o_ref, *, nslab):
    acc = _dot(x_ref[...], w_ref[...])
    for s in range(nslab):
        o_ref[s] = acc[:, s * LANE:(s + 1) * LANE]


def proj_slabs(x, w, tm=1024, tn=384):
    m, k = x.shape
    n = w.shape[1]
    assert n % tn == 0 and m % tm == 0
    nslab = tn // LANE
    return pl.pallas_call(
        functools.partial(_proj_kernel, nslab=nslab),
        grid=(m // tm, n // tn),
        in_specs=[pl.BlockSpec((tm, k), lambda i, j: (i, 0)),
                  pl.BlockSpec((k, tn), lambda i, j: (0, j))],
        out_specs=pl.BlockSpec((nslab, tm, LANE), lambda i, j: (j, i, 0)),
        out_shape=jax.ShapeDtypeStruct((n // LANE, m, LANE), F32),
        compiler_params=_cparams(("arbitrary", "arbitrary")),
        name="proj_slabs",
    )(x, w)


def _proj_heads_kernel(x_ref, w_ref, g_ref, o_ref, *, nslab, norm, transpose):
    acc = _dot(x_ref[...], w_ref[...])
    for s in range(nslab):
        y = acc[:, s * LANE:(s + 1) * LANE]
        if norm:
            y = y * lax.rsqrt(jnp.mean(y * y, axis=-1, keepdims=True) + EPS) * g_ref[...]
        o_ref[s] = (y.T if transpose else y).astype(o_ref.dtype)


def proj_heads(x, w, g=None, *, transpose, tm=1024, tn=1024):
    m, k = x.shape
    n = w.shape[1]
    tn = min(tn, n)
    assert n % tn == 0 and m % tm == 0
    nslab = tn // LANE
    norm = g is not None
    if transpose:
        out_spec = pl.BlockSpec((nslab, LANE, tm), lambda i, j: (j, 0, i))
        out_shape = jax.ShapeDtypeStruct((n // LANE, LANE, m), BF16)
    else:
        out_spec = pl.BlockSpec((nslab, tm, LANE), lambda i, j: (j, i, 0))
        out_shape = jax.ShapeDtypeStruct((n // LANE, m, LANE), BF16)
    g2 = (g if norm else jnp.ones((LANE,), F32)).reshape(1, LANE)
    return pl.pallas_call(
        functools.partial(_proj_heads_kernel, nslab=nslab, norm=norm, transpose=transpose),
        grid=(m // tm, n // tn),
        in_specs=[pl.BlockSpec((tm, k), lambda i, j: (i, 0)),
                  pl.BlockSpec((k, tn), lambda i, j: (0, j)),
                  pl.BlockSpec((1, LANE), lambda i, j: (0, 0))],
        out_specs=out_spec,
        out_shape=out_shape,
        compiler_params=_cparams(("arbitrary", "arbitrary")),
        name="proj_heads",
    )(x, w, g2)


def _normproj_kernel(x_ref, g_ref, w_ref, o_ref, *, kslab, nslab):
    x = jnp.concatenate([x_ref[s] for s in range(kslab)], axis=1)
    y = x * lax.rsqrt(jnp.mean(x * x, axis=-1, keepdims=True) + EPS) * g_ref[...]
    acc = _dot(y.astype(BF16), w_ref[...])
    for s in range(nslab):
        o_ref[s] = acc[:, s * LANE:(s + 1) * LANE]


def normproj_slabs(x_slabs, g, w, tm=512, tn=512):
    kslab, m, _ = x_slabs.shape
    k = kslab * LANE
    n = w.shape[1]
    assert n % tn == 0
    nslab = tn // LANE
    return pl.pallas_call(
        functools.partial(_normproj_kernel, kslab=kslab, nslab=nslab),
        grid=(m // tm, n // tn),
        in_specs=[pl.BlockSpec((kslab, tm, LANE), lambda i, j: (0, i, 0)),
                  pl.BlockSpec((1, k), lambda i, j: (0, 0)),
                  pl.BlockSpec((k, tn), lambda i, j: (0, j))],
        out_specs=pl.BlockSpec((nslab, tm, LANE), lambda i, j: (j, i, 0)),
        out_shape=jax.ShapeDtypeStruct((n // LANE, m, LANE), F32),
        compiler_params=_cparams(("arbitrary", "arbitrary")),
        name="normproj_slabs",
    )(x_slabs, g.reshape(1, k), w)


def _resproj_kernel(*refs, npair):
    xres_ref, gate_ref = refs[2 * npair], refs[2 * npair + 1]
    o_ref = refs[2 * npair + 2]
    acc = _dot(refs[0][...], refs[1][...])
    for p in range(1, npair):
        acc = acc + _dot(refs[2 * p][...], refs[2 * p + 1][...])
    o_ref[...] = xres_ref[...] + gate_ref[0] * acc


def resproj(pairs, xres, gate, seq, tm=1024, tn=512):
    m, n = xres.shape
    tpb = seq // tm
    in_specs, args = [], []
    for x, w in pairs:
        k = x.shape[1]
        in_specs += [pl.BlockSpec((tm, k), lambda i, j: (i, 0)),
                     pl.BlockSpec((k, tn), lambda i, j: (0, j))]
        args += [x, w]
    in_specs += [pl.BlockSpec((tm, tn), lambda i, j: (i, j)),
                 pl.BlockSpec((1, 1, tn), lambda i, j: (i // tpb, 0, j))]
    args += [xres, gate.reshape(-1, 1, n)]
    return pl.pallas_call(
        functools.partial(_resproj_kernel, npair=len(pairs)),
        grid=(m // tm, n // tn),
        in_specs=in_specs,
        out_specs=pl.BlockSpec((tm, tn), lambda i, j: (i, j)),
        out_shape=jax.ShapeDtypeStruct((m, n), F32),
        compiler_params=_cparams(("arbitrary", "arbitrary")),
        name="resproj",
    )(*args)


HALO = 8


def _ffn_up_kernel(h_ref, wg_ref, wv_ref, cwg_ref, cwv_ref, cbg_ref, cbv_ref, o_ref,
                   ug_ref, uv_ref, *, tm, tiles_per_seq):
    i = pl.program_id(1)
    first = (i % tiles_per_seq) == 0

    @pl.when(first)
    def _():
        ug_ref[0:HALO, :] = jnp.zeros((HALO, ug_ref.shape[1]), F32)
        uv_ref[0:HALO, :] = jnp.zeros((HALO, uv_ref.shape[1]), F32)

    @pl.when(jnp.logical_not(first))
    def _():
        ug_ref[0:HALO, :] = ug_ref[tm:tm + HALO, :]
        uv_ref[0:HALO, :] = uv_ref[tm:tm + HALO, :]

    h = h_ref[...]
    ug_ref[HALO:HALO + tm, :] = _dot(h, wg_ref[...])
    uv_ref[HALO:HALO + tm, :] = _dot(h, wv_ref[...])

    def conv(u_ref, cw_ref, cb_ref):
        out = cb_ref[...]
        for j in range(CONV_WIDTH):
            off = HALO - (CONV_WIDTH - 1) + j
            out = out + cw_ref[j:j + 1, :] * u_ref[off:off + tm, :]
        return out

    g = conv(ug_ref, cwg_ref, cbg_ref)
    v = conv(uv_ref, cwv_ref, cbv_ref)
    o_ref[...] = (g * jax.nn.sigmoid(g) * v).astype(o_ref.dtype)


def ffn_up(h, w_up, conv_w, conv_b, seq, tm=1024, tn=512):
    m, d = h.shape
    f = w_up.shape[1] // 2
    nj = f // tn
    tps = seq // tm
    cb = conv_b.reshape(1, 2 * f)
    return pl.pallas_call(
        functools.partial(_ffn_up_kernel, tm=tm, tiles_per_seq=tps),
        grid=(nj, m // tm),
        in_specs=[pl.BlockSpec((tm, d), lambda j, i: (i, 0)),
                  pl.BlockSpec((d, tn), lambda j, i: (0, j)),
                  pl.BlockSpec((d, tn), lambda j, i: (0, nj + j)),
                  pl.BlockSpec((CONV_WIDTH, tn), lambda j, i: (0, j)),
                  pl.BlockSpec((CONV_WIDTH, tn), lambda j, i: (0, nj + j)),
                  pl.BlockSpec((1, tn), lambda j, i: (0, j)),
                  pl.BlockSpec((1, tn), lambda j, i: (0, nj + j))],
        out_specs=pl.BlockSpec((tm, tn), lambda j, i: (i, j)),
        out_shape=jax.ShapeDtypeStruct((m, f), BF16),
        scratch_shapes=[pltpu.VMEM((tm + HALO, tn), F32), pltpu.VMEM((tm + HALO, tn), F32)],
        compiler_params=_cparams(("arbitrary", "arbitrary")),
        name="ffn_up_conv",
    )(h, w_up, w_up, conv_w, conv_w, cb, cb)


LOG2E = 1.4426950408889634
CWIN = 16


def _t5_shifted(dist, tbl_ref, h):
    val = jnp.full(dist.shape, tbl_ref[0, h], F32)
    for b in range(1, REL_BUCKETS):
        val = jnp.where(dist >= T5_THR[b - 1], tbl_ref[b, h], val)
    return (val - tbl_ref[REL_BUCKETS - 1, h]) * LOG2E


def _bias_tiles_kernel(tbl_ref, dt_ref, dc_ref):
    h = pl.program_id(0)
    key = lax.broadcasted_iota(jnp.int32, (LANE, LANE), 0)
    q = lax.broadcasted_iota(jnp.int32, (LANE, LANE), 1)
    for rel in range(2):
        dt_ref[0, rel] = _t5_shifted(rel * LANE + q - key, tbl_ref, h)
    dt_ref[0, 2] = jnp.zeros((LANE, LANE), F32)
    u = lax.broadcasted_iota(jnp.int32, (CWIN, LANE), 0)
    qc = lax.broadcasted_iota(jnp.int32, (CWIN, LANE), 1)
    dc_ref[0] = _t5_shifted(qc - CMP_STRIDE * (u - CWIN // 2) - (CMP_BLOCK - 1), tbl_ref, h)


def bias_tiles(rel_bias):
    nh = rel_bias.shape[1]
    return pl.pallas_call(
        _bias_tiles_kernel,
        grid=(nh,),
        in_specs=[pl.BlockSpec(memory_space=pltpu.SMEM)],
        out_specs=[pl.BlockSpec((1, 3, LANE, LANE), lambda h: (h, 0, 0, 0)),
                   pl.BlockSpec((1, CWIN, LANE), lambda h: (h, 0, 0))],
        out_shape=[jax.ShapeDtypeStruct((nh, 3, LANE, LANE), F32),
                   jax.ShapeDtypeStruct((nh, CWIN, LANE), F32)],
        compiler_params=_cparams(("arbitrary",)),
        name="t5_bias_tiles",
    )(rel_bias)


def _compress_kernel(x_ref, pe_ref, w1_ref, b1_ref, w2_ref, b2_ref, g_ref, o_ref, *, half):
    kv = pl.program_id(0)
    x = x_ref[0]
    a = _dot((x + pe_ref[0, :, :half]).astype(BF16), w1_ref[0, :half, :])
    b = _dot((x + pe_ref[0, :, half:]).astype(BF16), w1_ref[0, half:, :])
    b_next = jnp.concatenate([b[1:], jnp.zeros((1, b.shape[1]), F32)], axis=0)
    hid = jax.nn.gelu(a + b_next + b1_ref[0])
    out = _dot(hid.astype(BF16), w2_ref[0]) + b2_ref[0]
    normed = out * lax.rsqrt(jnp.mean(out * out, axis=-1, keepdims=True) + EPS) * g_ref[...]
    out = jnp.where(kv == 0, normed, out)
    o_ref[0, 0] = out.astype(o_ref.dtype)


def compress_kv(proj, slab0, bsz, seq, cmp_pe, cmp_w1, cmp_b1, cmp_w2, cmp_b2, g_k):
    nslab, m, _ = proj.shape
    nchunk = seq // CMP_STRIDE
    half = CMP_STRIDE * HEAD_DIM
    xv = proj.reshape(nslab, m // CMP_STRIDE, half)
    pe = cmp_pe.reshape(2, 1, CMP_BLOCK * HEAD_DIM)
    return pl.pallas_call(
        functools.partial(_compress_kernel, half=half),
        grid=(2, bsz, NSA_GROUPS),
        in_specs=[pl.BlockSpec((1, nchunk, half), lambda kv, b, g: (slab0 + 2 * kv + g, b, 0)),
                  pl.BlockSpec((1, 1, 2 * half), lambda kv, b, g: (kv, 0, 0)),
                  pl.BlockSpec((1, 2 * half, CMP_HIDDEN), lambda kv, b, g: (kv, 0, 0)),
                  pl.BlockSpec((1, 1, CMP_HIDDEN), lambda kv, b, g: (kv, 0, 0)),
                  pl.BlockSpec((1, CMP_HIDDEN, HEAD_DIM), lambda kv, b, g: (kv, 0, 0)),
                  pl.BlockSpec((1, 1, HEAD_DIM), lambda kv, b, g: (kv, 0, 0)),
                  pl.BlockSpec((1, HEAD_DIM), lambda kv, b, g: (0, 0))],
        out_specs=pl.BlockSpec((1, 1, nchunk, HEAD_DIM), lambda kv, b, g: (kv, g, b, 0)),
        out_shape=jax.ShapeDtypeStruct((2, NSA_GROUPS, bsz * nchunk, HEAD_DIM), BF16),
        compiler_params=_cparams(("arbitrary", "arbitrary", "arbitrary")),
        name="nsa_compress",
    )(xv, pe, cmp_w1.astype(BF16), cmp_b1.reshape(2, 1, CMP_HIDDEN), cmp_w2.astype(BF16),
      cmp_b2.reshape(2, 1, HEAD_DIM), g_k.reshape(1, HEAD_DIM))


KW = 512
PV_KEYS = 256


def _tile_lanes(x, n):
    return jnp.concatenate([x] * n, axis=1)


def _flash_init(m_ref, l_ref, acc_ref):
    m_ref[...] = jnp.full(m_ref.shape, NEG, F32)
    l_ref[...] = jnp.zeros(l_ref.shape, F32)
    acc_ref[...] = jnp.zeros(acc_ref.shape, F32)


def _zero_after(x):
    bits = pltpu.bitcast(x, jnp.int32)
    return lax.shift_right_logical(lax.shift_right_logical(bits, 16), 16).astype(F32)


def _flash_update(s, v_t, m_ref, l_ref, acc_ref, col_max=None, after=None):
    m_old = m_ref[...]
    if col_max is None:
        col_max = jnp.max(s, axis=0, keepdims=True)
    m_new = jnp.maximum(m_old, col_max)
    alpha = jnp.exp2(m_old - m_new)
    l_new = alpha * l_ref[...]
    acc = alpha * acc_ref[...]
    nk = s.shape[0]
    for k0 in range(0, nk, PV_KEYS):
        p = jnp.exp2(s[k0:k0 + PV_KEYS] - m_new)
        l_new = l_new + jnp.sum(p, axis=0, keepdims=True)
        acc = acc + _dot(v_t[:, k0:k0 + PV_KEYS], p.astype(BF16))
    l_ref[...] = l_new
    acc_ref[...] = acc
    m_ref[...] = m_new if after is None else m_new + _zero_after(after)


def _inv_den(m, den):
    ok = m > 0.5 * NEG
    return jnp.where(ok, 1.0 / jnp.where(ok, den, 1.0), 0.0)


def _flash_result(m_ref, l_ref, acc_ref):
    return acc_ref[...] * _inv_den(m_ref[...], l_ref[...])


def _softmax_cols(s):
    m = jnp.max(s, axis=0, keepdims=True)
    p = jnp.exp2(s - m)
    return p * _inv_den(m, jnp.sum(p, axis=0, keepdims=True))


def _near_bias(dt_ref, heads, qi, kt0, ntile):
    rows = []
    for j in range(ntile):
        rel = jnp.clip(qi - (kt0 + j), 0, 2)
        rows.append(jnp.concatenate([dt_ref[h, rel] for h in heads], axis=1))
    return jnp.concatenate(rows, axis=0)


def _pipelined_chunks(n, qk_stage, soft_stage):
    @pl.when(n > 0)
    def _():
        qk_stage(0, 0)

    def pair(p, x):
        c = 2 * p
        ahead = qk_stage(c + 1, 1)
        soft_stage(c, 0, ahead)
        ahead = qk_stage(jnp.minimum(c + 2, n - 1), 0)
        soft_stage(c + 1, 1, ahead)
        return x

    lax.fori_loop(0, n // 2, pair, 0)

    @pl.when(n % 2 == 1)
    def _():
        soft_stage(n - 1, 0, None)


def _nsa_kernel(qt_ref, gt_ref, kc_ref, vct_ref, ks_ref, vst_ref, kw_ref, vwt_ref,
                dt_ref, dc_ref, ext_ref, o_ref,
                sc_ref, ps_ref, m_ref, l_ref, acc_ref, sbuf0, sbuf1, cbuf0, cbuf1, *, seq, nc):
    sbuf, cbuf = (sbuf0, sbuf1), (cbuf0, cbuf1)
    qi = pl.program_id(2)
    q0 = qi * QB
    hpg = NSA_HPG
    heads = list(range(hpg))
    ncp = kc_ref.shape[1]
    ns = seq // SEL_BLOCK
    q_t = jnp.concatenate([qt_ref[h] for h in heads], axis=1)

    pad = CWIN // 2
    sc_ref[0:pad, :] = jnp.zeros((pad, hpg * QB), F32)
    sc_ref[pad + ncp:2 * pad + ncp, :] = jnp.zeros((pad, hpg * QB), F32)
    sc_ref[pad:pad + ncp, :] = _dot(kc_ref[0], q_t)
    r0 = pl.multiple_of(qi * (QB // CMP_STRIDE), 8)
    sc_ref[pl.ds(r0, CWIN), :] = sc_ref[pl.ds(r0, CWIN), :] + jnp.concatenate(
        [dc_ref[h] for h in heads], axis=1)
    ci = lax.broadcasted_iota(jnp.int32, (ncp, QB), 0)
    tc = q0 + lax.broadcasted_iota(jnp.int32, (ncp, QB), 1)
    valid_c = (ci * CMP_STRIDE + CMP_BLOCK - 1 <= tc) & (ci < nc)
    p_c = _softmax_cols(sc_ref[pad:pad + ncp, :] + _tile_lanes(jnp.where(valid_c, 0.0, NEG), hpg))
    oc_t = _dot(vct_ref[0], p_c.astype(BF16))
    p_sum = p_c[:, 0:QB]
    for h in range(1, hpg):
        p_sum = p_sum + p_c[:, h * QB:(h + 1) * QB]

    wkeys = WINDOW + QB
    start = pl.multiple_of(jnp.maximum(q0 - WINDOW, 0), LANE)
    s_w = _dot(kw_ref[0, pl.ds(start, wkeys), :], q_t)
    s_w = s_w + _near_bias(dt_ref, heads, qi, start // LANE, wkeys // LANE)
    dist_w = (q0 + lax.broadcasted_iota(jnp.int32, (wkeys, QB), 1)) - (
        start + lax.broadcasted_iota(jnp.int32, (wkeys, QB), 0))
    mask_w = (dist_w >= 0) & (dist_w < WINDOW)
    p_w = _softmax_cols(s_w + _tile_lanes(jnp.where(mask_w, 0.0, NEG), hpg))
    ow_t = _dot(vwt_ref[0, :, pl.ds(start, wkeys)], p_w.astype(BF16))

    ps_ref[0:8, :] = jnp.zeros((8, QB), F32)
    ps_ref[8:8 + ncp, :] = p_sum
    per = SEL_BLOCK // CMP_STRIDE
    band = [ps_ref[pl.ds(8 + r, ns, stride=per), :] for r in range(-1, per)]
    imp = 0.5 * band[0] + band[1] + band[2] + band[3] + 0.5 * band[4]
    if ns < LANE:
        imp = jnp.concatenate([imp, jnp.zeros((LANE - ns, QB), F32)], axis=0)
    blk = lax.broadcasted_iota(jnp.int32, (LANE, QB), 0)
    t = q0 + lax.broadcasted_iota(jnp.int32, (LANE, QB), 1)
    tb = t // SEL_BLOCK
    forced = (blk == 0) | (blk == tb) | (blk == tb - 1)
    score = jnp.where(forced, FORCE, jnp.where(blk * SEL_BLOCK <= t, imp, NEG))
    score = jnp.where(blk < ns, score, -jnp.inf)
    blk_f = blk.astype(F32)
    sel = jnp.zeros((LANE, QB), F32)
    for _ in range(min(SEL_TOP_N, ns)):
        mx = jnp.max(score, axis=0, keepdims=True)
        first = jnp.min(jnp.where(score == mx, blk_f, float(LANE)), axis=0, keepdims=True)
        pick = blk_f == first
        sel = jnp.where(pick, 1.0, sel)
        score = jnp.where(pick, -jnp.inf, score)
    sel_b = sel.astype(BF16)

    _flash_init(m_ref, l_ref, acc_ref)
    kpos = lax.broadcasted_iota(jnp.int32, (KW, QB), 0)
    tq = q0 + lax.broadcasted_iota(jnp.int32, (KW, QB), 1)

    def block_mask(c0):
        chosen = _dot(ext_ref[pl.ds(c0, KW), :], sel_b)
        return (chosen - 1.0) * (-NEG)

    def qk_stage(c, buf):
        c0 = pl.multiple_of(c * KW, KW)
        s = _dot(ks_ref[0, pl.ds(c0, KW), :], q_t) + _tile_lanes(block_mask(c0), hpg)
        sbuf[buf][...] = s
        col_max = jnp.max(s, axis=0, keepdims=True)
        cbuf[buf][...] = col_max
        return col_max

    def soft_stage(c, buf, ahead):
        c0 = pl.multiple_of(c * KW, KW)
        _flash_update(sbuf[buf][...], vst_ref[0, :, pl.ds(c0, KW)], m_ref, l_ref, acc_ref,
                      col_max=cbuf[buf][...], after=ahead)

    def near_step(c, x):
        c0 = pl.multiple_of(c * KW, KW)
        madd = block_mask(c0) + jnp.where(c0 + kpos <= tq, 0.0, NEG)
        s = _dot(ks_ref[0, pl.ds(c0, KW), :], q_t) + _near_bias(dt_ref, heads, qi, c * (KW // LANE), KW // LANE)
        _flash_update(s + _tile_lanes(madd, hpg), vst_ref[0, :, pl.ds(c0, KW)], m_ref, l_ref, acc_ref)
        return x

    c_near = jnp.maximum(qi - 1, 0) // (KW // LANE)
    _pipelined_chunks(c_near, qk_stage, soft_stage)
    lax.fori_loop(c_near, qi // (KW // LANE) + 1, near_step, 0)
    os_t = _flash_result(m_ref, l_ref, acc_ref)

    gates = jax.nn.sigmoid(gt_ref[0])
    for h in heads:
        sl = slice(h * QB, (h + 1) * QB)
        o_t = (gates[3 * h:3 * h + 1] * oc_t[:, sl] + gates[3 * h + 1:3 * h + 2] * os_t[:, sl]
               + gates[3 * h + 2:3 * h + 3] * ow_t[:, sl])
        o_ref[:, h * HEAD_DIM:(h + 1) * HEAD_DIM] = o_t.T.astype(o_ref.dtype)


def nsa_attention(q_t, gates_t, kc, vc_t, k_sw, v_sw_t, dt, dc, bsz, seq):
    nq = seq // QB
    ncp = seq // CMP_STRIDE
    nc = ncp - 1
    ns = seq // SEL_BLOCK
    assert ns <= LANE and seq >= WINDOW + QB and seq % KW == 0
    assert CMP_BLOCK == 2 * CMP_STRIDE and SEL_BLOCK == 4 * CMP_STRIDE
    expand =((np.arange(seq)[:, None] // SEL_BLOCK) == np.arange(LANE)[None, :]).astype(np.float32)
    ng = NSA_GROUPS
    ks_spec = pl.BlockSpec((1, seq, HEAD_DIM), lambda b, g, i: (g, b, 0))
    kw_spec = pl.BlockSpec((1, seq, HEAD_DIM), lambda b, g, i: (ng + g, b, 0))
    vs_spec = pl.BlockSpec((1, HEAD_DIM, seq), lambda b, g, i: (g, 0, b))
    vw_spec = pl.BlockSpec((1, HEAD_DIM, seq), lambda b, g, i: (ng + g, 0, b))
    lanes = NSA_HPG * QB
    return pl.pallas_call(
        functools.partial(_nsa_kernel, seq=seq, nc=nc),
        grid=(bsz, NSA_GROUPS, nq),
        in_specs=[pl.BlockSpec((NSA_HPG, HEAD_DIM, QB), lambda b, g, i: (g, 0, b * nq + i)),
                  pl.BlockSpec((1, 16, QB), lambda b, g, i: (g, 0, b * nq + i)),
                  pl.BlockSpec((1, ncp, HEAD_DIM), lambda b, g, i: (g, b, 0)),
                  pl.BlockSpec((1, HEAD_DIM, ncp), lambda b, g, i: (g, 0, b)),
                  ks_spec, vs_spec, kw_spec, vw_spec,
                  pl.BlockSpec((NSA_HPG, 3, LANE, LANE), lambda b, g, i: (g, 0, 0, 0)),
                  pl.BlockSpec((NSA_HPG, CWIN, LANE), lambda b, g, i: (g, 0, 0)),
                  pl.BlockSpec((seq, LANE), lambda b, g, i: (0, 0))],
        out_specs=pl.BlockSpec((QB, NSA_HPG * HEAD_DIM), lambda b, g, i: (b * nq + i, g)),
        out_shape=jax.ShapeDtypeStruct((bsz * seq, NSA_HEADS * HEAD_DIM), BF16),
        scratch_shapes=[pltpu.VMEM((ncp + CWIN, lanes), F32), pltpu.VMEM((ncp + 8, QB), F32),
                        pltpu.VMEM((1, lanes), F32), pltpu.VMEM((1, lanes), F32),
                        pltpu.VMEM((HEAD_DIM, lanes), F32),
                        pltpu.VMEM((KW, lanes), F32), pltpu.VMEM((KW, lanes), F32),
                        pltpu.VMEM((1, lanes), F32), pltpu.VMEM((1, lanes), F32)],
        compiler_params=_cparams(("arbitrary", "arbitrary", "arbitrary")),
        name="nsa_attention",
    )(q_t, gates_t, kc, vc_t, k_sw, v_sw_t, k_sw, v_sw_t, dt, dc, jnp.asarray(expand, BF16))


MLA_HPS = 2


def _mla_kernel(qt_ref, k_ref, vt_ref, o_ref, *scratch):
    qi = pl.program_id(2)
    chains = [scratch[3 * h:3 * h + 3] for h in range(MLA_HPS)]
    sbuf = [scratch[(3 + b) * MLA_HPS:(4 + b) * MLA_HPS] for b in range(2)]
    cbuf = [scratch[(5 + b) * MLA_HPS:(6 + b) * MLA_HPS] for b in range(2)]
    for ch in chains:
        _flash_init(*ch)

    def qk_stage(c, buf):
        c0 = pl.multiple_of(c * KW, KW)
        col_max = []
        for h in range(MLA_HPS):
            s = _dot(k_ref[h, pl.ds(c0, KW), :], qt_ref[h])
            sbuf[buf][h][...] = s
            col_max.append(jnp.max(s, axis=0, keepdims=True))
            cbuf[buf][h][...] = col_max[-1]
        return col_max

    def soft_stage(c, buf, ahead):
        c0 = pl.multiple_of(c * KW, KW)
        for h, ch in enumerate(chains):
            _flash_update(sbuf[buf][h][...], vt_ref[h, :, pl.ds(c0, KW)], *ch, col_max=cbuf[buf][h][...],
                          after=None if ahead is None else ahead[h])

    _pipelined_chunks(qi, qk_stage, soft_stage)
    c0 = pl.multiple_of(qi * KW, KW)
    kpos = lax.broadcasted_iota(jnp.int32, (KW, KW), 0)
    tq = lax.broadcasted_iota(jnp.int32, (KW, KW), 1)
    causal = jnp.where(kpos <= tq, 0.0, NEG)
    dv = vt_ref.shape[1]
    scores = [_dot(k_ref[h, pl.ds(c0, KW), :], qt_ref[h]) + causal for h in range(MLA_HPS)]
    for h, ch in enumerate(chains):
        _flash_update(scores[h], vt_ref[h, :, pl.ds(c0, KW)], *ch)
        o_ref[:, h * dv:(h + 1) * dv] = _flash_result(*ch).T.astype(o_ref.dtype)


def mla_attention(q_t, k, v_t, bsz, seq):
    nh, dqk, _ = q_t.shape
    dv = v_t.shape[1]
    nq = seq // KW
    hps = MLA_HPS
    state = [pltpu.VMEM((1, KW), F32), pltpu.VMEM((1, KW), F32), pltpu.VMEM((dv, KW), F32)] * hps
    state += [pltpu.VMEM((KW, KW), F32)] * (2 * hps)
    state += [pltpu.VMEM((1, KW), F32)] * (2 * hps)
    return pl.pallas_call(
        _mla_kernel,
        grid=(bsz, nh // hps, nq),
        in_specs=[pl.BlockSpec((hps, dqk, KW), lambda b, h, i: (h, 0, b * nq + i)),
                  pl.BlockSpec((hps, seq, dqk), lambda b, h, i: (h, b, 0)),
                  pl.BlockSpec((hps, dv, seq), lambda b, h, i: (h, 0, b))],
        out_specs=pl.BlockSpec((KW, hps * dv), lambda b, h, i: (b * nq + i, h)),
        out_shape=jax.ShapeDtypeStruct((bsz * seq, nh * dv), BF16),
        scratch_shapes=state,
        compiler_params=_cparams(("arbitrary", "arbitrary", "arbitrary")),
        name="mla_attention",
    )(q_t, k, v_t)


INT_MIN = -2 ** 31
NEG_KEY = int(np.array(NEG, np.float32).view(np.int32)) ^ 0x7FFFFFFF


def _sort_key(x):
    bits = pltpu.bitcast(x + 0.0, jnp.int32)
    return jnp.where(bits < 0, bits ^ 0x7FFFFFFF, bits)


def _dsa_kernel(iqt_ref, iwt_ref, ik_ref, qt_ref, k_ref, vt_ref, dt_ref, o_ref,
                key_ref, madd_ref, *state, seq, k_sel):
    qi = pl.program_id(1)
    q0 = qi * QB
    n_chunk = (q0 + QB + KW - 1) // KW
    n_rest = seq - n_chunk * KW
    kpos = lax.broadcasted_iota(jnp.int32, (KW, QB), 0)
    tq = q0 + lax.broadcasted_iota(jnp.int32, (KW, QB), 1)
    hpp = KW // QB

    def score_chunk(c, x):
        c0 = pl.multiple_of(c * KW, KW)
        ikc = ik_ref[pl.ds(c0, KW), :]
        acc = jnp.zeros((KW, QB), F32)
        for piece in range(IDX_HEADS // hpp):
            sl = slice(piece * KW, (piece + 1) * KW)
            s = jnp.maximum(_dot(ikc, iqt_ref[0, :, sl]), 0.0) * iwt_ref[0, :, sl]
            for j in range(hpp):
                acc = acc + s[:, j * QB:(j + 1) * QB]
        acc = jnp.where(c0 + kpos <= tq, acc, NEG)
        key_ref[pl.ds(c0, KW), :] = _sort_key(acc)
        return x

    lax.fori_loop(0, n_chunk, score_chunk, 0)

    def count(pred):
        def body(c, acc):
            c0 = pl.multiple_of(c * KW, KW)
            hit = jnp.where(pred(key_ref[pl.ds(c0, KW), :], c0), 1.0, 0.0)
            parts = [hit[8 * i:8 * (i + 1)] for i in range(KW // 8)]
            while len(parts) > 1:
                parts = [parts[i] + parts[i + 1] for i in range(0, len(parts), 2)]
            return acc + parts[0]
        acc = lax.fori_loop(0, n_chunk, body, jnp.zeros((8, QB), F32))
        return jnp.sum(acc, axis=0, keepdims=True)

    rest = n_rest.astype(F32)
    kf = float(k_sel)

    def bit_cond(st):
        i, _, _, settled = st
        return (i < 32) & (jnp.min(settled) < 0.5)

    def bit_step(st):
        i, u, thr_s, settled = st
        bit = jnp.left_shift(jnp.int32(1), 31 - i)
        trial = (u | bit) ^ INT_MIN
        cnt = count(lambda keys, c0: keys >= trial) + jnp.where(NEG_KEY >= trial, rest, 0.0)
        new = (cnt == kf) & (settled < 0.5)
        return (i + 1, jnp.where(cnt >= kf, u | bit, u), jnp.where(new, trial, thr_s),
                jnp.where(new, 1.0, settled))

    _, u, thr_s, settled = lax.while_loop(
        bit_cond, bit_step,
        (jnp.int32(0), jnp.zeros((1, QB), jnp.int32), jnp.zeros((1, QB), jnp.int32), jnp.zeros((1, QB), F32)))
    is_settled = settled > 0.5
    thr = jnp.where(is_settled, thr_s, u ^ INT_MIN)
    def edge_counts():
        return (count(lambda keys, c0: keys > thr) + jnp.where(NEG_KEY > thr, rest, 0.0),
                count(lambda keys, c0: keys >= thr) + jnp.where(NEG_KEY >= thr, rest, 0.0))

    zero_cnt = jnp.zeros((1, QB), F32)
    cnt_gt, cnt_ge = lax.cond(jnp.min(settled) > 0.5, lambda: (zero_cnt, zero_cnt), edge_counts)
    need = kf - cnt_gt
    tie_q = (cnt_ge > kf) & (thr != NEG_KEY) & jnp.logical_not(is_settled)
    idx_bits = (seq - 1).bit_length()
    no_cut = 2 ** 30

    def tie_cut():
        def idx_step(i, x):
            bit = jnp.left_shift(jnp.int32(1), idx_bits - 1 - i)
            trial = x | bit
            f = count(lambda keys, c0: (keys == thr) & (c0 + kpos < trial))
            return jnp.where(f <= need - 1.0, trial, x)
        return lax.fori_loop(0, idx_bits, idx_step, jnp.zeros((1, QB), jnp.int32))

    any_tie = jnp.max(jnp.where(tie_q, 1.0, 0.0)) > 0.0
    x_cut = lax.cond(any_tie, tie_cut, lambda: jnp.full((1, QB), no_cut, jnp.int32))
    x_cut = jnp.where(tie_q, x_cut, no_cut)

    def mask_chunk(c, x):
        c0 = pl.multiple_of(c * KW, KW)
        keys = key_ref[pl.ds(c0, KW), :]
        pos = c0 + kpos
        chosen = (keys > thr) | ((keys == thr) & (pos <= x_cut))
        madd_ref[pl.ds(c0, KW), :] = jnp.where(chosen & (pos <= tq), 0.0, NEG)
        return x

    lax.fori_loop(0, n_chunk, mask_chunk, 0)

    c_near = jnp.maximum(qi - 1, 0) // (KW // LANE)
    ng = DSA_KV_HEADS
    chains = [state[3 * g:3 * g + 3] for g in range(ng)]
    sbuf = [state[(3 + b) * ng:(4 + b) * ng] for b in range(2)]
    cbuf = [state[(5 + b) * ng:(6 + b) * ng] for b in range(2)]
    group_heads = [[g * DSA_HPG + h for h in range(DSA_HPG)] for g in range(ng)]
    for ch in chains:
        _flash_init(*ch)

    def raw_scores(c0, g):
        q_t = jnp.concatenate([qt_ref[h] for h in group_heads[g]], axis=1)
        return _dot(k_ref[g, pl.ds(c0, KW), :], q_t)

    def qk_stage(c, buf):
        c0 = pl.multiple_of(c * KW, KW)
        madd = _tile_lanes(madd_ref[pl.ds(c0, KW), :], DSA_HPG)
        col_max = []
        for g in range(ng):
            s = raw_scores(c0, g) + madd
            sbuf[buf][g][...] = s
            col_max.append(jnp.max(s, axis=0, keepdims=True))
            cbuf[buf][g][...] = col_max[-1]
        return col_max

    def soft_stage(c, buf, ahead):
        c0 = pl.multiple_of(c * KW, KW)
        for g, ch in enumerate(chains):
            _flash_update(sbuf[buf][g][...], vt_ref[g, :, pl.ds(c0, KW)], *ch, col_max=cbuf[buf][g][...],
                          after=None if ahead is None else ahead[g])

    _pipelined_chunks(c_near, qk_stage, soft_stage)

    def near_step(c, x):
        c0 = pl.multiple_of(c * KW, KW)
        madd = _tile_lanes(madd_ref[pl.ds(c0, KW), :], DSA_HPG)
        scores = [raw_scores(c0, g) + madd + _near_bias(dt_ref, group_heads[g], qi, c * (KW // LANE), KW // LANE)
                  for g in range(ng)]
        for g, ch in enumerate(chains):
            _flash_update(scores[g], vt_ref[g, :, pl.ds(c0, KW)], *ch)
        return x

    lax.fori_loop(c_near, n_chunk, near_step, 0)
    for g, ch in enumerate(chains):
        o_t = _flash_result(*ch)
        for h in range(DSA_HPG):
            hh = group_heads[g][h]
            o_ref[:, hh * HEAD_DIM:(hh + 1) * HEAD_DIM] = o_t[:, h * QB:(h + 1) * QB].T.astype(o_ref.dtype)


def dsa_attention(iq_t, iw_t, ik, q_t, k, v_t, dt, bsz, seq):
    nq = seq // QB
    k_sel = min(DSA_TOPK_MAX, seq // 4)
    assert seq % KW == 0
    lanes = DSA_HPG * QB
    return pl.pallas_call(
        functools.partial(_dsa_kernel, seq=seq, k_sel=k_sel),
        grid=(bsz, nq),
        in_specs=[pl.BlockSpec((1, LANE, IDX_HEADS * QB), lambda b, i: (b * nq + i, 0, 0)),
                  pl.BlockSpec((1, 1, IDX_HEADS * QB), lambda b, i: (b * nq + i, 0, 0)),
                  pl.BlockSpec((seq, LANE), lambda b, i: (b, 0)),
                  pl.BlockSpec((DSA_HEADS, HEAD_DIM, QB), lambda b, i: (0, 0, b * nq + i)),
                  pl.BlockSpec((DSA_KV_HEADS, seq, HEAD_DIM), lambda b, i: (0, b, 0),
                               pipeline_mode=pl.Buffered(1)),
                  pl.BlockSpec((DSA_KV_HEADS, HEAD_DIM, seq), lambda b, i: (0, 0, b),
                               pipeline_mode=pl.Buffered(1)),
                  pl.BlockSpec((DSA_HEADS, 3, LANE, LANE), lambda b, i: (0, 0, 0, 0),
                               pipeline_mode=pl.Buffered(1))],
        out_specs=pl.BlockSpec((QB, DSA_HEADS * HEAD_DIM), lambda b, i: (b * nq + i, 0)),
        out_shape=jax.ShapeDtypeStruct((bsz * seq, DSA_HEADS * HEAD_DIM), BF16),
        scratch_shapes=[pltpu.VMEM((seq, QB), jnp.int32), pltpu.VMEM((seq, QB), F32)]
        + [pltpu.VMEM((1, lanes), F32), pltpu.VMEM((1, lanes), F32),
           pltpu.VMEM((HEAD_DIM, lanes), F32)] * DSA_KV_HEADS
        + [pltpu.VMEM((KW, lanes), F32)] * (2 * DSA_KV_HEADS)
        + [pltpu.VMEM((1, lanes), F32)] * (2 * DSA_KV_HEADS),
        compiler_params=_cparams(("arbitrary", "arbitrary")),
        name="dsa_attention",
    )(iq_t, iw_t, ik, q_t, k, v_t, dt)


def _rms(x, g):
    return x * lax.rsqrt(jnp.mean(x * x, axis=-1, keepdims=True) + EPS) * g


def _rope_tables(seq, dim):
    half = dim // 2
    inv = ROPE_THETA ** (-jnp.arange(half, dtype=F32) / half)
    ang = jnp.arange(seq, dtype=F32)[:, None] * inv[None, :]
    return jnp.cos(ang), jnp.sin(ang)


def _rope(x, cos, sin):
    half = x.shape[-1] // 2
    x1, x2 = x[..., :half], x[..., half:]
    return jnp.concatenate([x1 * cos - x2 * sin, x1 * sin + x2 * cos], axis=-1)


def _pad_cols(w, n):
    return jnp.pad(w, ((0, 0), (0, n - w.shape[1])))


def _t(x):
    return jnp.swapaxes(x, -1, -2)


def _even_mixer(h, x2, gate, dt, dc, bsz, seq, w_in, w_out, nsa_qk_g, cmp_pe, cmp_w1, cmp_b1,
                cmp_w2, cmp_b2, q_norm_g, kv_norm_g, w_uq, w_ukv, nope_g, rope_g):
    m = bsz * seq
    nq_cols = NSA_HEADS * HEAD_DIM
    nkv_cols = 6 * NSA_GROUPS * HEAD_DIM
    ngate = 3 * NSA_HEADS
    o_gate = nq_cols + nkv_cols
    o_cq = o_gate + ngate
    o_ckv = o_cq + MLA_Q_RANK
    o_kpe = o_ckv + MLA_KV_RANK
    gw = NSA_GROUPS * HEAD_DIM
    kvw = [w_in[:, nq_cols + i * gw:nq_cols + (i + 1) * gw] for i in range(6)]
    scale = HEAD_DIM ** -0.5 * LOG2E
    q_t = proj_heads(h, w_in[:, :nq_cols].astype(BF16), nsa_qk_g[0] * scale, transpose=True)
    k_sw = proj_heads(h, jnp.concatenate([kvw[2], kvw[4]], axis=1).astype(BF16), nsa_qk_g[1],
                      transpose=False)
    v_sw_t = proj_heads(h, jnp.concatenate([kvw[3], kvw[5]], axis=1).astype(BF16), transpose=True)
    tail = jnp.concatenate([w_in[:, o_kpe:], w_in[:, o_gate:o_cq]], axis=1)
    w_r = jnp.concatenate([kvw[0], kvw[1], w_in[:, o_cq:o_kpe], _pad_cols(tail, LANE)], axis=1).astype(BF16)
    proj = proj_slabs(h, w_r, tn=w_r.shape[1])
    s_cq = 2 * NSA_GROUPS
    s_ckv = s_cq + MLA_Q_RANK // LANE
    s_tail = s_ckv + MLA_KV_RANK // LANE
    kvc = compress_kv(proj, 0, bsz, seq, cmp_pe, cmp_w1, cmp_b1, cmp_w2, cmp_b2, nsa_qk_g[1])
    tail_v = proj[s_tail]
    gates = tail_v[:, MLA_ROPE:MLA_ROPE + ngate].reshape(m, NSA_GROUPS, 3 * NSA_HPG)
    gates_t = jnp.pad(jnp.transpose(gates, (1, 2, 0)), ((0, 0), (0, 16 - 3 * NSA_HPG), (0, 0)))
    o_nsa = nsa_attention(q_t, gates_t, kvc[0], _t(kvc[1]), k_sw, v_sw_t,
                          dt[:NSA_HEADS], dc[:NSA_HEADS], bsz, seq)

    dq = MLA_NOPE + MLA_ROPE
    wq = w_uq.reshape(MLA_Q_RANK, MLA_HEADS, dq)
    wq_r = jnp.concatenate([wq[:, :, :MLA_NOPE].reshape(MLA_Q_RANK, -1),
                            wq[:, :, MLA_NOPE:].reshape(MLA_Q_RANK, -1)], axis=1).astype(BF16)
    qf = normproj_slabs(proj[s_cq:s_ckv], q_norm_g, wq_r, tn=512)
    kvf = normproj_slabs(proj[s_ckv:s_tail], kv_norm_g, w_ukv.astype(BF16), tn=512)
    cos, sin = _rope_tables(seq, MLA_ROPE)
    cos = jnp.tile(cos, (bsz, 1))
    sin = jnp.tile(sin, (bsz, 1))
    mscale = dq ** -0.5 * LOG2E
    q_nope = _rms(qf[:MLA_HEADS], nope_g[0])
    q_pe = qf[MLA_HEADS:].reshape(MLA_HEADS // 2, m, 2, MLA_ROPE)
    q_pe = jnp.transpose(q_pe, (0, 2, 1, 3)).reshape(MLA_HEADS, m, MLA_ROPE)
    q_pe = _rope(_rms(q_pe, rope_g[0]), cos[None], sin[None])
    zpad = jnp.zeros((MLA_HEADS, m, 2 * LANE - dq), F32)
    q_mla_t = _t((jnp.concatenate([q_nope, q_pe, zpad], axis=-1) * mscale).astype(BF16))
    kvf = kvf.reshape(MLA_HEADS, 2, m, LANE)
    k_nope = _rms(kvf[:, 0], nope_g[1])
    k_pe = _rope(_rms(tail_v[:, :MLA_ROPE], rope_g[1]), cos, sin)
    k_mla = jnp.concatenate([k_nope, jnp.broadcast_to(k_pe[None], (MLA_HEADS, m, MLA_ROPE)), zpad],
                            axis=-1).astype(BF16)
    v_mla_t = _t(kvf[:, 1].astype(BF16))
    o_mla = mla_attention(q_mla_t, k_mla, v_mla_t, bsz, seq)
    w_o = w_out.astype(BF16)
    return resproj([(o_nsa, w_o[:nq_cols]), (o_mla, w_o[nq_cols:])], x2, gate, seq)


def _odd_mixer(h, x2, gate, dt, bsz, seq, w_in, w_out, qk_g):
    m = bsz * seq
    nt = m // QB
    nq = DSA_HEADS * HEAD_DIM
    nkv = DSA_KV_HEADS * HEAD_DIM
    niq = IDX_HEADS * IDX_DIM
    o_k, o_v, o_iq = nq, nq + nkv, nq + 2 * nkv
    q_t = proj_heads(h, w_in[:, :o_k].astype(BF16), qk_g[0] * (HEAD_DIM ** -0.5 * LOG2E), transpose=True)
    k = proj_heads(h, w_in[:, o_k:o_v].astype(BF16), qk_g[1], transpose=False)
    v_t = proj_heads(h, w_in[:, o_v:o_iq].astype(BF16), transpose=True)
    w_idx = w_in[:, o_iq:]
    proj = proj_slabs(h, _pad_cols(w_idx, niq + LANE).astype(BF16), tn=niq + LANE)
    s_tail = niq // LANE
    cos, sin = _rope_tables(seq, IDX_ROPE)
    cos = jnp.tile(cos, (bsz, 1))
    sin = jnp.tile(sin, (bsz, 1))
    iq = proj[:s_tail].reshape(IDX_HEADS // 2, m, 2, IDX_DIM)
    iq = jnp.transpose(iq, (0, 2, 1, 3)).reshape(IDX_HEADS, m, IDX_DIM)
    iq = jnp.concatenate([_rope(iq[..., :IDX_ROPE], cos[None], sin[None]), iq[..., IDX_ROPE:]], axis=-1)
    iq = (iq * IDX_DIM ** -0.5).astype(BF16).reshape(IDX_HEADS, nt, QB, IDX_DIM)
    iq_t = jnp.transpose(iq, (1, 3, 0, 2)).reshape(nt, IDX_DIM, IDX_HEADS * QB)
    iq_t = jnp.pad(iq_t, ((0, 0), (0, LANE - IDX_DIM), (0, 0)))
    tail = proj[s_tail]
    ik = tail[:, :IDX_DIM]
    ik = jnp.concatenate([_rope(ik[:, :IDX_ROPE], cos, sin), ik[:, IDX_ROPE:]], axis=-1)
    ik = jnp.pad(ik, ((0, 0), (0, LANE - IDX_DIM))).astype(BF16)
    iw = (tail[:, IDX_DIM:IDX_DIM + IDX_HEADS] * IDX_HEADS ** -0.5).reshape(nt, QB, IDX_HEADS)
    iw_t = jnp.transpose(iw, (0, 2, 1)).reshape(nt, 1, IDX_HEADS * QB)
    o = dsa_attention(iq_t, iw_t, ik, q_t, k, v_t, dt, bsz, seq)
    return resproj([(o, w_out.astype(BF16))], x2, gate, seq)


def _conv_ffn(h, x2, gate, seq, w_up, conv_w, conv_b, w_down):
    a = ffn_up(h, w_up.astype(BF16), conv_w, conv_b, seq)
    return resproj([(a, w_down.astype(BF16))], x2, gate, seq)


def kernel(x, c, rel_bias, ada_w, ada_b, norm_g, ev_w_in, ev_w_out, nsa_qk_g, cmp_pe, cmp_w1, cmp_b1, cmp_w2, cmp_b2, mla_q_norm_g, mla_kv_norm_g, mla_w_uq, mla_w_ukv, mla_nope_g, mla_rope_g, od_w_in, od_w_out, dsa_qk_g, ffn_w_up, ffn_conv_w, ffn_conv_b, ffn_w_down):
    bsz, seq, d = x.shape
    depth = ada_w.shape[0]
    x2 = x.reshape(bsz * seq, d)
    mods = ada_all(c, ada_w, ada_b)
    dt, dc = bias_tiles(rel_bias)
    for i in range(depth):
        j = i // 2
        shift, scale, gate = jnp.split(mods[i, 0], 3, axis=-1)
        h = modnorm(x2, norm_g[i, 0], scale, shift, seq)
        if i % 2 == 0:
            x2 = _even_mixer(h, x2, gate, dt, dc, bsz, seq, ev_w_in[j], ev_w_out[j], nsa_qk_g[j],
                             cmp_pe[j], cmp_w1[j], cmp_b1[j], cmp_w2[j], cmp_b2[j], mla_q_norm_g[j],
                             mla_kv_norm_g[j], mla_w_uq[j], mla_w_ukv[j], mla_nope_g[j], mla_rope_g[j])
        else:
            x2 = _odd_mixer(h, x2, gate, dt, bsz, seq, od_w_in[j], od_w_out[j], dsa_qk_g[j])
        shift, scale, gate = jnp.split(mods[i, 1], 3, axis=-1)
        h = modnorm(x2, norm_g[i, 1], scale, shift, seq)
        x2 = _conv_ffn(h, x2, gate, seq, ffn_w_up[i], ffn_conv_w[i], ffn_conv_b[i], ffn_w_down[i])
    return x2.reshape(bsz, seq, d)
```

```python
import functools
import math

import numpy as np
import jax
import jax.numpy as jnp
from jax import lax
from jax.experimental import pallas as pl
from jax.experimental.pallas import tpu as pltpu

HEAD_DIM = 128
NSA_HEADS = 8
NSA_GROUPS = 2
NSA_HPG = NSA_HEADS // NSA_GROUPS
CMP_BLOCK = 32
CMP_STRIDE = 16
CMP_HIDDEN = 256
SEL_BLOCK = 64
SEL_TOP_N = 16
WINDOW = 512
MLA_HEADS = 8
MLA_Q_RANK = 512
MLA_KV_RANK = 256
MLA_NOPE = 128
MLA_ROPE = 64
MLA_V = 128
DSA_HEADS = 16
DSA_KV_HEADS = 4
DSA_HPG = DSA_HEADS // DSA_KV_HEADS
IDX_HEADS = 16
IDX_DIM = 64
IDX_ROPE = 32
DSA_TOPK_MAX = 256
REL_BUCKETS = 32
REL_MAX_DIST = 128
CONV_WIDTH = 3
ROPE_THETA = 10000.0
EPS = 1e-6
NEG = -1e30
FORCE = 1e9

LANE = 128
QB = 128
VMEM_LIMIT = 56 * 1024 * 1024

F32 = jnp.float32
BF16 = jnp.bfloat16


def _t5_thresholds():
    d = np.arange(0, 4 * REL_MAX_DIST)
    half = REL_BUCKETS // 2
    val = np.log(np.maximum(d, 1) / half) / math.log(REL_MAX_DIST / half) * (REL_BUCKETS - half)
    large = np.minimum(half + np.floor(np.maximum(val, 0.0)).astype(np.int64), REL_BUCKETS - 1)
    bucket = np.where(d < half, d, large)
    return [int(np.argmax(bucket >= b)) for b in range(1, REL_BUCKETS)]


T5_THR = _t5_thresholds()
T5_FAR = T5_THR[-1]
assert T5_FAR <= LANE


def _cparams(sem):
    return pltpu.CompilerParams(dimension_semantics=sem, vmem_limit_bytes=VMEM_LIMIT)


def _dot(a, b):
    return jnp.dot(a, b, preferred_element_type=F32)


def _ada_kernel(c_ref, w_ref, b_ref, o_ref):
    c = c_ref[...]
    a = c * jax.nn.sigmoid(c)
    o_ref[0] = jnp.dot(a, w_ref[0], preferred_element_type=F32,
                       precision=lax.Precision.HIGHEST) + b_ref[0]


def ada_all(c, ada_w, ada_b):
    depth, two, d, n3 = ada_w.shape
    bsz = c.shape[0]
    rows = 8
    cp = jnp.zeros((rows, d), F32).at[:bsz].set(c)
    w = ada_w.reshape(depth * two, d, n3)
    b = ada_b.reshape(depth * two, 1, n3)
    tn = 512
    out = pl.pallas_call(
        _ada_kernel,
        grid=(depth * two, n3 // tn),
        in_specs=[pl.BlockSpec((rows, d), lambda l, j: (0, 0)),
                  pl.BlockSpec((1, d, tn), lambda l, j: (l, 0, j)),
                  pl.BlockSpec((1, 1, tn), lambda l, j: (l, 0, j))],
        out_specs=pl.BlockSpec((1, rows, tn), lambda l, j: (l, 0, j)),
        out_shape=jax.ShapeDtypeStruct((depth * two, rows, n3), F32),
        compiler_params=_cparams(("arbitrary", "arbitrary")),
        name="ada_mod",
    )(cp, w, b)
    return out[:, :bsz].reshape(depth, two, bsz, n3)


def _modnorm_kernel(x_ref, g_ref, sc_ref, sh_ref, o_ref):
    x = x_ref[...]
    y = x * lax.rsqrt(jnp.mean(x * x, axis=-1, keepdims=True) + EPS)
    h = (y * g_ref[...]) * (1.0 + sc_ref[0]) + sh_ref[0]
    o_ref[...] = h.astype(o_ref.dtype)


def modnorm(x2, g, scale, shift, seq):
    m, d = x2.shape
    tm = 512
    tpb = seq // tm
    return pl.pallas_call(
        _modnorm_kernel,
        grid=(m // tm,),
        in_specs=[pl.BlockSpec((tm, d), lambda i: (i, 0)),
                  pl.BlockSpec((1, d), lambda i: (0, 0)),
                  pl.BlockSpec((1, 1, d), lambda i: (i // tpb, 0, 0)),
                  pl.BlockSpec((1, 1, d), lambda i: (i // tpb, 0, 0))],
        out_specs=pl.BlockSpec((tm, d), lambda i: (i, 0)),
        out_shape=jax.ShapeDtypeStruct((m, d), BF16),
        compiler_params=_cparams(("arbitrary",)),
        name="modnorm",
    )(x2, g.reshape(1, d), scale.reshape(-1, 1, d), shift.reshape(-1, 1, d))


def _proj_kernel(x_ref, w_ref, o_ref, *, nslab):
    acc = _dot(x_ref[...], w_ref[...])
    for s in range(nslab):
        o_ref[s] = acc[:, s * LANE:(s + 1) * LANE]


def proj_slabs(x, w, tm=1024, tn=384):
    m, k = x.shape
    n = w.shape[1]
    assert n % tn == 0 and m % tm == 0
    nslab = tn // LANE
    return pl.pallas_call(
        functools.partial(_proj_kernel, nslab=nslab),
        grid=(m // tm, n // tn),
        in_specs=[pl.BlockSpec((tm, k), lambda i, j: (i, 0)),
                  pl.BlockSpec((k, tn), lambda i, j: (0, j))],
        out_specs=pl.BlockSpec((nslab, tm, LANE), lambda i, j: (j, i, 0)),
        out_shape=jax.ShapeDtypeStruct((n // LANE, m, LANE), F32),
        compiler_params=_cparams(("arbitrary", "arbitrary")),
        name="proj_slabs",
    )(x, w)


def _proj_heads_kernel(x_ref, w_ref, g_ref, o_ref, *, nslab, norm, transpose):
    acc = _dot(x_ref[...], w_ref[...])
    for s in range(nslab):
        y = acc[:, s * LANE:(s + 1) * LANE]
        if norm:
            y = y * lax.rsqrt(jnp.mean(y * y, axis=-1, keepdims=True) + EPS) * g_ref[...]
        o_ref[s] = (y.T if transpose else y).astype(o_ref.dtype)


def proj_heads(x, w, g=None, *, transpose, tm=1024, tn=1024):
    m, k = x.shape
    n = w.shape[1]
    tn = min(tn, n)
    assert n % tn == 0 and m % tm == 0
    nslab = tn // LANE
    norm = g is not None
    if transpose:
        out_spec = pl.BlockSpec((nslab, LANE, tm), lambda i, j: (j, 0, i))
        out_shape = jax.ShapeDtypeStruct((n // LANE, LANE, m), BF16)
    else:
        out_spec = pl.BlockSpec((nslab, tm, LANE), lambda i, j: (j, i, 0))
        out_shape = jax.ShapeDtypeStruct((n // LANE, m, LANE), BF16)
    g2 = (g if norm else jnp.ones((LANE,), F32)).reshape(1, LANE)
    return pl.pallas_call(
        functools.partial(_proj_heads_kernel, nslab=nslab, norm=norm, transpose=transpose),
        grid=(m // tm, n // tn),
        in_specs=[pl.BlockSpec((tm, k), lambda i, j: (i, 0)),
                  pl.BlockSpec((k, tn), lambda i, j: (0, j)),
                  pl.BlockSpec((1, LANE), lambda i, j: (0, 0))],
        out_specs=out_spec,
        out_shape=out_shape,
        compiler_params=_cparams(("arbitrary", "arbitrary")),
        name="proj_heads",
    )(x, w, g2)


def _rms_rows(x, g):
    return x * lax.rsqrt(jnp.mean(x * x, axis=-1, keepdims=True) + EPS) * g


def _rope_rows(x, cos, sin):
    half = x.shape[-1] // 2
    x1, x2 = x[:, :half], x[:, half:]
    return jnp.concatenate([x1 * cos - x2 * sin, x1 * sin + x2 * cos], axis=1)


def _latent(x_ref, g_ref):
    x = jnp.concatenate([x_ref[s] for s in range(x_ref.shape[0])], axis=1)
    return _rms_rows(x, g_ref[...]).astype(BF16)


def _mla_q_kernel(x_ref, g_ref, w_ref, gn_ref, gr_ref, cos_ref, sin_ref, o_ref, *, scale):
    acc = _dot(_latent(x_ref, g_ref), w_ref[...])
    tm = acc.shape[0]
    cos, sin = cos_ref[...], sin_ref[...]
    pad = jnp.zeros((tm, LANE - MLA_ROPE), F32)
    for h in range(MLA_HEADS):
        nope = _rms_rows(acc[:, h * MLA_NOPE:(h + 1) * MLA_NOPE], gn_ref[...]) * scale
        r0 = MLA_HEADS * MLA_NOPE + h * MLA_ROPE
        pe = _rope_rows(_rms_rows(acc[:, r0:r0 + MLA_ROPE], gr_ref[...]), cos, sin) * scale
        o_ref[h, 0:MLA_NOPE, :] = nope.T.astype(o_ref.dtype)
        o_ref[h, MLA_NOPE:MLA_NOPE + LANE, :] = jnp.concatenate([pe, pad], axis=1).T.astype(o_ref.dtype)


def _mla_kv_kernel(x_ref, g_ref, w_ref, tail_ref, gn_ref, gr_ref, cos_ref, sin_ref, ok_ref, ov_ref):
    acc = _dot(_latent(x_ref, g_ref), w_ref[...])
    tm = acc.shape[0]
    k_pe = _rope_rows(_rms_rows(tail_ref[0][:, :MLA_ROPE], gr_ref[...]), cos_ref[...], sin_ref[...])
    k_pe = jnp.concatenate([k_pe, jnp.zeros((tm, LANE - MLA_ROPE), F32)], axis=1).astype(ok_ref.dtype)
    for h in range(MLA_HEADS):
        c0 = h * (MLA_NOPE + MLA_V)
        ok_ref[h, :, 0:MLA_NOPE] = _rms_rows(acc[:, c0:c0 + MLA_NOPE], gn_ref[...]).astype(ok_ref.dtype)
        ok_ref[h, :, MLA_NOPE:MLA_NOPE + LANE] = k_pe
        ov_ref[h] = acc[:, c0 + MLA_NOPE:c0 + MLA_NOPE + MLA_V].T.astype(ov_ref.dtype)


def mla_project(proj, s_cq, s_ckv, s_tail, seq, q_norm_g, kv_norm_g, wq_r, w_ukv, nope_g, rope_g, cos, sin,
                scale, tm=512):
    _, m, _ = proj.shape
    kq, kkv = s_ckv - s_cq, s_tail - s_ckv
    tps = seq // tm
    dqk = MLA_NOPE + LANE
    half = MLA_ROPE // 2
    rope_specs = [pl.BlockSpec((tm, half), lambda i: (i % tps, 0))] * 2
    gain_specs = [pl.BlockSpec((1, MLA_NOPE), lambda i: (0, 0)), pl.BlockSpec((1, MLA_ROPE), lambda i: (0, 0))]
    q_t = pl.pallas_call(
        functools.partial(_mla_q_kernel, scale=scale),
        grid=(m // tm,),
        in_specs=[pl.BlockSpec((kq, tm, LANE), lambda i: (s_cq // kq, i, 0)),
                  pl.BlockSpec((1, kq * LANE), lambda i: (0, 0)),
                  pl.BlockSpec(wq_r.shape, lambda i: (0, 0))] + gain_specs + rope_specs,
        out_specs=pl.BlockSpec((MLA_HEADS, dqk, tm), lambda i: (0, 0, i)),
        out_shape=jax.ShapeDtypeStruct((MLA_HEADS, dqk, m), BF16),
        compiler_params=_cparams(("arbitrary",)),
        name="mla_q_project",
    )(proj, q_norm_g.reshape(1, -1), wq_r, nope_g[0].reshape(1, -1), rope_g[0].reshape(1, -1), cos, sin)
    k, v_t = pl.pallas_call(
        _mla_kv_kernel,
        grid=(m // tm,),
        in_specs=[pl.BlockSpec((kkv, tm, LANE), lambda i: (s_ckv // kkv, i, 0)),
                  pl.BlockSpec((1, kkv * LANE), lambda i: (0, 0)),
                  pl.BlockSpec(w_ukv.shape, lambda i: (0, 0)),
                  pl.BlockSpec((1, tm, LANE), lambda i: (s_tail, i, 0))] + gain_specs + rope_specs,
        out_specs=[pl.BlockSpec((MLA_HEADS, tm, dqk), lambda i: (0, i, 0)),
                   pl.BlockSpec((MLA_HEADS, MLA_V, tm), lambda i: (0, 0, i))],
        out_shape=[jax.ShapeDtypeStruct((MLA_HEADS, m, dqk), BF16),
                   jax.ShapeDtypeStruct((MLA_HEADS, MLA_V, m), BF16)],
        compiler_params=_cparams(("arbitrary",)),
        name="mla_kv_project",
    )(proj, kv_norm_g.reshape(1, -1), w_ukv, proj, nope_g[1].reshape(1, -1), rope_g[1].reshape(1, -1), cos, sin)
    return q_t, k, v_t


def _resproj_kernel(*refs, npair):
    xres_ref, gate_ref = refs[2 * npair], refs[2 * npair + 1]
    o_ref = refs[2 * npair + 2]
    acc = _dot(refs[0][...], refs[1][...])
    for p in range(1, npair):
        acc = acc + _dot(refs[2 * p][...], refs[2 * p + 1][...])
    o_ref[...] = xres_ref[...] + gate_ref[0] * acc


def resproj(pairs, xres, gate, seq, tm=1024, tn=512):
    m, n = xres.shape
    tpb = seq // tm
    in_specs, args = [], []
    for x, w in pairs:
        k = x.shape[1]
        in_specs += [pl.BlockSpec((tm, k), lambda i, j: (i, 0)),
                     pl.BlockSpec((k, tn), lambda i, j: (0, j))]
        args += [x, w]
    in_specs += [pl.BlockSpec((tm, tn), lambda i, j: (i, j)),
                 pl.BlockSpec((1, 1, tn), lambda i, j: (i // tpb, 0, j))]
    args += [xres, gate.reshape(-1, 1, n)]
    return pl.pallas_call(
        functools.partial(_resproj_kernel, npair=len(pairs)),
        grid=(m // tm, n // tn),
        in_specs=in_specs,
        out_specs=pl.BlockSpec((tm, tn), lambda i, j: (i, j)),
        out_shape=jax.ShapeDtypeStruct((m, n), F32),
        compiler_params=_cparams(("arbitrary", "arbitrary")),
        name="resproj",
    )(*args)


HALO = 8


def _ffn_up_kernel(h_ref, wg_ref, wv_ref, cwg_ref, cwv_ref, cbg_ref, cbv_ref, o_ref,
                   ug_ref, uv_ref, *, tm, tiles_per_seq):
    i = pl.program_id(1)
    first = (i % tiles_per_seq) == 0

    @pl.when(first)
    def _():
        ug_ref[0:HALO, :] = jnp.zeros((HALO, ug_ref.shape[1]), F32)
        uv_ref[0:HALO, :] = jnp.zeros((HALO, uv_ref.shape[1]), F32)

    @pl.when(jnp.logical_not(first))
    def _():
        ug_ref[0:HALO, :] = ug_ref[tm:tm + HALO, :]
        uv_ref[0:HALO, :] = uv_ref[tm:tm + HALO, :]

    h = h_ref[...]
    ug_ref[HALO:HALO + tm, :] = _dot(h, wg_ref[...])
    uv_ref[HALO:HALO + tm, :] = _dot(h, wv_ref[...])

    def conv(u_ref, cw_ref, cb_ref):
        out = cb_ref[...]
        for j in range(CONV_WIDTH):
            off = HALO - (CONV_WIDTH - 1) + j
            out = out + cw_ref[j:j + 1, :] * u_ref[off:off + tm, :]
        return out

    g = conv(ug_ref, cwg_ref, cbg_ref)
    v = conv(uv_ref, cwv_ref, cbv_ref)
    o_ref[...] = (g * jax.nn.sigmoid(g) * v).astype(o_ref.dtype)


def ffn_up(h, w_up, conv_w, conv_b, seq, tm=1024, tn=512):
    m, d = h.shape
    f = w_up.shape[1] // 2
    nj = f // tn
    tps = seq // tm
    cb = conv_b.reshape(1, 2 * f)
    return pl.pallas_call(
        functools.partial(_ffn_up_kernel, tm=tm, tiles_per_seq=tps),
        grid=(nj, m // tm),
        in_specs=[pl.BlockSpec((tm, d), lambda j, i: (i, 0)),
                  pl.BlockSpec((d, tn), lambda j, i: (0, j)),
                  pl.BlockSpec((d, tn), lambda j, i: (0, nj + j)),
                  pl.BlockSpec((CONV_WIDTH, tn), lambda j, i: (0, j)),
                  pl.BlockSpec((CONV_WIDTH, tn), lambda j, i: (0, nj + j)),
                  pl.BlockSpec((1, tn), lambda j, i: (0, j)),
                  pl.BlockSpec((1, tn), lambda j, i: (0, nj + j))],
        out_specs=pl.BlockSpec((tm, tn), lambda j, i: (i, j)),
        out_shape=jax.ShapeDtypeStruct((m, f), BF16),
        scratch_shapes=[pltpu.VMEM((tm + HALO, tn), F32), pltpu.VMEM((tm + HALO, tn), F32)],
        compiler_params=_cparams(("arbitrary", "arbitrary")),
        name="ffn_up_conv",
    )(h, w_up, w_up, conv_w, conv_w, cb, cb)


LOG2E = 1.4426950408889634
CWIN = 16


def _t5_shifted(dist, tbl_ref, h):
    val = jnp.full(dist.shape, tbl_ref[0, h], F32)
    for b in range(1, REL_BUCKETS):
        val = jnp.where(dist >= T5_THR[b - 1], tbl_ref[b, h], val)
    return (val - tbl_ref[REL_BUCKETS - 1, h]) * LOG2E


def _bias_tiles_kernel(tbl_ref, dt_ref, dc_ref):
    h = pl.program_id(0)
    key = lax.broadcasted_iota(jnp.int32, (LANE, LANE), 0)
    q = lax.broadcasted_iota(jnp.int32, (LANE, LANE), 1)
    for rel in range(2):
        dt_ref[0, rel] = _t5_shifted(rel * LANE + q - key, tbl_ref, h)
    dt_ref[0, 2] = jnp.zeros((LANE, LANE), F32)
    u = lax.broadcasted_iota(jnp.int32, (CWIN, LANE), 0)
    qc = lax.broadcasted_iota(jnp.int32, (CWIN, LANE), 1)
    dc_ref[0] = _t5_shifted(qc - CMP_STRIDE * (u - CWIN // 2) - (CMP_BLOCK - 1), tbl_ref, h)


def bias_tiles(rel_bias):
    nh = rel_bias.shape[1]
    return pl.pallas_call(
        _bias_tiles_kernel,
        grid=(nh,),
        in_specs=[pl.BlockSpec(memory_space=pltpu.SMEM)],
        out_specs=[pl.BlockSpec((1, 3, LANE, LANE), lambda h: (h, 0, 0, 0)),
                   pl.BlockSpec((1, CWIN, LANE), lambda h: (h, 0, 0))],
        out_shape=[jax.ShapeDtypeStruct((nh, 3, LANE, LANE), F32),
                   jax.ShapeDtypeStruct((nh, CWIN, LANE), F32)],
        compiler_params=_cparams(("arbitrary",)),
        name="t5_bias_tiles",
    )(rel_bias)


def _compress_kernel(x_ref, pe_ref, w1_ref, b1_ref, w2_ref, b2_ref, g_ref, o_ref, *, half):
    kv = pl.program_id(0)
    x = x_ref[0]
    a = _dot((x + pe_ref[0, :, :half]).astype(BF16), w1_ref[0, :half, :])
    b = _dot((x + pe_ref[0, :, half:]).astype(BF16), w1_ref[0, half:, :])
    b_next = jnp.concatenate([b[1:], jnp.zeros((1, b.shape[1]), F32)], axis=0)
    hid = jax.nn.gelu(a + b_next + b1_ref[0])
    out = _dot(hid.astype(BF16), w2_ref[0]) + b2_ref[0]
    normed = out * lax.rsqrt(jnp.mean(out * out, axis=-1, keepdims=True) + EPS) * g_ref[...]
    out = jnp.where(kv == 0, normed, out)
    o_ref[0, 0] = out.astype(o_ref.dtype)


def compress_kv(proj, slab0, bsz, seq, cmp_pe, cmp_w1, cmp_b1, cmp_w2, cmp_b2, g_k):
    nslab, m, _ = proj.shape
    nchunk = seq // CMP_STRIDE
    half = CMP_STRIDE * HEAD_DIM
    xv = proj.reshape(nslab, m // CMP_STRIDE, half)
    pe = cmp_pe.reshape(2, 1, CMP_BLOCK * HEAD_DIM)
    return pl.pallas_call(
        functools.partial(_compress_kernel, half=half),
        grid=(2, bsz, NSA_GROUPS),
        in_specs=[pl.BlockSpec((1, nchunk, half), lambda kv, b, g: (slab0 + 2 * kv + g, b, 0)),
                  pl.BlockSpec((1, 1, 2 * half), lambda kv, b, g: (kv, 0, 0)),
                  pl.BlockSpec((1, 2 * half, CMP_HIDDEN), lambda kv, b, g: (kv, 0, 0)),
                  pl.BlockSpec((1, 1, CMP_HIDDEN), lambda kv, b, g: (kv, 0, 0)),
                  pl.BlockSpec((1, CMP_HIDDEN, HEAD_DIM), lambda kv, b, g: (kv, 0, 0)),
                  pl.BlockSpec((1, 1, HEAD_DIM), lambda kv, b, g: (kv, 0, 0)),
                  pl.BlockSpec((1, HEAD_DIM), lambda kv, b, g: (0, 0))],
        out_specs=pl.BlockSpec((1, 1, nchunk, HEAD_DIM), lambda kv, b, g: (kv, g, b, 0)),
        out_shape=jax.ShapeDtypeStruct((2, NSA_GROUPS, bsz * nchunk, HEAD_DIM), BF16),
        compiler_params=_cparams(("arbitrary", "arbitrary", "arbitrary")),
        name="nsa_compress",
    )(xv, pe, cmp_w1.astype(BF16), cmp_b1.reshape(2, 1, CMP_HIDDEN), cmp_w2.astype(BF16),
      cmp_b2.reshape(2, 1, HEAD_DIM), g_k.reshape(1, HEAD_DIM))


KW = 512
PV_KEYS = 256


def _tile_lanes(x, n):
    return jnp.concatenate([x] * n, axis=1)


def _flash_init(m_ref, l_ref, acc_ref):
    m_ref[...] = jnp.full(m_ref.shape, NEG, F32)
    l_ref[...] = jnp.zeros(l_ref.shape, F32)
    acc_ref[...] = jnp.zeros(acc_ref.shape, F32)


def _zero_after(x):
    bits = pltpu.bitcast(x, jnp.int32)
    return lax.shift_right_logical(lax.shift_right_logical(bits, 16), 16).astype(F32)


def _flash_update(s, v_t, m_ref, l_ref, acc_ref, col_max=None, after=None):
    m_old = m_ref[...]
    if col_max is None:
        col_max = jnp.max(s, axis=0, keepdims=True)
    m_new = jnp.maximum(m_old, col_max)
    alpha = jnp.exp2(m_old - m_new)
    l_new = alpha * l_ref[...]
    acc = alpha * acc_ref[...]
    nk = s.shape[0]
    for k0 in range(0, nk, PV_KEYS):
        p = jnp.exp2(s[k0:k0 + PV_KEYS] - m_new)
        l_new = l_new + jnp.sum(p, axis=0, keepdims=True)
        acc = acc + _dot(v_t[:, k0:k0 + PV_KEYS], p.astype(BF16))
    l_ref[...] = l_new
    acc_ref[...] = acc
    m_ref[...] = m_new if after is None else m_new + _zero_after(after)


def _inv_den(m, den):
    ok = m > 0.5 * NEG
    return jnp.where(ok, 1.0 / jnp.where(ok, den, 1.0), 0.0)


def _flash_result(m_ref, l_ref, acc_ref):
    return acc_ref[...] * _inv_den(m_ref[...], l_ref[...])


def _softmax_cols(s):
    m = jnp.max(s, axis=0, keepdims=True)
    p = jnp.exp2(s - m)
    return p * _inv_den(m, jnp.sum(p, axis=0, keepdims=True))


def _near_bias(dt_ref, heads, qi, kt0, ntile):
    rows = []
    for j in range(ntile):
        rel = jnp.clip(qi - (kt0 + j), 0, 2)
        rows.append(jnp.concatenate([dt_ref[h, rel] for h in heads], axis=1))
    return jnp.concatenate(rows, axis=0)


def _pipelined_chunks(n, qk_stage, soft_stage):
    @pl.when(n > 0)
    def _():
        qk_stage(0, 0)

    def pair(p, x):
        c = 2 * p
        ahead = qk_stage(c + 1, 1)
        soft_stage(c, 0, ahead)
        ahead = qk_stage(jnp.minimum(c + 2, n - 1), 0)
        soft_stage(c + 1, 1, ahead)
        return x

    lax.fori_loop(0, n // 2, pair, 0)

    @pl.when(n % 2 == 1)
    def _():
        soft_stage(n - 1, 0, None)


def _nsa_kernel(qt_ref, gt_ref, kc_ref, vct_ref, ks_ref, vst_ref, kw_ref, vwt_ref,
                dt_ref, dc_ref, ext_ref, o_ref,
                sc_ref, ps_ref, m_ref, l_ref, acc_ref, sbuf0, sbuf1, cbuf0, cbuf1, *, seq, nc):
    sbuf, cbuf = (sbuf0, sbuf1), (cbuf0, cbuf1)
    qi = pl.program_id(2)
    q0 = qi * QB
    hpg = NSA_HPG
    heads = list(range(hpg))
    ncp = kc_ref.shape[1]
    ns = seq // SEL_BLOCK
    q_t = jnp.concatenate([qt_ref[h] for h in heads], axis=1)

    pad = CWIN // 2
    sc_ref[0:pad, :] = jnp.zeros((pad, hpg * QB), F32)
    sc_ref[pad + ncp:2 * pad + ncp, :] = jnp.zeros((pad, hpg * QB), F32)
    sc_ref[pad:pad + ncp, :] = _dot(kc_ref[0], q_t)
    r0 = pl.multiple_of(qi * (QB // CMP_STRIDE), 8)
    sc_ref[pl.ds(r0, CWIN), :] = sc_ref[pl.ds(r0, CWIN), :] + jnp.concatenate(
        [dc_ref[h] for h in heads], axis=1)
    ci = lax.broadcasted_iota(jnp.int32, (ncp, QB), 0)
    tc = q0 + lax.broadcasted_iota(jnp.int32, (ncp, QB), 1)
    valid_c = (ci * CMP_STRIDE + CMP_BLOCK - 1 <= tc) & (ci < nc)
    p_c = _softmax_cols(sc_ref[pad:pad + ncp, :] + _tile_lanes(jnp.where(valid_c, 0.0, NEG), hpg))
    oc_t = _dot(vct_ref[0], p_c.astype(BF16))
    p_sum = p_c[:, 0:QB]
    for h in range(1, hpg):
        p_sum = p_sum + p_c[:, h * QB:(h + 1) * QB]

    wkeys = WINDOW + QB
    start = pl.multiple_of(jnp.maximum(q0 - WINDOW, 0), LANE)
    s_w = _dot(kw_ref[0, pl.ds(start, wkeys), :], q_t)
    s_w = s_w + _near_bias(dt_ref, heads, qi, start // LANE, wkeys // LANE)
    dist_w = (q0 + lax.broadcasted_iota(jnp.int32, (wkeys, QB), 1)) - (
        start + lax.broadcasted_iota(jnp.int32, (wkeys, QB), 0))
    mask_w = (dist_w >= 0) & (dist_w < WINDOW)
    p_w = _softmax_cols(s_w + _tile_lanes(jnp.where(mask_w, 0.0, NEG), hpg))
    ow_t = _dot(vwt_ref[0, :, pl.ds(start, wkeys)], p_w.astype(BF16))

    ps_ref[0:8, :] = jnp.zeros((8, QB), F32)
    ps_ref[8:8 + ncp, :] = p_sum
    per = SEL_BLOCK // CMP_STRIDE
    band = [ps_ref[pl.ds(8 + r, ns, stride=per), :] for r in range(-1, per)]
    imp = 0.5 * band[0] + band[1] + band[2] + band[3] + 0.5 * band[4]
    if ns < LANE:
        imp = jnp.concatenate([imp, jnp.zeros((LANE - ns, QB), F32)], axis=0)
    blk = lax.broadcasted_iota(jnp.int32, (LANE, QB), 0)
    t = q0 + lax.broadcasted_iota(jnp.int32, (LANE, QB), 1)
    tb = t // SEL_BLOCK
    forced = (blk == 0) | (blk == tb) | (blk == tb - 1)
    score = jnp.where(forced, FORCE, jnp.where(blk * SEL_BLOCK <= t, imp, NEG))
    score = jnp.where(blk < ns, score, -jnp.inf)
    blk_f = blk.astype(F32)
    sel = jnp.zeros((LANE, QB), F32)
    for _ in range(min(SEL_TOP_N, ns)):
        mx = jnp.max(score, axis=0, keepdims=True)
        first = jnp.min(jnp.where(score == mx, blk_f, float(LANE)), axis=0, keepdims=True)
        pick = blk_f == first
        sel = jnp.where(pick, 1.0, sel)
        score = jnp.where(pick, -jnp.inf, score)
    sel_b = sel.astype(BF16)

    _flash_init(m_ref, l_ref, acc_ref)
    kpos = lax.broadcasted_iota(jnp.int32, (KW, QB), 0)
    tq = q0 + lax.broadcasted_iota(jnp.int32, (KW, QB), 1)

    def block_mask(c0):
        chosen = _dot(ext_ref[pl.ds(c0, KW), :], sel_b)
        return (chosen - 1.0) * (-NEG)

    def qk_stage(c, buf):
        c0 = pl.multiple_of(c * KW, KW)
        s = _dot(ks_ref[0, pl.ds(c0, KW), :], q_t) + _tile_lanes(block_mask(c0), hpg)
        sbuf[buf][...] = s
        col_max = jnp.max(s, axis=0, keepdims=True)
        cbuf[buf][...] = col_max
        return col_max

    def soft_stage(c, buf, ahead):
        c0 = pl.multiple_of(c * KW, KW)
        _flash_update(sbuf[buf][...], vst_ref[0, :, pl.ds(c0, KW)], m_ref, l_ref, acc_ref,
                      col_max=cbuf[buf][...], after=ahead)

    def near_step(c, x):
        c0 = pl.multiple_of(c * KW, KW)
        madd = block_mask(c0) + jnp.where(c0 + kpos <= tq, 0.0, NEG)
        s = _dot(ks_ref[0, pl.ds(c0, KW), :], q_t) + _near_bias(dt_ref, heads, qi, c * (KW // LANE), KW // LANE)
        _flash_update(s + _tile_lanes(madd, hpg), vst_ref[0, :, pl.ds(c0, KW)], m_ref, l_ref, acc_ref)
        return x

    c_near = jnp.maximum(qi - 1, 0) // (KW // LANE)
    _pipelined_chunks(c_near, qk_stage, soft_stage)
    lax.fori_loop(c_near, qi // (KW // LANE) + 1, near_step, 0)
    os_t = _flash_result(m_ref, l_ref, acc_ref)

    gates = jax.nn.sigmoid(gt_ref[0])
    for h in heads:
        sl = slice(h * QB, (h + 1) * QB)
        o_t = (gates[3 * h:3 * h + 1] * oc_t[:, sl] + gates[3 * h + 1:3 * h + 2] * os_t[:, sl]
               + gates[3 * h + 2:3 * h + 3] * ow_t[:, sl])
        o_ref[:, h * HEAD_DIM:(h + 1) * HEAD_DIM] = o_t.T.astype(o_ref.dtype)


def nsa_attention(q_t, gates_t, kc, vc_t, k_sw, v_sw_t, dt, dc, bsz, seq):
    nq = seq // QB
    ncp = seq // CMP_STRIDE
    nc = ncp - 1
    ns = seq // SEL_BLOCK
    assert ns <= LANE and seq >= WINDOW + QB and seq % KW == 0
    assert CMP_BLOCK == 2 * CMP_STRIDE and SEL_BLOCK == 4 * CMP_STRIDE
    expand =((np.arange(seq)[:, None] // SEL_BLOCK) == np.arange(LANE)[None, :]).astype(np.float32)
    ng = NSA_GROUPS
    ks_spec = pl.BlockSpec((1, seq, HEAD_DIM), lambda b, g, i: (g, b, 0))
    kw_spec = pl.BlockSpec((1, seq, HEAD_DIM), lambda b, g, i: (ng + g, b, 0))
    vs_spec = pl.BlockSpec((1, HEAD_DIM, seq), lambda b, g, i: (g, 0, b))
    vw_spec = pl.BlockSpec((1, HEAD_DIM, seq), lambda b, g, i: (ng + g, 0, b))
    lanes = NSA_HPG * QB
    return pl.pallas_call(
        functools.partial(_nsa_kernel, seq=seq, nc=nc),
        grid=(bsz, NSA_GROUPS, nq),
        in_specs=[pl.BlockSpec((NSA_HPG, HEAD_DIM, QB), lambda b, g, i: (g, 0, b * nq + i)),
                  pl.BlockSpec((1, 16, QB), lambda b, g, i: (g, 0, b * nq + i)),
                  pl.BlockSpec((1, ncp, HEAD_DIM), lambda b, g, i: (g, b, 0)),
                  pl.BlockSpec((1, HEAD_DIM, ncp), lambda b, g, i: (g, 0, b)),
                  ks_spec, vs_spec, kw_spec, vw_spec,
                  pl.BlockSpec((NSA_HPG, 3, LANE, LANE), lambda b, g, i: (g, 0, 0, 0)),
                  pl.BlockSpec((NSA_HPG, CWIN, LANE), lambda b, g, i: (g, 0, 0)),
                  pl.BlockSpec((seq, LANE), lambda b, g, i: (0, 0))],
        out_specs=pl.BlockSpec((QB, NSA_HPG * HEAD_DIM), lambda b, g, i: (b * nq + i, g)),
        out_shape=jax.ShapeDtypeStruct((bsz * seq, NSA_HEADS * HEAD_DIM), BF16),
        scratch_shapes=[pltpu.VMEM((ncp + CWIN, lanes), F32), pltpu.VMEM((ncp + 8, QB), F32),
                        pltpu.VMEM((1, lanes), F32), pltpu.VMEM((1, lanes), F32),
                        pltpu.VMEM((HEAD_DIM, lanes), F32),
                        pltpu.VMEM((KW, lanes), F32), pltpu.VMEM((KW, lanes), F32),
                        pltpu.VMEM((1, lanes), F32), pltpu.VMEM((1, lanes), F32)],
        compiler_params=_cparams(("arbitrary", "arbitrary", "arbitrary")),
        name="nsa_attention",
    )(q_t, gates_t, kc, vc_t, k_sw, v_sw_t, k_sw, v_sw_t, dt, dc, jnp.asarray(expand, BF16))


MLA_HPS = 2


def _mla_kernel(qt_ref, k_ref, vt_ref, o_ref, *scratch):
    qi = pl.program_id(2)
    chains = [scratch[3 * h:3 * h + 3] for h in range(MLA_HPS)]
    sbuf = [scratch[(3 + b) * MLA_HPS:(4 + b) * MLA_HPS] for b in range(2)]
    cbuf = [scratch[(5 + b) * MLA_HPS:(6 + b) * MLA_HPS] for b in range(2)]
    for ch in chains:
        _flash_init(*ch)

    def qk_stage(c, buf):
        c0 = pl.multiple_of(c * KW, KW)
        col_max = []
        for h in range(MLA_HPS):
            s = _dot(k_ref[h, pl.ds(c0, KW), :], qt_ref[h])
            sbuf[buf][h][...] = s
            col_max.append(jnp.max(s, axis=0, keepdims=True))
            cbuf[buf][h][...] = col_max[-1]
        return col_max

    def soft_stage(c, buf, ahead):
        c0 = pl.multiple_of(c * KW, KW)
        for h, ch in enumerate(chains):
            _flash_update(sbuf[buf][h][...], vt_ref[h, :, pl.ds(c0, KW)], *ch, col_max=cbuf[buf][h][...],
                          after=None if ahead is None else ahead[h])

    _pipelined_chunks(qi, qk_stage, soft_stage)
    c0 = pl.multiple_of(qi * KW, KW)
    kpos = lax.broadcasted_iota(jnp.int32, (KW, KW), 0)
    tq = lax.broadcasted_iota(jnp.int32, (KW, KW), 1)
    causal = jnp.where(kpos <= tq, 0.0, NEG)
    dv = vt_ref.shape[1]
    scores = [_dot(k_ref[h, pl.ds(c0, KW), :], qt_ref[h]) + causal for h in range(MLA_HPS)]
    for h, ch in enumerate(chains):
        _flash_update(scores[h], vt_ref[h, :, pl.ds(c0, KW)], *ch)
        o_ref[:, h * dv:(h + 1) * dv] = _flash_result(*ch).T.astype(o_ref.dtype)


def mla_attention(q_t, k, v_t, bsz, seq):
    nh, dqk, _ = q_t.shape
    dv = v_t.shape[1]
    nq = seq // KW
    hps = MLA_HPS
    state = [pltpu.VMEM((1, KW), F32), pltpu.VMEM((1, KW), F32), pltpu.VMEM((dv, KW), F32)] * hps
    state += [pltpu.VMEM((KW, KW), F32)] * (2 * hps)
    state += [pltpu.VMEM((1, KW), F32)] * (2 * hps)
    return pl.pallas_call(
        _mla_kernel,
        grid=(bsz, nh // hps, nq),
        in_specs=[pl.BlockSpec((hps, dqk, KW), lambda b, h, i: (h, 0, b * nq + i)),
                  pl.BlockSpec((hps, seq, dqk), lambda b, h, i: (h, b, 0)),
                  pl.BlockSpec((hps, dv, seq), lambda b, h, i: (h, 0, b))],
        out_specs=pl.BlockSpec((KW, hps * dv), lambda b, h, i: (b * nq + i, h)),
        out_shape=jax.ShapeDtypeStruct((bsz * seq, nh * dv), BF16),
        scratch_shapes=state,
        compiler_params=_cparams(("arbitrary", "arbitrary", "arbitrary")),
        name="mla_attention",
    )(q_t, k, v_t)


INT_MIN = -2 ** 31
NEG_KEY = int(np.array(NEG, np.float32).view(np.int32)) ^ 0x7FFFFFFF


def _sort_key(x):
    bits = pltpu.bitcast(x + 0.0, jnp.int32)
    return jnp.where(bits < 0, bits ^ 0x7FFFFFFF, bits)


def _dsa_kernel(iqt_ref, iwt_ref, ik_ref, qt_ref, k_ref, vt_ref, dt_ref, o_ref,
                key_ref, madd_ref, *state, seq, k_sel):
    qi = pl.program_id(1)
    q0 = qi * QB
    n_chunk = (q0 + QB + KW - 1) // KW
    n_rest = seq - n_chunk * KW
    kpos = lax.broadcasted_iota(jnp.int32, (KW, QB), 0)
    tq = q0 + lax.broadcasted_iota(jnp.int32, (KW, QB), 1)
    hpp = KW // QB

    def score_chunk(c, x):
        c0 = pl.multiple_of(c * KW, KW)
        ikc = ik_ref[pl.ds(c0, KW), :]
        acc = jnp.zeros((KW, QB), F32)
        for piece in range(IDX_HEADS // hpp):
            sl = slice(piece * KW, (piece + 1) * KW)
            s = jnp.maximum(_dot(ikc, iqt_ref[0, :, sl]), 0.0) * iwt_ref[0, :, sl]
            for j in range(hpp):
                acc = acc + s[:, j * QB:(j + 1) * QB]
        acc = jnp.where(c0 + kpos <= tq, acc, NEG)
        key_ref[pl.ds(c0, KW), :] = _sort_key(acc)
        return x

    lax.fori_loop(0, n_chunk, score_chunk, 0)

    def count(pred):
        def body(c, acc):
            c0 = pl.multiple_of(c * KW, KW)
            hit = jnp.where(pred(key_ref[pl.ds(c0, KW), :], c0), 1.0, 0.0)
            parts = [hit[8 * i:8 * (i + 1)] for i in range(KW // 8)]
            while len(parts) > 1:
                parts = [parts[i] + parts[i + 1] for i in range(0, len(parts), 2)]
            return acc + parts[0]
        acc = lax.fori_loop(0, n_chunk, body, jnp.zeros((8, QB), F32))
        return jnp.sum(acc, axis=0, keepdims=True)

    rest = n_rest.astype(F32)
    kf = float(k_sel)

    def bit_cond(st):
        i, _, _, settled = st
        return (i < 32) & (jnp.min(settled) < 0.5)

    def bit_step(st):
        i, u, thr_s, settled = st
        bit = jnp.left_shift(jnp.int32(1), 31 - i)
        trial = (u | bit) ^ INT_MIN
        cnt = count(lambda keys, c0: keys >= trial) + jnp.where(NEG_KEY >= trial, rest, 0.0)
        new = (cnt == kf) & (settled < 0.5)
        return (i + 1, jnp.where(cnt >= kf, u | bit, u), jnp.where(new, trial, thr_s),
                jnp.where(new, 1.0, settled))

    _, u, thr_s, settled = lax.while_loop(
        bit_cond, bit_step,
        (jnp.int32(0), jnp.zeros((1, QB), jnp.int32), jnp.zeros((1, QB), jnp.int32), jnp.zeros((1, QB), F32)))
    is_settled = settled > 0.5
    thr = jnp.where(is_settled, thr_s, u ^ INT_MIN)
    def edge_counts():
        return (count(lambda keys, c0: keys > thr) + jnp.where(NEG_KEY > thr, rest, 0.0),
                count(lambda keys, c0: keys >= thr) + jnp.where(NEG_KEY >= thr, rest, 0.0))

    zero_cnt = jnp.zeros((1, QB), F32)
    cnt_gt, cnt_ge = lax.cond(jnp.min(settled) > 0.5, lambda: (zero_cnt, zero_cnt), edge_counts)
    need = kf - cnt_gt
    tie_q = (cnt_ge > kf) & (thr != NEG_KEY) & jnp.logical_not(is_settled)
    idx_bits = (seq - 1).bit_length()
    no_cut = 2 ** 30

    def tie_cut():
        def idx_step(i, x):
            bit = jnp.left_shift(jnp.int32(1), idx_bits - 1 - i)
            trial = x | bit
            f = count(lambda keys, c0: (keys == thr) & (c0 + kpos < trial))
            return jnp.where(f <= need - 1.0, trial, x)
        return lax.fori_loop(0, idx_bits, idx_step, jnp.zeros((1, QB), jnp.int32))

    any_tie = jnp.max(jnp.where(tie_q, 1.0, 0.0)) > 0.0
    x_cut = lax.cond(any_tie, tie_cut, lambda: jnp.full((1, QB), no_cut, jnp.int32))
    x_cut = jnp.where(tie_q, x_cut, no_cut)

    def mask_chunk(c, x):
        c0 = pl.multiple_of(c * KW, KW)
        keys = key_ref[pl.ds(c0, KW), :]
        pos = c0 + kpos
        chosen = (keys > thr) | ((keys == thr) & (pos <= x_cut))
        madd_ref[pl.ds(c0, KW), :] = jnp.where(chosen & (pos <= tq), 0.0, NEG)
        return x

    lax.fori_loop(0, n_chunk, mask_chunk, 0)

    c_near = jnp.maximum(qi - 1, 0) // (KW // LANE)
    ng = DSA_KV_HEADS
    chains = [state[3 * g:3 * g + 3] for g in range(ng)]
    sbuf = [state[(3 + b) * ng:(4 + b) * ng] for b in range(2)]
    cbuf = [state[(5 + b) * ng:(6 + b) * ng] for b in range(2)]
    group_heads = [[g * DSA_HPG + h for h in range(DSA_HPG)] for g in range(ng)]
    for ch in chains:
        _flash_init(*ch)

    def raw_scores(c0, g):
        q_t = jnp.concatenate([qt_ref[h] for h in group_heads[g]], axis=1)
        return _dot(k_ref[g, pl.ds(c0, KW), :], q_t)

    def qk_stage(c, buf):
        c0 = pl.multiple_of(c * KW, KW)
        madd = _tile_lanes(madd_ref[pl.ds(c0, KW), :], DSA_HPG)
        col_max = []
        for g in range(ng):
            s = raw_scores(c0, g) + madd
            sbuf[buf][g][...] = s
            col_max.append(jnp.max(s, axis=0, keepdims=True))
            cbuf[buf][g][...] = col_max[-1]
        return col_max

    def soft_stage(c, buf, ahead):
        c0 = pl.multiple_of(c * KW, KW)
        for g, ch in enumerate(chains):
            _flash_update(sbuf[buf][g][...], vt_ref[g, :, pl.ds(c0, KW)], *ch, col_max=cbuf[buf][g][...],
                          after=None if ahead is None else ahead[g])

    _pipelined_chunks(c_near, qk_stage, soft_stage)

    def near_step(c, x):
        c0 = pl.multiple_of(c * KW, KW)
        madd = _tile_lanes(madd_ref[pl.ds(c0, KW), :], DSA_HPG)
        scores = [raw_scores(c0, g) + madd + _near_bias(dt_ref, group_heads[g], qi, c * (KW // LANE), KW // LANE)
                  for g in range(ng)]
        for g, ch in enumerate(chains):
            _flash_update(scores[g], vt_ref[g, :, pl.ds(c0, KW)], *ch)
        return x

    lax.fori_loop(c_near, n_chunk, near_step, 0)
    for g, ch in enumerate(chains):
        o_t = _flash_result(*ch)
        for h in range(DSA_HPG):
            hh = group_heads[g][h]
            o_ref[:, hh * HEAD_DIM:(hh + 1) * HEAD_DIM] = o_t[:, h * QB:(h + 1) * QB].T.astype(o_ref.dtype)


def _idx_prep_kernel(p_ref, cos_ref, sin_ref, iqt_ref, ik_ref, iwt_ref, *, ntile):
    nslab_q = IDX_HEADS * IDX_DIM // LANE
    per = LANE // IDX_DIM
    zpad = jnp.zeros((QB, LANE - IDX_DIM), F32)

    def rope_part(x, cos, sin):
        return jnp.concatenate([_rope_rows(x[:, :IDX_ROPE], cos, sin), x[:, IDX_ROPE:]], axis=1)

    for t in range(ntile):
        rows = slice(t * QB, (t + 1) * QB)
        cos, sin = cos_ref[rows, :], sin_ref[rows, :]
        cols = []
        for s in range(nslab_q):
            slab = p_ref[s, rows, :]
            for j in range(per):
                x = rope_part(slab[:, j * IDX_DIM:(j + 1) * IDX_DIM], cos, sin) * IDX_DIM ** -0.5
                cols.append(jnp.concatenate([x, zpad], axis=1).T)
        iqt_ref[t] = jnp.concatenate(cols, axis=1).astype(iqt_ref.dtype)
        tail = p_ref[nslab_q, rows, :]
        ik = rope_part(tail[:, :IDX_DIM], cos, sin)
        ik_ref[rows, :] = jnp.concatenate([ik, zpad], axis=1).astype(ik_ref.dtype)
        w_t = (tail * IDX_HEADS ** -0.5).T
        iwt_ref[t] = jnp.concatenate([w_t[IDX_DIM + h:IDX_DIM + h + 1, :] for h in range(IDX_HEADS)], axis=1)


def indexer_operands(proj, seq, tm=512):
    _, m, _ = proj.shape
    ntile = tm // QB
    tps = seq // tm
    half = IDX_ROPE // 2
    cos, sin = _rope_tables(seq, IDX_ROPE)
    lanes = IDX_HEADS * QB
    return pl.pallas_call(
        functools.partial(_idx_prep_kernel, ntile=ntile),
        grid=(m // tm,),
        in_specs=[pl.BlockSpec((proj.shape[0], tm, LANE), lambda i: (0, i, 0)),
                  pl.BlockSpec((tm, half), lambda i: (i % tps, 0)),
                  pl.BlockSpec((tm, half), lambda i: (i % tps, 0))],
        out_specs=[pl.BlockSpec((ntile, LANE, lanes), lambda i: (i, 0, 0)),
                   pl.BlockSpec((tm, LANE), lambda i: (i, 0)),
                   pl.BlockSpec((ntile, 1, lanes), lambda i: (i, 0, 0))],
        out_shape=[jax.ShapeDtypeStruct((m // QB, LANE, lanes), BF16),
                   jax.ShapeDtypeStruct((m, LANE), BF16),
                   jax.ShapeDtypeStruct((m // QB, 1, lanes), F32)],
        compiler_params=_cparams(("arbitrary",)),
        name="dsa_indexer_operands",
    )(proj, cos, sin)


def dsa_attention(iq_t, iw_t, ik, q_t, k, v_t, dt, bsz, seq):
    nq = seq // QB
    k_sel = min(DSA_TOPK_MAX, seq // 4)
    assert seq % KW == 0
    lanes = DSA_HPG * QB
    return pl.pallas_call(
        functools.partial(_dsa_kernel, seq=seq, k_sel=k_sel),
        grid=(bsz, nq),
        in_specs=[pl.BlockSpec((1, LANE, IDX_HEADS * QB), lambda b, i: (b * nq + i, 0, 0)),
                  pl.BlockSpec((1, 1, IDX_HEADS * QB), lambda b, i: (b * nq + i, 0, 0)),
                  pl.BlockSpec((seq, LANE), lambda b, i: (b, 0)),
                  pl.BlockSpec((DSA_HEADS, HEAD_DIM, QB), lambda b, i: (0, 0, b * nq + i)),
                  pl.BlockSpec((DSA_KV_HEADS, seq, HEAD_DIM), lambda b, i: (0, b, 0),
                               pipeline_mode=pl.Buffered(1)),
                  pl.BlockSpec((DSA_KV_HEADS, HEAD_DIM, seq), lambda b, i: (0, 0, b),
                               pipeline_mode=pl.Buffered(1)),
                  pl.BlockSpec((DSA_HEADS, 3, LANE, LANE), lambda b, i: (0, 0, 0, 0),
                               pipeline_mode=pl.Buffered(1))],
        out_specs=pl.BlockSpec((QB, DSA_HEADS * HEAD_DIM), lambda b, i: (b * nq + i, 0)),
        out_shape=jax.ShapeDtypeStruct((bsz * seq, DSA_HEADS * HEAD_DIM), BF16),
        scratch_shapes=[pltpu.VMEM((seq, QB), jnp.int32), pltpu.VMEM((seq, QB), F32)]
        + [pltpu.VMEM((1, lanes), F32), pltpu.VMEM((1, lanes), F32),
           pltpu.VMEM((HEAD_DIM, lanes), F32)] * DSA_KV_HEADS
        + [pltpu.VMEM((KW, lanes), F32)] * (2 * DSA_KV_HEADS)
        + [pltpu.VMEM((1, lanes), F32)] * (2 * DSA_KV_HEADS),
        compiler_params=_cparams(("arbitrary", "arbitrary")),
        name="dsa_attention",
    )(iq_t, iw_t, ik, q_t, k, v_t, dt)


def _rope_tables(seq, dim):
    half = dim // 2
    inv = ROPE_THETA ** (-jnp.arange(half, dtype=F32) / half)
    ang = jnp.arange(seq, dtype=F32)[:, None] * inv[None, :]
    return jnp.cos(ang), jnp.sin(ang)


def _pad_cols(w, n):
    return jnp.pad(w, ((0, 0), (0, n - w.shape[1])))


def _t(x):
    return jnp.swapaxes(x, -1, -2)


def _even_mixer(h, x2, gate, dt, dc, bsz, seq, w_in, w_out, nsa_qk_g, cmp_pe, cmp_w1, cmp_b1,
                cmp_w2, cmp_b2, q_norm_g, kv_norm_g, w_uq, w_ukv, nope_g, rope_g):
    m = bsz * seq
    nq_cols = NSA_HEADS * HEAD_DIM
    nkv_cols = 6 * NSA_GROUPS * HEAD_DIM
    ngate = 3 * NSA_HEADS
    o_gate = nq_cols + nkv_cols
    o_cq = o_gate + ngate
    o_ckv = o_cq + MLA_Q_RANK
    o_kpe = o_ckv + MLA_KV_RANK
    gw = NSA_GROUPS * HEAD_DIM
    kvw = [w_in[:, nq_cols + i * gw:nq_cols + (i + 1) * gw] for i in range(6)]
    scale = HEAD_DIM ** -0.5 * LOG2E
    q_t = proj_heads(h, w_in[:, :nq_cols].astype(BF16), nsa_qk_g[0] * scale, transpose=True)
    k_sw = proj_heads(h, jnp.concatenate([kvw[2], kvw[4]], axis=1).astype(BF16), nsa_qk_g[1],
                      transpose=False)
    v_sw_t = proj_heads(h, jnp.concatenate([kvw[3], kvw[5]], axis=1).astype(BF16), transpose=True)
    tail = jnp.concatenate([w_in[:, o_kpe:], w_in[:, o_gate:o_cq]], axis=1)
    w_r = jnp.concatenate([kvw[0], kvw[1], w_in[:, o_cq:o_kpe], _pad_cols(tail, LANE)], axis=1).astype(BF16)
    proj = proj_slabs(h, w_r, tn=w_r.shape[1])
    s_cq = 2 * NSA_GROUPS
    s_ckv = s_cq + MLA_Q_RANK // LANE
    s_tail = s_ckv + MLA_KV_RANK // LANE
    kvc = compress_kv(proj, 0, bsz, seq, cmp_pe, cmp_w1, cmp_b1, cmp_w2, cmp_b2, nsa_qk_g[1])
    tail_v = proj[s_tail]
    gates = tail_v[:, MLA_ROPE:MLA_ROPE + ngate].reshape(m, NSA_GROUPS, 3 * NSA_HPG)
    gates_t = jnp.pad(jnp.transpose(gates, (1, 2, 0)), ((0, 0), (0, 16 - 3 * NSA_HPG), (0, 0)))
    o_nsa = nsa_attention(q_t, gates_t, kvc[0], _t(kvc[1]), k_sw, v_sw_t,
                          dt[:NSA_HEADS], dc[:NSA_HEADS], bsz, seq)

    dq = MLA_NOPE + MLA_ROPE
    wq = w_uq.reshape(MLA_Q_RANK, MLA_HEADS, dq)
    wq_r = jnp.concatenate([wq[:, :, :MLA_NOPE].reshape(MLA_Q_RANK, -1),
                            wq[:, :, MLA_NOPE:].reshape(MLA_Q_RANK, -1)], axis=1).astype(BF16)
    cos, sin = _rope_tables(seq, MLA_ROPE)
    q_mla_t, k_mla, v_mla_t = mla_project(proj, s_cq, s_ckv, s_tail, seq, q_norm_g, kv_norm_g, wq_r,
                                          w_ukv.astype(BF16), nope_g, rope_g, cos, sin, dq ** -0.5 * LOG2E)
    o_mla = mla_attention(q_mla_t, k_mla, v_mla_t, bsz, seq)
    w_o = w_out.astype(BF16)
    return resproj([(o_nsa, w_o[:nq_cols]), (o_mla, w_o[nq_cols:])], x2, gate, seq)


def _odd_mixer(h, x2, gate, dt, bsz, seq, w_in, w_out, qk_g):
    nq = DSA_HEADS * HEAD_DIM
    nkv = DSA_KV_HEADS * HEAD_DIM
    niq = IDX_HEADS * IDX_DIM
    o_k, o_v, o_iq = nq, nq + nkv, nq + 2 * nkv
    q_t = proj_heads(h, w_in[:, :o_k].astype(BF16), qk_g[0] * (HEAD_DIM ** -0.5 * LOG2E), transpose=True)
    k = proj_heads(h, w_in[:, o_k:o_v].astype(BF16), qk_g[1], transpose=False)
    v_t = proj_heads(h, w_in[:, o_v:o_iq].astype(BF16), transpose=True)
    w_idx = w_in[:, o_iq:]
    proj = proj_slabs(h, _pad_cols(w_idx, niq + LANE).astype(BF16), tn=niq + LANE)
    iq_t, ik, iw_t = indexer_operands(proj, seq)
    o = dsa_attention(iq_t, iw_t, ik, q_t, k, v_t, dt, bsz, seq)
    return resproj([(o, w_out.astype(BF16))], x2, gate, seq)


def _conv_ffn(h, x2, gate, seq, w_up, conv_w, conv_b, w_down):
    a = ffn_up(h, w_up.astype(BF16), conv_w, conv_b, seq)
    return resproj([(a, w_down.astype(BF16))], x2, gate, seq)


def kernel(x, c, rel_bias, ada_w, ada_b, norm_g, ev_w_in, ev_w_out, nsa_qk_g, cmp_pe, cmp_w1, cmp_b1, cmp_w2, cmp_b2, mla_q_norm_g, mla_kv_norm_g, mla_w_uq, mla_w_ukv, mla_nope_g, mla_rope_g, od_w_in, od_w_out, dsa_qk_g, ffn_w_up, ffn_conv_w, ffn_conv_b, ffn_w_down):
    bsz, seq, d = x.shape
    depth = ada_w.shape[0]
    x2 = x.reshape(bsz * seq, d)
    mods = ada_all(c, ada_w, ada_b)
    dt, dc = bias_tiles(rel_bias)
    for i in range(depth):
        j = i // 2
        shift, scale, gate = jnp.split(mods[i, 0], 3, axis=-1)
        h = modnorm(x2, norm_g[i, 0], scale, shift, seq)
        if i % 2 == 0:
            x2 = _even_mixer(h, x2, gate, dt, dc, bsz, seq, ev_w_in[j], ev_w_out[j], nsa_qk_g[j],
                             cmp_pe[j], cmp_w1[j], cmp_b1[j], cmp_w2[j], cmp_b2[j], mla_q_norm_g[j],
                             mla_kv_norm_g[j], mla_w_uq[j], mla_w_ukv[j], mla_nope_g[j], mla_rope_g[j])
        else:
            x2 = _odd_mixer(h, x2, gate, dt, bsz, seq, od_w_in[j], od_w_out[j], dsa_qk_g[j])
        shift, scale, gate = jnp.split(mods[i, 1], 3, axis=-1)
        h = modnorm(x2, norm_g[i, 1], scale, shift, seq)
        x2 = _conv_ffn(h, x2, gate, seq, ffn_w_up[i], ffn_conv_w[i], ffn_conv_b[i], ffn_w_down[i])
    return x2.reshape(bsz, seq, d)
```

```python
import functools
import math

import numpy as np
import jax
import jax.numpy as jnp
from jax import lax
from jax.experimental import pallas as pl
from jax.experimental.pallas import tpu as pltpu

HEAD_DIM = 128
NSA_HEADS = 8
NSA_GROUPS = 2
NSA_HPG = NSA_HEADS // NSA_GROUPS
CMP_BLOCK = 32
CMP_STRIDE = 16
CMP_HIDDEN = 256
SEL_BLOCK = 64
SEL_TOP_N = 16
WINDOW = 512
MLA_HEADS = 8
MLA_Q_RANK = 512
MLA_KV_RANK = 256
MLA_NOPE = 128
MLA_ROPE = 64
MLA_V = 128
DSA_HEADS = 16
DSA_KV_HEADS = 4
DSA_HPG = DSA_HEADS // DSA_KV_HEADS
IDX_HEADS = 16
IDX_DIM = 64
IDX_ROPE = 32
DSA_TOPK_MAX = 256
REL_BUCKETS = 32
REL_MAX_DIST = 128
CONV_WIDTH = 3
ROPE_THETA = 10000.0
EPS = 1e-6
NEG = -1e30
FORCE = 1e9

LANE = 128
QB = 128
VMEM_LIMIT = 56 * 1024 * 1024

F32 = jnp.float32
BF16 = jnp.bfloat16


def _t5_thresholds():
    d = np.arange(0, 4 * REL_MAX_DIST)
    half = REL_BUCKETS // 2
    val = np.log(np.maximum(d, 1) / half) / math.log(REL_MAX_DIST / half) * (REL_BUCKETS - half)
    large = np.minimum(half + np.floor(np.maximum(val, 0.0)).astype(np.int64), REL_BUCKETS - 1)
    bucket = np.where(d < half, d, large)
    return [int(np.argmax(bucket >= b)) for b in range(1, REL_BUCKETS)]


T5_THR = _t5_thresholds()
T5_FAR = T5_THR[-1]
assert T5_FAR <= LANE


def _cparams(sem):
    return pltpu.CompilerParams(dimension_semantics=sem, vmem_limit_bytes=VMEM_LIMIT)


def _dot(a, b):
    return jnp.dot(a, b, preferred_element_type=F32)


def _ada_kernel(c_ref, w_ref, b_ref, o_ref):
    c = c_ref[...]
    a = c * jax.nn.sigmoid(c)
    o_ref[0] = jnp.dot(a, w_ref[0], preferred_element_type=F32,
                       precision=lax.Precision.HIGHEST) + b_ref[0]


def ada_all(c, ada_w, ada_b):
    depth, two, d, n3 = ada_w.shape
    bsz = c.shape[0]
    rows = 8
    cp = jnp.zeros((rows, d), F32).at[:bsz].set(c)
    w = ada_w.reshape(depth * two, d, n3)
    b = ada_b.reshape(depth * two, 1, n3)
    tn = 512
    out = pl.pallas_call(
        _ada_kernel,
        grid=(depth * two, n3 // tn),
        in_specs=[pl.BlockSpec((rows, d), lambda l, j: (0, 0)),
                  pl.BlockSpec((1, d, tn), lambda l, j: (l, 0, j)),
                  pl.BlockSpec((1, 1, tn), lambda l, j: (l, 0, j))],
        out_specs=pl.BlockSpec((1, rows, tn), lambda l, j: (l, 0, j)),
        out_shape=jax.ShapeDtypeStruct((depth * two, rows, n3), F32),
        compiler_params=_cparams(("arbitrary", "arbitrary")),
        name="ada_mod",
    )(cp, w, b)
    return out[:, :bsz].reshape(depth, two, bsz, n3)


def _modnorm_kernel(x_ref, g_ref, sc_ref, sh_ref, o_ref):
    x = x_ref[...]
    y = x * lax.rsqrt(jnp.mean(x * x, axis=-1, keepdims=True) + EPS)
    h = (y * g_ref[...]) * (1.0 + sc_ref[0]) + sh_ref[0]
    o_ref[...] = h.astype(o_ref.dtype)


def modnorm(x2, g, scale, shift, seq):
    m, d = x2.shape
    tm = 512
    tpb = seq // tm
    return pl.pallas_call(
        _modnorm_kernel,
        grid=(m // tm,),
        in_specs=[pl.BlockSpec((tm, d), lambda i: (i, 0)),
                  pl.BlockSpec((1, d), lambda i: (0, 0)),
                  pl.BlockSpec((1, 1, d), lambda i: (i // tpb, 0, 0)),
                  pl.BlockSpec((1, 1, d), lambda i: (i // tpb, 0, 0))],
        out_specs=pl.BlockSpec((tm, d), lambda i: (i, 0)),
        out_shape=jax.ShapeDtypeStruct((m, d), BF16),
        compiler_params=_cparams(("arbitrary",)),
        name="modnorm",
    )(x2, g.reshape(1, d), scale.reshape(-1, 1, d), shift.reshape(-1, 1, d))


def _proj_kernel(x_ref, w_ref, o_ref, *, nslab):
    acc = _dot(x_ref[...], w_ref[...])
    for s in range(nslab):
        o_ref[s] = acc[:, s * LANE:(s + 1) * LANE]


def proj_slabs(x, w, tm=1024, tn=384):
    m, k = x.shape
    n = w.shape[1]
    assert n % tn == 0 and m % tm == 0
    nslab = tn // LANE
    return pl.pallas_call(
        functools.partial(_proj_kernel, nslab=nslab),
        grid=(m // tm, n // tn),
        in_specs=[pl.BlockSpec((tm, k), lambda i, j: (i, 0)),
                  pl.BlockSpec((k, tn), lambda i, j: (0, j))],
        out_specs=pl.BlockSpec((nslab, tm, LANE), lambda i, j: (j, i, 0)),
        out_shape=jax.ShapeDtypeStruct((n // LANE, m, LANE), F32),
        compiler_params=_cparams(("arbitrary", "arbitrary")),
        name="proj_slabs",
    )(x, w)


def _proj_heads_kernel(x_ref, w_ref, g_ref, o_ref, *, nslab, norm, transpose):
    acc = _dot(x_ref[...], w_ref[...])
    for s in range(nslab):
        y = acc[:, s * LANE:(s + 1) * LANE]
        if norm:
            y = y * lax.rsqrt(jnp.mean(y * y, axis=-1, keepdims=True) + EPS) * g_ref[...]
        o_ref[s] = (y.T if transpose else y).astype(o_ref.dtype)


def proj_heads(x, w, g=None, *, transpose, tm=1024, tn=1024):
    m, k = x.shape
    n = w.shape[1]
    tn = min(tn, n)
    assert n % tn == 0 and m % tm == 0
    nslab = tn // LANE
    norm = g is not None
    if transpose:
        out_spec = pl.BlockSpec((nslab, LANE, tm), lambda i, j: (j, 0, i))
        out_shape = jax.ShapeDtypeStruct((n // LANE, LANE, m), BF16)
    else:
        out_spec = pl.BlockSpec((nslab, tm, LANE), lambda i, j: (j, i, 0))
        out_shape = jax.ShapeDtypeStruct((n // LANE, m, LANE), BF16)
    g2 = (g if norm else jnp.ones((LANE,), F32)).reshape(1, LANE)
    return pl.pallas_call(
        functools.partial(_proj_heads_kernel, nslab=nslab, norm=norm, transpose=transpose),
        grid=(m // tm, n // tn),
        in_specs=[pl.BlockSpec((tm, k), lambda i, j: (i, 0)),
                  pl.BlockSpec((k, tn), lambda i, j: (0, j)),
                  pl.BlockSpec((1, LANE), lambda i, j: (0, 0))],
        out_specs=out_spec,
        out_shape=out_shape,
        compiler_params=_cparams(("arbitrary", "arbitrary")),
        name="proj_heads",
    )(x, w, g2)


def _rms_rows(x, g):
    return x * lax.rsqrt(jnp.mean(x * x, axis=-1, keepdims=True) + EPS) * g


def _rope_rows(x, cos, sin):
    half = x.shape[-1] // 2
    x1, x2 = x[:, :half], x[:, half:]
    return jnp.concatenate([x1 * cos - x2 * sin, x1 * sin + x2 * cos], axis=1)


def _latent(x_ref, g_ref):
    x = jnp.concatenate([x_ref[s] for s in range(x_ref.shape[0])], axis=1)
    return _rms_rows(x, g_ref[...]).astype(BF16)


def _mla_q_kernel(x_ref, g_ref, w_ref, gn_ref, gr_ref, cos_ref, sin_ref, o_ref, *, scale):
    acc = _dot(_latent(x_ref, g_ref), w_ref[...])
    tm = acc.shape[0]
    cos, sin = cos_ref[...], sin_ref[...]
    pad = jnp.zeros((tm, LANE - MLA_ROPE), F32)
    for h in range(MLA_HEADS):
        nope = _rms_rows(acc[:, h * MLA_NOPE:(h + 1) * MLA_NOPE], gn_ref[...]) * scale
        r0 = MLA_HEADS * MLA_NOPE + h * MLA_ROPE
        pe = _rope_rows(_rms_rows(acc[:, r0:r0 + MLA_ROPE], gr_ref[...]), cos, sin) * scale
        o_ref[h, 0:MLA_NOPE, :] = nope.T.astype(o_ref.dtype)
        o_ref[h, MLA_NOPE:MLA_NOPE + LANE, :] = jnp.concatenate([pe, pad], axis=1).T.astype(o_ref.dtype)


def _mla_kv_kernel(x_ref, g_ref, w_ref, tail_ref, gn_ref, gr_ref, cos_ref, sin_ref, ok_ref, ov_ref):
    acc = _dot(_latent(x_ref, g_ref), w_ref[...])
    tm = acc.shape[0]
    k_pe = _rope_rows(_rms_rows(tail_ref[0][:, :MLA_ROPE], gr_ref[...]), cos_ref[...], sin_ref[...])
    k_pe = jnp.concatenate([k_pe, jnp.zeros((tm, LANE - MLA_ROPE), F32)], axis=1).astype(ok_ref.dtype)
    for h in range(MLA_HEADS):
        c0 = h * (MLA_NOPE + MLA_V)
        ok_ref[h, :, 0:MLA_NOPE] = _rms_rows(acc[:, c0:c0 + MLA_NOPE], gn_ref[...]).astype(ok_ref.dtype)
        ok_ref[h, :, MLA_NOPE:MLA_NOPE + LANE] = k_pe
        ov_ref[h] = acc[:, c0 + MLA_NOPE:c0 + MLA_NOPE + MLA_V].T.astype(ov_ref.dtype)


def mla_project(proj, s_cq, s_ckv, s_tail, seq, q_norm_g, kv_norm_g, wq_r, w_ukv, nope_g, rope_g, cos, sin,
                scale, tm=512):
    _, m, _ = proj.shape
    kq, kkv = s_ckv - s_cq, s_tail - s_ckv
    tps = seq // tm
    dqk = MLA_NOPE + LANE
    half = MLA_ROPE // 2
    rope_specs = [pl.BlockSpec((tm, half), lambda i: (i % tps, 0))] * 2
    gain_specs = [pl.BlockSpec((1, MLA_NOPE), lambda i: (0, 0)), pl.BlockSpec((1, MLA_ROPE), lambda i: (0, 0))]
    q_t = pl.pallas_call(
        functools.partial(_mla_q_kernel, scale=scale),
        grid=(m // tm,),
        in_specs=[pl.BlockSpec((kq, tm, LANE), lambda i: (s_cq // kq, i, 0)),
                  pl.BlockSpec((1, kq * LANE), lambda i: (0, 0)),
                  pl.BlockSpec(wq_r.shape, lambda i: (0, 0))] + gain_specs + rope_specs,
        out_specs=pl.BlockSpec((MLA_HEADS, dqk, tm), lambda i: (0, 0, i)),
        out_shape=jax.ShapeDtypeStruct((MLA_HEADS, dqk, m), BF16),
        compiler_params=_cparams(("arbitrary",)),
        name="mla_q_project",
    )(proj, q_norm_g.reshape(1, -1), wq_r, nope_g[0].reshape(1, -1), rope_g[0].reshape(1, -1), cos, sin)
    k, v_t = pl.pallas_call(
        _mla_kv_kernel,
        grid=(m // tm,),
        in_specs=[pl.BlockSpec((kkv, tm, LANE), lambda i: (s_ckv // kkv, i, 0)),
                  pl.BlockSpec((1, kkv * LANE), lambda i: (0, 0)),
                  pl.BlockSpec(w_ukv.shape, lambda i: (0, 0)),
                  pl.BlockSpec((1, tm, LANE), lambda i: (s_tail, i, 0))] + gain_specs + rope_specs,
        out_specs=[pl.BlockSpec((MLA_HEADS, tm, dqk), lambda i: (0, i, 0)),
                   pl.BlockSpec((MLA_HEADS, MLA_V, tm), lambda i: (0, 0, i))],
        out_shape=[jax.ShapeDtypeStruct((MLA_HEADS, m, dqk), BF16),
                   jax.ShapeDtypeStruct((MLA_HEADS, MLA_V, m), BF16)],
        compiler_params=_cparams(("arbitrary",)),
        name="mla_kv_project",
    )(proj, kv_norm_g.reshape(1, -1), w_ukv, proj, nope_g[1].reshape(1, -1), rope_g[1].reshape(1, -1), cos, sin)
    return q_t, k, v_t


def _resproj_kernel(*refs, npair):
    xres_ref, gate_ref = refs[2 * npair], refs[2 * npair + 1]
    o_ref = refs[2 * npair + 2]
    acc = _dot(refs[0][...], refs[1][...])
    for p in range(1, npair):
        acc = acc + _dot(refs[2 * p][...], refs[2 * p + 1][...])
    o_ref[...] = xres_ref[...] + gate_ref[0] * acc


def resproj(pairs, xres, gate, seq, tm=1024, tn=512):
    m, n = xres.shape
    tpb = seq // tm
    in_specs, args = [], []
    for x, w in pairs:
        k = x.shape[1]
        in_specs += [pl.BlockSpec((tm, k), lambda i, j: (i, 0)),
                     pl.BlockSpec((k, tn), lambda i, j: (0, j))]
        args += [x, w]
    in_specs += [pl.BlockSpec((tm, tn), lambda i, j: (i, j)),
                 pl.BlockSpec((1, 1, tn), lambda i, j: (i // tpb, 0, j))]
    args += [xres, gate.reshape(-1, 1, n)]
    return pl.pallas_call(
        functools.partial(_resproj_kernel, npair=len(pairs)),
        grid=(m // tm, n // tn),
        in_specs=in_specs,
        out_specs=pl.BlockSpec((tm, tn), lambda i, j: (i, j)),
        out_shape=jax.ShapeDtypeStruct((m, n), F32),
        compiler_params=_cparams(("arbitrary", "arbitrary")),
        name="resproj",
    )(*args)


HALO = 8


def _ffn_up_kernel(h_ref, wg_ref, wv_ref, cwg_ref, cwv_ref, cbg_ref, cbv_ref, o_ref,
                   ug_ref, uv_ref, *, tm, tiles_per_seq):
    i = pl.program_id(1)
    first = (i % tiles_per_seq) == 0

    @pl.when(first)
    def _():
        ug_ref[0:HALO, :] = jnp.zeros((HALO, ug_ref.shape[1]), F32)
        uv_ref[0:HALO, :] = jnp.zeros((HALO, uv_ref.shape[1]), F32)

    @pl.when(jnp.logical_not(first))
    def _():
        ug_ref[0:HALO, :] = ug_ref[tm:tm + HALO, :]
        uv_ref[0:HALO, :] = uv_ref[tm:tm + HALO, :]

    h = h_ref[...]
    ug_ref[HALO:HALO + tm, :] = _dot(h, wg_ref[...])
    uv_ref[HALO:HALO + tm, :] = _dot(h, wv_ref[...])

    def conv(u_ref, cw_ref, cb_ref):
        out = cb_ref[...]
        for j in range(CONV_WIDTH):
            off = HALO - (CONV_WIDTH - 1) + j
            out = out + cw_ref[j:j + 1, :] * u_ref[off:off + tm, :]
        return out

    g = conv(ug_ref, cwg_ref, cbg_ref)
    v = conv(uv_ref, cwv_ref, cbv_ref)
    o_ref[...] = (g * jax.nn.sigmoid(g) * v).astype(o_ref.dtype)


def ffn_up(h, w_up, conv_w, conv_b, seq, tm=1024, tn=512):
    m, d = h.shape
    f = w_up.shape[1] // 2
    nj = f // tn
    tps = seq // tm
    cb = conv_b.reshape(1, 2 * f)
    return pl.pallas_call(
        functools.partial(_ffn_up_kernel, tm=tm, tiles_per_seq=tps),
        grid=(nj, m // tm),
        in_specs=[pl.BlockSpec((tm, d), lambda j, i: (i, 0)),
                  pl.BlockSpec((d, tn), lambda j, i: (0, j)),
                  pl.BlockSpec((d, tn), lambda j, i: (0, nj + j)),
                  pl.BlockSpec((CONV_WIDTH, tn), lambda j, i: (0, j)),
                  pl.BlockSpec((CONV_WIDTH, tn), lambda j, i: (0, nj + j)),
                  pl.BlockSpec((1, tn), lambda j, i: (0, j)),
                  pl.BlockSpec((1, tn), lambda j, i: (0, nj + j))],
        out_specs=pl.BlockSpec((tm, tn), lambda j, i: (i, j)),
        out_shape=jax.ShapeDtypeStruct((m, f), BF16),
        scratch_shapes=[pltpu.VMEM((tm + HALO, tn), F32), pltpu.VMEM((tm + HALO, tn), F32)],
        compiler_params=_cparams(("arbitrary", "arbitrary")),
        name="ffn_up_conv",
    )(h, w_up, w_up, conv_w, conv_w, cb, cb)


LOG2E = 1.4426950408889634
CWIN = 16


def _t5_shifted(dist, tbl_ref, h):
    val = jnp.full(dist.shape, tbl_ref[0, h], F32)
    for b in range(1, REL_BUCKETS):
        val = jnp.where(dist >= T5_THR[b - 1], tbl_ref[b, h], val)
    return (val - tbl_ref[REL_BUCKETS - 1, h]) * LOG2E


def _bias_tiles_kernel(tbl_ref, dt_ref, dc_ref):
    h = pl.program_id(0)
    key = lax.broadcasted_iota(jnp.int32, (LANE, LANE), 0)
    q = lax.broadcasted_iota(jnp.int32, (LANE, LANE), 1)
    for rel in range(2):
        dt_ref[0, rel] = _t5_shifted(rel * LANE + q - key, tbl_ref, h)
    dt_ref[0, 2] = jnp.zeros((LANE, LANE), F32)
    u = lax.broadcasted_iota(jnp.int32, (CWIN, LANE), 0)
    qc = lax.broadcasted_iota(jnp.int32, (CWIN, LANE), 1)
    dc_ref[0] = _t5_shifted(qc - CMP_STRIDE * (u - CWIN // 2) - (CMP_BLOCK - 1), tbl_ref, h)


def bias_tiles(rel_bias):
    nh = rel_bias.shape[1]
    return pl.pallas_call(
        _bias_tiles_kernel,
        grid=(nh,),
        in_specs=[pl.BlockSpec(memory_space=pltpu.SMEM)],
        out_specs=[pl.BlockSpec((1, 3, LANE, LANE), lambda h: (h, 0, 0, 0)),
                   pl.BlockSpec((1, CWIN, LANE), lambda h: (h, 0, 0))],
        out_shape=[jax.ShapeDtypeStruct((nh, 3, LANE, LANE), F32),
                   jax.ShapeDtypeStruct((nh, CWIN, LANE), F32)],
        compiler_params=_cparams(("arbitrary",)),
        name="t5_bias_tiles",
    )(rel_bias)


def _compress_kernel(x_ref, pe_ref, w1_ref, b1_ref, w2_ref, b2_ref, g_ref, o_ref, *, half):
    kv = pl.program_id(0)
    x = x_ref[0]
    a = _dot((x + pe_ref[0, :, :half]).astype(BF16), w1_ref[0, :half, :])
    b = _dot((x + pe_ref[0, :, half:]).astype(BF16), w1_ref[0, half:, :])
    b_next = jnp.concatenate([b[1:], jnp.zeros((1, b.shape[1]), F32)], axis=0)
    hid = jax.nn.gelu(a + b_next + b1_ref[0])
    out = _dot(hid.astype(BF16), w2_ref[0]) + b2_ref[0]
    normed = out * lax.rsqrt(jnp.mean(out * out, axis=-1, keepdims=True) + EPS) * g_ref[...]
    out = jnp.where(kv == 0, normed, out)
    o_ref[0, 0] = out.astype(o_ref.dtype)


def compress_kv(proj, slab0, bsz, seq, cmp_pe, cmp_w1, cmp_b1, cmp_w2, cmp_b2, g_k):
    nslab, m, _ = proj.shape
    nchunk = seq // CMP_STRIDE
    half = CMP_STRIDE * HEAD_DIM
    xv = proj.reshape(nslab, m // CMP_STRIDE, half)
    pe = cmp_pe.reshape(2, 1, CMP_BLOCK * HEAD_DIM)
    return pl.pallas_call(
        functools.partial(_compress_kernel, half=half),
        grid=(2, bsz, NSA_GROUPS),
        in_specs=[pl.BlockSpec((1, nchunk, half), lambda kv, b, g: (slab0 + 2 * kv + g, b, 0)),
                  pl.BlockSpec((1, 1, 2 * half), lambda kv, b, g: (kv, 0, 0)),
                  pl.BlockSpec((1, 2 * half, CMP_HIDDEN), lambda kv, b, g: (kv, 0, 0)),
                  pl.BlockSpec((1, 1, CMP_HIDDEN), lambda kv, b, g: (kv, 0, 0)),
                  pl.BlockSpec((1, CMP_HIDDEN, HEAD_DIM), lambda kv, b, g: (kv, 0, 0)),
                  pl.BlockSpec((1, 1, HEAD_DIM), lambda kv, b, g: (kv, 0, 0)),
                  pl.BlockSpec((1, HEAD_DIM), lambda kv, b, g: (0, 0))],
        out_specs=pl.BlockSpec((1, 1, nchunk, HEAD_DIM), lambda kv, b, g: (kv, g, b, 0)),
        out_shape=jax.ShapeDtypeStruct((2, NSA_GROUPS, bsz * nchunk, HEAD_DIM), BF16),
        compiler_params=_cparams(("arbitrary", "arbitrary", "arbitrary")),
        name="nsa_compress",
    )(xv, pe, cmp_w1.astype(BF16), cmp_b1.reshape(2, 1, CMP_HIDDEN), cmp_w2.astype(BF16),
      cmp_b2.reshape(2, 1, HEAD_DIM), g_k.reshape(1, HEAD_DIM))


KW = 512
PV_KEYS = 256


def _tile_lanes(x, n):
    return jnp.concatenate([x] * n, axis=1)


def _flash_init(m_ref, l_ref, acc_ref):
    m_ref[...] = jnp.full(m_ref.shape, NEG, F32)
    l_ref[...] = jnp.zeros(l_ref.shape, F32)
    acc_ref[...] = jnp.zeros(acc_ref.shape, F32)


def _zero_after(x):
    bits = pltpu.bitcast(x, jnp.int32)
    return lax.shift_right_logical(lax.shift_right_logical(bits, 16), 16).astype(F32)


def _flash_update(s, v_t, m_ref, l_ref, acc_ref, col_max=None, after=None):
    m_old = m_ref[...]
    if col_max is None:
        col_max = jnp.max(s, axis=0, keepdims=True)
    m_new = jnp.maximum(m_old, col_max)
    alpha = jnp.exp2(m_old - m_new)
    l_new = alpha * l_ref[...]
    acc = alpha * acc_ref[...]
    nk = s.shape[0]
    for k0 in range(0, nk, PV_KEYS):
        p = jnp.exp2(s[k0:k0 + PV_KEYS] - m_new)
        l_new = l_new + jnp.sum(p, axis=0, keepdims=True)
        acc = acc + _dot(v_t[:, k0:k0 + PV_KEYS], p.astype(BF16))
    l_ref[...] = l_new
    acc_ref[...] = acc
    m_ref[...] = m_new if after is None else m_new + _zero_after(after)


def _inv_den(m, den):
    ok = m > 0.5 * NEG
    return jnp.where(ok, 1.0 / jnp.where(ok, den, 1.0), 0.0)


def _flash_result(m_ref, l_ref, acc_ref):
    return acc_ref[...] * _inv_den(m_ref[...], l_ref[...])


def _softmax_cols(s):
    m = jnp.max(s, axis=0, keepdims=True)
    p = jnp.exp2(s - m)
    return p * _inv_den(m, jnp.sum(p, axis=0, keepdims=True))


def _near_bias(dt_ref, heads, qi, kt0, ntile):
    rows = []
    for j in range(ntile):
        rel = jnp.clip(qi - (kt0 + j), 0, 2)
        rows.append(jnp.concatenate([dt_ref[h, rel] for h in heads], axis=1))
    return jnp.concatenate(rows, axis=0)


def _pipelined_chunks(n, qk_stage, soft_stage):
    @pl.when(n > 0)
    def _():
        qk_stage(0, 0)

    def pair(p, x):
        c = 2 * p
        ahead = qk_stage(c + 1, 1)
        soft_stage(c, 0, ahead)
        ahead = qk_stage(jnp.minimum(c + 2, n - 1), 0)
        soft_stage(c + 1, 1, ahead)
        return x

    lax.fori_loop(0, n // 2, pair, 0)

    @pl.when(n % 2 == 1)
    def _():
        soft_stage(n - 1, 0, None)


def _nsa_kernel(qt_ref, gt_ref, kc_ref, vct_ref, ks_ref, vst_ref, kw_ref, vwt_ref,
                dt_ref, dc_ref, ext_ref, o_ref,
                sc_ref, ps_ref, m_ref, l_ref, acc_ref, sbuf0, sbuf1, cbuf0, cbuf1, *, seq, nc):
    sbuf, cbuf = (sbuf0, sbuf1), (cbuf0, cbuf1)
    qi = pl.program_id(2)
    q0 = qi * QB
    hpg = NSA_HPG
    heads = list(range(hpg))
    ncp = kc_ref.shape[1]
    ns = seq // SEL_BLOCK
    q_t = jnp.concatenate([qt_ref[h] for h in heads], axis=1)

    pad = CWIN // 2
    sc_ref[0:pad, :] = jnp.zeros((pad, hpg * QB), F32)
    sc_ref[pad + ncp:2 * pad + ncp, :] = jnp.zeros((pad, hpg * QB), F32)
    sc_ref[pad:pad + ncp, :] = _dot(kc_ref[0], q_t)
    r0 = pl.multiple_of(qi * (QB // CMP_STRIDE), 8)
    sc_ref[pl.ds(r0, CWIN), :] = sc_ref[pl.ds(r0, CWIN), :] + jnp.concatenate(
        [dc_ref[h] for h in heads], axis=1)
    ci = lax.broadcasted_iota(jnp.int32, (ncp, QB), 0)
    tc = q0 + lax.broadcasted_iota(jnp.int32, (ncp, QB), 1)
    valid_c = (ci * CMP_STRIDE + CMP_BLOCK - 1 <= tc) & (ci < nc)
    p_c = _softmax_cols(sc_ref[pad:pad + ncp, :] + _tile_lanes(jnp.where(valid_c, 0.0, NEG), hpg))
    oc_t = _dot(vct_ref[0], p_c.astype(BF16))
    p_sum = p_c[:, 0:QB]
    for h in range(1, hpg):
        p_sum = p_sum + p_c[:, h * QB:(h + 1) * QB]

    wkeys = WINDOW + QB
    start = pl.multiple_of(jnp.maximum(q0 - WINDOW, 0), LANE)
    s_w = _dot(kw_ref[0, pl.ds(start, wkeys), :], q_t)
    s_w = s_w + _near_bias(dt_ref, heads, qi, start // LANE, wkeys // LANE)
    dist_w = (q0 + lax.broadcasted_iota(jnp.int32, (wkeys, QB), 1)) - (
        start + lax.broadcasted_iota(jnp.int32, (wkeys, QB), 0))
    mask_w = (dist_w >= 0) & (dist_w < WINDOW)
    p_w = _softmax_cols(s_w + _tile_lanes(jnp.where(mask_w, 0.0, NEG), hpg))
    ow_t = _dot(vwt_ref[0, :, pl.ds(start, wkeys)], p_w.astype(BF16))

    ps_ref[0:8, :] = jnp.zeros((8, QB), F32)
    ps_ref[8:8 + ncp, :] = p_sum
    per = SEL_BLOCK // CMP_STRIDE
    band = [ps_ref[pl.ds(8 + r, ns, stride=per), :] for r in range(-1, per)]
    imp = 0.5 * band[0] + band[1] + band[2] + band[3] + 0.5 * band[4]
    if ns < LANE:
        imp = jnp.concatenate([imp, jnp.zeros((LANE - ns, QB), F32)], axis=0)
    blk = lax.broadcasted_iota(jnp.int32, (LANE, QB), 0)
    t = q0 + lax.broadcasted_iota(jnp.int32, (LANE, QB), 1)
    tb = t // SEL_BLOCK
    forced = (blk == 0) | (blk == tb) | (blk == tb - 1)
    score = jnp.where(forced, FORCE, jnp.where(blk * SEL_BLOCK <= t, imp, NEG))
    score = jnp.where(blk < ns, score, -jnp.inf)
    blk_f = blk.astype(F32)
    sel = jnp.zeros((LANE, QB), F32)
    for _ in range(min(SEL_TOP_N, ns)):
        mx = jnp.max(score, axis=0, keepdims=True)
        first = jnp.min(jnp.where(score == mx, blk_f, float(LANE)), axis=0, keepdims=True)
        pick = blk_f == first
        sel = jnp.where(pick, 1.0, sel)
        score = jnp.where(pick, -jnp.inf, score)
    sel_b = sel.astype(BF16)

    _flash_init(m_ref, l_ref, acc_ref)
    kpos = lax.broadcasted_iota(jnp.int32, (KW, QB), 0)
    tq = q0 + lax.broadcasted_iota(jnp.int32, (KW, QB), 1)

    def block_mask(c0):
        chosen = _dot(ext_ref[pl.ds(c0, KW), :], sel_b)
        return (chosen - 1.0) * (-NEG)

    def qk_stage(c, buf):
        c0 = pl.multiple_of(c * KW, KW)
        s = _dot(ks_ref[0, pl.ds(c0, KW), :], q_t) + _tile_lanes(block_mask(c0), hpg)
        sbuf[buf][...] = s
        col_max = jnp.max(s, axis=0, keepdims=True)
        cbuf[buf][...] = col_max
        return col_max

    def soft_stage(c, buf, ahead):
        c0 = pl.multiple_of(c * KW, KW)
        _flash_update(sbuf[buf][...], vst_ref[0, :, pl.ds(c0, KW)], m_ref, l_ref, acc_ref,
                      col_max=cbuf[buf][...], after=ahead)

    def near_step(c, x):
        c0 = pl.multiple_of(c * KW, KW)
        madd = block_mask(c0) + jnp.where(c0 + kpos <= tq, 0.0, NEG)
        s = _dot(ks_ref[0, pl.ds(c0, KW), :], q_t) + _near_bias(dt_ref, heads, qi, c * (KW // LANE), KW // LANE)
        _flash_update(s + _tile_lanes(madd, hpg), vst_ref[0, :, pl.ds(c0, KW)], m_ref, l_ref, acc_ref)
        return x

    c_near = jnp.maximum(qi - 1, 0) // (KW // LANE)
    _pipelined_chunks(c_near, qk_stage, soft_stage)
    lax.fori_loop(c_near, qi // (KW // LANE) + 1, near_step, 0)
    os_t = _flash_result(m_ref, l_ref, acc_ref)

    gates = jax.nn.sigmoid(gt_ref[0])
    for h in heads:
        sl = slice(h * QB, (h + 1) * QB)
        o_t = (gates[3 * h:3 * h + 1] * oc_t[:, sl] + gates[3 * h + 1:3 * h + 2] * os_t[:, sl]
               + gates[3 * h + 2:3 * h + 3] * ow_t[:, sl])
        o_ref[:, h * HEAD_DIM:(h + 1) * HEAD_DIM] = o_t.T.astype(o_ref.dtype)


def nsa_attention(q_t, gates_t, kc, vc_t, k_sw, v_sw_t, dt, dc, bsz, seq):
    nq = seq // QB
    ncp = seq // CMP_STRIDE
    nc = ncp - 1
    ns = seq // SEL_BLOCK
    assert ns <= LANE and seq >= WINDOW + QB and seq % KW == 0
    assert CMP_BLOCK == 2 * CMP_STRIDE and SEL_BLOCK == 4 * CMP_STRIDE
    expand =((np.arange(seq)[:, None] // SEL_BLOCK) == np.arange(LANE)[None, :]).astype(np.float32)
    ng = NSA_GROUPS
    ks_spec = pl.BlockSpec((1, seq, HEAD_DIM), lambda b, g, i: (g, b, 0))
    kw_spec = pl.BlockSpec((1, seq, HEAD_DIM), lambda b, g, i: (ng + g, b, 0))
    vs_spec = pl.BlockSpec((1, HEAD_DIM, seq), lambda b, g, i: (g, 0, b))
    vw_spec = pl.BlockSpec((1, HEAD_DIM, seq), lambda b, g, i: (ng + g, 0, b))
    lanes = NSA_HPG * QB
    return pl.pallas_call(
        functools.partial(_nsa_kernel, seq=seq, nc=nc),
        grid=(bsz, NSA_GROUPS, nq),
        in_specs=[pl.BlockSpec((NSA_HPG, HEAD_DIM, QB), lambda b, g, i: (g, 0, b * nq + i)),
                  pl.BlockSpec((1, 16, QB), lambda b, g, i: (g, 0, b * nq + i)),
                  pl.BlockSpec((1, ncp, HEAD_DIM), lambda b, g, i: (g, b, 0)),
                  pl.BlockSpec((1, HEAD_DIM, ncp), lambda b, g, i: (g, 0, b)),
                  ks_spec, vs_spec, kw_spec, vw_spec,
                  pl.BlockSpec((NSA_HPG, 3, LANE, LANE), lambda b, g, i: (g, 0, 0, 0)),
                  pl.BlockSpec((NSA_HPG, CWIN, LANE), lambda b, g, i: (g, 0, 0)),
                  pl.BlockSpec((seq, LANE), lambda b, g, i: (0, 0))],
        out_specs=pl.BlockSpec((QB, NSA_HPG * HEAD_DIM), lambda b, g, i: (b * nq + i, g)),
        out_shape=jax.ShapeDtypeStruct((bsz * seq, NSA_HEADS * HEAD_DIM), BF16),
        scratch_shapes=[pltpu.VMEM((ncp + CWIN, lanes), F32), pltpu.VMEM((ncp + 8, QB), F32),
                        pltpu.VMEM((1, lanes), F32), pltpu.VMEM((1, lanes), F32),
                        pltpu.VMEM((HEAD_DIM, lanes), F32),
                        pltpu.VMEM((KW, lanes), F32), pltpu.VMEM((KW, lanes), F32),
                        pltpu.VMEM((1, lanes), F32), pltpu.VMEM((1, lanes), F32)],
        compiler_params=_cparams(("arbitrary", "arbitrary", "arbitrary")),
        name="nsa_attention",
    )(q_t, gates_t, kc, vc_t, k_sw, v_sw_t, k_sw, v_sw_t, dt, dc, jnp.asarray(expand, BF16))


MLA_HPS = 2


def _mla_kernel(qt_ref, k_ref, vt_ref, o_ref, *scratch):
    qi = pl.program_id(2)
    chains = [scratch[3 * h:3 * h + 3] for h in range(MLA_HPS)]
    sbuf = [scratch[(3 + b) * MLA_HPS:(4 + b) * MLA_HPS] for b in range(2)]
    cbuf = [scratch[(5 + b) * MLA_HPS:(6 + b) * MLA_HPS] for b in range(2)]
    for ch in chains:
        _flash_init(*ch)

    def qk_stage(c, buf):
        c0 = pl.multiple_of(c * KW, KW)
        col_max = []
        for h in range(MLA_HPS):
            s = _dot(k_ref[h, pl.ds(c0, KW), :], qt_ref[h])
            sbuf[buf][h][...] = s
            col_max.append(jnp.max(s, axis=0, keepdims=True))
            cbuf[buf][h][...] = col_max[-1]
        return col_max

    def soft_stage(c, buf, ahead):
        c0 = pl.multiple_of(c * KW, KW)
        for h, ch in enumerate(chains):
            _flash_update(sbuf[buf][h][...], vt_ref[h, :, pl.ds(c0, KW)], *ch, col_max=cbuf[buf][h][...],
                          after=None if ahead is None else ahead[h])

    _pipelined_chunks(qi, qk_stage, soft_stage)
    c0 = pl.multiple_of(qi * KW, KW)
    kpos = lax.broadcasted_iota(jnp.int32, (KW, KW), 0)
    tq = lax.broadcasted_iota(jnp.int32, (KW, KW), 1)
    causal = jnp.where(kpos <= tq, 0.0, NEG)
    dv = vt_ref.shape[1]
    scores = [_dot(k_ref[h, pl.ds(c0, KW), :], qt_ref[h]) + causal for h in range(MLA_HPS)]
    for h, ch in enumerate(chains):
        _flash_update(scores[h], vt_ref[h, :, pl.ds(c0, KW)], *ch)
        o_ref[:, h * dv:(h + 1) * dv] = _flash_result(*ch).T.astype(o_ref.dtype)


def mla_attention(q_t, k, v_t, bsz, seq):
    nh, dqk, _ = q_t.shape
    dv = v_t.shape[1]
    nq = seq // KW
    hps = MLA_HPS
    state = [pltpu.VMEM((1, KW), F32), pltpu.VMEM((1, KW), F32), pltpu.VMEM((dv, KW), F32)] * hps
    state += [pltpu.VMEM((KW, KW), F32)] * (2 * hps)
    state += [pltpu.VMEM((1, KW), F32)] * (2 * hps)
    return pl.pallas_call(
        _mla_kernel,
        grid=(bsz, nh // hps, nq),
        in_specs=[pl.BlockSpec((hps, dqk, KW), lambda b, h, i: (h, 0, b * nq + i)),
                  pl.BlockSpec((hps, seq, dqk), lambda b, h, i: (h, b, 0)),
                  pl.BlockSpec((hps, dv, seq), lambda b, h, i: (h, 0, b))],
        out_specs=pl.BlockSpec((KW, hps * dv), lambda b, h, i: (b * nq + i, h)),
        out_shape=jax.ShapeDtypeStruct((bsz * seq, nh * dv), BF16),
        scratch_shapes=state,
        compiler_params=_cparams(("arbitrary", "arbitrary", "arbitrary")),
        name="mla_attention",
    )(q_t, k, v_t)


INT_MIN = -2 ** 31
NEG_KEY = int(np.array(NEG, np.float32).view(np.int32)) ^ 0x7FFFFFFF
NEG_UHI = ((NEG_KEY ^ INT_MIN) & 0xFFFFFFFF) >> 16
NEG_ULO = (NEG_KEY ^ INT_MIN) & 0xFFFF


def _sort_key(x):
    bits = pltpu.bitcast(x + 0.0, jnp.int32)
    return jnp.where(bits < 0, bits ^ 0x7FFFFFFF, bits)


def _dsa_kernel(iqt_ref, iwt_ref, ik_ref, qt_ref, k_ref, vt_ref, dt_ref, o_ref,
                key_ref, hi_ref, lo_ref, madd_ref, *state, seq, k_sel):
    qi = pl.program_id(1)
    q0 = qi * QB
    n_chunk = (q0 + QB + KW - 1) // KW
    n_rest = seq - n_chunk * KW
    kpos = lax.broadcasted_iota(jnp.int32, (KW, QB), 0)
    tq = q0 + lax.broadcasted_iota(jnp.int32, (KW, QB), 1)
    hpp = KW // QB

    def score_chunk(c, x):
        c0 = pl.multiple_of(c * KW, KW)
        ikc = ik_ref[pl.ds(c0, KW), :]
        acc = jnp.zeros((KW, QB), F32)
        for piece in range(IDX_HEADS // hpp):
            sl = slice(piece * KW, (piece + 1) * KW)
            s = jnp.maximum(_dot(ikc, iqt_ref[0, :, sl]), 0.0) * iwt_ref[0, :, sl]
            for j in range(hpp):
                acc = acc + s[:, j * QB:(j + 1) * QB]
        key = _sort_key(jnp.where(c0 + kpos <= tq, acc, NEG))
        key_ref[pl.ds(c0, KW), :] = key
        hi_ref[pl.ds(c0, KW), :] = lax.shift_right_logical(key, 16).astype(jnp.int16)
        lo_ref[pl.ds(c0, KW), :] = (key ^ 0x8000).astype(jnp.int16)
        return x

    lax.fori_loop(0, n_chunk, score_chunk, 0)

    def count(pred):
        def body(c, acc):
            c0 = pl.multiple_of(c * KW, KW)
            hit = jnp.where(pred(key_ref[pl.ds(c0, KW), :], c0), 1.0, 0.0)
            parts = [hit[8 * i:8 * (i + 1)] for i in range(KW // 8)]
            while len(parts) > 1:
                parts = [parts[i] + parts[i + 1] for i in range(0, len(parts), 2)]
            return acc + parts[0]
        acc = lax.fori_loop(0, n_chunk, body, jnp.zeros((8, QB), F32))
        return jnp.sum(acc, axis=0, keepdims=True)

    rest = n_rest.astype(F32)
    kf = float(k_sel)

    def count16(ref, pred):
        def body(c, acc):
            c0 = pl.multiple_of(c * KW, KW)
            hit = jnp.where(pred(ref[pl.ds(c0, KW), :]), jnp.int16(1), jnp.int16(0))
            parts = [hit[16 * i:16 * (i + 1)] for i in range(KW // 16)]
            while len(parts) > 1:
                parts = [parts[i] + parts[i + 1] for i in range(0, len(parts), 2)]
            return acc + parts[0]
        acc = lax.fori_loop(0, n_chunk, body, jnp.zeros((16, QB), jnp.int16))
        return jnp.sum(acc.astype(jnp.int32), axis=0, keepdims=True).astype(F32)

    def as_half(u):
        return (u ^ 0x8000).astype(jnp.int16)

    def search_half(ref, rank, rest_counts):
        def step(i, u):
            trial = u | jnp.left_shift(jnp.int32(1), 15 - i)
            half = as_half(trial)
            cnt = count16(ref, lambda v: v >= half) + jnp.where(rest_counts(trial), rest, 0.0)
            return jnp.where(cnt >= rank, trial, u)
        return lax.fori_loop(0, 16, step, jnp.zeros((1, QB), jnp.int32))

    u_hi = search_half(hi_ref, kf, lambda trial: NEG_UHI >= trial)
    top = as_half(u_hi)
    above = count16(hi_ref, lambda v: v > top) + jnp.where(NEG_UHI > u_hi, rest, 0.0)

    def keep_candidates(c, x):
        c0 = pl.multiple_of(c * KW, KW)
        lo_ref[pl.ds(c0, KW), :] = jnp.where(hi_ref[pl.ds(c0, KW), :] == top, lo_ref[pl.ds(c0, KW), :],
                                             jnp.int16(-2 ** 15))
        return x

    lax.fori_loop(0, n_chunk, keep_candidates, 0)
    u_lo = search_half(lo_ref, kf - above, lambda trial: (u_hi == NEG_UHI) & (NEG_ULO >= trial))
    thr = (jnp.left_shift(u_hi, 16) | u_lo) ^ INT_MIN
    cnt_gt = count(lambda keys, c0: keys > thr) + jnp.where(NEG_KEY > thr, rest, 0.0)
    cnt_ge = count(lambda keys, c0: keys >= thr) + jnp.where(NEG_KEY >= thr, rest, 0.0)
    need = kf - cnt_gt
    tie_q = (cnt_ge > kf) & (thr != NEG_KEY)
    idx_bits = (seq - 1).bit_length()
    no_cut = 2 ** 30

    def tie_cut():
        def idx_step(i, x):
            bit = jnp.left_shift(jnp.int32(1), idx_bits - 1 - i)
            trial = x | bit
            f = count(lambda keys, c0: (keys == thr) & (c0 + kpos < trial))
            return jnp.where(f <= need - 1.0, trial, x)
        return lax.fori_loop(0, idx_bits, idx_step, jnp.zeros((1, QB), jnp.int32))

    any_tie = jnp.max(jnp.where(tie_q, 1.0, 0.0)) > 0.0
    x_cut = lax.cond(any_tie, tie_cut, lambda: jnp.full((1, QB), no_cut, jnp.int32))
    x_cut = jnp.where(tie_q, x_cut, no_cut)

    def mask_chunk(c, x):
        c0 = pl.multiple_of(c * KW, KW)
        keys = key_ref[pl.ds(c0, KW), :]
        pos = c0 + kpos
        chosen = (keys > thr) | ((keys == thr) & (pos <= x_cut))
        madd_ref[pl.ds(c0, KW), :] = jnp.where(chosen & (pos <= tq), 0.0, NEG)
        return x

    lax.fori_loop(0, n_chunk, mask_chunk, 0)

    c_near = jnp.maximum(qi - 1, 0) // (KW // LANE)
    ng = DSA_KV_HEADS
    chains = [state[3 * g:3 * g + 3] for g in range(ng)]
    sbuf = [state[(3 + b) * ng:(4 + b) * ng] for b in range(2)]
    cbuf = [state[(5 + b) * ng:(6 + b) * ng] for b in range(2)]
    group_heads = [[g * DSA_HPG + h for h in range(DSA_HPG)] for g in range(ng)]
    for ch in chains:
        _flash_init(*ch)

    def raw_scores(c0, g):
        q_t = jnp.concatenate([qt_ref[h] for h in group_heads[g]], axis=1)
        return _dot(k_ref[g, pl.ds(c0, KW), :], q_t)

    def qk_stage(c, buf):
        c0 = pl.multiple_of(c * KW, KW)
        madd = _tile_lanes(madd_ref[pl.ds(c0, KW), :], DSA_HPG)
        col_max = []
        for g in range(ng):
            s = raw_scores(c0, g) + madd
            sbuf[buf][g][...] = s
            col_max.append(jnp.max(s, axis=0, keepdims=True))
            cbuf[buf][g][...] = col_max[-1]
        return col_max

    def soft_stage(c, buf, ahead):
        c0 = pl.multiple_of(c * KW, KW)
        for g, ch in enumerate(chains):
            _flash_update(sbuf[buf][g][...], vt_ref[g, :, pl.ds(c0, KW)], *ch, col_max=cbuf[buf][g][...],
                          after=None if ahead is None else ahead[g])

    _pipelined_chunks(c_near, qk_stage, soft_stage)

    def near_step(c, x):
        c0 = pl.multiple_of(c * KW, KW)
        madd = _tile_lanes(madd_ref[pl.ds(c0, KW), :], DSA_HPG)
        scores = [raw_scores(c0, g) + madd + _near_bias(dt_ref, group_heads[g], qi, c * (KW // LANE), KW // LANE)
                  for g in range(ng)]
        for g, ch in enumerate(chains):
            _flash_update(scores[g], vt_ref[g, :, pl.ds(c0, KW)], *ch)
        return x

    lax.fori_loop(c_near, n_chunk, near_step, 0)
    for g, ch in enumerate(chains):
        o_t = _flash_result(*ch)
        for h in range(DSA_HPG):
            hh = group_heads[g][h]
            o_ref[:, hh * HEAD_DIM:(hh + 1) * HEAD_DIM] = o_t[:, h * QB:(h + 1) * QB].T.astype(o_ref.dtype)


def _idx_prep_kernel(p_ref, cos_ref, sin_ref, iqt_ref, ik_ref, iwt_ref, *, ntile):
    nslab_q = IDX_HEADS * IDX_DIM // LANE
    per = LANE // IDX_DIM
    zpad = jnp.zeros((QB, LANE - IDX_DIM), F32)

    def rope_part(x, cos, sin):
        return jnp.concatenate([_rope_rows(x[:, :IDX_ROPE], cos, sin), x[:, IDX_ROPE:]], axis=1)

    for t in range(ntile):
        rows = slice(t * QB, (t + 1) * QB)
        cos, sin = cos_ref[rows, :], sin_ref[rows, :]
        cols = []
        for s in range(nslab_q):
            slab = p_ref[s, rows, :]
            for j in range(per):
                x = rope_part(slab[:, j * IDX_DIM:(j + 1) * IDX_DIM], cos, sin) * IDX_DIM ** -0.5
                cols.append(jnp.concatenate([x, zpad], axis=1).T)
        iqt_ref[t] = jnp.concatenate(cols, axis=1).astype(iqt_ref.dtype)
        tail = p_ref[nslab_q, rows, :]
        ik = rope_part(tail[:, :IDX_DIM], cos, sin)
        ik_ref[rows, :] = jnp.concatenate([ik, zpad], axis=1).astype(ik_ref.dtype)
        w_t = (tail * IDX_HEADS ** -0.5).T
        iwt_ref[t] = jnp.concatenate([w_t[IDX_DIM + h:IDX_DIM + h + 1, :] for h in range(IDX_HEADS)], axis=1)


def indexer_operands(proj, seq, tm=512):
    _, m, _ = proj.shape
    ntile = tm // QB
    tps = seq // tm
    half = IDX_ROPE // 2
    cos, sin = _rope_tables(seq, IDX_ROPE)
    lanes = IDX_HEADS * QB
    return pl.pallas_call(
        functools.partial(_idx_prep_kernel, ntile=ntile),
        grid=(m // tm,),
        in_specs=[pl.BlockSpec((proj.shape[0], tm, LANE), lambda i: (0, i, 0)),
                  pl.BlockSpec((tm, half), lambda i: (i % tps, 0)),
                  pl.BlockSpec((tm, half), lambda i: (i % tps, 0))],
        out_specs=[pl.BlockSpec((ntile, LANE, lanes), lambda i: (i, 0, 0)),
                   pl.BlockSpec((tm, LANE), lambda i: (i, 0)),
                   pl.BlockSpec((ntile, 1, lanes), lambda i: (i, 0, 0))],
        out_shape=[jax.ShapeDtypeStruct((m // QB, LANE, lanes), BF16),
                   jax.ShapeDtypeStruct((m, LANE), BF16),
                   jax.ShapeDtypeStruct((m // QB, 1, lanes), F32)],
        compiler_params=_cparams(("arbitrary",)),
        name="dsa_indexer_operands",
    )(proj, cos, sin)


def dsa_attention(iq_t, iw_t, ik, q_t, k, v_t, dt, bsz, seq):
    nq = seq // QB
    k_sel = min(DSA_TOPK_MAX, seq // 4)
    assert seq % KW == 0
    lanes = DSA_HPG * QB
    return pl.pallas_call(
        functools.partial(_dsa_kernel, seq=seq, k_sel=k_sel),
        grid=(bsz, nq),
        in_specs=[pl.BlockSpec((1, LANE, IDX_HEADS * QB), lambda b, i: (b * nq + i, 0, 0)),
                  pl.BlockSpec((1, 1, IDX_HEADS * QB), lambda b, i: (b * nq + i, 0, 0)),
                  pl.BlockSpec((seq, LANE), lambda b, i: (b, 0)),
                  pl.BlockSpec((DSA_HEADS, HEAD_DIM, QB), lambda b, i: (0, 0, b * nq + i)),
                  pl.BlockSpec((DSA_KV_HEADS, seq, HEAD_DIM), lambda b, i: (0, b, 0),
                               pipeline_mode=pl.Buffered(1)),
                  pl.BlockSpec((DSA_KV_HEADS, HEAD_DIM, seq), lambda b, i: (0, 0, b),
                               pipeline_mode=pl.Buffered(1)),
                  pl.BlockSpec((DSA_HEADS, 3, LANE, LANE), lambda b, i: (0, 0, 0, 0),
                               pipeline_mode=pl.Buffered(1))],
        out_specs=pl.BlockSpec((QB, DSA_HEADS * HEAD_DIM), lambda b, i: (b * nq + i, 0)),
        out_shape=jax.ShapeDtypeStruct((bsz * seq, DSA_HEADS * HEAD_DIM), BF16),
        scratch_shapes=[pltpu.VMEM((seq, QB), jnp.int32), pltpu.VMEM((seq, QB), jnp.int16),
                        pltpu.VMEM((seq, QB), jnp.int16), pltpu.VMEM((seq, QB), F32)]
        + [pltpu.VMEM((1, lanes), F32), pltpu.VMEM((1, lanes), F32),
           pltpu.VMEM((HEAD_DIM, lanes), F32)] * DSA_KV_HEADS
        + [pltpu.VMEM((KW, lanes), F32)] * (2 * DSA_KV_HEADS)
        + [pltpu.VMEM((1, lanes), F32)] * (2 * DSA_KV_HEADS),
        compiler_params=_cparams(("arbitrary", "arbitrary")),
        name="dsa_attention",
    )(iq_t, iw_t, ik, q_t, k, v_t, dt)


def _rope_tables(seq, dim):
    half = dim // 2
    inv = ROPE_THETA ** (-jnp.arange(half, dtype=F32) / half)
    ang = jnp.arange(seq, dtype=F32)[:, None] * inv[None, :]
    return jnp.cos(ang), jnp.sin(ang)


def _pad_cols(w, n):
    return jnp.pad(w, ((0, 0), (0, n - w.shape[1])))


def _t(x):
    return jnp.swapaxes(x, -1, -2)


def _even_mixer(h, x2, gate, dt, dc, bsz, seq, w_in, w_out, nsa_qk_g, cmp_pe, cmp_w1, cmp_b1,
                cmp_w2, cmp_b2, q_norm_g, kv_norm_g, w_uq, w_ukv, nope_g, rope_g):
    m = bsz * seq
    nq_cols = NSA_HEADS * HEAD_DIM
    nkv_cols = 6 * NSA_GROUPS * HEAD_DIM
    ngate = 3 * NSA_HEADS
    o_gate = nq_cols + nkv_cols
    o_cq = o_gate + ngate
    o_ckv = o_cq + MLA_Q_RANK
    o_kpe = o_ckv + MLA_KV_RANK
    gw = NSA_GROUPS * HEAD_DIM
    kvw = [w_in[:, nq_cols + i * gw:nq_cols + (i + 1) * gw] for i in range(6)]
    scale = HEAD_DIM ** -0.5 * LOG2E
    q_t = proj_heads(h, w_in[:, :nq_cols].astype(BF16), nsa_qk_g[0] * scale, transpose=True)
    k_sw = proj_heads(h, jnp.concatenate([kvw[2], kvw[4]], axis=1).astype(BF16), nsa_qk_g[1],
                      transpose=False)
    v_sw_t = proj_heads(h, jnp.concatenate([kvw[3], kvw[5]], axis=1).astype(BF16), transpose=True)
    tail = jnp.concatenate([w_in[:, o_kpe:], w_in[:, o_gate:o_cq]], axis=1)
    w_r = jnp.concatenate([kvw[0], kvw[1], w_in[:, o_cq:o_kpe], _pad_cols(tail, LANE)], axis=1).astype(BF16)
    proj = proj_slabs(h, w_r, tn=w_r.shape[1])
    s_cq = 2 * NSA_GROUPS
    s_ckv = s_cq + MLA_Q_RANK // LANE
    s_tail = s_ckv + MLA_KV_RANK // LANE
    kvc = compress_kv(proj, 0, bsz, seq, cmp_pe, cmp_w1, cmp_b1, cmp_w2, cmp_b2, nsa_qk_g[1])
    tail_v = proj[s_tail]
    gates = tail_v[:, MLA_ROPE:MLA_ROPE + ngate].reshape(m, NSA_GROUPS, 3 * NSA_HPG)
    gates_t = jnp.pad(jnp.transpose(gates, (1, 2, 0)), ((0, 0), (0, 16 - 3 * NSA_HPG), (0, 0)))
    o_nsa = nsa_attention(q_t, gates_t, kvc[0], _t(kvc[1]), k_sw, v_sw_t,
                          dt[:NSA_HEADS], dc[:NSA_HEADS], bsz, seq)

    dq = MLA_NOPE + MLA_ROPE
    wq = w_uq.reshape(MLA_Q_RANK, MLA_HEADS, dq)
    wq_r = jnp.concatenate([wq[:, :, :MLA_NOPE].reshape(MLA_Q_RANK, -1),
                            wq[:, :, MLA_NOPE:].reshape(MLA_Q_RANK, -1)], axis=1).astype(BF16)
    cos, sin = _rope_tables(seq, MLA_ROPE)
    q_mla_t, k_mla, v_mla_t = mla_project(proj, s_cq, s_ckv, s_tail, seq, q_norm_g, kv_norm_g, wq_r,
                                          w_ukv.astype(BF16), nope_g, rope_g, cos, sin, dq ** -0.5 * LOG2E)
    o_mla = mla_attention(q_mla_t, k_mla, v_mla_t, bsz, seq)
    w_o = w_out.astype(BF16)
    return resproj([(o_nsa, w_o[:nq_cols]), (o_mla, w_o[nq_cols:])], x2, gate, seq)


def _odd_mixer(h, x2, gate, dt, bsz, seq, w_in, w_out, qk_g):
    nq = DSA_HEADS * HEAD_DIM
    nkv = DSA_KV_HEADS * HEAD_DIM
    niq = IDX_HEADS * IDX_DIM
    o_k, o_v, o_iq = nq, nq + nkv, nq + 2 * nkv
    q_t = proj_heads(h, w_in[:, :o_k].astype(BF16), qk_g[0] * (HEAD_DIM ** -0.5 * LOG2E), transpose=True)
    k = proj_heads(h, w_in[:, o_k:o_v].astype(BF16), qk_g[1], transpose=False)
    v_t = proj_heads(h, w_in[:, o_v:o_iq].astype(BF16), transpose=True)
    w_idx = w_in[:, o_iq:]
    proj = proj_slabs(h, _pad_cols(w_idx, niq + LANE).astype(BF16), tn=niq + LANE)
    iq_t, ik, iw_t = indexer_operands(proj, seq)
    o = dsa_attention(iq_t, iw_t, ik, q_t, k, v_t, dt, bsz, seq)
    return resproj([(o, w_out.astype(BF16))], x2, gate, seq)


def _conv_ffn(h, x2, gate, seq, w_up, conv_w, conv_b, w_down):
    a = ffn_up(h, w_up.astype(BF16), conv_w, conv_b, seq)
    return resproj([(a, w_down.astype(BF16))], x2, gate, seq)


def kernel(x, c, rel_bias, ada_w, ada_b, norm_g, ev_w_in, ev_w_out, nsa_qk_g, cmp_pe, cmp_w1, cmp_b1, cmp_w2, cmp_b2, mla_q_norm_g, mla_kv_norm_g, mla_w_uq, mla_w_ukv, mla_nope_g, mla_rope_g, od_w_in, od_w_out, dsa_qk_g, ffn_w_up, ffn_conv_w, ffn_conv_b, ffn_w_down):
    bsz, seq, d = x.shape
    depth = ada_w.shape[0]
    x2 = x.reshape(bsz * seq, d)
    mods = ada_all(c, ada_w, ada_b)
    dt, dc = bias_tiles(rel_bias)
    for i in range(depth):
        j = i // 2
        shift, scale, gate = jnp.split(mods[i, 0], 3, axis=-1)
        h = modnorm(x2, norm_g[i, 0], scale, shift, seq)
        if i % 2 == 0:
            x2 = _even_mixer(h, x2, gate, dt, dc, bsz, seq, ev_w_in[j], ev_w_out[j], nsa_qk_g[j],
                             cmp_pe[j], cmp_w1[j], cmp_b1[j], cmp_w2[j], cmp_b2[j], mla_q_norm_g[j],
                             mla_kv_norm_g[j], mla_w_uq[j], mla_w_ukv[j], mla_nope_g[j], mla_rope_g[j])
        else:
            x2 = _odd_mixer(h, x2, gate, dt, bsz, seq, od_w_in[j], od_w_out[j], dsa_qk_g[j])
        shift, scale, gate = jnp.split(mods[i, 1], 3, axis=-1)
        h = modnorm(x2, norm_g[i, 1], scale, shift, seq)
        x2 = _conv_ffn(h, x2, gate, seq, ffn_w_up[i], ffn_conv_w[i], ffn_conv_b[i], ffn_w_down[i])
    return x2.reshape(bsz, seq, d)
```

```python
import functools
import math

import numpy as np
import jax
import jax.numpy as jnp
from jax import lax
from jax.experimental import pallas as pl
from jax.experimental.pallas import tpu as pltpu

HEAD_DIM = 128
NSA_HEADS = 8
NSA_GROUPS = 2
NSA_HPG = NSA_HEADS // NSA_GROUPS
CMP_BLOCK = 32
CMP_STRIDE = 16
CMP_HIDDEN = 256
SEL_BLOCK = 64
SEL_TOP_N = 16
WINDOW = 512
MLA_HEADS = 8
MLA_Q_RANK = 512
MLA_KV_RANK = 256
MLA_NOPE = 128
MLA_ROPE = 64
MLA_V = 128
DSA_HEADS = 16
DSA_KV_HEADS = 4
DSA_HPG = DSA_HEADS // DSA_KV_HEADS
IDX_HEADS = 16
IDX_DIM = 64
IDX_ROPE = 32
DSA_TOPK_MAX = 256
REL_BUCKETS = 32
REL_MAX_DIST = 128
CONV_WIDTH = 3
ROPE_THETA = 10000.0
EPS = 1e-6
NEG = -1e30
FORCE = 1e9

LANE = 128
QB = 128
VMEM_LIMIT = 56 * 1024 * 1024

F32 = jnp.float32
BF16 = jnp.bfloat16


def _t5_thresholds():
    d = np.arange(0, 4 * REL_MAX_DIST)
    half = REL_BUCKETS // 2
    val = np.log(np.maximum(d, 1) / half) / math.log(REL_MAX_DIST / half) * (REL_BUCKETS - half)
    large = np.minimum(half + np.floor(np.maximum(val, 0.0)).astype(np.int64), REL_BUCKETS - 1)
    bucket = np.where(d < half, d, large)
    return [int(np.argmax(bucket >= b)) for b in range(1, REL_BUCKETS)]


T5_THR = _t5_thresholds()
T5_FAR = T5_THR[-1]
assert T5_FAR <= LANE


def _cparams(sem):
    return pltpu.CompilerParams(dimension_semantics=sem, vmem_limit_bytes=VMEM_LIMIT)


def _dot(a, b):
    return jnp.dot(a, b, preferred_element_type=F32)


def _ada_kernel(c_ref, w_ref, b_ref, o_ref):
    c = c_ref[...]
    a = c * jax.nn.sigmoid(c)
    o_ref[0] = jnp.dot(a, w_ref[0], preferred_element_type=F32,
                       precision=lax.Precision.HIGHEST) + b_ref[0]


def ada_all(c, ada_w, ada_b):
    depth, two, d, n3 = ada_w.shape
    bsz = c.shape[0]
    rows = 8
    cp = jnp.zeros((rows, d), F32).at[:bsz].set(c)
    w = ada_w.reshape(depth * two, d, n3)
    b = ada_b.reshape(depth * two, 1, n3)
    tn = 512
    out = pl.pallas_call(
        _ada_kernel,
        grid=(depth * two, n3 // tn),
        in_specs=[pl.BlockSpec((rows, d), lambda l, j: (0, 0)),
                  pl.BlockSpec((1, d, tn), lambda l, j: (l, 0, j)),
                  pl.BlockSpec((1, 1, tn), lambda l, j: (l, 0, j))],
        out_specs=pl.BlockSpec((1, rows, tn), lambda l, j: (l, 0, j)),
        out_shape=jax.ShapeDtypeStruct((depth * two, rows, n3), F32),
        compiler_params=_cparams(("arbitrary", "arbitrary")),
        name="ada_mod",
    )(cp, w, b)
    return out[:, :bsz].reshape(depth, two, bsz, n3)


def _modnorm_kernel(x_ref, g_ref, sc_ref, sh_ref, o_ref):
    x = x_ref[...]
    y = x * lax.rsqrt(jnp.mean(x * x, axis=-1, keepdims=True) + EPS)
    h = (y * g_ref[...]) * (1.0 + sc_ref[0]) + sh_ref[0]
    o_ref[...] = h.astype(o_ref.dtype)


def modnorm(x2, g, scale, shift, seq):
    m, d = x2.shape
    tm = 512
    tpb = seq // tm
    return pl.pallas_call(
        _modnorm_kernel,
        grid=(m // tm,),
        in_specs=[pl.BlockSpec((tm, d), lambda i: (i, 0)),
                  pl.BlockSpec((1, d), lambda i: (0, 0)),
                  pl.BlockSpec((1, 1, d), lambda i: (i // tpb, 0, 0)),
                  pl.BlockSpec((1, 1, d), lambda i: (i // tpb, 0, 0))],
        out_specs=pl.BlockSpec((tm, d), lambda i: (i, 0)),
        out_shape=jax.ShapeDtypeStruct((m, d), BF16),
        compiler_params=_cparams(("arbitrary",)),
        name="modnorm",
    )(x2, g.reshape(1, d), scale.reshape(-1, 1, d), shift.reshape(-1, 1, d))


def _proj_kernel(x_ref, w_ref, o_ref, *, nslab):
    acc = _dot(x_ref[...], w_ref[...])
    for s in range(nslab):
        o_ref[s] = acc[:, s * LANE:(s + 1) * LANE]


def proj_slabs(x, w, tm=1024, tn=384):
    m, k = x.shape
    n = w.shape[1]
    assert n % tn == 0 and m % tm == 0
    nslab = tn // LANE
    return pl.pallas_call(
        functools.partial(_proj_kernel, nslab=nslab),
        grid=(m // tm, n // tn),
        in_specs=[pl.BlockSpec((tm, k), lambda i, j: (i, 0)),
                  pl.BlockSpec((k, tn), lambda i, j: (0, j))],
        out_specs=pl.BlockSpec((nslab, tm, LANE), lambda i, j: (j, i, 0)),
        out_shape=jax.ShapeDtypeStruct((n // LANE, m, LANE), F32),
        compiler_params=_cparams(("arbitrary", "arbitrary")),
        name="proj_slabs",
    )(x, w)


def _proj_heads_kernel(x_ref, w_ref, g_ref, o_ref, *, nslab, norm, transpose):
    acc = _dot(x_ref[...], w_ref[...])
    for s in range(nslab):
        y = acc[:, s * LANE:(s + 1) * LANE]
        if norm:
            y = y * lax.rsqrt(jnp.mean(y * y, axis=-1, keepdims=True) + EPS) * g_ref[...]
        o_ref[s] = (y.T if transpose else y).astype(o_ref.dtype)


def proj_heads(x, w, g=None, *, transpose, tm=1024, tn=1024):
    m, k = x.shape
    n = w.shape[1]
    tn = min(tn, n)
    assert n % tn == 0 and m % tm == 0
    nslab = tn // LANE
    norm = g is not None
    if transpose:
        out_spec = pl.BlockSpec((nslab, LANE, tm), lambda i, j: (j, 0, i))
        out_shape = jax.ShapeDtypeStruct((n // LANE, LANE, m), BF16)
    else:
        out_spec = pl.BlockSpec((nslab, tm, LANE), lambda i, j: (j, i, 0))
        out_shape = jax.ShapeDtypeStruct((n // LANE, m, LANE), BF16)
    g2 = (g if norm else jnp.ones((LANE,), F32)).reshape(1, LANE)
    return pl.pallas_call(
        functools.partial(_proj_heads_kernel, nslab=nslab, norm=norm, transpose=transpose),
        grid=(m // tm, n // tn),
        in_specs=[pl.BlockSpec((tm, k), lambda i, j: (i, 0)),
                  pl.BlockSpec((k, tn), lambda i, j: (0, j)),
                  pl.BlockSpec((1, LANE), lambda i, j: (0, 0))],
        out_specs=out_spec,
        out_shape=out_shape,
        compiler_params=_cparams(("arbitrary", "arbitrary")),
        name="proj_heads",
    )(x, w, g2)


def _rms_rows(x, g):
    return x * lax.rsqrt(jnp.mean(x * x, axis=-1, keepdims=True) + EPS) * g


def _rope_rows(x, cos, sin):
    half = x.shape[-1] // 2
    x1, x2 = x[:, :half], x[:, half:]
    return jnp.concatenate([x1 * cos - x2 * sin, x1 * sin + x2 * cos], axis=1)


def _latent(x_ref, g_ref):
    x = jnp.concatenate([x_ref[s] for s in range(x_ref.shape[0])], axis=1)
    return _rms_rows(x, g_ref[...]).astype(BF16)


def _mla_q_kernel(shift_ref, x_ref, g_ref, w_ref, gn_ref, gr_ref, cos_ref, sin_ref, o_ref, *, scale):
    acc = _dot(_latent(x_ref, g_ref), w_ref[...])
    tm = acc.shape[0]
    cos, sin = cos_ref[...], sin_ref[...]
    first = lax.broadcasted_iota(jnp.int32, (tm, LANE - MLA_ROPE), 1) == 0
    pad = jnp.where(first, -shift_ref[0], 0.0)
    for h in range(MLA_HEADS):
        nope = _rms_rows(acc[:, h * MLA_NOPE:(h + 1) * MLA_NOPE], gn_ref[...]) * scale
        r0 = MLA_HEADS * MLA_NOPE + h * MLA_ROPE
        pe = _rope_rows(_rms_rows(acc[:, r0:r0 + MLA_ROPE], gr_ref[...]), cos, sin) * scale
        o_ref[h, 0:MLA_NOPE, :] = nope.T.astype(o_ref.dtype)
        o_ref[h, MLA_NOPE:MLA_NOPE + LANE, :] = jnp.concatenate([pe, pad], axis=1).T.astype(o_ref.dtype)


def _mla_kv_kernel(x_ref, g_ref, w_ref, tail_ref, gn_ref, gr_ref, cos_ref, sin_ref, ok_ref, ov_ref):
    acc = _dot(_latent(x_ref, g_ref), w_ref[...])
    tm = acc.shape[0]
    k_pe = _rope_rows(_rms_rows(tail_ref[0][:, :MLA_ROPE], gr_ref[...]), cos_ref[...], sin_ref[...])
    first = lax.broadcasted_iota(jnp.int32, (tm, LANE - MLA_ROPE), 1) == 0
    k_pe = jnp.concatenate([k_pe, jnp.where(first, 1.0, 0.0)], axis=1).astype(ok_ref.dtype)
    for h in range(MLA_HEADS):
        c0 = h * (MLA_NOPE + MLA_V)
        ok_ref[h, :, 0:MLA_NOPE] = _rms_rows(acc[:, c0:c0 + MLA_NOPE], gn_ref[...]).astype(ok_ref.dtype)
        ok_ref[h, :, MLA_NOPE:MLA_NOPE + LANE] = k_pe
        ov_ref[h] = acc[:, c0 + MLA_NOPE:c0 + MLA_NOPE + MLA_V].T.astype(ov_ref.dtype)


def mla_project(proj, s_cq, s_ckv, s_tail, seq, q_norm_g, kv_norm_g, wq_r, w_ukv, nope_g, rope_g, cos, sin,
                scale, shift, tm=512):
    _, m, _ = proj.shape
    kq, kkv = s_ckv - s_cq, s_tail - s_ckv
    tps = seq // tm
    dqk = MLA_NOPE + LANE
    half = MLA_ROPE // 2
    rope_specs = [pl.BlockSpec((tm, half), lambda i: (i % tps, 0))] * 2
    gain_specs = [pl.BlockSpec((1, MLA_NOPE), lambda i: (0, 0)), pl.BlockSpec((1, MLA_ROPE), lambda i: (0, 0))]
    q_t = pl.pallas_call(
        functools.partial(_mla_q_kernel, scale=scale),
        grid=(m // tm,),
        in_specs=[pl.BlockSpec(memory_space=pltpu.SMEM),
                  pl.BlockSpec((kq, tm, LANE), lambda i: (s_cq // kq, i, 0)),
                  pl.BlockSpec((1, kq * LANE), lambda i: (0, 0)),
                  pl.BlockSpec(wq_r.shape, lambda i: (0, 0))] + gain_specs + rope_specs,
        out_specs=pl.BlockSpec((MLA_HEADS, dqk, tm), lambda i: (0, 0, i)),
        out_shape=jax.ShapeDtypeStruct((MLA_HEADS, dqk, m), BF16),
        compiler_params=_cparams(("arbitrary",)),
        name="mla_q_project",
    )(jnp.reshape(shift, (1,)).astype(F32), proj, q_norm_g.reshape(1, -1), wq_r, nope_g[0].reshape(1, -1),
      rope_g[0].reshape(1, -1), cos, sin)
    k, v_t = pl.pallas_call(
        _mla_kv_kernel,
        grid=(m // tm,),
        in_specs=[pl.BlockSpec((kkv, tm, LANE), lambda i: (s_ckv // kkv, i, 0)),
                  pl.BlockSpec((1, kkv * LANE), lambda i: (0, 0)),
                  pl.BlockSpec(w_ukv.shape, lambda i: (0, 0)),
                  pl.BlockSpec((1, tm, LANE), lambda i: (s_tail, i, 0))] + gain_specs + rope_specs,
        out_specs=[pl.BlockSpec((MLA_HEADS, tm, dqk), lambda i: (0, i, 0)),
                   pl.BlockSpec((MLA_HEADS, MLA_V, tm), lambda i: (0, 0, i))],
        out_shape=[jax.ShapeDtypeStruct((MLA_HEADS, m, dqk), BF16),
                   jax.ShapeDtypeStruct((MLA_HEADS, MLA_V, m), BF16)],
        compiler_params=_cparams(("arbitrary",)),
        name="mla_kv_project",
    )(proj, kv_norm_g.reshape(1, -1), w_ukv, proj, nope_g[1].reshape(1, -1), rope_g[1].reshape(1, -1), cos, sin)
    return q_t, k, v_t


def _resproj_kernel(*refs, npair):
    xres_ref, gate_ref = refs[2 * npair], refs[2 * npair + 1]
    o_ref = refs[2 * npair + 2]
    acc = _dot(refs[0][...], refs[1][...])
    for p in range(1, npair):
        acc = acc + _dot(refs[2 * p][...], refs[2 * p + 1][...])
    o_ref[...] = xres_ref[...] + gate_ref[0] * acc


def resproj(pairs, xres, gate, seq, tm=1024, tn=512):
    m, n = xres.shape
    tpb = seq // tm
    in_specs, args = [], []
    for x, w in pairs:
        k = x.shape[1]
        in_specs += [pl.BlockSpec((tm, k), lambda i, j: (i, 0)),
                     pl.BlockSpec((k, tn), lambda i, j: (0, j))]
        args += [x, w]
    in_specs += [pl.BlockSpec((tm, tn), lambda i, j: (i, j)),
                 pl.BlockSpec((1, 1, tn), lambda i, j: (i // tpb, 0, j))]
    args += [xres, gate.reshape(-1, 1, n)]
    return pl.pallas_call(
        functools.partial(_resproj_kernel, npair=len(pairs)),
        grid=(m // tm, n // tn),
        in_specs=in_specs,
        out_specs=pl.BlockSpec((tm, tn), lambda i, j: (i, j)),
        out_shape=jax.ShapeDtypeStruct((m, n), F32),
        compiler_params=_cparams(("arbitrary", "arbitrary")),
        name="resproj",
    )(*args)


HALO = 8


def _ffn_up_kernel(h_ref, wg_ref, wv_ref, cwg_ref, cwv_ref, cbg_ref, cbv_ref, o_ref,
                   ug_ref, uv_ref, *, tm, tiles_per_seq):
    i = pl.program_id(1)
    first = (i % tiles_per_seq) == 0

    @pl.when(first)
    def _():
        ug_ref[0:HALO, :] = jnp.zeros((HALO, ug_ref.shape[1]), F32)
        uv_ref[0:HALO, :] = jnp.zeros((HALO, uv_ref.shape[1]), F32)

    @pl.when(jnp.logical_not(first))
    def _():
        ug_ref[0:HALO, :] = ug_ref[tm:tm + HALO, :]
        uv_ref[0:HALO, :] = uv_ref[tm:tm + HALO, :]

    h = h_ref[...]
    ug_ref[HALO:HALO + tm, :] = _dot(h, wg_ref[...])
    uv_ref[HALO:HALO + tm, :] = _dot(h, wv_ref[...])

    def conv(u_ref, cw_ref, cb_ref):
        out = cb_ref[...]
        for j in range(CONV_WIDTH):
            off = HALO - (CONV_WIDTH - 1) + j
            out = out + cw_ref[j:j + 1, :] * u_ref[off:off + tm, :]
        return out

    g = conv(ug_ref, cwg_ref, cbg_ref)
    v = conv(uv_ref, cwv_ref, cbv_ref)
    o_ref[...] = (g * jax.nn.sigmoid(g) * v).astype(o_ref.dtype)


def ffn_up(h, w_up, conv_w, conv_b, seq, tm=1024, tn=512):
    m, d = h.shape
    f = w_up.shape[1] // 2
    nj = f // tn
    tps = seq // tm
    cb = conv_b.reshape(1, 2 * f)
    return pl.pallas_call(
        functools.partial(_ffn_up_kernel, tm=tm, tiles_per_seq=tps),
        grid=(nj, m // tm),
        in_specs=[pl.BlockSpec((tm, d), lambda j, i: (i, 0)),
                  pl.BlockSpec((d, tn), lambda j, i: (0, j)),
                  pl.BlockSpec((d, tn), lambda j, i: (0, nj + j)),
                  pl.BlockSpec((CONV_WIDTH, tn), lambda j, i: (0, j)),
                  pl.BlockSpec((CONV_WIDTH, tn), lambda j, i: (0, nj + j)),
                  pl.BlockSpec((1, tn), lambda j, i: (0, j)),
                  pl.BlockSpec((1, tn), lambda j, i: (0, nj + j))],
        out_specs=pl.BlockSpec((tm, tn), lambda j, i: (i, j)),
        out_shape=jax.ShapeDtypeStruct((m, f), BF16),
        scratch_shapes=[pltpu.VMEM((tm + HALO, tn), F32), pltpu.VMEM((tm + HALO, tn), F32)],
        compiler_params=_cparams(("arbitrary", "arbitrary")),
        name="ffn_up_conv",
    )(h, w_up, w_up, conv_w, conv_w, cb, cb)


LOG2E = 1.4426950408889634
CWIN = 16


def _t5_shifted(dist, tbl_ref, h):
    val = jnp.full(dist.shape, tbl_ref[0, h], F32)
    for b in range(1, REL_BUCKETS):
        val = jnp.where(dist >= T5_THR[b - 1], tbl_ref[b, h], val)
    return (val - tbl_ref[REL_BUCKETS - 1, h]) * LOG2E


def _bias_tiles_kernel(tbl_ref, dt_ref, dc_ref):
    h = pl.program_id(0)
    key = lax.broadcasted_iota(jnp.int32, (LANE, LANE), 0)
    q = lax.broadcasted_iota(jnp.int32, (LANE, LANE), 1)
    for rel in range(2):
        dt_ref[0, rel] = _t5_shifted(rel * LANE + q - key, tbl_ref, h)
    dt_ref[0, 2] = jnp.zeros((LANE, LANE), F32)
    u = lax.broadcasted_iota(jnp.int32, (CWIN, LANE), 0)
    qc = lax.broadcasted_iota(jnp.int32, (CWIN, LANE), 1)
    dc_ref[0] = _t5_shifted(qc - CMP_STRIDE * (u - CWIN // 2) - (CMP_BLOCK - 1), tbl_ref, h)


def bias_tiles(rel_bias):
    nh = rel_bias.shape[1]
    return pl.pallas_call(
        _bias_tiles_kernel,
        grid=(nh,),
        in_specs=[pl.BlockSpec(memory_space=pltpu.SMEM)],
        out_specs=[pl.BlockSpec((1, 3, LANE, LANE), lambda h: (h, 0, 0, 0)),
                   pl.BlockSpec((1, CWIN, LANE), lambda h: (h, 0, 0))],
        out_shape=[jax.ShapeDtypeStruct((nh, 3, LANE, LANE), F32),
                   jax.ShapeDtypeStruct((nh, CWIN, LANE), F32)],
        compiler_params=_cparams(("arbitrary",)),
        name="t5_bias_tiles",
    )(rel_bias)


def _compress_kernel(x_ref, pe_ref, w1_ref, b1_ref, w2_ref, b2_ref, g_ref, o_ref, *, half):
    kv = pl.program_id(0)
    x = x_ref[0]
    a = _dot((x + pe_ref[0, :, :half]).astype(BF16), w1_ref[0, :half, :])
    b = _dot((x + pe_ref[0, :, half:]).astype(BF16), w1_ref[0, half:, :])
    b_next = jnp.concatenate([b[1:], jnp.zeros((1, b.shape[1]), F32)], axis=0)
    hid = jax.nn.gelu(a + b_next + b1_ref[0])
    out = _dot(hid.astype(BF16), w2_ref[0]) + b2_ref[0]
    normed = out * lax.rsqrt(jnp.mean(out * out, axis=-1, keepdims=True) + EPS) * g_ref[...]
    out = jnp.where(kv == 0, normed, out)
    o_ref[0, 0] = out.astype(o_ref.dtype)


def compress_kv(proj, slab0, bsz, seq, cmp_pe, cmp_w1, cmp_b1, cmp_w2, cmp_b2, g_k):
    nslab, m, _ = proj.shape
    nchunk = seq // CMP_STRIDE
    half = CMP_STRIDE * HEAD_DIM
    xv = proj.reshape(nslab, m // CMP_STRIDE, half)
    pe = cmp_pe.reshape(2, 1, CMP_BLOCK * HEAD_DIM)
    return pl.pallas_call(
        functools.partial(_compress_kernel, half=half),
        grid=(2, bsz, NSA_GROUPS),
        in_specs=[pl.BlockSpec((1, nchunk, half), lambda kv, b, g: (slab0 + 2 * kv + g, b, 0)),
                  pl.BlockSpec((1, 1, 2 * half), lambda kv, b, g: (kv, 0, 0)),
                  pl.BlockSpec((1, 2 * half, CMP_HIDDEN), lambda kv, b, g: (kv, 0, 0)),
                  pl.BlockSpec((1, 1, CMP_HIDDEN), lambda kv, b, g: (kv, 0, 0)),
                  pl.BlockSpec((1, CMP_HIDDEN, HEAD_DIM), lambda kv, b, g: (kv, 0, 0)),
                  pl.BlockSpec((1, 1, HEAD_DIM), lambda kv, b, g: (kv, 0, 0)),
                  pl.BlockSpec((1, HEAD_DIM), lambda kv, b, g: (0, 0))],
        out_specs=pl.BlockSpec((1, 1, nchunk, HEAD_DIM), lambda kv, b, g: (kv, g, b, 0)),
        out_shape=jax.ShapeDtypeStruct((2, NSA_GROUPS, bsz * nchunk, HEAD_DIM), BF16),
        compiler_params=_cparams(("arbitrary", "arbitrary", "arbitrary")),
        name="nsa_compress",
    )(xv, pe, cmp_w1.astype(BF16), cmp_b1.reshape(2, 1, CMP_HIDDEN), cmp_w2.astype(BF16),
      cmp_b2.reshape(2, 1, HEAD_DIM), g_k.reshape(1, HEAD_DIM))


KW = 512
PV_KEYS = 256


def _tile_lanes(x, n):
    return jnp.concatenate([x] * n, axis=1)


def _flash_init(m_ref, l_ref, acc_ref):
    m_ref[...] = jnp.full(m_ref.shape, NEG, F32)
    l_ref[...] = jnp.zeros(l_ref.shape, F32)
    acc_ref[...] = jnp.zeros(acc_ref.shape, F32)


def _zero_after(x):
    bits = pltpu.bitcast(x, jnp.int32)
    return lax.shift_right_logical(lax.shift_right_logical(bits, 16), 16).astype(F32)


def _flash_update(s, v_t, m_ref, l_ref, acc_ref, col_max=None, after=None):
    m_old = m_ref[...]
    if col_max is None:
        col_max = jnp.max(s, axis=0, keepdims=True)
    m_new = jnp.maximum(m_old, col_max)
    alpha = jnp.exp2(m_old - m_new)
    l_new = alpha * l_ref[...]
    acc = alpha * acc_ref[...]
    nk = s.shape[0]
    for k0 in range(0, nk, PV_KEYS):
        p = jnp.exp2(s[k0:k0 + PV_KEYS] - m_new)
        l_new = l_new + jnp.sum(p, axis=0, keepdims=True)
        acc = acc + _dot(v_t[:, k0:k0 + PV_KEYS], p.astype(BF16))
    l_ref[...] = l_new
    acc_ref[...] = acc
    m_ref[...] = m_new if after is None else m_new + _zero_after(after)


SAFE_LOG2_BOUND = 60.0


def _flash_accumulate(s, v_t, l_ref, acc_ref, after=None):
    l_new = l_ref[...]
    acc = acc_ref[...]
    for k0 in range(0, s.shape[0], PV_KEYS):
        p = jnp.exp2(s[k0:k0 + PV_KEYS])
        l_new = l_new + jnp.sum(p, axis=0, keepdims=True)
        acc = acc + _dot(v_t[:, k0:k0 + PV_KEYS], p.astype(BF16))
    if after is not None:
        l_new = l_new + jnp.max(_zero_after(after), axis=0, keepdims=True)
    l_ref[...] = l_new
    acc_ref[...] = acc


def _sum_result(l_ref, acc_ref):
    den = l_ref[...]
    ok = den > 0.0
    return acc_ref[...] * jnp.where(ok, 1.0 / jnp.where(ok, den, 1.0), 0.0)


def _inv_den(m, den):
    ok = m > 0.5 * NEG
    return jnp.where(ok, 1.0 / jnp.where(ok, den, 1.0), 0.0)


def _flash_result(m_ref, l_ref, acc_ref):
    return acc_ref[...] * _inv_den(m_ref[...], l_ref[...])


def _softmax_cols(s):
    m = jnp.max(s, axis=0, keepdims=True)
    p = jnp.exp2(s - m)
    return p * _inv_den(m, jnp.sum(p, axis=0, keepdims=True))


def _near_bias(dt_ref, heads, qi, kt0, ntile):
    rows = []
    for j in range(ntile):
        rel = jnp.clip(qi - (kt0 + j), 0, 2)
        rows.append(jnp.concatenate([dt_ref[h, rel] for h in heads], axis=1))
    return jnp.concatenate(rows, axis=0)


def _pipelined_chunks(n, qk_stage, soft_stage):
    @pl.when(n > 0)
    def _():
        qk_stage(0, 0)

    def pair(p, x):
        c = 2 * p
        ahead = qk_stage(c + 1, 1)
        soft_stage(c, 0, ahead)
        ahead = qk_stage(jnp.minimum(c + 2, n - 1), 0)
        soft_stage(c + 1, 1, ahead)
        return x

    lax.fori_loop(0, n // 2, pair, 0)

    @pl.when(n % 2 == 1)
    def _():
        soft_stage(n - 1, 0, None)


def _nsa_kernel(bound_ref, qt_ref, gt_ref, kc_ref, vct_ref, ks_ref, vst_ref, kw_ref, vwt_ref,
                dt_ref, dc_ref, ext_ref, o_ref,
                sc_ref, ps_ref, m_ref, l_ref, acc_ref, sbuf0, sbuf1, cbuf0, cbuf1, *, seq, nc):
    sbuf, cbuf = (sbuf0, sbuf1), (cbuf0, cbuf1)
    qi = pl.program_id(2)
    q0 = qi * QB
    hpg = NSA_HPG
    heads = list(range(hpg))
    ncp = kc_ref.shape[1]
    ns = seq // SEL_BLOCK
    q_t = jnp.concatenate([qt_ref[h] for h in heads], axis=1)

    pad = CWIN // 2
    sc_ref[0:pad, :] = jnp.zeros((pad, hpg * QB), F32)
    sc_ref[pad + ncp:2 * pad + ncp, :] = jnp.zeros((pad, hpg * QB), F32)
    sc_ref[pad:pad + ncp, :] = _dot(kc_ref[0], q_t)
    r0 = pl.multiple_of(qi * (QB // CMP_STRIDE), 8)
    sc_ref[pl.ds(r0, CWIN), :] = sc_ref[pl.ds(r0, CWIN), :] + jnp.concatenate(
        [dc_ref[h] for h in heads], axis=1)
    ci = lax.broadcasted_iota(jnp.int32, (ncp, QB), 0)
    tc = q0 + lax.broadcasted_iota(jnp.int32, (ncp, QB), 1)
    valid_c = (ci * CMP_STRIDE + CMP_BLOCK - 1 <= tc) & (ci < nc)
    p_c = _softmax_cols(sc_ref[pad:pad + ncp, :] + _tile_lanes(jnp.where(valid_c, 0.0, NEG), hpg))
    oc_t = _dot(vct_ref[0], p_c.astype(BF16))
    p_sum = p_c[:, 0:QB]
    for h in range(1, hpg):
        p_sum = p_sum + p_c[:, h * QB:(h + 1) * QB]

    wkeys = WINDOW + QB
    start = pl.multiple_of(jnp.maximum(q0 - WINDOW, 0), LANE)
    s_w = _dot(kw_ref[0, pl.ds(start, wkeys), :], q_t)
    s_w = s_w + _near_bias(dt_ref, heads, qi, start // LANE, wkeys // LANE)
    dist_w = (q0 + lax.broadcasted_iota(jnp.int32, (wkeys, QB), 1)) - (
        start + lax.broadcasted_iota(jnp.int32, (wkeys, QB), 0))
    mask_w = (dist_w >= 0) & (dist_w < WINDOW)
    p_w = _softmax_cols(s_w + _tile_lanes(jnp.where(mask_w, 0.0, NEG), hpg))
    ow_t = _dot(vwt_ref[0, :, pl.ds(start, wkeys)], p_w.astype(BF16))

    ps_ref[0:8, :] = jnp.zeros((8, QB), F32)
    ps_ref[8:8 + ncp, :] = p_sum
    per = SEL_BLOCK // CMP_STRIDE
    band = [ps_ref[pl.ds(8 + r, ns, stride=per), :] for r in range(-1, per)]
    imp = 0.5 * band[0] + band[1] + band[2] + band[3] + 0.5 * band[4]
    if ns < LANE:
        imp = jnp.concatenate([imp, jnp.zeros((LANE - ns, QB), F32)], axis=0)
    blk = lax.broadcasted_iota(jnp.int32, (LANE, QB), 0)
    t = q0 + lax.broadcasted_iota(jnp.int32, (LANE, QB), 1)
    tb = t // SEL_BLOCK
    forced = (blk == 0) | (blk == tb) | (blk == tb - 1)
    score = jnp.where(forced, FORCE, jnp.where(blk * SEL_BLOCK <= t, imp, NEG))
    score = jnp.where(blk < ns, score, -jnp.inf)
    blk_f = blk.astype(F32)
    sel = jnp.zeros((LANE, QB), F32)
    for _ in range(min(SEL_TOP_N, ns)):
        mx = jnp.max(score, axis=0, keepdims=True)
        first = jnp.min(jnp.where(score == mx, blk_f, float(LANE)), axis=0, keepdims=True)
        pick = blk_f == first
        sel = jnp.where(pick, 1.0, sel)
        score = jnp.where(pick, -jnp.inf, score)
    sel_b = sel.astype(BF16)

    kpos = lax.broadcasted_iota(jnp.int32, (KW, QB), 0)
    tq = q0 + lax.broadcasted_iota(jnp.int32, (KW, QB), 1)
    bounded_ok = bound_ref[0] <= SAFE_LOG2_BOUND
    shift = jnp.where(bounded_ok, bound_ref[0], 0.0)
    c_near = jnp.maximum(qi - 1, 0) // (KW // LANE)

    def block_mask(c0):
        chosen = _dot(ext_ref[pl.ds(c0, KW), :], sel_b)
        return (chosen - 1.0) * (-NEG) - shift

    def attend(bounded):
        _flash_init(m_ref, l_ref, acc_ref)

        def qk_stage(c, buf):
            c0 = pl.multiple_of(c * KW, KW)
            s = _dot(ks_ref[0, pl.ds(c0, KW), :], q_t) + _tile_lanes(block_mask(c0), hpg)
            sbuf[buf][...] = s
            if bounded:
                return s[KW - 8:KW]
            col_max = jnp.max(s, axis=0, keepdims=True)
            cbuf[buf][...] = col_max
            return col_max

        def soft_stage(c, buf, ahead):
            c0 = pl.multiple_of(c * KW, KW)
            if bounded:
                _flash_accumulate(sbuf[buf][...], vst_ref[0, :, pl.ds(c0, KW)], l_ref, acc_ref, after=ahead)
            else:
                _flash_update(sbuf[buf][...], vst_ref[0, :, pl.ds(c0, KW)], m_ref, l_ref, acc_ref,
                              col_max=cbuf[buf][...], after=ahead)

        def near_step(c, x):
            c0 = pl.multiple_of(c * KW, KW)
            madd = block_mask(c0) + jnp.where(c0 + kpos <= tq, 0.0, NEG)
            s = (_dot(ks_ref[0, pl.ds(c0, KW), :], q_t) + _tile_lanes(madd, hpg)
                 + _near_bias(dt_ref, heads, qi, c * (KW // LANE), KW // LANE))
            if bounded:
                _flash_accumulate(s, vst_ref[0, :, pl.ds(c0, KW)], l_ref, acc_ref)
            else:
                _flash_update(s, vst_ref[0, :, pl.ds(c0, KW)], m_ref, l_ref, acc_ref)
            return x

        _pipelined_chunks(c_near, qk_stage, soft_stage)
        lax.fori_loop(c_near, qi // (KW // LANE) + 1, near_step, 0)
        acc_ref[...] = _sum_result(l_ref, acc_ref) if bounded else _flash_result(m_ref, l_ref, acc_ref)

    pl.when(bounded_ok)(lambda: attend(True))
    pl.when(jnp.logical_not(bounded_ok))(lambda: attend(False))
    os_t = acc_ref[...]

    gates = jax.nn.sigmoid(gt_ref[0])
    for h in heads:
        sl = slice(h * QB, (h + 1) * QB)
        o_t = (gates[3 * h:3 * h + 1] * oc_t[:, sl] + gates[3 * h + 1:3 * h + 2] * os_t[:, sl]
               + gates[3 * h + 2:3 * h + 3] * ow_t[:, sl])
        o_ref[:, h * HEAD_DIM:(h + 1) * HEAD_DIM] = o_t.T.astype(o_ref.dtype)


def nsa_attention(logit_bound, q_t, gates_t, kc, vc_t, k_sw, v_sw_t, dt, dc, bsz, seq):
    nq = seq // QB
    ncp = seq // CMP_STRIDE
    nc = ncp - 1
    ns = seq // SEL_BLOCK
    assert ns <= LANE and seq >= WINDOW + QB and seq % KW == 0
    assert CMP_BLOCK == 2 * CMP_STRIDE and SEL_BLOCK == 4 * CMP_STRIDE
    expand =((np.arange(seq)[:, None] // SEL_BLOCK) == np.arange(LANE)[None, :]).astype(np.float32)
    ng = NSA_GROUPS
    ks_spec = pl.BlockSpec((1, seq, HEAD_DIM), lambda b, g, i: (g, b, 0))
    kw_spec = pl.BlockSpec((1, seq, HEAD_DIM), lambda b, g, i: (ng + g, b, 0))
    vs_spec = pl.BlockSpec((1, HEAD_DIM, seq), lambda b, g, i: (g, 0, b))
    vw_spec = pl.BlockSpec((1, HEAD_DIM, seq), lambda b, g, i: (ng + g, 0, b))
    lanes = NSA_HPG * QB
    return pl.pallas_call(
        functools.partial(_nsa_kernel, seq=seq, nc=nc),
        grid=(bsz, NSA_GROUPS, nq),
        in_specs=[pl.BlockSpec(memory_space=pltpu.SMEM),
                  pl.BlockSpec((NSA_HPG, HEAD_DIM, QB), lambda b, g, i: (g, 0, b * nq + i)),
                  pl.BlockSpec((1, 16, QB), lambda b, g, i: (g, 0, b * nq + i)),
                  pl.BlockSpec((1, ncp, HEAD_DIM), lambda b, g, i: (g, b, 0)),
                  pl.BlockSpec((1, HEAD_DIM, ncp), lambda b, g, i: (g, 0, b)),
                  ks_spec, vs_spec, kw_spec, vw_spec,
                  pl.BlockSpec((NSA_HPG, 3, LANE, LANE), lambda b, g, i: (g, 0, 0, 0)),
                  pl.BlockSpec((NSA_HPG, CWIN, LANE), lambda b, g, i: (g, 0, 0)),
                  pl.BlockSpec((seq, LANE), lambda b, g, i: (0, 0))],
        out_specs=pl.BlockSpec((QB, NSA_HPG * HEAD_DIM), lambda b, g, i: (b * nq + i, g)),
        out_shape=jax.ShapeDtypeStruct((bsz * seq, NSA_HEADS * HEAD_DIM), BF16),
        scratch_shapes=[pltpu.VMEM((ncp + CWIN, lanes), F32), pltpu.VMEM((ncp + 8, QB), F32),
                        pltpu.VMEM((1, lanes), F32), pltpu.VMEM((1, lanes), F32),
                        pltpu.VMEM((HEAD_DIM, lanes), F32),
                        pltpu.VMEM((KW, lanes), F32), pltpu.VMEM((KW, lanes), F32),
                        pltpu.VMEM((1, lanes), F32), pltpu.VMEM((1, lanes), F32)],
        compiler_params=_cparams(("arbitrary", "arbitrary", "arbitrary")),
        name="nsa_attention",
    )(jnp.reshape(logit_bound, (1,)).astype(F32), q_t, gates_t, kc, vc_t, k_sw, v_sw_t, k_sw, v_sw_t, dt, dc,
      jnp.asarray(expand, BF16))


MLA_HPS = 2


def _mla_kernel(bound_ref, qt_ref, k_ref, vt_ref, o_ref, *scratch):
    qi = pl.program_id(2)
    chains = [scratch[3 * h:3 * h + 3] for h in range(MLA_HPS)]
    sbuf = [scratch[(3 + b) * MLA_HPS:(4 + b) * MLA_HPS] for b in range(2)]
    cbuf = [scratch[(5 + b) * MLA_HPS:(6 + b) * MLA_HPS] for b in range(2)]
    c_diag = pl.multiple_of(qi * KW, KW)
    kpos = lax.broadcasted_iota(jnp.int32, (KW, KW), 0)
    tq = lax.broadcasted_iota(jnp.int32, (KW, KW), 1)
    dv = vt_ref.shape[1]

    def attend(bounded):
        for ch in chains:
            _flash_init(*ch)

        def qk_stage(c, buf):
            c0 = pl.multiple_of(c * KW, KW)
            ahead = []
            for h in range(MLA_HPS):
                s = _dot(k_ref[h, pl.ds(c0, KW), :], qt_ref[h])
                sbuf[buf][h][...] = s
                if bounded:
                    ahead.append(s[KW - 8:KW])
                else:
                    ahead.append(jnp.max(s, axis=0, keepdims=True))
                    cbuf[buf][h][...] = ahead[-1]
            return ahead

        def soft_stage(c, buf, ahead):
            c0 = pl.multiple_of(c * KW, KW)
            for h, (m_ref, l_ref, acc_ref) in enumerate(chains):
                after = None if ahead is None else ahead[h]
                if bounded:
                    _flash_accumulate(sbuf[buf][h][...], vt_ref[h, :, pl.ds(c0, KW)], l_ref, acc_ref, after=after)
                else:
                    _flash_update(sbuf[buf][h][...], vt_ref[h, :, pl.ds(c0, KW)], m_ref, l_ref, acc_ref,
                                  col_max=cbuf[buf][h][...], after=after)

        _pipelined_chunks(qi, qk_stage, soft_stage)
        causal = jnp.where(kpos <= tq, 0.0, NEG)
        scores = [_dot(k_ref[h, pl.ds(c_diag, KW), :], qt_ref[h]) + causal for h in range(MLA_HPS)]
        for h, (m_ref, l_ref, acc_ref) in enumerate(chains):
            if bounded:
                _flash_accumulate(scores[h], vt_ref[h, :, pl.ds(c_diag, KW)], l_ref, acc_ref)
                o_t = _sum_result(l_ref, acc_ref)
            else:
                _flash_update(scores[h], vt_ref[h, :, pl.ds(c_diag, KW)], m_ref, l_ref, acc_ref)
                o_t = _flash_result(m_ref, l_ref, acc_ref)
            o_ref[:, h * dv:(h + 1) * dv] = o_t.T.astype(o_ref.dtype)

    bounded_ok = bound_ref[0] <= SAFE_LOG2_BOUND
    pl.when(bounded_ok)(lambda: attend(True))
    pl.when(jnp.logical_not(bounded_ok))(lambda: attend(False))


def mla_attention(logit_bound, q_t, k, v_t, bsz, seq):
    nh, dqk, _ = q_t.shape
    dv = v_t.shape[1]
    nq = seq // KW
    hps = MLA_HPS
    state = [pltpu.VMEM((1, KW), F32), pltpu.VMEM((1, KW), F32), pltpu.VMEM((dv, KW), F32)] * hps
    state += [pltpu.VMEM((KW, KW), F32)] * (2 * hps)
    state += [pltpu.VMEM((1, KW), F32)] * (2 * hps)
    return pl.pallas_call(
        _mla_kernel,
        grid=(bsz, nh // hps, nq),
        in_specs=[pl.BlockSpec(memory_space=pltpu.SMEM),
                  pl.BlockSpec((hps, dqk, KW), lambda b, h, i: (h, 0, b * nq + i)),
                  pl.BlockSpec((hps, seq, dqk), lambda b, h, i: (h, b, 0)),
                  pl.BlockSpec((hps, dv, seq), lambda b, h, i: (h, 0, b))],
        out_specs=pl.BlockSpec((KW, hps * dv), lambda b, h, i: (b * nq + i, h)),
        out_shape=jax.ShapeDtypeStruct((bsz * seq, nh * dv), BF16),
        scratch_shapes=state,
        compiler_params=_cparams(("arbitrary", "arbitrary", "arbitrary")),
        name="mla_attention",
    )(jnp.reshape(logit_bound, (1,)).astype(F32), q_t, k, v_t)


INT_MIN = -2 ** 31
NEG_KEY = int(np.array(NEG, np.float32).view(np.int32)) ^ 0x7FFFFFFF
KEY_BITS = 32


def _sort_key(x):
    bits = pltpu.bitcast(x + 0.0, jnp.int32)
    return jnp.where(bits < 0, bits ^ 0x7FFFFFFF, bits)


def _dsa_kernel(bound_ref, iqt_ref, iwt_ref, ik_ref, qt_ref, k_ref, vt_ref, dt_ref, o_ref,
                key_ref, madd_ref, *state, seq, k_sel):
    qi = pl.program_id(1)
    q0 = qi * QB
    n_chunk = (q0 + QB + KW - 1) // KW
    n_rest = seq - n_chunk * KW
    kpos = lax.broadcasted_iota(jnp.int32, (KW, QB), 0)
    tq = q0 + lax.broadcasted_iota(jnp.int32, (KW, QB), 1)
    hpp = KW // QB

    def score_chunk(c, x):
        c0 = pl.multiple_of(c * KW, KW)
        ikc = ik_ref[pl.ds(c0, KW), :]
        acc = jnp.zeros((KW, QB), F32)
        for piece in range(IDX_HEADS // hpp):
            sl = slice(piece * KW, (piece + 1) * KW)
            s = jnp.maximum(_dot(ikc, iqt_ref[0, :, sl]), 0.0) * iwt_ref[0, :, sl]
            for j in range(hpp):
                acc = acc + s[:, j * QB:(j + 1) * QB]
        acc = jnp.where(c0 + kpos <= tq, acc, NEG)
        key_ref[pl.ds(c0, KW), :] = _sort_key(acc)
        return x

    lax.fori_loop(0, n_chunk, score_chunk, 0)

    def count(pred):
        def body(c, acc):
            c0 = pl.multiple_of(c * KW, KW)
            hit = jnp.where(pred(key_ref[pl.ds(c0, KW), :], c0), 1.0, 0.0)
            parts = [hit[8 * i:8 * (i + 1)] for i in range(KW // 8)]
            while len(parts) > 1:
                parts = [parts[i] + parts[i + 1] for i in range(0, len(parts), 2)]
            return acc + parts[0]
        acc = lax.fori_loop(0, n_chunk, body, jnp.zeros((8, QB), F32))
        return jnp.sum(acc, axis=0, keepdims=True)

    rest = n_rest.astype(F32)
    kf = float(k_sel)

    def bit_step(i, u):
        bit = jnp.left_shift(jnp.int32(1), 31 - i)
        trial = (u | bit) ^ INT_MIN
        cnt = count(lambda keys, c0: keys >= trial) + jnp.where(NEG_KEY >= trial, rest, 0.0)
        return jnp.where(cnt >= kf, u | bit, u)

    u = lax.fori_loop(0, KEY_BITS, bit_step, jnp.zeros((1, QB), jnp.int32))
    thr = u ^ INT_MIN
    cnt_gt = count(lambda keys, c0: keys > thr) + jnp.where(NEG_KEY > thr, rest, 0.0)
    cnt_ge = count(lambda keys, c0: keys >= thr) + jnp.where(NEG_KEY >= thr, rest, 0.0)
    need = kf - cnt_gt
    tie_q = (cnt_ge > kf) & (thr != NEG_KEY)
    idx_bits = (seq - 1).bit_length()
    no_cut = 2 ** 30

    def tie_cut():
        def idx_step(i, x):
            bit = jnp.left_shift(jnp.int32(1), idx_bits - 1 - i)
            trial = x | bit
            f = count(lambda keys, c0: (keys == thr) & (c0 + kpos < trial))
            return jnp.where(f <= need - 1.0, trial, x)
        return lax.fori_loop(0, idx_bits, idx_step, jnp.zeros((1, QB), jnp.int32))

    any_tie = jnp.max(jnp.where(tie_q, 1.0, 0.0)) > 0.0
    x_cut = lax.cond(any_tie, tie_cut, lambda: jnp.full((1, QB), no_cut, jnp.int32))
    x_cut = jnp.where(tie_q, x_cut, no_cut)

    def mask_chunk(c, x):
        c0 = pl.multiple_of(c * KW, KW)
        keys = key_ref[pl.ds(c0, KW), :]
        pos = c0 + kpos
        chosen = (keys > thr) | ((keys == thr) & (pos <= x_cut))
        madd_ref[pl.ds(c0, KW), :] = jnp.where(chosen & (pos <= tq), -shift, NEG)
        return x

    bounded_ok = bound_ref[0] <= SAFE_LOG2_BOUND
    shift = jnp.where(bounded_ok, bound_ref[0], 0.0)
    lax.fori_loop(0, n_chunk, mask_chunk, 0)

    c_near = jnp.maximum(qi - 1, 0) // (KW // LANE)
    ng = DSA_KV_HEADS
    chains = [state[3 * g:3 * g + 3] for g in range(ng)]
    sbuf = [state[(3 + b) * ng:(4 + b) * ng] for b in range(2)]
    cbuf = [state[(5 + b) * ng:(6 + b) * ng] for b in range(2)]
    group_heads = [[g * DSA_HPG + h for h in range(DSA_HPG)] for g in range(ng)]

    def raw_scores(c0, g):
        q_t = jnp.concatenate([qt_ref[h] for h in group_heads[g]], axis=1)
        return _dot(k_ref[g, pl.ds(c0, KW), :], q_t)

    def attend(bounded):
        for ch in chains:
            _flash_init(*ch)

        def qk_stage(c, buf):
            c0 = pl.multiple_of(c * KW, KW)
            madd = _tile_lanes(madd_ref[pl.ds(c0, KW), :], DSA_HPG)
            ahead = []
            for g in range(ng):
                s = raw_scores(c0, g) + madd
                sbuf[buf][g][...] = s
                if bounded:
                    ahead.append(s[KW - 8:KW])
                else:
                    ahead.append(jnp.max(s, axis=0, keepdims=True))
                    cbuf[buf][g][...] = ahead[-1]
            return ahead

        def soft_stage(c, buf, ahead):
            c0 = pl.multiple_of(c * KW, KW)
            for g, (m_ref, l_ref, acc_ref) in enumerate(chains):
                after = None if ahead is None else ahead[g]
                if bounded:
                    _flash_accumulate(sbuf[buf][g][...], vt_ref[g, :, pl.ds(c0, KW)], l_ref, acc_ref, after=after)
                else:
                    _flash_update(sbuf[buf][g][...], vt_ref[g, :, pl.ds(c0, KW)], m_ref, l_ref, acc_ref,
                                  col_max=cbuf[buf][g][...], after=after)

        _pipelined_chunks(c_near, qk_stage, soft_stage)

        def near_step(c, x):
            c0 = pl.multiple_of(c * KW, KW)
            madd = _tile_lanes(madd_ref[pl.ds(c0, KW), :], DSA_HPG)
            scores = [raw_scores(c0, g) + madd
                      + _near_bias(dt_ref, group_heads[g], qi, c * (KW // LANE), KW // LANE)
                      for g in range(ng)]
            for g, (m_ref, l_ref, acc_ref) in enumerate(chains):
                if bounded:
                    _flash_accumulate(scores[g], vt_ref[g, :, pl.ds(c0, KW)], l_ref, acc_ref)
                else:
                    _flash_update(scores[g], vt_ref[g, :, pl.ds(c0, KW)], m_ref, l_ref, acc_ref)
            return x

        lax.fori_loop(c_near, n_chunk, near_step, 0)
        for g, (m_ref, l_ref, acc_ref) in enumerate(chains):
            o_t = _sum_result(l_ref, acc_ref) if bounded else _flash_result(m_ref, l_ref, acc_ref)
            for h in range(DSA_HPG):
                hh = group_heads[g][h]
                o_ref[:, hh * HEAD_DIM:(hh + 1) * HEAD_DIM] = o_t[:, h * QB:(h + 1) * QB].T.astype(o_ref.dtype)

    pl.when(bounded_ok)(lambda: attend(True))
    pl.when(jnp.logical_not(bounded_ok))(lambda: attend(False))


def _idx_prep_kernel(p_ref, c_ref, sa_ref, sb_ref, iqt_ref, ik_ref, iwt_ref, *, ntile):
    nslab_q = IDX_HEADS * IDX_DIM // LANE
    per = LANE // IDX_DIM
    half = IDX_ROPE // 2
    zrows = jnp.zeros((LANE - IDX_DIM, QB), F32)

    def rope_slab(x, c, sa, sb):
        return x * c + pltpu.roll(x, LANE - half, axis=1) * sa + pltpu.roll(x, half, axis=1) * sb

    for t in range(ntile):
        rows = slice(t * QB, (t + 1) * QB)
        c, sa, sb = c_ref[rows, :], sa_ref[rows, :], sb_ref[rows, :]
        cols = []
        for s in range(nslab_q):
            x_t = (rope_slab(p_ref[s, rows, :], c, sa, sb) * IDX_DIM ** -0.5).T
            for j in range(per):
                cols.append(jnp.concatenate([x_t[j * IDX_DIM:(j + 1) * IDX_DIM], zrows], axis=0))
        iqt_ref[t] = jnp.concatenate(cols, axis=1).astype(iqt_ref.dtype)
        tail = p_ref[nslab_q, rows, :]
        lane = lax.broadcasted_iota(jnp.int32, (QB, LANE), 1)
        ik_ref[rows, :] = jnp.where(lane < IDX_DIM, rope_slab(tail, c, sa, sb), 0.0).astype(ik_ref.dtype)
        w_t = (tail * IDX_HEADS ** -0.5).T
        iwt_ref[t] = jnp.concatenate([w_t[IDX_DIM + h:IDX_DIM + h + 1, :] for h in range(IDX_HEADS)], axis=1)


def indexer_operands(proj, seq, tm=512):
    _, m, _ = proj.shape
    ntile = tm // QB
    tps = seq // tm
    cos, sin = _rope_tables(seq, IDX_ROPE)
    zero = jnp.zeros_like(sin)
    rest = IDX_DIM - IDX_ROPE
    per = LANE // IDX_DIM
    c_tab = jnp.tile(jnp.concatenate([cos, cos, jnp.ones((seq, rest), F32)], axis=1), (1, per))
    sa_tab = jnp.tile(jnp.concatenate([-sin, zero, jnp.zeros((seq, rest), F32)], axis=1), (1, per))
    sb_tab = jnp.tile(jnp.concatenate([zero, sin, jnp.zeros((seq, rest), F32)], axis=1), (1, per))
    lanes = IDX_HEADS * QB
    tab_spec = pl.BlockSpec((tm, LANE), lambda i: (i % tps, 0))
    return pl.pallas_call(
        functools.partial(_idx_prep_kernel, ntile=ntile),
        grid=(m // tm,),
        in_specs=[pl.BlockSpec((proj.shape[0], tm, LANE), lambda i: (0, i, 0)), tab_spec, tab_spec, tab_spec],
        out_specs=[pl.BlockSpec((ntile, LANE, lanes), lambda i: (i, 0, 0)),
                   pl.BlockSpec((tm, LANE), lambda i: (i, 0)),
                   pl.BlockSpec((ntile, 1, lanes), lambda i: (i, 0, 0))],
        out_shape=[jax.ShapeDtypeStruct((m // QB, LANE, lanes), BF16),
                   jax.ShapeDtypeStruct((m, LANE), BF16),
                   jax.ShapeDtypeStruct((m // QB, 1, lanes), F32)],
        compiler_params=_cparams(("arbitrary",)),
        name="dsa_indexer_operands",
    )(proj, c_tab, sa_tab, sb_tab)


def dsa_attention(logit_bound, iq_t, iw_t, ik, q_t, k, v_t, dt, bsz, seq):
    nq = seq // QB
    k_sel = min(DSA_TOPK_MAX, seq // 4)
    assert seq % KW == 0
    lanes = DSA_HPG * QB
    return pl.pallas_call(
        functools.partial(_dsa_kernel, seq=seq, k_sel=k_sel),
        grid=(bsz, nq),
        in_specs=[pl.BlockSpec(memory_space=pltpu.SMEM),
                  pl.BlockSpec((1, LANE, IDX_HEADS * QB), lambda b, i: (b * nq + i, 0, 0)),
                  pl.BlockSpec((1, 1, IDX_HEADS * QB), lambda b, i: (b * nq + i, 0, 0)),
                  pl.BlockSpec((seq, LANE), lambda b, i: (b, 0)),
                  pl.BlockSpec((DSA_HEADS, HEAD_DIM, QB), lambda b, i: (0, 0, b * nq + i)),
                  pl.BlockSpec((DSA_KV_HEADS, seq, HEAD_DIM), lambda b, i: (0, b, 0),
                               pipeline_mode=pl.Buffered(1)),
                  pl.BlockSpec((DSA_KV_HEADS, HEAD_DIM, seq), lambda b, i: (0, 0, b),
                               pipeline_mode=pl.Buffered(1)),
                  pl.BlockSpec((DSA_HEADS, 3, LANE, LANE), lambda b, i: (0, 0, 0, 0),
                               pipeline_mode=pl.Buffered(1))],
        out_specs=pl.BlockSpec((QB, DSA_HEADS * HEAD_DIM), lambda b, i: (b * nq + i, 0)),
        out_shape=jax.ShapeDtypeStruct((bsz * seq, DSA_HEADS * HEAD_DIM), BF16),
        scratch_shapes=[pltpu.VMEM((seq, QB), jnp.int32), pltpu.VMEM((seq, QB), F32)]
        + [pltpu.VMEM((1, lanes), F32), pltpu.VMEM((1, lanes), F32),
           pltpu.VMEM((HEAD_DIM, lanes), F32)] * DSA_KV_HEADS
        + [pltpu.VMEM((KW, lanes), F32)] * (2 * DSA_KV_HEADS)
        + [pltpu.VMEM((1, lanes), F32)] * (2 * DSA_KV_HEADS),
        compiler_params=_cparams(("arbitrary", "arbitrary")),
        name="dsa_attention",
    )(jnp.reshape(logit_bound, (1,)).astype(F32), iq_t, iw_t, ik, q_t, k, v_t, dt)


def _rope_tables(seq, dim):
    half = dim // 2
    inv = ROPE_THETA ** (-jnp.arange(half, dtype=F32) / half)
    ang = jnp.arange(seq, dtype=F32)[:, None] * inv[None, :]
    return jnp.cos(ang), jnp.sin(ang)


def _logit_bound(gq, gk, dim, scale):
    return dim * scale * jnp.max(jnp.abs(gq)) * jnp.max(jnp.abs(gk)) * (1.0 + 2.0 ** -7)


def _pad_cols(w, n):
    return jnp.pad(w, ((0, 0), (0, n - w.shape[1])))


def _t(x):
    return jnp.swapaxes(x, -1, -2)


def _even_mixer(h, x2, gate, dt, dc, bias_bound, bsz, seq, w_in, w_out, nsa_qk_g, cmp_pe, cmp_w1, cmp_b1,
                cmp_w2, cmp_b2, q_norm_g, kv_norm_g, w_uq, w_ukv, nope_g, rope_g):
    m = bsz * seq
    nq_cols = NSA_HEADS * HEAD_DIM
    nkv_cols = 6 * NSA_GROUPS * HEAD_DIM
    ngate = 3 * NSA_HEADS
    o_gate = nq_cols + nkv_cols
    o_cq = o_gate + ngate
    o_ckv = o_cq + MLA_Q_RANK
    o_kpe = o_ckv + MLA_KV_RANK
    gw = NSA_GROUPS * HEAD_DIM
    kvw = [w_in[:, nq_cols + i * gw:nq_cols + (i + 1) * gw] for i in range(6)]
    scale = HEAD_DIM ** -0.5 * LOG2E
    q_t = proj_heads(h, w_in[:, :nq_cols].astype(BF16), nsa_qk_g[0] * scale, transpose=True)
    k_sw = proj_heads(h, jnp.concatenate([kvw[2], kvw[4]], axis=1).astype(BF16), nsa_qk_g[1],
                      transpose=False)
    v_sw_t = proj_heads(h, jnp.concatenate([kvw[3], kvw[5]], axis=1).astype(BF16), transpose=True)
    tail = jnp.concatenate([w_in[:, o_kpe:], w_in[:, o_gate:o_cq]], axis=1)
    w_r = jnp.concatenate([kvw[0], kvw[1], w_in[:, o_cq:o_kpe], _pad_cols(tail, LANE)], axis=1).astype(BF16)
    proj = proj_slabs(h, w_r, tn=w_r.shape[1])
    s_cq = 2 * NSA_GROUPS
    s_ckv = s_cq + MLA_Q_RANK // LANE
    s_tail = s_ckv + MLA_KV_RANK // LANE
    kvc = compress_kv(proj, 0, bsz, seq, cmp_pe, cmp_w1, cmp_b1, cmp_w2, cmp_b2, nsa_qk_g[1])
    tail_v = proj[s_tail]
    gates = tail_v[:, MLA_ROPE:MLA_ROPE + ngate].reshape(m, NSA_GROUPS, 3 * NSA_HPG)
    gates_t = jnp.pad(jnp.transpose(gates, (1, 2, 0)), ((0, 0), (0, 16 - 3 * NSA_HPG), (0, 0)))
    nsa_bound = _logit_bound(nsa_qk_g[0], nsa_qk_g[1], HEAD_DIM, scale) + bias_bound
    o_nsa = nsa_attention(nsa_bound, q_t, gates_t, kvc[0], _t(kvc[1]), k_sw, v_sw_t,
                          dt[:NSA_HEADS], dc[:NSA_HEADS], bsz, seq)

    dq = MLA_NOPE + MLA_ROPE
    wq = w_uq.reshape(MLA_Q_RANK, MLA_HEADS, dq)
    wq_r = jnp.concatenate([wq[:, :, :MLA_NOPE].reshape(MLA_Q_RANK, -1),
                            wq[:, :, MLA_NOPE:].reshape(MLA_Q_RANK, -1)], axis=1).astype(BF16)
    cos, sin = _rope_tables(seq, MLA_ROPE)
    mscale = dq ** -0.5 * LOG2E
    side = [jnp.sqrt(MLA_NOPE * jnp.max(jnp.abs(nope_g[i])) ** 2 + MLA_ROPE * jnp.max(jnp.abs(rope_g[i])) ** 2)
            for i in range(2)]
    mla_bound = mscale * side[0] * side[1] * (1.0 + 2.0 ** -7)
    mla_shift = jnp.where(mla_bound <= SAFE_LOG2_BOUND, mla_bound, 0.0)
    q_mla_t, k_mla, v_mla_t = mla_project(proj, s_cq, s_ckv, s_tail, seq, q_norm_g, kv_norm_g, wq_r,
                                          w_ukv.astype(BF16), nope_g, rope_g, cos, sin, mscale, mla_shift)
    o_mla = mla_attention(mla_bound, q_mla_t, k_mla, v_mla_t, bsz, seq)
    w_o = w_out.astype(BF16)
    return resproj([(o_nsa, w_o[:nq_cols]), (o_mla, w_o[nq_cols:])], x2, gate, seq)


def _odd_mixer(h, x2, gate, dt, bias_bound, bsz, seq, w_in, w_out, qk_g):
    nq = DSA_HEADS * HEAD_DIM
    nkv = DSA_KV_HEADS * HEAD_DIM
    niq = IDX_HEADS * IDX_DIM
    o_k, o_v, o_iq = nq, nq + nkv, nq + 2 * nkv
    q_t = proj_heads(h, w_in[:, :o_k].astype(BF16), qk_g[0] * (HEAD_DIM ** -0.5 * LOG2E), transpose=True)
    k = proj_heads(h, w_in[:, o_k:o_v].astype(BF16), qk_g[1], transpose=False)
    v_t = proj_heads(h, w_in[:, o_v:o_iq].astype(BF16), transpose=True)
    w_idx = w_in[:, o_iq:]
    proj = proj_slabs(h, _pad_cols(w_idx, niq + LANE).astype(BF16), tn=niq + LANE)
    iq_t, ik, iw_t = indexer_operands(proj, seq)
    bound = _logit_bound(qk_g[0], qk_g[1], HEAD_DIM, HEAD_DIM ** -0.5 * LOG2E) + bias_bound
    o = dsa_attention(bound, iq_t, iw_t, ik, q_t, k, v_t, dt, bsz, seq)
    return resproj([(o, w_out.astype(BF16))], x2, gate, seq)


def _conv_ffn(h, x2, gate, seq, w_up, conv_w, conv_b, w_down):
    a = ffn_up(h, w_up.astype(BF16), conv_w, conv_b, seq)
    return resproj([(a, w_down.astype(BF16))], x2, gate, seq)


def kernel(x, c, rel_bias, ada_w, ada_b, norm_g, ev_w_in, ev_w_out, nsa_qk_g, cmp_pe, cmp_w1, cmp_b1, cmp_w2, cmp_b2, mla_q_norm_g, mla_kv_norm_g, mla_w_uq, mla_w_ukv, mla_nope_g, mla_rope_g, od_w_in, od_w_out, dsa_qk_g, ffn_w_up, ffn_conv_w, ffn_conv_b, ffn_w_down):
    bsz, seq, d = x.shape
    depth = ada_w.shape[0]
    x2 = x.reshape(bsz * seq, d)
    mods = ada_all(c, ada_w, ada_b)
    dt, dc = bias_tiles(rel_bias)
    bias_bound = 2.0 * LOG2E * jnp.max(jnp.abs(rel_bias))
    for i in range(depth):
        j = i // 2
        shift, scale, gate = jnp.split(mods[i, 0], 3, axis=-1)
        h = modnorm(x2, norm_g[i, 0], scale, shift, seq)
        if i % 2 == 0:
            x2 = _even_mixer(h, x2, gate, dt, dc, bias_bound, bsz, seq, ev_w_in[j], ev_w_out[j], nsa_qk_g[j],
                             cmp_pe[j], cmp_w1[j], cmp_b1[j], cmp_w2[j], cmp_b2[j], mla_q_norm_g[j],
                             mla_kv_norm_g[j], mla_w_uq[j], mla_w_ukv[j], mla_nope_g[j], mla_rope_g[j])
        else:
            x2 = _odd_mixer(h, x2, gate, dt, bias_bound, bsz, seq, od_w_in[j], od_w_out[j], dsa_qk_g[j])
        shift, scale, gate = jnp.split(mods[i, 1], 3, axis=-1)
        h = modnorm(x2, norm_g[i, 1], scale, shift, seq)
        x2 = _conv_ffn(h, x2, gate, seq, ffn_w_up[i], ffn_conv_w[i], ffn_conv_b[i], ffn_w_down[i])
    return x2.reshape(bsz, seq, d)
```

```python
import functools
import math

import numpy as np
import jax
import jax.numpy as jnp
from jax import lax
from jax.experimental import pallas as pl
from jax.experimental.pallas import tpu as pltpu

HEAD_DIM = 128
NSA_HEADS = 8
NSA_GROUPS = 2
NSA_HPG = NSA_HEADS // NSA_GROUPS
CMP_BLOCK = 32
CMP_STRIDE = 16
CMP_HIDDEN = 256
SEL_BLOCK = 64
SEL_TOP_N = 16
WINDOW = 512
MLA_HEADS = 8
MLA_Q_RANK = 512
MLA_KV_RANK = 256
MLA_NOPE = 128
MLA_ROPE = 64
MLA_V = 128
DSA_HEADS = 16
DSA_KV_HEADS = 4
DSA_HPG = DSA_HEADS // DSA_KV_HEADS
IDX_HEADS = 16
IDX_DIM = 64
IDX_ROPE = 32
DSA_TOPK_MAX = 256
REL_BUCKETS = 32
REL_MAX_DIST = 128
CONV_WIDTH = 3
ROPE_THETA = 10000.0
EPS = 1e-6
NEG = -1e30
FORCE = 1e9

LANE = 128
QB = 128
VMEM_LIMIT = 56 * 1024 * 1024

F32 = jnp.float32
BF16 = jnp.bfloat16


def _t5_thresholds():
    d = np.arange(0, 4 * REL_MAX_DIST)
    half = REL_BUCKETS // 2
    val = np.log(np.maximum(d, 1) / half) / math.log(REL_MAX_DIST / half) * (REL_BUCKETS - half)
    large = np.minimum(half + np.floor(np.maximum(val, 0.0)).astype(np.int64), REL_BUCKETS - 1)
    bucket = np.where(d < half, d, large)
    return [int(np.argmax(bucket >= b)) for b in range(1, REL_BUCKETS)]


T5_THR = _t5_thresholds()
T5_FAR = T5_THR[-1]
assert T5_FAR <= LANE


def _cparams(sem):
    return pltpu.CompilerParams(dimension_semantics=sem, vmem_limit_bytes=VMEM_LIMIT)


def _dot(a, b):
    return jnp.dot(a, b, preferred_element_type=F32)


def _ada_kernel(c_ref, w_ref, b_ref, o_ref):
    c = c_ref[...]
    a = c * jax.nn.sigmoid(c)
    o_ref[0] = jnp.dot(a, w_ref[0], preferred_element_type=F32,
                       precision=lax.Precision.HIGHEST) + b_ref[0]


def ada_all(c, ada_w, ada_b):
    depth, two, d, n3 = ada_w.shape
    bsz = c.shape[0]
    rows = 8
    cp = jnp.zeros((rows, d), F32).at[:bsz].set(c)
    w = ada_w.reshape(depth * two, d, n3)
    b = ada_b.reshape(depth * two, 1, n3)
    tn = 512
    out = pl.pallas_call(
        _ada_kernel,
        grid=(depth * two, n3 // tn),
        in_specs=[pl.BlockSpec((rows, d), lambda l, j: (0, 0)),
                  pl.BlockSpec((1, d, tn), lambda l, j: (l, 0, j)),
                  pl.BlockSpec((1, 1, tn), lambda l, j: (l, 0, j))],
        out_specs=pl.BlockSpec((1, rows, tn), lambda l, j: (l, 0, j)),
        out_shape=jax.ShapeDtypeStruct((depth * two, rows, n3), F32),
        compiler_params=_cparams(("arbitrary", "arbitrary")),
        name="ada_mod",
    )(cp, w, b)
    return out[:, :bsz].reshape(depth, two, bsz, n3)


def _modnorm_kernel(x_ref, g_ref, sc_ref, sh_ref, o_ref):
    x = x_ref[...]
    y = x * lax.rsqrt(jnp.mean(x * x, axis=-1, keepdims=True) + EPS)
    h = (y * g_ref[...]) * (1.0 + sc_ref[0]) + sh_ref[0]
    o_ref[...] = h.astype(o_ref.dtype)


def modnorm(x2, g, scale, shift, seq):
    m, d = x2.shape
    tm = 512
    tpb = seq // tm
    return pl.pallas_call(
        _modnorm_kernel,
        grid=(m // tm,),
        in_specs=[pl.BlockSpec((tm, d), lambda i: (i, 0)),
                  pl.BlockSpec((1, d), lambda i: (0, 0)),
                  pl.BlockSpec((1, 1, d), lambda i: (i // tpb, 0, 0)),
                  pl.BlockSpec((1, 1, d), lambda i: (i // tpb, 0, 0))],
        out_specs=pl.BlockSpec((tm, d), lambda i: (i, 0)),
        out_shape=jax.ShapeDtypeStruct((m, d), BF16),
        compiler_params=_cparams(("arbitrary",)),
        name="modnorm",
    )(x2, g.reshape(1, d), scale.reshape(-1, 1, d), shift.reshape(-1, 1, d))


def _proj_kernel(x_ref, w_ref, o_ref, *, nslab):
    acc = _dot(x_ref[...], w_ref[...])
    for s in range(nslab):
        o_ref[s] = acc[:, s * LANE:(s + 1) * LANE]


def proj_slabs(x, w, tm=1024, tn=384):
    m, k = x.shape
    n = w.shape[1]
    assert n % tn == 0 and m % tm == 0
    nslab = tn // LANE
    return pl.pallas_call(
        functools.partial(_proj_kernel, nslab=nslab),
        grid=(m // tm, n // tn),
        in_specs=[pl.BlockSpec((tm, k), lambda i, j: (i, 0)),
                  pl.BlockSpec((k, tn), lambda i, j: (0, j))],
        out_specs=pl.BlockSpec((nslab, tm, LANE), lambda i, j: (j, i, 0)),
        out_shape=jax.ShapeDtypeStruct((n // LANE, m, LANE), F32),
        compiler_params=_cparams(("arbitrary", "arbitrary")),
        name="proj_slabs",
    )(x, w)


def _proj_heads_kernel(x_ref, w_ref, g_ref, o_ref, *, nslab, norm, transpose):
    acc = _dot(x_ref[...], w_ref[...])
    for s in range(nslab):
        y = acc[:, s * LANE:(s + 1) * LANE]
        if norm:
            y = y * lax.rsqrt(jnp.mean(y * y, axis=-1, keepdims=True) + EPS) * g_ref[...]
        o_ref[s] = (y.T if transpose else y).astype(o_ref.dtype)


def proj_heads(x, w, g=None, *, transpose, tm=1024, tn=1024):
    m, k = x.shape
    n = w.shape[1]
    tn = min(tn, n)
    assert n % tn == 0 and m % tm == 0
    nslab = tn // LANE
    norm = g is not None
    if transpose:
        out_spec = pl.BlockSpec((nslab, LANE, tm), lambda i, j: (j, 0, i))
        out_shape = jax.ShapeDtypeStruct((n // LANE, LANE, m), BF16)
    else:
        out_spec = pl.BlockSpec((nslab, tm, LANE), lambda i, j: (j, i, 0))
        out_shape = jax.ShapeDtypeStruct((n // LANE, m, LANE), BF16)
    g2 = (g if norm else jnp.ones((LANE,), F32)).reshape(1, LANE)
    return pl.pallas_call(
        functools.partial(_proj_heads_kernel, nslab=nslab, norm=norm, transpose=transpose),
        grid=(m // tm, n // tn),
        in_specs=[pl.BlockSpec((tm, k), lambda i, j: (i, 0)),
                  pl.BlockSpec((k, tn), lambda i, j: (0, j)),
                  pl.BlockSpec((1, LANE), lambda i, j: (0, 0))],
        out_specs=out_spec,
        out_shape=out_shape,
        compiler_params=_cparams(("arbitrary", "arbitrary")),
        name="proj_heads",
    )(x, w, g2)


def _rms_rows(x, g):
    return x * lax.rsqrt(jnp.mean(x * x, axis=-1, keepdims=True) + EPS) * g


def _rope_rows(x, cos, sin):
    half = x.shape[-1] // 2
    x1, x2 = x[:, :half], x[:, half:]
    return jnp.concatenate([x1 * cos - x2 * sin, x1 * sin + x2 * cos], axis=1)


def _latent(x_ref, g_ref):
    x = jnp.concatenate([x_ref[s] for s in range(x_ref.shape[0])], axis=1)
    return _rms_rows(x, g_ref[...]).astype(BF16)


def _mla_q_kernel(shift_ref, x_ref, g_ref, w_ref, gn_ref, gr_ref, cos_ref, sin_ref, o_ref, *, scale):
    acc = _dot(_latent(x_ref, g_ref), w_ref[...])
    tm = acc.shape[0]
    cos, sin = cos_ref[...], sin_ref[...]
    first = lax.broadcasted_iota(jnp.int32, (tm, LANE - MLA_ROPE), 1) == 0
    pad = jnp.where(first, -shift_ref[0], 0.0)
    for h in range(MLA_HEADS):
        nope = _rms_rows(acc[:, h * MLA_NOPE:(h + 1) * MLA_NOPE], gn_ref[...]) * scale
        r0 = MLA_HEADS * MLA_NOPE + h * MLA_ROPE
        pe = _rope_rows(_rms_rows(acc[:, r0:r0 + MLA_ROPE], gr_ref[...]), cos, sin) * scale
        o_ref[h, 0:MLA_NOPE, :] = nope.T.astype(o_ref.dtype)
        o_ref[h, MLA_NOPE:MLA_NOPE + LANE, :] = jnp.concatenate([pe, pad], axis=1).T.astype(o_ref.dtype)


def _mla_kv_kernel(x_ref, g_ref, w_ref, tail_ref, gn_ref, gr_ref, cos_ref, sin_ref, ok_ref, ov_ref):
    acc = _dot(_latent(x_ref, g_ref), w_ref[...])
    tm = acc.shape[0]
    k_pe = _rope_rows(_rms_rows(tail_ref[0][:, :MLA_ROPE], gr_ref[...]), cos_ref[...], sin_ref[...])
    first = lax.broadcasted_iota(jnp.int32, (tm, LANE - MLA_ROPE), 1) == 0
    k_pe = jnp.concatenate([k_pe, jnp.where(first, 1.0, 0.0)], axis=1).astype(ok_ref.dtype)
    for h in range(MLA_HEADS):
        c0 = h * (MLA_NOPE + MLA_V)
        ok_ref[h, :, 0:MLA_NOPE] = _rms_rows(acc[:, c0:c0 + MLA_NOPE], gn_ref[...]).astype(ok_ref.dtype)
        ok_ref[h, :, MLA_NOPE:MLA_NOPE + LANE] = k_pe
        ov_ref[h] = acc[:, c0 + MLA_NOPE:c0 + MLA_NOPE + MLA_V].T.astype(ov_ref.dtype)


def mla_project(proj, s_cq, s_ckv, s_tail, seq, q_norm_g, kv_norm_g, wq_r, w_ukv, nope_g, rope_g, cos, sin,
                scale, shift, tm=512):
    _, m, _ = proj.shape
    kq, kkv = s_ckv - s_cq, s_tail - s_ckv
    tps = seq // tm
    dqk = MLA_NOPE + LANE
    half = MLA_ROPE // 2
    rope_specs = [pl.BlockSpec((tm, half), lambda i: (i % tps, 0))] * 2
    gain_specs = [pl.BlockSpec((1, MLA_NOPE), lambda i: (0, 0)), pl.BlockSpec((1, MLA_ROPE), lambda i: (0, 0))]
    q_t = pl.pallas_call(
        functools.partial(_mla_q_kernel, scale=scale),
        grid=(m // tm,),
        in_specs=[pl.BlockSpec(memory_space=pltpu.SMEM),
                  pl.BlockSpec((kq, tm, LANE), lambda i: (s_cq // kq, i, 0)),
                  pl.BlockSpec((1, kq * LANE), lambda i: (0, 0)),
                  pl.BlockSpec(wq_r.shape, lambda i: (0, 0))] + gain_specs + rope_specs,
        out_specs=pl.BlockSpec((MLA_HEADS, dqk, tm), lambda i: (0, 0, i)),
        out_shape=jax.ShapeDtypeStruct((MLA_HEADS, dqk, m), BF16),
        compiler_params=_cparams(("arbitrary",)),
        name="mla_q_project",
    )(jnp.reshape(shift, (1,)).astype(F32), proj, q_norm_g.reshape(1, -1), wq_r, nope_g[0].reshape(1, -1),
      rope_g[0].reshape(1, -1), cos, sin)
    k, v_t = pl.pallas_call(
        _mla_kv_kernel,
        grid=(m // tm,),
        in_specs=[pl.BlockSpec((kkv, tm, LANE), lambda i: (s_ckv // kkv, i, 0)),
                  pl.BlockSpec((1, kkv * LANE), lambda i: (0, 0)),
                  pl.BlockSpec(w_ukv.shape, lambda i: (0, 0)),
                  pl.BlockSpec((1, tm, LANE), lambda i: (s_tail, i, 0))] + gain_specs + rope_specs,
        out_specs=[pl.BlockSpec((MLA_HEADS, tm, dqk), lambda i: (0, i, 0)),
                   pl.BlockSpec((MLA_HEADS, MLA_V, tm), lambda i: (0, 0, i))],
        out_shape=[jax.ShapeDtypeStruct((MLA_HEADS, m, dqk), BF16),
                   jax.ShapeDtypeStruct((MLA_HEADS, MLA_V, m), BF16)],
        compiler_params=_cparams(("arbitrary",)),
        name="mla_kv_project",
    )(proj, kv_norm_g.reshape(1, -1), w_ukv, proj, nope_g[1].reshape(1, -1), rope_g[1].reshape(1, -1), cos, sin)
    return q_t, k, v_t


def _resproj_kernel(*refs, npair):
    xres_ref, gate_ref = refs[2 * npair], refs[2 * npair + 1]
    o_ref = refs[2 * npair + 2]
    acc = _dot(refs[0][...], refs[1][...])
    for p in range(1, npair):
        acc = acc + _dot(refs[2 * p][...], refs[2 * p + 1][...])
    o_ref[...] = xres_ref[...] + gate_ref[0] * acc


def resproj(pairs, xres, gate, seq, tm=1024, tn=512):
    m, n = xres.shape
    tpb = seq // tm
    in_specs, args = [], []
    for x, w in pairs:
        k = x.shape[1]
        in_specs += [pl.BlockSpec((tm, k), lambda i, j: (i, 0)),
                     pl.BlockSpec((k, tn), lambda i, j: (0, j))]
        args += [x, w]
    in_specs += [pl.BlockSpec((tm, tn), lambda i, j: (i, j)),
                 pl.BlockSpec((1, 1, tn), lambda i, j: (i // tpb, 0, j))]
    args += [xres, gate.reshape(-1, 1, n)]
    return pl.pallas_call(
        functools.partial(_resproj_kernel, npair=len(pairs)),
        grid=(m // tm, n // tn),
        in_specs=in_specs,
        out_specs=pl.BlockSpec((tm, tn), lambda i, j: (i, j)),
        out_shape=jax.ShapeDtypeStruct((m, n), F32),
        compiler_params=_cparams(("arbitrary", "arbitrary")),
        name="resproj",
    )(*args)


HALO = 8


def _ffn_up_kernel(h_ref, wg_ref, wv_ref, cwg_ref, cwv_ref, cbg_ref, cbv_ref, o_ref,
                   ug_ref, uv_ref, *, tm, tiles_per_seq):
    i = pl.program_id(1)
    first = (i % tiles_per_seq) == 0

    @pl.when(first)
    def _():
        ug_ref[0:HALO, :] = jnp.zeros((HALO, ug_ref.shape[1]), F32)
        uv_ref[0:HALO, :] = jnp.zeros((HALO, uv_ref.shape[1]), F32)

    @pl.when(jnp.logical_not(first))
    def _():
        ug_ref[0:HALO, :] = ug_ref[tm:tm + HALO, :]
        uv_ref[0:HALO, :] = uv_ref[tm:tm + HALO, :]

    h = h_ref[...]
    ug_ref[HALO:HALO + tm, :] = _dot(h, wg_ref[...])
    uv_ref[HALO:HALO + tm, :] = _dot(h, wv_ref[...])

    def conv(u_ref, cw_ref, cb_ref):
        out = cb_ref[...]
        for j in range(CONV_WIDTH):
            off = HALO - (CONV_WIDTH - 1) + j
            out = out + cw_ref[j:j + 1, :] * u_ref[off:off + tm, :]
        return out

    g = conv(ug_ref, cwg_ref, cbg_ref)
    v = conv(uv_ref, cwv_ref, cbv_ref)
    o_ref[...] = (g * jax.nn.sigmoid(g) * v).astype(o_ref.dtype)


def ffn_up(h, w_up, conv_w, conv_b, seq, tm=1024, tn=512):
    m, d = h.shape
    f = w_up.shape[1] // 2
    nj = f // tn
    tps = seq // tm
    cb = conv_b.reshape(1, 2 * f)
    return pl.pallas_call(
        functools.partial(_ffn_up_kernel, tm=tm, tiles_per_seq=tps),
        grid=(nj, m // tm),
        in_specs=[pl.BlockSpec((tm, d), lambda j, i: (i, 0)),
                  pl.BlockSpec((d, tn), lambda j, i: (0, j)),
                  pl.BlockSpec((d, tn), lambda j, i: (0, nj + j)),
                  pl.BlockSpec((CONV_WIDTH, tn), lambda j, i: (0, j)),
                  pl.BlockSpec((CONV_WIDTH, tn), lambda j, i: (0, nj + j)),
                  pl.BlockSpec((1, tn), lambda j, i: (0, j)),
                  pl.BlockSpec((1, tn), lambda j, i: (0, nj + j))],
        out_specs=pl.BlockSpec((tm, tn), lambda j, i: (i, j)),
        out_shape=jax.ShapeDtypeStruct((m, f), BF16),
        scratch_shapes=[pltpu.VMEM((tm + HALO, tn), F32), pltpu.VMEM((tm + HALO, tn), F32)],
        compiler_params=_cparams(("arbitrary", "arbitrary")),
        name="ffn_up_conv",
    )(h, w_up, w_up, conv_w, conv_w, cb, cb)


LOG2E = 1.4426950408889634
CWIN = 16


def _t5_shifted(dist, tbl_ref, h):
    val = jnp.full(dist.shape, tbl_ref[0, h], F32)
    for b in range(1, REL_BUCKETS):
        val = jnp.where(dist >= T5_THR[b - 1], tbl_ref[b, h], val)
    return (val - tbl_ref[REL_BUCKETS - 1, h]) * LOG2E


def _bias_tiles_kernel(tbl_ref, dt_ref, dc_ref):
    h = pl.program_id(0)
    key = lax.broadcasted_iota(jnp.int32, (LANE, LANE), 0)
    q = lax.broadcasted_iota(jnp.int32, (LANE, LANE), 1)
    for rel in range(2):
        dt_ref[0, rel] = _t5_shifted(rel * LANE + q - key, tbl_ref, h)
    dt_ref[0, 2] = jnp.zeros((LANE, LANE), F32)
    u = lax.broadcasted_iota(jnp.int32, (CWIN, LANE), 0)
    qc = lax.broadcasted_iota(jnp.int32, (CWIN, LANE), 1)
    dc_ref[0] = _t5_shifted(qc - CMP_STRIDE * (u - CWIN // 2) - (CMP_BLOCK - 1), tbl_ref, h)


def bias_tiles(rel_bias):
    nh = rel_bias.shape[1]
    return pl.pallas_call(
        _bias_tiles_kernel,
        grid=(nh,),
        in_specs=[pl.BlockSpec(memory_space=pltpu.SMEM)],
        out_specs=[pl.BlockSpec((1, 3, LANE, LANE), lambda h: (h, 0, 0, 0)),
                   pl.BlockSpec((1, CWIN, LANE), lambda h: (h, 0, 0))],
        out_shape=[jax.ShapeDtypeStruct((nh, 3, LANE, LANE), F32),
                   jax.ShapeDtypeStruct((nh, CWIN, LANE), F32)],
        compiler_params=_cparams(("arbitrary",)),
        name="t5_bias_tiles",
    )(rel_bias)


def _compress_kernel(x_ref, pe_ref, w1_ref, b1_ref, w2_ref, b2_ref, g_ref, o_ref, *, half):
    kv = pl.program_id(0)
    x = x_ref[0]
    a = _dot((x + pe_ref[0, :, :half]).astype(BF16), w1_ref[0, :half, :])
    b = _dot((x + pe_ref[0, :, half:]).astype(BF16), w1_ref[0, half:, :])
    b_next = jnp.concatenate([b[1:], jnp.zeros((1, b.shape[1]), F32)], axis=0)
    hid = jax.nn.gelu(a + b_next + b1_ref[0])
    out = _dot(hid.astype(BF16), w2_ref[0]) + b2_ref[0]
    normed = out * lax.rsqrt(jnp.mean(out * out, axis=-1, keepdims=True) + EPS) * g_ref[...]
    out = jnp.where(kv == 0, normed, out)
    o_ref[0, 0] = out.astype(o_ref.dtype)


def compress_kv(proj, slab0, bsz, seq, cmp_pe, cmp_w1, cmp_b1, cmp_w2, cmp_b2, g_k):
    nslab, m, _ = proj.shape
    nchunk = seq // CMP_STRIDE
    half = CMP_STRIDE * HEAD_DIM
    xv = proj.reshape(nslab, m // CMP_STRIDE, half)
    pe = cmp_pe.reshape(2, 1, CMP_BLOCK * HEAD_DIM)
    return pl.pallas_call(
        functools.partial(_compress_kernel, half=half),
        grid=(2, bsz, NSA_GROUPS),
        in_specs=[pl.BlockSpec((1, nchunk, half), lambda kv, b, g: (slab0 + 2 * kv + g, b, 0)),
                  pl.BlockSpec((1, 1, 2 * half), lambda kv, b, g: (kv, 0, 0)),
                  pl.BlockSpec((1, 2 * half, CMP_HIDDEN), lambda kv, b, g: (kv, 0, 0)),
                  pl.BlockSpec((1, 1, CMP_HIDDEN), lambda kv, b, g: (kv, 0, 0)),
                  pl.BlockSpec((1, CMP_HIDDEN, HEAD_DIM), lambda kv, b, g: (kv, 0, 0)),
                  pl.BlockSpec((1, 1, HEAD_DIM), lambda kv, b, g: (kv, 0, 0)),
                  pl.BlockSpec((1, HEAD_DIM), lambda kv, b, g: (0, 0))],
        out_specs=pl.BlockSpec((1, 1, nchunk, HEAD_DIM), lambda kv, b, g: (kv, g, b, 0)),
        out_shape=jax.ShapeDtypeStruct((2, NSA_GROUPS, bsz * nchunk, HEAD_DIM), BF16),
        compiler_params=_cparams(("arbitrary", "arbitrary", "arbitrary")),
        name="nsa_compress",
    )(xv, pe, cmp_w1.astype(BF16), cmp_b1.reshape(2, 1, CMP_HIDDEN), cmp_w2.astype(BF16),
      cmp_b2.reshape(2, 1, HEAD_DIM), g_k.reshape(1, HEAD_DIM))


KW = 512
PV_KEYS = 256


def _tile_lanes(x, n):
    return jnp.concatenate([x] * n, axis=1)


def _flash_init(m_ref, l_ref, acc_ref):
    m_ref[...] = jnp.full(m_ref.shape, NEG, F32)
    l_ref[...] = jnp.zeros(l_ref.shape, F32)
    acc_ref[...] = jnp.zeros(acc_ref.shape, F32)


def _zero_after(x):
    bits = pltpu.bitcast(x, jnp.int32)
    return lax.shift_right_logical(lax.shift_right_logical(bits, 16), 16).astype(F32)


def _flash_update(s, v_t, m_ref, l_ref, acc_ref, col_max=None, after=None):
    m_old = m_ref[...]
    if col_max is None:
        col_max = jnp.max(s, axis=0, keepdims=True)
    m_new = jnp.maximum(m_old, col_max)
    alpha = jnp.exp2(m_old - m_new)
    l_new = alpha * l_ref[...]
    acc = alpha * acc_ref[...]
    nk = s.shape[0]
    for k0 in range(0, nk, PV_KEYS):
        p = jnp.exp2(s[k0:k0 + PV_KEYS] - m_new)
        l_new = l_new + jnp.sum(p, axis=0, keepdims=True)
        acc = acc + _dot(v_t[:, k0:k0 + PV_KEYS], p.astype(BF16))
    l_ref[...] = l_new
    acc_ref[...] = acc
    m_ref[...] = m_new if after is None else m_new + _zero_after(after)


SAFE_LOG2_BOUND = 60.0


def _flash_accumulate(s, v_t, l_ref, acc_ref, after=None):
    l_new = l_ref[...]
    acc = acc_ref[...]
    for k0 in range(0, s.shape[0], PV_KEYS):
        p = jnp.exp2(s[k0:k0 + PV_KEYS])
        l_new = l_new + jnp.sum(p, axis=0, keepdims=True)
        acc = acc + _dot(v_t[:, k0:k0 + PV_KEYS], p.astype(BF16))
    if after is not None:
        l_new = l_new + jnp.max(_zero_after(after), axis=0, keepdims=True)
    l_ref[...] = l_new
    acc_ref[...] = acc


def _sum_result(l_ref, acc_ref):
    den = l_ref[...]
    ok = den > 0.0
    return acc_ref[...] * jnp.where(ok, 1.0 / jnp.where(ok, den, 1.0), 0.0)


def _inv_den(m, den):
    ok = m > 0.5 * NEG
    return jnp.where(ok, 1.0 / jnp.where(ok, den, 1.0), 0.0)


def _flash_result(m_ref, l_ref, acc_ref):
    return acc_ref[...] * _inv_den(m_ref[...], l_ref[...])


def _softmax_cols(s):
    m = jnp.max(s, axis=0, keepdims=True)
    p = jnp.exp2(s - m)
    return p * _inv_den(m, jnp.sum(p, axis=0, keepdims=True))


def _near_bias(dt_ref, heads, qi, kt0, ntile):
    rows = []
    for j in range(ntile):
        rel = jnp.clip(qi - (kt0 + j), 0, 2)
        rows.append(jnp.concatenate([dt_ref[h, rel] for h in heads], axis=1))
    return jnp.concatenate(rows, axis=0)


def _pipelined_chunks(n, qk_stage, soft_stage):
    @pl.when(n > 0)
    def _():
        qk_stage(0, 0)

    def pair(p, x):
        c = 2 * p
        ahead = qk_stage(c + 1, 1)
        soft_stage(c, 0, ahead)
        ahead = qk_stage(jnp.minimum(c + 2, n - 1), 0)
        soft_stage(c + 1, 1, ahead)
        return x

    lax.fori_loop(0, n // 2, pair, 0)

    @pl.when(n % 2 == 1)
    def _():
        soft_stage(n - 1, 0, None)


def _nsa_kernel(bound_ref, qt_ref, gt_ref, kc_ref, vct_ref, ks_ref, vst_ref, kw_ref, vwt_ref,
                dt_ref, dc_ref, ext_ref, o_ref,
                sc_ref, ps_ref, m_ref, l_ref, acc_ref, sbuf0, sbuf1, cbuf0, cbuf1, *, seq, nc):
    sbuf, cbuf = (sbuf0, sbuf1), (cbuf0, cbuf1)
    qi = pl.program_id(2)
    q0 = qi * QB
    hpg = NSA_HPG
    heads = list(range(hpg))
    ncp = kc_ref.shape[1]
    ns = seq // SEL_BLOCK
    q_t = jnp.concatenate([qt_ref[h] for h in heads], axis=1)

    pad = CWIN // 2
    sc_ref[0:pad, :] = jnp.zeros((pad, hpg * QB), F32)
    sc_ref[pad + ncp:2 * pad + ncp, :] = jnp.zeros((pad, hpg * QB), F32)
    sc_ref[pad:pad + ncp, :] = _dot(kc_ref[0], q_t)
    r0 = pl.multiple_of(qi * (QB // CMP_STRIDE), 8)
    sc_ref[pl.ds(r0, CWIN), :] = sc_ref[pl.ds(r0, CWIN), :] + jnp.concatenate(
        [dc_ref[h] for h in heads], axis=1)
    ci = lax.broadcasted_iota(jnp.int32, (ncp, QB), 0)
    tc = q0 + lax.broadcasted_iota(jnp.int32, (ncp, QB), 1)
    valid_c = (ci * CMP_STRIDE + CMP_BLOCK - 1 <= tc) & (ci < nc)
    p_c = _softmax_cols(sc_ref[pad:pad + ncp, :] + _tile_lanes(jnp.where(valid_c, 0.0, NEG), hpg))
    oc_t = _dot(vct_ref[0], p_c.astype(BF16))
    p_sum = p_c[:, 0:QB]
    for h in range(1, hpg):
        p_sum = p_sum + p_c[:, h * QB:(h + 1) * QB]

    wkeys = WINDOW + QB
    start = pl.multiple_of(jnp.maximum(q0 - WINDOW, 0), LANE)
    s_w = _dot(kw_ref[0, pl.ds(start, wkeys), :], q_t)
    s_w = s_w + _near_bias(dt_ref, heads, qi, start // LANE, wkeys // LANE)
    dist_w = (q0 + lax.broadcasted_iota(jnp.int32, (wkeys, QB), 1)) - (
        start + lax.broadcasted_iota(jnp.int32, (wkeys, QB), 0))
    mask_w = (dist_w >= 0) & (dist_w < WINDOW)
    p_w = _softmax_cols(s_w + _tile_lanes(jnp.where(mask_w, 0.0, NEG), hpg))
    ow_t = _dot(vwt_ref[0, :, pl.ds(start, wkeys)], p_w.astype(BF16))

    ps_ref[0:8, :] = jnp.zeros((8, QB), F32)
    ps_ref[8:8 + ncp, :] = p_sum
    per = SEL_BLOCK // CMP_STRIDE
    band = [ps_ref[pl.ds(8 + r, ns, stride=per), :] for r in range(-1, per)]
    imp = 0.5 * band[0] + band[1] + band[2] + band[3] + 0.5 * band[4]
    if ns < LANE:
        imp = jnp.concatenate([imp, jnp.zeros((LANE - ns, QB), F32)], axis=0)
    blk = lax.broadcasted_iota(jnp.int32, (LANE, QB), 0)
    t = q0 + lax.broadcasted_iota(jnp.int32, (LANE, QB), 1)
    tb = t // SEL_BLOCK
    forced = (blk == 0) | (blk == tb) | (blk == tb - 1)
    score = jnp.where(forced, FORCE, jnp.where(blk * SEL_BLOCK <= t, imp, NEG))
    score = jnp.where(blk < ns, score, -jnp.inf)
    blk_f = blk.astype(F32)
    sel = jnp.zeros((LANE, QB), F32)
    for _ in range(min(SEL_TOP_N, ns)):
        mx = jnp.max(score, axis=0, keepdims=True)
        first = jnp.min(jnp.where(score == mx, blk_f, float(LANE)), axis=0, keepdims=True)
        pick = blk_f == first
        sel = jnp.where(pick, 1.0, sel)
        score = jnp.where(pick, -jnp.inf, score)
    sel_b = sel.astype(BF16)

    kpos = lax.broadcasted_iota(jnp.int32, (KW, QB), 0)
    tq = q0 + lax.broadcasted_iota(jnp.int32, (KW, QB), 1)
    bounded_ok = bound_ref[0] <= SAFE_LOG2_BOUND
    shift = jnp.where(bounded_ok, bound_ref[0], 0.0)
    c_near = jnp.maximum(qi - 1, 0) // (KW // LANE)

    def block_mask(c0):
        chosen = _dot(ext_ref[pl.ds(c0, KW), :], sel_b)
        return (chosen - 1.0) * (-NEG) - shift

    def attend(bounded):
        _flash_init(m_ref, l_ref, acc_ref)

        def qk_stage(c, buf):
            c0 = pl.multiple_of(c * KW, KW)
            s = _dot(ks_ref[0, pl.ds(c0, KW), :], q_t) + _tile_lanes(block_mask(c0), hpg)
            sbuf[buf][...] = s
            if bounded:
                return s[KW - 8:KW]
            col_max = jnp.max(s, axis=0, keepdims=True)
            cbuf[buf][...] = col_max
            return col_max

        def soft_stage(c, buf, ahead):
            c0 = pl.multiple_of(c * KW, KW)
            if bounded:
                _flash_accumulate(sbuf[buf][...], vst_ref[0, :, pl.ds(c0, KW)], l_ref, acc_ref, after=ahead)
            else:
                _flash_update(sbuf[buf][...], vst_ref[0, :, pl.ds(c0, KW)], m_ref, l_ref, acc_ref,
                              col_max=cbuf[buf][...], after=ahead)

        def near_step(c, x):
            c0 = pl.multiple_of(c * KW, KW)
            madd = block_mask(c0) + jnp.where(c0 + kpos <= tq, 0.0, NEG)
            s = (_dot(ks_ref[0, pl.ds(c0, KW), :], q_t) + _tile_lanes(madd, hpg)
                 + _near_bias(dt_ref, heads, qi, c * (KW // LANE), KW // LANE))
            if bounded:
                _flash_accumulate(s, vst_ref[0, :, pl.ds(c0, KW)], l_ref, acc_ref)
            else:
                _flash_update(s, vst_ref[0, :, pl.ds(c0, KW)], m_ref, l_ref, acc_ref)
            return x

        _pipelined_chunks(c_near, qk_stage, soft_stage)
        lax.fori_loop(c_near, qi // (KW // LANE) + 1, near_step, 0)
        acc_ref[...] = _sum_result(l_ref, acc_ref) if bounded else _flash_result(m_ref, l_ref, acc_ref)

    pl.when(bounded_ok)(lambda: attend(True))
    pl.when(jnp.logical_not(bounded_ok))(lambda: attend(False))
    os_t = acc_ref[...]

    gates = jax.nn.sigmoid(gt_ref[0])
    for h in heads:
        sl = slice(h * QB, (h + 1) * QB)
        o_t = (gates[3 * h:3 * h + 1] * oc_t[:, sl] + gates[3 * h + 1:3 * h + 2] * os_t[:, sl]
               + gates[3 * h + 2:3 * h + 3] * ow_t[:, sl])
        o_ref[:, h * HEAD_DIM:(h + 1) * HEAD_DIM] = o_t.T.astype(o_ref.dtype)


def nsa_attention(logit_bound, q_t, gates_t, kc, vc_t, k_sw, v_sw_t, dt, dc, bsz, seq):
    nq = seq // QB
    ncp = seq // CMP_STRIDE
    nc = ncp - 1
    ns = seq // SEL_BLOCK
    assert ns <= LANE and seq >= WINDOW + QB and seq % KW == 0
    assert CMP_BLOCK == 2 * CMP_STRIDE and SEL_BLOCK == 4 * CMP_STRIDE
    expand =((np.arange(seq)[:, None] // SEL_BLOCK) == np.arange(LANE)[None, :]).astype(np.float32)
    ng = NSA_GROUPS
    ks_spec = pl.BlockSpec((1, seq, HEAD_DIM), lambda b, g, i: (g, b, 0))
    kw_spec = pl.BlockSpec((1, seq, HEAD_DIM), lambda b, g, i: (ng + g, b, 0))
    vs_spec = pl.BlockSpec((1, HEAD_DIM, seq), lambda b, g, i: (g, 0, b))
    vw_spec = pl.BlockSpec((1, HEAD_DIM, seq), lambda b, g, i: (ng + g, 0, b))
    lanes = NSA_HPG * QB
    return pl.pallas_call(
        functools.partial(_nsa_kernel, seq=seq, nc=nc),
        grid=(bsz, NSA_GROUPS, nq),
        in_specs=[pl.BlockSpec(memory_space=pltpu.SMEM),
                  pl.BlockSpec((NSA_HPG, HEAD_DIM, QB), lambda b, g, i: (g, 0, b * nq + i)),
                  pl.BlockSpec((1, 16, QB), lambda b, g, i: (g, 0, b * nq + i)),
                  pl.BlockSpec((1, ncp, HEAD_DIM), lambda b, g, i: (g, b, 0)),
                  pl.BlockSpec((1, HEAD_DIM, ncp), lambda b, g, i: (g, 0, b)),
                  ks_spec, vs_spec, kw_spec, vw_spec,
                  pl.BlockSpec((NSA_HPG, 3, LANE, LANE), lambda b, g, i: (g, 0, 0, 0)),
                  pl.BlockSpec((NSA_HPG, CWIN, LANE), lambda b, g, i: (g, 0, 0)),
                  pl.BlockSpec((seq, LANE), lambda b, g, i: (0, 0))],
        out_specs=pl.BlockSpec((QB, NSA_HPG * HEAD_DIM), lambda b, g, i: (b * nq + i, g)),
        out_shape=jax.ShapeDtypeStruct((bsz * seq, NSA_HEADS * HEAD_DIM), BF16),
        scratch_shapes=[pltpu.VMEM((ncp + CWIN, lanes), F32), pltpu.VMEM((ncp + 8, QB), F32),
                        pltpu.VMEM((1, lanes), F32), pltpu.VMEM((1, lanes), F32),
                        pltpu.VMEM((HEAD_DIM, lanes), F32),
                        pltpu.VMEM((KW, lanes), F32), pltpu.VMEM((KW, lanes), F32),
                        pltpu.VMEM((1, lanes), F32), pltpu.VMEM((1, lanes), F32)],
        compiler_params=_cparams(("arbitrary", "arbitrary", "arbitrary")),
        name="nsa_attention",
    )(jnp.reshape(logit_bound, (1,)).astype(F32), q_t, gates_t, kc, vc_t, k_sw, v_sw_t, k_sw, v_sw_t, dt, dc,
      jnp.asarray(expand, BF16))


MLA_HPS = 2


def _mla_kernel(bound_ref, qt_ref, k_ref, vt_ref, o_ref, *scratch):
    qi = pl.program_id(2)
    chains = [scratch[3 * h:3 * h + 3] for h in range(MLA_HPS)]
    sbuf = [scratch[(3 + b) * MLA_HPS:(4 + b) * MLA_HPS] for b in range(2)]
    cbuf = [scratch[(5 + b) * MLA_HPS:(6 + b) * MLA_HPS] for b in range(2)]
    c_diag = pl.multiple_of(qi * KW, KW)
    kpos = lax.broadcasted_iota(jnp.int32, (KW, KW), 0)
    tq = lax.broadcasted_iota(jnp.int32, (KW, KW), 1)
    dv = vt_ref.shape[1]

    def attend(bounded):
        for ch in chains:
            _flash_init(*ch)

        def qk_stage(c, buf):
            c0 = pl.multiple_of(c * KW, KW)
            ahead = []
            for h in range(MLA_HPS):
                s = _dot(k_ref[h, pl.ds(c0, KW), :], qt_ref[h])
                sbuf[buf][h][...] = s
                if bounded:
                    ahead.append(s[KW - 8:KW])
                else:
                    ahead.append(jnp.max(s, axis=0, keepdims=True))
                    cbuf[buf][h][...] = ahead[-1]
            return ahead

        def soft_stage(c, buf, ahead):
            c0 = pl.multiple_of(c * KW, KW)
            for h, (m_ref, l_ref, acc_ref) in enumerate(chains):
                after = None if ahead is None else ahead[h]
                if bounded:
                    _flash_accumulate(sbuf[buf][h][...], vt_ref[h, :, pl.ds(c0, KW)], l_ref, acc_ref, after=after)
                else:
                    _flash_update(sbuf[buf][h][...], vt_ref[h, :, pl.ds(c0, KW)], m_ref, l_ref, acc_ref,
                                  col_max=cbuf[buf][h][...], after=after)

        _pipelined_chunks(qi, qk_stage, soft_stage)
        causal = jnp.where(kpos <= tq, 0.0, NEG)
        scores = [_dot(k_ref[h, pl.ds(c_diag, KW), :], qt_ref[h]) + causal for h in range(MLA_HPS)]
        for h, (m_ref, l_ref, acc_ref) in enumerate(chains):
            if bounded:
                _flash_accumulate(scores[h], vt_ref[h, :, pl.ds(c_diag, KW)], l_ref, acc_ref)
                o_t = _sum_result(l_ref, acc_ref)
            else:
                _flash_update(scores[h], vt_ref[h, :, pl.ds(c_diag, KW)], m_ref, l_ref, acc_ref)
                o_t = _flash_result(m_ref, l_ref, acc_ref)
            o_ref[:, h * dv:(h + 1) * dv] = o_t.T.astype(o_ref.dtype)

    bounded_ok = bound_ref[0] <= SAFE_LOG2_BOUND
    pl.when(bounded_ok)(lambda: attend(True))
    pl.when(jnp.logical_not(bounded_ok))(lambda: attend(False))


def mla_attention(logit_bound, q_t, k, v_t, bsz, seq):
    nh, dqk, _ = q_t.shape
    dv = v_t.shape[1]
    nq = seq // KW
    hps = MLA_HPS
    state = [pltpu.VMEM((1, KW), F32), pltpu.VMEM((1, KW), F32), pltpu.VMEM((dv, KW), F32)] * hps
    state += [pltpu.VMEM((KW, KW), F32)] * (2 * hps)
    state += [pltpu.VMEM((1, KW), F32)] * (2 * hps)
    return pl.pallas_call(
        _mla_kernel,
        grid=(bsz, nh // hps, nq),
        in_specs=[pl.BlockSpec(memory_space=pltpu.SMEM),
                  pl.BlockSpec((hps, dqk, KW), lambda b, h, i: (h, 0, b * nq + i)),
                  pl.BlockSpec((hps, seq, dqk), lambda b, h, i: (h, b, 0)),
                  pl.BlockSpec((hps, dv, seq), lambda b, h, i: (h, 0, b))],
        out_specs=pl.BlockSpec((KW, hps * dv), lambda b, h, i: (b * nq + i, h)),
        out_shape=jax.ShapeDtypeStruct((bsz * seq, nh * dv), BF16),
        scratch_shapes=state,
        compiler_params=_cparams(("arbitrary", "arbitrary", "arbitrary")),
        name="mla_attention",
    )(jnp.reshape(logit_bound, (1,)).astype(F32), q_t, k, v_t)


INT_MIN = -2 ** 31
NEG_KEY = int(np.array(NEG, np.float32).view(np.int32)) ^ 0x7FFFFFFF
KEY_BITS = 32
SURE_BITS = 20


def _sort_key(x):
    bits = pltpu.bitcast(x + 0.0, jnp.int32)
    return jnp.where(bits < 0, bits ^ 0x7FFFFFFF, bits)


def _dsa_kernel(bound_ref, iqt_ref, iwt_ref, ik_ref, qt_ref, k_ref, vt_ref, dt_ref, o_ref,
                key_ref, madd_ref, *state, seq, k_sel):
    qi = pl.program_id(1)
    q0 = qi * QB
    n_chunk = (q0 + QB + KW - 1) // KW
    n_rest = seq - n_chunk * KW
    kpos = lax.broadcasted_iota(jnp.int32, (KW, QB), 0)
    tq = q0 + lax.broadcasted_iota(jnp.int32, (KW, QB), 1)
    hpp = KW // QB

    def score_chunk(c, x):
        c0 = pl.multiple_of(c * KW, KW)
        ikc = ik_ref[pl.ds(c0, KW), :]
        acc = jnp.zeros((KW, QB), F32)
        for piece in range(IDX_HEADS // hpp):
            sl = slice(piece * KW, (piece + 1) * KW)
            s = jnp.maximum(_dot(ikc, iqt_ref[0, :, sl]), 0.0) * iwt_ref[0, :, sl]
            for j in range(hpp):
                acc = acc + s[:, j * QB:(j + 1) * QB]
        acc = jnp.where(c0 + kpos <= tq, acc, NEG)
        key_ref[pl.ds(c0, KW), :] = _sort_key(acc)
        return x

    lax.fori_loop(0, n_chunk, score_chunk, 0)

    def count(pred):
        def body(c, acc):
            c0 = pl.multiple_of(c * KW, KW)
            hit = jnp.where(pred(key_ref[pl.ds(c0, KW), :], c0), 1.0, 0.0)
            parts = [hit[8 * i:8 * (i + 1)] for i in range(KW // 8)]
            while len(parts) > 1:
                parts = [parts[i] + parts[i + 1] for i in range(0, len(parts), 2)]
            return acc + parts[0]
        acc = lax.fori_loop(0, n_chunk, body, jnp.zeros((8, QB), F32))
        return jnp.sum(acc, axis=0, keepdims=True)

    rest = n_rest.astype(F32)
    kf = float(k_sel)

    def bit_step(i, st):
        u, thr_s, settled = st
        bit = jnp.left_shift(jnp.int32(1), KEY_BITS - 1 - i)
        trial = (u | bit) ^ INT_MIN
        cnt = count(lambda keys, c0: keys >= trial) + jnp.where(NEG_KEY >= trial, rest, 0.0)
        new = (cnt == kf) & (settled < 0.5)
        return (jnp.where(cnt >= kf, u | bit, u), jnp.where(new, trial, thr_s), jnp.where(new, 1.0, settled))

    st = (jnp.zeros((1, QB), jnp.int32), jnp.zeros((1, QB), jnp.int32), jnp.zeros((1, QB), F32))
    st = lax.fori_loop(0, SURE_BITS, bit_step, st)
    _, (u, thr_s, settled) = lax.while_loop(
        lambda c: (c[0] < KEY_BITS) & (jnp.min(c[1][2]) < 0.5),
        lambda c: (c[0] + 1, bit_step(c[0], c[1])), (jnp.int32(SURE_BITS), st))
    is_settled = settled > 0.5
    thr = jnp.where(is_settled, thr_s, u ^ INT_MIN)

    def edge_counts():
        return (count(lambda keys, c0: keys > thr) + jnp.where(NEG_KEY > thr, rest, 0.0),
                count(lambda keys, c0: keys >= thr) + jnp.where(NEG_KEY >= thr, rest, 0.0))

    zero_cnt = jnp.zeros((1, QB), F32)
    cnt_gt, cnt_ge = lax.cond(jnp.min(settled) > 0.5, lambda: (zero_cnt, zero_cnt), edge_counts)
    need = kf - cnt_gt
    tie_q = (cnt_ge > kf) & (thr != NEG_KEY) & jnp.logical_not(is_settled)
    idx_bits = (seq - 1).bit_length()
    no_cut = 2 ** 30

    def tie_cut():
        def idx_step(i, x):
            bit = jnp.left_shift(jnp.int32(1), idx_bits - 1 - i)
            trial = x | bit
            f = count(lambda keys, c0: (keys == thr) & (c0 + kpos < trial))
            return jnp.where(f <= need - 1.0, trial, x)
        return lax.fori_loop(0, idx_bits, idx_step, jnp.zeros((1, QB), jnp.int32))

    any_tie = jnp.max(jnp.where(tie_q, 1.0, 0.0)) > 0.0
    x_cut = lax.cond(any_tie, tie_cut, lambda: jnp.full((1, QB), no_cut, jnp.int32))
    x_cut = jnp.where(tie_q, x_cut, no_cut)

    def mask_chunk(c, x):
        c0 = pl.multiple_of(c * KW, KW)
        keys = key_ref[pl.ds(c0, KW), :]
        pos = c0 + kpos
        chosen = (keys > thr) | ((keys == thr) & (pos <= x_cut))
        madd_ref[pl.ds(c0, KW), :] = jnp.where(chosen & (pos <= tq), -shift, NEG)
        return x

    bounded_ok = bound_ref[0] <= SAFE_LOG2_BOUND
    shift = jnp.where(bounded_ok, bound_ref[0], 0.0)
    lax.fori_loop(0, n_chunk, mask_chunk, 0)

    c_near = jnp.maximum(qi - 1, 0) // (KW // LANE)
    ng = DSA_KV_HEADS
    chains = [state[3 * g:3 * g + 3] for g in range(ng)]
    sbuf = [state[(3 + b) * ng:(4 + b) * ng] for b in range(2)]
    cbuf = [state[(5 + b) * ng:(6 + b) * ng] for b in range(2)]
    group_heads = [[g * DSA_HPG + h for h in range(DSA_HPG)] for g in range(ng)]

    def raw_scores(c0, g):
        q_t = jnp.concatenate([qt_ref[h] for h in group_heads[g]], axis=1)
        return _dot(k_ref[g, pl.ds(c0, KW), :], q_t)

    def attend(bounded):
        for ch in chains:
            _flash_init(*ch)

        def qk_stage(c, buf):
            c0 = pl.multiple_of(c * KW, KW)
            madd = _tile_lanes(madd_ref[pl.ds(c0, KW), :], DSA_HPG)
            ahead = []
            for g in range(ng):
                s = raw_scores(c0, g) + madd
                sbuf[buf][g][...] = s
                if bounded:
                    ahead.append(s[KW - 8:KW])
                else:
                    ahead.append(jnp.max(s, axis=0, keepdims=True))
                    cbuf[buf][g][...] = ahead[-1]
            return ahead

        def soft_stage(c, buf, ahead):
            c0 = pl.multiple_of(c * KW, KW)
            for g, (m_ref, l_ref, acc_ref) in enumerate(chains):
                after = None if ahead is None else ahead[g]
                if bounded:
                    _flash_accumulate(sbuf[buf][g][...], vt_ref[g, :, pl.ds(c0, KW)], l_ref, acc_ref, after=after)
                else:
                    _flash_update(sbuf[buf][g][...], vt_ref[g, :, pl.ds(c0, KW)], m_ref, l_ref, acc_ref,
                                  col_max=cbuf[buf][g][...], after=after)

        _pipelined_chunks(c_near, qk_stage, soft_stage)

        def near_step(c, x):
            c0 = pl.multiple_of(c * KW, KW)
            madd = _tile_lanes(madd_ref[pl.ds(c0, KW), :], DSA_HPG)
            scores = [raw_scores(c0, g) + madd
                      + _near_bias(dt_ref, group_heads[g], qi, c * (KW // LANE), KW // LANE)
                      for g in range(ng)]
            for g, (m_ref, l_ref, acc_ref) in enumerate(chains):
                if bounded:
                    _flash_accumulate(scores[g], vt_ref[g, :, pl.ds(c0, KW)], l_ref, acc_ref)
                else:
                    _flash_update(scores[g], vt_ref[g, :, pl.ds(c0, KW)], m_ref, l_ref, acc_ref)
            return x

        lax.fori_loop(c_near, n_chunk, near_step, 0)
        for g, (m_ref, l_ref, acc_ref) in enumerate(chains):
            o_t = _sum_result(l_ref, acc_ref) if bounded else _flash_result(m_ref, l_ref, acc_ref)
            for h in range(DSA_HPG):
                hh = group_heads[g][h]
                o_ref[:, hh * HEAD_DIM:(hh + 1) * HEAD_DIM] = o_t[:, h * QB:(h + 1) * QB].T.astype(o_ref.dtype)

    pl.when(bounded_ok)(lambda: attend(True))
    pl.when(jnp.logical_not(bounded_ok))(lambda: attend(False))


def _idx_prep_kernel(p_ref, c_ref, sa_ref, sb_ref, iqt_ref, ik_ref, iwt_ref, *, ntile):
    nslab_q = IDX_HEADS * IDX_DIM // LANE
    per = LANE // IDX_DIM
    half = IDX_ROPE // 2
    zrows = jnp.zeros((LANE - IDX_DIM, QB), F32)

    def rope_slab(x, c, sa, sb):
        return x * c + pltpu.roll(x, LANE - half, axis=1) * sa + pltpu.roll(x, half, axis=1) * sb

    for t in range(ntile):
        rows = slice(t * QB, (t + 1) * QB)
        c, sa, sb = c_ref[rows, :], sa_ref[rows, :], sb_ref[rows, :]
        cols = []
        for s in range(nslab_q):
            x_t = (rope_slab(p_ref[s, rows, :], c, sa, sb) * IDX_DIM ** -0.5).T
            for j in range(per):
                cols.append(jnp.concatenate([x_t[j * IDX_DIM:(j + 1) * IDX_DIM], zrows], axis=0))
        iqt_ref[t] = jnp.concatenate(cols, axis=1).astype(iqt_ref.dtype)
        tail = p_ref[nslab_q, rows, :]
        lane = lax.broadcasted_iota(jnp.int32, (QB, LANE), 1)
        ik_ref[rows, :] = jnp.where(lane < IDX_DIM, rope_slab(tail, c, sa, sb), 0.0).astype(ik_ref.dtype)
        w_t = (tail * IDX_HEADS ** -0.5).T
        iwt_ref[t] = jnp.concatenate([w_t[IDX_DIM + h:IDX_DIM + h + 1, :] for h in range(IDX_HEADS)], axis=1)


def indexer_operands(proj, seq, tm=512):
    _, m, _ = proj.shape
    ntile = tm // QB
    tps = seq // tm
    cos, sin = _rope_tables(seq, IDX_ROPE)
    zero = jnp.zeros_like(sin)
    rest = IDX_DIM - IDX_ROPE
    per = LANE // IDX_DIM
    c_tab = jnp.tile(jnp.concatenate([cos, cos, jnp.ones((seq, rest), F32)], axis=1), (1, per))
    sa_tab = jnp.tile(jnp.concatenate([-sin, zero, jnp.zeros((seq, rest), F32)], axis=1), (1, per))
    sb_tab = jnp.tile(jnp.concatenate([zero, sin, jnp.zeros((seq, rest), F32)], axis=1), (1, per))
    lanes = IDX_HEADS * QB
    tab_spec = pl.BlockSpec((tm, LANE), lambda i: (i % tps, 0))
    return pl.pallas_call(
        functools.partial(_idx_prep_kernel, ntile=ntile),
        grid=(m // tm,),
        in_specs=[pl.BlockSpec((proj.shape[0], tm, LANE), lambda i: (0, i, 0)), tab_spec, tab_spec, tab_spec],
        out_specs=[pl.BlockSpec((ntile, LANE, lanes), lambda i: (i, 0, 0)),
                   pl.BlockSpec((tm, LANE), lambda i: (i, 0)),
                   pl.BlockSpec((ntile, 1, lanes), lambda i: (i, 0, 0))],
        out_shape=[jax.ShapeDtypeStruct((m // QB, LANE, lanes), BF16),
                   jax.ShapeDtypeStruct((m, LANE), BF16),
                   jax.ShapeDtypeStruct((m // QB, 1, lanes), F32)],
        compiler_params=_cparams(("arbitrary",)),
        name="dsa_indexer_operands",
    )(proj, c_tab, sa_tab, sb_tab)


def dsa_attention(logit_bound, iq_t, iw_t, ik, q_t, k, v_t, dt, bsz, seq):
    nq = seq // QB
    k_sel = min(DSA_TOPK_MAX, seq // 4)
    assert seq % KW == 0
    lanes = DSA_HPG * QB
    return pl.pallas_call(
        functools.partial(_dsa_kernel, seq=seq, k_sel=k_sel),
        grid=(bsz, nq),
        in_specs=[pl.BlockSpec(memory_space=pltpu.SMEM),
                  pl.BlockSpec((1, LANE, IDX_HEADS * QB), lambda b, i: (b * nq + i, 0, 0)),
                  pl.BlockSpec((1, 1, IDX_HEADS * QB), lambda b, i: (b * nq + i, 0, 0)),
                  pl.BlockSpec((seq, LANE), lambda b, i: (b, 0)),
                  pl.BlockSpec((DSA_HEADS, HEAD_DIM, QB), lambda b, i: (0, 0, b * nq + i)),
                  pl.BlockSpec((DSA_KV_HEADS, seq, HEAD_DIM), lambda b, i: (0, b, 0),
                               pipeline_mode=pl.Buffered(1)),
                  pl.BlockSpec((DSA_KV_HEADS, HEAD_DIM, seq), lambda b, i: (0, 0, b),
                               pipeline_mode=pl.Buffered(1)),
                  pl.BlockSpec((DSA_HEADS, 3, LANE, LANE), lambda b, i: (0, 0, 0, 0),
                               pipeline_mode=pl.Buffered(1))],
        out_specs=pl.BlockSpec((QB, DSA_HEADS * HEAD_DIM), lambda b, i: (b * nq + i, 0)),
        out_shape=jax.ShapeDtypeStruct((bsz * seq, DSA_HEADS * HEAD_DIM), BF16),
        scratch_shapes=[pltpu.VMEM((seq, QB), jnp.int32), pltpu.VMEM((seq, QB), F32)]
        + [pltpu.VMEM((1, lanes), F32), pltpu.VMEM((1, lanes), F32),
           pltpu.VMEM((HEAD_DIM, lanes), F32)] * DSA_KV_HEADS
        + [pltpu.VMEM((KW, lanes), F32)] * (2 * DSA_KV_HEADS)
        + [pltpu.VMEM((1, lanes), F32)] * (2 * DSA_KV_HEADS),
        compiler_params=_cparams(("arbitrary", "arbitrary")),
        name="dsa_attention",
    )(jnp.reshape(logit_bound, (1,)).astype(F32), iq_t, iw_t, ik, q_t, k, v_t, dt)


def _rope_tables(seq, dim):
    half = dim // 2
    inv = ROPE_THETA ** (-jnp.arange(half, dtype=F32) / half)
    ang = jnp.arange(seq, dtype=F32)[:, None] * inv[None, :]
    return jnp.cos(ang), jnp.sin(ang)


def _logit_bound(gq, gk, dim, scale):
    return dim * scale * jnp.max(jnp.abs(gq)) * jnp.max(jnp.abs(gk)) * (1.0 + 2.0 ** -7)


def _pad_cols(w, n):
    return jnp.pad(w, ((0, 0), (0, n - w.shape[1])))


def _t(x):
    return jnp.swapaxes(x, -1, -2)


def _even_mixer(h, x2, gate, dt, dc, bias_bound, bsz, seq, w_in, w_out, nsa_qk_g, cmp_pe, cmp_w1, cmp_b1,
                cmp_w2, cmp_b2, q_norm_g, kv_norm_g, w_uq, w_ukv, nope_g, rope_g):
    m = bsz * seq
    nq_cols = NSA_HEADS * HEAD_DIM
    nkv_cols = 6 * NSA_GROUPS * HEAD_DIM
    ngate = 3 * NSA_HEADS
    o_gate = nq_cols + nkv_cols
    o_cq = o_gate + ngate
    o_ckv = o_cq + MLA_Q_RANK
    o_kpe = o_ckv + MLA_KV_RANK
    gw = NSA_GROUPS * HEAD_DIM
    kvw = [w_in[:, nq_cols + i * gw:nq_cols + (i + 1) * gw] for i in range(6)]
    scale = HEAD_DIM ** -0.5 * LOG2E
    q_t = proj_heads(h, w_in[:, :nq_cols].astype(BF16), nsa_qk_g[0] * scale, transpose=True)
    k_sw = proj_heads(h, jnp.concatenate([kvw[2], kvw[4]], axis=1).astype(BF16), nsa_qk_g[1],
                      transpose=False)
    v_sw_t = proj_heads(h, jnp.concatenate([kvw[3], kvw[5]], axis=1).astype(BF16), transpose=True)
    tail = jnp.concatenate([w_in[:, o_kpe:], w_in[:, o_gate:o_cq]], axis=1)
    w_r = jnp.concatenate([kvw[0], kvw[1], w_in[:, o_cq:o_kpe], _pad_cols(tail, LANE)], axis=1).astype(BF16)
    proj = proj_slabs(h, w_r, tn=w_r.shape[1])
    s_cq = 2 * NSA_GROUPS
    s_ckv = s_cq + MLA_Q_RANK // LANE
    s_tail = s_ckv + MLA_KV_RANK // LANE
    kvc = compress_kv(proj, 0, bsz, seq, cmp_pe, cmp_w1, cmp_b1, cmp_w2, cmp_b2, nsa_qk_g[1])
    tail_v = proj[s_tail]
    gates = tail_v[:, MLA_ROPE:MLA_ROPE + ngate].reshape(m, NSA_GROUPS, 3 * NSA_HPG)
    gates_t = jnp.pad(jnp.transpose(gates, (1, 2, 0)), ((0, 0), (0, 16 - 3 * NSA_HPG), (0, 0)))
    nsa_bound = _logit_bound(nsa_qk_g[0], nsa_qk_g[1], HEAD_DIM, scale) + bias_bound
    o_nsa = nsa_attention(nsa_bound, q_t, gates_t, kvc[0], _t(kvc[1]), k_sw, v_sw_t,
                          dt[:NSA_HEADS], dc[:NSA_HEADS], bsz, seq)

    dq = MLA_NOPE + MLA_ROPE
    wq = w_uq.reshape(MLA_Q_RANK, MLA_HEADS, dq)
    wq_r = jnp.concatenate([wq[:, :, :MLA_NOPE].reshape(MLA_Q_RANK, -1),
                            wq[:, :, MLA_NOPE:].reshape(MLA_Q_RANK, -1)], axis=1).astype(BF16)
    cos, sin = _rope_tables(seq, MLA_ROPE)
    mscale = dq ** -0.5 * LOG2E
    side = [jnp.sqrt(MLA_NOPE * jnp.max(jnp.abs(nope_g[i])) ** 2 + MLA_ROPE * jnp.max(jnp.abs(rope_g[i])) ** 2)
            for i in range(2)]
    mla_bound = mscale * side[0] * side[1] * (1.0 + 2.0 ** -7)
    mla_shift = jnp.where(mla_bound <= SAFE_LOG2_BOUND, mla_bound, 0.0)
    q_mla_t, k_mla, v_mla_t = mla_project(proj, s_cq, s_ckv, s_tail, seq, q_norm_g, kv_norm_g, wq_r,
                                          w_ukv.astype(BF16), nope_g, rope_g, cos, sin, mscale, mla_shift)
    o_mla = mla_attention(mla_bound, q_mla_t, k_mla, v_mla_t, bsz, seq)
    w_o = w_out.astype(BF16)
    return resproj([(o_nsa, w_o[:nq_cols]), (o_mla, w_o[nq_cols:])], x2, gate, seq)


def _odd_mixer(h, x2, gate, dt, bias_bound, bsz, seq, w_in, w_out, qk_g):
    nq = DSA_HEADS * HEAD_DIM
    nkv = DSA_KV_HEADS * HEAD_DIM
    niq = IDX_HEADS * IDX_DIM
    o_k, o_v, o_iq = nq, nq + nkv, nq + 2 * nkv
    q_t = proj_heads(h, w_in[:, :o_k].astype(BF16), qk_g[0] * (HEAD_DIM ** -0.5 * LOG2E), transpose=True)
    k = proj_heads(h, w_in[:, o_k:o_v].astype(BF16), qk_g[1], transpose=False)
    v_t = proj_heads(h, w_in[:, o_v:o_iq].astype(BF16), transpose=True)
    w_idx = w_in[:, o_iq:]
    proj = proj_slabs(h, _pad_cols(w_idx, niq + LANE).astype(BF16), tn=niq + LANE)
    iq_t, ik, iw_t = indexer_operands(proj, seq)
    bound = _logit_bound(qk_g[0], qk_g[1], HEAD_DIM, HEAD_DIM ** -0.5 * LOG2E) + bias_bound
    o = dsa_attention(bound, iq_t, iw_t, ik, q_t, k, v_t, dt, bsz, seq)
    return resproj([(o, w_out.astype(BF16))], x2, gate, seq)


def _conv_ffn(h, x2, gate, seq, w_up, conv_w, conv_b, w_down):
    a = ffn_up(h, w_up.astype(BF16), conv_w, conv_b, seq)
    return resproj([(a, w_down.astype(BF16))], x2, gate, seq)


def kernel(x, c, rel_bias, ada_w, ada_b, norm_g, ev_w_in, ev_w_out, nsa_qk_g, cmp_pe, cmp_w1, cmp_b1, cmp_w2, cmp_b2, mla_q_norm_g, mla_kv_norm_g, mla_w_uq, mla_w_ukv, mla_nope_g, mla_rope_g, od_w_in, od_w_out, dsa_qk_g, ffn_w_up, ffn_conv_w, ffn_conv_b, ffn_w_down):
    bsz, seq, d = x.shape
    depth = ada_w.shape[0]
    x2 = x.reshape(bsz * seq, d)
    mods = ada_all(c, ada_w, ada_b)
    dt, dc = bias_tiles(rel_bias)
    bias_bound = 2.0 * LOG2E * jnp.max(jnp.abs(rel_bias))
    for i in range(depth):
        j = i // 2
        shift, scale, gate = jnp.split(mods[i, 0], 3, axis=-1)
        h = modnorm(x2, norm_g[i, 0], scale, shift, seq)
        if i % 2 == 0:
            x2 = _even_mixer(h, x2, gate, dt, dc, bias_bound, bsz, seq, ev_w_in[j], ev_w_out[j], nsa_qk_g[j],
                             cmp_pe[j], cmp_w1[j], cmp_b1[j], cmp_w2[j], cmp_b2[j], mla_q_norm_g[j],
                             mla_kv_norm_g[j], mla_w_uq[j], mla_w_ukv[j], mla_nope_g[j], mla_rope_g[j])
        else:
            x2 = _odd_mixer(h, x2, gate, dt, bias_bound, bsz, seq, od_w_in[j], od_w_out[j], dsa_qk_g[j])
        shift, scale, gate = jnp.split(mods[i, 1], 3, axis=-1)
        h = modnorm(x2, norm_g[i, 1], scale, shift, seq)
        x2 = _conv_ffn(h, x2, gate, seq, ffn_w_up[i], ffn_conv_w[i], ffn_conv_b[i], ffn_w_down[i])
    return x2.reshape(bsz, seq, d)
```

```python
import functools
import math

import numpy as np
import jax
import jax.numpy as jnp
from jax import lax
from jax.experimental import pallas as pl
from jax.experimental.pallas import tpu as pltpu

HEAD_DIM = 128
NSA_HEADS = 8
NSA_GROUPS = 2
NSA_HPG = NSA_HEADS // NSA_GROUPS
CMP_BLOCK = 32
CMP_STRIDE = 16
CMP_HIDDEN = 256
SEL_BLOCK = 64
SEL_TOP_N = 16
WINDOW = 512
MLA_HEADS = 8
MLA_Q_RANK = 512
MLA_KV_RANK = 256
MLA_NOPE = 128
MLA_ROPE = 64
MLA_V = 128
DSA_HEADS = 16
DSA_KV_HEADS = 4
DSA_HPG = DSA_HEADS // DSA_KV_HEADS
IDX_HEADS = 16
IDX_DIM = 64
IDX_ROPE = 32
DSA_TOPK_MAX = 256
REL_BUCKETS = 32
REL_MAX_DIST = 128
CONV_WIDTH = 3
ROPE_THETA = 10000.0
EPS = 1e-6
NEG = -1e30
FORCE = 1e9

LANE = 128
QB = 128
VMEM_LIMIT = 56 * 1024 * 1024

F32 = jnp.float32
BF16 = jnp.bfloat16


def _t5_thresholds():
    d = np.arange(0, 4 * REL_MAX_DIST)
    half = REL_BUCKETS // 2
    val = np.log(np.maximum(d, 1) / half) / math.log(REL_MAX_DIST / half) * (REL_BUCKETS - half)
    large = np.minimum(half + np.floor(np.maximum(val, 0.0)).astype(np.int64), REL_BUCKETS - 1)
    bucket = np.where(d < half, d, large)
    return [int(np.argmax(bucket >= b)) for b in range(1, REL_BUCKETS)]


T5_THR = _t5_thresholds()
T5_FAR = T5_THR[-1]
assert T5_FAR <= LANE


def _cparams(sem):
    return pltpu.CompilerParams(dimension_semantics=sem, vmem_limit_bytes=VMEM_LIMIT)


def _dot(a, b):
    return jnp.dot(a, b, preferred_element_type=F32)


def _ada_kernel(c_ref, w_ref, b_ref, o_ref):
    c = c_ref[...]
    a = c * jax.nn.sigmoid(c)
    o_ref[0] = jnp.dot(a, w_ref[0], preferred_element_type=F32,
                       precision=lax.Precision.HIGHEST) + b_ref[0]


def ada_all(c, ada_w, ada_b):
    depth, two, d, n3 = ada_w.shape
    bsz = c.shape[0]
    rows = 8
    cp = jnp.zeros((rows, d), F32).at[:bsz].set(c)
    w = ada_w.reshape(depth * two, d, n3)
    b = ada_b.reshape(depth * two, 1, n3)
    tn = 512
    out = pl.pallas_call(
        _ada_kernel,
        grid=(depth * two, n3 // tn),
        in_specs=[pl.BlockSpec((rows, d), lambda l, j: (0, 0)),
                  pl.BlockSpec((1, d, tn), lambda l, j: (l, 0, j)),
                  pl.BlockSpec((1, 1, tn), lambda l, j: (l, 0, j))],
        out_specs=pl.BlockSpec((1, rows, tn), lambda l, j: (l, 0, j)),
        out_shape=jax.ShapeDtypeStruct((depth * two, rows, n3), F32),
        compiler_params=_cparams(("arbitrary", "arbitrary")),
        name="ada_mod",
    )(cp, w, b)
    return out[:, :bsz].reshape(depth, two, bsz, n3)


def _modnorm_kernel(x_ref, g_ref, sc_ref, sh_ref, o_ref):
    x = x_ref[...]
    y = x * lax.rsqrt(jnp.mean(x * x, axis=-1, keepdims=True) + EPS)
    h = (y * g_ref[...]) * (1.0 + sc_ref[0]) + sh_ref[0]
    o_ref[...] = h.astype(o_ref.dtype)


def modnorm(x2, g, scale, shift, seq):
    m, d = x2.shape
    tm = 512
    tpb = seq // tm
    return pl.pallas_call(
        _modnorm_kernel,
        grid=(m // tm,),
        in_specs=[pl.BlockSpec((tm, d), lambda i: (i, 0)),
                  pl.BlockSpec((1, d), lambda i: (0, 0)),
                  pl.BlockSpec((1, 1, d), lambda i: (i // tpb, 0, 0)),
                  pl.BlockSpec((1, 1, d), lambda i: (i // tpb, 0, 0))],
        out_specs=pl.BlockSpec((tm, d), lambda i: (i, 0)),
        out_shape=jax.ShapeDtypeStruct((m, d), BF16),
        compiler_params=_cparams(("arbitrary",)),
        name="modnorm",
    )(x2, g.reshape(1, d), scale.reshape(-1, 1, d), shift.reshape(-1, 1, d))


def _proj_kernel(x_ref, w_ref, o_ref, *, nslab):
    acc = _dot(x_ref[...], w_ref[...])
    for s in range(nslab):
        o_ref[s] = acc[:, s * LANE:(s + 1) * LANE]


def proj_slabs(x, w, tm=1024, tn=384):
    m, k = x.shape
    n = w.shape[1]
    assert n % tn == 0 and m % tm == 0
    nslab = tn // LANE
    return pl.pallas_call(
        functools.partial(_proj_kernel, nslab=nslab),
        grid=(m // tm, n // tn),
        in_specs=[pl.BlockSpec((tm, k), lambda i, j: (i, 0)),
                  pl.BlockSpec((k, tn), lambda i, j: (0, j))],
        out_specs=pl.BlockSpec((nslab, tm, LANE), lambda i, j: (j, i, 0)),
        out_shape=jax.ShapeDtypeStruct((n // LANE, m, LANE), F32),
        compiler_params=_cparams(("arbitrary", "arbitrary")),
        name="proj_slabs",
    )(x, w)


def _proj_heads_kernel(x_ref, w_ref, g_ref, o_ref, *, nslab, norm, transpose):
    acc = _dot(x_ref[...], w_ref[...])
    for s in range(nslab):
        y = acc[:, s * LANE:(s + 1) * LANE]
        if norm:
            y = y * lax.rsqrt(jnp.mean(y * y, axis=-1, keepdims=True) + EPS) * g_ref[...]
        o_ref[s] = (y.T if transpose else y).astype(o_ref.dtype)


def proj_heads(x, w, g=None, *, transpose, tm=1024, tn=1024):
    m, k = x.shape
    n = w.shape[1]
    tn = min(tn, n)
    assert n % tn == 0 and m % tm == 0
    nslab = tn // LANE
    norm = g is not None
    if transpose:
        out_spec = pl.BlockSpec((nslab, LANE, tm), lambda i, j: (j, 0, i))
        out_shape = jax.ShapeDtypeStruct((n // LANE, LANE, m), BF16)
    else:
        out_spec = pl.BlockSpec((nslab, tm, LANE), lambda i, j: (j, i, 0))
        out_shape = jax.ShapeDtypeStruct((n // LANE, m, LANE), BF16)
    g2 = (g if norm else jnp.ones((LANE,), F32)).reshape(1, LANE)
    return pl.pallas_call(
        functools.partial(_proj_heads_kernel, nslab=nslab, norm=norm, transpose=transpose),
        grid=(m // tm, n // tn),
        in_specs=[pl.BlockSpec((tm, k), lambda i, j: (i, 0)),
                  pl.BlockSpec((k, tn), lambda i, j: (0, j)),
                  pl.BlockSpec((1, LANE), lambda i, j: (0, 0))],
        out_specs=out_spec,
        out_shape=out_shape,
        compiler_params=_cparams(("arbitrary", "arbitrary")),
        name="proj_heads",
    )(x, w, g2)


def _rms_rows(x, g):
    return x * lax.rsqrt(jnp.mean(x * x, axis=-1, keepdims=True) + EPS) * g


def _rope_rows(x, cos, sin):
    half = x.shape[-1] // 2
    x1, x2 = x[:, :half], x[:, half:]
    return jnp.concatenate([x1 * cos - x2 * sin, x1 * sin + x2 * cos], axis=1)


def _latent(x_ref, g_ref):
    x = jnp.concatenate([x_ref[s] for s in range(x_ref.shape[0])], axis=1)
    return _rms_rows(x, g_ref[...]).astype(BF16)


def _mla_q_kernel(shift_ref, x_ref, g_ref, w_ref, gn_ref, gr_ref, cos_ref, sin_ref, o_ref, *, scale):
    acc = _dot(_latent(x_ref, g_ref), w_ref[...])
    tm = acc.shape[0]
    cos, sin = cos_ref[...], sin_ref[...]
    first = lax.broadcasted_iota(jnp.int32, (tm, LANE - MLA_ROPE), 1) == 0
    pad = jnp.where(first, -shift_ref[0], 0.0)
    for h in range(MLA_HEADS):
        nope = _rms_rows(acc[:, h * MLA_NOPE:(h + 1) * MLA_NOPE], gn_ref[...]) * scale
        r0 = MLA_HEADS * MLA_NOPE + h * MLA_ROPE
        pe = _rope_rows(_rms_rows(acc[:, r0:r0 + MLA_ROPE], gr_ref[...]), cos, sin) * scale
        o_ref[h, 0:MLA_NOPE, :] = nope.T.astype(o_ref.dtype)
        o_ref[h, MLA_NOPE:MLA_NOPE + LANE, :] = jnp.concatenate([pe, pad], axis=1).T.astype(o_ref.dtype)


def _mla_kv_kernel(x_ref, g_ref, w_ref, tail_ref, gn_ref, gr_ref, cos_ref, sin_ref, ok_ref, ov_ref):
    acc = _dot(_latent(x_ref, g_ref), w_ref[...])
    tm = acc.shape[0]
    k_pe = _rope_rows(_rms_rows(tail_ref[0][:, :MLA_ROPE], gr_ref[...]), cos_ref[...], sin_ref[...])
    first = lax.broadcasted_iota(jnp.int32, (tm, LANE - MLA_ROPE), 1) == 0
    k_pe = jnp.concatenate([k_pe, jnp.where(first, 1.0, 0.0)], axis=1).astype(ok_ref.dtype)
    for h in range(MLA_HEADS):
        c0 = h * (MLA_NOPE + MLA_V)
        ok_ref[h, :, 0:MLA_NOPE] = _rms_rows(acc[:, c0:c0 + MLA_NOPE], gn_ref[...]).astype(ok_ref.dtype)
        ok_ref[h, :, MLA_NOPE:MLA_NOPE + LANE] = k_pe
        ov_ref[h] = acc[:, c0 + MLA_NOPE:c0 + MLA_NOPE + MLA_V].T.astype(ov_ref.dtype)


def mla_project(proj, s_cq, s_ckv, s_tail, seq, q_norm_g, kv_norm_g, wq_r, w_ukv, nope_g, rope_g, cos, sin,
                scale, shift, tm=512):
    _, m, _ = proj.shape
    kq, kkv = s_ckv - s_cq, s_tail - s_ckv
    tps = seq // tm
    dqk = MLA_NOPE + LANE
    half = MLA_ROPE // 2
    rope_specs = [pl.BlockSpec((tm, half), lambda i: (i % tps, 0))] * 2
    gain_specs = [pl.BlockSpec((1, MLA_NOPE), lambda i: (0, 0)), pl.BlockSpec((1, MLA_ROPE), lambda i: (0, 0))]
    q_t = pl.pallas_call(
        functools.partial(_mla_q_kernel, scale=scale),
        grid=(m // tm,),
        in_specs=[pl.BlockSpec(memory_space=pltpu.SMEM),
                  pl.BlockSpec((kq, tm, LANE), lambda i: (s_cq // kq, i, 0)),
                  pl.BlockSpec((1, kq * LANE), lambda i: (0, 0)),
                  pl.BlockSpec(wq_r.shape, lambda i: (0, 0))] + gain_specs + rope_specs,
        out_specs=pl.BlockSpec((MLA_HEADS, dqk, tm), lambda i: (0, 0, i)),
        out_shape=jax.ShapeDtypeStruct((MLA_HEADS, dqk, m), BF16),
        compiler_params=_cparams(("arbitrary",)),
        name="mla_q_project",
    )(jnp.reshape(shift, (1,)).astype(F32), proj, q_norm_g.reshape(1, -1), wq_r, nope_g[0].reshape(1, -1),
      rope_g[0].reshape(1, -1), cos, sin)
    k, v_t = pl.pallas_call(
        _mla_kv_kernel,
        grid=(m // tm,),
        in_specs=[pl.BlockSpec((kkv, tm, LANE), lambda i: (s_ckv // kkv, i, 0)),
                  pl.BlockSpec((1, kkv * LANE), lambda i: (0, 0)),
                  pl.BlockSpec(w_ukv.shape, lambda i: (0, 0)),
                  pl.BlockSpec((1, tm, LANE), lambda i: (s_tail, i, 0))] + gain_specs + rope_specs,
        out_specs=[pl.BlockSpec((MLA_HEADS, tm, dqk), lambda i: (0, i, 0)),
                   pl.BlockSpec((MLA_HEADS, MLA_V, tm), lambda i: (0, 0, i))],
        out_shape=[jax.ShapeDtypeStruct((MLA_HEADS, m, dqk), BF16),
                   jax.ShapeDtypeStruct((MLA_HEADS, MLA_V, m), BF16)],
        compiler_params=_cparams(("arbitrary",)),
        name="mla_kv_project",
    )(proj, kv_norm_g.reshape(1, -1), w_ukv, proj, nope_g[1].reshape(1, -1), rope_g[1].reshape(1, -1), cos, sin)
    return q_t, k, v_t


def _resproj_kernel(*refs, npair):
    xres_ref, gate_ref = refs[2 * npair], refs[2 * npair + 1]
    o_ref = refs[2 * npair + 2]
    acc = _dot(refs[0][...], refs[1][...])
    for p in range(1, npair):
        acc = acc + _dot(refs[2 * p][...], refs[2 * p + 1][...])
    o_ref[...] = xres_ref[...] + gate_ref[0] * acc


def resproj(pairs, xres, gate, seq, tm=1024, tn=512):
    m, n = xres.shape
    tpb = seq // tm
    in_specs, args = [], []
    for x, w in pairs:
        k = x.shape[1]
        in_specs += [pl.BlockSpec((tm, k), lambda i, j: (i, 0)),
                     pl.BlockSpec((k, tn), lambda i, j: (0, j))]
        args += [x, w]
    in_specs += [pl.BlockSpec((tm, tn), lambda i, j: (i, j)),
                 pl.BlockSpec((1, 1, tn), lambda i, j: (i // tpb, 0, j))]
    args += [xres, gate.reshape(-1, 1, n)]
    return pl.pallas_call(
        functools.partial(_resproj_kernel, npair=len(pairs)),
        grid=(m // tm, n // tn),
        in_specs=in_specs,
        out_specs=pl.BlockSpec((tm, tn), lambda i, j: (i, j)),
        out_shape=jax.ShapeDtypeStruct((m, n), F32),
        compiler_params=_cparams(("arbitrary", "arbitrary")),
        name="resproj",
    )(*args)


HALO = 8


def _ffn_up_kernel(h_ref, wg_ref, wv_ref, cwg_ref, cwv_ref, cbg_ref, cbv_ref, o_ref,
                   ug_ref, uv_ref, *, tm, tiles_per_seq):
    i = pl.program_id(1)
    first = (i % tiles_per_seq) == 0

    @pl.when(first)
    def _():
        ug_ref[0:HALO, :] = jnp.zeros((HALO, ug_ref.shape[1]), F32)
        uv_ref[0:HALO, :] = jnp.zeros((HALO, uv_ref.shape[1]), F32)

    @pl.when(jnp.logical_not(first))
    def _():
        ug_ref[0:HALO, :] = ug_ref[tm:tm + HALO, :]
        uv_ref[0:HALO, :] = uv_ref[tm:tm + HALO, :]

    h = h_ref[...]
    ug_ref[HALO:HALO + tm, :] = _dot(h, wg_ref[...])
    uv_ref[HALO:HALO + tm, :] = _dot(h, wv_ref[...])

    def conv(u_ref, cw_ref, cb_ref):
        out = cb_ref[...]
        for j in range(CONV_WIDTH):
            off = HALO - (CONV_WIDTH - 1) + j
            out = out + cw_ref[j:j + 1, :] * u_ref[off:off + tm, :]
        return out

    g = conv(ug_ref, cwg_ref, cbg_ref)
    v = conv(uv_ref, cwv_ref, cbv_ref)
    o_ref[...] = (g * jax.nn.sigmoid(g) * v).astype(o_ref.dtype)


def ffn_up(h, w_up, conv_w, conv_b, seq, tm=1024, tn=512):
    m, d = h.shape
    f = w_up.shape[1] // 2
    nj = f // tn
    tps = seq // tm
    cb = conv_b.reshape(1, 2 * f)
    return pl.pallas_call(
        functools.partial(_ffn_up_kernel, tm=tm, tiles_per_seq=tps),
        grid=(nj, m // tm),
        in_specs=[pl.BlockSpec((tm, d), lambda j, i: (i, 0)),
                  pl.BlockSpec((d, tn), lambda j, i: (0, j)),
                  pl.BlockSpec((d, tn), lambda j, i: (0, nj + j)),
                  pl.BlockSpec((CONV_WIDTH, tn), lambda j, i: (0, j)),
                  pl.BlockSpec((CONV_WIDTH, tn), lambda j, i: (0, nj + j)),
                  pl.BlockSpec((1, tn), lambda j, i: (0, j)),
                  pl.BlockSpec((1, tn), lambda j, i: (0, nj + j))],
        out_specs=pl.BlockSpec((tm, tn), lambda j, i: (i, j)),
        out_shape=jax.ShapeDtypeStruct((m, f), BF16),
        scratch_shapes=[pltpu.VMEM((tm + HALO, tn), F32), pltpu.VMEM((tm + HALO, tn), F32)],
        compiler_params=_cparams(("arbitrary", "arbitrary")),
        name="ffn_up_conv",
    )(h, w_up, w_up, conv_w, conv_w, cb, cb)


LOG2E = 1.4426950408889634
CWIN = 16


def _t5_shifted(dist, tbl_ref, h):
    val = jnp.full(dist.shape, tbl_ref[0, h], F32)
    for b in range(1, REL_BUCKETS):
        val = jnp.where(dist >= T5_THR[b - 1], tbl_ref[b, h], val)
    return (val - tbl_ref[REL_BUCKETS - 1, h]) * LOG2E


def _bias_tiles_kernel(tbl_ref, dt_ref, dc_ref):
    h = pl.program_id(0)
    key = lax.broadcasted_iota(jnp.int32, (LANE, LANE), 0)
    q = lax.broadcasted_iota(jnp.int32, (LANE, LANE), 1)
    for rel in range(2):
        dt_ref[0, rel] = _t5_shifted(rel * LANE + q - key, tbl_ref, h)
    dt_ref[0, 2] = jnp.zeros((LANE, LANE), F32)
    u = lax.broadcasted_iota(jnp.int32, (CWIN, LANE), 0)
    qc = lax.broadcasted_iota(jnp.int32, (CWIN, LANE), 1)
    dc_ref[0] = _t5_shifted(qc - CMP_STRIDE * (u - CWIN // 2) - (CMP_BLOCK - 1), tbl_ref, h)


def bias_tiles(rel_bias):
    nh = rel_bias.shape[1]
    return pl.pallas_call(
        _bias_tiles_kernel,
        grid=(nh,),
        in_specs=[pl.BlockSpec(memory_space=pltpu.SMEM)],
        out_specs=[pl.BlockSpec((1, 3, LANE, LANE), lambda h: (h, 0, 0, 0)),
                   pl.BlockSpec((1, CWIN, LANE), lambda h: (h, 0, 0))],
        out_shape=[jax.ShapeDtypeStruct((nh, 3, LANE, LANE), F32),
                   jax.ShapeDtypeStruct((nh, CWIN, LANE), F32)],
        compiler_params=_cparams(("arbitrary",)),
        name="t5_bias_tiles",
    )(rel_bias)


def _compress_kernel(x_ref, pe_ref, w1_ref, b1_ref, w2_ref, b2_ref, g_ref, o_ref, *, half):
    kv = pl.program_id(0)
    x = x_ref[0]
    a = _dot((x + pe_ref[0, :, :half]).astype(BF16), w1_ref[0, :half, :])
    b = _dot((x + pe_ref[0, :, half:]).astype(BF16), w1_ref[0, half:, :])
    b_next = jnp.concatenate([b[1:], jnp.zeros((1, b.shape[1]), F32)], axis=0)
    hid = jax.nn.gelu(a + b_next + b1_ref[0])
    out = _dot(hid.astype(BF16), w2_ref[0]) + b2_ref[0]
    normed = out * lax.rsqrt(jnp.mean(out * out, axis=-1, keepdims=True) + EPS) * g_ref[...]
    out = jnp.where(kv == 0, normed, out)
    o_ref[0, 0] = out.astype(o_ref.dtype)


def compress_kv(proj, slab0, bsz, seq, cmp_pe, cmp_w1, cmp_b1, cmp_w2, cmp_b2, g_k):
    nslab, m, _ = proj.shape
    nchunk = seq // CMP_STRIDE
    half = CMP_STRIDE * HEAD_DIM
    xv = proj.reshape(nslab, m // CMP_STRIDE, half)
    pe = cmp_pe.reshape(2, 1, CMP_BLOCK * HEAD_DIM)
    return pl.pallas_call(
        functools.partial(_compress_kernel, half=half),
        grid=(2, bsz, NSA_GROUPS),
        in_specs=[pl.BlockSpec((1, nchunk, half), lambda kv, b, g: (slab0 + 2 * kv + g, b, 0)),
                  pl.BlockSpec((1, 1, 2 * half), lambda kv, b, g: (kv, 0, 0)),
                  pl.BlockSpec((1, 2 * half, CMP_HIDDEN), lambda kv, b, g: (kv, 0, 0)),
                  pl.BlockSpec((1, 1, CMP_HIDDEN), lambda kv, b, g: (kv, 0, 0)),
                  pl.BlockSpec((1, CMP_HIDDEN, HEAD_DIM), lambda kv, b, g: (kv, 0, 0)),
                  pl.BlockSpec((1, 1, HEAD_DIM), lambda kv, b, g: (kv, 0, 0)),
                  pl.BlockSpec((1, HEAD_DIM), lambda kv, b, g: (0, 0))],
        out_specs=pl.BlockSpec((1, 1, nchunk, HEAD_DIM), lambda kv, b, g: (kv, g, b, 0)),
        out_shape=jax.ShapeDtypeStruct((2, NSA_GROUPS, bsz * nchunk, HEAD_DIM), BF16),
        compiler_params=_cparams(("arbitrary", "arbitrary", "arbitrary")),
        name="nsa_compress",
    )(xv, pe, cmp_w1.astype(BF16), cmp_b1.reshape(2, 1, CMP_HIDDEN), cmp_w2.astype(BF16),
      cmp_b2.reshape(2, 1, HEAD_DIM), g_k.reshape(1, HEAD_DIM))


KW = 512
PV_KEYS = 256


def _tile_lanes(x, n):
    return jnp.concatenate([x] * n, axis=1)


def _flash_init(m_ref, l_ref, acc_ref):
    m_ref[...] = jnp.full(m_ref.shape, NEG, F32)
    l_ref[...] = jnp.zeros(l_ref.shape, F32)
    acc_ref[...] = jnp.zeros(acc_ref.shape, F32)


def _zero_after(x):
    bits = pltpu.bitcast(x, jnp.int32)
    return lax.shift_right_logical(lax.shift_right_logical(bits, 16), 16).astype(F32)


def _flash_update(s, v_t, m_ref, l_ref, acc_ref, col_max=None, after=None):
    m_old = m_ref[...]
    if col_max is None:
        col_max = jnp.max(s, axis=0, keepdims=True)
    m_new = jnp.maximum(m_old, col_max)
    alpha = jnp.exp2(m_old - m_new)
    l_new = alpha * l_ref[...]
    acc = alpha * acc_ref[...]
    nk = s.shape[0]
    for k0 in range(0, nk, PV_KEYS):
        p = jnp.exp2(s[k0:k0 + PV_KEYS] - m_new)
        l_new = l_new + jnp.sum(p, axis=0, keepdims=True)
        acc = acc + _dot(v_t[:, k0:k0 + PV_KEYS], p.astype(BF16))
    l_ref[...] = l_new
    acc_ref[...] = acc
    m_ref[...] = m_new if after is None else m_new + _zero_after(after)


SAFE_LOG2_BOUND = 60.0


def _flash_accumulate(s, v_t, l_ref, acc_ref, after=None):
    l_new = l_ref[...]
    acc = acc_ref[...]
    for k0 in range(0, s.shape[0], PV_KEYS):
        p = jnp.exp2(s[k0:k0 + PV_KEYS])
        l_new = l_new + jnp.sum(p, axis=0, keepdims=True)
        acc = acc + _dot(v_t[:, k0:k0 + PV_KEYS], p.astype(BF16))
    if after is not None:
        l_new = l_new + jnp.max(_zero_after(after), axis=0, keepdims=True)
    l_ref[...] = l_new
    acc_ref[...] = acc


def _sum_result(l_ref, acc_ref):
    den = l_ref[...]
    ok = den > 0.0
    return acc_ref[...] * jnp.where(ok, 1.0 / jnp.where(ok, den, 1.0), 0.0)


def _inv_den(m, den):
    ok = m > 0.5 * NEG
    return jnp.where(ok, 1.0 / jnp.where(ok, den, 1.0), 0.0)


def _flash_result(m_ref, l_ref, acc_ref):
    return acc_ref[...] * _inv_den(m_ref[...], l_ref[...])


def _softmax_cols(s):
    m = jnp.max(s, axis=0, keepdims=True)
    p = jnp.exp2(s - m)
    return p * _inv_den(m, jnp.sum(p, axis=0, keepdims=True))


def _near_bias(dt_ref, heads, qi, kt0, ntile):
    rows = []
    for j in range(ntile):
        rel = jnp.clip(qi - (kt0 + j), 0, 2)
        rows.append(jnp.concatenate([dt_ref[h, rel] for h in heads], axis=1))
    return jnp.concatenate(rows, axis=0)


def _pipelined_chunks(n, qk_stage, soft_stage):
    @pl.when(n > 0)
    def _():
        qk_stage(0, 0)

    def pair(p, x):
        c = 2 * p
        ahead = qk_stage(c + 1, 1)
        soft_stage(c, 0, ahead)
        ahead = qk_stage(jnp.minimum(c + 2, n - 1), 0)
        soft_stage(c + 1, 1, ahead)
        return x

    lax.fori_loop(0, n // 2, pair, 0)

    @pl.when(n % 2 == 1)
    def _():
        soft_stage(n - 1, 0, None)


NSA_STATE = 9


def _nsa_kernel(bound_ref, qt_ref, gt_ref, kc_ref, vct_ref, ks_ref, vst_ref, kw_ref, vwt_ref,
                dt_ref, dc_ref, ext_ref, o_ref, *scratch, seq, nc):
    ng = NSA_GROUPS
    state = [scratch[NSA_STATE * g:NSA_STATE * (g + 1)] for g in range(ng)]
    qi = pl.program_id(1)
    q0 = qi * QB
    hpg = NSA_HPG
    ncp = kc_ref.shape[1]
    ns = seq // SEL_BLOCK
    group_heads = [[g * hpg + h for h in range(hpg)] for g in range(ng)]
    q_ts = [jnp.concatenate([qt_ref[h] for h in group_heads[g]], axis=1) for g in range(ng)]
    pad = CWIN // 2
    wkeys = WINDOW + QB
    start = pl.multiple_of(jnp.maximum(q0 - WINDOW, 0), LANE)
    r0 = pl.multiple_of(qi * (QB // CMP_STRIDE), 8)

    s_w = []
    for g in range(ng):
        sc_ref = state[g][0]
        sc_ref[0:pad, :] = jnp.zeros((pad, hpg * QB), F32)
        sc_ref[pad + ncp:2 * pad + ncp, :] = jnp.zeros((pad, hpg * QB), F32)
        sc_ref[pad:pad + ncp, :] = _dot(kc_ref[g], q_ts[g])
        s_w.append(_dot(kw_ref[g, pl.ds(start, wkeys), :], q_ts[g]))

    ci = lax.broadcasted_iota(jnp.int32, (ncp, QB), 0)
    tc = q0 + lax.broadcasted_iota(jnp.int32, (ncp, QB), 1)
    valid_c = (ci * CMP_STRIDE + CMP_BLOCK - 1 <= tc) & (ci < nc)
    madd_c = _tile_lanes(jnp.where(valid_c, 0.0, NEG), hpg)
    oc_t, p_sum = [], []
    for g in range(ng):
        sc_ref = state[g][0]
        sc_ref[pl.ds(r0, CWIN), :] = sc_ref[pl.ds(r0, CWIN), :] + jnp.concatenate(
            [dc_ref[h] for h in group_heads[g]], axis=1)
        p_c = _softmax_cols(sc_ref[pad:pad + ncp, :] + madd_c)
        oc_t.append(_dot(vct_ref[g], p_c.astype(BF16)))
        ps = p_c[:, 0:QB]
        for h in range(1, hpg):
            ps = ps + p_c[:, h * QB:(h + 1) * QB]
        p_sum.append(ps)

    dist_w = (q0 + lax.broadcasted_iota(jnp.int32, (wkeys, QB), 1)) - (
        start + lax.broadcasted_iota(jnp.int32, (wkeys, QB), 0))
    madd_w = _tile_lanes(jnp.where((dist_w >= 0) & (dist_w < WINDOW), 0.0, NEG), hpg)
    ow_t = []
    for g in range(ng):
        p_w = _softmax_cols(s_w[g] + _near_bias(dt_ref, group_heads[g], qi, start // LANE, wkeys // LANE) + madd_w)
        ow_t.append(_dot(vwt_ref[g, :, pl.ds(start, wkeys)], p_w.astype(BF16)))

    per = SEL_BLOCK // CMP_STRIDE
    blk = lax.broadcasted_iota(jnp.int32, (LANE, QB), 0)
    t = q0 + lax.broadcasted_iota(jnp.int32, (LANE, QB), 1)
    tb = t // SEL_BLOCK
    forced = (blk == 0) | (blk == tb) | (blk == tb - 1)
    blk_f = blk.astype(F32)
    scores = []
    for g in range(ng):
        ps_ref = state[g][1]
        ps_ref[0:8, :] = jnp.zeros((8, QB), F32)
        ps_ref[8:8 + ncp, :] = p_sum[g]
        band = [ps_ref[pl.ds(8 + r, ns, stride=per), :] for r in range(-1, per)]
        imp = 0.5 * band[0] + band[1] + band[2] + band[3] + 0.5 * band[4]
        if ns < LANE:
            imp = jnp.concatenate([imp, jnp.zeros((LANE - ns, QB), F32)], axis=0)
        score = jnp.where(forced, FORCE, jnp.where(blk * SEL_BLOCK <= t, imp, NEG))
        scores.append(jnp.where(blk < ns, score, -jnp.inf))
    sels = [jnp.zeros((LANE, QB), F32) for _ in range(ng)]
    for _ in range(min(SEL_TOP_N, ns)):
        for g in range(ng):
            mx = jnp.max(scores[g], axis=0, keepdims=True)
            first = jnp.min(jnp.where(scores[g] == mx, blk_f, float(LANE)), axis=0, keepdims=True)
            pick = blk_f == first
            sels[g] = jnp.where(pick, 1.0, sels[g])
            scores[g] = jnp.where(pick, -jnp.inf, scores[g])
    sel_b = [s.astype(BF16) for s in sels]

    kpos = lax.broadcasted_iota(jnp.int32, (KW, QB), 0)
    tq = q0 + lax.broadcasted_iota(jnp.int32, (KW, QB), 1)
    bounded_ok = bound_ref[0] <= SAFE_LOG2_BOUND
    shift = jnp.where(bounded_ok, bound_ref[0], 0.0)
    c_near = jnp.maximum(qi - 1, 0) // (KW // LANE)

    def scores_of(g, c0):
        chosen = _dot(ext_ref[pl.ds(c0, KW), :], sel_b[g])
        return (chosen - 1.0) * (-NEG) - shift, _dot(ks_ref[g, pl.ds(c0, KW), :], q_ts[g])

    def attend(bounded):
        for g in range(ng):
            _flash_init(*state[g][2:5])

        def qk_stage(c, buf):
            c0 = pl.multiple_of(c * KW, KW)
            ahead = []
            for g in range(ng):
                madd, s = scores_of(g, c0)
                s = s + _tile_lanes(madd, hpg)
                state[g][5 + buf][...] = s
                if bounded:
                    ahead.append(s[KW - 8:KW])
                else:
                    ahead.append(jnp.max(s, axis=0, keepdims=True))
                    state[g][7 + buf][...] = ahead[-1]
            return ahead

        def soft_stage(c, buf, ahead):
            c0 = pl.multiple_of(c * KW, KW)
            for g in range(ng):
                m_ref, l_ref, acc_ref = state[g][2:5]
                after = None if ahead is None else ahead[g]
                v_t = vst_ref[g, :, pl.ds(c0, KW)]
                if bounded:
                    _flash_accumulate(state[g][5 + buf][...], v_t, l_ref, acc_ref, after=after)
                else:
                    _flash_update(state[g][5 + buf][...], v_t, m_ref, l_ref, acc_ref,
                                  col_max=state[g][7 + buf][...], after=after)

        def near_step(c, x):
            c0 = pl.multiple_of(c * KW, KW)
            causal = jnp.where(c0 + kpos <= tq, 0.0, NEG)
            scores = []
            for g in range(ng):
                madd, s = scores_of(g, c0)
                scores.append(s + _tile_lanes(madd + causal, hpg)
                              + _near_bias(dt_ref, group_heads[g], qi, c * (KW // LANE), KW // LANE))
            for g in range(ng):
                m_ref, l_ref, acc_ref = state[g][2:5]
                if bounded:
                    _flash_accumulate(scores[g], vst_ref[g, :, pl.ds(c0, KW)], l_ref, acc_ref)
                else:
                    _flash_update(scores[g], vst_ref[g, :, pl.ds(c0, KW)], m_ref, l_ref, acc_ref)
            return x

        _pipelined_chunks(c_near, qk_stage, soft_stage)
        lax.fori_loop(c_near, qi // (KW // LANE) + 1, near_step, 0)
        for g in range(ng):
            m_ref, l_ref, acc_ref = state[g][2:5]
            acc_ref[...] = _sum_result(l_ref, acc_ref) if bounded else _flash_result(m_ref, l_ref, acc_ref)

    pl.when(bounded_ok)(lambda: attend(True))
    pl.when(jnp.logical_not(bounded_ok))(lambda: attend(False))

    for g in range(ng):
        os_t = state[g][4][...]
        gates = jax.nn.sigmoid(gt_ref[g])
        for h in range(hpg):
            sl = slice(h * QB, (h + 1) * QB)
            o_t = (gates[3 * h:3 * h + 1] * oc_t[g][:, sl] + gates[3 * h + 1:3 * h + 2] * os_t[:, sl]
                   + gates[3 * h + 2:3 * h + 3] * ow_t[g][:, sl])
            hh = group_heads[g][h]
            o_ref[:, hh * HEAD_DIM:(hh + 1) * HEAD_DIM] = o_t.T.astype(o_ref.dtype)


def nsa_attention(logit_bound, q_t, gates_t, kc, vc_t, k_sw, v_sw_t, dt, dc, bsz, seq):
    nq = seq // QB
    ncp = seq // CMP_STRIDE
    nc = ncp - 1
    ns = seq // SEL_BLOCK
    assert ns <= LANE and seq >= WINDOW + QB and seq % KW == 0
    assert CMP_BLOCK == 2 * CMP_STRIDE and SEL_BLOCK == 4 * CMP_STRIDE
    expand =((np.arange(seq)[:, None] // SEL_BLOCK) == np.arange(LANE)[None, :]).astype(np.float32)
    ng = NSA_GROUPS
    once = dict(pipeline_mode=pl.Buffered(1))
    ks_spec = pl.BlockSpec((ng, seq, HEAD_DIM), lambda b, i: (0, b, 0), **once)
    kw_spec = pl.BlockSpec((ng, seq, HEAD_DIM), lambda b, i: (1, b, 0), **once)
    vs_spec = pl.BlockSpec((ng, HEAD_DIM, seq), lambda b, i: (0, 0, b), **once)
    vw_spec = pl.BlockSpec((ng, HEAD_DIM, seq), lambda b, i: (1, 0, b), **once)
    lanes = NSA_HPG * QB
    group_state = [pltpu.VMEM((ncp + CWIN, lanes), F32), pltpu.VMEM((ncp + 8, QB), F32),
                   pltpu.VMEM((1, lanes), F32), pltpu.VMEM((1, lanes), F32), pltpu.VMEM((HEAD_DIM, lanes), F32),
                   pltpu.VMEM((KW, lanes), F32), pltpu.VMEM((KW, lanes), F32),
                   pltpu.VMEM((1, lanes), F32), pltpu.VMEM((1, lanes), F32)]
    assert len(group_state) == NSA_STATE
    return pl.pallas_call(
        functools.partial(_nsa_kernel, seq=seq, nc=nc),
        grid=(bsz, nq),
        in_specs=[pl.BlockSpec(memory_space=pltpu.SMEM),
                  pl.BlockSpec((NSA_HEADS, HEAD_DIM, QB), lambda b, i: (0, 0, b * nq + i)),
                  pl.BlockSpec((ng, 16, QB), lambda b, i: (0, 0, b * nq + i)),
                  pl.BlockSpec((ng, ncp, HEAD_DIM), lambda b, i: (0, b, 0)),
                  pl.BlockSpec((ng, HEAD_DIM, ncp), lambda b, i: (0, 0, b)),
                  ks_spec, vs_spec, kw_spec, vw_spec,
                  pl.BlockSpec((NSA_HEADS, 3, LANE, LANE), lambda b, i: (0, 0, 0, 0)),
                  pl.BlockSpec((NSA_HEADS, CWIN, LANE), lambda b, i: (0, 0, 0)),
                  pl.BlockSpec((seq, LANE), lambda b, i: (0, 0))],
        out_specs=pl.BlockSpec((QB, NSA_HEADS * HEAD_DIM), lambda b, i: (b * nq + i, 0)),
        out_shape=jax.ShapeDtypeStruct((bsz * seq, NSA_HEADS * HEAD_DIM), BF16),
        scratch_shapes=group_state * ng,
        compiler_params=_cparams(("arbitrary", "arbitrary")),
        name="nsa_attention",
    )(jnp.reshape(logit_bound, (1,)).astype(F32), q_t, gates_t, kc, vc_t, k_sw, v_sw_t, k_sw, v_sw_t, dt, dc,
      jnp.asarray(expand, BF16))


MLA_HPS = 2


def _mla_kernel(bound_ref, qt_ref, k_ref, vt_ref, o_ref, *scratch):
    qi = pl.program_id(2)
    chains = [scratch[3 * h:3 * h + 3] for h in range(MLA_HPS)]
    sbuf = [scratch[(3 + b) * MLA_HPS:(4 + b) * MLA_HPS] for b in range(2)]
    cbuf = [scratch[(5 + b) * MLA_HPS:(6 + b) * MLA_HPS] for b in range(2)]
    c_diag = pl.multiple_of(qi * KW, KW)
    kpos = lax.broadcasted_iota(jnp.int32, (KW, KW), 0)
    tq = lax.broadcasted_iota(jnp.int32, (KW, KW), 1)
    dv = vt_ref.shape[1]

    def attend(bounded):
        for ch in chains:
            _flash_init(*ch)

        def qk_stage(c, buf):
            c0 = pl.multiple_of(c * KW, KW)
            ahead = []
            for h in range(MLA_HPS):
                s = _dot(k_ref[h, pl.ds(c0, KW), :], qt_ref[h])
                sbuf[buf][h][...] = s
                if bounded:
                    ahead.append(s[KW - 8:KW])
                else:
                    ahead.append(jnp.max(s, axis=0, keepdims=True))
                    cbuf[buf][h][...] = ahead[-1]
            return ahead

        def soft_stage(c, buf, ahead):
            c0 = pl.multiple_of(c * KW, KW)
            for h, (m_ref, l_ref, acc_ref) in enumerate(chains):
                after = None if ahead is None else ahead[h]
                if bounded:
                    _flash_accumulate(sbuf[buf][h][...], vt_ref[h, :, pl.ds(c0, KW)], l_ref, acc_ref, after=after)
                else:
                    _flash_update(sbuf[buf][h][...], vt_ref[h, :, pl.ds(c0, KW)], m_ref, l_ref, acc_ref,
                                  col_max=cbuf[buf][h][...], after=after)

        _pipelined_chunks(qi, qk_stage, soft_stage)
        causal = jnp.where(kpos <= tq, 0.0, NEG)
        scores = [_dot(k_ref[h, pl.ds(c_diag, KW), :], qt_ref[h]) + causal for h in range(MLA_HPS)]
        for h, (m_ref, l_ref, acc_ref) in enumerate(chains):
            if bounded:
                _flash_accumulate(scores[h], vt_ref[h, :, pl.ds(c_diag, KW)], l_ref, acc_ref)
                o_t = _sum_result(l_ref, acc_ref)
            else:
                _flash_update(scores[h], vt_ref[h, :, pl.ds(c_diag, KW)], m_ref, l_ref, acc_ref)
                o_t = _flash_result(m_ref, l_ref, acc_ref)
            o_ref[:, h * dv:(h + 1) * dv] = o_t.T.astype(o_ref.dtype)

    bounded_ok = bound_ref[0] <= SAFE_LOG2_BOUND
    pl.when(bounded_ok)(lambda: attend(True))
    pl.when(jnp.logical_not(bounded_ok))(lambda: attend(False))


def mla_attention(logit_bound, q_t, k, v_t, bsz, seq):
    nh, dqk, _ = q_t.shape
    dv = v_t.shape[1]
    nq = seq // KW
    hps = MLA_HPS
    state = [pltpu.VMEM((1, KW), F32), pltpu.VMEM((1, KW), F32), pltpu.VMEM((dv, KW), F32)] * hps
    state += [pltpu.VMEM((KW, KW), F32)] * (2 * hps)
    state += [pltpu.VMEM((1, KW), F32)] * (2 * hps)
    return pl.pallas_call(
        _mla_kernel,
        grid=(bsz, nh // hps, nq),
        in_specs=[pl.BlockSpec(memory_space=pltpu.SMEM),
                  pl.BlockSpec((hps, dqk, KW), lambda b, h, i: (h, 0, b * nq + i)),
                  pl.BlockSpec((hps, seq, dqk), lambda b, h, i: (h, b, 0)),
                  pl.BlockSpec((hps, dv, seq), lambda b, h, i: (h, 0, b))],
        out_specs=pl.BlockSpec((KW, hps * dv), lambda b, h, i: (b * nq + i, h)),
        out_shape=jax.ShapeDtypeStruct((bsz * seq, nh * dv), BF16),
        scratch_shapes=state,
        compiler_params=_cparams(("arbitrary", "arbitrary", "arbitrary")),
        name="mla_attention",
    )(jnp.reshape(logit_bound, (1,)).astype(F32), q_t, k, v_t)


INT_MIN = -2 ** 31
NEG_KEY = int(np.array(NEG, np.float32).view(np.int32)) ^ 0x7FFFFFFF
KEY_BITS = 32
SURE_BITS = 20


def _sort_key(x):
    bits = pltpu.bitcast(x + 0.0, jnp.int32)
    return jnp.where(bits < 0, bits ^ 0x7FFFFFFF, bits)


def _dsa_kernel(bound_ref, iqt_ref, iwt_ref, ik_ref, qt_ref, k_ref, vt_ref, dt_ref, o_ref,
                key_ref, madd_ref, *state, seq, k_sel):
    qi = pl.program_id(1)
    q0 = qi * QB
    n_chunk = (q0 + QB + KW - 1) // KW
    n_rest = seq - n_chunk * KW
    kpos = lax.broadcasted_iota(jnp.int32, (KW, QB), 0)
    tq = q0 + lax.broadcasted_iota(jnp.int32, (KW, QB), 1)
    hpp = KW // QB

    def score_chunk(c, x):
        c0 = pl.multiple_of(c * KW, KW)
        ikc = ik_ref[pl.ds(c0, KW), :]
        acc = jnp.zeros((KW, QB), F32)
        for piece in range(IDX_HEADS // hpp):
            sl = slice(piece * KW, (piece + 1) * KW)
            s = jnp.maximum(_dot(ikc, iqt_ref[0, :, sl]), 0.0) * iwt_ref[0, :, sl]
            for j in range(hpp):
                acc = acc + s[:, j * QB:(j + 1) * QB]
        acc = jnp.where(c0 + kpos <= tq, acc, NEG)
        key_ref[pl.ds(c0, KW), :] = _sort_key(acc)
        return x

    lax.fori_loop(0, n_chunk, score_chunk, 0)

    def count(pred):
        def body(c, acc):
            c0 = pl.multiple_of(c * KW, KW)
            hit = jnp.where(pred(key_ref[pl.ds(c0, KW), :], c0), 1.0, 0.0)
            parts = [hit[8 * i:8 * (i + 1)] for i in range(KW // 8)]
            while len(parts) > 1:
                parts = [parts[i] + parts[i + 1] for i in range(0, len(parts), 2)]
            return acc + parts[0]
        acc = lax.fori_loop(0, n_chunk, body, jnp.zeros((8, QB), F32))
        return jnp.sum(acc, axis=0, keepdims=True)

    rest = n_rest.astype(F32)
    kf = float(k_sel)

    def bit_step(i, st):
        u, thr_s, settled = st
        bit = jnp.left_shift(jnp.int32(1), KEY_BITS - 1 - i)
        trial = (u | bit) ^ INT_MIN
        cnt = count(lambda keys, c0: keys >= trial) + jnp.where(NEG_KEY >= trial, rest, 0.0)
        new = (cnt == kf) & (settled < 0.5)
        return (jnp.where(cnt >= kf, u | bit, u), jnp.where(new, trial, thr_s), jnp.where(new, 1.0, settled))

    st = (jnp.zeros((1, QB), jnp.int32), jnp.zeros((1, QB), jnp.int32), jnp.zeros((1, QB), F32))
    st = lax.fori_loop(0, SURE_BITS, bit_step, st)
    _, (u, thr_s, settled) = lax.while_loop(
        lambda c: (c[0] < KEY_BITS) & (jnp.min(c[1][2]) < 0.5),
        lambda c: (c[0] + 1, bit_step(c[0], c[1])), (jnp.int32(SURE_BITS), st))
    is_settled = settled > 0.5
    thr = jnp.where(is_settled, thr_s, u ^ INT_MIN)

    def edge_counts():
        return (count(lambda keys, c0: keys > thr) + jnp.where(NEG_KEY > thr, rest, 0.0),
                count(lambda keys, c0: keys >= thr) + jnp.where(NEG_KEY >= thr, rest, 0.0))

    zero_cnt = jnp.zeros((1, QB), F32)
    cnt_gt, cnt_ge = lax.cond(jnp.min(settled) > 0.5, lambda: (zero_cnt, zero_cnt), edge_counts)
    need = kf - cnt_gt
    tie_q = (cnt_ge > kf) & (thr != NEG_KEY) & jnp.logical_not(is_settled)
    idx_bits = (seq - 1).bit_length()
    no_cut = 2 ** 30

    def tie_cut():
        def idx_step(i, x):
            bit = jnp.left_shift(jnp.int32(1), idx_bits - 1 - i)
            trial = x | bit
            f = count(lambda keys, c0: (keys == thr) & (c0 + kpos < trial))
            return jnp.where(f <= need - 1.0, trial, x)
        return lax.fori_loop(0, idx_bits, idx_step, jnp.zeros((1, QB), jnp.int32))

    any_tie = jnp.max(jnp.where(tie_q, 1.0, 0.0)) > 0.0
    x_cut = lax.cond(any_tie, tie_cut, lambda: jnp.full((1, QB), no_cut, jnp.int32))
    x_cut = jnp.where(tie_q, x_cut, no_cut)

    def mask_chunk(c, x):
        c0 = pl.multiple_of(c * KW, KW)
        keys = key_ref[pl.ds(c0, KW), :]
        pos = c0 + kpos
        chosen = (keys > thr) | ((keys == thr) & (pos <= x_cut))
        madd_ref[pl.ds(c0, KW), :] = jnp.where(chosen & (pos <= tq), -shift, NEG)
        return x

    bounded_ok = bound_ref[0] <= SAFE_LOG2_BOUND
    shift = jnp.where(bounded_ok, bound_ref[0], 0.0)
    lax.fori_loop(0, n_chunk, mask_chunk, 0)

    c_near = jnp.maximum(qi - 1, 0) // (KW // LANE)
    ng = DSA_KV_HEADS
    chains = [state[3 * g:3 * g + 3] for g in range(ng)]
    sbuf = [state[(3 + b) * ng:(4 + b) * ng] for b in range(2)]
    cbuf = [state[(5 + b) * ng:(6 + b) * ng] for b in range(2)]
    group_heads = [[g * DSA_HPG + h for h in range(DSA_HPG)] for g in range(ng)]

    def raw_scores(c0, g):
        q_t = jnp.concatenate([qt_ref[h] for h in group_heads[g]], axis=1)
        return _dot(k_ref[g, pl.ds(c0, KW), :], q_t)

    def attend(bounded):
        for ch in chains:
            _flash_init(*ch)

        def qk_stage(c, buf):
            c0 = pl.multiple_of(c * KW, KW)
            madd = _tile_lanes(madd_ref[pl.ds(c0, KW), :], DSA_HPG)
            ahead = []
            for g in range(ng):
                s = raw_scores(c0, g) + madd
                sbuf[buf][g][...] = s
                if bounded:
                    ahead.append(s[KW - 8:KW])
                else:
                    ahead.append(jnp.max(s, axis=0, keepdims=True))
                    cbuf[buf][g][...] = ahead[-1]
            return ahead

        def soft_stage(c, buf, ahead):
            c0 = pl.multiple_of(c * KW, KW)
            for g, (m_ref, l_ref, acc_ref) in enumerate(chains):
                after = None if ahead is None else ahead[g]
                if bounded:
                    _flash_accumulate(sbuf[buf][g][...], vt_ref[g, :, pl.ds(c0, KW)], l_ref, acc_ref, after=after)
                else:
                    _flash_update(sbuf[buf][g][...], vt_ref[g, :, pl.ds(c0, KW)], m_ref, l_ref, acc_ref,
                                  col_max=cbuf[buf][g][...], after=after)

        _pipelined_chunks(c_near, qk_stage, soft_stage)

        def near_step(c, x):
            c0 = pl.multiple_of(c * KW, KW)
            madd = _tile_lanes(madd_ref[pl.ds(c0, KW), :], DSA_HPG)
            scores = [raw_scores(c0, g) + madd
                      + _near_bias(dt_ref, group_heads[g], qi, c * (KW // LANE), KW // LANE)
                      for g in range(ng)]
            for g, (m_ref, l_ref, acc_ref) in enumerate(chains):
                if bounded:
                    _flash_accumulate(scores[g], vt_ref[g, :, pl.ds(c0, KW)], l_ref, acc_ref)
                else:
                    _flash_update(scores[g], vt_ref[g, :, pl.ds(c0, KW)], m_ref, l_ref, acc_ref)
            return x

        lax.fori_loop(c_near, n_chunk, near_step, 0)
        for g, (m_ref, l_ref, acc_ref) in enumerate(chains):
            o_t = _sum_result(l_ref, acc_ref) if bounded else _flash_result(m_ref, l_ref, acc_ref)
            for h in range(DSA_HPG):
                hh = group_heads[g][h]
                o_ref[:, hh * HEAD_DIM:(hh + 1) * HEAD_DIM] = o_t[:, h * QB:(h + 1) * QB].T.astype(o_ref.dtype)

    pl.when(bounded_ok)(lambda: attend(True))
    pl.when(jnp.logical_not(bounded_ok))(lambda: attend(False))


def _idx_prep_kernel(p_ref, c_ref, sa_ref, sb_ref, iqt_ref, ik_ref, iwt_ref, *, ntile):
    nslab_q = IDX_HEADS * IDX_DIM // LANE
    per = LANE // IDX_DIM
    half = IDX_ROPE // 2
    zrows = jnp.zeros((LANE - IDX_DIM, QB), F32)

    def rope_slab(x, c, sa, sb):
        return x * c + pltpu.roll(x, LANE - half, axis=1) * sa + pltpu.roll(x, half, axis=1) * sb

    for t in range(ntile):
        rows = slice(t * QB, (t + 1) * QB)
        c, sa, sb = c_ref[rows, :], sa_ref[rows, :], sb_ref[rows, :]
        cols = []
        for s in range(nslab_q):
            x_t = (rope_slab(p_ref[s, rows, :], c, sa, sb) * IDX_DIM ** -0.5).T
            for j in range(per):
                cols.append(jnp.concatenate([x_t[j * IDX_DIM:(j + 1) * IDX_DIM], zrows], axis=0))
        iqt_ref[t] = jnp.concatenate(cols, axis=1).astype(iqt_ref.dtype)
        tail = p_ref[nslab_q, rows, :]
        lane = lax.broadcasted_iota(jnp.int32, (QB, LANE), 1)
        ik_ref[rows, :] = jnp.where(lane < IDX_DIM, rope_slab(tail, c, sa, sb), 0.0).astype(ik_ref.dtype)
        w_t = (tail * IDX_HEADS ** -0.5).T
        iwt_ref[t] = jnp.concatenate([w_t[IDX_DIM + h:IDX_DIM + h + 1, :] for h in range(IDX_HEADS)], axis=1)


def indexer_operands(proj, seq, tm=512):
    _, m, _ = proj.shape
    ntile = tm // QB
    tps = seq // tm
    cos, sin = _rope_tables(seq, IDX_ROPE)
    zero = jnp.zeros_like(sin)
    rest = IDX_DIM - IDX_ROPE
    per = LANE // IDX_DIM
    c_tab = jnp.tile(jnp.concatenate([cos, cos, jnp.ones((seq, rest), F32)], axis=1), (1, per))
    sa_tab = jnp.tile(jnp.concatenate([-sin, zero, jnp.zeros((seq, rest), F32)], axis=1), (1, per))
    sb_tab = jnp.tile(jnp.concatenate([zero, sin, jnp.zeros((seq, rest), F32)], axis=1), (1, per))
    lanes = IDX_HEADS * QB
    tab_spec = pl.BlockSpec((tm, LANE), lambda i: (i % tps, 0))
    return pl.pallas_call(
        functools.partial(_idx_prep_kernel, ntile=ntile),
        grid=(m // tm,),
        in_specs=[pl.BlockSpec((proj.shape[0], tm, LANE), lambda i: (0, i, 0)), tab_spec, tab_spec, tab_spec],
        out_specs=[pl.BlockSpec((ntile, LANE, lanes), lambda i: (i, 0, 0)),
                   pl.BlockSpec((tm, LANE), lambda i: (i, 0)),
                   pl.BlockSpec((ntile, 1, lanes), lambda i: (i, 0, 0))],
        out_shape=[jax.ShapeDtypeStruct((m // QB, LANE, lanes), BF16),
                   jax.ShapeDtypeStruct((m, LANE), BF16),
                   jax.ShapeDtypeStruct((m // QB, 1, lanes), F32)],
        compiler_params=_cparams(("arbitrary",)),
        name="dsa_indexer_operands",
    )(proj, c_tab, sa_tab, sb_tab)


def dsa_attention(logit_bound, iq_t, iw_t, ik, q_t, k, v_t, dt, bsz, seq):
    nq = seq // QB
    k_sel = min(DSA_TOPK_MAX, seq // 4)
    assert seq % KW == 0
    lanes = DSA_HPG * QB
    return pl.pallas_call(
        functools.partial(_dsa_kernel, seq=seq, k_sel=k_sel),
        grid=(bsz, nq),
        in_specs=[pl.BlockSpec(memory_space=pltpu.SMEM),
                  pl.BlockSpec((1, LANE, IDX_HEADS * QB), lambda b, i: (b * nq + i, 0, 0)),
                  pl.BlockSpec((1, 1, IDX_HEADS * QB), lambda b, i: (b * nq + i, 0, 0)),
                  pl.BlockSpec((seq, LANE), lambda b, i: (b, 0)),
                  pl.BlockSpec((DSA_HEADS, HEAD_DIM, QB), lambda b, i: (0, 0, b * nq + i)),
                  pl.BlockSpec((DSA_KV_HEADS, seq, HEAD_DIM), lambda b, i: (0, b, 0),
                               pipeline_mode=pl.Buffered(1)),
                  pl.BlockSpec((DSA_KV_HEADS, HEAD_DIM, seq), lambda b, i: (0, 0, b),
                               pipeline_mode=pl.Buffered(1)),
                  pl.BlockSpec((DSA_HEADS, 3, LANE, LANE), lambda b, i: (0, 0, 0, 0),
                               pipeline_mode=pl.Buffered(1))],
        out_specs=pl.BlockSpec((QB, DSA_HEADS * HEAD_DIM), lambda b, i: (b * nq + i, 0)),
        out_shape=jax.ShapeDtypeStruct((bsz * seq, DSA_HEADS * HEAD_DIM), BF16),
        scratch_shapes=[pltpu.VMEM((seq, QB), jnp.int32), pltpu.VMEM((seq, QB), F32)]
        + [pltpu.VMEM((1, lanes), F32), pltpu.VMEM((1, lanes), F32),
           pltpu.VMEM((HEAD_DIM, lanes), F32)] * DSA_KV_HEADS
        + [pltpu.VMEM((KW, lanes), F32)] * (2 * DSA_KV_HEADS)
        + [pltpu.VMEM((1, lanes), F32)] * (2 * DSA_KV_HEADS),
        compiler_params=_cparams(("arbitrary", "arbitrary")),
        name="dsa_attention",
    )(jnp.reshape(logit_bound, (1,)).astype(F32), iq_t, iw_t, ik, q_t, k, v_t, dt)


def _rope_tables(seq, dim):
    half = dim // 2
    inv = ROPE_THETA ** (-jnp.arange(half, dtype=F32) / half)
    ang = jnp.arange(seq, dtype=F32)[:, None] * inv[None, :]
    return jnp.cos(ang), jnp.sin(ang)


def _logit_bound(gq, gk, dim, scale):
    return dim * scale * jnp.max(jnp.abs(gq)) * jnp.max(jnp.abs(gk)) * (1.0 + 2.0 ** -7)


def _pad_cols(w, n):
    return jnp.pad(w, ((0, 0), (0, n - w.shape[1])))


def _t(x):
    return jnp.swapaxes(x, -1, -2)


def _even_mixer(h, x2, gate, dt, dc, bias_bound, bsz, seq, w_in, w_out, nsa_qk_g, cmp_pe, cmp_w1, cmp_b1,
                cmp_w2, cmp_b2, q_norm_g, kv_norm_g, w_uq, w_ukv, nope_g, rope_g):
    m = bsz * seq
    nq_cols = NSA_HEADS * HEAD_DIM
    nkv_cols = 6 * NSA_GROUPS * HEAD_DIM
    ngate = 3 * NSA_HEADS
    o_gate = nq_cols + nkv_cols
    o_cq = o_gate + ngate
    o_ckv = o_cq + MLA_Q_RANK
    o_kpe = o_ckv + MLA_KV_RANK
    gw = NSA_GROUPS * HEAD_DIM
    kvw = [w_in[:, nq_cols + i * gw:nq_cols + (i + 1) * gw] for i in range(6)]
    scale = HEAD_DIM ** -0.5 * LOG2E
    q_t = proj_heads(h, w_in[:, :nq_cols].astype(BF16), nsa_qk_g[0] * scale, transpose=True)
    k_sw = proj_heads(h, jnp.concatenate([kvw[2], kvw[4]], axis=1).astype(BF16), nsa_qk_g[1],
                      transpose=False)
    v_sw_t = proj_heads(h, jnp.concatenate([kvw[3], kvw[5]], axis=1).astype(BF16), transpose=True)
    tail = jnp.concatenate([w_in[:, o_kpe:], w_in[:, o_gate:o_cq]], axis=1)
    w_r = jnp.concatenate([kvw[0], kvw[1], w_in[:, o_cq:o_kpe], _pad_cols(tail, LANE)], axis=1).astype(BF16)
    proj = proj_slabs(h, w_r, tn=w_r.shape[1])
    s_cq = 2 * NSA_GROUPS
    s_ckv = s_cq + MLA_Q_RANK // LANE
    s_tail = s_ckv + MLA_KV_RANK // LANE
    kvc = compress_kv(proj, 0, bsz, seq, cmp_pe, cmp_w1, cmp_b1, cmp_w2, cmp_b2, nsa_qk_g[1])
    tail_v = proj[s_tail]
    gates = tail_v[:, MLA_ROPE:MLA_ROPE + ngate].reshape(m, NSA_GROUPS, 3 * NSA_HPG)
    gates_t = jnp.pad(jnp.transpose(gates, (1, 2, 0)), ((0, 0), (0, 16 - 3 * NSA_HPG), (0, 0)))
    nsa_bound = _logit_bound(nsa_qk_g[0], nsa_qk_g[1], HEAD_DIM, scale) + bias_bound
    o_nsa = nsa_attention(nsa_bound, q_t, gates_t, kvc[0], _t(kvc[1]), k_sw, v_sw_t,
                          dt[:NSA_HEADS], dc[:NSA_HEADS], bsz, seq)

    dq = MLA_NOPE + MLA_ROPE
    wq = w_uq.reshape(MLA_Q_RANK, MLA_HEADS, dq)
    wq_r = jnp.concatenate([wq[:, :, :MLA_NOPE].reshape(MLA_Q_RANK, -1),
                            wq[:, :, MLA_NOPE:].reshape(MLA_Q_RANK, -1)], axis=1).astype(BF16)
    cos, sin = _rope_tables(seq, MLA_ROPE)
    mscale = dq ** -0.5 * LOG2E
    side = [jnp.sqrt(MLA_NOPE * jnp.max(jnp.abs(nope_g[i])) ** 2 + MLA_ROPE * jnp.max(jnp.abs(rope_g[i])) ** 2)
            for i in range(2)]
    mla_bound = mscale * side[0] * side[1] * (1.0 + 2.0 ** -7)
    mla_shift = jnp.where(mla_bound <= SAFE_LOG2_BOUND, mla_bound, 0.0)
    q_mla_t, k_mla, v_mla_t = mla_project(proj, s_cq, s_ckv, s_tail, seq, q_norm_g, kv_norm_g, wq_r,
                                          w_ukv.astype(BF16), nope_g, rope_g, cos, sin, mscale, mla_shift)
    o_mla = mla_attention(mla_bound, q_mla_t, k_mla, v_mla_t, bsz, seq)
    w_o = w_out.astype(BF16)
    return resproj([(o_nsa, w_o[:nq_cols]), (o_mla, w_o[nq_cols:])], x2, gate, seq)


def _odd_mixer(h, x2, gate, dt, bias_bound, bsz, seq, w_in, w_out, qk_g):
    nq = DSA_HEADS * HEAD_DIM
    nkv = DSA_KV_HEADS * HEAD_DIM
    niq = IDX_HEADS * IDX_DIM
    o_k, o_v, o_iq = nq, nq + nkv, nq + 2 * nkv
    q_t = proj_heads(h, w_in[:, :o_k].astype(BF16), qk_g[0] * (HEAD_DIM ** -0.5 * LOG2E), transpose=True)
    k = proj_heads(h, w_in[:, o_k:o_v].astype(BF16), qk_g[1], transpose=False)
    v_t = proj_heads(h, w_in[:, o_v:o_iq].astype(BF16), transpose=True)
    w_idx = w_in[:, o_iq:]
    proj = proj_slabs(h, _pad_cols(w_idx, niq + LANE).astype(BF16), tn=niq + LANE)
    iq_t, ik, iw_t = indexer_operands(proj, seq)
    bound = _logit_bound(qk_g[0], qk_g[1], HEAD_DIM, HEAD_DIM ** -0.5 * LOG2E) + bias_bound
    o = dsa_attention(bound, iq_t, iw_t, ik, q_t, k, v_t, dt, bsz, seq)
    return resproj([(o, w_out.astype(BF16))], x2, gate, seq)


def _conv_ffn(h, x2, gate, seq, w_up, conv_w, conv_b, w_down):
    a = ffn_up(h, w_up.astype(BF16), conv_w, conv_b, seq)
    return resproj([(a, w_down.astype(BF16))], x2, gate, seq)


def kernel(x, c, rel_bias, ada_w, ada_b, norm_g, ev_w_in, ev_w_out, nsa_qk_g, cmp_pe, cmp_w1, cmp_b1, cmp_w2, cmp_b2, mla_q_norm_g, mla_kv_norm_g, mla_w_uq, mla_w_ukv, mla_nope_g, mla_rope_g, od_w_in, od_w_out, dsa_qk_g, ffn_w_up, ffn_conv_w, ffn_conv_b, ffn_w_down):
    bsz, seq, d = x.shape
    depth = ada_w.shape[0]
    x2 = x.reshape(bsz * seq, d)
    mods = ada_all(c, ada_w, ada_b)
    dt, dc = bias_tiles(rel_bias)
    bias_bound = 2.0 * LOG2E * jnp.max(jnp.abs(rel_bias))
    for i in range(depth):
        j = i // 2
        shift, scale, gate = jnp.split(mods[i, 0], 3, axis=-1)
        h = modnorm(x2, norm_g[i, 0], scale, shift, seq)
        if i % 2 == 0:
            x2 = _even_mixer(h, x2, gate, dt, dc, bias_bound, bsz, seq, ev_w_in[j], ev_w_out[j], nsa_qk_g[j],
                             cmp_pe[j], cmp_w1[j], cmp_b1[j], cmp_w2[j], cmp_b2[j], mla_q_norm_g[j],
                             mla_kv_norm_g[j], mla_w_uq[j], mla_w_ukv[j], mla_nope_g[j], mla_rope_g[j])
        else:
            x2 = _odd_mixer(h, x2, gate, dt, bias_bound, bsz, seq, od_w_in[j], od_w_out[j], dsa_qk_g[j])
        shift, scale, gate = jnp.split(mods[i, 1], 3, axis=-1)
        h = modnorm(x2, norm_g[i, 1], scale, shift, seq)
        x2 = _conv_ffn(h, x2, gate, seq, ffn_w_up[i], ffn_conv_w[i], ffn_conv_b[i], ffn_w_down[i])
    return x2.reshape(bsz, seq, d)
```

```python
import functools
import math

import numpy as np
import jax
import jax.numpy as jnp
from jax import lax
from jax.experimental import pallas as pl
from jax.experimental.pallas import tpu as pltpu

HEAD_DIM = 128
NSA_HEADS = 8
NSA_GROUPS = 2
NSA_HPG = NSA_HEADS // NSA_GROUPS
CMP_BLOCK = 32
CMP_STRIDE = 16
CMP_HIDDEN = 256
SEL_BLOCK = 64
SEL_TOP_N = 16
WINDOW = 512
MLA_HEADS = 8
MLA_Q_RANK = 512
MLA_KV_RANK = 256
MLA_NOPE = 128
MLA_ROPE = 64
MLA_V = 128
DSA_HEADS = 16
DSA_KV_HEADS = 4
DSA_HPG = DSA_HEADS // DSA_KV_HEADS
IDX_HEADS = 16
IDX_DIM = 64
IDX_ROPE = 32
DSA_TOPK_MAX = 256
REL_BUCKETS = 32
REL_MAX_DIST = 128
CONV_WIDTH = 3
ROPE_THETA = 10000.0
EPS = 1e-6
NEG = -1e30
FORCE = 1e9

LANE = 128
SUBLANE = 8
QB = 128
VMEM_LIMIT = 56 * 1024 * 1024

F32 = jnp.float32
BF16 = jnp.bfloat16


def _t5_thresholds():
    d = np.arange(0, 4 * REL_MAX_DIST)
    half = REL_BUCKETS // 2
    val = np.log(np.maximum(d, 1) / half) / math.log(REL_MAX_DIST / half) * (REL_BUCKETS - half)
    large = np.minimum(half + np.floor(np.maximum(val, 0.0)).astype(np.int64), REL_BUCKETS - 1)
    bucket = np.where(d < half, d, large)
    return [int(np.argmax(bucket >= b)) for b in range(1, REL_BUCKETS)]


T5_THR = _t5_thresholds()
T5_FAR = T5_THR[-1]
assert T5_FAR <= LANE


def _cparams(sem):
    return pltpu.CompilerParams(dimension_semantics=sem, vmem_limit_bytes=VMEM_LIMIT)


def _dot(a, b):
    return jnp.dot(a, b, preferred_element_type=F32)


def _ada_kernel(c_ref, w_ref, b_ref, o_ref):
    c = c_ref[...]
    a = c * jax.nn.sigmoid(c)
    o_ref[0] = jnp.dot(a, w_ref[0], preferred_element_type=F32,
                       precision=lax.Precision.HIGHEST) + b_ref[0]


def ada_all(c, ada_w, ada_b):
    depth, two, d, n3 = ada_w.shape
    bsz = c.shape[0]
    rows = SUBLANE
    cp = jnp.zeros((rows, d), F32).at[:bsz].set(c)
    w = ada_w.reshape(depth * two, d, n3)
    b = ada_b.reshape(depth * two, 1, n3)
    tn = 512
    out = pl.pallas_call(
        _ada_kernel,
        grid=(depth * two, n3 // tn),
        in_specs=[pl.BlockSpec((rows, d), lambda l, j: (0, 0)),
                  pl.BlockSpec((1, d, tn), lambda l, j: (l, 0, j)),
                  pl.BlockSpec((1, 1, tn), lambda l, j: (l, 0, j))],
        out_specs=pl.BlockSpec((1, rows, tn), lambda l, j: (l, 0, j)),
        out_shape=jax.ShapeDtypeStruct((depth * two, rows, n3), F32),
        compiler_params=_cparams(("arbitrary", "arbitrary")),
        name="ada_mod",
    )(cp, w, b)
    return out[:, :bsz].reshape(depth, two, bsz, n3)


def _modnorm_kernel(x_ref, g_ref, sc_ref, sh_ref, o_ref):
    x = x_ref[...]
    y = x * lax.rsqrt(jnp.mean(x * x, axis=-1, keepdims=True) + EPS)
    h = (y * g_ref[...]) * (1.0 + sc_ref[0]) + sh_ref[0]
    o_ref[...] = h.astype(o_ref.dtype)


def modnorm(x2, g, scale, shift, seq):
    m, d = x2.shape
    tm = 512
    tpb = seq // tm
    return pl.pallas_call(
        _modnorm_kernel,
        grid=(m // tm,),
        in_specs=[pl.BlockSpec((tm, d), lambda i: (i, 0)),
                  pl.BlockSpec((1, d), lambda i: (0, 0)),
                  pl.BlockSpec((1, 1, d), lambda i: (i // tpb, 0, 0)),
                  pl.BlockSpec((1, 1, d), lambda i: (i // tpb, 0, 0))],
        out_specs=pl.BlockSpec((tm, d), lambda i: (i, 0)),
        out_shape=jax.ShapeDtypeStruct((m, d), BF16),
        compiler_params=_cparams(("arbitrary",)),
        name="modnorm",
    )(x2, g.reshape(1, d), scale.reshape(-1, 1, d), shift.reshape(-1, 1, d))


def _proj_kernel(x_ref, w_ref, o_ref, *, nslab):
    acc = _dot(x_ref[...], w_ref[...])
    for s in range(nslab):
        o_ref[s] = acc[:, s * LANE:(s + 1) * LANE]


def proj_slabs(x, w, tm=1024, tn=384):
    m, k = x.shape
    n = w.shape[1]
    assert n % tn == 0 and m % tm == 0
    nslab = tn // LANE
    return pl.pallas_call(
        functools.partial(_proj_kernel, nslab=nslab),
        grid=(m // tm, n // tn),
        in_specs=[pl.BlockSpec((tm, k), lambda i, j: (i, 0)),
                  pl.BlockSpec((k, tn), lambda i, j: (0, j))],
        out_specs=pl.BlockSpec((nslab, tm, LANE), lambda i, j: (j, i, 0)),
        out_shape=jax.ShapeDtypeStruct((n // LANE, m, LANE), F32),
        compiler_params=_cparams(("arbitrary", "arbitrary")),
        name="proj_slabs",
    )(x, w)


def _proj_heads_kernel(x_ref, w_ref, g_ref, o_ref, *, nslab, norm, transpose):
    acc = _dot(x_ref[...], w_ref[...])
    for s in range(nslab):
        y = acc[:, s * LANE:(s + 1) * LANE]
        if norm:
            y = y * lax.rsqrt(jnp.mean(y * y, axis=-1, keepdims=True) + EPS) * g_ref[...]
        o_ref[s] = (y.T if transpose else y).astype(o_ref.dtype)


def proj_heads(x, w, g=None, *, transpose, tm=1024, tn=1024):
    m, k = x.shape
    n = w.shape[1]
    tn = min(tn, n)
    assert n % tn == 0 and m % tm == 0
    nslab = tn // LANE
    norm = g is not None
    if transpose:
        out_spec = pl.BlockSpec((nslab, LANE, tm), lambda i, j: (j, 0, i))
        out_shape = jax.ShapeDtypeStruct((n // LANE, LANE, m), BF16)
    else:
        out_spec = pl.BlockSpec((nslab, tm, LANE), lambda i, j: (j, i, 0))
        out_shape = jax.ShapeDtypeStruct((n // LANE, m, LANE), BF16)
    g2 = (g if norm else jnp.ones((LANE,), F32)).reshape(1, LANE)
    return pl.pallas_call(
        functools.partial(_proj_heads_kernel, nslab=nslab, norm=norm, transpose=transpose),
        grid=(m // tm, n // tn),
        in_specs=[pl.BlockSpec((tm, k), lambda i, j: (i, 0)),
                  pl.BlockSpec((k, tn), lambda i, j: (0, j)),
                  pl.BlockSpec((1, LANE), lambda i, j: (0, 0))],
        out_specs=out_spec,
        out_shape=out_shape,
        compiler_params=_cparams(("arbitrary", "arbitrary")),
        name="proj_heads",
    )(x, w, g2)


def _rms_rows(x, g):
    return x * lax.rsqrt(jnp.mean(x * x, axis=-1, keepdims=True) + EPS) * g


def _rope_rows(x, cos, sin):
    half = x.shape[-1] // 2
    x1, x2 = x[:, :half], x[:, half:]
    return jnp.concatenate([x1 * cos - x2 * sin, x1 * sin + x2 * cos], axis=1)


def _latent(x_ref, g_ref):
    x = jnp.concatenate([x_ref[s] for s in range(x_ref.shape[0])], axis=1)
    return _rms_rows(x, g_ref[...]).astype(BF16)


def _mla_q_kernel(shift_ref, x_ref, g_ref, w_ref, gn_ref, gr_ref, cos_ref, sin_ref, o_ref, *, scale):
    acc = _dot(_latent(x_ref, g_ref), w_ref[...])
    tm = acc.shape[0]
    cos, sin = cos_ref[...], sin_ref[...]
    first = lax.broadcasted_iota(jnp.int32, (tm, LANE - MLA_ROPE), 1) == 0
    pad = jnp.where(first, -shift_ref[0], 0.0)
    for h in range(MLA_HEADS):
        nope = _rms_rows(acc[:, h * MLA_NOPE:(h + 1) * MLA_NOPE], gn_ref[...]) * scale
        r0 = MLA_HEADS * MLA_NOPE + h * MLA_ROPE
        pe = _rope_rows(_rms_rows(acc[:, r0:r0 + MLA_ROPE], gr_ref[...]), cos, sin) * scale
        o_ref[h, 0:MLA_NOPE, :] = nope.T.astype(o_ref.dtype)
        o_ref[h, MLA_NOPE:MLA_NOPE + LANE, :] = jnp.concatenate([pe, pad], axis=1).T.astype(o_ref.dtype)


def _mla_kv_kernel(x_ref, g_ref, w_ref, tail_ref, gn_ref, gr_ref, cos_ref, sin_ref, ok_ref, ov_ref):
    acc = _dot(_latent(x_ref, g_ref), w_ref[...])
    tm = acc.shape[0]
    k_pe = _rope_rows(_rms_rows(tail_ref[0][:, :MLA_ROPE], gr_ref[...]), cos_ref[...], sin_ref[...])
    first = lax.broadcasted_iota(jnp.int32, (tm, LANE - MLA_ROPE), 1) == 0
    k_pe = jnp.concatenate([k_pe, jnp.where(first, 1.0, 0.0)], axis=1).astype(ok_ref.dtype)
    for h in range(MLA_HEADS):
        c0 = h * (MLA_NOPE + MLA_V)
        ok_ref[h, :, 0:MLA_NOPE] = _rms_rows(acc[:, c0:c0 + MLA_NOPE], gn_ref[...]).astype(ok_ref.dtype)
        ok_ref[h, :, MLA_NOPE:MLA_NOPE + LANE] = k_pe
        ov_ref[h] = acc[:, c0 + MLA_NOPE:c0 + MLA_NOPE + MLA_V].T.astype(ov_ref.dtype)


def mla_project(proj, s_cq, s_ckv, s_tail, seq, q_norm_g, kv_norm_g, wq_r, w_ukv, nope_g, rope_g, cos, sin,
                scale, shift, tm=512):
    _, m, _ = proj.shape
    kq, kkv = s_ckv - s_cq, s_tail - s_ckv
    tps = seq // tm
    dqk = MLA_NOPE + LANE
    half = MLA_ROPE // 2
    rope_specs = [pl.BlockSpec((tm, half), lambda i: (i % tps, 0))] * 2
    gain_specs = [pl.BlockSpec((1, MLA_NOPE), lambda i: (0, 0)), pl.BlockSpec((1, MLA_ROPE), lambda i: (0, 0))]
    q_t = pl.pallas_call(
        functools.partial(_mla_q_kernel, scale=scale),
        grid=(m // tm,),
        in_specs=[pl.BlockSpec(memory_space=pltpu.SMEM),
                  pl.BlockSpec((kq, tm, LANE), lambda i: (s_cq // kq, i, 0)),
                  pl.BlockSpec((1, kq * LANE), lambda i: (0, 0)),
                  pl.BlockSpec(wq_r.shape, lambda i: (0, 0))] + gain_specs + rope_specs,
        out_specs=pl.BlockSpec((MLA_HEADS, dqk, tm), lambda i: (0, 0, i)),
        out_shape=jax.ShapeDtypeStruct((MLA_HEADS, dqk, m), BF16),
        compiler_params=_cparams(("arbitrary",)),
        name="mla_q_project",
    )(jnp.reshape(shift, (1,)).astype(F32), proj, q_norm_g.reshape(1, -1), wq_r, nope_g[0].reshape(1, -1),
      rope_g[0].reshape(1, -1), cos, sin)
    k, v_t = pl.pallas_call(
        _mla_kv_kernel,
        grid=(m // tm,),
        in_specs=[pl.BlockSpec((kkv, tm, LANE), lambda i: (s_ckv // kkv, i, 0)),
                  pl.BlockSpec((1, kkv * LANE), lambda i: (0, 0)),
                  pl.BlockSpec(w_ukv.shape, lambda i: (0, 0)),
                  pl.BlockSpec((1, tm, LANE), lambda i: (s_tail, i, 0))] + gain_specs + rope_specs,
        out_specs=[pl.BlockSpec((MLA_HEADS, tm, dqk), lambda i: (0, i, 0)),
                   pl.BlockSpec((MLA_HEADS, MLA_V, tm), lambda i: (0, 0, i))],
        out_shape=[jax.ShapeDtypeStruct((MLA_HEADS, m, dqk), BF16),
                   jax.ShapeDtypeStruct((MLA_HEADS, MLA_V, m), BF16)],
        compiler_params=_cparams(("arbitrary",)),
        name="mla_kv_project",
    )(proj, kv_norm_g.reshape(1, -1), w_ukv, proj, nope_g[1].reshape(1, -1), rope_g[1].reshape(1, -1), cos, sin)
    return q_t, k, v_t


def _resproj_kernel(*refs, npair):
    xres_ref, gate_ref = refs[2 * npair], refs[2 * npair + 1]
    o_ref = refs[2 * npair + 2]
    acc = _dot(refs[0][...], refs[1][...])
    for p in range(1, npair):
        acc = acc + _dot(refs[2 * p][...], refs[2 * p + 1][...])
    o_ref[...] = xres_ref[...] + gate_ref[0] * acc


def resproj(pairs, xres, gate, seq, tm=1024, tn=512):
    m, n = xres.shape
    tpb = seq // tm
    in_specs, args = [], []
    for x, w in pairs:
        k = x.shape[1]
        in_specs += [pl.BlockSpec((tm, k), lambda i, j: (i, 0)),
                     pl.BlockSpec((k, tn), lambda i, j: (0, j))]
        args += [x, w]
    in_specs += [pl.BlockSpec((tm, tn), lambda i, j: (i, j)),
                 pl.BlockSpec((1, 1, tn), lambda i, j: (i // tpb, 0, j))]
    args += [xres, gate.reshape(-1, 1, n)]
    return pl.pallas_call(
        functools.partial(_resproj_kernel, npair=len(pairs)),
        grid=(m // tm, n // tn),
        in_specs=in_specs,
        out_specs=pl.BlockSpec((tm, tn), lambda i, j: (i, j)),
        out_shape=jax.ShapeDtypeStruct((m, n), F32),
        compiler_params=_cparams(("arbitrary", "arbitrary")),
        name="resproj",
    )(*args)


HALO = 8


def _ffn_up_kernel(h_ref, wg32_ref, wv32_ref, cwg_ref, cwv_ref, cbg_ref, cbv_ref, o_ref,
                   ug_ref, uv_ref, wg_ref, wv_ref, *, tm, tiles_per_seq):
    i = pl.program_id(1)
    first = (i % tiles_per_seq) == 0

    @pl.when(i == 0)
    def _():
        wg_ref[...] = wg32_ref[...].astype(wg_ref.dtype)
        wv_ref[...] = wv32_ref[...].astype(wv_ref.dtype)

    @pl.when(first)
    def _():
        ug_ref[0:HALO, :] = jnp.zeros((HALO, ug_ref.shape[1]), F32)
        uv_ref[0:HALO, :] = jnp.zeros((HALO, uv_ref.shape[1]), F32)

    @pl.when(jnp.logical_not(first))
    def _():
        ug_ref[0:HALO, :] = ug_ref[tm:tm + HALO, :]
        uv_ref[0:HALO, :] = uv_ref[tm:tm + HALO, :]

    h = h_ref[...]
    ug_ref[HALO:HALO + tm, :] = _dot(h, wg_ref[...])
    uv_ref[HALO:HALO + tm, :] = _dot(h, wv_ref[...])

    def conv(u_ref, cw_ref, cb_ref):
        out = cb_ref[...]
        for j in range(CONV_WIDTH):
            off = HALO - (CONV_WIDTH - 1) + j
            out = out + cw_ref[j:j + 1, :] * u_ref[off:off + tm, :]
        return out

    g = conv(ug_ref, cwg_ref, cbg_ref)
    v = conv(uv_ref, cwv_ref, cbv_ref)
    o_ref[...] = (g * jax.nn.sigmoid(g) * v).astype(o_ref.dtype)


def ffn_up(h, w_up, conv_w, conv_b, seq, tm=1024, tn=512):
    m, d = h.shape
    f = w_up.shape[1] // 2
    nj = f // tn
    tps = seq // tm
    cb = conv_b.reshape(1, 2 * f)
    return pl.pallas_call(
        functools.partial(_ffn_up_kernel, tm=tm, tiles_per_seq=tps),
        grid=(nj, m // tm),
        in_specs=[pl.BlockSpec((tm, d), lambda j, i: (i, 0)),
                  pl.BlockSpec((d, tn), lambda j, i: (0, j)),
                  pl.BlockSpec((d, tn), lambda j, i: (0, nj + j)),
                  pl.BlockSpec((CONV_WIDTH, tn), lambda j, i: (0, j)),
                  pl.BlockSpec((CONV_WIDTH, tn), lambda j, i: (0, nj + j)),
                  pl.BlockSpec((1, tn), lambda j, i: (0, j)),
                  pl.BlockSpec((1, tn), lambda j, i: (0, nj + j))],
        out_specs=pl.BlockSpec((tm, tn), lambda j, i: (i, j)),
        out_shape=jax.ShapeDtypeStruct((m, f), BF16),
        scratch_shapes=[pltpu.VMEM((tm + HALO, tn), F32), pltpu.VMEM((tm + HALO, tn), F32),
                        pltpu.VMEM((d, tn), BF16), pltpu.VMEM((d, tn), BF16)],
        compiler_params=_cparams(("arbitrary", "arbitrary")),
        name="ffn_up_conv",
    )(h, w_up, w_up, conv_w, conv_w, cb, cb)


LOG2E = 1.4426950408889634
CWIN = 16


def _t5_shifted(dist, tbl_ref, h):
    val = jnp.full(dist.shape, tbl_ref[0, h], F32)
    for b in range(1, REL_BUCKETS):
        val = jnp.where(dist >= T5_THR[b - 1], tbl_ref[b, h], val)
    return (val - tbl_ref[REL_BUCKETS - 1, h]) * LOG2E


def _bias_tiles_kernel(tbl_ref, dt_ref, dc_ref):
    h = pl.program_id(0)
    key = lax.broadcasted_iota(jnp.int32, (LANE, LANE), 0)
    q = lax.broadcasted_iota(jnp.int32, (LANE, LANE), 1)
    for rel in range(2):
        dt_ref[0, rel] = _t5_shifted(rel * LANE + q - key, tbl_ref, h)
    dt_ref[0, 2] = jnp.zeros((LANE, LANE), F32)
    u = lax.broadcasted_iota(jnp.int32, (CWIN, LANE), 0)
    qc = lax.broadcasted_iota(jnp.int32, (CWIN, LANE), 1)
    dc_ref[0] = _t5_shifted(qc - CMP_STRIDE * (u - CWIN // 2) - (CMP_BLOCK - 1), tbl_ref, h)


def bias_tiles(rel_bias):
    nh = rel_bias.shape[1]
    return pl.pallas_call(
        _bias_tiles_kernel,
        grid=(nh,),
        in_specs=[pl.BlockSpec(memory_space=pltpu.SMEM)],
        out_specs=[pl.BlockSpec((1, 3, LANE, LANE), lambda h: (h, 0, 0, 0)),
                   pl.BlockSpec((1, CWIN, LANE), lambda h: (h, 0, 0))],
        out_shape=[jax.ShapeDtypeStruct((nh, 3, LANE, LANE), F32),
                   jax.ShapeDtypeStruct((nh, CWIN, LANE), F32)],
        compiler_params=_cparams(("arbitrary",)),
        name="t5_bias_tiles",
    )(rel_bias)


def _compress_kernel(x_ref, pe_ref, w1_ref, b1_ref, w2_ref, b2_ref, g_ref, o_ref, *, half):
    kv = pl.program_id(0)
    x = x_ref[0]
    a = _dot((x + pe_ref[0, :, :half]).astype(BF16), w1_ref[0, :half, :])
    b = _dot((x + pe_ref[0, :, half:]).astype(BF16), w1_ref[0, half:, :])
    b_next = jnp.concatenate([b[1:], jnp.zeros((1, b.shape[1]), F32)], axis=0)
    hid = jax.nn.gelu(a + b_next + b1_ref[0])
    out = _dot(hid.astype(BF16), w2_ref[0]) + b2_ref[0]
    normed = out * lax.rsqrt(jnp.mean(out * out, axis=-1, keepdims=True) + EPS) * g_ref[...]
    out = jnp.where(kv == 0, normed, out)
    o_ref[0, 0] = out.astype(o_ref.dtype)


def compress_kv(proj, slab0, bsz, seq, cmp_pe, cmp_w1, cmp_b1, cmp_w2, cmp_b2, g_k):
    nslab, m, _ = proj.shape
    nchunk = seq // CMP_STRIDE
    half = CMP_STRIDE * HEAD_DIM
    xv = proj.reshape(nslab, m // CMP_STRIDE, half)
    pe = cmp_pe.reshape(2, 1, CMP_BLOCK * HEAD_DIM)
    return pl.pallas_call(
        functools.partial(_compress_kernel, half=half),
        grid=(2, bsz, NSA_GROUPS),
        in_specs=[pl.BlockSpec((1, nchunk, half), lambda kv, b, g: (slab0 + 2 * kv + g, b, 0)),
                  pl.BlockSpec((1, 1, 2 * half), lambda kv, b, g: (kv, 0, 0)),
                  pl.BlockSpec((1, 2 * half, CMP_HIDDEN), lambda kv, b, g: (kv, 0, 0)),
                  pl.BlockSpec((1, 1, CMP_HIDDEN), lambda kv, b, g: (kv, 0, 0)),
                  pl.BlockSpec((1, CMP_HIDDEN, HEAD_DIM), lambda kv, b, g: (kv, 0, 0)),
                  pl.BlockSpec((1, 1, HEAD_DIM), lambda kv, b, g: (kv, 0, 0)),
                  pl.BlockSpec((1, HEAD_DIM), lambda kv, b, g: (0, 0))],
        out_specs=pl.BlockSpec((1, 1, nchunk, HEAD_DIM), lambda kv, b, g: (kv, g, b, 0)),
        out_shape=jax.ShapeDtypeStruct((2, NSA_GROUPS, bsz * nchunk, HEAD_DIM), BF16),
        compiler_params=_cparams(("arbitrary", "arbitrary", "arbitrary")),
        name="nsa_compress",
    )(xv, pe, cmp_w1.astype(BF16), cmp_b1.reshape(2, 1, CMP_HIDDEN), cmp_w2.astype(BF16),
      cmp_b2.reshape(2, 1, HEAD_DIM), g_k.reshape(1, HEAD_DIM))


KW = 512
PV_KEYS = 256


def _tile_lanes(x, n):
    return jnp.concatenate([x] * n, axis=1)


def _flash_init(m_ref, l_ref, acc_ref):
    m_ref[...] = jnp.full(m_ref.shape, NEG, F32)
    l_ref[...] = jnp.zeros(l_ref.shape, F32)
    acc_ref[...] = jnp.zeros(acc_ref.shape, F32)


def _zero_after(x):
    bits = pltpu.bitcast(x, jnp.int32)
    return lax.shift_right_logical(lax.shift_right_logical(bits, 16), 16).astype(F32)


def _flash_update(s, v_t, m_ref, l_ref, acc_ref, col_max=None, after=None):
    m_old = m_ref[...]
    if col_max is None:
        col_max = jnp.max(s, axis=0, keepdims=True)
    m_new = jnp.maximum(m_old, col_max)
    alpha = jnp.exp2(m_old - m_new)
    l_new = alpha * l_ref[...]
    acc = alpha * acc_ref[...]
    nk = s.shape[0]
    for k0 in range(0, nk, PV_KEYS):
        p = jnp.exp2(s[k0:k0 + PV_KEYS] - m_new)
        l_new = l_new + jnp.sum(p, axis=0, keepdims=True)
        acc = acc + _dot(v_t[:, k0:k0 + PV_KEYS], p.astype(BF16))
    l_ref[...] = l_new
    acc_ref[...] = acc
    m_ref[...] = m_new if after is None else m_new + _zero_after(after)


SAFE_LOG2_BOUND = 60.0


def _flash_accumulate(s, v_t, l_ref, acc_ref, after=None):
    l_new = l_ref[...]
    acc = acc_ref[...]
    for k0 in range(0, s.shape[0], PV_KEYS):
        p = jnp.exp2(s[k0:k0 + PV_KEYS])
        l_new = l_new + jnp.sum(p, axis=0, keepdims=True)
        acc = acc + _dot(v_t[:, k0:k0 + PV_KEYS], p.astype(BF16))
    if after is not None:
        l_new = l_new + jnp.max(_zero_after(after), axis=0, keepdims=True)
    l_ref[...] = l_new
    acc_ref[...] = acc


def _sum_result(l_ref, acc_ref):
    den = l_ref[...]
    ok = den > 0.0
    return acc_ref[...] * jnp.where(ok, 1.0 / jnp.where(ok, den, 1.0), 0.0)


def _inv_den(m, den):
    ok = m > 0.5 * NEG
    return jnp.where(ok, 1.0 / jnp.where(ok, den, 1.0), 0.0)


def _flash_result(m_ref, l_ref, acc_ref):
    return acc_ref[...] * _inv_den(m_ref[...], l_ref[...])


def _softmax_cols(s):
    m = jnp.max(s, axis=0, keepdims=True)
    p = jnp.exp2(s - m)
    return p * _inv_den(m, jnp.sum(p, axis=0, keepdims=True))


def _near_bias(dt_ref, heads, qi, kt0, ntile):
    rows = []
    for j in range(ntile):
        rel = jnp.clip(qi - (kt0 + j), 0, 2)
        rows.append(jnp.concatenate([dt_ref[h, rel] for h in heads], axis=1))
    return jnp.concatenate(rows, axis=0)


def _pipelined_chunks(n, qk_stage, soft_stage):
    @pl.when(n > 0)
    def _():
        qk_stage(0, 0)

    def pair(p, x):
        c = 2 * p
        ahead = qk_stage(c + 1, 1)
        soft_stage(c, 0, ahead)
        ahead = qk_stage(jnp.minimum(c + 2, n - 1), 0)
        soft_stage(c + 1, 1, ahead)
        return x

    lax.fori_loop(0, n // 2, pair, 0)

    @pl.when(n % 2 == 1)
    def _():
        soft_stage(n - 1, 0, None)


NSA_STATE = 9
GATE_ROWS = -(-3 * NSA_HPG // SUBLANE) * SUBLANE


def _nsa_kernel(bound_ref, qt_ref, gt_ref, kc_ref, vct_ref, ks_ref, vst_ref, kw_ref, vwt_ref,
                dt_ref, dc_ref, ext_ref, o_ref, *scratch, seq, nc):
    ng = NSA_GROUPS
    state = [scratch[NSA_STATE * g:NSA_STATE * (g + 1)] for g in range(ng)]
    qi = pl.program_id(1)
    q0 = qi * QB
    hpg = NSA_HPG
    ncp = kc_ref.shape[1]
    ns = seq // SEL_BLOCK
    group_heads = [[g * hpg + h for h in range(hpg)] for g in range(ng)]
    q_ts = [jnp.concatenate([qt_ref[h] for h in group_heads[g]], axis=1) for g in range(ng)]
    pad = CWIN // 2
    wkeys = WINDOW + QB
    start = pl.multiple_of(jnp.maximum(q0 - WINDOW, 0), LANE)
    r0 = pl.multiple_of(qi * (QB // CMP_STRIDE), 8)

    s_w = []
    for g in range(ng):
        sc_ref = state[g][0]
        sc_ref[0:pad, :] = jnp.zeros((pad, hpg * QB), F32)
        sc_ref[pad + ncp:2 * pad + ncp, :] = jnp.zeros((pad, hpg * QB), F32)
        sc_ref[pad:pad + ncp, :] = _dot(kc_ref[g], q_ts[g])
        s_w.append(_dot(kw_ref[g, pl.ds(start, wkeys), :], q_ts[g]))

    ci = lax.broadcasted_iota(jnp.int32, (ncp, QB), 0)
    tc = q0 + lax.broadcasted_iota(jnp.int32, (ncp, QB), 1)
    valid_c = (ci * CMP_STRIDE + CMP_BLOCK - 1 <= tc) & (ci < nc)
    madd_c = _tile_lanes(jnp.where(valid_c, 0.0, NEG), hpg)
    oc_t, p_sum = [], []
    for g in range(ng):
        sc_ref = state[g][0]
        sc_ref[pl.ds(r0, CWIN), :] = sc_ref[pl.ds(r0, CWIN), :] + jnp.concatenate(
            [dc_ref[h] for h in group_heads[g]], axis=1)
        p_c = _softmax_cols(sc_ref[pad:pad + ncp, :] + madd_c)
        oc_t.append(_dot(vct_ref[g], p_c.astype(BF16)))
        ps = p_c[:, 0:QB]
        for h in range(1, hpg):
            ps = ps + p_c[:, h * QB:(h + 1) * QB]
        p_sum.append(ps)

    dist_w = (q0 + lax.broadcasted_iota(jnp.int32, (wkeys, QB), 1)) - (
        start + lax.broadcasted_iota(jnp.int32, (wkeys, QB), 0))
    madd_w = _tile_lanes(jnp.where((dist_w >= 0) & (dist_w < WINDOW), 0.0, NEG), hpg)
    ow_t = []
    for g in range(ng):
        p_w = _softmax_cols(s_w[g] + _near_bias(dt_ref, group_heads[g], qi, start // LANE, wkeys // LANE) + madd_w)
        ow_t.append(_dot(vwt_ref[g, :, pl.ds(start, wkeys)], p_w.astype(BF16)))

    per = SEL_BLOCK // CMP_STRIDE
    blk = lax.broadcasted_iota(jnp.int32, (LANE, QB), 0)
    t = q0 + lax.broadcasted_iota(jnp.int32, (LANE, QB), 1)
    tb = t // SEL_BLOCK
    forced = (blk == 0) | (blk == tb) | (blk == tb - 1)
    blk_f = blk.astype(F32)
    scores = []
    for g in range(ng):
        ps_ref = state[g][1]
        ps_ref[0:SUBLANE, :] = jnp.zeros((SUBLANE, QB), F32)
        ps_ref[SUBLANE:SUBLANE + ncp, :] = p_sum[g]
        band = [ps_ref[pl.ds(SUBLANE + r, ns, stride=per), :] for r in range(-1, per)]
        imp = 0.5 * band[0] + band[1] + band[2] + band[3] + 0.5 * band[4]
        if ns < LANE:
            imp = jnp.concatenate([imp, jnp.zeros((LANE - ns, QB), F32)], axis=0)
        score = jnp.where(forced, FORCE, jnp.where(blk * SEL_BLOCK <= t, imp, NEG))
        scores.append(jnp.where(blk < ns, score, -jnp.inf))
    sels = [jnp.zeros((LANE, QB), F32) for _ in range(ng)]
    for _ in range(min(SEL_TOP_N, ns)):
        for g in range(ng):
            mx = jnp.max(scores[g], axis=0, keepdims=True)
            first = jnp.min(jnp.where(scores[g] == mx, blk_f, float(LANE)), axis=0, keepdims=True)
            pick = blk_f == first
            sels[g] = jnp.where(pick, 1.0, sels[g])
            scores[g] = jnp.where(pick, -jnp.inf, scores[g])
    sel_b = [s.astype(BF16) for s in sels]

    kpos = lax.broadcasted_iota(jnp.int32, (KW, QB), 0)
    tq = q0 + lax.broadcasted_iota(jnp.int32, (KW, QB), 1)
    bounded_ok = bound_ref[0] <= SAFE_LOG2_BOUND
    shift = jnp.where(bounded_ok, bound_ref[0], 0.0)
    c_near = jnp.maximum(qi - 1, 0) // (KW // LANE)

    def scores_of(g, c0):
        chosen = _dot(ext_ref[pl.ds(c0, KW), :], sel_b[g])
        return (chosen - 1.0) * (-NEG) - shift, _dot(ks_ref[g, pl.ds(c0, KW), :], q_ts[g])

    def attend(bounded):
        for g in range(ng):
            _flash_init(*state[g][2:5])

        def qk_stage(c, buf):
            c0 = pl.multiple_of(c * KW, KW)
            ahead = []
            for g in range(ng):
                madd, s = scores_of(g, c0)
                s = s + _tile_lanes(madd, hpg)
                state[g][5 + buf][...] = s
                if bounded:
                    ahead.append(s[KW - 8:KW])
                else:
                    ahead.append(jnp.max(s, axis=0, keepdims=True))
                    state[g][7 + buf][...] = ahead[-1]
            return ahead

        def soft_stage(c, buf, ahead):
            c0 = pl.multiple_of(c * KW, KW)
            for g in range(ng):
                m_ref, l_ref, acc_ref = state[g][2:5]
                after = None if ahead is None else ahead[g]
                v_t = vst_ref[g, :, pl.ds(c0, KW)]
                if bounded:
                    _flash_accumulate(state[g][5 + buf][...], v_t, l_ref, acc_ref, after=after)
                else:
                    _flash_update(state[g][5 + buf][...], v_t, m_ref, l_ref, acc_ref,
                                  col_max=state[g][7 + buf][...], after=after)

        def near_step(c, x):
            c0 = pl.multiple_of(c * KW, KW)
            causal = jnp.where(c0 + kpos <= tq, 0.0, NEG)
            scores = []
            for g in range(ng):
                madd, s = scores_of(g, c0)
                scores.append(s + _tile_lanes(madd + causal, hpg)
                              + _near_bias(dt_ref, group_heads[g], qi, c * (KW // LANE), KW // LANE))
            for g in range(ng):
                m_ref, l_ref, acc_ref = state[g][2:5]
                if bounded:
                    _flash_accumulate(scores[g], vst_ref[g, :, pl.ds(c0, KW)], l_ref, acc_ref)
                else:
                    _flash_update(scores[g], vst_ref[g, :, pl.ds(c0, KW)], m_ref, l_ref, acc_ref)
            return x

        _pipelined_chunks(c_near, qk_stage, soft_stage)
        lax.fori_loop(c_near, qi // (KW // LANE) + 1, near_step, 0)
        for g in range(ng):
            m_ref, l_ref, acc_ref = state[g][2:5]
            acc_ref[...] = _sum_result(l_ref, acc_ref) if bounded else _flash_result(m_ref, l_ref, acc_ref)

    pl.when(bounded_ok)(lambda: attend(True))
    pl.when(jnp.logical_not(bounded_ok))(lambda: attend(False))

    for g in range(ng):
        os_t = state[g][4][...]
        gates = jax.nn.sigmoid(gt_ref[g])
        for h in range(hpg):
            sl = slice(h * QB, (h + 1) * QB)
            o_t = (gates[3 * h:3 * h + 1] * oc_t[g][:, sl] + gates[3 * h + 1:3 * h + 2] * os_t[:, sl]
                   + gates[3 * h + 2:3 * h + 3] * ow_t[g][:, sl])
            hh = group_heads[g][h]
            o_ref[:, hh * HEAD_DIM:(hh + 1) * HEAD_DIM] = o_t.T.astype(o_ref.dtype)


def nsa_attention(logit_bound, q_t, gates_t, kc, vc_t, k_sw, v_sw_t, dt, dc, bsz, seq):
    nq = seq // QB
    ncp = seq // CMP_STRIDE
    nc = ncp - 1
    ns = seq // SEL_BLOCK
    assert ns <= LANE and seq >= WINDOW + QB and seq % KW == 0
    assert CMP_BLOCK == 2 * CMP_STRIDE and SEL_BLOCK == 4 * CMP_STRIDE
    expand =((np.arange(seq)[:, None] // SEL_BLOCK) == np.arange(LANE)[None, :]).astype(np.float32)
    ng = NSA_GROUPS
    once = dict(pipeline_mode=pl.Buffered(1))
    ks_spec = pl.BlockSpec((ng, seq, HEAD_DIM), lambda b, i: (0, b, 0), **once)
    kw_spec = pl.BlockSpec((ng, seq, HEAD_DIM), lambda b, i: (1, b, 0), **once)
    vs_spec = pl.BlockSpec((ng, HEAD_DIM, seq), lambda b, i: (0, 0, b), **once)
    vw_spec = pl.BlockSpec((ng, HEAD_DIM, seq), lambda b, i: (1, 0, b), **once)
    lanes = NSA_HPG * QB
    group_state = [pltpu.VMEM((ncp + CWIN, lanes), F32), pltpu.VMEM((ncp + SUBLANE, QB), F32),
                   pltpu.VMEM((1, lanes), F32), pltpu.VMEM((1, lanes), F32), pltpu.VMEM((HEAD_DIM, lanes), F32),
                   pltpu.VMEM((KW, lanes), F32), pltpu.VMEM((KW, lanes), F32),
                   pltpu.VMEM((1, lanes), F32), pltpu.VMEM((1, lanes), F32)]
    assert len(group_state) == NSA_STATE
    return pl.pallas_call(
        functools.partial(_nsa_kernel, seq=seq, nc=nc),
        grid=(bsz, nq),
        in_specs=[pl.BlockSpec(memory_space=pltpu.SMEM),
                  pl.BlockSpec((NSA_HEADS, HEAD_DIM, QB), lambda b, i: (0, 0, b * nq + i)),
                  pl.BlockSpec((ng, GATE_ROWS, QB), lambda b, i: (0, 0, b * nq + i)),
                  pl.BlockSpec((ng, ncp, HEAD_DIM), lambda b, i: (0, b, 0)),
                  pl.BlockSpec((ng, HEAD_DIM, ncp), lambda b, i: (0, 0, b)),
                  ks_spec, vs_spec, kw_spec, vw_spec,
                  pl.BlockSpec((NSA_HEADS, 3, LANE, LANE), lambda b, i: (0, 0, 0, 0)),
                  pl.BlockSpec((NSA_HEADS, CWIN, LANE), lambda b, i: (0, 0, 0)),
                  pl.BlockSpec((seq, LANE), lambda b, i: (0, 0))],
        out_specs=pl.BlockSpec((QB, NSA_HEADS * HEAD_DIM), lambda b, i: (b * nq + i, 0)),
        out_shape=jax.ShapeDtypeStruct((bsz * seq, NSA_HEADS * HEAD_DIM), BF16),
        scratch_shapes=group_state * ng,
        compiler_params=_cparams(("arbitrary", "arbitrary")),
        name="nsa_attention",
    )(jnp.reshape(logit_bound, (1,)).astype(F32), q_t, gates_t, kc, vc_t, k_sw, v_sw_t, k_sw, v_sw_t, dt, dc,
      jnp.asarray(expand, BF16))


MLA_HPS = 2


def _mla_kernel(bound_ref, qt_ref, k_ref, vt_ref, o_ref, *scratch):
    qi = pl.program_id(2)
    chains = [scratch[3 * h:3 * h + 3] for h in range(MLA_HPS)]
    sbuf = [scratch[(3 + b) * MLA_HPS:(4 + b) * MLA_HPS] for b in range(2)]
    cbuf = [scratch[(5 + b) * MLA_HPS:(6 + b) * MLA_HPS] for b in range(2)]
    c_diag = pl.multiple_of(qi * KW, KW)
    kpos = lax.broadcasted_iota(jnp.int32, (KW, KW), 0)
    tq = lax.broadcasted_iota(jnp.int32, (KW, KW), 1)
    dv = vt_ref.shape[1]

    def attend(bounded):
        for ch in chains:
            _flash_init(*ch)

        def qk_stage(c, buf):
            c0 = pl.multiple_of(c * KW, KW)
            ahead = []
            for h in range(MLA_HPS):
                s = _dot(k_ref[h, pl.ds(c0, KW), :], qt_ref[h])
                sbuf[buf][h][...] = s
                if bounded:
                    ahead.append(s[KW - 8:KW])
                else:
                    ahead.append(jnp.max(s, axis=0, keepdims=True))
                    cbuf[buf][h][...] = ahead[-1]
            return ahead

        def soft_stage(c, buf, ahead):
            c0 = pl.multiple_of(c * KW, KW)
            for h, (m_ref, l_ref, acc_ref) in enumerate(chains):
                after = None if ahead is None else ahead[h]
                if bounded:
                    _flash_accumulate(sbuf[buf][h][...], vt_ref[h, :, pl.ds(c0, KW)], l_ref, acc_ref, after=after)
                else:
                    _flash_update(sbuf[buf][h][...], vt_ref[h, :, pl.ds(c0, KW)], m_ref, l_ref, acc_ref,
                                  col_max=cbuf[buf][h][...], after=after)

        _pipelined_chunks(qi, qk_stage, soft_stage)
        causal = jnp.where(kpos <= tq, 0.0, NEG)
        scores = [_dot(k_ref[h, pl.ds(c_diag, KW), :], qt_ref[h]) + causal for h in range(MLA_HPS)]
        for h, (m_ref, l_ref, acc_ref) in enumerate(chains):
            if bounded:
                _flash_accumulate(scores[h], vt_ref[h, :, pl.ds(c_diag, KW)], l_ref, acc_ref)
                o_t = _sum_result(l_ref, acc_ref)
            else:
                _flash_update(scores[h], vt_ref[h, :, pl.ds(c_diag, KW)], m_ref, l_ref, acc_ref)
                o_t = _flash_result(m_ref, l_ref, acc_ref)
            o_ref[:, h * dv:(h + 1) * dv] = o_t.T.astype(o_ref.dtype)

    bounded_ok = bound_ref[0] <= SAFE_LOG2_BOUND
    pl.when(bounded_ok)(lambda: attend(True))
    pl.when(jnp.logical_not(bounded_ok))(lambda: attend(False))


def mla_attention(logit_bound, q_t, k, v_t, bsz, seq):
    nh, dqk, _ = q_t.shape
    dv = v_t.shape[1]
    nq = seq // KW
    hps = MLA_HPS
    state = [pltpu.VMEM((1, KW), F32), pltpu.VMEM((1, KW), F32), pltpu.VMEM((dv, KW), F32)] * hps
    state += [pltpu.VMEM((KW, KW), F32)] * (2 * hps)
    state += [pltpu.VMEM((1, KW), F32)] * (2 * hps)
    return pl.pallas_call(
        _mla_kernel,
        grid=(bsz, nh // hps, nq),
        in_specs=[pl.BlockSpec(memory_space=pltpu.SMEM),
                  pl.BlockSpec((hps, dqk, KW), lambda b, h, i: (h, 0, b * nq + i)),
                  pl.BlockSpec((hps, seq, dqk), lambda b, h, i: (h, b, 0)),
                  pl.BlockSpec((hps, dv, seq), lambda b, h, i: (h, 0, b))],
        out_specs=pl.BlockSpec((KW, hps * dv), lambda b, h, i: (b * nq + i, h)),
        out_shape=jax.ShapeDtypeStruct((bsz * seq, nh * dv), BF16),
        scratch_shapes=state,
        compiler_params=_cparams(("arbitrary", "arbitrary", "arbitrary")),
        name="mla_attention",
    )(jnp.reshape(logit_bound, (1,)).astype(F32), q_t, k, v_t)


INT_MIN = -2 ** 31
NEG_KEY = int(np.array(NEG, np.float32).view(np.int32)) ^ 0x7FFFFFFF
KEY_BITS = 32
SURE_BITS = 20


def _sort_key(x):
    bits = pltpu.bitcast(x + 0.0, jnp.int32)
    return jnp.where(bits < 0, bits ^ 0x7FFFFFFF, bits)


def _dsa_kernel(bound_ref, iqt_ref, iwt_ref, ik_ref, qt_ref, k_ref, vt_ref, dt_ref, o_ref,
                key_ref, madd_ref, *state, seq, k_sel):
    qi = pl.program_id(1)
    q0 = qi * QB
    n_chunk = (q0 + QB + KW - 1) // KW
    n_rest = seq - n_chunk * KW
    kpos = lax.broadcasted_iota(jnp.int32, (KW, QB), 0)
    tq = q0 + lax.broadcasted_iota(jnp.int32, (KW, QB), 1)
    hpp = KW // QB

    def score_chunk(c, x):
        c0 = pl.multiple_of(c * KW, KW)
        ikc = ik_ref[pl.ds(c0, KW), :]
        acc = jnp.zeros((KW, QB), F32)
        for piece in range(IDX_HEADS // hpp):
            sl = slice(piece * KW, (piece + 1) * KW)
            s = jnp.maximum(_dot(ikc, iqt_ref[0, :, sl]), 0.0) * iwt_ref[0, :, sl]
            for j in range(hpp):
                acc = acc + s[:, j * QB:(j + 1) * QB]
        acc = jnp.where(c0 + kpos <= tq, acc, NEG)
        key_ref[pl.ds(c0, KW), :] = _sort_key(acc)
        return x

    lax.fori_loop(0, n_chunk, score_chunk, 0)

    def count(pred):
        def body(c, acc):
            c0 = pl.multiple_of(c * KW, KW)
            hit = jnp.where(pred(key_ref[pl.ds(c0, KW), :], c0), 1.0, 0.0)
            parts = [hit[SUBLANE * i:SUBLANE * (i + 1)] for i in range(KW // SUBLANE)]
            while len(parts) > 1:
                parts = [parts[i] + parts[i + 1] for i in range(0, len(parts), 2)]
            return acc + parts[0]
        acc = lax.fori_loop(0, n_chunk, body, jnp.zeros((SUBLANE, QB), F32))
        return jnp.sum(acc, axis=0, keepdims=True)

    rest = n_rest.astype(F32)
    kf = float(k_sel)

    def bit_step(i, st):
        u, thr_s, settled = st
        bit = jnp.left_shift(jnp.int32(1), KEY_BITS - 1 - i)
        trial = (u | bit) ^ INT_MIN
        cnt = count(lambda keys, c0: keys >= trial) + jnp.where(NEG_KEY >= trial, rest, 0.0)
        new = (cnt == kf) & (settled < 0.5)
        return (jnp.where(cnt >= kf, u | bit, u), jnp.where(new, trial, thr_s), jnp.where(new, 1.0, settled))

    st = (jnp.zeros((1, QB), jnp.int32), jnp.zeros((1, QB), jnp.int32), jnp.zeros((1, QB), F32))
    st = lax.fori_loop(0, SURE_BITS, bit_step, st)
    _, (u, thr_s, settled) = lax.while_loop(
        lambda c: (c[0] < KEY_BITS) & (jnp.min(c[1][2]) < 0.5),
        lambda c: (c[0] + 1, bit_step(c[0], c[1])), (jnp.int32(SURE_BITS), st))
    is_settled = settled > 0.5
    thr = jnp.where(is_settled, thr_s, u ^ INT_MIN)

    def edge_counts():
        return (count(lambda keys, c0: keys > thr) + jnp.where(NEG_KEY > thr, rest, 0.0),
                count(lambda keys, c0: keys >= thr) + jnp.where(NEG_KEY >= thr, rest, 0.0))

    zero_cnt = jnp.zeros((1, QB), F32)
    cnt_gt, cnt_ge = lax.cond(jnp.min(settled) > 0.5, lambda: (zero_cnt, zero_cnt), edge_counts)
    need = kf - cnt_gt
    tie_q = (cnt_ge > kf) & (thr != NEG_KEY) & jnp.logical_not(is_settled)
    idx_bits = (seq - 1).bit_length()
    no_cut = 2 ** 30

    def tie_cut():
        def idx_step(i, x):
            bit = jnp.left_shift(jnp.int32(1), idx_bits - 1 - i)
            trial = x | bit
            f = count(lambda keys, c0: (keys == thr) & (c0 + kpos < trial))
            return jnp.where(f <= need - 1.0, trial, x)
        return lax.fori_loop(0, idx_bits, idx_step, jnp.zeros((1, QB), jnp.int32))

    any_tie = jnp.max(jnp.where(tie_q, 1.0, 0.0)) > 0.0
    x_cut = lax.cond(any_tie, tie_cut, lambda: jnp.full((1, QB), no_cut, jnp.int32))
    x_cut = jnp.where(tie_q, x_cut, no_cut)

    def mask_chunk(c, x):
        c0 = pl.multiple_of(c * KW, KW)
        keys = key_ref[pl.ds(c0, KW), :]
        pos = c0 + kpos
        chosen = (keys > thr) | ((keys == thr) & (pos <= x_cut))
        madd_ref[pl.ds(c0, KW), :] = jnp.where(chosen & (pos <= tq), -shift, NEG)
        return x

    bounded_ok = bound_ref[0] <= SAFE_LOG2_BOUND
    shift = jnp.where(bounded_ok, bound_ref[0], 0.0)
    lax.fori_loop(0, n_chunk, mask_chunk, 0)

    c_near = jnp.maximum(qi - 1, 0) // (KW // LANE)
    ng = DSA_KV_HEADS
    chains = [state[3 * g:3 * g + 3] for g in range(ng)]
    sbuf = [state[(3 + b) * ng:(4 + b) * ng] for b in range(2)]
    cbuf = [state[(5 + b) * ng:(6 + b) * ng] for b in range(2)]
    group_heads = [[g * DSA_HPG + h for h in range(DSA_HPG)] for g in range(ng)]

    def raw_scores(c0, g):
        q_t = jnp.concatenate([qt_ref[h] for h in group_heads[g]], axis=1)
        return _dot(k_ref[g, pl.ds(c0, KW), :], q_t)

    def attend(bounded):
        for ch in chains:
            _flash_init(*ch)

        def qk_stage(c, buf):
            c0 = pl.multiple_of(c * KW, KW)
            madd = _tile_lanes(madd_ref[pl.ds(c0, KW), :], DSA_HPG)
            ahead = []
            for g in range(ng):
                s = raw_scores(c0, g) + madd
                sbuf[buf][g][...] = s
                if bounded:
                    ahead.append(s[KW - 8:KW])
                else:
                    ahead.append(jnp.max(s, axis=0, keepdims=True))
                    cbuf[buf][g][...] = ahead[-1]
            return ahead

        def soft_stage(c, buf, ahead):
            c0 = pl.multiple_of(c * KW, KW)
            for g, (m_ref, l_ref, acc_ref) in enumerate(chains):
                after = None if ahead is None else ahead[g]
                if bounded:
                    _flash_accumulate(sbuf[buf][g][...], vt_ref[g, :, pl.ds(c0, KW)], l_ref, acc_ref, after=after)
                else:
                    _flash_update(sbuf[buf][g][...], vt_ref[g, :, pl.ds(c0, KW)], m_ref, l_ref, acc_ref,
                                  col_max=cbuf[buf][g][...], after=after)

        _pipelined_chunks(c_near, qk_stage, soft_stage)

        def near_step(c, x):
            c0 = pl.multiple_of(c * KW, KW)
            madd = _tile_lanes(madd_ref[pl.ds(c0, KW), :], DSA_HPG)
            scores = [raw_scores(c0, g) + madd
                      + _near_bias(dt_ref, group_heads[g], qi, c * (KW // LANE), KW // LANE)
                      for g in range(ng)]
            for g, (m_ref, l_ref, acc_ref) in enumerate(chains):
                if bounded:
                    _flash_accumulate(scores[g], vt_ref[g, :, pl.ds(c0, KW)], l_ref, acc_ref)
                else:
                    _flash_update(scores[g], vt_ref[g, :, pl.ds(c0, KW)], m_ref, l_ref, acc_ref)
            return x

        lax.fori_loop(c_near, n_chunk, near_step, 0)
        for g, (m_ref, l_ref, acc_ref) in enumerate(chains):
            o_t = _sum_result(l_ref, acc_ref) if bounded else _flash_result(m_ref, l_ref, acc_ref)
            for h in range(DSA_HPG):
                hh = group_heads[g][h]
                o_ref[:, hh * HEAD_DIM:(hh + 1) * HEAD_DIM] = o_t[:, h * QB:(h + 1) * QB].T.astype(o_ref.dtype)

    pl.when(bounded_ok)(lambda: attend(True))
    pl.when(jnp.logical_not(bounded_ok))(lambda: attend(False))


def _idx_prep_kernel(p_ref, c_ref, sa_ref, sb_ref, iqt_ref, ik_ref, iwt_ref, *, ntile):
    nslab_q = IDX_HEADS * IDX_DIM // LANE
    per = LANE // IDX_DIM
    half = IDX_ROPE // 2
    zrows = jnp.zeros((LANE - IDX_DIM, QB), F32)

    def rope_slab(x, c, sa, sb):
        return x * c + pltpu.roll(x, LANE - half, axis=1) * sa + pltpu.roll(x, half, axis=1) * sb

    for t in range(ntile):
        rows = slice(t * QB, (t + 1) * QB)
        c, sa, sb = c_ref[rows, :], sa_ref[rows, :], sb_ref[rows, :]
        cols = []
        for s in range(nslab_q):
            x_t = (rope_slab(p_ref[s, rows, :], c, sa, sb) * IDX_DIM ** -0.5).T
            for j in range(per):
                cols.append(jnp.concatenate([x_t[j * IDX_DIM:(j + 1) * IDX_DIM], zrows], axis=0))
        iqt_ref[t] = jnp.concatenate(cols, axis=1).astype(iqt_ref.dtype)
        tail = p_ref[nslab_q, rows, :]
        lane = lax.broadcasted_iota(jnp.int32, (QB, LANE), 1)
        ik_ref[rows, :] = jnp.where(lane < IDX_DIM, rope_slab(tail, c, sa, sb), 0.0).astype(ik_ref.dtype)
        w_t = (tail * IDX_HEADS ** -0.5).T
        iwt_ref[t] = jnp.concatenate([w_t[IDX_DIM + h:IDX_DIM + h + 1, :] for h in range(IDX_HEADS)], axis=1)


def indexer_operands(proj, seq, tm=512):
    _, m, _ = proj.shape
    ntile = tm // QB
    tps = seq // tm
    cos, sin = _rope_tables(seq, IDX_ROPE)
    zero = jnp.zeros_like(sin)
    rest = IDX_DIM - IDX_ROPE
    per = LANE // IDX_DIM
    c_tab = jnp.tile(jnp.concatenate([cos, cos, jnp.ones((seq, rest), F32)], axis=1), (1, per))
    sa_tab = jnp.tile(jnp.concatenate([-sin, zero, jnp.zeros((seq, rest), F32)], axis=1), (1, per))
    sb_tab = jnp.tile(jnp.concatenate([zero, sin, jnp.zeros((seq, rest), F32)], axis=1), (1, per))
    lanes = IDX_HEADS * QB
    tab_spec = pl.BlockSpec((tm, LANE), lambda i: (i % tps, 0))
    return pl.pallas_call(
        functools.partial(_idx_prep_kernel, ntile=ntile),
        grid=(m // tm,),
        in_specs=[pl.BlockSpec((proj.shape[0], tm, LANE), lambda i: (0, i, 0)), tab_spec, tab_spec, tab_spec],
        out_specs=[pl.BlockSpec((ntile, LANE, lanes), lambda i: (i, 0, 0)),
                   pl.BlockSpec((tm, LANE), lambda i: (i, 0)),
                   pl.BlockSpec((ntile, 1, lanes), lambda i: (i, 0, 0))],
        out_shape=[jax.ShapeDtypeStruct((m // QB, LANE, lanes), BF16),
                   jax.ShapeDtypeStruct((m, LANE), BF16),
                   jax.ShapeDtypeStruct((m // QB, 1, lanes), F32)],
        compiler_params=_cparams(("arbitrary",)),
        name="dsa_indexer_operands",
    )(proj, c_tab, sa_tab, sb_tab)


def dsa_attention(logit_bound, iq_t, iw_t, ik, q_t, k, v_t, dt, bsz, seq):
    nq = seq // QB
    k_sel = min(DSA_TOPK_MAX, seq // 4)
    assert seq % KW == 0
    lanes = DSA_HPG * QB
    return pl.pallas_call(
        functools.partial(_dsa_kernel, seq=seq, k_sel=k_sel),
        grid=(bsz, nq),
        in_specs=[pl.BlockSpec(memory_space=pltpu.SMEM),
                  pl.BlockSpec((1, LANE, IDX_HEADS * QB), lambda b, i: (b * nq + i, 0, 0)),
                  pl.BlockSpec((1, 1, IDX_HEADS * QB), lambda b, i: (b * nq + i, 0, 0)),
                  pl.BlockSpec((seq, LANE), lambda b, i: (b, 0)),
                  pl.BlockSpec((DSA_HEADS, HEAD_DIM, QB), lambda b, i: (0, 0, b * nq + i)),
                  pl.BlockSpec((DSA_KV_HEADS, seq, HEAD_DIM), lambda b, i: (0, b, 0),
                               pipeline_mode=pl.Buffered(1)),
                  pl.BlockSpec((DSA_KV_HEADS, HEAD_DIM, seq), lambda b, i: (0, 0, b),
                               pipeline_mode=pl.Buffered(1)),
                  pl.BlockSpec((DSA_HEADS, 3, LANE, LANE), lambda b, i: (0, 0, 0, 0),
                               pipeline_mode=pl.Buffered(1))],
        out_specs=pl.BlockSpec((QB, DSA_HEADS * HEAD_DIM), lambda b, i: (b * nq + i, 0)),
        out_shape=jax.ShapeDtypeStruct((bsz * seq, DSA_HEADS * HEAD_DIM), BF16),
        scratch_shapes=[pltpu.VMEM((seq, QB), jnp.int32), pltpu.VMEM((seq, QB), F32)]
        + [pltpu.VMEM((1, lanes), F32), pltpu.VMEM((1, lanes), F32),
           pltpu.VMEM((HEAD_DIM, lanes), F32)] * DSA_KV_HEADS
        + [pltpu.VMEM((KW, lanes), F32)] * (2 * DSA_KV_HEADS)
        + [pltpu.VMEM((1, lanes), F32)] * (2 * DSA_KV_HEADS),
        compiler_params=_cparams(("arbitrary", "arbitrary")),
        name="dsa_attention",
    )(jnp.reshape(logit_bound, (1,)).astype(F32), iq_t, iw_t, ik, q_t, k, v_t, dt)


def _rope_tables(seq, dim):
    half = dim // 2
    inv = ROPE_THETA ** (-jnp.arange(half, dtype=F32) / half)
    ang = jnp.arange(seq, dtype=F32)[:, None] * inv[None, :]
    return jnp.cos(ang), jnp.sin(ang)


def _logit_bound(gq, gk, dim, scale):
    return dim * scale * jnp.max(jnp.abs(gq)) * jnp.max(jnp.abs(gk)) * (1.0 + 2.0 ** -7)


def _pad_cols(w, n):
    return jnp.pad(w, ((0, 0), (0, n - w.shape[1])))


def _t(x):
    return jnp.swapaxes(x, -1, -2)


def _even_mixer(h, x2, gate, dt, dc, bias_bound, bsz, seq, w_in, w_out, nsa_qk_g, cmp_pe, cmp_w1, cmp_b1,
                cmp_w2, cmp_b2, q_norm_g, kv_norm_g, w_uq, w_ukv, nope_g, rope_g):
    m = bsz * seq
    nq_cols = NSA_HEADS * HEAD_DIM
    nkv_cols = 6 * NSA_GROUPS * HEAD_DIM
    ngate = 3 * NSA_HEADS
    o_gate = nq_cols + nkv_cols
    o_cq = o_gate + ngate
    o_ckv = o_cq + MLA_Q_RANK
    o_kpe = o_ckv + MLA_KV_RANK
    gw = NSA_GROUPS * HEAD_DIM
    kvw = [w_in[:, nq_cols + i * gw:nq_cols + (i + 1) * gw] for i in range(6)]
    scale = HEAD_DIM ** -0.5 * LOG2E
    q_t = proj_heads(h, w_in[:, :nq_cols].astype(BF16), nsa_qk_g[0] * scale, transpose=True)
    k_sw = proj_heads(h, jnp.concatenate([kvw[2], kvw[4]], axis=1).astype(BF16), nsa_qk_g[1],
                      transpose=False)
    v_sw_t = proj_heads(h, jnp.concatenate([kvw[3], kvw[5]], axis=1).astype(BF16), transpose=True)
    tail = jnp.concatenate([w_in[:, o_kpe:], w_in[:, o_gate:o_cq]], axis=1)
    w_r = jnp.concatenate([kvw[0], kvw[1], w_in[:, o_cq:o_kpe], _pad_cols(tail, LANE)], axis=1).astype(BF16)
    proj = proj_slabs(h, w_r, tn=w_r.shape[1])
    s_cq = 2 * NSA_GROUPS
    s_ckv = s_cq + MLA_Q_RANK // LANE
    s_tail = s_ckv + MLA_KV_RANK // LANE
    kvc = compress_kv(proj, 0, bsz, seq, cmp_pe, cmp_w1, cmp_b1, cmp_w2, cmp_b2, nsa_qk_g[1])
    tail_v = proj[s_tail]
    gates = tail_v[:, MLA_ROPE:MLA_ROPE + ngate].reshape(m, NSA_GROUPS, 3 * NSA_HPG)
    gates_t = jnp.pad(jnp.transpose(gates, (1, 2, 0)), ((0, 0), (0, GATE_ROWS - 3 * NSA_HPG), (0, 0)))
    nsa_bound = _logit_bound(nsa_qk_g[0], nsa_qk_g[1], HEAD_DIM, scale) + bias_bound
    o_nsa = nsa_attention(nsa_bound, q_t, gates_t, kvc[0], _t(kvc[1]), k_sw, v_sw_t,
                          dt[:NSA_HEADS], dc[:NSA_HEADS], bsz, seq)

    dq = MLA_NOPE + MLA_ROPE
    wq = w_uq.reshape(MLA_Q_RANK, MLA_HEADS, dq)
    wq_r = jnp.concatenate([wq[:, :, :MLA_NOPE].reshape(MLA_Q_RANK, -1),
                            wq[:, :, MLA_NOPE:].reshape(MLA_Q_RANK, -1)], axis=1).astype(BF16)
    cos, sin = _rope_tables(seq, MLA_ROPE)
    mscale = dq ** -0.5 * LOG2E
    side = [jnp.sqrt(MLA_NOPE * jnp.max(jnp.abs(nope_g[i])) ** 2 + MLA_ROPE * jnp.max(jnp.abs(rope_g[i])) ** 2)
            for i in range(2)]
    mla_bound = mscale * side[0] * side[1] * (1.0 + 2.0 ** -7)
    mla_shift = jnp.where(mla_bound <= SAFE_LOG2_BOUND, mla_bound, 0.0)
    q_mla_t, k_mla, v_mla_t = mla_project(proj, s_cq, s_ckv, s_tail, seq, q_norm_g, kv_norm_g, wq_r,
                                          w_ukv.astype(BF16), nope_g, rope_g, cos, sin, mscale, mla_shift)
    o_mla = mla_attention(mla_bound, q_mla_t, k_mla, v_mla_t, bsz, seq)
    w_o = w_out.astype(BF16)
    return resproj([(o_nsa, w_o[:nq_cols]), (o_mla, w_o[nq_cols:])], x2, gate, seq)


def _odd_mixer(h, x2, gate, dt, bias_bound, bsz, seq, w_in, w_out, qk_g):
    nq = DSA_HEADS * HEAD_DIM
    nkv = DSA_KV_HEADS * HEAD_DIM
    niq = IDX_HEADS * IDX_DIM
    o_k, o_v, o_iq = nq, nq + nkv, nq + 2 * nkv
    q_t = proj_heads(h, w_in[:, :o_k].astype(BF16), qk_g[0] * (HEAD_DIM ** -0.5 * LOG2E), transpose=True)
    k = proj_heads(h, w_in[:, o_k:o_v].astype(BF16), qk_g[1], transpose=False)
    v_t = proj_heads(h, w_in[:, o_v:o_iq].astype(BF16), transpose=True)
    w_idx = w_in[:, o_iq:]
    proj = proj_slabs(h, _pad_cols(w_idx, niq + LANE).astype(BF16), tn=niq + LANE)
    iq_t, ik, iw_t = indexer_operands(proj, seq)
    bound = _logit_bound(qk_g[0], qk_g[1], HEAD_DIM, HEAD_DIM ** -0.5 * LOG2E) + bias_bound
    o = dsa_attention(bound, iq_t, iw_t, ik, q_t, k, v_t, dt, bsz, seq)
    return resproj([(o, w_out.astype(BF16))], x2, gate, seq)


def _conv_ffn(h, x2, gate, seq, w_up, conv_w, conv_b, w_down):
    a = ffn_up(h, w_up, conv_w, conv_b, seq)
    return resproj([(a, w_down.astype(BF16))], x2, gate, seq)


def kernel(x, c, rel_bias, ada_w, ada_b, norm_g, ev_w_in, ev_w_out, nsa_qk_g, cmp_pe, cmp_w1, cmp_b1, cmp_w2, cmp_b2, mla_q_norm_g, mla_kv_norm_g, mla_w_uq, mla_w_ukv, mla_nope_g, mla_rope_g, od_w_in, od_w_out, dsa_qk_g, ffn_w_up, ffn_conv_w, ffn_conv_b, ffn_w_down):
    bsz, seq, d = x.shape
    depth = ada_w.shape[0]
    x2 = x.reshape(bsz * seq, d)
    mods = ada_all(c, ada_w, ada_b)
    dt, dc = bias_tiles(rel_bias)
    bias_bound = 2.0 * LOG2E * jnp.max(jnp.abs(rel_bias))
    for i in range(depth):
        j = i // 2
        shift, scale, gate = jnp.split(mods[i, 0], 3, axis=-1)
        h = modnorm(x2, norm_g[i, 0], scale, shift, seq)
        if i % 2 == 0:
            x2 = _even_mixer(h, x2, gate, dt, dc, bias_bound, bsz, seq, ev_w_in[j], ev_w_out[j], nsa_qk_g[j],
                             cmp_pe[j], cmp_w1[j], cmp_b1[j], cmp_w2[j], cmp_b2[j], mla_q_norm_g[j],
                             mla_kv_norm_g[j], mla_w_uq[j], mla_w_ukv[j], mla_nope_g[j], mla_rope_g[j])
        else:
            x2 = _odd_mixer(h, x2, gate, dt, bias_bound, bsz, seq, od_w_in[j], od_w_out[j], dsa_qk_g[j])
        shift, scale, gate = jnp.split(mods[i, 1], 3, axis=-1)
        h = modnorm(x2, norm_g[i, 1], scale, shift, seq)
        x2 = _conv_ffn(h, x2, gate, seq, ffn_w_up[i], ffn_conv_w[i], ffn_conv_b[i], ffn_w_down[i])
    return x2.reshape(bsz, seq, d)
```

```python
import functools
import math

import numpy as np
import jax
import jax.numpy as jnp
from jax import lax
from jax.experimental import pallas as pl
from jax.experimental.pallas import tpu as pltpu

HEAD_DIM = 128
NSA_HEADS = 8
NSA_GROUPS = 2
NSA_HPG = NSA_HEADS // NSA_GROUPS
CMP_BLOCK = 32
CMP_STRIDE = 16
CMP_HIDDEN = 256
SEL_BLOCK = 64
SEL_TOP_N = 16
WINDOW = 512
MLA_HEADS = 8
MLA_Q_RANK = 512
MLA_KV_RANK = 256
MLA_NOPE = 128
MLA_ROPE = 64
MLA_V = 128
DSA_HEADS = 16
DSA_KV_HEADS = 4
DSA_HPG = DSA_HEADS // DSA_KV_HEADS
IDX_HEADS = 16
IDX_DIM = 64
IDX_ROPE = 32
DSA_TOPK_MAX = 256
REL_BUCKETS = 32
REL_MAX_DIST = 128
CONV_WIDTH = 3
ROPE_THETA = 10000.0
EPS = 1e-6
NEG = -1e30
FORCE = 1e9

LANE = 128
SUBLANE = 8
QB = 128
VMEM_LIMIT = 56 * 1024 * 1024

F32 = jnp.float32
BF16 = jnp.bfloat16


def _t5_thresholds():
    d = np.arange(0, 4 * REL_MAX_DIST)
    half = REL_BUCKETS // 2
    val = np.log(np.maximum(d, 1) / half) / math.log(REL_MAX_DIST / half) * (REL_BUCKETS - half)
    large = np.minimum(half + np.floor(np.maximum(val, 0.0)).astype(np.int64), REL_BUCKETS - 1)
    bucket = np.where(d < half, d, large)
    return [int(np.argmax(bucket >= b)) for b in range(1, REL_BUCKETS)]


T5_THR = _t5_thresholds()
T5_FAR = T5_THR[-1]
assert T5_FAR <= LANE


def _cparams(sem):
    return pltpu.CompilerParams(dimension_semantics=sem, vmem_limit_bytes=VMEM_LIMIT)


def _dot(a, b):
    return jnp.dot(a, b, preferred_element_type=F32)


def _ada_kernel(c_ref, w_ref, b_ref, o_ref):
    c = c_ref[...]
    a = c * jax.nn.sigmoid(c)
    o_ref[0] = jnp.dot(a, w_ref[0], preferred_element_type=F32,
                       precision=lax.Precision.HIGHEST) + b_ref[0]


def ada_all(c, ada_w, ada_b):
    depth, two, d, n3 = ada_w.shape
    bsz = c.shape[0]
    rows = SUBLANE
    cp = jnp.zeros((rows, d), F32).at[:bsz].set(c)
    w = ada_w.reshape(depth * two, d, n3)
    b = ada_b.reshape(depth * two, 1, n3)
    tn = 512
    out = pl.pallas_call(
        _ada_kernel,
        grid=(depth * two, n3 // tn),
        in_specs=[pl.BlockSpec((rows, d), lambda l, j: (0, 0)),
                  pl.BlockSpec((1, d, tn), lambda l, j: (l, 0, j)),
                  pl.BlockSpec((1, 1, tn), lambda l, j: (l, 0, j))],
        out_specs=pl.BlockSpec((1, rows, tn), lambda l, j: (l, 0, j)),
        out_shape=jax.ShapeDtypeStruct((depth * two, rows, n3), F32),
        compiler_params=_cparams(("arbitrary", "arbitrary")),
        name="ada_mod",
    )(cp, w, b)
    return out[:, :bsz].reshape(depth, two, bsz, n3)


def _modnorm_kernel(x_ref, g_ref, sc_ref, sh_ref, o_ref):
    x = x_ref[...]
    y = x * lax.rsqrt(jnp.mean(x * x, axis=-1, keepdims=True) + EPS)
    h = (y * g_ref[...]) * (1.0 + sc_ref[0]) + sh_ref[0]
    o_ref[...] = h.astype(o_ref.dtype)


def modnorm(x2, g, scale, shift, seq):
    m, d = x2.shape
    tm = 512
    tpb = seq // tm
    return pl.pallas_call(
        _modnorm_kernel,
        grid=(m // tm,),
        in_specs=[pl.BlockSpec((tm, d), lambda i: (i, 0)),
                  pl.BlockSpec((1, d), lambda i: (0, 0)),
                  pl.BlockSpec((1, 1, d), lambda i: (i // tpb, 0, 0)),
                  pl.BlockSpec((1, 1, d), lambda i: (i // tpb, 0, 0))],
        out_specs=pl.BlockSpec((tm, d), lambda i: (i, 0)),
        out_shape=jax.ShapeDtypeStruct((m, d), BF16),
        compiler_params=_cparams(("arbitrary",)),
        name="modnorm",
    )(x2, g.reshape(1, d), scale.reshape(-1, 1, d), shift.reshape(-1, 1, d))


def _proj_kernel(x_ref, w_ref, o_ref, *, nslab):
    acc = _dot(x_ref[...], w_ref[...])
    for s in range(nslab):
        o_ref[s] = acc[:, s * LANE:(s + 1) * LANE]


def proj_slabs(x, w, tm=1024, tn=384):
    m, k = x.shape
    n = w.shape[1]
    assert n % tn == 0 and m % tm == 0
    nslab = tn // LANE
    return pl.pallas_call(
        functools.partial(_proj_kernel, nslab=nslab),
        grid=(m // tm, n // tn),
        in_specs=[pl.BlockSpec((tm, k), lambda i, j: (i, 0)),
                  pl.BlockSpec((k, tn), lambda i, j: (0, j))],
        out_specs=pl.BlockSpec((nslab, tm, LANE), lambda i, j: (j, i, 0)),
        out_shape=jax.ShapeDtypeStruct((n // LANE, m, LANE), F32),
        compiler_params=_cparams(("arbitrary", "arbitrary")),
        name="proj_slabs",
    )(x, w)


def _proj_heads_kernel(x_ref, w_ref, g_ref, o_ref, *, nslab, norm, transpose):
    acc = _dot(x_ref[...], w_ref[...])
    for s in range(nslab):
        y = acc[:, s * LANE:(s + 1) * LANE]
        if norm:
            y = y * lax.rsqrt(jnp.mean(y * y, axis=-1, keepdims=True) + EPS) * g_ref[...]
        o_ref[s] = (y.T if transpose else y).astype(o_ref.dtype)


def proj_heads(x, w, g=None, *, transpose, tm=1024, tn=1024):
    m, k = x.shape
    n = w.shape[1]
    tn = min(tn, n)
    assert n % tn == 0 and m % tm == 0
    nslab = tn // LANE
    norm = g is not None
    if transpose:
        out_spec = pl.BlockSpec((nslab, LANE, tm), lambda i, j: (j, 0, i))
        out_shape = jax.ShapeDtypeStruct((n // LANE, LANE, m), BF16)
    else:
        out_spec = pl.BlockSpec((nslab, tm, LANE), lambda i, j: (j, i, 0))
        out_shape = jax.ShapeDtypeStruct((n // LANE, m, LANE), BF16)
    g2 = (g if norm else jnp.ones((LANE,), F32)).reshape(1, LANE)
    return pl.pallas_call(
        functools.partial(_proj_heads_kernel, nslab=nslab, norm=norm, transpose=transpose),
        grid=(m // tm, n // tn),
        in_specs=[pl.BlockSpec((tm, k), lambda i, j: (i, 0)),
                  pl.BlockSpec((k, tn), lambda i, j: (0, j)),
                  pl.BlockSpec((1, LANE), lambda i, j: (0, 0))],
        out_specs=out_spec,
        out_shape=out_shape,
        compiler_params=_cparams(("arbitrary", "arbitrary")),
        name="proj_heads",
    )(x, w, g2)


def _rms_rows(x, g):
    return x * lax.rsqrt(jnp.mean(x * x, axis=-1, keepdims=True) + EPS) * g


def _rope_rows(x, cos, sin):
    half = x.shape[-1] // 2
    x1, x2 = x[:, :half], x[:, half:]
    return jnp.concatenate([x1 * cos - x2 * sin, x1 * sin + x2 * cos], axis=1)


def _latent(x_ref, g_ref):
    x = jnp.concatenate([x_ref[s] for s in range(x_ref.shape[0])], axis=1)
    return _rms_rows(x, g_ref[...]).astype(BF16)


def _mla_q_kernel(shift_ref, x_ref, g_ref, w_ref, gn_ref, gr_ref, cos_ref, sin_ref, o_ref, *, scale):
    acc = _dot(_latent(x_ref, g_ref), w_ref[...])
    tm = acc.shape[0]
    cos, sin = cos_ref[...], sin_ref[...]
    first = lax.broadcasted_iota(jnp.int32, (tm, LANE - MLA_ROPE), 1) == 0
    pad = jnp.where(first, -shift_ref[0], 0.0)
    for h in range(MLA_HEADS):
        nope = _rms_rows(acc[:, h * MLA_NOPE:(h + 1) * MLA_NOPE], gn_ref[...]) * scale
        r0 = MLA_HEADS * MLA_NOPE + h * MLA_ROPE
        pe = _rope_rows(_rms_rows(acc[:, r0:r0 + MLA_ROPE], gr_ref[...]), cos, sin) * scale
        o_ref[h, 0:MLA_NOPE, :] = nope.T.astype(o_ref.dtype)
        o_ref[h, MLA_NOPE:MLA_NOPE + LANE, :] = jnp.concatenate([pe, pad], axis=1).T.astype(o_ref.dtype)


def _mla_kv_kernel(x_ref, g_ref, w_ref, tail_ref, gn_ref, gr_ref, cos_ref, sin_ref, ok_ref, ov_ref):
    acc = _dot(_latent(x_ref, g_ref), w_ref[...])
    tm = acc.shape[0]
    k_pe = _rope_rows(_rms_rows(tail_ref[0][:, :MLA_ROPE], gr_ref[...]), cos_ref[...], sin_ref[...])
    first = lax.broadcasted_iota(jnp.int32, (tm, LANE - MLA_ROPE), 1) == 0
    k_pe = jnp.concatenate([k_pe, jnp.where(first, 1.0, 0.0)], axis=1).astype(ok_ref.dtype)
    for h in range(MLA_HEADS):
        c0 = h * (MLA_NOPE + MLA_V)
        ok_ref[h, :, 0:MLA_NOPE] = _rms_rows(acc[:, c0:c0 + MLA_NOPE], gn_ref[...]).astype(ok_ref.dtype)
        ok_ref[h, :, MLA_NOPE:MLA_NOPE + LANE] = k_pe
        ov_ref[h] = acc[:, c0 + MLA_NOPE:c0 + MLA_NOPE + MLA_V].T.astype(ov_ref.dtype)


def mla_project(proj, s_cq, s_ckv, s_tail, seq, q_norm_g, kv_norm_g, wq_r, w_ukv, nope_g, rope_g, cos, sin,
                scale, shift, tm=512):
    _, m, _ = proj.shape
    kq, kkv = s_ckv - s_cq, s_tail - s_ckv
    tps = seq // tm
    dqk = MLA_NOPE + LANE
    half = MLA_ROPE // 2
    rope_specs = [pl.BlockSpec((tm, half), lambda i: (i % tps, 0))] * 2
    gain_specs = [pl.BlockSpec((1, MLA_NOPE), lambda i: (0, 0)), pl.BlockSpec((1, MLA_ROPE), lambda i: (0, 0))]
    q_t = pl.pallas_call(
        functools.partial(_mla_q_kernel, scale=scale),
        grid=(m // tm,),
        in_specs=[pl.BlockSpec(memory_space=pltpu.SMEM),
                  pl.BlockSpec((kq, tm, LANE), lambda i: (s_cq // kq, i, 0)),
                  pl.BlockSpec((1, kq * LANE), lambda i: (0, 0)),
                  pl.BlockSpec(wq_r.shape, lambda i: (0, 0))] + gain_specs + rope_specs,
        out_specs=pl.BlockSpec((MLA_HEADS, dqk, tm), lambda i: (0, 0, i)),
        out_shape=jax.ShapeDtypeStruct((MLA_HEADS, dqk, m), BF16),
        compiler_params=_cparams(("arbitrary",)),
        name="mla_q_project",
    )(jnp.reshape(shift, (1,)).astype(F32), proj, q_norm_g.reshape(1, -1), wq_r, nope_g[0].reshape(1, -1),
      rope_g[0].reshape(1, -1), cos, sin)
    k, v_t = pl.pallas_call(
        _mla_kv_kernel,
        grid=(m // tm,),
        in_specs=[pl.BlockSpec((kkv, tm, LANE), lambda i: (s_ckv // kkv, i, 0)),
                  pl.BlockSpec((1, kkv * LANE), lambda i: (0, 0)),
                  pl.BlockSpec(w_ukv.shape, lambda i: (0, 0)),
                  pl.BlockSpec((1, tm, LANE), lambda i: (s_tail, i, 0))] + gain_specs + rope_specs,
        out_specs=[pl.BlockSpec((MLA_HEADS, tm, dqk), lambda i: (0, i, 0)),
                   pl.BlockSpec((MLA_HEADS, MLA_V, tm), lambda i: (0, 0, i))],
        out_shape=[jax.ShapeDtypeStruct((MLA_HEADS, m, dqk), BF16),
                   jax.ShapeDtypeStruct((MLA_HEADS, MLA_V, m), BF16)],
        compiler_params=_cparams(("arbitrary",)),
        name="mla_kv_project",
    )(proj, kv_norm_g.reshape(1, -1), w_ukv, proj, nope_g[1].reshape(1, -1), rope_g[1].reshape(1, -1), cos, sin)
    return q_t, k, v_t


def _resproj_kernel(*refs, npair):
    xres_ref, gate_ref = refs[2 * npair], refs[2 * npair + 1]
    o_ref = refs[2 * npair + 2]
    acc = _dot(refs[0][...], refs[1][...])
    for p in range(1, npair):
        acc = acc + _dot(refs[2 * p][...], refs[2 * p + 1][...])
    o_ref[...] = xres_ref[...] + gate_ref[0] * acc


def resproj(pairs, xres, gate, seq, tm=1024, tn=512):
    m, n = xres.shape
    tpb = seq // tm
    in_specs, args = [], []
    for x, w in pairs:
        k = x.shape[1]
        in_specs += [pl.BlockSpec((tm, k), lambda i, j: (i, 0)),
                     pl.BlockSpec((k, tn), lambda i, j: (0, j))]
        args += [x, w]
    in_specs += [pl.BlockSpec((tm, tn), lambda i, j: (i, j)),
                 pl.BlockSpec((1, 1, tn), lambda i, j: (i // tpb, 0, j))]
    args += [xres, gate.reshape(-1, 1, n)]
    return pl.pallas_call(
        functools.partial(_resproj_kernel, npair=len(pairs)),
        grid=(m // tm, n // tn),
        in_specs=in_specs,
        out_specs=pl.BlockSpec((tm, tn), lambda i, j: (i, j)),
        out_shape=jax.ShapeDtypeStruct((m, n), F32),
        compiler_params=_cparams(("arbitrary", "arbitrary")),
        name="resproj",
    )(*args)


HALO = 8


def _ffn_up_kernel(h_ref, wg32_ref, wv32_ref, cwg_ref, cwv_ref, cbg_ref, cbv_ref, o_ref,
                   ug_ref, uv_ref, wg_ref, wv_ref, *, tm, tiles_per_seq):
    i = pl.program_id(1)
    first = (i % tiles_per_seq) == 0

    @pl.when(i == 0)
    def _():
        wg_ref[...] = wg32_ref[...].astype(wg_ref.dtype)
        wv_ref[...] = wv32_ref[...].astype(wv_ref.dtype)

    @pl.when(first)
    def _():
        ug_ref[0:HALO, :] = jnp.zeros((HALO, ug_ref.shape[1]), F32)
        uv_ref[0:HALO, :] = jnp.zeros((HALO, uv_ref.shape[1]), F32)

    @pl.when(jnp.logical_not(first))
    def _():
        ug_ref[0:HALO, :] = ug_ref[tm:tm + HALO, :]
        uv_ref[0:HALO, :] = uv_ref[tm:tm + HALO, :]

    h = h_ref[...]
    ug_ref[HALO:HALO + tm, :] = _dot(h, wg_ref[...])
    uv_ref[HALO:HALO + tm, :] = _dot(h, wv_ref[...])

    def conv(u_ref, cw_ref, cb_ref):
        out = cb_ref[...]
        for j in range(CONV_WIDTH):
            off = HALO - (CONV_WIDTH - 1) + j
            out = out + cw_ref[j:j + 1, :] * u_ref[off:off + tm, :]
        return out

    g = conv(ug_ref, cwg_ref, cbg_ref)
    v = conv(uv_ref, cwv_ref, cbv_ref)
    o_ref[...] = (g * jax.nn.sigmoid(g) * v).astype(o_ref.dtype)


def ffn_up(h, w_up_all, layer, conv_w, conv_b, seq, tm=1024, tn=512):
    m, d = h.shape
    f = w_up_all.shape[2] // 2
    nj = f // tn
    tps = seq // tm
    cb = conv_b.reshape(1, 2 * f)
    return pl.pallas_call(
        functools.partial(_ffn_up_kernel, tm=tm, tiles_per_seq=tps),
        grid=(nj, m // tm),
        in_specs=[pl.BlockSpec((tm, d), lambda j, i: (i, 0)),
                  pl.BlockSpec((None, d, tn), lambda j, i: (layer, 0, j)),
                  pl.BlockSpec((None, d, tn), lambda j, i: (layer, 0, nj + j)),
                  pl.BlockSpec((CONV_WIDTH, tn), lambda j, i: (0, j)),
                  pl.BlockSpec((CONV_WIDTH, tn), lambda j, i: (0, nj + j)),
                  pl.BlockSpec((1, tn), lambda j, i: (0, j)),
                  pl.BlockSpec((1, tn), lambda j, i: (0, nj + j))],
        out_specs=pl.BlockSpec((tm, tn), lambda j, i: (i, j)),
        out_shape=jax.ShapeDtypeStruct((m, f), BF16),
        scratch_shapes=[pltpu.VMEM((tm + HALO, tn), F32), pltpu.VMEM((tm + HALO, tn), F32),
                        pltpu.VMEM((d, tn), BF16), pltpu.VMEM((d, tn), BF16)],
        compiler_params=_cparams(("arbitrary", "arbitrary")),
        name="ffn_up_conv",
    )(h, w_up_all, w_up_all, conv_w, conv_w, cb, cb)


LOG2E = 1.4426950408889634
CWIN = 16


def _t5_shifted(dist, tbl_ref, h):
    val = jnp.full(dist.shape, tbl_ref[0, h], F32)
    for b in range(1, REL_BUCKETS):
        val = jnp.where(dist >= T5_THR[b - 1], tbl_ref[b, h], val)
    return (val - tbl_ref[REL_BUCKETS - 1, h]) * LOG2E


def _bias_tiles_kernel(tbl_ref, dt_ref, dc_ref):
    h = pl.program_id(0)
    key = lax.broadcasted_iota(jnp.int32, (LANE, LANE), 0)
    q = lax.broadcasted_iota(jnp.int32, (LANE, LANE), 1)
    for rel in range(2):
        dt_ref[0, rel] = _t5_shifted(rel * LANE + q - key, tbl_ref, h)
    dt_ref[0, 2] = jnp.zeros((LANE, LANE), F32)
    u = lax.broadcasted_iota(jnp.int32, (CWIN, LANE), 0)
    qc = lax.broadcasted_iota(jnp.int32, (CWIN, LANE), 1)
    dc_ref[0] = _t5_shifted(qc - CMP_STRIDE * (u - CWIN // 2) - (CMP_BLOCK - 1), tbl_ref, h)


def bias_tiles(rel_bias):
    nh = rel_bias.shape[1]
    return pl.pallas_call(
        _bias_tiles_kernel,
        grid=(nh,),
        in_specs=[pl.BlockSpec(memory_space=pltpu.SMEM)],
        out_specs=[pl.BlockSpec((1, 3, LANE, LANE), lambda h: (h, 0, 0, 0)),
                   pl.BlockSpec((1, CWIN, LANE), lambda h: (h, 0, 0))],
        out_shape=[jax.ShapeDtypeStruct((nh, 3, LANE, LANE), F32),
                   jax.ShapeDtypeStruct((nh, CWIN, LANE), F32)],
        compiler_params=_cparams(("arbitrary",)),
        name="t5_bias_tiles",
    )(rel_bias)


def _compress_kernel(x_ref, pe_ref, w1_ref, b1_ref, w2_ref, b2_ref, g_ref, o_ref, *, half):
    kv = pl.program_id(0)
    nchunk = x_ref.shape[1] // CMP_STRIDE
    a = jnp.zeros((nchunk, CMP_HIDDEN), F32)
    b = jnp.zeros((nchunk, CMP_HIDDEN), F32)
    for p in range(CMP_STRIDE):
        xp = x_ref[0, pl.ds(p, nchunk, stride=CMP_STRIDE), :]
        rows = slice(p * HEAD_DIM, (p + 1) * HEAD_DIM)
        a = a + _dot((xp + pe_ref[0, p:p + 1, :]).astype(BF16), w1_ref[0, rows, :])
        q = CMP_STRIDE + p
        b = b + _dot((xp + pe_ref[0, q:q + 1, :]).astype(BF16),
                     w1_ref[0, half + p * HEAD_DIM:half + (p + 1) * HEAD_DIM, :])
    b_next = jnp.concatenate([b[1:], jnp.zeros((1, b.shape[1]), F32)], axis=0)
    hid = jax.nn.gelu(a + b_next + b1_ref[0])
    out = _dot(hid.astype(BF16), w2_ref[0]) + b2_ref[0]
    normed = out * lax.rsqrt(jnp.mean(out * out, axis=-1, keepdims=True) + EPS) * g_ref[...]
    out = jnp.where(kv == 0, normed, out)
    o_ref[0, 0] = out.astype(o_ref.dtype)


def compress_kv(proj, slab0, bsz, seq, cmp_pe, cmp_w1, cmp_b1, cmp_w2, cmp_b2, g_k):
    nslab, m, _ = proj.shape
    nchunk = seq // CMP_STRIDE
    half = CMP_STRIDE * HEAD_DIM
    del nslab, m
    return pl.pallas_call(
        functools.partial(_compress_kernel, half=half),
        grid=(2, bsz, NSA_GROUPS),
        in_specs=[pl.BlockSpec((1, seq, HEAD_DIM), lambda kv, b, g: (slab0 + 2 * kv + g, b, 0)),
                  pl.BlockSpec((1, CMP_BLOCK, HEAD_DIM), lambda kv, b, g: (kv, 0, 0)),
                  pl.BlockSpec((1, 2 * half, CMP_HIDDEN), lambda kv, b, g: (kv, 0, 0)),
                  pl.BlockSpec((1, 1, CMP_HIDDEN), lambda kv, b, g: (kv, 0, 0)),
                  pl.BlockSpec((1, CMP_HIDDEN, HEAD_DIM), lambda kv, b, g: (kv, 0, 0)),
                  pl.BlockSpec((1, 1, HEAD_DIM), lambda kv, b, g: (kv, 0, 0)),
                  pl.BlockSpec((1, HEAD_DIM), lambda kv, b, g: (0, 0))],
        out_specs=pl.BlockSpec((1, 1, nchunk, HEAD_DIM), lambda kv, b, g: (kv, g, b, 0)),
        out_shape=jax.ShapeDtypeStruct((2, NSA_GROUPS, bsz * nchunk, HEAD_DIM), BF16),
        compiler_params=_cparams(("arbitrary", "arbitrary", "arbitrary")),
        name="nsa_compress",
    )(proj, cmp_pe, cmp_w1.astype(BF16), cmp_b1.reshape(2, 1, CMP_HIDDEN), cmp_w2.astype(BF16),
      cmp_b2.reshape(2, 1, HEAD_DIM), g_k.reshape(1, HEAD_DIM))


KW = 512
PV_KEYS = 256


def _tile_lanes(x, n):
    return jnp.concatenate([x] * n, axis=1)


def _flash_init(m_ref, l_ref, acc_ref):
    m_ref[...] = jnp.full(m_ref.shape, NEG, F32)
    l_ref[...] = jnp.zeros(l_ref.shape, F32)
    acc_ref[...] = jnp.zeros(acc_ref.shape, F32)


def _zero_after(x):
    bits = pltpu.bitcast(x, jnp.int32)
    return lax.shift_right_logical(lax.shift_right_logical(bits, 16), 16).astype(F32)


def _flash_update(s, v_t, m_ref, l_ref, acc_ref, col_max=None, after=None):
    m_old = m_ref[...]
    if col_max is None:
        col_max = jnp.max(s, axis=0, keepdims=True)
    m_new = jnp.maximum(m_old, col_max)
    alpha = jnp.exp2(m_old - m_new)
    l_new = alpha * l_ref[...]
    acc = alpha * acc_ref[...]
    nk = s.shape[0]
    for k0 in range(0, nk, PV_KEYS):
        p = jnp.exp2(s[k0:k0 + PV_KEYS] - m_new)
        l_new = l_new + jnp.sum(p, axis=0, keepdims=True)
        acc = acc + _dot(v_t[:, k0:k0 + PV_KEYS], p.astype(BF16))
    l_ref[...] = l_new
    acc_ref[...] = acc
    m_ref[...] = m_new if after is None else m_new + _zero_after(after)


SAFE_LOG2_BOUND = 60.0


def _flash_accumulate(s, v_t, l_ref, acc_ref, after=None):
    l_new = l_ref[...]
    acc = acc_ref[...]
    for k0 in range(0, s.shape[0], PV_KEYS):
        p = jnp.exp2(s[k0:k0 + PV_KEYS])
        l_new = l_new + jnp.sum(p, axis=0, keepdims=True)
        acc = acc + _dot(v_t[:, k0:k0 + PV_KEYS], p.astype(BF16))
    if after is not None:
        l_new = l_new + jnp.max(_zero_after(after), axis=0, keepdims=True)
    l_ref[...] = l_new
    acc_ref[...] = acc


def _sum_result(l_ref, acc_ref):
    den = l_ref[...]
    ok = den > 0.0
    return acc_ref[...] * jnp.where(ok, 1.0 / jnp.where(ok, den, 1.0), 0.0)


def _inv_den(m, den):
    ok = m > 0.5 * NEG
    return jnp.where(ok, 1.0 / jnp.where(ok, den, 1.0), 0.0)


def _flash_result(m_ref, l_ref, acc_ref):
    return acc_ref[...] * _inv_den(m_ref[...], l_ref[...])


def _softmax_cols(s):
    m = jnp.max(s, axis=0, keepdims=True)
    p = jnp.exp2(s - m)
    return p * _inv_den(m, jnp.sum(p, axis=0, keepdims=True))


def _near_bias(dt_ref, heads, qi, kt0, ntile):
    rows = []
    for j in range(ntile):
        rel = jnp.clip(qi - (kt0 + j), 0, 2)
        rows.append(jnp.concatenate([dt_ref[h, rel] for h in heads], axis=1))
    return jnp.concatenate(rows, axis=0)


def _pipelined_chunks(n, qk_stage, soft_stage):
    @pl.when(n > 0)
    def _():
        qk_stage(0, 0)

    def pair(p, x):
        c = 2 * p
        ahead = qk_stage(c + 1, 1)
        soft_stage(c, 0, ahead)
        ahead = qk_stage(jnp.minimum(c + 2, n - 1), 0)
        soft_stage(c + 1, 1, ahead)
        return x

    lax.fori_loop(0, n // 2, pair, 0)

    @pl.when(n % 2 == 1)
    def _():
        soft_stage(n - 1, 0, None)


NSA_STATE = 9
GATE_ROWS = -(-3 * NSA_HPG // SUBLANE) * SUBLANE


def _nsa_kernel(bound_ref, qt_ref, gt_ref, kc_ref, vct_ref, ks_ref, vst_ref, kw_ref, vwt_ref,
                dt_ref, dc_ref, ext_ref, o_ref, *scratch, seq, nc):
    ng = NSA_GROUPS
    state = [scratch[NSA_STATE * g:NSA_STATE * (g + 1)] for g in range(ng)]
    qi = pl.program_id(1)
    q0 = qi * QB
    hpg = NSA_HPG
    ncp = kc_ref.shape[1]
    ns = seq // SEL_BLOCK
    group_heads = [[g * hpg + h for h in range(hpg)] for g in range(ng)]
    q_ts = [jnp.concatenate([qt_ref[h] for h in group_heads[g]], axis=1) for g in range(ng)]
    pad = CWIN // 2
    wkeys = WINDOW + QB
    start = pl.multiple_of(jnp.maximum(q0 - WINDOW, 0), LANE)
    r0 = pl.multiple_of(qi * (QB // CMP_STRIDE), 8)

    s_w = []
    for g in range(ng):
        sc_ref = state[g][0]
        sc_ref[0:pad, :] = jnp.zeros((pad, hpg * QB), F32)
        sc_ref[pad + ncp:2 * pad + ncp, :] = jnp.zeros((pad, hpg * QB), F32)
        sc_ref[pad:pad + ncp, :] = _dot(kc_ref[g], q_ts[g])
        s_w.append(_dot(kw_ref[g, pl.ds(start, wkeys), :], q_ts[g]))

    ci = lax.broadcasted_iota(jnp.int32, (ncp, QB), 0)
    tc = q0 + lax.broadcasted_iota(jnp.int32, (ncp, QB), 1)
    valid_c = (ci * CMP_STRIDE + CMP_BLOCK - 1 <= tc) & (ci < nc)
    madd_c = _tile_lanes(jnp.where(valid_c, 0.0, NEG), hpg)
    oc_t, p_sum = [], []
    for g in range(ng):
        sc_ref = state[g][0]
        sc_ref[pl.ds(r0, CWIN), :] = sc_ref[pl.ds(r0, CWIN), :] + jnp.concatenate(
            [dc_ref[h] for h in group_heads[g]], axis=1)
        p_c = _softmax_cols(sc_ref[pad:pad + ncp, :] + madd_c)
        oc_t.append(_dot(vct_ref[g], p_c.astype(BF16)))
        ps = p_c[:, 0:QB]
        for h in range(1, hpg):
            ps = ps + p_c[:, h * QB:(h + 1) * QB]
        p_sum.append(ps)

    dist_w = (q0 + lax.broadcasted_iota(jnp.int32, (wkeys, QB), 1)) - (
        start + lax.broadcasted_iota(jnp.int32, (wkeys, QB), 0))
    madd_w = _tile_lanes(jnp.where((dist_w >= 0) & (dist_w < WINDOW), 0.0, NEG), hpg)
    ow_t = []
    for g in range(ng):
        p_w = _softmax_cols(s_w[g] + _near_bias(dt_ref, group_heads[g], qi, start // LANE, wkeys // LANE) + madd_w)
        ow_t.append(_dot(vwt_ref[g, :, pl.ds(start, wkeys)], p_w.astype(BF16)))

    per = SEL_BLOCK // CMP_STRIDE
    blk = lax.broadcasted_iota(jnp.int32, (LANE, QB), 0)
    t = q0 + lax.broadcasted_iota(jnp.int32, (LANE, QB), 1)
    tb = t // SEL_BLOCK
    forced = (blk == 0) | (blk == tb) | (blk == tb - 1)
    blk_f = blk.astype(F32)
    scores = []
    for g in range(ng):
        ps_ref = state[g][1]
        ps_ref[0:SUBLANE, :] = jnp.zeros((SUBLANE, QB), F32)
        ps_ref[SUBLANE:SUBLANE + ncp, :] = p_sum[g]
        band = [ps_ref[pl.ds(SUBLANE + r, ns, stride=per), :] for r in range(-1, per)]
        imp = 0.5 * band[0] + band[1] + band[2] + band[3] + 0.5 * band[4]
        if ns < LANE:
            imp = jnp.concatenate([imp, jnp.zeros((LANE - ns, QB), F32)], axis=0)
        score = jnp.where(forced, FORCE, jnp.where(blk * SEL_BLOCK <= t, imp, NEG))
        scores.append(jnp.where(blk < ns, score, -jnp.inf))
    sels = [jnp.zeros((LANE, QB), F32) for _ in range(ng)]
    for _ in range(min(SEL_TOP_N, ns)):
        for g in range(ng):
            mx = jnp.max(scores[g], axis=0, keepdims=True)
            first = jnp.min(jnp.where(scores[g] == mx, blk_f, float(LANE)), axis=0, keepdims=True)
            pick = blk_f == first
            sels[g] = jnp.where(pick, 1.0, sels[g])
            scores[g] = jnp.where(pick, -jnp.inf, scores[g])
    sel_b = [s.astype(BF16) for s in sels]

    kpos = lax.broadcasted_iota(jnp.int32, (KW, QB), 0)
    tq = q0 + lax.broadcasted_iota(jnp.int32, (KW, QB), 1)
    bounded_ok = bound_ref[0] <= SAFE_LOG2_BOUND
    shift = jnp.where(bounded_ok, bound_ref[0], 0.0)
    c_near = jnp.maximum(qi - 1, 0) // (KW // LANE)

    def scores_of(g, c0):
        chosen = _dot(ext_ref[pl.ds(c0, KW), :], sel_b[g])
        return (chosen - 1.0) * (-NEG) - shift, _dot(ks_ref[g, pl.ds(c0, KW), :], q_ts[g])

    def attend(bounded):
        for g in range(ng):
            _flash_init(*state[g][2:5])

        def qk_stage(c, buf):
            c0 = pl.multiple_of(c * KW, KW)
            ahead = []
            for g in range(ng):
                madd, s = scores_of(g, c0)
                s = s + _tile_lanes(madd, hpg)
                state[g][5 + buf][...] = s
                if bounded:
                    ahead.append(s[KW - 8:KW])
                else:
                    ahead.append(jnp.max(s, axis=0, keepdims=True))
                    state[g][7 + buf][...] = ahead[-1]
            return ahead

        def soft_stage(c, buf, ahead):
            c0 = pl.multiple_of(c * KW, KW)
            for g in range(ng):
                m_ref, l_ref, acc_ref = state[g][2:5]
                after = None if ahead is None else ahead[g]
                v_t = vst_ref[g, :, pl.ds(c0, KW)]
                if bounded:
                    _flash_accumulate(state[g][5 + buf][...], v_t, l_ref, acc_ref, after=after)
                else:
                    _flash_update(state[g][5 + buf][...], v_t, m_ref, l_ref, acc_ref,
                                  col_max=state[g][7 + buf][...], after=after)

        def near_step(c, x):
            c0 = pl.multiple_of(c * KW, KW)
            causal = jnp.where(c0 + kpos <= tq, 0.0, NEG)
            scores = []
            for g in range(ng):
                madd, s = scores_of(g, c0)
                scores.append(s + _tile_lanes(madd + causal, hpg)
                              + _near_bias(dt_ref, group_heads[g], qi, c * (KW // LANE), KW // LANE))
            for g in range(ng):
                m_ref, l_ref, acc_ref = state[g][2:5]
                if bounded:
                    _flash_accumulate(scores[g], vst_ref[g, :, pl.ds(c0, KW)], l_ref, acc_ref)
                else:
                    _flash_update(scores[g], vst_ref[g, :, pl.ds(c0, KW)], m_ref, l_ref, acc_ref)
            return x

        _pipelined_chunks(c_near, qk_stage, soft_stage)
        lax.fori_loop(c_near, qi // (KW // LANE) + 1, near_step, 0)
        for g in range(ng):
            m_ref, l_ref, acc_ref = state[g][2:5]
            acc_ref[...] = _sum_result(l_ref, acc_ref) if bounded else _flash_result(m_ref, l_ref, acc_ref)

    pl.when(bounded_ok)(lambda: attend(True))
    pl.when(jnp.logical_not(bounded_ok))(lambda: attend(False))

    for g in range(ng):
        os_t = state[g][4][...]
        gates = jax.nn.sigmoid(gt_ref[g])
        for h in range(hpg):
            sl = slice(h * QB, (h + 1) * QB)
            o_t = (gates[3 * h:3 * h + 1] * oc_t[g][:, sl] + gates[3 * h + 1:3 * h + 2] * os_t[:, sl]
                   + gates[3 * h + 2:3 * h + 3] * ow_t[g][:, sl])
            hh = group_heads[g][h]
            o_ref[:, hh * HEAD_DIM:(hh + 1) * HEAD_DIM] = o_t.T.astype(o_ref.dtype)


def nsa_attention(logit_bound, q_t, gates_t, kc, vc_t, k_sw, v_sw_t, dt, dc, bsz, seq):
    nq = seq // QB
    ncp = seq // CMP_STRIDE
    nc = ncp - 1
    ns = seq // SEL_BLOCK
    assert ns <= LANE and seq >= WINDOW + QB and seq % KW == 0
    assert CMP_BLOCK == 2 * CMP_STRIDE and SEL_BLOCK == 4 * CMP_STRIDE
    expand =((np.arange(seq)[:, None] // SEL_BLOCK) == np.arange(LANE)[None, :]).astype(np.float32)
    ng = NSA_GROUPS
    once = dict(pipeline_mode=pl.Buffered(1))
    ks_spec = pl.BlockSpec((ng, seq, HEAD_DIM), lambda b, i: (0, b, 0), **once)
    kw_spec = pl.BlockSpec((ng, seq, HEAD_DIM), lambda b, i: (1, b, 0), **once)
    vs_spec = pl.BlockSpec((ng, HEAD_DIM, seq), lambda b, i: (0, 0, b), **once)
    vw_spec = pl.BlockSpec((ng, HEAD_DIM, seq), lambda b, i: (1, 0, b), **once)
    lanes = NSA_HPG * QB
    group_state = [pltpu.VMEM((ncp + CWIN, lanes), F32), pltpu.VMEM((ncp + SUBLANE, QB), F32),
                   pltpu.VMEM((1, lanes), F32), pltpu.VMEM((1, lanes), F32), pltpu.VMEM((HEAD_DIM, lanes), F32),
                   pltpu.VMEM((KW, lanes), F32), pltpu.VMEM((KW, lanes), F32),
                   pltpu.VMEM((1, lanes), F32), pltpu.VMEM((1, lanes), F32)]
    assert len(group_state) == NSA_STATE
    return pl.pallas_call(
        functools.partial(_nsa_kernel, seq=seq, nc=nc),
        grid=(bsz, nq),
        in_specs=[pl.BlockSpec(memory_space=pltpu.SMEM),
                  pl.BlockSpec((NSA_HEADS, HEAD_DIM, QB), lambda b, i: (0, 0, b * nq + i)),
                  pl.BlockSpec((ng, GATE_ROWS, QB), lambda b, i: (0, 0, b * nq + i)),
                  pl.BlockSpec((ng, ncp, HEAD_DIM), lambda b, i: (0, b, 0)),
                  pl.BlockSpec((ng, HEAD_DIM, ncp), lambda b, i: (0, 0, b)),
                  ks_spec, vs_spec, kw_spec, vw_spec,
                  pl.BlockSpec((NSA_HEADS, 3, LANE, LANE), lambda b, i: (0, 0, 0, 0)),
                  pl.BlockSpec((NSA_HEADS, CWIN, LANE), lambda b, i: (0, 0, 0)),
                  pl.BlockSpec((seq, LANE), lambda b, i: (0, 0))],
        out_specs=pl.BlockSpec((QB, NSA_HEADS * HEAD_DIM), lambda b, i: (b * nq + i, 0)),
        out_shape=jax.ShapeDtypeStruct((bsz * seq, NSA_HEADS * HEAD_DIM), BF16),
        scratch_shapes=group_state * ng,
        compiler_params=_cparams(("arbitrary", "arbitrary")),
        name="nsa_attention",
    )(jnp.reshape(logit_bound, (1,)).astype(F32), q_t, gates_t, kc, vc_t, k_sw, v_sw_t, k_sw, v_sw_t, dt, dc,
      jnp.asarray(expand, BF16))


MLA_HPS = 2


def _mla_kernel(bound_ref, qt_ref, k_ref, vt_ref, o_ref, *scratch):
    qi = pl.program_id(2)
    chains = [scratch[3 * h:3 * h + 3] for h in range(MLA_HPS)]
    sbuf = [scratch[(3 + b) * MLA_HPS:(4 + b) * MLA_HPS] for b in range(2)]
    cbuf = [scratch[(5 + b) * MLA_HPS:(6 + b) * MLA_HPS] for b in range(2)]
    c_diag = pl.multiple_of(qi * KW, KW)
    kpos = lax.broadcasted_iota(jnp.int32, (KW, KW), 0)
    tq = lax.broadcasted_iota(jnp.int32, (KW, KW), 1)
    dv = vt_ref.shape[1]

    def attend(bounded):
        for ch in chains:
            _flash_init(*ch)

        def qk_stage(c, buf):
            c0 = pl.multiple_of(c * KW, KW)
            ahead = []
            for h in range(MLA_HPS):
                s = _dot(k_ref[h, pl.ds(c0, KW), :], qt_ref[h])
                sbuf[buf][h][...] = s
                if bounded:
                    ahead.append(s[KW - 8:KW])
                else:
                    ahead.append(jnp.max(s, axis=0, keepdims=True))
                    cbuf[buf][h][...] = ahead[-1]
            return ahead

        def soft_stage(c, buf, ahead):
            c0 = pl.multiple_of(c * KW, KW)
            for h, (m_ref, l_ref, acc_ref) in enumerate(chains):
                after = None if ahead is None else ahead[h]
                if bounded:
                    _flash_accumulate(sbuf[buf][h][...], vt_ref[h, :, pl.ds(c0, KW)], l_ref, acc_ref, after=after)
                else:
                    _flash_update(sbuf[buf][h][...], vt_ref[h, :, pl.ds(c0, KW)], m_ref, l_ref, acc_ref,
                                  col_max=cbuf[buf][h][...], after=after)

        _pipelined_chunks(qi, qk_stage, soft_stage)
        causal = jnp.where(kpos <= tq, 0.0, NEG)
        scores = [_dot(k_ref[h, pl.ds(c_diag, KW), :], qt_ref[h]) + causal for h in range(MLA_HPS)]
        for h, (m_ref, l_ref, acc_ref) in enumerate(chains):
            if bounded:
                _flash_accumulate(scores[h], vt_ref[h, :, pl.ds(c_diag, KW)], l_ref, acc_ref)
                o_t = _sum_result(l_ref, acc_ref)
            else:
                _flash_update(scores[h], vt_ref[h, :, pl.ds(c_diag, KW)], m_ref, l_ref, acc_ref)
                o_t = _flash_result(m_ref, l_ref, acc_ref)
            o_ref[:, h * dv:(h + 1) * dv] = o_t.T.astype(o_ref.dtype)

    bounded_ok = bound_ref[0] <= SAFE_LOG2_BOUND
    pl.when(bounded_ok)(lambda: attend(True))
    pl.when(jnp.logical_not(bounded_ok))(lambda: attend(False))


def mla_attention(logit_bound, q_t, k, v_t, bsz, seq):
    nh, dqk, _ = q_t.shape
    dv = v_t.shape[1]
    nq = seq // KW
    hps = MLA_HPS
    state = [pltpu.VMEM((1, KW), F32), pltpu.VMEM((1, KW), F32), pltpu.VMEM((dv, KW), F32)] * hps
    state += [pltpu.VMEM((KW, KW), F32)] * (2 * hps)
    state += [pltpu.VMEM((1, KW), F32)] * (2 * hps)
    return pl.pallas_call(
        _mla_kernel,
        grid=(bsz, nh // hps, nq),
        in_specs=[pl.BlockSpec(memory_space=pltpu.SMEM),
                  pl.BlockSpec((hps, dqk, KW), lambda b, h, i: (h, 0, b * nq + i)),
                  pl.BlockSpec((hps, seq, dqk), lambda b, h, i: (h, b, 0)),
                  pl.BlockSpec((hps, dv, seq), lambda b, h, i: (h, 0, b))],
        out_specs=pl.BlockSpec((KW, hps * dv), lambda b, h, i: (b * nq + i, h)),
        out_shape=jax.ShapeDtypeStruct((bsz * seq, nh * dv), BF16),
        scratch_shapes=state,
        compiler_params=_cparams(("arbitrary", "arbitrary", "arbitrary")),
        name="mla_attention",
    )(jnp.reshape(logit_bound, (1,)).astype(F32), q_t, k, v_t)


INT_MIN = -2 ** 31
NEG_KEY = int(np.array(NEG, np.float32).view(np.int32)) ^ 0x7FFFFFFF
KEY_BITS = 32
SURE_BITS = 20


def _sort_key(x):
    bits = pltpu.bitcast(x + 0.0, jnp.int32)
    return jnp.where(bits < 0, bits ^ 0x7FFFFFFF, bits)


def _dsa_kernel(bound_ref, iqt_ref, iwt_ref, ik_ref, qt_ref, k_ref, vt_ref, dt_ref, o_ref,
                key_ref, madd_ref, *state, seq, k_sel):
    qi = pl.program_id(1)
    q0 = qi * QB
    n_chunk = (q0 + QB + KW - 1) // KW
    n_rest = seq - n_chunk * KW
    kpos = lax.broadcasted_iota(jnp.int32, (KW, QB), 0)
    tq = q0 + lax.broadcasted_iota(jnp.int32, (KW, QB), 1)
    hpp = KW // QB

    def score_chunk(c, x):
        c0 = pl.multiple_of(c * KW, KW)
        ikc = ik_ref[pl.ds(c0, KW), :]
        acc = jnp.zeros((KW, QB), F32)
        for piece in range(IDX_HEADS // hpp):
            sl = slice(piece * KW, (piece + 1) * KW)
            s = jnp.maximum(_dot(ikc, iqt_ref[0, :, sl]), 0.0) * iwt_ref[0, :, sl]
            for j in range(hpp):
                acc = acc + s[:, j * QB:(j + 1) * QB]
        acc = jnp.where(c0 + kpos <= tq, acc, NEG)
        key_ref[pl.ds(c0, KW), :] = _sort_key(acc)
        return x

    lax.fori_loop(0, n_chunk, score_chunk, 0)

    def count(pred):
        def body(c, acc):
            c0 = pl.multiple_of(c * KW, KW)
            hit = jnp.where(pred(key_ref[pl.ds(c0, KW), :], c0), 1.0, 0.0)
            parts = [hit[SUBLANE * i:SUBLANE * (i + 1)] for i in range(KW // SUBLANE)]
            while len(parts) > 1:
                parts = [parts[i] + parts[i + 1] for i in range(0, len(parts), 2)]
            return acc + parts[0]
        acc = lax.fori_loop(0, n_chunk, body, jnp.zeros((SUBLANE, QB), F32))
        return jnp.sum(acc, axis=0, keepdims=True)

    rest = n_rest.astype(F32)
    kf = float(k_sel)

    def bit_step(i, st):
        u, thr_s, settled = st
        bit = jnp.left_shift(jnp.int32(1), KEY_BITS - 1 - i)
        trial = (u | bit) ^ INT_MIN
        cnt = count(lambda keys, c0: keys >= trial) + jnp.where(NEG_KEY >= trial, rest, 0.0)
        new = (cnt == kf) & (settled < 0.5)
        return (jnp.where(cnt >= kf, u | bit, u), jnp.where(new, trial, thr_s), jnp.where(new, 1.0, settled))

    st = (jnp.zeros((1, QB), jnp.int32), jnp.zeros((1, QB), jnp.int32), jnp.zeros((1, QB), F32))
    st = lax.fori_loop(0, SURE_BITS, bit_step, st)
    _, (u, thr_s, settled) = lax.while_loop(
        lambda c: (c[0] < KEY_BITS) & (jnp.min(c[1][2]) < 0.5),
        lambda c: (c[0] + 1, bit_step(c[0], c[1])), (jnp.int32(SURE_BITS), st))
    is_settled = settled > 0.5
    thr = jnp.where(is_settled, thr_s, u ^ INT_MIN)

    def edge_counts():
        return (count(lambda keys, c0: keys > thr) + jnp.where(NEG_KEY > thr, rest, 0.0),
                count(lambda keys, c0: keys >= thr) + jnp.where(NEG_KEY >= thr, rest, 0.0))

    zero_cnt = jnp.zeros((1, QB), F32)
    cnt_gt, cnt_ge = lax.cond(jnp.min(settled) > 0.5, lambda: (zero_cnt, zero_cnt), edge_counts)
    need = kf - cnt_gt
    tie_q = (cnt_ge > kf) & (thr != NEG_KEY) & jnp.logical_not(is_settled)
    idx_bits = (seq - 1).bit_length()
    no_cut = 2 ** 30

    def tie_cut():
        def idx_step(i, x):
            bit = jnp.left_shift(jnp.int32(1), idx_bits - 1 - i)
            trial = x | bit
            f = count(lambda keys, c0: (keys == thr) & (c0 + kpos < trial))
            return jnp.where(f <= need - 1.0, trial, x)
        return lax.fori_loop(0, idx_bits, idx_step, jnp.zeros((1, QB), jnp.int32))

    any_tie = jnp.max(jnp.where(tie_q, 1.0, 0.0)) > 0.0
    x_cut = lax.cond(any_tie, tie_cut, lambda: jnp.full((1, QB), no_cut, jnp.int32))
    x_cut = jnp.where(tie_q, x_cut, no_cut)

    def mask_chunk(c, x):
        c0 = pl.multiple_of(c * KW, KW)
        keys = key_ref[pl.ds(c0, KW), :]
        pos = c0 + kpos
        chosen = (keys > thr) | ((keys == thr) & (pos <= x_cut))
        madd_ref[pl.ds(c0, KW), :] = jnp.where(chosen & (pos <= tq), -shift, NEG)
        return x

    bounded_ok = bound_ref[0] <= SAFE_LOG2_BOUND
    shift = jnp.where(bounded_ok, bound_ref[0], 0.0)
    lax.fori_loop(0, n_chunk, mask_chunk, 0)

    c_near = jnp.maximum(qi - 1, 0) // (KW // LANE)
    ng = DSA_KV_HEADS
    chains = [state[3 * g:3 * g + 3] for g in range(ng)]
    sbuf = [state[(3 + b) * ng:(4 + b) * ng] for b in range(2)]
    cbuf = [state[(5 + b) * ng:(6 + b) * ng] for b in range(2)]
    group_heads = [[g * DSA_HPG + h for h in range(DSA_HPG)] for g in range(ng)]

    def raw_scores(c0, g):
        q_t = jnp.concatenate([qt_ref[h] for h in group_heads[g]], axis=1)
        return _dot(k_ref[g, pl.ds(c0, KW), :], q_t)

    def attend(bounded):
        for ch in chains:
            _flash_init(*ch)

        def qk_stage(c, buf):
            c0 = pl.multiple_of(c * KW, KW)
            madd = _tile_lanes(madd_ref[pl.ds(c0, KW), :], DSA_HPG)
            ahead = []
            for g in range(ng):
                s = raw_scores(c0, g) + madd
                sbuf[buf][g][...] = s
                if bounded:
                    ahead.append(s[KW - 8:KW])
                else:
                    ahead.append(jnp.max(s, axis=0, keepdims=True))
                    cbuf[buf][g][...] = ahead[-1]
            return ahead

        def soft_stage(c, buf, ahead):
            c0 = pl.multiple_of(c * KW, KW)
            for g, (m_ref, l_ref, acc_ref) in enumerate(chains):
                after = None if ahead is None else ahead[g]
                if bounded:
                    _flash_accumulate(sbuf[buf][g][...], vt_ref[g, :, pl.ds(c0, KW)], l_ref, acc_ref, after=after)
                else:
                    _flash_update(sbuf[buf][g][...], vt_ref[g, :, pl.ds(c0, KW)], m_ref, l_ref, acc_ref,
                                  col_max=cbuf[buf][g][...], after=after)

        _pipelined_chunks(c_near, qk_stage, soft_stage)

        def near_step(c, x):
            c0 = pl.multiple_of(c * KW, KW)
            madd = _tile_lanes(madd_ref[pl.ds(c0, KW), :], DSA_HPG)
            scores = [raw_scores(c0, g) + madd
                      + _near_bias(dt_ref, group_heads[g], qi, c * (KW // LANE), KW // LANE)
                      for g in range(ng)]
            for g, (m_ref, l_ref, acc_ref) in enumerate(chains):
                if bounded:
                    _flash_accumulate(scores[g], vt_ref[g, :, pl.ds(c0, KW)], l_ref, acc_ref)
                else:
                    _flash_update(scores[g], vt_ref[g, :, pl.ds(c0, KW)], m_ref, l_ref, acc_ref)
            return x

        lax.fori_loop(c_near, n_chunk, near_step, 0)
        for g, (m_ref, l_ref, acc_ref) in enumerate(chains):
            o_t = _sum_result(l_ref, acc_ref) if bounded else _flash_result(m_ref, l_ref, acc_ref)
            for h in range(DSA_HPG):
                hh = group_heads[g][h]
                o_ref[:, hh * HEAD_DIM:(hh + 1) * HEAD_DIM] = o_t[:, h * QB:(h + 1) * QB].T.astype(o_ref.dtype)

    pl.when(bounded_ok)(lambda: attend(True))
    pl.when(jnp.logical_not(bounded_ok))(lambda: attend(False))


def _idx_prep_kernel(p_ref, c_ref, sa_ref, sb_ref, iqt_ref, ik_ref, iwt_ref, *, ntile):
    nslab_q = IDX_HEADS * IDX_DIM // LANE
    per = LANE // IDX_DIM
    half = IDX_ROPE // 2
    zrows = jnp.zeros((LANE - IDX_DIM, QB), F32)

    def rope_slab(x, c, sa, sb):
        return x * c + pltpu.roll(x, LANE - half, axis=1) * sa + pltpu.roll(x, half, axis=1) * sb

    for t in range(ntile):
        rows = slice(t * QB, (t + 1) * QB)
        c, sa, sb = c_ref[rows, :], sa_ref[rows, :], sb_ref[rows, :]
        cols = []
        for s in range(nslab_q):
            x_t = (rope_slab(p_ref[s, rows, :], c, sa, sb) * IDX_DIM ** -0.5).T
            for j in range(per):
                cols.append(jnp.concatenate([x_t[j * IDX_DIM:(j + 1) * IDX_DIM], zrows], axis=0))
        iqt_ref[t] = jnp.concatenate(cols, axis=1).astype(iqt_ref.dtype)
        tail = p_ref[nslab_q, rows, :]
        lane = lax.broadcasted_iota(jnp.int32, (QB, LANE), 1)
        ik_ref[rows, :] = jnp.where(lane < IDX_DIM, rope_slab(tail, c, sa, sb), 0.0).astype(ik_ref.dtype)
        w_t = (tail * IDX_HEADS ** -0.5).T
        iwt_ref[t] = jnp.concatenate([w_t[IDX_DIM + h:IDX_DIM + h + 1, :] for h in range(IDX_HEADS)], axis=1)


def indexer_operands(proj, seq, tm=512):
    _, m, _ = proj.shape
    ntile = tm // QB
    tps = seq // tm
    cos, sin = _rope_tables(seq, IDX_ROPE)
    zero = jnp.zeros_like(sin)
    rest = IDX_DIM - IDX_ROPE
    per = LANE // IDX_DIM
    c_tab = jnp.tile(jnp.concatenate([cos, cos, jnp.ones((seq, rest), F32)], axis=1), (1, per))
    sa_tab = jnp.tile(jnp.concatenate([-sin, zero, jnp.zeros((seq, rest), F32)], axis=1), (1, per))
    sb_tab = jnp.tile(jnp.concatenate([zero, sin, jnp.zeros((seq, rest), F32)], axis=1), (1, per))
    lanes = IDX_HEADS * QB
    tab_spec = pl.BlockSpec((tm, LANE), lambda i: (i % tps, 0))
    return pl.pallas_call(
        functools.partial(_idx_prep_kernel, ntile=ntile),
        grid=(m // tm,),
        in_specs=[pl.BlockSpec((proj.shape[0], tm, LANE), lambda i: (0, i, 0)), tab_spec, tab_spec, tab_spec],
        out_specs=[pl.BlockSpec((ntile, LANE, lanes), lambda i: (i, 0, 0)),
                   pl.BlockSpec((tm, LANE), lambda i: (i, 0)),
                   pl.BlockSpec((ntile, 1, lanes), lambda i: (i, 0, 0))],
        out_shape=[jax.ShapeDtypeStruct((m // QB, LANE, lanes), BF16),
                   jax.ShapeDtypeStruct((m, LANE), BF16),
                   jax.ShapeDtypeStruct((m // QB, 1, lanes), F32)],
        compiler_params=_cparams(("arbitrary",)),
        name="dsa_indexer_operands",
    )(proj, c_tab, sa_tab, sb_tab)


def dsa_attention(logit_bound, iq_t, iw_t, ik, q_t, k, v_t, dt, bsz, seq):
    nq = seq // QB
    k_sel = min(DSA_TOPK_MAX, seq // 4)
    assert seq % KW == 0
    lanes = DSA_HPG * QB
    return pl.pallas_call(
        functools.partial(_dsa_kernel, seq=seq, k_sel=k_sel),
        grid=(bsz, nq),
        in_specs=[pl.BlockSpec(memory_space=pltpu.SMEM),
                  pl.BlockSpec((1, LANE, IDX_HEADS * QB), lambda b, i: (b * nq + i, 0, 0)),
                  pl.BlockSpec((1, 1, IDX_HEADS * QB), lambda b, i: (b * nq + i, 0, 0)),
                  pl.BlockSpec((seq, LANE), lambda b, i: (b, 0)),
                  pl.BlockSpec((DSA_HEADS, HEAD_DIM, QB), lambda b, i: (0, 0, b * nq + i)),
                  pl.BlockSpec((DSA_KV_HEADS, seq, HEAD_DIM), lambda b, i: (0, b, 0),
                               pipeline_mode=pl.Buffered(1)),
                  pl.BlockSpec((DSA_KV_HEADS, HEAD_DIM, seq), lambda b, i: (0, 0, b),
                               pipeline_mode=pl.Buffered(1)),
                  pl.BlockSpec((DSA_HEADS, 3, LANE, LANE), lambda b, i: (0, 0, 0, 0),
                               pipeline_mode=pl.Buffered(1))],
        out_specs=pl.BlockSpec((QB, DSA_HEADS * HEAD_DIM), lambda b, i: (b * nq + i, 0)),
        out_shape=jax.ShapeDtypeStruct((bsz * seq, DSA_HEADS * HEAD_DIM), BF16),
        scratch_shapes=[pltpu.VMEM((seq, QB), jnp.int32), pltpu.VMEM((seq, QB), F32)]
        + [pltpu.VMEM((1, lanes), F32), pltpu.VMEM((1, lanes), F32),
           pltpu.VMEM((HEAD_DIM, lanes), F32)] * DSA_KV_HEADS
        + [pltpu.VMEM((KW, lanes), F32)] * (2 * DSA_KV_HEADS)
        + [pltpu.VMEM((1, lanes), F32)] * (2 * DSA_KV_HEADS),
        compiler_params=_cparams(("arbitrary", "arbitrary")),
        name="dsa_attention",
    )(jnp.reshape(logit_bound, (1,)).astype(F32), iq_t, iw_t, ik, q_t, k, v_t, dt)


def _rope_tables(seq, dim):
    half = dim // 2
    inv = ROPE_THETA ** (-jnp.arange(half, dtype=F32) / half)
    ang = jnp.arange(seq, dtype=F32)[:, None] * inv[None, :]
    return jnp.cos(ang), jnp.sin(ang)


def _logit_bound(gq, gk, dim, scale):
    return dim * scale * jnp.max(jnp.abs(gq)) * jnp.max(jnp.abs(gk)) * (1.0 + 2.0 ** -7)


def _pad_cols(w, n):
    return jnp.pad(w, ((0, 0), (0, n - w.shape[1])))


def _t(x):
    return jnp.swapaxes(x, -1, -2)


def _even_mixer(h, x2, gate, dt, dc, bias_bound, bsz, seq, w_in, w_out, nsa_qk_g, cmp_pe, cmp_w1, cmp_b1,
                cmp_w2, cmp_b2, q_norm_g, kv_norm_g, w_uq, w_ukv, nope_g, rope_g):
    m = bsz * seq
    nq_cols = NSA_HEADS * HEAD_DIM
    nkv_cols = 6 * NSA_GROUPS * HEAD_DIM
    ngate = 3 * NSA_HEADS
    o_gate = nq_cols + nkv_cols
    o_cq = o_gate + ngate
    o_ckv = o_cq + MLA_Q_RANK
    o_kpe = o_ckv + MLA_KV_RANK
    gw = NSA_GROUPS * HEAD_DIM
    kvw = [w_in[:, nq_cols + i * gw:nq_cols + (i + 1) * gw] for i in range(6)]
    scale = HEAD_DIM ** -0.5 * LOG2E
    q_t = proj_heads(h, w_in[:, :nq_cols].astype(BF16), nsa_qk_g[0] * scale, transpose=True)
    k_sw = proj_heads(h, jnp.concatenate([kvw[2], kvw[4]], axis=1).astype(BF16), nsa_qk_g[1],
                      transpose=False)
    v_sw_t = proj_heads(h, jnp.concatenate([kvw[3], kvw[5]], axis=1).astype(BF16), transpose=True)
    tail = jnp.concatenate([w_in[:, o_kpe:], w_in[:, o_gate:o_cq]], axis=1)
    w_r = jnp.concatenate([kvw[0], kvw[1], w_in[:, o_cq:o_kpe], _pad_cols(tail, LANE)], axis=1).astype(BF16)
    proj = proj_slabs(h, w_r, tn=w_r.shape[1])
    s_cq = 2 * NSA_GROUPS
    s_ckv = s_cq + MLA_Q_RANK // LANE
    s_tail = s_ckv + MLA_KV_RANK // LANE
    kvc = compress_kv(proj, 0, bsz, seq, cmp_pe, cmp_w1, cmp_b1, cmp_w2, cmp_b2, nsa_qk_g[1])
    tail_v = proj[s_tail]
    gates = tail_v[:, MLA_ROPE:MLA_ROPE + ngate].reshape(m, NSA_GROUPS, 3 * NSA_HPG)
    gates_t = jnp.pad(jnp.transpose(gates, (1, 2, 0)), ((0, 0), (0, GATE_ROWS - 3 * NSA_HPG), (0, 0)))
    nsa_bound = _logit_bound(nsa_qk_g[0], nsa_qk_g[1], HEAD_DIM, scale) + bias_bound
    o_nsa = nsa_attention(nsa_bound, q_t, gates_t, kvc[0], _t(kvc[1]), k_sw, v_sw_t,
                          dt[:NSA_HEADS], dc[:NSA_HEADS], bsz, seq)

    dq = MLA_NOPE + MLA_ROPE
    wq = w_uq.reshape(MLA_Q_RANK, MLA_HEADS, dq)
    wq_r = jnp.concatenate([wq[:, :, :MLA_NOPE].reshape(MLA_Q_RANK, -1),
                            wq[:, :, MLA_NOPE:].reshape(MLA_Q_RANK, -1)], axis=1).astype(BF16)
    cos, sin = _rope_tables(seq, MLA_ROPE)
    mscale = dq ** -0.5 * LOG2E
    side = [jnp.sqrt(MLA_NOPE * jnp.max(jnp.abs(nope_g[i])) ** 2 + MLA_ROPE * jnp.max(jnp.abs(rope_g[i])) ** 2)
            for i in range(2)]
    mla_bound = mscale * side[0] * side[1] * (1.0 + 2.0 ** -7)
    mla_shift = jnp.where(mla_bound <= SAFE_LOG2_BOUND, mla_bound, 0.0)
    q_mla_t, k_mla, v_mla_t = mla_project(proj, s_cq, s_ckv, s_tail, seq, q_norm_g, kv_norm_g, wq_r,
                                          w_ukv.astype(BF16), nope_g, rope_g, cos, sin, mscale, mla_shift)
    o_mla = mla_attention(mla_bound, q_mla_t, k_mla, v_mla_t, bsz, seq)
    w_o = w_out.astype(BF16)
    return resproj([(o_nsa, w_o[:nq_cols]), (o_mla, w_o[nq_cols:])], x2, gate, seq)


def _odd_mixer(h, x2, gate, dt, bias_bound, bsz, seq, w_in, w_out, qk_g):
    nq = DSA_HEADS * HEAD_DIM
    nkv = DSA_KV_HEADS * HEAD_DIM
    niq = IDX_HEADS * IDX_DIM
    o_k, o_v, o_iq = nq, nq + nkv, nq + 2 * nkv
    q_t = proj_heads(h, w_in[:, :o_k].astype(BF16), qk_g[0] * (HEAD_DIM ** -0.5 * LOG2E), transpose=True)
    k = proj_heads(h, w_in[:, o_k:o_v].astype(BF16), qk_g[1], transpose=False)
    v_t = proj_heads(h, w_in[:, o_v:o_iq].astype(BF16), transpose=True)
    w_idx = w_in[:, o_iq:]
    proj = proj_slabs(h, _pad_cols(w_idx, niq + LANE).astype(BF16), tn=niq + LANE)
    iq_t, ik, iw_t = indexer_operands(proj, seq)
    bound = _logit_bound(qk_g[0], qk_g[1], HEAD_DIM, HEAD_DIM ** -0.5 * LOG2E) + bias_bound
    o = dsa_attention(bound, iq_t, iw_t, ik, q_t, k, v_t, dt, bsz, seq)
    return resproj([(o, w_out.astype(BF16))], x2, gate, seq)


def _conv_ffn(h, x2, gate, seq, w_up_all, layer, conv_w, conv_b, w_down):
    a = ffn_up(h, w_up_all, layer, conv_w, conv_b, seq)
    return resproj([(a, w_down.astype(BF16))], x2, gate, seq)


def kernel(x, c, rel_bias, ada_w, ada_b, norm_g, ev_w_in, ev_w_out, nsa_qk_g, cmp_pe, cmp_w1, cmp_b1, cmp_w2, cmp_b2, mla_q_norm_g, mla_kv_norm_g, mla_w_uq, mla_w_ukv, mla_nope_g, mla_rope_g, od_w_in, od_w_out, dsa_qk_g, ffn_w_up, ffn_conv_w, ffn_conv_b, ffn_w_down):
    bsz, seq, d = x.shape
    depth = ada_w.shape[0]
    x2 = x.reshape(bsz * seq, d)
    mods = ada_all(c, ada_w, ada_b)
    dt, dc = bias_tiles(rel_bias)
    bias_bound = 2.0 * LOG2E * jnp.max(jnp.abs(rel_bias))
    for i in range(depth):
        j = i // 2
        shift, scale, gate = jnp.split(mods[i, 0], 3, axis=-1)
        h = modnorm(x2, norm_g[i, 0], scale, shift, seq)
        if i % 2 == 0:
            x2 = _even_mixer(h, x2, gate, dt, dc, bias_bound, bsz, seq, ev_w_in[j], ev_w_out[j], nsa_qk_g[j],
                             cmp_pe[j], cmp_w1[j], cmp_b1[j], cmp_w2[j], cmp_b2[j], mla_q_norm_g[j],
                             mla_kv_norm_g[j], mla_w_uq[j], mla_w_ukv[j], mla_nope_g[j], mla_rope_g[j])
        else:
            x2 = _odd_mixer(h, x2, gate, dt, bias_bound, bsz, seq, od_w_in[j], od_w_out[j], dsa_qk_g[j])
        shift, scale, gate = jnp.split(mods[i, 1], 3, axis=-1)
        h = modnorm(x2, norm_g[i, 1], scale, shift, seq)
        x2 = _conv_ffn(h, x2, gate, seq, ffn_w_up, i, ffn_conv_w[i], ffn_conv_b[i], ffn_w_down[i])
    return x2.reshape(bsz, seq, d)
```

```python
import functools
import math

import numpy as np
import jax
import jax.numpy as jnp
from jax import lax
from jax.experimental import pallas as pl
from jax.experimental.pallas import tpu as pltpu

HEAD_DIM = 128
NSA_HEADS = 8
NSA_GROUPS = 2
NSA_HPG = NSA_HEADS // NSA_GROUPS
CMP_BLOCK = 32
CMP_STRIDE = 16
CMP_HIDDEN = 256
SEL_BLOCK = 64
SEL_TOP_N = 16
WINDOW = 512
MLA_HEADS = 8
MLA_Q_RANK = 512
MLA_KV_RANK = 256
MLA_NOPE = 128
MLA_ROPE = 64
MLA_V = 128
DSA_HEADS = 16
DSA_KV_HEADS = 4
DSA_HPG = DSA_HEADS // DSA_KV_HEADS
IDX_HEADS = 16
IDX_DIM = 64
IDX_ROPE = 32
DSA_TOPK_MAX = 256
REL_BUCKETS = 32
REL_MAX_DIST = 128
CONV_WIDTH = 3
ROPE_THETA = 10000.0
EPS = 1e-6
NEG = -1e30
FORCE = 1e9

LANE = 128
SUBLANE = 8
QB = 128
VMEM_LIMIT = 56 * 1024 * 1024

F32 = jnp.float32
BF16 = jnp.bfloat16


def _t5_thresholds():
    d = np.arange(0, 4 * REL_MAX_DIST)
    half = REL_BUCKETS // 2
    val = np.log(np.maximum(d, 1) / half) / math.log(REL_MAX_DIST / half) * (REL_BUCKETS - half)
    large = np.minimum(half + np.floor(np.maximum(val, 0.0)).astype(np.int64), REL_BUCKETS - 1)
    bucket = np.where(d < half, d, large)
    return [int(np.argmax(bucket >= b)) for b in range(1, REL_BUCKETS)]


T5_THR = _t5_thresholds()
T5_FAR = T5_THR[-1]
assert T5_FAR <= LANE


def _cparams(sem):
    return pltpu.CompilerParams(dimension_semantics=sem, vmem_limit_bytes=VMEM_LIMIT)


def _dot(a, b):
    return jnp.dot(a, b, preferred_element_type=F32)


def _ada_kernel(c_ref, w_ref, b_ref, o_ref):
    c = c_ref[...]
    a = c * jax.nn.sigmoid(c)
    o_ref[0] = jnp.dot(a, w_ref[0], preferred_element_type=F32,
                       precision=lax.Precision.HIGHEST) + b_ref[0]


def ada_all(c, ada_w, ada_b):
    depth, two, d, n3 = ada_w.shape
    bsz = c.shape[0]
    rows = SUBLANE
    cp = jnp.zeros((rows, d), F32).at[:bsz].set(c)
    w = ada_w.reshape(depth * two, d, n3)
    b = ada_b.reshape(depth * two, 1, n3)
    tn = 512
    out = pl.pallas_call(
        _ada_kernel,
        grid=(depth * two, n3 // tn),
        in_specs=[pl.BlockSpec((rows, d), lambda l, j: (0, 0)),
                  pl.BlockSpec((1, d, tn), lambda l, j: (l, 0, j)),
                  pl.BlockSpec((1, 1, tn), lambda l, j: (l, 0, j))],
        out_specs=pl.BlockSpec((1, rows, tn), lambda l, j: (l, 0, j)),
        out_shape=jax.ShapeDtypeStruct((depth * two, rows, n3), F32),
        compiler_params=_cparams(("arbitrary", "arbitrary")),
        name="ada_mod",
    )(cp, w, b)
    return out[:, :bsz].reshape(depth, two, bsz, n3)


def _modnorm_kernel(x_ref, g_ref, sc_ref, sh_ref, o_ref):
    x = x_ref[...]
    y = x * lax.rsqrt(jnp.mean(x * x, axis=-1, keepdims=True) + EPS)
    h = (y * g_ref[...]) * (1.0 + sc_ref[0]) + sh_ref[0]
    o_ref[...] = h.astype(o_ref.dtype)


def modnorm(x2, g, scale, shift, seq, tm=1024):
    m, d = x2.shape
    tpb = seq // tm
    return pl.pallas_call(
        _modnorm_kernel,
        grid=(m // tm,),
        in_specs=[pl.BlockSpec((tm, d), lambda i: (i, 0)),
                  pl.BlockSpec((1, d), lambda i: (0, 0)),
                  pl.BlockSpec((1, 1, d), lambda i: (i // tpb, 0, 0)),
                  pl.BlockSpec((1, 1, d), lambda i: (i // tpb, 0, 0))],
        out_specs=pl.BlockSpec((tm, d), lambda i: (i, 0)),
        out_shape=jax.ShapeDtypeStruct((m, d), BF16),
        compiler_params=_cparams(("arbitrary",)),
        name="modnorm",
    )(x2, g.reshape(1, d), scale.reshape(-1, 1, d), shift.reshape(-1, 1, d))


def _proj_kernel(x_ref, w_ref, o_ref, *, nslab):
    acc = _dot(x_ref[...], w_ref[...])
    for s in range(nslab):
        o_ref[s] = acc[:, s * LANE:(s + 1) * LANE]


def proj_slabs(x, w, tm=1024, tn=384):
    m, k = x.shape
    n = w.shape[1]
    assert n % tn == 0 and m % tm == 0
    nslab = tn // LANE
    return pl.pallas_call(
        functools.partial(_proj_kernel, nslab=nslab),
        grid=(m // tm, n // tn),
        in_specs=[pl.BlockSpec((tm, k), lambda i, j: (i, 0)),
                  pl.BlockSpec((k, tn), lambda i, j: (0, j))],
        out_specs=pl.BlockSpec((nslab, tm, LANE), lambda i, j: (j, i, 0)),
        out_shape=jax.ShapeDtypeStruct((n // LANE, m, LANE), F32),
        compiler_params=_cparams(("arbitrary", "arbitrary")),
        name="proj_slabs",
    )(x, w)


def _proj_heads_kernel(x_ref, w_ref, g_ref, o_ref, *, nslab, norm, transpose):
    acc = _dot(x_ref[...], w_ref[...])
    for s in range(nslab):
        y = acc[:, s * LANE:(s + 1) * LANE]
        if norm:
            y = y * lax.rsqrt(jnp.mean(y * y, axis=-1, keepdims=True) + EPS) * g_ref[...]
        o_ref[s] = (y.T if transpose else y).astype(o_ref.dtype)


def proj_heads(x, w, g=None, *, transpose, tm=1024, tn=1024):
    m, k = x.shape
    n = w.shape[1]
    tn = min(tn, n)
    assert n % tn == 0 and m % tm == 0
    nslab = tn // LANE
    norm = g is not None
    if transpose:
        out_spec = pl.BlockSpec((nslab, LANE, tm), lambda i, j: (j, 0, i))
        out_shape = jax.ShapeDtypeStruct((n // LANE, LANE, m), BF16)
    else:
        out_spec = pl.BlockSpec((nslab, tm, LANE), lambda i, j: (j, i, 0))
        out_shape = jax.ShapeDtypeStruct((n // LANE, m, LANE), BF16)
    g2 = (g if norm else jnp.ones((LANE,), F32)).reshape(1, LANE)
    return pl.pallas_call(
        functools.partial(_proj_heads_kernel, nslab=nslab, norm=norm, transpose=transpose),
        grid=(m // tm, n // tn),
        in_specs=[pl.BlockSpec((tm, k), lambda i, j: (i, 0)),
                  pl.BlockSpec((k, tn), lambda i, j: (0, j)),
                  pl.BlockSpec((1, LANE), lambda i, j: (0, 0))],
        out_specs=out_spec,
        out_shape=out_shape,
        compiler_params=_cparams(("arbitrary", "arbitrary")),
        name="proj_heads",
    )(x, w, g2)


def _rms_rows(x, g):
    return x * lax.rsqrt(jnp.mean(x * x, axis=-1, keepdims=True) + EPS) * g


def _rope_rows(x, cos, sin):
    half = x.shape[-1] // 2
    x1, x2 = x[:, :half], x[:, half:]
    return jnp.concatenate([x1 * cos - x2 * sin, x1 * sin + x2 * cos], axis=1)


def _latent(x_ref, g_ref):
    x = jnp.concatenate([x_ref[s] for s in range(x_ref.shape[0])], axis=1)
    return _rms_rows(x, g_ref[...]).astype(BF16)


def _mla_q_kernel(shift_ref, x_ref, g_ref, w_ref, gn_ref, gr2_ref, c_ref, sa_ref, sb_ref, o_ref, *, scale):
    acc = _dot(_latent(x_ref, g_ref), w_ref[...])
    tm = acc.shape[0]
    for h in range(MLA_HEADS):
        nope = _rms_rows(acc[:, h * MLA_NOPE:(h + 1) * MLA_NOPE], gn_ref[...]) * scale
        o_ref[h, 0:MLA_NOPE, :] = nope.T.astype(o_ref.dtype)
    first = lax.broadcasted_iota(jnp.int32, (LANE - MLA_ROPE, tm), 0) == 0
    pad_rows = jnp.where(first, -shift_ref[0], 0.0)
    low = lax.broadcasted_iota(jnp.int32, (tm, LANE), 1) < MLA_ROPE
    c, sa, sb = c_ref[...], sa_ref[...], sb_ref[...]
    half = MLA_ROPE // 2
    per = LANE // MLA_ROPE
    for s in range(MLA_HEADS // per):
        x = acc[:, MLA_HEADS * MLA_NOPE + s * LANE:MLA_HEADS * MLA_NOPE + (s + 1) * LANE]
        sq = x * x
        s_low = jnp.sum(jnp.where(low, sq, 0.0), axis=-1, keepdims=True)
        s_all = jnp.sum(sq, axis=-1, keepdims=True)
        inv = jnp.where(low, lax.rsqrt(s_low / MLA_ROPE + EPS), lax.rsqrt((s_all - s_low) / MLA_ROPE + EPS))
        y = x * inv * gr2_ref[...]
        roped = y * c + pltpu.roll(y, LANE - half, axis=1) * sa + pltpu.roll(y, half, axis=1) * sb
        x_t = (roped * scale).T
        for j in range(per):
            o_ref[per * s + j, MLA_NOPE:MLA_NOPE + LANE, :] = jnp.concatenate(
                [x_t[j * MLA_ROPE:(j + 1) * MLA_ROPE], pad_rows], axis=0).astype(o_ref.dtype)


def _mla_kv_kernel(x_ref, g_ref, w_ref, tail_ref, gn_ref, gr_ref, cos_ref, sin_ref, ok_ref, ov_ref):
    acc = _dot(_latent(x_ref, g_ref), w_ref[...])
    tm = acc.shape[0]
    k_pe = _rope_rows(_rms_rows(tail_ref[0][:, :MLA_ROPE], gr_ref[...]), cos_ref[...], sin_ref[...])
    first = lax.broadcasted_iota(jnp.int32, (tm, LANE - MLA_ROPE), 1) == 0
    k_pe = jnp.concatenate([k_pe, jnp.where(first, 1.0, 0.0)], axis=1).astype(ok_ref.dtype)
    for h in range(MLA_HEADS):
        c0 = h * (MLA_NOPE + MLA_V)
        ok_ref[h, :, 0:MLA_NOPE] = _rms_rows(acc[:, c0:c0 + MLA_NOPE], gn_ref[...]).astype(ok_ref.dtype)
        ok_ref[h, :, MLA_NOPE:MLA_NOPE + LANE] = k_pe
        ov_ref[h] = acc[:, c0 + MLA_NOPE:c0 + MLA_NOPE + MLA_V].T.astype(ov_ref.dtype)


def mla_project(proj, s_cq, s_ckv, s_tail, seq, q_norm_g, kv_norm_g, wq_r, w_ukv, nope_g, rope_g, cos, sin,
                scale, shift, tm=512):
    _, m, _ = proj.shape
    kq, kkv = s_ckv - s_cq, s_tail - s_ckv
    tps = seq // tm
    dqk = MLA_NOPE + LANE
    half = MLA_ROPE // 2
    rope_specs = [pl.BlockSpec((tm, half), lambda i: (i % tps, 0))] * 2
    gain_specs = [pl.BlockSpec((1, MLA_NOPE), lambda i: (0, 0)), pl.BlockSpec((1, MLA_ROPE), lambda i: (0, 0))]
    per = LANE // MLA_ROPE
    zero = jnp.zeros_like(sin)
    c_tab = jnp.tile(jnp.concatenate([cos, cos], axis=1), (1, per))
    sa_tab = jnp.tile(jnp.concatenate([-sin, zero], axis=1), (1, per))
    sb_tab = jnp.tile(jnp.concatenate([zero, sin], axis=1), (1, per))
    tab_spec = pl.BlockSpec((tm, LANE), lambda i: (i % tps, 0))
    q_t = pl.pallas_call(
        functools.partial(_mla_q_kernel, scale=scale),
        grid=(m // tm,),
        in_specs=[pl.BlockSpec(memory_space=pltpu.SMEM),
                  pl.BlockSpec((kq, tm, LANE), lambda i: (s_cq // kq, i, 0)),
                  pl.BlockSpec((1, kq * LANE), lambda i: (0, 0)),
                  pl.BlockSpec(wq_r.shape, lambda i: (0, 0)),
                  pl.BlockSpec((1, MLA_NOPE), lambda i: (0, 0)), pl.BlockSpec((1, LANE), lambda i: (0, 0)),
                  tab_spec, tab_spec, tab_spec],
        out_specs=pl.BlockSpec((MLA_HEADS, dqk, tm), lambda i: (0, 0, i)),
        out_shape=jax.ShapeDtypeStruct((MLA_HEADS, dqk, m), BF16),
        compiler_params=_cparams(("arbitrary",)),
        name="mla_q_project",
    )(jnp.reshape(shift, (1,)).astype(F32), proj, q_norm_g.reshape(1, -1), wq_r, nope_g[0].reshape(1, -1),
      jnp.tile(rope_g[0].reshape(1, -1), (1, per)), c_tab, sa_tab, sb_tab)
    k, v_t = pl.pallas_call(
        _mla_kv_kernel,
        grid=(m // tm,),
        in_specs=[pl.BlockSpec((kkv, tm, LANE), lambda i: (s_ckv // kkv, i, 0)),
                  pl.BlockSpec((1, kkv * LANE), lambda i: (0, 0)),
                  pl.BlockSpec(w_ukv.shape, lambda i: (0, 0)),
                  pl.BlockSpec((1, tm, LANE), lambda i: (s_tail, i, 0))] + gain_specs + rope_specs,
        out_specs=[pl.BlockSpec((MLA_HEADS, tm, dqk), lambda i: (0, i, 0)),
                   pl.BlockSpec((MLA_HEADS, MLA_V, tm), lambda i: (0, 0, i))],
        out_shape=[jax.ShapeDtypeStruct((MLA_HEADS, m, dqk), BF16),
                   jax.ShapeDtypeStruct((MLA_HEADS, MLA_V, m), BF16)],
        compiler_params=_cparams(("arbitrary",)),
        name="mla_kv_project",
    )(proj, kv_norm_g.reshape(1, -1), w_ukv, proj, nope_g[1].reshape(1, -1), rope_g[1].reshape(1, -1), cos, sin)
    return q_t, k, v_t


def _resproj_kernel(*refs, npair):
    xres_ref, gate_ref = refs[2 * npair], refs[2 * npair + 1]
    o_ref = refs[2 * npair + 2]
    acc = _dot(refs[0][...], refs[1][...])
    for p in range(1, npair):
        acc = acc + _dot(refs[2 * p][...], refs[2 * p + 1][...])
    o_ref[...] = xres_ref[...] + gate_ref[0] * acc


def resproj(pairs, xres, gate, seq, tm=1024, tn=512):
    m, n = xres.shape
    tpb = seq // tm
    in_specs, args = [], []
    for x, w in pairs:
        k = x.shape[1]
        in_specs += [pl.BlockSpec((tm, k), lambda i, j: (i, 0)),
                     pl.BlockSpec((k, tn), lambda i, j: (0, j))]
        args += [x, w]
    in_specs += [pl.BlockSpec((tm, tn), lambda i, j: (i, j)),
                 pl.BlockSpec((1, 1, tn), lambda i, j: (i // tpb, 0, j))]
    args += [xres, gate.reshape(-1, 1, n)]
    return pl.pallas_call(
        functools.partial(_resproj_kernel, npair=len(pairs)),
        grid=(m // tm, n // tn),
        in_specs=in_specs,
        out_specs=pl.BlockSpec((tm, tn), lambda i, j: (i, j)),
        out_shape=jax.ShapeDtypeStruct((m, n), F32),
        compiler_params=_cparams(("arbitrary", "arbitrary")),
        name="resproj",
    )(*args)


HALO = 8


def _ffn_up_kernel(h_ref, wg32_ref, wv32_ref, cwg_ref, cwv_ref, cbg_ref, cbv_ref, o_ref,
                   ug_ref, uv_ref, wg_ref, wv_ref, *, tm, tiles_per_seq):
    i = pl.program_id(1)
    first = (i % tiles_per_seq) == 0

    @pl.when(i == 0)
    def _():
        wg_ref[...] = wg32_ref[...].astype(wg_ref.dtype)
        wv_ref[...] = wv32_ref[...].astype(wv_ref.dtype)

    @pl.when(first)
    def _():
        ug_ref[0:HALO, :] = jnp.zeros((HALO, ug_ref.shape[1]), F32)
        uv_ref[0:HALO, :] = jnp.zeros((HALO, uv_ref.shape[1]), F32)

    @pl.when(jnp.logical_not(first))
    def _():
        ug_ref[0:HALO, :] = ug_ref[tm:tm + HALO, :]
        uv_ref[0:HALO, :] = uv_ref[tm:tm + HALO, :]

    h = h_ref[...]
    ug_ref[HALO:HALO + tm, :] = _dot(h, wg_ref[...])
    uv_ref[HALO:HALO + tm, :] = _dot(h, wv_ref[...])

    def conv(u_ref, cw_ref, cb_ref):
        out = cb_ref[...]
        for j in range(CONV_WIDTH):
            off = HALO - (CONV_WIDTH - 1) + j
            out = out + cw_ref[j:j + 1, :] * u_ref[off:off + tm, :]
        return out

    g = conv(ug_ref, cwg_ref, cbg_ref)
    v = conv(uv_ref, cwv_ref, cbv_ref)
    o_ref[...] = (g * jax.nn.sigmoid(g) * v).astype(o_ref.dtype)


def ffn_up(h, w_up_all, layer, conv_w, conv_b, seq, tm=1024, tn=512):
    m, d = h.shape
    f = w_up_all.shape[2] // 2
    nj = f // tn
    tps = seq // tm
    cb = conv_b.reshape(1, 2 * f)
    return pl.pallas_call(
        functools.partial(_ffn_up_kernel, tm=tm, tiles_per_seq=tps),
        grid=(nj, m // tm),
        in_specs=[pl.BlockSpec((tm, d), lambda j, i: (i, 0)),
                  pl.BlockSpec((None, d, tn), lambda j, i: (layer, 0, j)),
                  pl.BlockSpec((None, d, tn), lambda j, i: (layer, 0, nj + j)),
                  pl.BlockSpec((CONV_WIDTH, tn), lambda j, i: (0, j)),
                  pl.BlockSpec((CONV_WIDTH, tn), lambda j, i: (0, nj + j)),
                  pl.BlockSpec((1, tn), lambda j, i: (0, j)),
                  pl.BlockSpec((1, tn), lambda j, i: (0, nj + j))],
        out_specs=pl.BlockSpec((tm, tn), lambda j, i: (i, j)),
        out_shape=jax.ShapeDtypeStruct((m, f), BF16),
        scratch_shapes=[pltpu.VMEM((tm + HALO, tn), F32), pltpu.VMEM((tm + HALO, tn), F32),
                        pltpu.VMEM((d, tn), BF16), pltpu.VMEM((d, tn), BF16)],
        compiler_params=_cparams(("arbitrary", "arbitrary")),
        name="ffn_up_conv",
    )(h, w_up_all, w_up_all, conv_w, conv_w, cb, cb)


LOG2E = 1.4426950408889634
CWIN = 16


def _t5_shifted(dist, tbl_ref, h):
    val = jnp.full(dist.shape, tbl_ref[0, h], F32)
    for b in range(1, REL_BUCKETS):
        val = jnp.where(dist >= T5_THR[b - 1], tbl_ref[b, h], val)
    return (val - tbl_ref[REL_BUCKETS - 1, h]) * LOG2E


def _bias_tiles_kernel(tbl_ref, dt_ref, dc_ref):
    h = pl.program_id(0)
    key = lax.broadcasted_iota(jnp.int32, (LANE, LANE), 0)
    q = lax.broadcasted_iota(jnp.int32, (LANE, LANE), 1)
    for rel in range(2):
        dt_ref[0, rel] = _t5_shifted(rel * LANE + q - key, tbl_ref, h)
    dt_ref[0, 2] = jnp.zeros((LANE, LANE), F32)
    u = lax.broadcasted_iota(jnp.int32, (CWIN, LANE), 0)
    qc = lax.broadcasted_iota(jnp.int32, (CWIN, LANE), 1)
    dc_ref[0] = _t5_shifted(qc - CMP_STRIDE * (u - CWIN // 2) - (CMP_BLOCK - 1), tbl_ref, h)


def bias_tiles(rel_bias):
    nh = rel_bias.shape[1]
    return pl.pallas_call(
        _bias_tiles_kernel,
        grid=(nh,),
        in_specs=[pl.BlockSpec(memory_space=pltpu.SMEM)],
        out_specs=[pl.BlockSpec((1, 3, LANE, LANE), lambda h: (h, 0, 0, 0)),
                   pl.BlockSpec((1, CWIN, LANE), lambda h: (h, 0, 0))],
        out_shape=[jax.ShapeDtypeStruct((nh, 3, LANE, LANE), F32),
                   jax.ShapeDtypeStruct((nh, CWIN, LANE), F32)],
        compiler_params=_cparams(("arbitrary",)),
        name="t5_bias_tiles",
    )(rel_bias)


def _compress_kernel(x_ref, pe_ref, w1_ref, b1_ref, w2_ref, b2_ref, g_ref, o_ref, *, half):
    kv = pl.program_id(0)
    nchunk = x_ref.shape[1] // CMP_STRIDE
    a = jnp.zeros((nchunk, CMP_HIDDEN), F32)
    b = jnp.zeros((nchunk, CMP_HIDDEN), F32)
    for p in range(CMP_STRIDE):
        xp = x_ref[0, pl.ds(p, nchunk, stride=CMP_STRIDE), :]
        rows = slice(p * HEAD_DIM, (p + 1) * HEAD_DIM)
        a = a + _dot((xp + pe_ref[0, p:p + 1, :]).astype(BF16), w1_ref[0, rows, :])
        q = CMP_STRIDE + p
        b = b + _dot((xp + pe_ref[0, q:q + 1, :]).astype(BF16),
                     w1_ref[0, half + p * HEAD_DIM:half + (p + 1) * HEAD_DIM, :])
    b_next = jnp.concatenate([b[1:], jnp.zeros((1, b.shape[1]), F32)], axis=0)
    hid = jax.nn.gelu(a + b_next + b1_ref[0])
    out = _dot(hid.astype(BF16), w2_ref[0]) + b2_ref[0]
    normed = out * lax.rsqrt(jnp.mean(out * out, axis=-1, keepdims=True) + EPS) * g_ref[...]
    out = jnp.where(kv == 0, normed, out)
    o_ref[0, 0] = out.astype(o_ref.dtype)


def compress_kv(proj, slab0, bsz, seq, cmp_pe, cmp_w1, cmp_b1, cmp_w2, cmp_b2, g_k):
    nslab, m, _ = proj.shape
    nchunk = seq // CMP_STRIDE
    half = CMP_STRIDE * HEAD_DIM
    del nslab, m
    return pl.pallas_call(
        functools.partial(_compress_kernel, half=half),
        grid=(2, bsz, NSA_GROUPS),
        in_specs=[pl.BlockSpec((1, seq, HEAD_DIM), lambda kv, b, g: (slab0 + 2 * kv + g, b, 0)),
                  pl.BlockSpec((1, CMP_BLOCK, HEAD_DIM), lambda kv, b, g: (kv, 0, 0)),
                  pl.BlockSpec((1, 2 * half, CMP_HIDDEN), lambda kv, b, g: (kv, 0, 0)),
                  pl.BlockSpec((1, 1, CMP_HIDDEN), lambda kv, b, g: (kv, 0, 0)),
                  pl.BlockSpec((1, CMP_HIDDEN, HEAD_DIM), lambda kv, b, g: (kv, 0, 0)),
                  pl.BlockSpec((1, 1, HEAD_DIM), lambda kv, b, g: (kv, 0, 0)),
                  pl.BlockSpec((1, HEAD_DIM), lambda kv, b, g: (0, 0))],
        out_specs=pl.BlockSpec((1, 1, nchunk, HEAD_DIM), lambda kv, b, g: (kv, g, b, 0)),
        out_shape=jax.ShapeDtypeStruct((2, NSA_GROUPS, bsz * nchunk, HEAD_DIM), BF16),
        compiler_params=_cparams(("arbitrary", "arbitrary", "arbitrary")),
        name="nsa_compress",
    )(proj, cmp_pe, cmp_w1.astype(BF16), cmp_b1.reshape(2, 1, CMP_HIDDEN), cmp_w2.astype(BF16),
      cmp_b2.reshape(2, 1, HEAD_DIM), g_k.reshape(1, HEAD_DIM))


KW = 512
PV_KEYS = 256


def _tile_lanes(x, n):
    return jnp.concatenate([x] * n, axis=1)


def _flash_init(m_ref, l_ref, acc_ref):
    m_ref[...] = jnp.full(m_ref.shape, NEG, F32)
    l_ref[...] = jnp.zeros(l_ref.shape, F32)
    acc_ref[...] = jnp.zeros(acc_ref.shape, F32)


def _zero_after(x):
    bits = pltpu.bitcast(x, jnp.int32)
    return lax.shift_right_logical(lax.shift_right_logical(bits, 16), 16).astype(F32)


def _flash_update(s, v_t, m_ref, l_ref, acc_ref, col_max=None, after=None):
    m_old = m_ref[...]
    if col_max is None:
        col_max = jnp.max(s, axis=0, keepdims=True)
    m_new = jnp.maximum(m_old, col_max)
    alpha = jnp.exp2(m_old - m_new)
    l_new = alpha * l_ref[...]
    acc = alpha * acc_ref[...]
    nk = s.shape[0]
    for k0 in range(0, nk, PV_KEYS):
        p = jnp.exp2(s[k0:k0 + PV_KEYS] - m_new)
        l_new = l_new + jnp.sum(p, axis=0, keepdims=True)
        acc = acc + _dot(v_t[:, k0:k0 + PV_KEYS], p.astype(BF16))
    l_ref[...] = l_new
    acc_ref[...] = acc
    m_ref[...] = m_new if after is None else m_new + _zero_after(after)


SAFE_LOG2_BOUND = 60.0


def _flash_accumulate(s, v_t, l_ref, acc_ref, after=None):
    l_new = l_ref[...]
    acc = acc_ref[...]
    for k0 in range(0, s.shape[0], PV_KEYS):
        p = jnp.exp2(s[k0:k0 + PV_KEYS])
        l_new = l_new + jnp.sum(p, axis=0, keepdims=True)
        acc = acc + _dot(v_t[:, k0:k0 + PV_KEYS], p.astype(BF16))
    if after is not None:
        l_new = l_new + jnp.max(_zero_after(after), axis=0, keepdims=True)
    l_ref[...] = l_new
    acc_ref[...] = acc


def _sum_result(l_ref, acc_ref):
    den = l_ref[...]
    ok = den > 0.0
    return acc_ref[...] * jnp.where(ok, 1.0 / jnp.where(ok, den, 1.0), 0.0)


def _inv_den(m, den):
    ok = m > 0.5 * NEG
    return jnp.where(ok, 1.0 / jnp.where(ok, den, 1.0), 0.0)


def _flash_result(m_ref, l_ref, acc_ref):
    return acc_ref[...] * _inv_den(m_ref[...], l_ref[...])


def _softmax_cols(s):
    m = jnp.max(s, axis=0, keepdims=True)
    p = jnp.exp2(s - m)
    return p * _inv_den(m, jnp.sum(p, axis=0, keepdims=True))


def _near_bias(dt_ref, heads, qi, kt0, ntile):
    rows = []
    for j in range(ntile):
        rel = jnp.clip(qi - (kt0 + j), 0, 2)
        rows.append(jnp.concatenate([dt_ref[h, rel] for h in heads], axis=1))
    return jnp.concatenate(rows, axis=0)


def _pipelined_chunks(n, qk_stage, soft_stage):
    @pl.when(n > 0)
    def _():
        qk_stage(0, 0)

    def pair(p, x):
        c = 2 * p
        ahead = qk_stage(c + 1, 1)
        soft_stage(c, 0, ahead)
        ahead = qk_stage(jnp.minimum(c + 2, n - 1), 0)
        soft_stage(c + 1, 1, ahead)
        return x

    lax.fori_loop(0, n // 2, pair, 0)

    @pl.when(n % 2 == 1)
    def _():
        soft_stage(n - 1, 0, None)


NSA_STATE = 9
GATE_ROWS = -(-3 * NSA_HPG // SUBLANE) * SUBLANE


def _nsa_kernel(bound_ref, qt_ref, gt_ref, kc_ref, vct_ref, ks_ref, vst_ref, kw_ref, vwt_ref,
                dt_ref, dc_ref, ext_ref, o_ref, *scratch, seq, nc):
    ng = NSA_GROUPS
    state = [scratch[NSA_STATE * g:NSA_STATE * (g + 1)] for g in range(ng)]
    qi = pl.program_id(1)
    q0 = qi * QB
    hpg = NSA_HPG
    ncp = kc_ref.shape[1]
    ns = seq // SEL_BLOCK
    group_heads = [[g * hpg + h for h in range(hpg)] for g in range(ng)]
    q_ts = [jnp.concatenate([qt_ref[h] for h in group_heads[g]], axis=1) for g in range(ng)]
    pad = CWIN // 2
    wkeys = WINDOW + QB
    start = pl.multiple_of(jnp.maximum(q0 - WINDOW, 0), LANE)
    r0 = pl.multiple_of(qi * (QB // CMP_STRIDE), 8)

    s_w = []
    for g in range(ng):
        sc_ref = state[g][0]
        sc_ref[0:pad, :] = jnp.zeros((pad, hpg * QB), F32)
        sc_ref[pad + ncp:2 * pad + ncp, :] = jnp.zeros((pad, hpg * QB), F32)
        sc_ref[pad:pad + ncp, :] = _dot(kc_ref[g], q_ts[g])
        s_w.append(_dot(kw_ref[g, pl.ds(start, wkeys), :], q_ts[g]))

    ci = lax.broadcasted_iota(jnp.int32, (ncp, QB), 0)
    tc = q0 + lax.broadcasted_iota(jnp.int32, (ncp, QB), 1)
    valid_c = (ci * CMP_STRIDE + CMP_BLOCK - 1 <= tc) & (ci < nc)
    madd_c = _tile_lanes(jnp.where(valid_c, 0.0, NEG), hpg)
    oc_t, p_sum = [], []
    for g in range(ng):
        sc_ref = state[g][0]
        sc_ref[pl.ds(r0, CWIN), :] = sc_ref[pl.ds(r0, CWIN), :] + jnp.concatenate(
            [dc_ref[h] for h in group_heads[g]], axis=1)
        p_c = _softmax_cols(sc_ref[pad:pad + ncp, :] + madd_c)
        oc_t.append(_dot(vct_ref[g], p_c.astype(BF16)))
        ps = p_c[:, 0:QB]
        for h in range(1, hpg):
            ps = ps + p_c[:, h * QB:(h + 1) * QB]
        p_sum.append(ps)

    dist_w = (q0 + lax.broadcasted_iota(jnp.int32, (wkeys, QB), 1)) - (
        start + lax.broadcasted_iota(jnp.int32, (wkeys, QB), 0))
    madd_w = _tile_lanes(jnp.where((dist_w >= 0) & (dist_w < WINDOW), 0.0, NEG), hpg)
    ow_t = []
    for g in range(ng):
        p_w = _softmax_cols(s_w[g] + _near_bias(dt_ref, group_heads[g], qi, start // LANE, wkeys // LANE) + madd_w)
        ow_t.append(_dot(vwt_ref[g, :, pl.ds(start, wkeys)], p_w.astype(BF16)))

    per = SEL_BLOCK // CMP_STRIDE
    blk = lax.broadcasted_iota(jnp.int32, (LANE, QB), 0)
    t = q0 + lax.broadcasted_iota(jnp.int32, (LANE, QB), 1)
    tb = t // SEL_BLOCK
    forced = (blk == 0) | (blk == tb) | (blk == tb - 1)
    blk_f = blk.astype(F32)
    scores = []
    for g in range(ng):
        ps_ref = state[g][1]
        ps_ref[0:SUBLANE, :] = jnp.zeros((SUBLANE, QB), F32)
        ps_ref[SUBLANE:SUBLANE + ncp, :] = p_sum[g]
        band = [ps_ref[pl.ds(SUBLANE + r, ns, stride=per), :] for r in range(-1, per)]
        imp = 0.5 * band[0] + band[1] + band[2] + band[3] + 0.5 * band[4]
        if ns < LANE:
            imp = jnp.concatenate([imp, jnp.zeros((LANE - ns, QB), F32)], axis=0)
        score = jnp.where(forced, FORCE, jnp.where(blk * SEL_BLOCK <= t, imp, NEG))
        scores.append(jnp.where(blk < ns, score, -jnp.inf))
    sels = [jnp.zeros((LANE, QB), F32) for _ in range(ng)]
    for _ in range(min(SEL_TOP_N, ns)):
        for g in range(ng):
            mx = jnp.max(scores[g], axis=0, keepdims=True)
            first = jnp.min(jnp.where(scores[g] == mx, blk_f, float(LANE)), axis=0, keepdims=True)
            pick = blk_f == first
            sels[g] = jnp.where(pick, 1.0, sels[g])
            scores[g] = jnp.where(pick, -jnp.inf, scores[g])
    sel_b = [s.astype(BF16) for s in sels]

    kpos = lax.broadcasted_iota(jnp.int32, (KW, QB), 0)
    tq = q0 + lax.broadcasted_iota(jnp.int32, (KW, QB), 1)
    bounded_ok = bound_ref[0] <= SAFE_LOG2_BOUND
    shift = jnp.where(bounded_ok, bound_ref[0], 0.0)
    c_near = jnp.maximum(qi - 1, 0) // (KW // LANE)

    def scores_of(g, c0):
        chosen = _dot(ext_ref[pl.ds(c0, KW), :], sel_b[g])
        return (chosen - 1.0) * (-NEG) - shift, _dot(ks_ref[g, pl.ds(c0, KW), :], q_ts[g])

    def attend(bounded):
        for g in range(ng):
            _flash_init(*state[g][2:5])

        def qk_stage(c, buf):
            c0 = pl.multiple_of(c * KW, KW)
            ahead = []
            for g in range(ng):
                madd, s = scores_of(g, c0)
                s = s + _tile_lanes(madd, hpg)
                state[g][5 + buf][...] = s
                if bounded:
                    ahead.append(s[KW - 8:KW])
                else:
                    ahead.append(jnp.max(s, axis=0, keepdims=True))
                    state[g][7 + buf][...] = ahead[-1]
            return ahead

        def soft_stage(c, buf, ahead):
            c0 = pl.multiple_of(c * KW, KW)
            for g in range(ng):
                m_ref, l_ref, acc_ref = state[g][2:5]
                after = None if ahead is None else ahead[g]
                v_t = vst_ref[g, :, pl.ds(c0, KW)]
                if bounded:
                    _flash_accumulate(state[g][5 + buf][...], v_t, l_ref, acc_ref, after=after)
                else:
                    _flash_update(state[g][5 + buf][...], v_t, m_ref, l_ref, acc_ref,
                                  col_max=state[g][7 + buf][...], after=after)

        def near_step(c, x):
            c0 = pl.multiple_of(c * KW, KW)
            causal = jnp.where(c0 + kpos <= tq, 0.0, NEG)
            scores = []
            for g in range(ng):
                madd, s = scores_of(g, c0)
                scores.append(s + _tile_lanes(madd + causal, hpg)
                              + _near_bias(dt_ref, group_heads[g], qi, c * (KW // LANE), KW // LANE))
            for g in range(ng):
                m_ref, l_ref, acc_ref = state[g][2:5]
                if bounded:
                    _flash_accumulate(scores[g], vst_ref[g, :, pl.ds(c0, KW)], l_ref, acc_ref)
                else:
                    _flash_update(scores[g], vst_ref[g, :, pl.ds(c0, KW)], m_ref, l_ref, acc_ref)
            return x

        _pipelined_chunks(c_near, qk_stage, soft_stage)
        lax.fori_loop(c_near, qi // (KW // LANE) + 1, near_step, 0)
        for g in range(ng):
            m_ref, l_ref, acc_ref = state[g][2:5]
            acc_ref[...] = _sum_result(l_ref, acc_ref) if bounded else _flash_result(m_ref, l_ref, acc_ref)

    pl.when(bounded_ok)(lambda: attend(True))
    pl.when(jnp.logical_not(bounded_ok))(lambda: attend(False))

    for g in range(ng):
        os_t = state[g][4][...]
        gates = jax.nn.sigmoid(gt_ref[g])
        for h in range(hpg):
            sl = slice(h * QB, (h + 1) * QB)
            o_t = (gates[3 * h:3 * h + 1] * oc_t[g][:, sl] + gates[3 * h + 1:3 * h + 2] * os_t[:, sl]
                   + gates[3 * h + 2:3 * h + 3] * ow_t[g][:, sl])
            hh = group_heads[g][h]
            o_ref[:, hh * HEAD_DIM:(hh + 1) * HEAD_DIM] = o_t.T.astype(o_ref.dtype)


def nsa_attention(logit_bound, q_t, gates_t, kc, vc_t, k_sw, v_sw_t, dt, dc, bsz, seq):
    nq = seq // QB
    ncp = seq // CMP_STRIDE
    nc = ncp - 1
    ns = seq // SEL_BLOCK
    assert ns <= LANE and seq >= WINDOW + QB and seq % KW == 0
    assert CMP_BLOCK == 2 * CMP_STRIDE and SEL_BLOCK == 4 * CMP_STRIDE
    expand =((np.arange(seq)[:, None] // SEL_BLOCK) == np.arange(LANE)[None, :]).astype(np.float32)
    ng = NSA_GROUPS
    once = dict(pipeline_mode=pl.Buffered(1))
    ks_spec = pl.BlockSpec((ng, seq, HEAD_DIM), lambda b, i: (0, b, 0), **once)
    kw_spec = pl.BlockSpec((ng, seq, HEAD_DIM), lambda b, i: (1, b, 0), **once)
    vs_spec = pl.BlockSpec((ng, HEAD_DIM, seq), lambda b, i: (0, 0, b), **once)
    vw_spec = pl.BlockSpec((ng, HEAD_DIM, seq), lambda b, i: (1, 0, b), **once)
    lanes = NSA_HPG * QB
    group_state = [pltpu.VMEM((ncp + CWIN, lanes), F32), pltpu.VMEM((ncp + SUBLANE, QB), F32),
                   pltpu.VMEM((1, lanes), F32), pltpu.VMEM((1, lanes), F32), pltpu.VMEM((HEAD_DIM, lanes), F32),
                   pltpu.VMEM((KW, lanes), F32), pltpu.VMEM((KW, lanes), F32),
                   pltpu.VMEM((1, lanes), F32), pltpu.VMEM((1, lanes), F32)]
    assert len(group_state) == NSA_STATE
    return pl.pallas_call(
        functools.partial(_nsa_kernel, seq=seq, nc=nc),
        grid=(bsz, nq),
        in_specs=[pl.BlockSpec(memory_space=pltpu.SMEM),
                  pl.BlockSpec((NSA_HEADS, HEAD_DIM, QB), lambda b, i: (0, 0, b * nq + i)),
                  pl.BlockSpec((ng, GATE_ROWS, QB), lambda b, i: (0, 0, b * nq + i)),
                  pl.BlockSpec((ng, ncp, HEAD_DIM), lambda b, i: (0, b, 0)),
                  pl.BlockSpec((ng, HEAD_DIM, ncp), lambda b, i: (0, 0, b)),
                  ks_spec, vs_spec, kw_spec, vw_spec,
                  pl.BlockSpec((NSA_HEADS, 3, LANE, LANE), lambda b, i: (0, 0, 0, 0)),
                  pl.BlockSpec((NSA_HEADS, CWIN, LANE), lambda b, i: (0, 0, 0)),
                  pl.BlockSpec((seq, LANE), lambda b, i: (0, 0))],
        out_specs=pl.BlockSpec((QB, NSA_HEADS * HEAD_DIM), lambda b, i: (b * nq + i, 0)),
        out_shape=jax.ShapeDtypeStruct((bsz * seq, NSA_HEADS * HEAD_DIM), BF16),
        scratch_shapes=group_state * ng,
        compiler_params=_cparams(("arbitrary", "arbitrary")),
        name="nsa_attention",
    )(jnp.reshape(logit_bound, (1,)).astype(F32), q_t, gates_t, kc, vc_t, k_sw, v_sw_t, k_sw, v_sw_t, dt, dc,
      jnp.asarray(expand, BF16))


MLA_HPS = 2


def _mla_kernel(bound_ref, qt_ref, k_ref, vt_ref, o_ref, *scratch):
    qi = pl.program_id(2)
    chains = [scratch[3 * h:3 * h + 3] for h in range(MLA_HPS)]
    sbuf = [scratch[(3 + b) * MLA_HPS:(4 + b) * MLA_HPS] for b in range(2)]
    cbuf = [scratch[(5 + b) * MLA_HPS:(6 + b) * MLA_HPS] for b in range(2)]
    c_diag = pl.multiple_of(qi * KW, KW)
    kpos = lax.broadcasted_iota(jnp.int32, (KW, KW), 0)
    tq = lax.broadcasted_iota(jnp.int32, (KW, KW), 1)
    dv = vt_ref.shape[1]

    def attend(bounded):
        for ch in chains:
            _flash_init(*ch)

        def qk_stage(c, buf):
            c0 = pl.multiple_of(c * KW, KW)
            ahead = []
            for h in range(MLA_HPS):
                s = _dot(k_ref[h, pl.ds(c0, KW), :], qt_ref[h])
                sbuf[buf][h][...] = s
                if bounded:
                    ahead.append(s[KW - 8:KW])
                else:
                    ahead.append(jnp.max(s, axis=0, keepdims=True))
                    cbuf[buf][h][...] = ahead[-1]
            return ahead

        def soft_stage(c, buf, ahead):
            c0 = pl.multiple_of(c * KW, KW)
            for h, (m_ref, l_ref, acc_ref) in enumerate(chains):
                after = None if ahead is None else ahead[h]
                if bounded:
                    _flash_accumulate(sbuf[buf][h][...], vt_ref[h, :, pl.ds(c0, KW)], l_ref, acc_ref, after=after)
                else:
                    _flash_update(sbuf[buf][h][...], vt_ref[h, :, pl.ds(c0, KW)], m_ref, l_ref, acc_ref,
                                  col_max=cbuf[buf][h][...], after=after)

        _pipelined_chunks(qi, qk_stage, soft_stage)
        causal = jnp.where(kpos <= tq, 0.0, NEG)
        scores = [_dot(k_ref[h, pl.ds(c_diag, KW), :], qt_ref[h]) + causal for h in range(MLA_HPS)]
        for h, (m_ref, l_ref, acc_ref) in enumerate(chains):
            if bounded:
                _flash_accumulate(scores[h], vt_ref[h, :, pl.ds(c_diag, KW)], l_ref, acc_ref)
                o_t = _sum_result(l_ref, acc_ref)
            else:
                _flash_update(scores[h], vt_ref[h, :, pl.ds(c_diag, KW)], m_ref, l_ref, acc_ref)
                o_t = _flash_result(m_ref, l_ref, acc_ref)
            o_ref[:, h * dv:(h + 1) * dv] = o_t.T.astype(o_ref.dtype)

    bounded_ok = bound_ref[0] <= SAFE_LOG2_BOUND
    pl.when(bounded_ok)(lambda: attend(True))
    pl.when(jnp.logical_not(bounded_ok))(lambda: attend(False))


def mla_attention(logit_bound, q_t, k, v_t, bsz, seq):
    nh, dqk, _ = q_t.shape
    dv = v_t.shape[1]
    nq = seq // KW
    hps = MLA_HPS
    state = [pltpu.VMEM((1, KW), F32), pltpu.VMEM((1, KW), F32), pltpu.VMEM((dv, KW), F32)] * hps
    state += [pltpu.VMEM((KW, KW), F32)] * (2 * hps)
    state += [pltpu.VMEM((1, KW), F32)] * (2 * hps)
    return pl.pallas_call(
        _mla_kernel,
        grid=(bsz, nh // hps, nq),
        in_specs=[pl.BlockSpec(memory_space=pltpu.SMEM),
                  pl.BlockSpec((hps, dqk, KW), lambda b, h, i: (h, 0, b * nq + i)),
                  pl.BlockSpec((hps, seq, dqk), lambda b, h, i: (h, b, 0)),
                  pl.BlockSpec((hps, dv, seq), lambda b, h, i: (h, 0, b))],
        out_specs=pl.BlockSpec((KW, hps * dv), lambda b, h, i: (b * nq + i, h)),
        out_shape=jax.ShapeDtypeStruct((bsz * seq, nh * dv), BF16),
        scratch_shapes=state,
        compiler_params=_cparams(("arbitrary", "arbitrary", "arbitrary")),
        name="mla_attention",
    )(jnp.reshape(logit_bound, (1,)).astype(F32), q_t, k, v_t)


INT_MIN = -2 ** 31
NEG_KEY = int(np.array(NEG, np.float32).view(np.int32)) ^ 0x7FFFFFFF
KEY_BITS = 32
SURE_BITS = 22


def _sort_key(x):
    bits = pltpu.bitcast(x + 0.0, jnp.int32)
    return jnp.where(bits < 0, bits ^ 0x7FFFFFFF, bits)


def _dsa_kernel(bound_ref, iqt_ref, iwt_ref, ik_ref, qt_ref, k_ref, vt_ref, dt_ref, o_ref,
                key_ref, madd_ref, *state, seq, k_sel):
    qi = pl.program_id(1)
    q0 = qi * QB
    n_chunk = (q0 + QB + KW - 1) // KW
    n_rest = seq - n_chunk * KW
    kpos = lax.broadcasted_iota(jnp.int32, (KW, QB), 0)
    tq = q0 + lax.broadcasted_iota(jnp.int32, (KW, QB), 1)
    hpp = KW // QB

    def score_chunk(c, x):
        c0 = pl.multiple_of(c * KW, KW)
        ikc = ik_ref[pl.ds(c0, KW), :]
        acc = jnp.zeros((KW, QB), F32)
        for piece in range(IDX_HEADS // hpp):
            sl = slice(piece * KW, (piece + 1) * KW)
            s = jnp.maximum(_dot(ikc, iqt_ref[0, :, sl]), 0.0) * iwt_ref[0, :, sl]
            for j in range(hpp):
                acc = acc + s[:, j * QB:(j + 1) * QB]
        acc = jnp.where(c0 + kpos <= tq, acc, NEG)
        key_ref[pl.ds(c0, KW), :] = _sort_key(acc)
        return x

    lax.fori_loop(0, n_chunk, score_chunk, 0)

    def count(pred):
        def body(c, acc):
            c0 = pl.multiple_of(c * KW, KW)
            hit = jnp.where(pred(key_ref[pl.ds(c0, KW), :], c0), 1.0, 0.0)
            parts = [hit[SUBLANE * i:SUBLANE * (i + 1)] for i in range(KW // SUBLANE)]
            while len(parts) > 1:
                parts = [parts[i] + parts[i + 1] for i in range(0, len(parts), 2)]
            return acc + parts[0]
        acc = lax.fori_loop(0, n_chunk, body, jnp.zeros((SUBLANE, QB), F32))
        return jnp.sum(acc, axis=0, keepdims=True)

    rest = n_rest.astype(F32)
    kf = float(k_sel)

    def bit_step(i, st):
        u, thr_s, settled = st
        bit = jnp.left_shift(jnp.int32(1), KEY_BITS - 1 - i)
        trial = (u | bit) ^ INT_MIN
        cnt = count(lambda keys, c0: keys >= trial) + jnp.where(NEG_KEY >= trial, rest, 0.0)
        new = (cnt == kf) & (settled < 0.5)
        return (jnp.where(cnt >= kf, u | bit, u), jnp.where(new, trial, thr_s), jnp.where(new, 1.0, settled))

    st = (jnp.zeros((1, QB), jnp.int32), jnp.zeros((1, QB), jnp.int32), jnp.zeros((1, QB), F32))
    st = lax.fori_loop(0, SURE_BITS, bit_step, st)
    _, (u, thr_s, settled) = lax.while_loop(
        lambda c: (c[0] < KEY_BITS) & (jnp.min(c[1][2]) < 0.5),
        lambda c: (c[0] + 1, bit_step(c[0], c[1])), (jnp.int32(SURE_BITS), st))
    is_settled = settled > 0.5
    thr = jnp.where(is_settled, thr_s, u ^ INT_MIN)

    def edge_counts():
        return (count(lambda keys, c0: keys > thr) + jnp.where(NEG_KEY > thr, rest, 0.0),
                count(lambda keys, c0: keys >= thr) + jnp.where(NEG_KEY >= thr, rest, 0.0))

    zero_cnt = jnp.zeros((1, QB), F32)
    cnt_gt, cnt_ge = lax.cond(jnp.min(settled) > 0.5, lambda: (zero_cnt, zero_cnt), edge_counts)
    need = kf - cnt_gt
    tie_q = (cnt_ge > kf) & (thr != NEG_KEY) & jnp.logical_not(is_settled)
    idx_bits = (seq - 1).bit_length()
    no_cut = 2 ** 30

    def tie_cut():
        def idx_step(i, x):
            bit = jnp.left_shift(jnp.int32(1), idx_bits - 1 - i)
            trial = x | bit
            f = count(lambda keys, c0: (keys == thr) & (c0 + kpos < trial))
            return jnp.where(f <= need - 1.0, trial, x)
        return lax.fori_loop(0, idx_bits, idx_step, jnp.zeros((1, QB), jnp.int32))

    any_tie = jnp.max(jnp.where(tie_q, 1.0, 0.0)) > 0.0
    x_cut = lax.cond(any_tie, tie_cut, lambda: jnp.full((1, QB), no_cut, jnp.int32))
    x_cut = jnp.where(tie_q, x_cut, no_cut)

    def mask_chunk(c, x):
        c0 = pl.multiple_of(c * KW, KW)
        keys = key_ref[pl.ds(c0, KW), :]
        pos = c0 + kpos
        chosen = (keys > thr) | ((keys == thr) & (pos <= x_cut))
        madd_ref[pl.ds(c0, KW), :] = jnp.where(chosen & (pos <= tq), -shift, NEG)
        return x

    bounded_ok = bound_ref[0] <= SAFE_LOG2_BOUND
    shift = jnp.where(bounded_ok, bound_ref[0], 0.0)
    lax.fori_loop(0, n_chunk, mask_chunk, 0)

    c_near = jnp.maximum(qi - 1, 0) // (KW // LANE)
    ng = DSA_KV_HEADS
    chains = [state[3 * g:3 * g + 3] for g in range(ng)]
    sbuf = [state[(3 + b) * ng:(4 + b) * ng] for b in range(2)]
    cbuf = [state[(5 + b) * ng:(6 + b) * ng] for b in range(2)]
    group_heads = [[g * DSA_HPG + h for h in range(DSA_HPG)] for g in range(ng)]

    def raw_scores(c0, g):
        q_t = jnp.concatenate([qt_ref[h] for h in group_heads[g]], axis=1)
        return _dot(k_ref[g, pl.ds(c0, KW), :], q_t)

    def attend(bounded):
        for ch in chains:
            _flash_init(*ch)

        def qk_stage(c, buf):
            c0 = pl.multiple_of(c * KW, KW)
            madd = _tile_lanes(madd_ref[pl.ds(c0, KW), :], DSA_HPG)
            ahead = []
            for g in range(ng):
                s = raw_scores(c0, g) + madd
                sbuf[buf][g][...] = s
                if bounded:
                    ahead.append(s[KW - 8:KW])
                else:
                    ahead.append(jnp.max(s, axis=0, keepdims=True))
                    cbuf[buf][g][...] = ahead[-1]
            return ahead

        def soft_stage(c, buf, ahead):
            c0 = pl.multiple_of(c * KW, KW)
            for g, (m_ref, l_ref, acc_ref) in enumerate(chains):
                after = None if ahead is None else ahead[g]
                if bounded:
                    _flash_accumulate(sbuf[buf][g][...], vt_ref[g, :, pl.ds(c0, KW)], l_ref, acc_ref, after=after)
                else:
                    _flash_update(sbuf[buf][g][...], vt_ref[g, :, pl.ds(c0, KW)], m_ref, l_ref, acc_ref,
                                  col_max=cbuf[buf][g][...], after=after)

        _pipelined_chunks(c_near, qk_stage, soft_stage)

        def near_step(c, x):
            c0 = pl.multiple_of(c * KW, KW)
            madd = _tile_lanes(madd_ref[pl.ds(c0, KW), :], DSA_HPG)
            scores = [raw_scores(c0, g) + madd
                      + _near_bias(dt_ref, group_heads[g], qi, c * (KW // LANE), KW // LANE)
                      for g in range(ng)]
            for g, (m_ref, l_ref, acc_ref) in enumerate(chains):
                if bounded:
                    _flash_accumulate(scores[g], vt_ref[g, :, pl.ds(c0, KW)], l_ref, acc_ref)
                else:
                    _flash_update(scores[g], vt_ref[g, :, pl.ds(c0, KW)], m_ref, l_ref, acc_ref)
            return x

        lax.fori_loop(c_near, n_chunk, near_step, 0)
        for g, (m_ref, l_ref, acc_ref) in enumerate(chains):
            o_t = _sum_result(l_ref, acc_ref) if bounded else _flash_result(m_ref, l_ref, acc_ref)
            for h in range(DSA_HPG):
                hh = group_heads[g][h]
                o_ref[:, hh * HEAD_DIM:(hh + 1) * HEAD_DIM] = o_t[:, h * QB:(h + 1) * QB].T.astype(o_ref.dtype)

    pl.when(bounded_ok)(lambda: attend(True))
    pl.when(jnp.logical_not(bounded_ok))(lambda: attend(False))


def _idx_prep_kernel(p_ref, c_ref, sa_ref, sb_ref, iqt_ref, ik_ref, iwt_ref, *, ntile):
    nslab_q = IDX_HEADS * IDX_DIM // LANE
    per = LANE // IDX_DIM
    half = IDX_ROPE // 2
    zrows = jnp.zeros((LANE - IDX_DIM, QB), F32)

    def rope_slab(x, c, sa, sb):
        return x * c + pltpu.roll(x, LANE - half, axis=1) * sa + pltpu.roll(x, half, axis=1) * sb

    for t in range(ntile):
        rows = slice(t * QB, (t + 1) * QB)
        c, sa, sb = c_ref[rows, :], sa_ref[rows, :], sb_ref[rows, :]
        cols = []
        for s in range(nslab_q):
            x_t = (rope_slab(p_ref[s, rows, :], c, sa, sb) * IDX_DIM ** -0.5).T
            for j in range(per):
                cols.append(jnp.concatenate([x_t[j * IDX_DIM:(j + 1) * IDX_DIM], zrows], axis=0))
        iqt_ref[t] = jnp.concatenate(cols, axis=1).astype(iqt_ref.dtype)
        tail = p_ref[nslab_q, rows, :]
        lane = lax.broadcasted_iota(jnp.int32, (QB, LANE), 1)
        ik_ref[rows, :] = jnp.where(lane < IDX_DIM, rope_slab(tail, c, sa, sb), 0.0).astype(ik_ref.dtype)
        w_t = (tail * IDX_HEADS ** -0.5).T
        iwt_ref[t] = jnp.concatenate([w_t[IDX_DIM + h:IDX_DIM + h + 1, :] for h in range(IDX_HEADS)], axis=1)


def indexer_operands(proj, seq, tm=512):
    _, m, _ = proj.shape
    ntile = tm // QB
    tps = seq // tm
    cos, sin = _rope_tables(seq, IDX_ROPE)
    zero = jnp.zeros_like(sin)
    rest = IDX_DIM - IDX_ROPE
    per = LANE // IDX_DIM
    c_tab = jnp.tile(jnp.concatenate([cos, cos, jnp.ones((seq, rest), F32)], axis=1), (1, per))
    sa_tab = jnp.tile(jnp.concatenate([-sin, zero, jnp.zeros((seq, rest), F32)], axis=1), (1, per))
    sb_tab = jnp.tile(jnp.concatenate([zero, sin, jnp.zeros((seq, rest), F32)], axis=1), (1, per))
    lanes = IDX_HEADS * QB
    tab_spec = pl.BlockSpec((tm, LANE), lambda i: (i % tps, 0))
    return pl.pallas_call(
        functools.partial(_idx_prep_kernel, ntile=ntile),
        grid=(m // tm,),
        in_specs=[pl.BlockSpec((proj.shape[0], tm, LANE), lambda i: (0, i, 0)), tab_spec, tab_spec, tab_spec],
        out_specs=[pl.BlockSpec((ntile, LANE, lanes), lambda i: (i, 0, 0)),
                   pl.BlockSpec((tm, LANE), lambda i: (i, 0)),
                   pl.BlockSpec((ntile, 1, lanes), lambda i: (i, 0, 0))],
        out_shape=[jax.ShapeDtypeStruct((m // QB, LANE, lanes), BF16),
                   jax.ShapeDtypeStruct((m, LANE), BF16),
                   jax.ShapeDtypeStruct((m // QB, 1, lanes), F32)],
        compiler_params=_cparams(("arbitrary",)),
        name="dsa_indexer_operands",
    )(proj, c_tab, sa_tab, sb_tab)


def dsa_attention(logit_bound, iq_t, iw_t, ik, q_t, k, v_t, dt, bsz, seq):
    nq = seq // QB
    k_sel = min(DSA_TOPK_MAX, seq // 4)
    assert seq % KW == 0
    lanes = DSA_HPG * QB
    return pl.pallas_call(
        functools.partial(_dsa_kernel, seq=seq, k_sel=k_sel),
        grid=(bsz, nq),
        in_specs=[pl.BlockSpec(memory_space=pltpu.SMEM),
                  pl.BlockSpec((1, LANE, IDX_HEADS * QB), lambda b, i: (b * nq + i, 0, 0)),
                  pl.BlockSpec((1, 1, IDX_HEADS * QB), lambda b, i: (b * nq + i, 0, 0)),
                  pl.BlockSpec((seq, LANE), lambda b, i: (b, 0)),
                  pl.BlockSpec((DSA_HEADS, HEAD_DIM, QB), lambda b, i: (0, 0, b * nq + i)),
                  pl.BlockSpec((DSA_KV_HEADS, seq, HEAD_DIM), lambda b, i: (0, b, 0),
                               pipeline_mode=pl.Buffered(1)),
                  pl.BlockSpec((DSA_KV_HEADS, HEAD_DIM, seq), lambda b, i: (0, 0, b),
                               pipeline_mode=pl.Buffered(1)),
                  pl.BlockSpec((DSA_HEADS, 3, LANE, LANE), lambda b, i: (0, 0, 0, 0),
                               pipeline_mode=pl.Buffered(1))],
        out_specs=pl.BlockSpec((QB, DSA_HEADS * HEAD_DIM), lambda b, i: (b * nq + i, 0)),
        out_shape=jax.ShapeDtypeStruct((bsz * seq, DSA_HEADS * HEAD_DIM), BF16),
        scratch_shapes=[pltpu.VMEM((seq, QB), jnp.int32), pltpu.VMEM((seq, QB), F32)]
        + [pltpu.VMEM((1, lanes), F32), pltpu.VMEM((1, lanes), F32),
           pltpu.VMEM((HEAD_DIM, lanes), F32)] * DSA_KV_HEADS
        + [pltpu.VMEM((KW, lanes), F32)] * (2 * DSA_KV_HEADS)
        + [pltpu.VMEM((1, lanes), F32)] * (2 * DSA_KV_HEADS),
        compiler_params=_cparams(("arbitrary", "arbitrary")),
        name="dsa_attention",
    )(jnp.reshape(logit_bound, (1,)).astype(F32), iq_t, iw_t, ik, q_t, k, v_t, dt)


def _rope_tables(seq, dim):
    half = dim // 2
    inv = ROPE_THETA ** (-jnp.arange(half, dtype=F32) / half)
    ang = jnp.arange(seq, dtype=F32)[:, None] * inv[None, :]
    return jnp.cos(ang), jnp.sin(ang)


def _logit_bound(gq, gk, dim, scale):
    return dim * scale * jnp.max(jnp.abs(gq)) * jnp.max(jnp.abs(gk)) * (1.0 + 2.0 ** -7)


def _pad_cols(w, n):
    return jnp.pad(w, ((0, 0), (0, n - w.shape[1])))


def _t(x):
    return jnp.swapaxes(x, -1, -2)


def _even_mixer(h, x2, gate, dt, dc, bias_bound, bsz, seq, w_in, w_out, nsa_qk_g, cmp_pe, cmp_w1, cmp_b1,
                cmp_w2, cmp_b2, q_norm_g, kv_norm_g, w_uq, w_ukv, nope_g, rope_g):
    m = bsz * seq
    nq_cols = NSA_HEADS * HEAD_DIM
    nkv_cols = 6 * NSA_GROUPS * HEAD_DIM
    ngate = 3 * NSA_HEADS
    o_gate = nq_cols + nkv_cols
    o_cq = o_gate + ngate
    o_ckv = o_cq + MLA_Q_RANK
    o_kpe = o_ckv + MLA_KV_RANK
    gw = NSA_GROUPS * HEAD_DIM
    kvw = [w_in[:, nq_cols + i * gw:nq_cols + (i + 1) * gw] for i in range(6)]
    scale = HEAD_DIM ** -0.5 * LOG2E
    q_t = proj_heads(h, w_in[:, :nq_cols].astype(BF16), nsa_qk_g[0] * scale, transpose=True)
    k_sw = proj_heads(h, jnp.concatenate([kvw[2], kvw[4]], axis=1).astype(BF16), nsa_qk_g[1],
                      transpose=False)
    v_sw_t = proj_heads(h, jnp.concatenate([kvw[3], kvw[5]], axis=1).astype(BF16), transpose=True)
    tail = jnp.concatenate([w_in[:, o_kpe:], w_in[:, o_gate:o_cq]], axis=1)
    w_r = jnp.concatenate([kvw[0], kvw[1], w_in[:, o_cq:o_kpe], _pad_cols(tail, LANE)], axis=1).astype(BF16)
    proj = proj_slabs(h, w_r, tn=w_r.shape[1])
    s_cq = 2 * NSA_GROUPS
    s_ckv = s_cq + MLA_Q_RANK // LANE
    s_tail = s_ckv + MLA_KV_RANK // LANE
    kvc = compress_kv(proj, 0, bsz, seq, cmp_pe, cmp_w1, cmp_b1, cmp_w2, cmp_b2, nsa_qk_g[1])
    tail_v = proj[s_tail]
    gates = tail_v[:, MLA_ROPE:MLA_ROPE + ngate].reshape(m, NSA_GROUPS, 3 * NSA_HPG)
    gates_t = jnp.pad(jnp.transpose(gates, (1, 2, 0)), ((0, 0), (0, GATE_ROWS - 3 * NSA_HPG), (0, 0)))
    nsa_bound = _logit_bound(nsa_qk_g[0], nsa_qk_g[1], HEAD_DIM, scale) + bias_bound
    o_nsa = nsa_attention(nsa_bound, q_t, gates_t, kvc[0], _t(kvc[1]), k_sw, v_sw_t,
                          dt[:NSA_HEADS], dc[:NSA_HEADS], bsz, seq)

    dq = MLA_NOPE + MLA_ROPE
    wq = w_uq.reshape(MLA_Q_RANK, MLA_HEADS, dq)
    wq_r = jnp.concatenate([wq[:, :, :MLA_NOPE].reshape(MLA_Q_RANK, -1),
                            wq[:, :, MLA_NOPE:].reshape(MLA_Q_RANK, -1)], axis=1).astype(BF16)
    cos, sin = _rope_tables(seq, MLA_ROPE)
    mscale = dq ** -0.5 * LOG2E
    side = [jnp.sqrt(MLA_NOPE * jnp.max(jnp.abs(nope_g[i])) ** 2 + MLA_ROPE * jnp.max(jnp.abs(rope_g[i])) ** 2)
            for i in range(2)]
    mla_bound = mscale * side[0] * side[1] * (1.0 + 2.0 ** -7)
    mla_shift = jnp.where(mla_bound <= SAFE_LOG2_BOUND, mla_bound, 0.0)
    q_mla_t, k_mla, v_mla_t = mla_project(proj, s_cq, s_ckv, s_tail, seq, q_norm_g, kv_norm_g, wq_r,
                                          w_ukv.astype(BF16), nope_g, rope_g, cos, sin, mscale, mla_shift)
    o_mla = mla_attention(mla_bound, q_mla_t, k_mla, v_mla_t, bsz, seq)
    w_o = w_out.astype(BF16)
    return resproj([(o_nsa, w_o[:nq_cols]), (o_mla, w_o[nq_cols:])], x2, gate, seq)


def _odd_mixer(h, x2, gate, dt, bias_bound, bsz, seq, w_in, w_out, qk_g):
    nq = DSA_HEADS * HEAD_DIM
    nkv = DSA_KV_HEADS * HEAD_DIM
    niq = IDX_HEADS * IDX_DIM
    o_k, o_v, o_iq = nq, nq + nkv, nq + 2 * nkv
    q_t = proj_heads(h, w_in[:, :o_k].astype(BF16), qk_g[0] * (HEAD_DIM ** -0.5 * LOG2E), transpose=True)
    k = proj_heads(h, w_in[:, o_k:o_v].astype(BF16), qk_g[1], transpose=False)
    v_t = proj_heads(h, w_in[:, o_v:o_iq].astype(BF16), transpose=True)
    w_idx = w_in[:, o_iq:]
    proj = proj_slabs(h, _pad_cols(w_idx, niq + LANE).astype(BF16), tn=niq + LANE)
    iq_t, ik, iw_t = indexer_operands(proj, seq)
    bound = _logit_bound(qk_g[0], qk_g[1], HEAD_DIM, HEAD_DIM ** -0.5 * LOG2E) + bias_bound
    o = dsa_attention(bound, iq_t, iw_t, ik, q_t, k, v_t, dt, bsz, seq)
    return resproj([(o, w_out.astype(BF16))], x2, gate, seq)


def _conv_ffn(h, x2, gate, seq, w_up_all, layer, conv_w, conv_b, w_down):
    a = ffn_up(h, w_up_all, layer, conv_w, conv_b, seq)
    return resproj([(a, w_down.astype(BF16))], x2, gate, seq)


def kernel(x, c, rel_bias, ada_w, ada_b, norm_g, ev_w_in, ev_w_out, nsa_qk_g, cmp_pe, cmp_w1, cmp_b1, cmp_w2, cmp_b2, mla_q_norm_g, mla_kv_norm_g, mla_w_uq, mla_w_ukv, mla_nope_g, mla_rope_g, od_w_in, od_w_out, dsa_qk_g, ffn_w_up, ffn_conv_w, ffn_conv_b, ffn_w_down):
    bsz, seq, d = x.shape
    depth = ada_w.shape[0]
    x2 = x.reshape(bsz * seq, d)
    mods = ada_all(c, ada_w, ada_b)
    dt, dc = bias_tiles(rel_bias)
    bias_bound = 2.0 * LOG2E * jnp.max(jnp.abs(rel_bias))
    for i in range(depth):
        j = i // 2
        shift, scale, gate = jnp.split(mods[i, 0], 3, axis=-1)
        h = modnorm(x2, norm_g[i, 0], scale, shift, seq)
        if i % 2 == 0:
            x2 = _even_mixer(h, x2, gate, dt, dc, bias_bound, bsz, seq, ev_w_in[j], ev_w_out[j], nsa_qk_g[j],
                             cmp_pe[j], cmp_w1[j], cmp_b1[j], cmp_w2[j], cmp_b2[j], mla_q_norm_g[j],
                             mla_kv_norm_g[j], mla_w_uq[j], mla_w_ukv[j], mla_nope_g[j], mla_rope_g[j])
        else:
            x2 = _odd_mixer(h, x2, gate, dt, bias_bound, bsz, seq, od_w_in[j], od_w_out[j], dsa_qk_g[j])
        shift, scale, gate = jnp.split(mods[i, 1], 3, axis=-1)
        h = modnorm(x2, norm_g[i, 1], scale, shift, seq)
        x2 = _conv_ffn(h, x2, gate, seq, ffn_w_up, i, ffn_conv_w[i], ffn_conv_b[i], ffn_w_down[i])
    return x2.reshape(bsz, seq, d)
```

```python
import functools
import math

import numpy as np
import jax
import jax.numpy as jnp
from jax import lax
from jax.experimental import pallas as pl
from jax.experimental.pallas import tpu as pltpu

HEAD_DIM = 128
NSA_HEADS = 8
NSA_GROUPS = 2
NSA_HPG = NSA_HEADS // NSA_GROUPS
CMP_BLOCK = 32
CMP_STRIDE = 16
CMP_HIDDEN = 256
SEL_BLOCK = 64
SEL_TOP_N = 16
WINDOW = 512
MLA_HEADS = 8
MLA_Q_RANK = 512
MLA_KV_RANK = 256
MLA_NOPE = 128
MLA_ROPE = 64
MLA_V = 128
DSA_HEADS = 16
DSA_KV_HEADS = 4
DSA_HPG = DSA_HEADS // DSA_KV_HEADS
IDX_HEADS = 16
IDX_DIM = 64
IDX_ROPE = 32
DSA_TOPK_MAX = 256
REL_BUCKETS = 32
REL_MAX_DIST = 128
CONV_WIDTH = 3
ROPE_THETA = 10000.0
EPS = 1e-6
NEG = -1e30
FORCE = 1e9

LANE = 128
SUBLANE = 8
QB = 128
VMEM_LIMIT = 56 * 1024 * 1024

F32 = jnp.float32
BF16 = jnp.bfloat16


def _t5_thresholds():
    d = np.arange(0, 4 * REL_MAX_DIST)
    half = REL_BUCKETS // 2
    val = np.log(np.maximum(d, 1) / half) / math.log(REL_MAX_DIST / half) * (REL_BUCKETS - half)
    large = np.minimum(half + np.floor(np.maximum(val, 0.0)).astype(np.int64), REL_BUCKETS - 1)
    bucket = np.where(d < half, d, large)
    return [int(np.argmax(bucket >= b)) for b in range(1, REL_BUCKETS)]


T5_THR = _t5_thresholds()
T5_FAR = T5_THR[-1]
assert T5_FAR <= LANE


def _cparams(sem):
    return pltpu.CompilerParams(dimension_semantics=sem, vmem_limit_bytes=VMEM_LIMIT)


def _dot(a, b):
    return jnp.dot(a, b, preferred_element_type=F32)


def _ada_kernel(c_ref, w_ref, b_ref, o_ref):
    c = c_ref[...]
    a = c * jax.nn.sigmoid(c)
    o_ref[0] = jnp.dot(a, w_ref[0], preferred_element_type=F32,
                       precision=lax.Precision.HIGHEST) + b_ref[0]


def ada_all(c, ada_w, ada_b):
    depth, two, d, n3 = ada_w.shape
    bsz = c.shape[0]
    rows = SUBLANE
    cp = jnp.zeros((rows, d), F32).at[:bsz].set(c)
    w = ada_w.reshape(depth * two, d, n3)
    b = ada_b.reshape(depth * two, 1, n3)
    tn = 512
    out = pl.pallas_call(
        _ada_kernel,
        grid=(depth * two, n3 // tn),
        in_specs=[pl.BlockSpec((rows, d), lambda l, j: (0, 0)),
                  pl.BlockSpec((1, d, tn), lambda l, j: (l, 0, j)),
                  pl.BlockSpec((1, 1, tn), lambda l, j: (l, 0, j))],
        out_specs=pl.BlockSpec((1, rows, tn), lambda l, j: (l, 0, j)),
        out_shape=jax.ShapeDtypeStruct((depth * two, rows, n3), F32),
        compiler_params=_cparams(("arbitrary", "arbitrary")),
        name="ada_mod",
    )(cp, w, b)
    return out[:, :bsz].reshape(depth, two, bsz, n3)


def _modnorm_kernel(x_ref, g_ref, sc_ref, sh_ref, o_ref):
    x = x_ref[...]
    y = x * lax.rsqrt(jnp.mean(x * x, axis=-1, keepdims=True) + EPS)
    h = (y * g_ref[...]) * (1.0 + sc_ref[0]) + sh_ref[0]
    o_ref[...] = h.astype(o_ref.dtype)


def modnorm(x2, g, scale, shift, seq, tm=1024):
    m, d = x2.shape
    tpb = seq // tm
    return pl.pallas_call(
        _modnorm_kernel,
        grid=(m // tm,),
        in_specs=[pl.BlockSpec((tm, d), lambda i: (i, 0)),
                  pl.BlockSpec((1, d), lambda i: (0, 0)),
                  pl.BlockSpec((1, 1, d), lambda i: (i // tpb, 0, 0)),
                  pl.BlockSpec((1, 1, d), lambda i: (i // tpb, 0, 0))],
        out_specs=pl.BlockSpec((tm, d), lambda i: (i, 0)),
        out_shape=jax.ShapeDtypeStruct((m, d), BF16),
        compiler_params=_cparams(("arbitrary",)),
        name="modnorm",
    )(x2, g.reshape(1, d), scale.reshape(-1, 1, d), shift.reshape(-1, 1, d))


def _proj_kernel(x_ref, w_ref, o_ref, *, nslab):
    acc = _dot(x_ref[...], w_ref[...])
    for s in range(nslab):
        o_ref[s] = acc[:, s * LANE:(s + 1) * LANE]


def proj_slabs(x, w, tm=1024, tn=384):
    m, k = x.shape
    n = w.shape[1]
    assert n % tn == 0 and m % tm == 0
    nslab = tn // LANE
    return pl.pallas_call(
        functools.partial(_proj_kernel, nslab=nslab),
        grid=(m // tm, n // tn),
        in_specs=[pl.BlockSpec((tm, k), lambda i, j: (i, 0)),
                  pl.BlockSpec((k, tn), lambda i, j: (0, j))],
        out_specs=pl.BlockSpec((nslab, tm, LANE), lambda i, j: (j, i, 0)),
        out_shape=jax.ShapeDtypeStruct((n // LANE, m, LANE), F32),
        compiler_params=_cparams(("arbitrary", "arbitrary")),
        name="proj_slabs",
    )(x, w)


def _proj_heads_kernel(x_ref, w_ref, g_ref, o_ref, *, nslab, norm, transpose):
    acc = _dot(x_ref[...], w_ref[...])
    for s in range(nslab):
        y = acc[:, s * LANE:(s + 1) * LANE]
        if norm:
            y = y * lax.rsqrt(jnp.mean(y * y, axis=-1, keepdims=True) + EPS) * g_ref[...]
        o_ref[s] = (y.T if transpose else y).astype(o_ref.dtype)


def proj_heads(x, w, g=None, *, transpose, tm=1024, tn=1024):
    m, k = x.shape
    n = w.shape[1]
    tn = min(tn, n)
    assert n % tn == 0 and m % tm == 0
    nslab = tn // LANE
    norm = g is not None
    if transpose:
        out_spec = pl.BlockSpec((nslab, LANE, tm), lambda i, j: (j, 0, i))
        out_shape = jax.ShapeDtypeStruct((n // LANE, LANE, m), BF16)
    else:
        out_spec = pl.BlockSpec((nslab, tm, LANE), lambda i, j: (j, i, 0))
        out_shape = jax.ShapeDtypeStruct((n // LANE, m, LANE), BF16)
    g2 = (g if norm else jnp.ones((LANE,), F32)).reshape(1, LANE)
    return pl.pallas_call(
        functools.partial(_proj_heads_kernel, nslab=nslab, norm=norm, transpose=transpose),
        grid=(m // tm, n // tn),
        in_specs=[pl.BlockSpec((tm, k), lambda i, j: (i, 0)),
                  pl.BlockSpec((k, tn), lambda i, j: (0, j)),
                  pl.BlockSpec((1, LANE), lambda i, j: (0, 0))],
        out_specs=out_spec,
        out_shape=out_shape,
        compiler_params=_cparams(("arbitrary", "arbitrary")),
        name="proj_heads",
    )(x, w, g2)


def _rms_rows(x, g):
    return x * lax.rsqrt(jnp.mean(x * x, axis=-1, keepdims=True) + EPS) * g


def _rope_rows(x, cos, sin):
    half = x.shape[-1] // 2
    x1, x2 = x[:, :half], x[:, half:]
    return jnp.concatenate([x1 * cos - x2 * sin, x1 * sin + x2 * cos], axis=1)


def _latent(x_ref, g_ref):
    x = jnp.concatenate([x_ref[s] for s in range(x_ref.shape[0])], axis=1)
    return _rms_rows(x, g_ref[...]).astype(BF16)


def _mla_q_kernel(shift_ref, x_ref, g_ref, w_ref, gn_ref, gr2_ref, c_ref, sa_ref, sb_ref, o_ref, *, scale):
    acc = _dot(_latent(x_ref, g_ref), w_ref[...])
    tm = acc.shape[0]
    for h in range(MLA_HEADS):
        nope = _rms_rows(acc[:, h * MLA_NOPE:(h + 1) * MLA_NOPE], gn_ref[...]) * scale
        o_ref[h, 0:MLA_NOPE, :] = nope.T.astype(o_ref.dtype)
    first = lax.broadcasted_iota(jnp.int32, (LANE - MLA_ROPE, tm), 0) == 0
    pad_rows = jnp.where(first, -shift_ref[0], 0.0)
    low = lax.broadcasted_iota(jnp.int32, (tm, LANE), 1) < MLA_ROPE
    c, sa, sb = c_ref[...], sa_ref[...], sb_ref[...]
    half = MLA_ROPE // 2
    per = LANE // MLA_ROPE
    for s in range(MLA_HEADS // per):
        x = acc[:, MLA_HEADS * MLA_NOPE + s * LANE:MLA_HEADS * MLA_NOPE + (s + 1) * LANE]
        sq = x * x
        s_low = jnp.sum(jnp.where(low, sq, 0.0), axis=-1, keepdims=True)
        s_all = jnp.sum(sq, axis=-1, keepdims=True)
        inv = jnp.where(low, lax.rsqrt(s_low / MLA_ROPE + EPS), lax.rsqrt((s_all - s_low) / MLA_ROPE + EPS))
        y = x * inv * gr2_ref[...]
        roped = y * c + pltpu.roll(y, LANE - half, axis=1) * sa + pltpu.roll(y, half, axis=1) * sb
        x_t = (roped * scale).T
        for j in range(per):
            o_ref[per * s + j, MLA_NOPE:MLA_NOPE + LANE, :] = jnp.concatenate(
                [x_t[j * MLA_ROPE:(j + 1) * MLA_ROPE], pad_rows], axis=0).astype(o_ref.dtype)


def _mla_kv_kernel(x_ref, g_ref, w_ref, tail_ref, gn_ref, gr_ref, cos_ref, sin_ref, ok_ref, ov_ref):
    acc = _dot(_latent(x_ref, g_ref), w_ref[...])
    tm = acc.shape[0]
    k_pe = _rope_rows(_rms_rows(tail_ref[0][:, :MLA_ROPE], gr_ref[...]), cos_ref[...], sin_ref[...])
    first = lax.broadcasted_iota(jnp.int32, (tm, LANE - MLA_ROPE), 1) == 0
    k_pe = jnp.concatenate([k_pe, jnp.where(first, 1.0, 0.0)], axis=1).astype(ok_ref.dtype)
    for h in range(MLA_HEADS):
        c0 = h * (MLA_NOPE + MLA_V)
        ok_ref[h, :, 0:MLA_NOPE] = _rms_rows(acc[:, c0:c0 + MLA_NOPE], gn_ref[...]).astype(ok_ref.dtype)
        ok_ref[h, :, MLA_NOPE:MLA_NOPE + LANE] = k_pe
        ov_ref[h] = acc[:, c0 + MLA_NOPE:c0 + MLA_NOPE + MLA_V].T.astype(ov_ref.dtype)


def mla_project(proj, s_cq, s_ckv, s_tail, seq, q_norm_g, kv_norm_g, wq_r, w_ukv, nope_g, rope_g, cos, sin,
                scale, shift, tm=512):
    _, m, _ = proj.shape
    kq, kkv = s_ckv - s_cq, s_tail - s_ckv
    tps = seq // tm
    dqk = MLA_NOPE + LANE
    half = MLA_ROPE // 2
    rope_specs = [pl.BlockSpec((tm, half), lambda i: (i % tps, 0))] * 2
    gain_specs = [pl.BlockSpec((1, MLA_NOPE), lambda i: (0, 0)), pl.BlockSpec((1, MLA_ROPE), lambda i: (0, 0))]
    per = LANE // MLA_ROPE
    zero = jnp.zeros_like(sin)
    c_tab = jnp.tile(jnp.concatenate([cos, cos], axis=1), (1, per))
    sa_tab = jnp.tile(jnp.concatenate([-sin, zero], axis=1), (1, per))
    sb_tab = jnp.tile(jnp.concatenate([zero, sin], axis=1), (1, per))
    tab_spec = pl.BlockSpec((tm, LANE), lambda i: (i % tps, 0))
    q_t = pl.pallas_call(
        functools.partial(_mla_q_kernel, scale=scale),
        grid=(m // tm,),
        in_specs=[pl.BlockSpec(memory_space=pltpu.SMEM),
                  pl.BlockSpec((kq, tm, LANE), lambda i: (s_cq // kq, i, 0)),
                  pl.BlockSpec((1, kq * LANE), lambda i: (0, 0)),
                  pl.BlockSpec(wq_r.shape, lambda i: (0, 0)),
                  pl.BlockSpec((1, MLA_NOPE), lambda i: (0, 0)), pl.BlockSpec((1, LANE), lambda i: (0, 0)),
                  tab_spec, tab_spec, tab_spec],
        out_specs=pl.BlockSpec((MLA_HEADS, dqk, tm), lambda i: (0, 0, i)),
        out_shape=jax.ShapeDtypeStruct((MLA_HEADS, dqk, m), BF16),
        compiler_params=_cparams(("arbitrary",)),
        name="mla_q_project",
    )(jnp.reshape(shift, (1,)).astype(F32), proj, q_norm_g.reshape(1, -1), wq_r, nope_g[0].reshape(1, -1),
      jnp.tile(rope_g[0].reshape(1, -1), (1, per)), c_tab, sa_tab, sb_tab)
    k, v_t = pl.pallas_call(
        _mla_kv_kernel,
        grid=(m // tm,),
        in_specs=[pl.BlockSpec((kkv, tm, LANE), lambda i: (s_ckv // kkv, i, 0)),
                  pl.BlockSpec((1, kkv * LANE), lambda i: (0, 0)),
                  pl.BlockSpec(w_ukv.shape, lambda i: (0, 0)),
                  pl.BlockSpec((1, tm, LANE), lambda i: (s_tail, i, 0))] + gain_specs + rope_specs,
        out_specs=[pl.BlockSpec((MLA_HEADS, tm, dqk), lambda i: (0, i, 0)),
                   pl.BlockSpec((MLA_HEADS, MLA_V, tm), lambda i: (0, 0, i))],
        out_shape=[jax.ShapeDtypeStruct((MLA_HEADS, m, dqk), BF16),
                   jax.ShapeDtypeStruct((MLA_HEADS, MLA_V, m), BF16)],
        compiler_params=_cparams(("arbitrary",)),
        name="mla_kv_project",
    )(proj, kv_norm_g.reshape(1, -1), w_ukv, proj, nope_g[1].reshape(1, -1), rope_g[1].reshape(1, -1), cos, sin)
    return q_t, k, v_t


def _resproj_kernel(*refs, npair):
    xres_ref, gate_ref = refs[2 * npair], refs[2 * npair + 1]
    o_ref = refs[2 * npair + 2]
    acc = _dot(refs[0][...], refs[1][...])
    for p in range(1, npair):
        acc = acc + _dot(refs[2 * p][...], refs[2 * p + 1][...])
    o_ref[...] = xres_ref[...] + gate_ref[0] * acc


def resproj(pairs, xres, gate, seq, tm=1024, tn=512):
    m, n = xres.shape
    tpb = seq // tm
    in_specs, args = [], []
    for x, w in pairs:
        k = x.shape[1]
        in_specs += [pl.BlockSpec((tm, k), lambda i, j: (i, 0)),
                     pl.BlockSpec((k, tn), lambda i, j: (0, j))]
        args += [x, w]
    in_specs += [pl.BlockSpec((tm, tn), lambda i, j: (i, j)),
                 pl.BlockSpec((1, 1, tn), lambda i, j: (i // tpb, 0, j))]
    args += [xres, gate.reshape(-1, 1, n)]
    return pl.pallas_call(
        functools.partial(_resproj_kernel, npair=len(pairs)),
        grid=(m // tm, n // tn),
        in_specs=in_specs,
        out_specs=pl.BlockSpec((tm, tn), lambda i, j: (i, j)),
        out_shape=jax.ShapeDtypeStruct((m, n), F32),
        compiler_params=_cparams(("arbitrary", "arbitrary")),
        name="resproj",
    )(*args)


HALO = 8


def _ffn_up_kernel(h_ref, wg32_ref, wv32_ref, cwg_ref, cwv_ref, cbg_ref, cbv_ref, o_ref,
                   ug_ref, uv_ref, wg_ref, wv_ref, *, tm, tiles_per_seq):
    i = pl.program_id(1)
    first = (i % tiles_per_seq) == 0

    @pl.when(i == 0)
    def _():
        wg_ref[...] = wg32_ref[...].astype(wg_ref.dtype)
        wv_ref[...] = wv32_ref[...].astype(wv_ref.dtype)

    @pl.when(first)
    def _():
        ug_ref[0:HALO, :] = jnp.zeros((HALO, ug_ref.shape[1]), F32)
        uv_ref[0:HALO, :] = jnp.zeros((HALO, uv_ref.shape[1]), F32)

    @pl.when(jnp.logical_not(first))
    def _():
        ug_ref[0:HALO, :] = ug_ref[tm:tm + HALO, :]
        uv_ref[0:HALO, :] = uv_ref[tm:tm + HALO, :]

    h = h_ref[...]
    ug_ref[HALO:HALO + tm, :] = _dot(h, wg_ref[...])
    uv_ref[HALO:HALO + tm, :] = _dot(h, wv_ref[...])

    def conv(u_ref, cw_ref, cb_ref):
        out = cb_ref[...]
        for j in range(CONV_WIDTH):
            off = HALO - (CONV_WIDTH - 1) + j
            out = out + cw_ref[j:j + 1, :] * u_ref[off:off + tm, :]
        return out

    g = conv(ug_ref, cwg_ref, cbg_ref)
    v = conv(uv_ref, cwv_ref, cbv_ref)
    o_ref[...] = (g * jax.nn.sigmoid(g) * v).astype(o_ref.dtype)


def ffn_up(h, w_up_all, layer, conv_w, conv_b, seq, tm=1024, tn=512):
    m, d = h.shape
    f = w_up_all.shape[2] // 2
    nj = f // tn
    tps = seq // tm
    cb = conv_b.reshape(1, 2 * f)
    return pl.pallas_call(
        functools.partial(_ffn_up_kernel, tm=tm, tiles_per_seq=tps),
        grid=(nj, m // tm),
        in_specs=[pl.BlockSpec((tm, d), lambda j, i: (i, 0)),
                  pl.BlockSpec((None, d, tn), lambda j, i: (layer, 0, j)),
                  pl.BlockSpec((None, d, tn), lambda j, i: (layer, 0, nj + j)),
                  pl.BlockSpec((CONV_WIDTH, tn), lambda j, i: (0, j)),
                  pl.BlockSpec((CONV_WIDTH, tn), lambda j, i: (0, nj + j)),
                  pl.BlockSpec((1, tn), lambda j, i: (0, j)),
                  pl.BlockSpec((1, tn), lambda j, i: (0, nj + j))],
        out_specs=pl.BlockSpec((tm, tn), lambda j, i: (i, j)),
        out_shape=jax.ShapeDtypeStruct((m, f), BF16),
        scratch_shapes=[pltpu.VMEM((tm + HALO, tn), F32), pltpu.VMEM((tm + HALO, tn), F32),
                        pltpu.VMEM((d, tn), BF16), pltpu.VMEM((d, tn), BF16)],
        compiler_params=_cparams(("arbitrary", "arbitrary")),
        name="ffn_up_conv",
    )(h, w_up_all, w_up_all, conv_w, conv_w, cb, cb)


LOG2E = 1.4426950408889634
CWIN = 16


def _t5_shifted(dist, tbl_ref, h):
    val = jnp.full(dist.shape, tbl_ref[0, h], F32)
    for b in range(1, REL_BUCKETS):
        val = jnp.where(dist >= T5_THR[b - 1], tbl_ref[b, h], val)
    return (val - tbl_ref[REL_BUCKETS - 1, h]) * LOG2E


def _bias_tiles_kernel(tbl_ref, dt_ref, dc_ref):
    h = pl.program_id(0)
    key = lax.broadcasted_iota(jnp.int32, (LANE, LANE), 0)
    q = lax.broadcasted_iota(jnp.int32, (LANE, LANE), 1)
    for rel in range(2):
        dt_ref[0, rel] = _t5_shifted(rel * LANE + q - key, tbl_ref, h)
    dt_ref[0, 2] = jnp.zeros((LANE, LANE), F32)
    u = lax.broadcasted_iota(jnp.int32, (CWIN, LANE), 0)
    qc = lax.broadcasted_iota(jnp.int32, (CWIN, LANE), 1)
    dc_ref[0] = _t5_shifted(qc - CMP_STRIDE * (u - CWIN // 2) - (CMP_BLOCK - 1), tbl_ref, h)


def bias_tiles(rel_bias):
    nh = rel_bias.shape[1]
    return pl.pallas_call(
        _bias_tiles_kernel,
        grid=(nh,),
        in_specs=[pl.BlockSpec(memory_space=pltpu.SMEM)],
        out_specs=[pl.BlockSpec((1, 3, LANE, LANE), lambda h: (h, 0, 0, 0)),
                   pl.BlockSpec((1, CWIN, LANE), lambda h: (h, 0, 0))],
        out_shape=[jax.ShapeDtypeStruct((nh, 3, LANE, LANE), F32),
                   jax.ShapeDtypeStruct((nh, CWIN, LANE), F32)],
        compiler_params=_cparams(("arbitrary",)),
        name="t5_bias_tiles",
    )(rel_bias)


def _compress_kernel(x_ref, pe_ref, w1_ref, b1_ref, w2_ref, b2_ref, g_ref, o_ref, *, half):
    kv = pl.program_id(0)
    nchunk = x_ref.shape[1] // CMP_STRIDE
    a = jnp.zeros((nchunk, CMP_HIDDEN), F32)
    b = jnp.zeros((nchunk, CMP_HIDDEN), F32)
    for p in range(CMP_STRIDE):
        xp = x_ref[0, pl.ds(p, nchunk, stride=CMP_STRIDE), :]
        rows = slice(p * HEAD_DIM, (p + 1) * HEAD_DIM)
        a = a + _dot((xp + pe_ref[0, p:p + 1, :]).astype(BF16), w1_ref[0, rows, :])
        q = CMP_STRIDE + p
        b = b + _dot((xp + pe_ref[0, q:q + 1, :]).astype(BF16),
                     w1_ref[0, half + p * HEAD_DIM:half + (p + 1) * HEAD_DIM, :])
    b_next = jnp.concatenate([b[1:], jnp.zeros((1, b.shape[1]), F32)], axis=0)
    hid = jax.nn.gelu(a + b_next + b1_ref[0])
    out = _dot(hid.astype(BF16), w2_ref[0]) + b2_ref[0]
    normed = out * lax.rsqrt(jnp.mean(out * out, axis=-1, keepdims=True) + EPS) * g_ref[...]
    out = jnp.where(kv == 0, normed, out)
    o_ref[0, 0] = out.astype(o_ref.dtype)


def compress_kv(proj, slab0, bsz, seq, cmp_pe, cmp_w1, cmp_b1, cmp_w2, cmp_b2, g_k):
    nslab, m, _ = proj.shape
    nchunk = seq // CMP_STRIDE
    half = CMP_STRIDE * HEAD_DIM
    del nslab, m
    return pl.pallas_call(
        functools.partial(_compress_kernel, half=half),
        grid=(2, bsz, NSA_GROUPS),
        in_specs=[pl.BlockSpec((1, seq, HEAD_DIM), lambda kv, b, g: (slab0 + 2 * kv + g, b, 0)),
                  pl.BlockSpec((1, CMP_BLOCK, HEAD_DIM), lambda kv, b, g: (kv, 0, 0)),
                  pl.BlockSpec((1, 2 * half, CMP_HIDDEN), lambda kv, b, g: (kv, 0, 0)),
                  pl.BlockSpec((1, 1, CMP_HIDDEN), lambda kv, b, g: (kv, 0, 0)),
                  pl.BlockSpec((1, CMP_HIDDEN, HEAD_DIM), lambda kv, b, g: (kv, 0, 0)),
                  pl.BlockSpec((1, 1, HEAD_DIM), lambda kv, b, g: (kv, 0, 0)),
                  pl.BlockSpec((1, HEAD_DIM), lambda kv, b, g: (0, 0))],
        out_specs=pl.BlockSpec((1, 1, nchunk, HEAD_DIM), lambda kv, b, g: (kv, g, b, 0)),
        out_shape=jax.ShapeDtypeStruct((2, NSA_GROUPS, bsz * nchunk, HEAD_DIM), BF16),
        compiler_params=_cparams(("arbitrary", "arbitrary", "arbitrary")),
        name="nsa_compress",
    )(proj, cmp_pe, cmp_w1.astype(BF16), cmp_b1.reshape(2, 1, CMP_HIDDEN), cmp_w2.astype(BF16),
      cmp_b2.reshape(2, 1, HEAD_DIM), g_k.reshape(1, HEAD_DIM))


KW = 512
PV_KEYS = 256


def _tile_lanes(x, n):
    return jnp.concatenate([x] * n, axis=1)


def _flash_init(m_ref, l_ref, acc_ref):
    m_ref[...] = jnp.full(m_ref.shape, NEG, F32)
    l_ref[...] = jnp.zeros(l_ref.shape, F32)
    acc_ref[...] = jnp.zeros(acc_ref.shape, F32)


def _zero_after(x):
    bits = pltpu.bitcast(x, jnp.int32)
    return lax.shift_right_logical(lax.shift_right_logical(bits, 16), 16).astype(F32)


def _flash_update(s, v_t, m_ref, l_ref, acc_ref, col_max=None, after=None):
    m_old = m_ref[...]
    if col_max is None:
        col_max = jnp.max(s, axis=0, keepdims=True)
    m_new = jnp.maximum(m_old, col_max)
    alpha = jnp.exp2(m_old - m_new)
    l_new = alpha * l_ref[...]
    acc = alpha * acc_ref[...]
    nk = s.shape[0]
    for k0 in range(0, nk, PV_KEYS):
        p = jnp.exp2(s[k0:k0 + PV_KEYS] - m_new)
        l_new = l_new + jnp.sum(p, axis=0, keepdims=True)
        acc = acc + _dot(v_t[:, k0:k0 + PV_KEYS], p.astype(BF16))
    l_ref[...] = l_new
    acc_ref[...] = acc
    m_ref[...] = m_new if after is None else m_new + _zero_after(after)


SAFE_LOG2_BOUND = 60.0


def _flash_accumulate(s, v_t, l_ref, acc_ref, after=None):
    l_new = l_ref[...]
    acc = acc_ref[...]
    for k0 in range(0, s.shape[0], PV_KEYS):
        p = jnp.exp2(s[k0:k0 + PV_KEYS])
        l_new = l_new + jnp.sum(p, axis=0, keepdims=True)
        acc = acc + _dot(v_t[:, k0:k0 + PV_KEYS], p.astype(BF16))
    if after is not None:
        l_new = l_new + jnp.max(_zero_after(after), axis=0, keepdims=True)
    l_ref[...] = l_new
    acc_ref[...] = acc


def _sum_result(l_ref, acc_ref):
    den = l_ref[...]
    ok = den > 0.0
    return acc_ref[...] * jnp.where(ok, 1.0 / jnp.where(ok, den, 1.0), 0.0)


def _inv_den(m, den):
    ok = m > 0.5 * NEG
    return jnp.where(ok, 1.0 / jnp.where(ok, den, 1.0), 0.0)


def _flash_result(m_ref, l_ref, acc_ref):
    return acc_ref[...] * _inv_den(m_ref[...], l_ref[...])


def _softmax_cols(s):
    m = jnp.max(s, axis=0, keepdims=True)
    p = jnp.exp2(s - m)
    return p * _inv_den(m, jnp.sum(p, axis=0, keepdims=True))


def _near_bias(dt_ref, heads, qi, kt0, ntile):
    rows = []
    for j in range(ntile):
        rel = jnp.clip(qi - (kt0 + j), 0, 2)
        rows.append(jnp.concatenate([dt_ref[h, rel] for h in heads], axis=1))
    return jnp.concatenate(rows, axis=0)


def _pipelined_chunks(n, qk_stage, soft_stage):
    @pl.when(n > 0)
    def _():
        qk_stage(0, 0)

    def pair(p, x):
        c = 2 * p
        ahead = qk_stage(c + 1, 1)
        soft_stage(c, 0, ahead)
        ahead = qk_stage(jnp.minimum(c + 2, n - 1), 0)
        soft_stage(c + 1, 1, ahead)
        return x

    lax.fori_loop(0, n // 2, pair, 0)

    @pl.when(n % 2 == 1)
    def _():
        soft_stage(n - 1, 0, None)


NSA_STATE = 9
GATE_ROWS = -(-3 * NSA_HPG // SUBLANE) * SUBLANE


def _nsa_kernel(bound_ref, qt_ref, gt_ref, kc_ref, vct_ref, ks_ref, vst_ref, kw_ref, vwt_ref,
                dt_ref, dc_ref, ext_ref, o_ref, *scratch, seq, nc):
    ng = NSA_GROUPS
    state = [scratch[NSA_STATE * g:NSA_STATE * (g + 1)] for g in range(ng)]
    qi = pl.program_id(1)
    q0 = qi * QB
    hpg = NSA_HPG
    ncp = kc_ref.shape[1]
    ns = seq // SEL_BLOCK
    group_heads = [[g * hpg + h for h in range(hpg)] for g in range(ng)]
    q_ts = [jnp.concatenate([qt_ref[h] for h in group_heads[g]], axis=1) for g in range(ng)]
    pad = CWIN // 2
    wkeys = WINDOW + QB
    start = pl.multiple_of(jnp.maximum(q0 - WINDOW, 0), LANE)
    r0 = pl.multiple_of(qi * (QB // CMP_STRIDE), 8)

    s_w = []
    for g in range(ng):
        sc_ref = state[g][0]
        sc_ref[0:pad, :] = jnp.zeros((pad, hpg * QB), F32)
        sc_ref[pad + ncp:2 * pad + ncp, :] = jnp.zeros((pad, hpg * QB), F32)
        sc_ref[pad:pad + ncp, :] = _dot(kc_ref[g], q_ts[g])
        s_w.append(_dot(kw_ref[g, pl.ds(start, wkeys), :], q_ts[g]))

    ci = lax.broadcasted_iota(jnp.int32, (ncp, QB), 0)
    tc = q0 + lax.broadcasted_iota(jnp.int32, (ncp, QB), 1)
    valid_c = (ci * CMP_STRIDE + CMP_BLOCK - 1 <= tc) & (ci < nc)
    madd_c = _tile_lanes(jnp.where(valid_c, 0.0, NEG), hpg)
    oc_t, p_sum = [], []
    for g in range(ng):
        sc_ref = state[g][0]
        sc_ref[pl.ds(r0, CWIN), :] = sc_ref[pl.ds(r0, CWIN), :] + jnp.concatenate(
            [dc_ref[h] for h in group_heads[g]], axis=1)
        p_c = _softmax_cols(sc_ref[pad:pad + ncp, :] + madd_c)
        oc_t.append(_dot(vct_ref[g], p_c.astype(BF16)))
        ps = p_c[:, 0:QB]
        for h in range(1, hpg):
            ps = ps + p_c[:, h * QB:(h + 1) * QB]
        p_sum.append(ps)

    dist_w = (q0 + lax.broadcasted_iota(jnp.int32, (wkeys, QB), 1)) - (
        start + lax.broadcasted_iota(jnp.int32, (wkeys, QB), 0))
    madd_w = _tile_lanes(jnp.where((dist_w >= 0) & (dist_w < WINDOW), 0.0, NEG), hpg)
    ow_t = []
    for g in range(ng):
        p_w = _softmax_cols(s_w[g] + _near_bias(dt_ref, group_heads[g], qi, start // LANE, wkeys // LANE) + madd_w)
        ow_t.append(_dot(vwt_ref[g, :, pl.ds(start, wkeys)], p_w.astype(BF16)))

    per = SEL_BLOCK // CMP_STRIDE
    blk = lax.broadcasted_iota(jnp.int32, (LANE, QB), 0)
    t = q0 + lax.broadcasted_iota(jnp.int32, (LANE, QB), 1)
    tb = t // SEL_BLOCK
    forced = (blk == 0) | (blk == tb) | (blk == tb - 1)
    blk_f = blk.astype(F32)
    scores = []
    for g in range(ng):
        ps_ref = state[g][1]
        ps_ref[0:SUBLANE, :] = jnp.zeros((SUBLANE, QB), F32)
        ps_ref[SUBLANE:SUBLANE + ncp, :] = p_sum[g]
        band = [ps_ref[pl.ds(SUBLANE + r, ns, stride=per), :] for r in range(-1, per)]
        imp = 0.5 * band[0] + band[1] + band[2] + band[3] + 0.5 * band[4]
        if ns < LANE:
            imp = jnp.concatenate([imp, jnp.zeros((LANE - ns, QB), F32)], axis=0)
        score = jnp.where(forced, FORCE, jnp.where(blk * SEL_BLOCK <= t, imp, NEG))
        scores.append(jnp.where(blk < ns, score, -jnp.inf))
    sels = [jnp.zeros((LANE, QB), F32) for _ in range(ng)]
    for _ in range(min(SEL_TOP_N, ns)):
        for g in range(ng):
            mx = jnp.max(scores[g], axis=0, keepdims=True)
            first = jnp.min(jnp.where(scores[g] == mx, blk_f, float(LANE)), axis=0, keepdims=True)
            pick = blk_f == first
            sels[g] = jnp.where(pick, 1.0, sels[g])
            scores[g] = jnp.where(pick, -jnp.inf, scores[g])
    sel_b = [s.astype(BF16) for s in sels]

    kpos = lax.broadcasted_iota(jnp.int32, (KW, QB), 0)
    tq = q0 + lax.broadcasted_iota(jnp.int32, (KW, QB), 1)
    bounded_ok = bound_ref[0] <= SAFE_LOG2_BOUND
    shift = jnp.where(bounded_ok, bound_ref[0], 0.0)
    c_near = jnp.maximum(qi - 1, 0) // (KW // LANE)

    def scores_of(g, c0):
        chosen = _dot(ext_ref[pl.ds(c0, KW), :], sel_b[g])
        return (chosen - 1.0) * (-NEG) - shift, _dot(ks_ref[g, pl.ds(c0, KW), :], q_ts[g])

    def attend(bounded):
        for g in range(ng):
            _flash_init(*state[g][2:5])

        def qk_stage(c, buf):
            c0 = pl.multiple_of(c * KW, KW)
            ahead = []
            for g in range(ng):
                madd, s = scores_of(g, c0)
                s = s + _tile_lanes(madd, hpg)
                state[g][5 + buf][...] = s
                if bounded:
                    ahead.append(s[KW - 8:KW])
                else:
                    ahead.append(jnp.max(s, axis=0, keepdims=True))
                    state[g][7 + buf][...] = ahead[-1]
            return ahead

        def soft_stage(c, buf, ahead):
            c0 = pl.multiple_of(c * KW, KW)
            for g in range(ng):
                m_ref, l_ref, acc_ref = state[g][2:5]
                after = None if ahead is None else ahead[g]
                v_t = vst_ref[g, :, pl.ds(c0, KW)]
                if bounded:
                    _flash_accumulate(state[g][5 + buf][...], v_t, l_ref, acc_ref, after=after)
                else:
                    _flash_update(state[g][5 + buf][...], v_t, m_ref, l_ref, acc_ref,
                                  col_max=state[g][7 + buf][...], after=after)

        def near_step(c, x):
            c0 = pl.multiple_of(c * KW, KW)
            causal = jnp.where(c0 + kpos <= tq, 0.0, NEG)
            scores = []
            for g in range(ng):
                madd, s = scores_of(g, c0)
                scores.append(s + _tile_lanes(madd + causal, hpg)
                              + _near_bias(dt_ref, group_heads[g], qi, c * (KW // LANE), KW // LANE))
            for g in range(ng):
                m_ref, l_ref, acc_ref = state[g][2:5]
                if bounded:
                    _flash_accumulate(scores[g], vst_ref[g, :, pl.ds(c0, KW)], l_ref, acc_ref)
                else:
                    _flash_update(scores[g], vst_ref[g, :, pl.ds(c0, KW)], m_ref, l_ref, acc_ref)
            return x

        _pipelined_chunks(c_near, qk_stage, soft_stage)
        lax.fori_loop(c_near, qi // (KW // LANE) + 1, near_step, 0)
        for g in range(ng):
            m_ref, l_ref, acc_ref = state[g][2:5]
            acc_ref[...] = _sum_result(l_ref, acc_ref) if bounded else _flash_result(m_ref, l_ref, acc_ref)

    pl.when(bounded_ok)(lambda: attend(True))
    pl.when(jnp.logical_not(bounded_ok))(lambda: attend(False))

    for g in range(ng):
        os_t = state[g][4][...]
        gates = jax.nn.sigmoid(gt_ref[g])
        for h in range(hpg):
            sl = slice(h * QB, (h + 1) * QB)
            o_t = (gates[3 * h:3 * h + 1] * oc_t[g][:, sl] + gates[3 * h + 1:3 * h + 2] * os_t[:, sl]
                   + gates[3 * h + 2:3 * h + 3] * ow_t[g][:, sl])
            hh = group_heads[g][h]
            o_ref[:, hh * HEAD_DIM:(hh + 1) * HEAD_DIM] = o_t.T.astype(o_ref.dtype)


def nsa_attention(logit_bound, q_t, gates_t, kc, vc_t, k_sw, v_sw_t, dt, dc, bsz, seq):
    nq = seq // QB
    ncp = seq // CMP_STRIDE
    nc = ncp - 1
    ns = seq // SEL_BLOCK
    assert ns <= LANE and seq >= WINDOW + QB and seq % KW == 0
    assert CMP_BLOCK == 2 * CMP_STRIDE and SEL_BLOCK == 4 * CMP_STRIDE
    expand =((np.arange(seq)[:, None] // SEL_BLOCK) == np.arange(LANE)[None, :]).astype(np.float32)
    ng = NSA_GROUPS
    once = dict(pipeline_mode=pl.Buffered(1))
    ks_spec = pl.BlockSpec((ng, seq, HEAD_DIM), lambda b, i: (0, b, 0), **once)
    kw_spec = pl.BlockSpec((ng, seq, HEAD_DIM), lambda b, i: (1, b, 0), **once)
    vs_spec = pl.BlockSpec((ng, HEAD_DIM, seq), lambda b, i: (0, 0, b), **once)
    vw_spec = pl.BlockSpec((ng, HEAD_DIM, seq), lambda b, i: (1, 0, b), **once)
    lanes = NSA_HPG * QB
    group_state = [pltpu.VMEM((ncp + CWIN, lanes), F32), pltpu.VMEM((ncp + SUBLANE, QB), F32),
                   pltpu.VMEM((1, lanes), F32), pltpu.VMEM((1, lanes), F32), pltpu.VMEM((HEAD_DIM, lanes), F32),
                   pltpu.VMEM((KW, lanes), F32), pltpu.VMEM((KW, lanes), F32),
                   pltpu.VMEM((1, lanes), F32), pltpu.VMEM((1, lanes), F32)]
    assert len(group_state) == NSA_STATE
    return pl.pallas_call(
        functools.partial(_nsa_kernel, seq=seq, nc=nc),
        grid=(bsz, nq),
        in_specs=[pl.BlockSpec(memory_space=pltpu.SMEM),
                  pl.BlockSpec((NSA_HEADS, HEAD_DIM, QB), lambda b, i: (0, 0, b * nq + i)),
                  pl.BlockSpec((ng, GATE_ROWS, QB), lambda b, i: (0, 0, b * nq + i)),
                  pl.BlockSpec((ng, ncp, HEAD_DIM), lambda b, i: (0, b, 0)),
                  pl.BlockSpec((ng, HEAD_DIM, ncp), lambda b, i: (0, 0, b)),
                  ks_spec, vs_spec, kw_spec, vw_spec,
                  pl.BlockSpec((NSA_HEADS, 3, LANE, LANE), lambda b, i: (0, 0, 0, 0)),
                  pl.BlockSpec((NSA_HEADS, CWIN, LANE), lambda b, i: (0, 0, 0)),
                  pl.BlockSpec((seq, LANE), lambda b, i: (0, 0))],
        out_specs=pl.BlockSpec((QB, NSA_HEADS * HEAD_DIM), lambda b, i: (b * nq + i, 0)),
        out_shape=jax.ShapeDtypeStruct((bsz * seq, NSA_HEADS * HEAD_DIM), BF16),
        scratch_shapes=group_state * ng,
        compiler_params=_cparams(("arbitrary", "arbitrary")),
        name="nsa_attention",
    )(jnp.reshape(logit_bound, (1,)).astype(F32), q_t, gates_t, kc, vc_t, k_sw, v_sw_t, k_sw, v_sw_t, dt, dc,
      jnp.asarray(expand, BF16))


MLA_HPS = 2


def _mla_kernel(bound_ref, qt_ref, k_ref, vt_ref, o_ref, *scratch):
    qi = pl.program_id(2)
    chains = [scratch[3 * h:3 * h + 3] for h in range(MLA_HPS)]
    sbuf = [scratch[(3 + b) * MLA_HPS:(4 + b) * MLA_HPS] for b in range(2)]
    cbuf = [scratch[(5 + b) * MLA_HPS:(6 + b) * MLA_HPS] for b in range(2)]
    c_diag = pl.multiple_of(qi * KW, KW)
    kpos = lax.broadcasted_iota(jnp.int32, (KW, KW), 0)
    tq = lax.broadcasted_iota(jnp.int32, (KW, KW), 1)
    dv = vt_ref.shape[1]

    def attend(bounded):
        for ch in chains:
            _flash_init(*ch)

        def qk_stage(c, buf):
            c0 = pl.multiple_of(c * KW, KW)
            ahead = []
            for h in range(MLA_HPS):
                s = _dot(k_ref[h, pl.ds(c0, KW), :], qt_ref[h])
                sbuf[buf][h][...] = s
                if bounded:
                    ahead.append(s[KW - 8:KW])
                else:
                    ahead.append(jnp.max(s, axis=0, keepdims=True))
                    cbuf[buf][h][...] = ahead[-1]
            return ahead

        def soft_stage(c, buf, ahead):
            c0 = pl.multiple_of(c * KW, KW)
            for h, (m_ref, l_ref, acc_ref) in enumerate(chains):
                after = None if ahead is None else ahead[h]
                if bounded:
                    _flash_accumulate(sbuf[buf][h][...], vt_ref[h, :, pl.ds(c0, KW)], l_ref, acc_ref, after=after)
                else:
                    _flash_update(sbuf[buf][h][...], vt_ref[h, :, pl.ds(c0, KW)], m_ref, l_ref, acc_ref,
                                  col_max=cbuf[buf][h][...], after=after)

        _pipelined_chunks(qi, qk_stage, soft_stage)
        causal = jnp.where(kpos <= tq, 0.0, NEG)
        scores = [_dot(k_ref[h, pl.ds(c_diag, KW), :], qt_ref[h]) + causal for h in range(MLA_HPS)]
        for h, (m_ref, l_ref, acc_ref) in enumerate(chains):
            if bounded:
                _flash_accumulate(scores[h], vt_ref[h, :, pl.ds(c_diag, KW)], l_ref, acc_ref)
                o_t = _sum_result(l_ref, acc_ref)
            else:
                _flash_update(scores[h], vt_ref[h, :, pl.ds(c_diag, KW)], m_ref, l_ref, acc_ref)
                o_t = _flash_result(m_ref, l_ref, acc_ref)
            o_ref[:, h * dv:(h + 1) * dv] = o_t.T.astype(o_ref.dtype)

    bounded_ok = bound_ref[0] <= SAFE_LOG2_BOUND
    pl.when(bounded_ok)(lambda: attend(True))
    pl.when(jnp.logical_not(bounded_ok))(lambda: attend(False))


def mla_attention(logit_bound, q_t, k, v_t, bsz, seq):
    nh, dqk, _ = q_t.shape
    dv = v_t.shape[1]
    nq = seq // KW
    hps = MLA_HPS
    state = [pltpu.VMEM((1, KW), F32), pltpu.VMEM((1, KW), F32), pltpu.VMEM((dv, KW), F32)] * hps
    state += [pltpu.VMEM((KW, KW), F32)] * (2 * hps)
    state += [pltpu.VMEM((1, KW), F32)] * (2 * hps)
    return pl.pallas_call(
        _mla_kernel,
        grid=(bsz, nh // hps, nq),
        in_specs=[pl.BlockSpec(memory_space=pltpu.SMEM),
                  pl.BlockSpec((hps, dqk, KW), lambda b, h, i: (h, 0, b * nq + i)),
                  pl.BlockSpec((hps, seq, dqk), lambda b, h, i: (h, b, 0)),
                  pl.BlockSpec((hps, dv, seq), lambda b, h, i: (h, 0, b))],
        out_specs=pl.BlockSpec((KW, hps * dv), lambda b, h, i: (b * nq + i, h)),
        out_shape=jax.ShapeDtypeStruct((bsz * seq, nh * dv), BF16),
        scratch_shapes=state,
        compiler_params=_cparams(("arbitrary", "arbitrary", "arbitrary")),
        name="mla_attention",
    )(jnp.reshape(logit_bound, (1,)).astype(F32), q_t, k, v_t)


INT_MIN = -2 ** 31
NEG_KEY = int(np.array(NEG, np.float32).view(np.int32)) ^ 0x7FFFFFFF
KEY_BITS = 32
SURE_BITS = 22


def _sort_key(x):
    bits = pltpu.bitcast(x + 0.0, jnp.int32)
    return jnp.where(bits < 0, bits ^ 0x7FFFFFFF, bits)


def _dsa_kernel(bound_ref, iqt_ref, iwt_ref, ik_ref, qt_ref, k_ref, vt_ref, dt_ref, o_ref,
                key_ref, *state, seq, k_sel):
    qi = pl.program_id(1)
    q0 = qi * QB
    n_chunk = (q0 + QB + KW - 1) // KW
    n_rest = seq - n_chunk * KW
    kpos = lax.broadcasted_iota(jnp.int32, (KW, QB), 0)
    tq = q0 + lax.broadcasted_iota(jnp.int32, (KW, QB), 1)
    hpp = KW // QB

    def score_chunk(c, x):
        c0 = pl.multiple_of(c * KW, KW)
        ikc = ik_ref[pl.ds(c0, KW), :]
        acc = jnp.zeros((KW, QB), F32)
        for piece in range(IDX_HEADS // hpp):
            sl = slice(piece * KW, (piece + 1) * KW)
            s = jnp.maximum(_dot(ikc, iqt_ref[0, :, sl]), 0.0) * iwt_ref[0, :, sl]
            for j in range(hpp):
                acc = acc + s[:, j * QB:(j + 1) * QB]
        acc = jnp.where(c0 + kpos <= tq, acc, NEG)
        key_ref[pl.ds(c0, KW), :] = _sort_key(acc)
        return x

    lax.fori_loop(0, n_chunk, score_chunk, 0)

    def count(pred):
        def chunk_hits(c):
            c0 = pl.multiple_of(c * KW, KW)
            hit = jnp.where(pred(key_ref[pl.ds(c0, KW), :], c0), 1.0, 0.0)
            parts = [hit[SUBLANE * i:SUBLANE * (i + 1)] for i in range(KW // SUBLANE)]
            while len(parts) > 1:
                parts = [parts[i] + parts[i + 1] for i in range(0, len(parts), 2)]
            return parts[0]

        def body(j, acc):
            second = 2 * j + 1
            weight = jnp.where(second < n_chunk, 1.0, 0.0)
            return acc + chunk_hits(2 * j) + chunk_hits(jnp.minimum(second, n_chunk - 1)) * weight

        acc = lax.fori_loop(0, (n_chunk + 1) // 2, body, jnp.zeros((SUBLANE, QB), F32))
        return jnp.sum(acc, axis=0, keepdims=True)

    rest = n_rest.astype(F32)
    kf = float(k_sel)

    def bit_step(i, st):
        u, thr_s, settled = st
        bit = jnp.left_shift(jnp.int32(1), KEY_BITS - 1 - i)
        trial = (u | bit) ^ INT_MIN
        cnt = count(lambda keys, c0: keys >= trial) + jnp.where(NEG_KEY >= trial, rest, 0.0)
        new = (cnt == kf) & (settled < 0.5)
        return (jnp.where(cnt >= kf, u | bit, u), jnp.where(new, trial, thr_s), jnp.where(new, 1.0, settled))

    st = (jnp.zeros((1, QB), jnp.int32), jnp.zeros((1, QB), jnp.int32), jnp.zeros((1, QB), F32))
    st = lax.fori_loop(0, SURE_BITS, bit_step, st)
    _, (u, thr_s, settled) = lax.while_loop(
        lambda c: (c[0] < KEY_BITS) & (jnp.min(c[1][2]) < 0.5),
        lambda c: (c[0] + 1, bit_step(c[0], c[1])), (jnp.int32(SURE_BITS), st))
    is_settled = settled > 0.5
    thr = jnp.where(is_settled, thr_s, u ^ INT_MIN)

    def edge_counts():
        return (count(lambda keys, c0: keys > thr) + jnp.where(NEG_KEY > thr, rest, 0.0),
                count(lambda keys, c0: keys >= thr) + jnp.where(NEG_KEY >= thr, rest, 0.0))

    zero_cnt = jnp.zeros((1, QB), F32)
    cnt_gt, cnt_ge = lax.cond(jnp.min(settled) > 0.5, lambda: (zero_cnt, zero_cnt), edge_counts)
    need = kf - cnt_gt
    tie_q = (cnt_ge > kf) & (thr != NEG_KEY) & jnp.logical_not(is_settled)
    idx_bits = (seq - 1).bit_length()
    no_cut = 2 ** 30

    def tie_cut():
        def idx_step(i, x):
            bit = jnp.left_shift(jnp.int32(1), idx_bits - 1 - i)
            trial = x | bit
            f = count(lambda keys, c0: (keys == thr) & (c0 + kpos < trial))
            return jnp.where(f <= need - 1.0, trial, x)
        return lax.fori_loop(0, idx_bits, idx_step, jnp.zeros((1, QB), jnp.int32))

    any_tie = jnp.max(jnp.where(tie_q, 1.0, 0.0)) > 0.0
    x_cut = lax.cond(any_tie, tie_cut, lambda: jnp.full((1, QB), no_cut, jnp.int32))
    x_cut = jnp.where(tie_q, x_cut, no_cut)

    bounded_ok = bound_ref[0] <= SAFE_LOG2_BOUND
    shift = jnp.where(bounded_ok, bound_ref[0], 0.0)

    def mask_add(c0):
        keys = key_ref[pl.ds(c0, KW), :]
        pos = c0 + kpos
        chosen = (keys > thr) | ((keys == thr) & (pos <= x_cut))
        return _tile_lanes(jnp.where(chosen & (pos <= tq), -shift, NEG), DSA_HPG)

    c_near = jnp.maximum(qi - 1, 0) // (KW // LANE)
    ng = DSA_KV_HEADS
    chains = [state[3 * g:3 * g + 3] for g in range(ng)]
    sbuf = [state[(3 + b) * ng:(4 + b) * ng] for b in range(2)]
    cbuf = [state[(5 + b) * ng:(6 + b) * ng] for b in range(2)]
    group_heads = [[g * DSA_HPG + h for h in range(DSA_HPG)] for g in range(ng)]

    def raw_scores(c0, g):
        q_t = jnp.concatenate([qt_ref[h] for h in group_heads[g]], axis=1)
        return _dot(k_ref[g, pl.ds(c0, KW), :], q_t)

    def attend(bounded):
        for ch in chains:
            _flash_init(*ch)

        def qk_stage(c, buf):
            c0 = pl.multiple_of(c * KW, KW)
            madd = mask_add(c0)
            ahead = []
            for g in range(ng):
                s = raw_scores(c0, g) + madd
                sbuf[buf][g][...] = s
                if bounded:
                    ahead.append(s[KW - 8:KW])
                else:
                    ahead.append(jnp.max(s, axis=0, keepdims=True))
                    cbuf[buf][g][...] = ahead[-1]
            return ahead

        def soft_stage(c, buf, ahead):
            c0 = pl.multiple_of(c * KW, KW)
            for g, (m_ref, l_ref, acc_ref) in enumerate(chains):
                after = None if ahead is None else ahead[g]
                if bounded:
                    _flash_accumulate(sbuf[buf][g][...], vt_ref[g, :, pl.ds(c0, KW)], l_ref, acc_ref, after=after)
                else:
                    _flash_update(sbuf[buf][g][...], vt_ref[g, :, pl.ds(c0, KW)], m_ref, l_ref, acc_ref,
                                  col_max=cbuf[buf][g][...], after=after)

        _pipelined_chunks(c_near, qk_stage, soft_stage)

        def near_step(c, x):
            c0 = pl.multiple_of(c * KW, KW)
            madd = mask_add(c0)
            scores = [raw_scores(c0, g) + madd
                      + _near_bias(dt_ref, group_heads[g], qi, c * (KW // LANE), KW // LANE)
                      for g in range(ng)]
            for g, (m_ref, l_ref, acc_ref) in enumerate(chains):
                if bounded:
                    _flash_accumulate(scores[g], vt_ref[g, :, pl.ds(c0, KW)], l_ref, acc_ref)
                else:
                    _flash_update(scores[g], vt_ref[g, :, pl.ds(c0, KW)], m_ref, l_ref, acc_ref)
            return x

        lax.fori_loop(c_near, n_chunk, near_step, 0)
        for g, (m_ref, l_ref, acc_ref) in enumerate(chains):
            o_t = _sum_result(l_ref, acc_ref) if bounded else _flash_result(m_ref, l_ref, acc_ref)
            for h in range(DSA_HPG):
                hh = group_heads[g][h]
                o_ref[:, hh * HEAD_DIM:(hh + 1) * HEAD_DIM] = o_t[:, h * QB:(h + 1) * QB].T.astype(o_ref.dtype)

    pl.when(bounded_ok)(lambda: attend(True))
    pl.when(jnp.logical_not(bounded_ok))(lambda: attend(False))


def _idx_prep_kernel(p_ref, c_ref, sa_ref, sb_ref, iqt_ref, ik_ref, iwt_ref, *, ntile):
    nslab_q = IDX_HEADS * IDX_DIM // LANE
    per = LANE // IDX_DIM
    half = IDX_ROPE // 2
    zrows = jnp.zeros((LANE - IDX_DIM, QB), F32)

    def rope_slab(x, c, sa, sb):
        return x * c + pltpu.roll(x, LANE - half, axis=1) * sa + pltpu.roll(x, half, axis=1) * sb

    for t in range(ntile):
        rows = slice(t * QB, (t + 1) * QB)
        c, sa, sb = c_ref[rows, :], sa_ref[rows, :], sb_ref[rows, :]
        cols = []
        for s in range(nslab_q):
            x_t = (rope_slab(p_ref[s, rows, :], c, sa, sb) * IDX_DIM ** -0.5).T
            for j in range(per):
                cols.append(jnp.concatenate([x_t[j * IDX_DIM:(j + 1) * IDX_DIM], zrows], axis=0))
        iqt_ref[t] = jnp.concatenate(cols, axis=1).astype(iqt_ref.dtype)
        tail = p_ref[nslab_q, rows, :]
        lane = lax.broadcasted_iota(jnp.int32, (QB, LANE), 1)
        ik_ref[rows, :] = jnp.where(lane < IDX_DIM, rope_slab(tail, c, sa, sb), 0.0).astype(ik_ref.dtype)
        w_t = (tail * IDX_HEADS ** -0.5).T
        iwt_ref[t] = jnp.concatenate([w_t[IDX_DIM + h:IDX_DIM + h + 1, :] for h in range(IDX_HEADS)], axis=1)


def indexer_operands(proj, seq, tm=512):
    _, m, _ = proj.shape
    ntile = tm // QB
    tps = seq // tm
    cos, sin = _rope_tables(seq, IDX_ROPE)
    zero = jnp.zeros_like(sin)
    rest = IDX_DIM - IDX_ROPE
    per = LANE // IDX_DIM
    c_tab = jnp.tile(jnp.concatenate([cos, cos, jnp.ones((seq, rest), F32)], axis=1), (1, per))
    sa_tab = jnp.tile(jnp.concatenate([-sin, zero, jnp.zeros((seq, rest), F32)], axis=1), (1, per))
    sb_tab = jnp.tile(jnp.concatenate([zero, sin, jnp.zeros((seq, rest), F32)], axis=1), (1, per))
    lanes = IDX_HEADS * QB
    tab_spec = pl.BlockSpec((tm, LANE), lambda i: (i % tps, 0))
    return pl.pallas_call(
        functools.partial(_idx_prep_kernel, ntile=ntile),
        grid=(m // tm,),
        in_specs=[pl.BlockSpec((proj.shape[0], tm, LANE), lambda i: (0, i, 0)), tab_spec, tab_spec, tab_spec],
        out_specs=[pl.BlockSpec((ntile, LANE, lanes), lambda i: (i, 0, 0)),
                   pl.BlockSpec((tm, LANE), lambda i: (i, 0)),
                   pl.BlockSpec((ntile, 1, lanes), lambda i: (i, 0, 0))],
        out_shape=[jax.ShapeDtypeStruct((m // QB, LANE, lanes), BF16),
                   jax.ShapeDtypeStruct((m, LANE), BF16),
                   jax.ShapeDtypeStruct((m // QB, 1, lanes), F32)],
        compiler_params=_cparams(("arbitrary",)),
        name="dsa_indexer_operands",
    )(proj, c_tab, sa_tab, sb_tab)


def dsa_attention(logit_bound, iq_t, iw_t, ik, q_t, k, v_t, dt, bsz, seq):
    nq = seq // QB
    k_sel = min(DSA_TOPK_MAX, seq // 4)
    assert seq % KW == 0
    lanes = DSA_HPG * QB
    return pl.pallas_call(
        functools.partial(_dsa_kernel, seq=seq, k_sel=k_sel),
        grid=(bsz, nq),
        in_specs=[pl.BlockSpec(memory_space=pltpu.SMEM),
                  pl.BlockSpec((1, LANE, IDX_HEADS * QB), lambda b, i: (b * nq + i, 0, 0)),
                  pl.BlockSpec((1, 1, IDX_HEADS * QB), lambda b, i: (b * nq + i, 0, 0)),
                  pl.BlockSpec((seq, LANE), lambda b, i: (b, 0)),
                  pl.BlockSpec((DSA_HEADS, HEAD_DIM, QB), lambda b, i: (0, 0, b * nq + i)),
                  pl.BlockSpec((DSA_KV_HEADS, seq, HEAD_DIM), lambda b, i: (0, b, 0),
                               pipeline_mode=pl.Buffered(1)),
                  pl.BlockSpec((DSA_KV_HEADS, HEAD_DIM, seq), lambda b, i: (0, 0, b),
                               pipeline_mode=pl.Buffered(1)),
                  pl.BlockSpec((DSA_HEADS, 3, LANE, LANE), lambda b, i: (0, 0, 0, 0),
                               pipeline_mode=pl.Buffered(1))],
        out_specs=pl.BlockSpec((QB, DSA_HEADS * HEAD_DIM), lambda b, i: (b * nq + i, 0)),
        out_shape=jax.ShapeDtypeStruct((bsz * seq, DSA_HEADS * HEAD_DIM), BF16),
        scratch_shapes=[pltpu.VMEM((seq, QB), jnp.int32)]
        + [pltpu.VMEM((1, lanes), F32), pltpu.VMEM((1, lanes), F32),
           pltpu.VMEM((HEAD_DIM, lanes), F32)] * DSA_KV_HEADS
        + [pltpu.VMEM((KW, lanes), F32)] * (2 * DSA_KV_HEADS)
        + [pltpu.VMEM((1, lanes), F32)] * (2 * DSA_KV_HEADS),
        compiler_params=_cparams(("arbitrary", "arbitrary")),
        name="dsa_attention",
    )(jnp.reshape(logit_bound, (1,)).astype(F32), iq_t, iw_t, ik, q_t, k, v_t, dt)


def _rope_tables(seq, dim):
    half = dim // 2
    inv = ROPE_THETA ** (-jnp.arange(half, dtype=F32) / half)
    ang = jnp.arange(seq, dtype=F32)[:, None] * inv[None, :]
    return jnp.cos(ang), jnp.sin(ang)


def _logit_bound(gq, gk, dim, scale):
    return dim * scale * jnp.max(jnp.abs(gq)) * jnp.max(jnp.abs(gk)) * (1.0 + 2.0 ** -7)


def _pad_cols(w, n):
    return jnp.pad(w, ((0, 0), (0, n - w.shape[1])))


def _t(x):
    return jnp.swapaxes(x, -1, -2)


def _even_mixer(h, x2, gate, dt, dc, bias_bound, bsz, seq, w_in, w_out, nsa_qk_g, cmp_pe, cmp_w1, cmp_b1,
                cmp_w2, cmp_b2, q_norm_g, kv_norm_g, w_uq, w_ukv, nope_g, rope_g):
    m = bsz * seq
    nq_cols = NSA_HEADS * HEAD_DIM
    nkv_cols = 6 * NSA_GROUPS * HEAD_DIM
    ngate = 3 * NSA_HEADS
    o_gate = nq_cols + nkv_cols
    o_cq = o_gate + ngate
    o_ckv = o_cq + MLA_Q_RANK
    o_kpe = o_ckv + MLA_KV_RANK
    gw = NSA_GROUPS * HEAD_DIM
    kvw = [w_in[:, nq_cols + i * gw:nq_cols + (i + 1) * gw] for i in range(6)]
    scale = HEAD_DIM ** -0.5 * LOG2E
    q_t = proj_heads(h, w_in[:, :nq_cols].astype(BF16), nsa_qk_g[0] * scale, transpose=True)
    k_sw = proj_heads(h, jnp.concatenate([kvw[2], kvw[4]], axis=1).astype(BF16), nsa_qk_g[1],
                      transpose=False)
    v_sw_t = proj_heads(h, jnp.concatenate([kvw[3], kvw[5]], axis=1).astype(BF16), transpose=True)
    tail = jnp.concatenate([w_in[:, o_kpe:], w_in[:, o_gate:o_cq]], axis=1)
    w_r = jnp.concatenate([kvw[0], kvw[1], w_in[:, o_cq:o_kpe], _pad_cols(tail, LANE)], axis=1).astype(BF16)
    proj = proj_slabs(h, w_r, tn=w_r.shape[1])
    s_cq = 2 * NSA_GROUPS
    s_ckv = s_cq + MLA_Q_RANK // LANE
    s_tail = s_ckv + MLA_KV_RANK // LANE
    kvc = compress_kv(proj, 0, bsz, seq, cmp_pe, cmp_w1, cmp_b1, cmp_w2, cmp_b2, nsa_qk_g[1])
    tail_v = proj[s_tail]
    gates = tail_v[:, MLA_ROPE:MLA_ROPE + ngate].reshape(m, NSA_GROUPS, 3 * NSA_HPG)
    gates_t = jnp.pad(jnp.transpose(gates, (1, 2, 0)), ((0, 0), (0, GATE_ROWS - 3 * NSA_HPG), (0, 0)))
    nsa_bound = _logit_bound(nsa_qk_g[0], nsa_qk_g[1], HEAD_DIM, scale) + bias_bound
    o_nsa = nsa_attention(nsa_bound, q_t, gates_t, kvc[0], _t(kvc[1]), k_sw, v_sw_t,
                          dt[:NSA_HEADS], dc[:NSA_HEADS], bsz, seq)

    dq = MLA_NOPE + MLA_ROPE
    wq = w_uq.reshape(MLA_Q_RANK, MLA_HEADS, dq)
    wq_r = jnp.concatenate([wq[:, :, :MLA_NOPE].reshape(MLA_Q_RANK, -1),
                            wq[:, :, MLA_NOPE:].reshape(MLA_Q_RANK, -1)], axis=1).astype(BF16)
    cos, sin = _rope_tables(seq, MLA_ROPE)
    mscale = dq ** -0.5 * LOG2E
    side = [jnp.sqrt(MLA_NOPE * jnp.max(jnp.abs(nope_g[i])) ** 2 + MLA_ROPE * jnp.max(jnp.abs(rope_g[i])) ** 2)
            for i in range(2)]
    mla_bound = mscale * side[0] * side[1] * (1.0 + 2.0 ** -7)
    mla_shift = jnp.where(mla_bound <= SAFE_LOG2_BOUND, mla_bound, 0.0)
    q_mla_t, k_mla, v_mla_t = mla_project(proj, s_cq, s_ckv, s_tail, seq, q_norm_g, kv_norm_g, wq_r,
                                          w_ukv.astype(BF16), nope_g, rope_g, cos, sin, mscale, mla_shift)
    o_mla = mla_attention(mla_bound, q_mla_t, k_mla, v_mla_t, bsz, seq)
    w_o = w_out.astype(BF16)
    return resproj([(o_nsa, w_o[:nq_cols]), (o_mla, w_o[nq_cols:])], x2, gate, seq)


def _odd_mixer(h, x2, gate, dt, bias_bound, bsz, seq, w_in, w_out, qk_g):
    nq = DSA_HEADS * HEAD_DIM
    nkv = DSA_KV_HEADS * HEAD_DIM
    niq = IDX_HEADS * IDX_DIM
    o_k, o_v, o_iq = nq, nq + nkv, nq + 2 * nkv
    q_t = proj_heads(h, w_in[:, :o_k].astype(BF16), qk_g[0] * (HEAD_DIM ** -0.5 * LOG2E), transpose=True)
    k = proj_heads(h, w_in[:, o_k:o_v].astype(BF16), qk_g[1], transpose=False)
    v_t = proj_heads(h, w_in[:, o_v:o_iq].astype(BF16), transpose=True)
    w_idx = w_in[:, o_iq:]
    proj = proj_slabs(h, _pad_cols(w_idx, niq + LANE).astype(BF16), tn=niq + LANE)
    iq_t, ik, iw_t = indexer_operands(proj, seq)
    bound = _logit_bound(qk_g[0], qk_g[1], HEAD_DIM, HEAD_DIM ** -0.5 * LOG2E) + bias_bound
    o = dsa_attention(bound, iq_t, iw_t, ik, q_t, k, v_t, dt, bsz, seq)
    return resproj([(o, w_out.astype(BF16))], x2, gate, seq)


def _conv_ffn(h, x2, gate, seq, w_up_all, layer, conv_w, conv_b, w_down):
    a = ffn_up(h, w_up_all, layer, conv_w, conv_b, seq)
    return resproj([(a, w_down.astype(BF16))], x2, gate, seq)


def kernel(x, c, rel_bias, ada_w, ada_b, norm_g, ev_w_in, ev_w_out, nsa_qk_g, cmp_pe, cmp_w1, cmp_b1, cmp_w2, cmp_b2, mla_q_norm_g, mla_kv_norm_g, mla_w_uq, mla_w_ukv, mla_nope_g, mla_rope_g, od_w_in, od_w_out, dsa_qk_g, ffn_w_up, ffn_conv_w, ffn_conv_b, ffn_w_down):
    bsz, seq, d = x.shape
    depth = ada_w.shape[0]
    x2 = x.reshape(bsz * seq, d)
    mods = ada_all(c, ada_w, ada_b)
    dt, dc = bias_tiles(rel_bias)
    bias_bound = 2.0 * LOG2E * jnp.max(jnp.abs(rel_bias))
    for i in range(depth):
        j = i // 2
        shift, scale, gate = jnp.split(mods[i, 0], 3, axis=-1)
        h = modnorm(x2, norm_g[i, 0], scale, shift, seq)
        if i % 2 == 0:
            x2 = _even_mixer(h, x2, gate, dt, dc, bias_bound, bsz, seq, ev_w_in[j], ev_w_out[j], nsa_qk_g[j],
                             cmp_pe[j], cmp_w1[j], cmp_b1[j], cmp_w2[j], cmp_b2[j], mla_q_norm_g[j],
                             mla_kv_norm_g[j], mla_w_uq[j], mla_w_ukv[j], mla_nope_g[j], mla_rope_g[j])
        else:
            x2 = _odd_mixer(h, x2, gate, dt, bias_bound, bsz, seq, od_w_in[j], od_w_out[j], dsa_qk_g[j])
        shift, scale, gate = jnp.split(mods[i, 1], 3, axis=-1)
        h = modnorm(x2, norm_g[i, 1], scale, shift, seq)
        x2 = _conv_ffn(h, x2, gate, seq, ffn_w_up, i, ffn_conv_w[i], ffn_conv_b[i], ffn_w_down[i])
    return x2.reshape(bsz, seq, d)
```

```python
import functools
import math

import numpy as np
import jax
import jax.numpy as jnp
from jax import lax
from jax.experimental import pallas as pl
from jax.experimental.pallas import tpu as pltpu

HEAD_DIM = 128
NSA_HEADS = 8
NSA_GROUPS = 2
NSA_HPG = NSA_HEADS // NSA_GROUPS
CMP_BLOCK = 32
CMP_STRIDE = 16
CMP_HIDDEN = 256
SEL_BLOCK = 64
SEL_TOP_N = 16
WINDOW = 512
MLA_HEADS = 8
MLA_Q_RANK = 512
MLA_KV_RANK = 256
MLA_NOPE = 128
MLA_ROPE = 64
MLA_V = 128
DSA_HEADS = 16
DSA_KV_HEADS = 4
DSA_HPG = DSA_HEADS // DSA_KV_HEADS
IDX_HEADS = 16
IDX_DIM = 64
IDX_ROPE = 32
DSA_TOPK_MAX = 256
REL_BUCKETS = 32
REL_MAX_DIST = 128
CONV_WIDTH = 3
ROPE_THETA = 10000.0
EPS = 1e-6
NEG = -1e30
FORCE = 1e9

LANE = 128
SUBLANE = 8
QB = 128
VMEM_LIMIT = 56 * 1024 * 1024

F32 = jnp.float32
BF16 = jnp.bfloat16


def _t5_thresholds():
    d = np.arange(0, 4 * REL_MAX_DIST)
    half = REL_BUCKETS // 2
    val = np.log(np.maximum(d, 1) / half) / math.log(REL_MAX_DIST / half) * (REL_BUCKETS - half)
    large = np.minimum(half + np.floor(np.maximum(val, 0.0)).astype(np.int64), REL_BUCKETS - 1)
    bucket = np.where(d < half, d, large)
    return [int(np.argmax(bucket >= b)) for b in range(1, REL_BUCKETS)]


T5_THR = _t5_thresholds()
T5_FAR = T5_THR[-1]
assert T5_FAR <= LANE


def _cparams(sem):
    return pltpu.CompilerParams(dimension_semantics=sem, vmem_limit_bytes=VMEM_LIMIT)


def _dot(a, b):
    return jnp.dot(a, b, preferred_element_type=F32)


def _ada_kernel(c_ref, w_ref, b_ref, o_ref):
    c = c_ref[...]
    a = c * jax.nn.sigmoid(c)
    o_ref[0] = jnp.dot(a, w_ref[0], preferred_element_type=F32,
                       precision=lax.Precision.HIGHEST) + b_ref[0]


def ada_all(c, ada_w, ada_b):
    depth, two, d, n3 = ada_w.shape
    bsz = c.shape[0]
    rows = SUBLANE
    cp = jnp.zeros((rows, d), F32).at[:bsz].set(c)
    w = ada_w.reshape(depth * two, d, n3)
    b = ada_b.reshape(depth * two, 1, n3)
    tn = 512
    out = pl.pallas_call(
        _ada_kernel,
        grid=(depth * two, n3 // tn),
        in_specs=[pl.BlockSpec((rows, d), lambda l, j: (0, 0)),
                  pl.BlockSpec((1, d, tn), lambda l, j: (l, 0, j)),
                  pl.BlockSpec((1, 1, tn), lambda l, j: (l, 0, j))],
        out_specs=pl.BlockSpec((1, rows, tn), lambda l, j: (l, 0, j)),
        out_shape=jax.ShapeDtypeStruct((depth * two, rows, n3), F32),
        compiler_params=_cparams(("arbitrary", "arbitrary")),
        name="ada_mod",
    )(cp, w, b)
    return out[:, :bsz].reshape(depth, two, bsz, n3)


def _modnorm_kernel(x_ref, g_ref, sc_ref, sh_ref, o_ref):
    x = x_ref[...]
    y = x * lax.rsqrt(jnp.mean(x * x, axis=-1, keepdims=True) + EPS)
    h = (y * g_ref[...]) * (1.0 + sc_ref[0]) + sh_ref[0]
    o_ref[...] = h.astype(o_ref.dtype)


def modnorm(x2, g, scale, shift, seq, tm=1024):
    m, d = x2.shape
    tpb = seq // tm
    return pl.pallas_call(
        _modnorm_kernel,
        grid=(m // tm,),
        in_specs=[pl.BlockSpec((tm, d), lambda i: (i, 0)),
                  pl.BlockSpec((1, d), lambda i: (0, 0)),
                  pl.BlockSpec((1, 1, d), lambda i: (i // tpb, 0, 0)),
                  pl.BlockSpec((1, 1, d), lambda i: (i // tpb, 0, 0))],
        out_specs=pl.BlockSpec((tm, d), lambda i: (i, 0)),
        out_shape=jax.ShapeDtypeStruct((m, d), BF16),
        compiler_params=_cparams(("arbitrary",)),
        name="modnorm",
    )(x2, g.reshape(1, d), scale.reshape(-1, 1, d), shift.reshape(-1, 1, d))


def _head_norm(y, g_ref):
    return y * lax.rsqrt(jnp.mean(y * y, axis=-1, keepdims=True) + EPS) * g_ref[...]


def _proj_qkvr_kernel(x_ref, wq_ref, wk_ref, wv_ref, wr_ref, gq_ref, gk_ref, oq_ref, ok_ref, ov_ref, or_ref):
    x = x_ref[...]
    acc = _dot(x, wq_ref[...])
    for s in range(oq_ref.shape[0]):
        oq_ref[s] = _head_norm(acc[:, s * LANE:(s + 1) * LANE], gq_ref).T.astype(oq_ref.dtype)
    acc = _dot(x, wk_ref[...])
    for s in range(ok_ref.shape[0]):
        ok_ref[s] = _head_norm(acc[:, s * LANE:(s + 1) * LANE], gk_ref).astype(ok_ref.dtype)
    acc = _dot(x, wv_ref[...])
    for s in range(ov_ref.shape[0]):
        ov_ref[s] = acc[:, s * LANE:(s + 1) * LANE].T.astype(ov_ref.dtype)
    acc = _dot(x, wr_ref[...])
    for s in range(or_ref.shape[0]):
        or_ref[s] = acc[:, s * LANE:(s + 1) * LANE]


def proj_qkv_raw(x, wq, g_q, wk, g_k, wv, wr, tm=512):
    m, k = x.shape
    nq, nk, nv, nr = (w.shape[1] // LANE for w in (wq, wk, wv, wr))
    whole = lambda w: pl.BlockSpec(w.shape, lambda i: (0, 0), pipeline_mode=pl.Buffered(1))
    gain = pl.BlockSpec((1, LANE), lambda i: (0, 0))
    return pl.pallas_call(
        _proj_qkvr_kernel,
        grid=(m // tm,),
        in_specs=[pl.BlockSpec((tm, k), lambda i: (i, 0)), whole(wq), whole(wk), whole(wv), whole(wr), gain, gain],
        out_specs=[pl.BlockSpec((nq, LANE, tm), lambda i: (0, 0, i)),
                   pl.BlockSpec((nk, tm, LANE), lambda i: (0, i, 0)),
                   pl.BlockSpec((nv, LANE, tm), lambda i: (0, 0, i)),
                   pl.BlockSpec((nr, tm, LANE), lambda i: (0, i, 0))],
        out_shape=[jax.ShapeDtypeStruct((nq, LANE, m), BF16),
                   jax.ShapeDtypeStruct((nk, m, LANE), BF16),
                   jax.ShapeDtypeStruct((nv, LANE, m), BF16),
                   jax.ShapeDtypeStruct((nr, m, LANE), F32)],
        compiler_params=_cparams(("arbitrary",)),
        name="proj_qkv_raw",
    )(x, wq, wk, wv, wr, g_q.reshape(1, LANE), g_k.reshape(1, LANE))


def _rms_rows(x, g):
    return x * lax.rsqrt(jnp.mean(x * x, axis=-1, keepdims=True) + EPS) * g


def _rope_rows(x, cos, sin):
    half = x.shape[-1] // 2
    x1, x2 = x[:, :half], x[:, half:]
    return jnp.concatenate([x1 * cos - x2 * sin, x1 * sin + x2 * cos], axis=1)


def _latent(x_ref, g_ref):
    x = jnp.concatenate([x_ref[s] for s in range(x_ref.shape[0])], axis=1)
    return _rms_rows(x, g_ref[...]).astype(BF16)


def _mla_q_kernel(shift_ref, x_ref, g_ref, w_ref, gn_ref, gr2_ref, c_ref, sa_ref, sb_ref, o_ref, *, scale):
    acc = _dot(_latent(x_ref, g_ref), w_ref[...])
    tm = acc.shape[0]
    for h in range(MLA_HEADS):
        nope = _rms_rows(acc[:, h * MLA_NOPE:(h + 1) * MLA_NOPE], gn_ref[...]) * scale
        o_ref[h, 0:MLA_NOPE, :] = nope.T.astype(o_ref.dtype)
    first = lax.broadcasted_iota(jnp.int32, (LANE - MLA_ROPE, tm), 0) == 0
    pad_rows = jnp.where(first, -shift_ref[0], 0.0)
    low = lax.broadcasted_iota(jnp.int32, (tm, LANE), 1) < MLA_ROPE
    c, sa, sb = c_ref[...], sa_ref[...], sb_ref[...]
    half = MLA_ROPE // 2
    per = LANE // MLA_ROPE
    for s in range(MLA_HEADS // per):
        x = acc[:, MLA_HEADS * MLA_NOPE + s * LANE:MLA_HEADS * MLA_NOPE + (s + 1) * LANE]
        sq = x * x
        s_low = jnp.sum(jnp.where(low, sq, 0.0), axis=-1, keepdims=True)
        s_all = jnp.sum(sq, axis=-1, keepdims=True)
        inv = jnp.where(low, lax.rsqrt(s_low / MLA_ROPE + EPS), lax.rsqrt((s_all - s_low) / MLA_ROPE + EPS))
        y = x * inv * gr2_ref[...]
        roped = y * c + pltpu.roll(y, LANE - half, axis=1) * sa + pltpu.roll(y, half, axis=1) * sb
        x_t = (roped * scale).T
        for j in range(per):
            o_ref[per * s + j, MLA_NOPE:MLA_NOPE + LANE, :] = jnp.concatenate(
                [x_t[j * MLA_ROPE:(j + 1) * MLA_ROPE], pad_rows], axis=0).astype(o_ref.dtype)


def _mla_kv_kernel(x_ref, g_ref, w_ref, tail_ref, gn_ref, gr_ref, cos_ref, sin_ref, ok_ref, ov_ref):
    acc = _dot(_latent(x_ref, g_ref), w_ref[...])
    tm = acc.shape[0]
    k_pe = _rope_rows(_rms_rows(tail_ref[0][:, :MLA_ROPE], gr_ref[...]), cos_ref[...], sin_ref[...])
    first = lax.broadcasted_iota(jnp.int32, (tm, LANE - MLA_ROPE), 1) == 0
    k_pe = jnp.concatenate([k_pe, jnp.where(first, 1.0, 0.0)], axis=1).astype(ok_ref.dtype)
    for h in range(MLA_HEADS):
        c0 = h * (MLA_NOPE + MLA_V)
        ok_ref[h, :, 0:MLA_NOPE] = _rms_rows(acc[:, c0:c0 + MLA_NOPE], gn_ref[...]).astype(ok_ref.dtype)
        ok_ref[h, :, MLA_NOPE:MLA_NOPE + LANE] = k_pe
        ov_ref[h] = acc[:, c0 + MLA_NOPE:c0 + MLA_NOPE + MLA_V].T.astype(ov_ref.dtype)


def mla_project(proj, s_cq, s_ckv, s_tail, seq, q_norm_g, kv_norm_g, wq_r, w_ukv, nope_g, rope_g, cos, sin,
                scale, shift, tm=512):
    _, m, _ = proj.shape
    kq, kkv = s_ckv - s_cq, s_tail - s_ckv
    tps = seq // tm
    dqk = MLA_NOPE + LANE
    half = MLA_ROPE // 2
    rope_specs = [pl.BlockSpec((tm, half), lambda i: (i % tps, 0))] * 2
    gain_specs = [pl.BlockSpec((1, MLA_NOPE), lambda i: (0, 0)), pl.BlockSpec((1, MLA_ROPE), lambda i: (0, 0))]
    per = LANE // MLA_ROPE
    zero = jnp.zeros_like(sin)
    c_tab = jnp.tile(jnp.concatenate([cos, cos], axis=1), (1, per))
    sa_tab = jnp.tile(jnp.concatenate([-sin, zero], axis=1), (1, per))
    sb_tab = jnp.tile(jnp.concatenate([zero, sin], axis=1), (1, per))
    tab_spec = pl.BlockSpec((tm, LANE), lambda i: (i % tps, 0))
    q_t = pl.pallas_call(
        functools.partial(_mla_q_kernel, scale=scale),
        grid=(m // tm,),
        in_specs=[pl.BlockSpec(memory_space=pltpu.SMEM),
                  pl.BlockSpec((kq, tm, LANE), lambda i: (s_cq // kq, i, 0)),
                  pl.BlockSpec((1, kq * LANE), lambda i: (0, 0)),
                  pl.BlockSpec(wq_r.shape, lambda i: (0, 0)),
                  pl.BlockSpec((1, MLA_NOPE), lambda i: (0, 0)), pl.BlockSpec((1, LANE), lambda i: (0, 0)),
                  tab_spec, tab_spec, tab_spec],
        out_specs=pl.BlockSpec((MLA_HEADS, dqk, tm), lambda i: (0, 0, i)),
        out_shape=jax.ShapeDtypeStruct((MLA_HEADS, dqk, m), BF16),
        compiler_params=_cparams(("arbitrary",)),
        name="mla_q_project",
    )(jnp.reshape(shift, (1,)).astype(F32), proj, q_norm_g.reshape(1, -1), wq_r, nope_g[0].reshape(1, -1),
      jnp.tile(rope_g[0].reshape(1, -1), (1, per)), c_tab, sa_tab, sb_tab)
    k, v_t = pl.pallas_call(
        _mla_kv_kernel,
        grid=(m // tm,),
        in_specs=[pl.BlockSpec((kkv, tm, LANE), lambda i: (s_ckv // kkv, i, 0)),
                  pl.BlockSpec((1, kkv * LANE), lambda i: (0, 0)),
                  pl.BlockSpec(w_ukv.shape, lambda i: (0, 0)),
                  pl.BlockSpec((1, tm, LANE), lambda i: (s_tail, i, 0))] + gain_specs + rope_specs,
        out_specs=[pl.BlockSpec((MLA_HEADS, tm, dqk), lambda i: (0, i, 0)),
                   pl.BlockSpec((MLA_HEADS, MLA_V, tm), lambda i: (0, 0, i))],
        out_shape=[jax.ShapeDtypeStruct((MLA_HEADS, m, dqk), BF16),
                   jax.ShapeDtypeStruct((MLA_HEADS, MLA_V, m), BF16)],
        compiler_params=_cparams(("arbitrary",)),
        name="mla_kv_project",
    )(proj, kv_norm_g.reshape(1, -1), w_ukv, proj, nope_g[1].reshape(1, -1), rope_g[1].reshape(1, -1), cos, sin)
    return q_t, k, v_t


RES_COLS = 512


def _resproj_kernel(*refs, npair, fuse_norm):
    xres_ref, gate_ref = refs[2 * npair], refs[2 * npair + 1]
    outs = refs[2 * npair + 2 + (3 if fuse_norm else 0):]
    o_ref = outs[0]
    n = o_ref.shape[1]
    for c0 in range(0, n, RES_COLS):
        cols = slice(c0, c0 + RES_COLS)
        acc = _dot(refs[0][...], refs[1][:, cols])
        for p in range(1, npair):
            acc = acc + _dot(refs[2 * p][...], refs[2 * p + 1][:, cols])
        o_ref[:, cols] = xres_ref[:, cols] + gate_ref[0][:, cols] * acc
    if fuse_norm:
        g_ref, sc_ref, sh_ref = refs[2 * npair + 2:2 * npair + 5]
        x = o_ref[...]
        y = x * lax.rsqrt(jnp.mean(x * x, axis=-1, keepdims=True) + EPS)
        outs[1][...] = ((y * g_ref[...]) * (1.0 + sc_ref[0]) + sh_ref[0]).astype(outs[1].dtype)


def resproj(pairs, xres, gate, seq, next_norm=None, tm=512):
    m, n = xres.shape
    tpb = seq // tm
    in_specs, args = [], []
    for x, w in pairs:
        k = x.shape[1]
        in_specs += [pl.BlockSpec((tm, k), lambda i: (i, 0)),
                     pl.BlockSpec((k, n), lambda i: (0, 0), pipeline_mode=pl.Buffered(1))]
        args += [x, w]
    per_batch = pl.BlockSpec((1, 1, n), lambda i: (i // tpb, 0, 0))
    in_specs += [pl.BlockSpec((tm, n), lambda i: (i, 0)), per_batch]
    args += [xres, gate.reshape(-1, 1, n)]
    out_specs = [pl.BlockSpec((tm, n), lambda i: (i, 0))]
    out_shape = [jax.ShapeDtypeStruct((m, n), F32)]
    if next_norm is not None:
        g, scale, shift = next_norm
        in_specs += [pl.BlockSpec((1, n), lambda i: (0, 0)), per_batch, per_batch]
        args += [g.reshape(1, n), scale.reshape(-1, 1, n), shift.reshape(-1, 1, n)]
        out_specs.append(pl.BlockSpec((tm, n), lambda i: (i, 0)))
        out_shape.append(jax.ShapeDtypeStruct((m, n), BF16))
    out = pl.pallas_call(
        functools.partial(_resproj_kernel, npair=len(pairs), fuse_norm=next_norm is not None),
        grid=(m // tm,),
        in_specs=in_specs,
        out_specs=out_specs,
        out_shape=out_shape,
        compiler_params=_cparams(("arbitrary",)),
        name="resproj",
    )(*args)
    return (out[0], out[1]) if next_norm is not None else (out[0], None)


HALO = 8


def _ffn_up_kernel(h_ref, wg32_ref, wv32_ref, cwg_ref, cwv_ref, cbg_ref, cbv_ref, o_ref,
                   ug_ref, uv_ref, wg_ref, wv_ref, *, tm, tiles_per_seq):
    i = pl.program_id(1)
    first = (i % tiles_per_seq) == 0

    @pl.when(i == 0)
    def _():
        wg_ref[...] = wg32_ref[...].astype(wg_ref.dtype)
        wv_ref[...] = wv32_ref[...].astype(wv_ref.dtype)

    @pl.when(first)
    def _():
        ug_ref[0:HALO, :] = jnp.zeros((HALO, ug_ref.shape[1]), F32)
        uv_ref[0:HALO, :] = jnp.zeros((HALO, uv_ref.shape[1]), F32)

    @pl.when(jnp.logical_not(first))
    def _():
        ug_ref[0:HALO, :] = ug_ref[tm:tm + HALO, :]
        uv_ref[0:HALO, :] = uv_ref[tm:tm + HALO, :]

    h = h_ref[...]
    ug_ref[HALO:HALO + tm, :] = _dot(h, wg_ref[...])
    uv_ref[HALO:HALO + tm, :] = _dot(h, wv_ref[...])

    def conv(u_ref, cw_ref, cb_ref):
        out = cb_ref[...]
        for j in range(CONV_WIDTH):
            off = HALO - (CONV_WIDTH - 1) + j
            out = out + cw_ref[j:j + 1, :] * u_ref[off:off + tm, :]
        return out

    g = conv(ug_ref, cwg_ref, cbg_ref)
    v = conv(uv_ref, cwv_ref, cbv_ref)
    o_ref[...] = (g * jax.nn.sigmoid(g) * v).astype(o_ref.dtype)


def ffn_up(h, w_up_all, layer, conv_w, conv_b, seq, tm=1024, tn=512):
    m, d = h.shape
    f = w_up_all.shape[2] // 2
    nj = f // tn
    tps = seq // tm
    cb = conv_b.reshape(1, 2 * f)
    return pl.pallas_call(
        functools.partial(_ffn_up_kernel, tm=tm, tiles_per_seq=tps),
        grid=(nj, m // tm),
        in_specs=[pl.BlockSpec((tm, d), lambda j, i: (i, 0)),
                  pl.BlockSpec((None, d, tn), lambda j, i: (layer, 0, j)),
                  pl.BlockSpec((None, d, tn), lambda j, i: (layer, 0, nj + j)),
                  pl.BlockSpec((CONV_WIDTH, tn), lambda j, i: (0, j)),
                  pl.BlockSpec((CONV_WIDTH, tn), lambda j, i: (0, nj + j)),
                  pl.BlockSpec((1, tn), lambda j, i: (0, j)),
                  pl.BlockSpec((1, tn), lambda j, i: (0, nj + j))],
        out_specs=pl.BlockSpec((tm, tn), lambda j, i: (i, j)),
        out_shape=jax.ShapeDtypeStruct((m, f), BF16),
        scratch_shapes=[pltpu.VMEM((tm + HALO, tn), F32), pltpu.VMEM((tm + HALO, tn), F32),
                        pltpu.VMEM((d, tn), BF16), pltpu.VMEM((d, tn), BF16)],
        compiler_params=_cparams(("arbitrary", "arbitrary")),
        name="ffn_up_conv",
    )(h, w_up_all, w_up_all, conv_w, conv_w, cb, cb)


LOG2E = 1.4426950408889634
CWIN = 16


def _t5_shifted(dist, tbl_ref, h):
    val = jnp.full(dist.shape, tbl_ref[0, h], F32)
    for b in range(1, REL_BUCKETS):
        val = jnp.where(dist >= T5_THR[b - 1], tbl_ref[b, h], val)
    return (val - tbl_ref[REL_BUCKETS - 1, h]) * LOG2E


def _bias_tiles_kernel(tbl_ref, dt_ref, dc_ref):
    h = pl.program_id(0)
    key = lax.broadcasted_iota(jnp.int32, (LANE, LANE), 0)
    q = lax.broadcasted_iota(jnp.int32, (LANE, LANE), 1)
    for rel in range(2):
        dt_ref[0, rel] = _t5_shifted(rel * LANE + q - key, tbl_ref, h)
    dt_ref[0, 2] = jnp.zeros((LANE, LANE), F32)
    u = lax.broadcasted_iota(jnp.int32, (CWIN, LANE), 0)
    qc = lax.broadcasted_iota(jnp.int32, (CWIN, LANE), 1)
    dc_ref[0] = _t5_shifted(qc - CMP_STRIDE * (u - CWIN // 2) - (CMP_BLOCK - 1), tbl_ref, h)


def bias_tiles(rel_bias):
    nh = rel_bias.shape[1]
    return pl.pallas_call(
        _bias_tiles_kernel,
        grid=(nh,),
        in_specs=[pl.BlockSpec(memory_space=pltpu.SMEM)],
        out_specs=[pl.BlockSpec((1, 3, LANE, LANE), lambda h: (h, 0, 0, 0)),
                   pl.BlockSpec((1, CWIN, LANE), lambda h: (h, 0, 0))],
        out_shape=[jax.ShapeDtypeStruct((nh, 3, LANE, LANE), F32),
                   jax.ShapeDtypeStruct((nh, CWIN, LANE), F32)],
        compiler_params=_cparams(("arbitrary",)),
        name="t5_bias_tiles",
    )(rel_bias)


def _compress_kernel(x_ref, pe_ref, w1_ref, b1_ref, w2_ref, b2_ref, g_ref, o_ref, *, half):
    kv = pl.program_id(0)
    nchunk = x_ref.shape[1] // CMP_STRIDE
    a = jnp.zeros((nchunk, CMP_HIDDEN), F32)
    b = jnp.zeros((nchunk, CMP_HIDDEN), F32)
    for p in range(CMP_STRIDE):
        xp = x_ref[0, pl.ds(p, nchunk, stride=CMP_STRIDE), :]
        rows = slice(p * HEAD_DIM, (p + 1) * HEAD_DIM)
        a = a + _dot((xp + pe_ref[0, p:p + 1, :]).astype(BF16), w1_ref[0, rows, :])
        q = CMP_STRIDE + p
        b = b + _dot((xp + pe_ref[0, q:q + 1, :]).astype(BF16),
                     w1_ref[0, half + p * HEAD_DIM:half + (p + 1) * HEAD_DIM, :])
    b_next = jnp.concatenate([b[1:], jnp.zeros((1, b.shape[1]), F32)], axis=0)
    hid = jax.nn.gelu(a + b_next + b1_ref[0])
    out = _dot(hid.astype(BF16), w2_ref[0]) + b2_ref[0]
    normed = out * lax.rsqrt(jnp.mean(out * out, axis=-1, keepdims=True) + EPS) * g_ref[...]
    out = jnp.where(kv == 0, normed, out)
    o_ref[0, 0] = out.astype(o_ref.dtype)


def compress_kv(proj, slab0, bsz, seq, cmp_pe, cmp_w1, cmp_b1, cmp_w2, cmp_b2, g_k):
    nslab, m, _ = proj.shape
    nchunk = seq // CMP_STRIDE
    half = CMP_STRIDE * HEAD_DIM
    del nslab, m
    return pl.pallas_call(
        functools.partial(_compress_kernel, half=half),
        grid=(2, bsz, NSA_GROUPS),
        in_specs=[pl.BlockSpec((1, seq, HEAD_DIM), lambda kv, b, g: (slab0 + 2 * kv + g, b, 0)),
                  pl.BlockSpec((1, CMP_BLOCK, HEAD_DIM), lambda kv, b, g: (kv, 0, 0)),
                  pl.BlockSpec((1, 2 * half, CMP_HIDDEN), lambda kv, b, g: (kv, 0, 0)),
                  pl.BlockSpec((1, 1, CMP_HIDDEN), lambda kv, b, g: (kv, 0, 0)),
                  pl.BlockSpec((1, CMP_HIDDEN, HEAD_DIM), lambda kv, b, g: (kv, 0, 0)),
                  pl.BlockSpec((1, 1, HEAD_DIM), lambda kv, b, g: (kv, 0, 0)),
                  pl.BlockSpec((1, HEAD_DIM), lambda kv, b, g: (0, 0))],
        out_specs=pl.BlockSpec((1, 1, nchunk, HEAD_DIM), lambda kv, b, g: (kv, g, b, 0)),
        out_shape=jax.ShapeDtypeStruct((2, NSA_GROUPS, bsz * nchunk, HEAD_DIM), BF16),
        compiler_params=_cparams(("arbitrary", "arbitrary", "arbitrary")),
        name="nsa_compress",
    )(proj, cmp_pe, cmp_w1.astype(BF16), cmp_b1.reshape(2, 1, CMP_HIDDEN), cmp_w2.astype(BF16),
      cmp_b2.reshape(2, 1, HEAD_DIM), g_k.reshape(1, HEAD_DIM))


KW = 512
PV_KEYS = 256


def _tile_lanes(x, n):
    return jnp.concatenate([x] * n, axis=1)


def _flash_init(m_ref, l_ref, acc_ref):
    m_ref[...] = jnp.full(m_ref.shape, NEG, F32)
    l_ref[...] = jnp.zeros(l_ref.shape, F32)
    acc_ref[...] = jnp.zeros(acc_ref.shape, F32)


def _zero_after(x):
    bits = pltpu.bitcast(x, jnp.int32)
    return lax.shift_right_logical(lax.shift_right_logical(bits, 16), 16).astype(F32)


def _flash_update(s, v_t, m_ref, l_ref, acc_ref, col_max=None, after=None):
    m_old = m_ref[...]
    if col_max is None:
        col_max = jnp.max(s, axis=0, keepdims=True)
    m_new = jnp.maximum(m_old, col_max)
    alpha = jnp.exp2(m_old - m_new)
    l_new = alpha * l_ref[...]
    acc = alpha * acc_ref[...]
    nk = s.shape[0]
    for k0 in range(0, nk, PV_KEYS):
        p = jnp.exp2(s[k0:k0 + PV_KEYS] - m_new)
        l_new = l_new + jnp.sum(p, axis=0, keepdims=True)
        acc = acc + _dot(v_t[:, k0:k0 + PV_KEYS], p.astype(BF16))
    l_ref[...] = l_new
    acc_ref[...] = acc
    m_ref[...] = m_new if after is None else m_new + _zero_after(after)


SAFE_LOG2_BOUND = 60.0


def _flash_accumulate(s, v_t, l_ref, acc_ref, after=None):
    l_new = l_ref[...]
    acc = acc_ref[...]
    for k0 in range(0, s.shape[0], PV_KEYS):
        p = jnp.exp2(s[k0:k0 + PV_KEYS])
        l_new = l_new + jnp.sum(p, axis=0, keepdims=True)
        acc = acc + _dot(v_t[:, k0:k0 + PV_KEYS], p.astype(BF16))
    if after is not None:
        l_new = l_new + jnp.max(_zero_after(after), axis=0, keepdims=True)
    l_ref[...] = l_new
    acc_ref[...] = acc


def _sum_result(l_ref, acc_ref):
    den = l_ref[...]
    ok = den > 0.0
    return acc_ref[...] * jnp.where(ok, 1.0 / jnp.where(ok, den, 1.0), 0.0)


def _inv_den(m, den):
    ok = m > 0.5 * NEG
    return jnp.where(ok, 1.0 / jnp.where(ok, den, 1.0), 0.0)


def _flash_result(m_ref, l_ref, acc_ref):
    return acc_ref[...] * _inv_den(m_ref[...], l_ref[...])


def _softmax_cols(s):
    m = jnp.max(s, axis=0, keepdims=True)
    p = jnp.exp2(s - m)
    return p * _inv_den(m, jnp.sum(p, axis=0, keepdims=True))


def _near_bias(dt_ref, heads, qi, kt0, ntile):
    rows = []
    for j in range(ntile):
        rel = jnp.clip(qi - (kt0 + j), 0, 2)
        rows.append(jnp.concatenate([dt_ref[h, rel] for h in heads], axis=1))
    return jnp.concatenate(rows, axis=0)


def _pipelined_chunks(n, qk_stage, soft_stage):
    @pl.when(n > 0)
    def _():
        qk_stage(0, 0)

    def pair(p, x):
        c = 2 * p
        ahead = qk_stage(c + 1, 1)
        soft_stage(c, 0, ahead)
        ahead = qk_stage(jnp.minimum(c + 2, n - 1), 0)
        soft_stage(c + 1, 1, ahead)
        return x

    lax.fori_loop(0, n // 2, pair, 0)

    @pl.when(n % 2 == 1)
    def _():
        soft_stage(n - 1, 0, None)


NSA_STATE = 9
GATE_ROWS = -(-3 * NSA_HPG // SUBLANE) * SUBLANE


def _nsa_kernel(bound_ref, qt_ref, gt_ref, kc_ref, vct_ref, ks_ref, vst_ref, kw_ref, vwt_ref,
                dt_ref, dc_ref, ext_ref, o_ref, *scratch, seq, nc):
    ng = NSA_GROUPS
    state = [scratch[NSA_STATE * g:NSA_STATE * (g + 1)] for g in range(ng)]
    qi = pl.program_id(1)
    q0 = qi * QB
    hpg = NSA_HPG
    ncp = kc_ref.shape[1]
    ns = seq // SEL_BLOCK
    group_heads = [[g * hpg + h for h in range(hpg)] for g in range(ng)]
    q_ts = [jnp.concatenate([qt_ref[h] for h in group_heads[g]], axis=1) for g in range(ng)]
    pad = CWIN // 2
    wkeys = WINDOW + QB
    start = pl.multiple_of(jnp.maximum(q0 - WINDOW, 0), LANE)
    r0 = pl.multiple_of(qi * (QB // CMP_STRIDE), 8)

    s_w = []
    for g in range(ng):
        sc_ref = state[g][0]
        sc_ref[0:pad, :] = jnp.zeros((pad, hpg * QB), F32)
        sc_ref[pad + ncp:2 * pad + ncp, :] = jnp.zeros((pad, hpg * QB), F32)
        sc_ref[pad:pad + ncp, :] = _dot(kc_ref[g], q_ts[g])
        s_w.append(_dot(kw_ref[g, pl.ds(start, wkeys), :], q_ts[g]))

    ci = lax.broadcasted_iota(jnp.int32, (ncp, QB), 0)
    tc = q0 + lax.broadcasted_iota(jnp.int32, (ncp, QB), 1)
    valid_c = (ci * CMP_STRIDE + CMP_BLOCK - 1 <= tc) & (ci < nc)
    madd_c = _tile_lanes(jnp.where(valid_c, 0.0, NEG), hpg)
    oc_t, p_sum = [], []
    for g in range(ng):
        sc_ref = state[g][0]
        sc_ref[pl.ds(r0, CWIN), :] = sc_ref[pl.ds(r0, CWIN), :] + jnp.concatenate(
            [dc_ref[h] for h in group_heads[g]], axis=1)
        p_c = _softmax_cols(sc_ref[pad:pad + ncp, :] + madd_c)
        oc_t.append(_dot(vct_ref[g], p_c.astype(BF16)))
        ps = p_c[:, 0:QB]
        for h in range(1, hpg):
            ps = ps + p_c[:, h * QB:(h + 1) * QB]
        p_sum.append(ps)

    dist_w = (q0 + lax.broadcasted_iota(jnp.int32, (wkeys, QB), 1)) - (
        start + lax.broadcasted_iota(jnp.int32, (wkeys, QB), 0))
    madd_w = _tile_lanes(jnp.where((dist_w >= 0) & (dist_w < WINDOW), 0.0, NEG), hpg)
    ow_t = []
    for g in range(ng):
        p_w = _softmax_cols(s_w[g] + _near_bias(dt_ref, group_heads[g], qi, start // LANE, wkeys // LANE) + madd_w)
        ow_t.append(_dot(vwt_ref[g, :, pl.ds(start, wkeys)], p_w.astype(BF16)))

    per = SEL_BLOCK // CMP_STRIDE
    blk = lax.broadcasted_iota(jnp.int32, (LANE, QB), 0)
    t = q0 + lax.broadcasted_iota(jnp.int32, (LANE, QB), 1)
    tb = t // SEL_BLOCK
    forced = (blk == 0) | (blk == tb) | (blk == tb - 1)
    blk_f = blk.astype(F32)
    scores = []
    for g in range(ng):
        ps_ref = state[g][1]
        ps_ref[0:SUBLANE, :] = jnp.zeros((SUBLANE, QB), F32)
        ps_ref[SUBLANE:SUBLANE + ncp, :] = p_sum[g]
        band = [ps_ref[pl.ds(SUBLANE + r, ns, stride=per), :] for r in range(-1, per)]
        imp = 0.5 * band[0] + band[1] + band[2] + band[3] + 0.5 * band[4]
        if ns < LANE:
            imp = jnp.concatenate([imp, jnp.zeros((LANE - ns, QB), F32)], axis=0)
        score = jnp.where(forced, FORCE, jnp.where(blk * SEL_BLOCK <= t, imp, NEG))
        scores.append(jnp.where(blk < ns, score, -jnp.inf))
    sels = [jnp.zeros((LANE, QB), F32) for _ in range(ng)]
    for _ in range(min(SEL_TOP_N, ns)):
        for g in range(ng):
            mx = jnp.max(scores[g], axis=0, keepdims=True)
            first = jnp.min(jnp.where(scores[g] == mx, blk_f, float(LANE)), axis=0, keepdims=True)
            pick = blk_f == first
            sels[g] = jnp.where(pick, 1.0, sels[g])
            scores[g] = jnp.where(pick, -jnp.inf, scores[g])
    sel_b = [s.astype(BF16) for s in sels]

    kpos = lax.broadcasted_iota(jnp.int32, (KW, QB), 0)
    tq = q0 + lax.broadcasted_iota(jnp.int32, (KW, QB), 1)
    bounded_ok = bound_ref[0] <= SAFE_LOG2_BOUND
    shift = jnp.where(bounded_ok, bound_ref[0], 0.0)
    c_near = jnp.maximum(qi - 1, 0) // (KW // LANE)

    def scores_of(g, c0):
        chosen = _dot(ext_ref[pl.ds(c0, KW), :], sel_b[g])
        return (chosen - 1.0) * (-NEG) - shift, _dot(ks_ref[g, pl.ds(c0, KW), :], q_ts[g])

    def attend(bounded):
        for g in range(ng):
            _flash_init(*state[g][2:5])

        def qk_stage(c, buf):
            c0 = pl.multiple_of(c * KW, KW)
            ahead = []
            for g in range(ng):
                madd, s = scores_of(g, c0)
                s = s + _tile_lanes(madd, hpg)
                state[g][5 + buf][...] = s
                if bounded:
                    ahead.append(s[KW - 8:KW])
                else:
                    ahead.append(jnp.max(s, axis=0, keepdims=True))
                    state[g][7 + buf][...] = ahead[-1]
            return ahead

        def soft_stage(c, buf, ahead):
            c0 = pl.multiple_of(c * KW, KW)
            for g in range(ng):
                m_ref, l_ref, acc_ref = state[g][2:5]
                after = None if ahead is None else ahead[g]
                v_t = vst_ref[g, :, pl.ds(c0, KW)]
                if bounded:
                    _flash_accumulate(state[g][5 + buf][...], v_t, l_ref, acc_ref, after=after)
                else:
                    _flash_update(state[g][5 + buf][...], v_t, m_ref, l_ref, acc_ref,
                                  col_max=state[g][7 + buf][...], after=after)

        def near_step(c, x):
            c0 = pl.multiple_of(c * KW, KW)
            causal = jnp.where(c0 + kpos <= tq, 0.0, NEG)
            scores = []
            for g in range(ng):
                madd, s = scores_of(g, c0)
                scores.append(s + _tile_lanes(madd + causal, hpg)
                              + _near_bias(dt_ref, group_heads[g], qi, c * (KW // LANE), KW // LANE))
            for g in range(ng):
                m_ref, l_ref, acc_ref = state[g][2:5]
                if bounded:
                    _flash_accumulate(scores[g], vst_ref[g, :, pl.ds(c0, KW)], l_ref, acc_ref)
                else:
                    _flash_update(scores[g], vst_ref[g, :, pl.ds(c0, KW)], m_ref, l_ref, acc_ref)
            return x

        _pipelined_chunks(c_near, qk_stage, soft_stage)
        lax.fori_loop(c_near, qi // (KW // LANE) + 1, near_step, 0)
        for g in range(ng):
            m_ref, l_ref, acc_ref = state[g][2:5]
            acc_ref[...] = _sum_result(l_ref, acc_ref) if bounded else _flash_result(m_ref, l_ref, acc_ref)

    pl.when(bounded_ok)(lambda: attend(True))
    pl.when(jnp.logical_not(bounded_ok))(lambda: attend(False))

    for g in range(ng):
        os_t = state[g][4][...]
        gates = jax.nn.sigmoid(gt_ref[g])
        for h in range(hpg):
            sl = slice(h * QB, (h + 1) * QB)
            o_t = (gates[3 * h:3 * h + 1] * oc_t[g][:, sl] + gates[3 * h + 1:3 * h + 2] * os_t[:, sl]
                   + gates[3 * h + 2:3 * h + 3] * ow_t[g][:, sl])
            hh = group_heads[g][h]
            o_ref[:, hh * HEAD_DIM:(hh + 1) * HEAD_DIM] = o_t.T.astype(o_ref.dtype)


def nsa_attention(logit_bound, q_t, gates_t, kc, vc_t, k_sw, v_sw_t, dt, dc, bsz, seq):
    nq = seq // QB
    ncp = seq // CMP_STRIDE
    nc = ncp - 1
    ns = seq // SEL_BLOCK
    assert ns <= LANE and seq >= WINDOW + QB and seq % KW == 0
    assert CMP_BLOCK == 2 * CMP_STRIDE and SEL_BLOCK == 4 * CMP_STRIDE
    expand =((np.arange(seq)[:, None] // SEL_BLOCK) == np.arange(LANE)[None, :]).astype(np.float32)
    ng = NSA_GROUPS
    once = dict(pipeline_mode=pl.Buffered(1))
    ks_spec = pl.BlockSpec((ng, seq, HEAD_DIM), lambda b, i: (0, b, 0), **once)
    kw_spec = pl.BlockSpec((ng, seq, HEAD_DIM), lambda b, i: (1, b, 0), **once)
    vs_spec = pl.BlockSpec((ng, HEAD_DIM, seq), lambda b, i: (0, 0, b), **once)
    vw_spec = pl.BlockSpec((ng, HEAD_DIM, seq), lambda b, i: (1, 0, b), **once)
    lanes = NSA_HPG * QB
    group_state = [pltpu.VMEM((ncp + CWIN, lanes), F32), pltpu.VMEM((ncp + SUBLANE, QB), F32),
                   pltpu.VMEM((1, lanes), F32), pltpu.VMEM((1, lanes), F32), pltpu.VMEM((HEAD_DIM, lanes), F32),
                   pltpu.VMEM((KW, lanes), F32), pltpu.VMEM((KW, lanes), F32),
                   pltpu.VMEM((1, lanes), F32), pltpu.VMEM((1, lanes), F32)]
    assert len(group_state) == NSA_STATE
    return pl.pallas_call(
        functools.partial(_nsa_kernel, seq=seq, nc=nc),
        grid=(bsz, nq),
        in_specs=[pl.BlockSpec(memory_space=pltpu.SMEM),
                  pl.BlockSpec((NSA_HEADS, HEAD_DIM, QB), lambda b, i: (0, 0, b * nq + i)),
                  pl.BlockSpec((ng, GATE_ROWS, QB), lambda b, i: (0, 0, b * nq + i)),
                  pl.BlockSpec((ng, ncp, HEAD_DIM), lambda b, i: (0, b, 0)),
                  pl.BlockSpec((ng, HEAD_DIM, ncp), lambda b, i: (0, 0, b)),
                  ks_spec, vs_spec, kw_spec, vw_spec,
                  pl.BlockSpec((NSA_HEADS, 3, LANE, LANE), lambda b, i: (0, 0, 0, 0)),
                  pl.BlockSpec((NSA_HEADS, CWIN, LANE), lambda b, i: (0, 0, 0)),
                  pl.BlockSpec((seq, LANE), lambda b, i: (0, 0))],
        out_specs=pl.BlockSpec((QB, NSA_HEADS * HEAD_DIM), lambda b, i: (b * nq + i, 0)),
        out_shape=jax.ShapeDtypeStruct((bsz * seq, NSA_HEADS * HEAD_DIM), BF16),
        scratch_shapes=group_state * ng,
        compiler_params=_cparams(("arbitrary", "arbitrary")),
        name="nsa_attention",
    )(jnp.reshape(logit_bound, (1,)).astype(F32), q_t, gates_t, kc, vc_t, k_sw, v_sw_t, k_sw, v_sw_t, dt, dc,
      jnp.asarray(expand, BF16))


MLA_HPS = 2


def _mla_kernel(bound_ref, qt_ref, k_ref, vt_ref, o_ref, *scratch):
    qi = pl.program_id(2)
    chains = [scratch[3 * h:3 * h + 3] for h in range(MLA_HPS)]
    sbuf = [scratch[(3 + b) * MLA_HPS:(4 + b) * MLA_HPS] for b in range(2)]
    cbuf = [scratch[(5 + b) * MLA_HPS:(6 + b) * MLA_HPS] for b in range(2)]
    c_diag = pl.multiple_of(qi * KW, KW)
    kpos = lax.broadcasted_iota(jnp.int32, (KW, KW), 0)
    tq = lax.broadcasted_iota(jnp.int32, (KW, KW), 1)
    dv = vt_ref.shape[1]

    def attend(bounded):
        for ch in chains:
            _flash_init(*ch)

        def qk_stage(c, buf):
            c0 = pl.multiple_of(c * KW, KW)
            ahead = []
            for h in range(MLA_HPS):
                s = _dot(k_ref[h, pl.ds(c0, KW), :], qt_ref[h])
                sbuf[buf][h][...] = s
                if bounded:
                    ahead.append(s[KW - 8:KW])
                else:
                    ahead.append(jnp.max(s, axis=0, keepdims=True))
                    cbuf[buf][h][...] = ahead[-1]
            return ahead

        def soft_stage(c, buf, ahead):
            c0 = pl.multiple_of(c * KW, KW)
            for h, (m_ref, l_ref, acc_ref) in enumerate(chains):
                after = None if ahead is None else ahead[h]
                if bounded:
                    _flash_accumulate(sbuf[buf][h][...], vt_ref[h, :, pl.ds(c0, KW)], l_ref, acc_ref, after=after)
                else:
                    _flash_update(sbuf[buf][h][...], vt_ref[h, :, pl.ds(c0, KW)], m_ref, l_ref, acc_ref,
                                  col_max=cbuf[buf][h][...], after=after)

        _pipelined_chunks(qi, qk_stage, soft_stage)
        causal = jnp.where(kpos <= tq, 0.0, NEG)
        scores = [_dot(k_ref[h, pl.ds(c_diag, KW), :], qt_ref[h]) + causal for h in range(MLA_HPS)]
        for h, (m_ref, l_ref, acc_ref) in enumerate(chains):
            if bounded:
                _flash_accumulate(scores[h], vt_ref[h, :, pl.ds(c_diag, KW)], l_ref, acc_ref)
                o_t = _sum_result(l_ref, acc_ref)
            else:
                _flash_update(scores[h], vt_ref[h, :, pl.ds(c_diag, KW)], m_ref, l_ref, acc_ref)
                o_t = _flash_result(m_ref, l_ref, acc_ref)
            o_ref[:, h * dv:(h + 1) * dv] = o_t.T.astype(o_ref.dtype)

    bounded_ok = bound_ref[0] <= SAFE_LOG2_BOUND
    pl.when(bounded_ok)(lambda: attend(True))
    pl.when(jnp.logical_not(bounded_ok))(lambda: attend(False))


def mla_attention(logit_bound, q_t, k, v_t, bsz, seq):
    nh, dqk, _ = q_t.shape
    dv = v_t.shape[1]
    nq = seq // KW
    hps = MLA_HPS
    state = [pltpu.VMEM((1, KW), F32), pltpu.VMEM((1, KW), F32), pltpu.VMEM((dv, KW), F32)] * hps
    state += [pltpu.VMEM((KW, KW), F32)] * (2 * hps)
    state += [pltpu.VMEM((1, KW), F32)] * (2 * hps)
    return pl.pallas_call(
        _mla_kernel,
        grid=(bsz, nh // hps, nq),
        in_specs=[pl.BlockSpec(memory_space=pltpu.SMEM),
                  pl.BlockSpec((hps, dqk, KW), lambda b, h, i: (h, 0, b * nq + i)),
                  pl.BlockSpec((hps, seq, dqk), lambda b, h, i: (h, b, 0)),
                  pl.BlockSpec((hps, dv, seq), lambda b, h, i: (h, 0, b))],
        out_specs=pl.BlockSpec((KW, hps * dv), lambda b, h, i: (b * nq + i, h)),
        out_shape=jax.ShapeDtypeStruct((bsz * seq, nh * dv), BF16),
        scratch_shapes=state,
        compiler_params=_cparams(("arbitrary", "arbitrary", "arbitrary")),
        name="mla_attention",
    )(jnp.reshape(logit_bound, (1,)).astype(F32), q_t, k, v_t)


INT_MIN = -2 ** 31
NEG_KEY = int(np.array(NEG, np.float32).view(np.int32)) ^ 0x7FFFFFFF
KEY_BITS = 32
SURE_BITS = 22


def _sort_key(x):
    bits = pltpu.bitcast(x + 0.0, jnp.int32)
    return jnp.where(bits < 0, bits ^ 0x7FFFFFFF, bits)


def _dsa_kernel(bound_ref, iqt_ref, iwt_ref, ik_ref, qt_ref, k_ref, vt_ref, dt_ref, o_ref,
                key_ref, *state, seq, k_sel):
    qi = pl.program_id(1)
    q0 = qi * QB
    n_chunk = (q0 + QB + KW - 1) // KW
    n_rest = seq - n_chunk * KW
    kpos = lax.broadcasted_iota(jnp.int32, (KW, QB), 0)
    tq = q0 + lax.broadcasted_iota(jnp.int32, (KW, QB), 1)
    hpp = KW // QB

    def score_chunk(c, x):
        c0 = pl.multiple_of(c * KW, KW)
        ikc = ik_ref[pl.ds(c0, KW), :]
        acc = jnp.zeros((KW, QB), F32)
        for piece in range(IDX_HEADS // hpp):
            sl = slice(piece * KW, (piece + 1) * KW)
            s = jnp.maximum(_dot(ikc, iqt_ref[0, :, sl]), 0.0) * iwt_ref[0, :, sl]
            for j in range(hpp):
                acc = acc + s[:, j * QB:(j + 1) * QB]
        acc = jnp.where(c0 + kpos <= tq, acc, NEG)
        key_ref[pl.ds(c0, KW), :] = _sort_key(acc)
        return x

    lax.fori_loop(0, n_chunk, score_chunk, 0)

    def count(pred):
        def chunk_hits(c):
            c0 = pl.multiple_of(c * KW, KW)
            hit = jnp.where(pred(key_ref[pl.ds(c0, KW), :], c0), 1.0, 0.0)
            parts = [hit[SUBLANE * i:SUBLANE * (i + 1)] for i in range(KW // SUBLANE)]
            while len(parts) > 1:
                parts = [parts[i] + parts[i + 1] for i in range(0, len(parts), 2)]
            return parts[0]

        def body(j, acc):
            second = 2 * j + 1
            weight = jnp.where(second < n_chunk, 1.0, 0.0)
            return acc + chunk_hits(2 * j) + chunk_hits(jnp.minimum(second, n_chunk - 1)) * weight

        acc = lax.fori_loop(0, (n_chunk + 1) // 2, body, jnp.zeros((SUBLANE, QB), F32))
        return jnp.sum(acc, axis=0, keepdims=True)

    rest = n_rest.astype(F32)
    kf = float(k_sel)

    def bit_step(i, st):
        u, thr_s, settled = st
        bit = jnp.left_shift(jnp.int32(1), KEY_BITS - 1 - i)
        trial = (u | bit) ^ INT_MIN
        cnt = count(lambda keys, c0: keys >= trial) + jnp.where(NEG_KEY >= trial, rest, 0.0)
        new = (cnt == kf) & (settled < 0.5)
        return (jnp.where(cnt >= kf, u | bit, u), jnp.where(new, trial, thr_s), jnp.where(new, 1.0, settled))

    st = (jnp.zeros((1, QB), jnp.int32), jnp.zeros((1, QB), jnp.int32), jnp.zeros((1, QB), F32))
    st = lax.fori_loop(0, SURE_BITS, bit_step, st)
    _, (u, thr_s, settled) = lax.while_loop(
        lambda c: (c[0] < KEY_BITS) & (jnp.min(c[1][2]) < 0.5),
        lambda c: (c[0] + 1, bit_step(c[0], c[1])), (jnp.int32(SURE_BITS), st))
    is_settled = settled > 0.5
    thr = jnp.where(is_settled, thr_s, u ^ INT_MIN)

    def edge_counts():
        return (count(lambda keys, c0: keys > thr) + jnp.where(NEG_KEY > thr, rest, 0.0),
                count(lambda keys, c0: keys >= thr) + jnp.where(NEG_KEY >= thr, rest, 0.0))

    zero_cnt = jnp.zeros((1, QB), F32)
    cnt_gt, cnt_ge = lax.cond(jnp.min(settled) > 0.5, lambda: (zero_cnt, zero_cnt), edge_counts)
    need = kf - cnt_gt
    tie_q = (cnt_ge > kf) & (thr != NEG_KEY) & jnp.logical_not(is_settled)
    idx_bits = (seq - 1).bit_length()
    no_cut = 2 ** 30

    def tie_cut():
        def idx_step(i, x):
            bit = jnp.left_shift(jnp.int32(1), idx_bits - 1 - i)
            trial = x | bit
            f = count(lambda keys, c0: (keys == thr) & (c0 + kpos < trial))
            return jnp.where(f <= need - 1.0, trial, x)
        return lax.fori_loop(0, idx_bits, idx_step, jnp.zeros((1, QB), jnp.int32))

    any_tie = jnp.max(jnp.where(tie_q, 1.0, 0.0)) > 0.0
    x_cut = lax.cond(any_tie, tie_cut, lambda: jnp.full((1, QB), no_cut, jnp.int32))
    x_cut = jnp.where(tie_q, x_cut, no_cut)

    bounded_ok = bound_ref[0] <= SAFE_LOG2_BOUND
    shift = jnp.where(bounded_ok, bound_ref[0], 0.0)

    def mask_add(c0):
        keys = key_ref[pl.ds(c0, KW), :]
        pos = c0 + kpos
        chosen = (keys > thr) | ((keys == thr) & (pos <= x_cut))
        return _tile_lanes(jnp.where(chosen & (pos <= tq), -shift, NEG), DSA_HPG)

    c_near = jnp.maximum(qi - 1, 0) // (KW // LANE)
    ng = DSA_KV_HEADS
    chains = [state[3 * g:3 * g + 3] for g in range(ng)]
    sbuf = [state[(3 + b) * ng:(4 + b) * ng] for b in range(2)]
    cbuf = [state[(5 + b) * ng:(6 + b) * ng] for b in range(2)]
    group_heads = [[g * DSA_HPG + h for h in range(DSA_HPG)] for g in range(ng)]

    def raw_scores(c0, g):
        q_t = jnp.concatenate([qt_ref[h] for h in group_heads[g]], axis=1)
        return _dot(k_ref[g, pl.ds(c0, KW), :], q_t)

    def attend(bounded):
        for ch in chains:
            _flash_init(*ch)

        def qk_stage(c, buf):
            c0 = pl.multiple_of(c * KW, KW)
            madd = mask_add(c0)
            ahead = []
            for g in range(ng):
                s = raw_scores(c0, g) + madd
                sbuf[buf][g][...] = s
                if bounded:
                    ahead.append(s[KW - 8:KW])
                else:
                    ahead.append(jnp.max(s, axis=0, keepdims=True))
                    cbuf[buf][g][...] = ahead[-1]
            return ahead

        def soft_stage(c, buf, ahead):
            c0 = pl.multiple_of(c * KW, KW)
            for g, (m_ref, l_ref, acc_ref) in enumerate(chains):
                after = None if ahead is None else ahead[g]
                if bounded:
                    _flash_accumulate(sbuf[buf][g][...], vt_ref[g, :, pl.ds(c0, KW)], l_ref, acc_ref, after=after)
                else:
                    _flash_update(sbuf[buf][g][...], vt_ref[g, :, pl.ds(c0, KW)], m_ref, l_ref, acc_ref,
                                  col_max=cbuf[buf][g][...], after=after)

        _pipelined_chunks(c_near, qk_stage, soft_stage)

        def near_step(c, x):
            c0 = pl.multiple_of(c * KW, KW)
            madd = mask_add(c0)
            scores = [raw_scores(c0, g) + madd
                      + _near_bias(dt_ref, group_heads[g], qi, c * (KW // LANE), KW // LANE)
                      for g in range(ng)]
            for g, (m_ref, l_ref, acc_ref) in enumerate(chains):
                if bounded:
                    _flash_accumulate(scores[g], vt_ref[g, :, pl.ds(c0, KW)], l_ref, acc_ref)
                else:
                    _flash_update(scores[g], vt_ref[g, :, pl.ds(c0, KW)], m_ref, l_ref, acc_ref)
            return x

        lax.fori_loop(c_near, n_chunk, near_step, 0)
        for g, (m_ref, l_ref, acc_ref) in enumerate(chains):
            o_t = _sum_result(l_ref, acc_ref) if bounded else _flash_result(m_ref, l_ref, acc_ref)
            for h in range(DSA_HPG):
                hh = group_heads[g][h]
                o_ref[:, hh * HEAD_DIM:(hh + 1) * HEAD_DIM] = o_t[:, h * QB:(h + 1) * QB].T.astype(o_ref.dtype)

    pl.when(bounded_ok)(lambda: attend(True))
    pl.when(jnp.logical_not(bounded_ok))(lambda: attend(False))


def _idx_prep_kernel(p_ref, c_ref, sa_ref, sb_ref, iqt_ref, ik_ref, iwt_ref, *, ntile):
    nslab_q = IDX_HEADS * IDX_DIM // LANE
    per = LANE // IDX_DIM
    half = IDX_ROPE // 2
    zrows = jnp.zeros((LANE - IDX_DIM, QB), F32)

    def rope_slab(x, c, sa, sb):
        return x * c + pltpu.roll(x, LANE - half, axis=1) * sa + pltpu.roll(x, half, axis=1) * sb

    for t in range(ntile):
        rows = slice(t * QB, (t + 1) * QB)
        c, sa, sb = c_ref[rows, :], sa_ref[rows, :], sb_ref[rows, :]
        cols = []
        for s in range(nslab_q):
            x_t = (rope_slab(p_ref[s, rows, :], c, sa, sb) * IDX_DIM ** -0.5).T
            for j in range(per):
                cols.append(jnp.concatenate([x_t[j * IDX_DIM:(j + 1) * IDX_DIM], zrows], axis=0))
        iqt_ref[t] = jnp.concatenate(cols, axis=1).astype(iqt_ref.dtype)
        tail = p_ref[nslab_q, rows, :]
        lane = lax.broadcasted_iota(jnp.int32, (QB, LANE), 1)
        ik_ref[rows, :] = jnp.where(lane < IDX_DIM, rope_slab(tail, c, sa, sb), 0.0).astype(ik_ref.dtype)
        w_t = (tail * IDX_HEADS ** -0.5).T
        iwt_ref[t] = jnp.concatenate([w_t[IDX_DIM + h:IDX_DIM + h + 1, :] for h in range(IDX_HEADS)], axis=1)


def indexer_operands(proj, seq, tm=512):
    _, m, _ = proj.shape
    ntile = tm // QB
    tps = seq // tm
    cos, sin = _rope_tables(seq, IDX_ROPE)
    zero = jnp.zeros_like(sin)
    rest = IDX_DIM - IDX_ROPE
    per = LANE // IDX_DIM
    c_tab = jnp.tile(jnp.concatenate([cos, cos, jnp.ones((seq, rest), F32)], axis=1), (1, per))
    sa_tab = jnp.tile(jnp.concatenate([-sin, zero, jnp.zeros((seq, rest), F32)], axis=1), (1, per))
    sb_tab = jnp.tile(jnp.concatenate([zero, sin, jnp.zeros((seq, rest), F32)], axis=1), (1, per))
    lanes = IDX_HEADS * QB
    tab_spec = pl.BlockSpec((tm, LANE), lambda i: (i % tps, 0))
    return pl.pallas_call(
        functools.partial(_idx_prep_kernel, ntile=ntile),
        grid=(m // tm,),
        in_specs=[pl.BlockSpec((proj.shape[0], tm, LANE), lambda i: (0, i, 0)), tab_spec, tab_spec, tab_spec],
        out_specs=[pl.BlockSpec((ntile, LANE, lanes), lambda i: (i, 0, 0)),
                   pl.BlockSpec((tm, LANE), lambda i: (i, 0)),
                   pl.BlockSpec((ntile, 1, lanes), lambda i: (i, 0, 0))],
        out_shape=[jax.ShapeDtypeStruct((m // QB, LANE, lanes), BF16),
                   jax.ShapeDtypeStruct((m, LANE), BF16),
                   jax.ShapeDtypeStruct((m // QB, 1, lanes), F32)],
        compiler_params=_cparams(("arbitrary",)),
        name="dsa_indexer_operands",
    )(proj, c_tab, sa_tab, sb_tab)


def dsa_attention(logit_bound, iq_t, iw_t, ik, q_t, k, v_t, dt, bsz, seq):
    nq = seq // QB
    k_sel = min(DSA_TOPK_MAX, seq // 4)
    assert seq % KW == 0
    lanes = DSA_HPG * QB
    return pl.pallas_call(
        functools.partial(_dsa_kernel, seq=seq, k_sel=k_sel),
        grid=(bsz, nq),
        in_specs=[pl.BlockSpec(memory_space=pltpu.SMEM),
                  pl.BlockSpec((1, LANE, IDX_HEADS * QB), lambda b, i: (b * nq + i, 0, 0)),
                  pl.BlockSpec((1, 1, IDX_HEADS * QB), lambda b, i: (b * nq + i, 0, 0)),
                  pl.BlockSpec((seq, LANE), lambda b, i: (b, 0)),
                  pl.BlockSpec((DSA_HEADS, HEAD_DIM, QB), lambda b, i: (0, 0, b * nq + i)),
                  pl.BlockSpec((DSA_KV_HEADS, seq, HEAD_DIM), lambda b, i: (0, b, 0),
                               pipeline_mode=pl.Buffered(1)),
                  pl.BlockSpec((DSA_KV_HEADS, HEAD_DIM, seq), lambda b, i: (0, 0, b),
                               pipeline_mode=pl.Buffered(1)),
                  pl.BlockSpec((DSA_HEADS, 3, LANE, LANE), lambda b, i: (0, 0, 0, 0),
                               pipeline_mode=pl.Buffered(1))],
        out_specs=pl.BlockSpec((QB, DSA_HEADS * HEAD_DIM), lambda b, i: (b * nq + i, 0)),
        out_shape=jax.ShapeDtypeStruct((bsz * seq, DSA_HEADS * HEAD_DIM), BF16),
        scratch_shapes=[pltpu.VMEM((seq, QB), jnp.int32)]
        + [pltpu.VMEM((1, lanes), F32), pltpu.VMEM((1, lanes), F32),
           pltpu.VMEM((HEAD_DIM, lanes), F32)] * DSA_KV_HEADS
        + [pltpu.VMEM((KW, lanes), F32)] * (2 * DSA_KV_HEADS)
        + [pltpu.VMEM((1, lanes), F32)] * (2 * DSA_KV_HEADS),
        compiler_params=_cparams(("arbitrary", "arbitrary")),
        name="dsa_attention",
    )(jnp.reshape(logit_bound, (1,)).astype(F32), iq_t, iw_t, ik, q_t, k, v_t, dt)


def _rope_tables(seq, dim):
    half = dim // 2
    inv = ROPE_THETA ** (-jnp.arange(half, dtype=F32) / half)
    ang = jnp.arange(seq, dtype=F32)[:, None] * inv[None, :]
    return jnp.cos(ang), jnp.sin(ang)


def _logit_bound(gq, gk, dim, scale):
    return dim * scale * jnp.max(jnp.abs(gq)) * jnp.max(jnp.abs(gk)) * (1.0 + 2.0 ** -7)


def _pad_cols(w, n):
    return jnp.pad(w, ((0, 0), (0, n - w.shape[1])))


def _t(x):
    return jnp.swapaxes(x, -1, -2)


def _even_mixer(h, x2, gate, next_norm, dt, dc, bias_bound, bsz, seq, w_in, w_out, nsa_qk_g, cmp_pe, cmp_w1, cmp_b1,
                cmp_w2, cmp_b2, q_norm_g, kv_norm_g, w_uq, w_ukv, nope_g, rope_g):
    m = bsz * seq
    nq_cols = NSA_HEADS * HEAD_DIM
    nkv_cols = 6 * NSA_GROUPS * HEAD_DIM
    ngate = 3 * NSA_HEADS
    o_gate = nq_cols + nkv_cols
    o_cq = o_gate + ngate
    o_ckv = o_cq + MLA_Q_RANK
    o_kpe = o_ckv + MLA_KV_RANK
    gw = NSA_GROUPS * HEAD_DIM
    kvw = [w_in[:, nq_cols + i * gw:nq_cols + (i + 1) * gw] for i in range(6)]
    scale = HEAD_DIM ** -0.5 * LOG2E
    tail = jnp.concatenate([w_in[:, o_kpe:], w_in[:, o_gate:o_cq]], axis=1)
    w_r = jnp.concatenate([kvw[0], kvw[1], w_in[:, o_cq:o_kpe], _pad_cols(tail, LANE)], axis=1).astype(BF16)
    q_t, k_sw, v_sw_t, proj = proj_qkv_raw(
        h, w_in[:, :nq_cols].astype(BF16), nsa_qk_g[0] * scale,
        jnp.concatenate([kvw[2], kvw[4]], axis=1).astype(BF16), nsa_qk_g[1],
        jnp.concatenate([kvw[3], kvw[5]], axis=1).astype(BF16), w_r)
    s_cq = 2 * NSA_GROUPS
    s_ckv = s_cq + MLA_Q_RANK // LANE
    s_tail = s_ckv + MLA_KV_RANK // LANE
    kvc = compress_kv(proj, 0, bsz, seq, cmp_pe, cmp_w1, cmp_b1, cmp_w2, cmp_b2, nsa_qk_g[1])
    tail_v = proj[s_tail]
    gates = tail_v[:, MLA_ROPE:MLA_ROPE + ngate].reshape(m, NSA_GROUPS, 3 * NSA_HPG)
    gates_t = jnp.pad(jnp.transpose(gates, (1, 2, 0)), ((0, 0), (0, GATE_ROWS - 3 * NSA_HPG), (0, 0)))
    nsa_bound = _logit_bound(nsa_qk_g[0], nsa_qk_g[1], HEAD_DIM, scale) + bias_bound
    o_nsa = nsa_attention(nsa_bound, q_t, gates_t, kvc[0], _t(kvc[1]), k_sw, v_sw_t,
                          dt[:NSA_HEADS], dc[:NSA_HEADS], bsz, seq)

    dq = MLA_NOPE + MLA_ROPE
    wq = w_uq.reshape(MLA_Q_RANK, MLA_HEADS, dq)
    wq_r = jnp.concatenate([wq[:, :, :MLA_NOPE].reshape(MLA_Q_RANK, -1),
                            wq[:, :, MLA_NOPE:].reshape(MLA_Q_RANK, -1)], axis=1).astype(BF16)
    cos, sin = _rope_tables(seq, MLA_ROPE)
    mscale = dq ** -0.5 * LOG2E
    side = [jnp.sqrt(MLA_NOPE * jnp.max(jnp.abs(nope_g[i])) ** 2 + MLA_ROPE * jnp.max(jnp.abs(rope_g[i])) ** 2)
            for i in range(2)]
    mla_bound = mscale * side[0] * side[1] * (1.0 + 2.0 ** -7)
    mla_shift = jnp.where(mla_bound <= SAFE_LOG2_BOUND, mla_bound, 0.0)
    q_mla_t, k_mla, v_mla_t = mla_project(proj, s_cq, s_ckv, s_tail, seq, q_norm_g, kv_norm_g, wq_r,
                                          w_ukv.astype(BF16), nope_g, rope_g, cos, sin, mscale, mla_shift)
    o_mla = mla_attention(mla_bound, q_mla_t, k_mla, v_mla_t, bsz, seq)
    w_o = w_out.astype(BF16)
    return resproj([(o_nsa, w_o[:nq_cols]), (o_mla, w_o[nq_cols:])], x2, gate, seq, next_norm)


def _odd_mixer(h, x2, gate, next_norm, dt, bias_bound, bsz, seq, w_in, w_out, qk_g):
    nq = DSA_HEADS * HEAD_DIM
    nkv = DSA_KV_HEADS * HEAD_DIM
    niq = IDX_HEADS * IDX_DIM
    o_k, o_v, o_iq = nq, nq + nkv, nq + 2 * nkv
    w_idx = w_in[:, o_iq:]
    q_t, k, v_t, proj = proj_qkv_raw(
        h, w_in[:, :o_k].astype(BF16), qk_g[0] * (HEAD_DIM ** -0.5 * LOG2E), w_in[:, o_k:o_v].astype(BF16), qk_g[1],
        w_in[:, o_v:o_iq].astype(BF16), _pad_cols(w_idx, niq + LANE).astype(BF16))
    iq_t, ik, iw_t = indexer_operands(proj, seq)
    bound = _logit_bound(qk_g[0], qk_g[1], HEAD_DIM, HEAD_DIM ** -0.5 * LOG2E) + bias_bound
    o = dsa_attention(bound, iq_t, iw_t, ik, q_t, k, v_t, dt, bsz, seq)
    return resproj([(o, w_out.astype(BF16))], x2, gate, seq, next_norm)


def _conv_ffn(h, x2, gate, next_norm, seq, w_up_all, layer, conv_w, conv_b, w_down):
    a = ffn_up(h, w_up_all, layer, conv_w, conv_b, seq)
    return resproj([(a, w_down.astype(BF16))], x2, gate, seq, next_norm)


def kernel(x, c, rel_bias, ada_w, ada_b, norm_g, ev_w_in, ev_w_out, nsa_qk_g, cmp_pe, cmp_w1, cmp_b1, cmp_w2, cmp_b2, mla_q_norm_g, mla_kv_norm_g, mla_w_uq, mla_w_ukv, mla_nope_g, mla_rope_g, od_w_in, od_w_out, dsa_qk_g, ffn_w_up, ffn_conv_w, ffn_conv_b, ffn_w_down):
    bsz, seq, d = x.shape
    depth = ada_w.shape[0]
    x2 = x.reshape(bsz * seq, d)
    mods = ada_all(c, ada_w, ada_b)
    dt, dc = bias_tiles(rel_bias)
    bias_bound = 2.0 * LOG2E * jnp.max(jnp.abs(rel_bias))
    def norm_of(i, sub):
        if i >= depth:
            return None
        shift, scale, _ = jnp.split(mods[i, sub], 3, axis=-1)
        return norm_g[i, sub], scale, shift

    g0, scale0, shift0 = norm_of(0, 0)
    h = modnorm(x2, g0, scale0, shift0, seq)
    for i in range(depth):
        j = i // 2
        gate = jnp.split(mods[i, 0], 3, axis=-1)[2]
        if i % 2 == 0:
            x2, h = _even_mixer(h, x2, gate, norm_of(i, 1), dt, dc, bias_bound, bsz, seq, ev_w_in[j],
                                ev_w_out[j], nsa_qk_g[j], cmp_pe[j], cmp_w1[j], cmp_b1[j], cmp_w2[j],
                                cmp_b2[j], mla_q_norm_g[j], mla_kv_norm_g[j], mla_w_uq[j], mla_w_ukv[j],
                                mla_nope_g[j], mla_rope_g[j])
        else:
            x2, h = _odd_mixer(h, x2, gate, norm_of(i, 1), dt, bias_bound, bsz, seq, od_w_in[j], od_w_out[j],
                               dsa_qk_g[j])
        gate = jnp.split(mods[i, 1], 3, axis=-1)[2]
        x2, h = _conv_ffn(h, x2, gate, norm_of(i + 1, 0), seq, ffn_w_up, i, ffn_conv_w[i], ffn_conv_b[i],
                          ffn_w_down[i])
    return x2.reshape(bsz, seq, d)
```

```python
import functools
import math

import numpy as np
import jax
import jax.numpy as jnp
from jax import lax
from jax.experimental import pallas as pl
from jax.experimental.pallas import tpu as pltpu

HEAD_DIM = 128
NSA_HEADS = 8
NSA_GROUPS = 2
NSA_HPG = NSA_HEADS // NSA_GROUPS
CMP_BLOCK = 32
CMP_STRIDE = 16
CMP_HIDDEN = 256
SEL_BLOCK = 64
SEL_TOP_N = 16
WINDOW = 512
MLA_HEADS = 8
MLA_Q_RANK = 512
MLA_KV_RANK = 256
MLA_NOPE = 128
MLA_ROPE = 64
MLA_V = 128
DSA_HEADS = 16
DSA_KV_HEADS = 4
DSA_HPG = DSA_HEADS // DSA_KV_HEADS
IDX_HEADS = 16
IDX_DIM = 64
IDX_ROPE = 32
DSA_TOPK_MAX = 256
REL_BUCKETS = 32
REL_MAX_DIST = 128
CONV_WIDTH = 3
ROPE_THETA = 10000.0
EPS = 1e-6
NEG = -1e30
FORCE = 1e9

LANE = 128
SUBLANE = 8
QB = 128
VMEM_LIMIT = 56 * 1024 * 1024

F32 = jnp.float32
BF16 = jnp.bfloat16


def _t5_thresholds():
    d = np.arange(0, 4 * REL_MAX_DIST)
    half = REL_BUCKETS // 2
    val = np.log(np.maximum(d, 1) / half) / math.log(REL_MAX_DIST / half) * (REL_BUCKETS - half)
    large = np.minimum(half + np.floor(np.maximum(val, 0.0)).astype(np.int64), REL_BUCKETS - 1)
    bucket = np.where(d < half, d, large)
    return [int(np.argmax(bucket >= b)) for b in range(1, REL_BUCKETS)]


T5_THR = _t5_thresholds()
T5_FAR = T5_THR[-1]
assert T5_FAR <= LANE


def _cparams(sem):
    return pltpu.CompilerParams(dimension_semantics=sem, vmem_limit_bytes=VMEM_LIMIT)


def _dot(a, b):
    return jnp.dot(a, b, preferred_element_type=F32)


def _ada_kernel(ct_ref, w_ref, b_ref, o_ref, *, bsz):
    ct = ct_ref[...]
    a = ct * jax.nn.sigmoid(ct)
    w = w_ref[0]
    rows = [jnp.sum(a[:, b:b + 1] * w, axis=0, keepdims=True) for b in range(bsz)]
    rows.append(jnp.zeros((o_ref.shape[1] - bsz, w.shape[1]), F32))
    o_ref[0] = jnp.concatenate(rows, axis=0) + b_ref[0]


def ada_all(c, ada_w, ada_b):
    depth, two, d, n3 = ada_w.shape
    bsz = c.shape[0]
    rows = -(-bsz // SUBLANE) * SUBLANE
    assert bsz <= LANE
    ct = jnp.zeros((d, LANE), F32).at[:, :bsz].set(c.T)
    w = ada_w.reshape(depth * two, d, n3)
    b = ada_b.reshape(depth * two, 1, n3)
    tn = 512
    out = pl.pallas_call(
        functools.partial(_ada_kernel, bsz=bsz),
        grid=(depth * two, n3 // tn),
        in_specs=[pl.BlockSpec((d, LANE), lambda l, j: (0, 0)),
                  pl.BlockSpec((1, d, tn), lambda l, j: (l, 0, j)),
                  pl.BlockSpec((1, 1, tn), lambda l, j: (l, 0, j))],
        out_specs=pl.BlockSpec((1, rows, tn), lambda l, j: (l, 0, j)),
        out_shape=jax.ShapeDtypeStruct((depth * two, rows, n3), F32),
        compiler_params=_cparams(("arbitrary", "arbitrary")),
        name="ada_mod",
    )(ct, w, b)
    return out[:, :bsz].reshape(depth, two, bsz, n3)


def _modnorm_kernel(x_ref, g_ref, sc_ref, sh_ref, o_ref):
    x = x_ref[...]
    y = x * lax.rsqrt(jnp.mean(x * x, axis=-1, keepdims=True) + EPS)
    h = (y * g_ref[...]) * (1.0 + sc_ref[0]) + sh_ref[0]
    o_ref[...] = h.astype(o_ref.dtype)


def modnorm(x2, g, scale, shift, seq, tm=1024):
    m, d = x2.shape
    tpb = seq // tm
    return pl.pallas_call(
        _modnorm_kernel,
        grid=(m // tm,),
        in_specs=[pl.BlockSpec((tm, d), lambda i: (i, 0)),
                  pl.BlockSpec((1, d), lambda i: (0, 0)),
                  pl.BlockSpec((1, 1, d), lambda i: (i // tpb, 0, 0)),
                  pl.BlockSpec((1, 1, d), lambda i: (i // tpb, 0, 0))],
        out_specs=pl.BlockSpec((tm, d), lambda i: (i, 0)),
        out_shape=jax.ShapeDtypeStruct((m, d), BF16),
        compiler_params=_cparams(("arbitrary",)),
        name="modnorm",
    )(x2, g.reshape(1, d), scale.reshape(-1, 1, d), shift.reshape(-1, 1, d))


def _head_norm(y, g_ref):
    return y * lax.rsqrt(jnp.mean(y * y, axis=-1, keepdims=True) + EPS) * g_ref[...]


def _proj_qkvr_kernel(x_ref, wq_ref, wk_ref, wv_ref, wr_ref, gq_ref, gk_ref, oq_ref, ok_ref, ov_ref, or_ref):
    x = x_ref[...]
    acc = _dot(x, wq_ref[...])
    for s in range(oq_ref.shape[0]):
        oq_ref[s] = _head_norm(acc[:, s * LANE:(s + 1) * LANE], gq_ref).T.astype(oq_ref.dtype)
    acc = _dot(x, wk_ref[...])
    for s in range(ok_ref.shape[0]):
        ok_ref[s] = _head_norm(acc[:, s * LANE:(s + 1) * LANE], gk_ref).astype(ok_ref.dtype)
    acc = _dot(x, wv_ref[...])
    for s in range(ov_ref.shape[0]):
        ov_ref[s] = acc[:, s * LANE:(s + 1) * LANE].T.astype(ov_ref.dtype)
    acc = _dot(x, wr_ref[...])
    for s in range(or_ref.shape[0]):
        or_ref[s] = acc[:, s * LANE:(s + 1) * LANE]


def proj_qkv_raw(x, wq, g_q, wk, g_k, wv, wr, tm=512):
    m, k = x.shape
    nq, nk, nv, nr = (w.shape[1] // LANE for w in (wq, wk, wv, wr))
    whole = lambda w: pl.BlockSpec(w.shape, lambda i: (0, 0), pipeline_mode=pl.Buffered(1))
    gain = pl.BlockSpec((1, LANE), lambda i: (0, 0))
    return pl.pallas_call(
        _proj_qkvr_kernel,
        grid=(m // tm,),
        in_specs=[pl.BlockSpec((tm, k), lambda i: (i, 0)), whole(wq), whole(wk), whole(wv), whole(wr), gain, gain],
        out_specs=[pl.BlockSpec((nq, LANE, tm), lambda i: (0, 0, i)),
                   pl.BlockSpec((nk, tm, LANE), lambda i: (0, i, 0)),
                   pl.BlockSpec((nv, LANE, tm), lambda i: (0, 0, i)),
                   pl.BlockSpec((nr, tm, LANE), lambda i: (0, i, 0))],
        out_shape=[jax.ShapeDtypeStruct((nq, LANE, m), BF16),
                   jax.ShapeDtypeStruct((nk, m, LANE), BF16),
                   jax.ShapeDtypeStruct((nv, LANE, m), BF16),
                   jax.ShapeDtypeStruct((nr, m, LANE), F32)],
        compiler_params=_cparams(("arbitrary",)),
        name="proj_qkv_raw",
    )(x, wq, wk, wv, wr, g_q.reshape(1, LANE), g_k.reshape(1, LANE))


def _rms_rows(x, g):
    return x * lax.rsqrt(jnp.mean(x * x, axis=-1, keepdims=True) + EPS) * g


def _rope_rows(x, cos, sin):
    half = x.shape[-1] // 2
    x1, x2 = x[:, :half], x[:, half:]
    return jnp.concatenate([x1 * cos - x2 * sin, x1 * sin + x2 * cos], axis=1)


def _latent(x_ref, g_ref):
    x = jnp.concatenate([x_ref[s] for s in range(x_ref.shape[0])], axis=1)
    return _rms_rows(x, g_ref[...]).astype(BF16)


def _mla_q_kernel(shift_ref, x_ref, g_ref, w_ref, gn_ref, gr2_ref, c_ref, sa_ref, sb_ref, o_ref, *, scale):
    acc = _dot(_latent(x_ref, g_ref), w_ref[...])
    tm = acc.shape[0]
    for h in range(MLA_HEADS):
        nope = _rms_rows(acc[:, h * MLA_NOPE:(h + 1) * MLA_NOPE], gn_ref[...]) * scale
        o_ref[h, 0:MLA_NOPE, :] = nope.T.astype(o_ref.dtype)
    first = lax.broadcasted_iota(jnp.int32, (LANE - MLA_ROPE, tm), 0) == 0
    pad_rows = jnp.where(first, -shift_ref[0], 0.0)
    low = lax.broadcasted_iota(jnp.int32, (tm, LANE), 1) < MLA_ROPE
    c, sa, sb = c_ref[...], sa_ref[...], sb_ref[...]
    half = MLA_ROPE // 2
    per = LANE // MLA_ROPE
    for s in range(MLA_HEADS // per):
        x = acc[:, MLA_HEADS * MLA_NOPE + s * LANE:MLA_HEADS * MLA_NOPE + (s + 1) * LANE]
        sq = x * x
        s_low = jnp.sum(jnp.where(low, sq, 0.0), axis=-1, keepdims=True)
        s_all = jnp.sum(sq, axis=-1, keepdims=True)
        inv = jnp.where(low, lax.rsqrt(s_low / MLA_ROPE + EPS), lax.rsqrt((s_all - s_low) / MLA_ROPE + EPS))
        y = x * inv * gr2_ref[...]
        roped = y * c + pltpu.roll(y, LANE - half, axis=1) * sa + pltpu.roll(y, half, axis=1) * sb
        x_t = (roped * scale).T
        for j in range(per):
            o_ref[per * s + j, MLA_NOPE:MLA_NOPE + LANE, :] = jnp.concatenate(
                [x_t[j * MLA_ROPE:(j + 1) * MLA_ROPE], pad_rows], axis=0).astype(o_ref.dtype)


def _mla_kv_kernel(x_ref, g_ref, w_ref, tail_ref, gn_ref, gr_ref, cos_ref, sin_ref, ok_ref, ov_ref):
    acc = _dot(_latent(x_ref, g_ref), w_ref[...])
    tm = acc.shape[0]
    k_pe = _rope_rows(_rms_rows(tail_ref[0][:, :MLA_ROPE], gr_ref[...]), cos_ref[...], sin_ref[...])
    first = lax.broadcasted_iota(jnp.int32, (tm, LANE - MLA_ROPE), 1) == 0
    k_pe = jnp.concatenate([k_pe, jnp.where(first, 1.0, 0.0)], axis=1).astype(ok_ref.dtype)
    for h in range(MLA_HEADS):
        c0 = h * (MLA_NOPE + MLA_V)
        ok_ref[h, :, 0:MLA_NOPE] = _rms_rows(acc[:, c0:c0 + MLA_NOPE], gn_ref[...]).astype(ok_ref.dtype)
        ok_ref[h, :, MLA_NOPE:MLA_NOPE + LANE] = k_pe
        ov_ref[h] = acc[:, c0 + MLA_NOPE:c0 + MLA_NOPE + MLA_V].T.astype(ov_ref.dtype)


def mla_project(proj, s_cq, s_ckv, s_tail, seq, q_norm_g, kv_norm_g, wq_r, w_ukv, nope_g, rope_g, cos, sin,
                scale, shift, tm=512):
    _, m, _ = proj.shape
    kq, kkv = s_ckv - s_cq, s_tail - s_ckv
    tps = seq // tm
    dqk = MLA_NOPE + LANE
    half = MLA_ROPE // 2
    rope_specs = [pl.BlockSpec((tm, half), lambda i: (i % tps, 0))] * 2
    gain_specs = [pl.BlockSpec((1, MLA_NOPE), lambda i: (0, 0)), pl.BlockSpec((1, MLA_ROPE), lambda i: (0, 0))]
    per = LANE // MLA_ROPE
    zero = jnp.zeros_like(sin)
    c_tab = jnp.tile(jnp.concatenate([cos, cos], axis=1), (1, per))
    sa_tab = jnp.tile(jnp.concatenate([-sin, zero], axis=1), (1, per))
    sb_tab = jnp.tile(jnp.concatenate([zero, sin], axis=1), (1, per))
    tab_spec = pl.BlockSpec((tm, LANE), lambda i: (i % tps, 0))
    q_t = pl.pallas_call(
        functools.partial(_mla_q_kernel, scale=scale),
        grid=(m // tm,),
        in_specs=[pl.BlockSpec(memory_space=pltpu.SMEM),
                  pl.BlockSpec((kq, tm, LANE), lambda i: (s_cq // kq, i, 0)),
                  pl.BlockSpec((1, kq * LANE), lambda i: (0, 0)),
                  pl.BlockSpec(wq_r.shape, lambda i: (0, 0)),
                  pl.BlockSpec((1, MLA_NOPE), lambda i: (0, 0)), pl.BlockSpec((1, LANE), lambda i: (0, 0)),
                  tab_spec, tab_spec, tab_spec],
        out_specs=pl.BlockSpec((MLA_HEADS, dqk, tm), lambda i: (0, 0, i)),
        out_shape=jax.ShapeDtypeStruct((MLA_HEADS, dqk, m), BF16),
        compiler_params=_cparams(("arbitrary",)),
        name="mla_q_project",
    )(jnp.reshape(shift, (1,)).astype(F32), proj, q_norm_g.reshape(1, -1), wq_r, nope_g[0].reshape(1, -1),
      jnp.tile(rope_g[0].reshape(1, -1), (1, per)), c_tab, sa_tab, sb_tab)
    k, v_t = pl.pallas_call(
        _mla_kv_kernel,
        grid=(m // tm,),
        in_specs=[pl.BlockSpec((kkv, tm, LANE), lambda i: (s_ckv // kkv, i, 0)),
                  pl.BlockSpec((1, kkv * LANE), lambda i: (0, 0)),
                  pl.BlockSpec(w_ukv.shape, lambda i: (0, 0)),
                  pl.BlockSpec((1, tm, LANE), lambda i: (s_tail, i, 0))] + gain_specs + rope_specs,
        out_specs=[pl.BlockSpec((MLA_HEADS, tm, dqk), lambda i: (0, i, 0)),
                   pl.BlockSpec((MLA_HEADS, MLA_V, tm), lambda i: (0, 0, i))],
        out_shape=[jax.ShapeDtypeStruct((MLA_HEADS, m, dqk), BF16),
                   jax.ShapeDtypeStruct((MLA_HEADS, MLA_V, m), BF16)],
        compiler_params=_cparams(("arbitrary",)),
        name="mla_kv_project",
    )(proj, kv_norm_g.reshape(1, -1), w_ukv, proj, nope_g[1].reshape(1, -1), rope_g[1].reshape(1, -1), cos, sin)
    return q_t, k, v_t


RES_COLS = 512


def _resproj_kernel(*refs, npair, fuse_norm):
    xres_ref, gate_ref = refs[2 * npair], refs[2 * npair + 1]
    outs = refs[2 * npair + 2 + (3 if fuse_norm else 0):]
    o_ref = outs[0]
    n = o_ref.shape[1]
    for c0 in range(0, n, RES_COLS):
        cols = slice(c0, c0 + RES_COLS)
        acc = _dot(refs[0][...], refs[1][:, cols])
        for p in range(1, npair):
            acc = acc + _dot(refs[2 * p][...], refs[2 * p + 1][:, cols])
        o_ref[:, cols] = xres_ref[:, cols] + gate_ref[0][:, cols] * acc
    if fuse_norm:
        g_ref, sc_ref, sh_ref = refs[2 * npair + 2:2 * npair + 5]
        x = o_ref[...]
        y = x * lax.rsqrt(jnp.mean(x * x, axis=-1, keepdims=True) + EPS)
        outs[1][...] = ((y * g_ref[...]) * (1.0 + sc_ref[0]) + sh_ref[0]).astype(outs[1].dtype)


def resproj(pairs, xres, gate, seq, next_norm=None, tm=512):
    m, n = xres.shape
    tpb = seq // tm
    in_specs, args = [], []
    for x, w in pairs:
        k = x.shape[1]
        in_specs += [pl.BlockSpec((tm, k), lambda i: (i, 0)),
                     pl.BlockSpec((k, n), lambda i: (0, 0), pipeline_mode=pl.Buffered(1))]
        args += [x, w]
    per_batch = pl.BlockSpec((1, 1, n), lambda i: (i // tpb, 0, 0))
    in_specs += [pl.BlockSpec((tm, n), lambda i: (i, 0)), per_batch]
    args += [xres, gate.reshape(-1, 1, n)]
    out_specs = [pl.BlockSpec((tm, n), lambda i: (i, 0))]
    out_shape = [jax.ShapeDtypeStruct((m, n), F32)]
    if next_norm is not None:
        g, scale, shift = next_norm
        in_specs += [pl.BlockSpec((1, n), lambda i: (0, 0)), per_batch, per_batch]
        args += [g.reshape(1, n), scale.reshape(-1, 1, n), shift.reshape(-1, 1, n)]
        out_specs.append(pl.BlockSpec((tm, n), lambda i: (i, 0)))
        out_shape.append(jax.ShapeDtypeStruct((m, n), BF16))
    out = pl.pallas_call(
        functools.partial(_resproj_kernel, npair=len(pairs), fuse_norm=next_norm is not None),
        grid=(m // tm,),
        in_specs=in_specs,
        out_specs=out_specs,
        out_shape=out_shape,
        compiler_params=_cparams(("arbitrary",)),
        name="resproj",
    )(*args)
    return (out[0], out[1]) if next_norm is not None else (out[0], None)


HALO = 8


def _ffn_up_kernel(h_ref, wg32_ref, wv32_ref, cwg_ref, cwv_ref, cbg_ref, cbv_ref, o_ref,
                   ug_ref, uv_ref, wg_ref, wv_ref, *, tm, tiles_per_seq):
    i = pl.program_id(1)
    first = (i % tiles_per_seq) == 0

    @pl.when(i == 0)
    def _():
        wg_ref[...] = wg32_ref[...].astype(wg_ref.dtype)
        wv_ref[...] = wv32_ref[...].astype(wv_ref.dtype)

    @pl.when(first)
    def _():
        ug_ref[0:HALO, :] = jnp.zeros((HALO, ug_ref.shape[1]), F32)
        uv_ref[0:HALO, :] = jnp.zeros((HALO, uv_ref.shape[1]), F32)

    @pl.when(jnp.logical_not(first))
    def _():
        ug_ref[0:HALO, :] = ug_ref[tm:tm + HALO, :]
        uv_ref[0:HALO, :] = uv_ref[tm:tm + HALO, :]

    h = h_ref[...]
    ug_ref[HALO:HALO + tm, :] = _dot(h, wg_ref[...])
    uv_ref[HALO:HALO + tm, :] = _dot(h, wv_ref[...])

    def conv(u_ref, cw_ref, cb_ref):
        out = cb_ref[...]
        for j in range(CONV_WIDTH):
            off = HALO - (CONV_WIDTH - 1) + j
            out = out + cw_ref[j:j + 1, :] * u_ref[off:off + tm, :]
        return out

    g = conv(ug_ref, cwg_ref, cbg_ref)
    v = conv(uv_ref, cwv_ref, cbv_ref)
    o_ref[...] = (g * jax.nn.sigmoid(g) * v).astype(o_ref.dtype)


def ffn_up(h, w_up_all, layer, conv_w, conv_b, seq, tm=1024, tn=512):
    m, d = h.shape
    f = w_up_all.shape[2] // 2
    nj = f // tn
    tps = seq // tm
    cb = conv_b.reshape(1, 2 * f)
    return pl.pallas_call(
        functools.partial(_ffn_up_kernel, tm=tm, tiles_per_seq=tps),
        grid=(nj, m // tm),
        in_specs=[pl.BlockSpec((tm, d), lambda j, i: (i, 0)),
                  pl.BlockSpec((None, d, tn), lambda j, i: (layer, 0, j)),
                  pl.BlockSpec((None, d, tn), lambda j, i: (layer, 0, nj + j)),
                  pl.BlockSpec((CONV_WIDTH, tn), lambda j, i: (0, j)),
                  pl.BlockSpec((CONV_WIDTH, tn), lambda j, i: (0, nj + j)),
                  pl.BlockSpec((1, tn), lambda j, i: (0, j)),
                  pl.BlockSpec((1, tn), lambda j, i: (0, nj + j))],
        out_specs=pl.BlockSpec((tm, tn), lambda j, i: (i, j)),
        out_shape=jax.ShapeDtypeStruct((m, f), BF16),
        scratch_shapes=[pltpu.VMEM((tm + HALO, tn), F32), pltpu.VMEM((tm + HALO, tn), F32),
                        pltpu.VMEM((d, tn), BF16), pltpu.VMEM((d, tn), BF16)],
        compiler_params=_cparams(("arbitrary", "arbitrary")),
        name="ffn_up_conv",
    )(h, w_up_all, w_up_all, conv_w, conv_w, cb, cb)


LOG2E = 1.4426950408889634
CWIN = 16


def _t5_shifted(dist, tbl_ref, h):
    val = jnp.full(dist.shape, tbl_ref[0, h], F32)
    for b in range(1, REL_BUCKETS):
        val = jnp.where(dist >= T5_THR[b - 1], tbl_ref[b, h], val)
    return (val - tbl_ref[REL_BUCKETS - 1, h]) * LOG2E


def _bias_tiles_kernel(tbl_ref, dt_ref, dc_ref):
    h = pl.program_id(0)
    key = lax.broadcasted_iota(jnp.int32, (LANE, LANE), 0)
    q = lax.broadcasted_iota(jnp.int32, (LANE, LANE), 1)
    for rel in range(2):
        dt_ref[0, rel] = _t5_shifted(rel * LANE + q - key, tbl_ref, h)
    dt_ref[0, 2] = jnp.zeros((LANE, LANE), F32)
    u = lax.broadcasted_iota(jnp.int32, (CWIN, LANE), 0)
    qc = lax.broadcasted_iota(jnp.int32, (CWIN, LANE), 1)
    dc_ref[0] = _t5_shifted(qc - CMP_STRIDE * (u - CWIN // 2) - (CMP_BLOCK - 1), tbl_ref, h)


def bias_tiles(rel_bias):
    nh = rel_bias.shape[1]
    return pl.pallas_call(
        _bias_tiles_kernel,
        grid=(nh,),
        in_specs=[pl.BlockSpec(memory_space=pltpu.SMEM)],
        out_specs=[pl.BlockSpec((1, 3, LANE, LANE), lambda h: (h, 0, 0, 0)),
                   pl.BlockSpec((1, CWIN, LANE), lambda h: (h, 0, 0))],
        out_shape=[jax.ShapeDtypeStruct((nh, 3, LANE, LANE), F32),
                   jax.ShapeDtypeStruct((nh, CWIN, LANE), F32)],
        compiler_params=_cparams(("arbitrary",)),
        name="t5_bias_tiles",
    )(rel_bias)


def _compress_kernel(x_ref, pe_ref, w1_ref, b1_ref, w2_ref, b2_ref, g_ref, o_ref, *, half):
    kv = pl.program_id(0)
    nchunk = x_ref.shape[1] // CMP_STRIDE
    a = jnp.zeros((nchunk, CMP_HIDDEN), F32)
    b = jnp.zeros((nchunk, CMP_HIDDEN), F32)
    for p in range(CMP_STRIDE):
        xp = x_ref[0, pl.ds(p, nchunk, stride=CMP_STRIDE), :]
        rows = slice(p * HEAD_DIM, (p + 1) * HEAD_DIM)
        a = a + _dot((xp + pe_ref[0, p:p + 1, :]).astype(BF16), w1_ref[0, rows, :])
        q = CMP_STRIDE + p
        b = b + _dot((xp + pe_ref[0, q:q + 1, :]).astype(BF16),
                     w1_ref[0, half + p * HEAD_DIM:half + (p + 1) * HEAD_DIM, :])
    b_next = jnp.concatenate([b[1:], jnp.zeros((1, b.shape[1]), F32)], axis=0)
    hid = jax.nn.gelu(a + b_next + b1_ref[0])
    out = _dot(hid.astype(BF16), w2_ref[0]) + b2_ref[0]
    normed = out * lax.rsqrt(jnp.mean(out * out, axis=-1, keepdims=True) + EPS) * g_ref[...]
    out = jnp.where(kv == 0, normed, out)
    o_ref[0, 0] = out.astype(o_ref.dtype)


def compress_kv(proj, slab0, bsz, seq, cmp_pe, cmp_w1, cmp_b1, cmp_w2, cmp_b2, g_k):
    nslab, m, _ = proj.shape
    nchunk = seq // CMP_STRIDE
    half = CMP_STRIDE * HEAD_DIM
    del nslab, m
    return pl.pallas_call(
        functools.partial(_compress_kernel, half=half),
        grid=(2, bsz, NSA_GROUPS),
        in_specs=[pl.BlockSpec((1, seq, HEAD_DIM), lambda kv, b, g: (slab0 + 2 * kv + g, b, 0)),
                  pl.BlockSpec((1, CMP_BLOCK, HEAD_DIM), lambda kv, b, g: (kv, 0, 0)),
                  pl.BlockSpec((1, 2 * half, CMP_HIDDEN), lambda kv, b, g: (kv, 0, 0)),
                  pl.BlockSpec((1, 1, CMP_HIDDEN), lambda kv, b, g: (kv, 0, 0)),
                  pl.BlockSpec((1, CMP_HIDDEN, HEAD_DIM), lambda kv, b, g: (kv, 0, 0)),
                  pl.BlockSpec((1, 1, HEAD_DIM), lambda kv, b, g: (kv, 0, 0)),
                  pl.BlockSpec((1, HEAD_DIM), lambda kv, b, g: (0, 0))],
        out_specs=pl.BlockSpec((1, 1, nchunk, HEAD_DIM), lambda kv, b, g: (kv, g, b, 0)),
        out_shape=jax.ShapeDtypeStruct((2, NSA_GROUPS, bsz * nchunk, HEAD_DIM), BF16),
        compiler_params=_cparams(("arbitrary", "arbitrary", "arbitrary")),
        name="nsa_compress",
    )(proj, cmp_pe, cmp_w1.astype(BF16), cmp_b1.reshape(2, 1, CMP_HIDDEN), cmp_w2.astype(BF16),
      cmp_b2.reshape(2, 1, HEAD_DIM), g_k.reshape(1, HEAD_DIM))


KW = 512
PV_KEYS = 256


def _tile_lanes(x, n):
    return jnp.concatenate([x] * n, axis=1)


def _flash_init(m_ref, l_ref, acc_ref):
    m_ref[...] = jnp.full(m_ref.shape, NEG, F32)
    l_ref[...] = jnp.zeros(l_ref.shape, F32)
    acc_ref[...] = jnp.zeros(acc_ref.shape, F32)


def _zero_after(x):
    bits = pltpu.bitcast(x, jnp.int32)
    return lax.shift_right_logical(lax.shift_right_logical(bits, 16), 16).astype(F32)


def _flash_update(s, v_t, m_ref, l_ref, acc_ref, col_max=None, after=None):
    m_old = m_ref[...]
    if col_max is None:
        col_max = jnp.max(s, axis=0, keepdims=True)
    m_new = jnp.maximum(m_old, col_max)
    alpha = jnp.exp2(m_old - m_new)
    l_new = alpha * l_ref[...]
    acc = alpha * acc_ref[...]
    nk = s.shape[0]
    for k0 in range(0, nk, PV_KEYS):
        p = jnp.exp2(s[k0:k0 + PV_KEYS] - m_new)
        l_new = l_new + jnp.sum(p, axis=0, keepdims=True)
        acc = acc + _dot(v_t[:, k0:k0 + PV_KEYS], p.astype(BF16))
    l_ref[...] = l_new
    acc_ref[...] = acc
    m_ref[...] = m_new if after is None else m_new + _zero_after(after)


SAFE_LOG2_BOUND = 60.0


def _flash_accumulate(s, v_t, l_ref, acc_ref, after=None):
    l_new = l_ref[...]
    acc = acc_ref[...]
    for k0 in range(0, s.shape[0], PV_KEYS):
        p = jnp.exp2(s[k0:k0 + PV_KEYS])
        l_new = l_new + jnp.sum(p, axis=0, keepdims=True)
        acc = acc + _dot(v_t[:, k0:k0 + PV_KEYS], p.astype(BF16))
    if after is not None:
        l_new = l_new + jnp.max(_zero_after(after), axis=0, keepdims=True)
    l_ref[...] = l_new
    acc_ref[...] = acc


def _sum_result(l_ref, acc_ref):
    den = l_ref[...]
    ok = den > 0.0
    return acc_ref[...] * jnp.where(ok, 1.0 / jnp.where(ok, den, 1.0), 0.0)


def _inv_den(m, den):
    ok = m > 0.5 * NEG
    return jnp.where(ok, 1.0 / jnp.where(ok, den, 1.0), 0.0)


def _flash_result(m_ref, l_ref, acc_ref):
    return acc_ref[...] * _inv_den(m_ref[...], l_ref[...])


def _softmax_cols(s):
    m = jnp.max(s, axis=0, keepdims=True)
    p = jnp.exp2(s - m)
    return p * _inv_den(m, jnp.sum(p, axis=0, keepdims=True))


def _near_bias(dt_ref, heads, qi, kt0, ntile):
    rows = []
    for j in range(ntile):
        rel = jnp.clip(qi - (kt0 + j), 0, 2)
        rows.append(jnp.concatenate([dt_ref[h, rel] for h in heads], axis=1))
    return jnp.concatenate(rows, axis=0)


def _pipelined_chunks(n, qk_stage, soft_stage):
    @pl.when(n > 0)
    def _():
        qk_stage(0, 0)

    def pair(p, x):
        c = 2 * p
        ahead = qk_stage(c + 1, 1)
        soft_stage(c, 0, ahead)
        ahead = qk_stage(jnp.minimum(c + 2, n - 1), 0)
        soft_stage(c + 1, 1, ahead)
        return x

    lax.fori_loop(0, n // 2, pair, 0)

    @pl.when(n % 2 == 1)
    def _():
        soft_stage(n - 1, 0, None)


NSA_STATE = 9
GATE_ROWS = -(-3 * NSA_HPG // SUBLANE) * SUBLANE


def _nsa_kernel(bound_ref, qt_ref, gt_ref, kc_ref, vct_ref, ks_ref, vst_ref, kw_ref, vwt_ref,
                dt_ref, dc_ref, ext_ref, o_ref, *scratch, seq, nc):
    ng = NSA_GROUPS
    state = [scratch[NSA_STATE * g:NSA_STATE * (g + 1)] for g in range(ng)]
    qi = pl.program_id(1)
    q0 = qi * QB
    hpg = NSA_HPG
    ncp = kc_ref.shape[1]
    ns = seq // SEL_BLOCK
    group_heads = [[g * hpg + h for h in range(hpg)] for g in range(ng)]
    q_ts = [jnp.concatenate([qt_ref[h] for h in group_heads[g]], axis=1) for g in range(ng)]
    pad = CWIN // 2
    wkeys = WINDOW + QB
    start = pl.multiple_of(jnp.maximum(q0 - WINDOW, 0), LANE)
    r0 = pl.multiple_of(qi * (QB // CMP_STRIDE), 8)

    s_w = []
    for g in range(ng):
        sc_ref = state[g][0]
        sc_ref[0:pad, :] = jnp.zeros((pad, hpg * QB), F32)
        sc_ref[pad + ncp:2 * pad + ncp, :] = jnp.zeros((pad, hpg * QB), F32)
        sc_ref[pad:pad + ncp, :] = _dot(kc_ref[g], q_ts[g])
        s_w.append(_dot(kw_ref[g, pl.ds(start, wkeys), :], q_ts[g]))

    ci = lax.broadcasted_iota(jnp.int32, (ncp, QB), 0)
    tc = q0 + lax.broadcasted_iota(jnp.int32, (ncp, QB), 1)
    valid_c = (ci * CMP_STRIDE + CMP_BLOCK - 1 <= tc) & (ci < nc)
    madd_c = _tile_lanes(jnp.where(valid_c, 0.0, NEG), hpg)
    oc_t, p_sum = [], []
    for g in range(ng):
        sc_ref = state[g][0]
        sc_ref[pl.ds(r0, CWIN), :] = sc_ref[pl.ds(r0, CWIN), :] + jnp.concatenate(
            [dc_ref[h] for h in group_heads[g]], axis=1)
        p_c = _softmax_cols(sc_ref[pad:pad + ncp, :] + madd_c)
        oc_t.append(_dot(vct_ref[g], p_c.astype(BF16)))
        ps = p_c[:, 0:QB]
        for h in range(1, hpg):
            ps = ps + p_c[:, h * QB:(h + 1) * QB]
        p_sum.append(ps)

    dist_w = (q0 + lax.broadcasted_iota(jnp.int32, (wkeys, QB), 1)) - (
        start + lax.broadcasted_iota(jnp.int32, (wkeys, QB), 0))
    madd_w = _tile_lanes(jnp.where((dist_w >= 0) & (dist_w < WINDOW), 0.0, NEG), hpg)
    ow_t = []
    for g in range(ng):
        p_w = _softmax_cols(s_w[g] + _near_bias(dt_ref, group_heads[g], qi, start // LANE, wkeys // LANE) + madd_w)
        ow_t.append(_dot(vwt_ref[g, :, pl.ds(start, wkeys)], p_w.astype(BF16)))

    per = SEL_BLOCK // CMP_STRIDE
    blk = lax.broadcasted_iota(jnp.int32, (LANE, QB), 0)
    t = q0 + lax.broadcasted_iota(jnp.int32, (LANE, QB), 1)
    tb = t // SEL_BLOCK
    forced = (blk == 0) | (blk == tb) | (blk == tb - 1)
    blk_f = blk.astype(F32)
    scores = []
    for g in range(ng):
        ps_ref = state[g][1]
        ps_ref[0:SUBLANE, :] = jnp.zeros((SUBLANE, QB), F32)
        ps_ref[SUBLANE:SUBLANE + ncp, :] = p_sum[g]
        band = [ps_ref[pl.ds(SUBLANE + r, ns, stride=per), :] for r in range(-1, per)]
        imp = 0.5 * band[0] + band[1] + band[2] + band[3] + 0.5 * band[4]
        if ns < LANE:
            imp = jnp.concatenate([imp, jnp.zeros((LANE - ns, QB), F32)], axis=0)
        score = jnp.where(forced, FORCE, jnp.where(blk * SEL_BLOCK <= t, imp, NEG))
        scores.append(jnp.where(blk < ns, score, -jnp.inf))
    sels = [jnp.zeros((LANE, QB), F32) for _ in range(ng)]
    for _ in range(min(SEL_TOP_N, ns)):
        for g in range(ng):
            mx = jnp.max(scores[g], axis=0, keepdims=True)
            first = jnp.min(jnp.where(scores[g] == mx, blk_f, float(LANE)), axis=0, keepdims=True)
            pick = blk_f == first
            sels[g] = jnp.where(pick, 1.0, sels[g])
            scores[g] = jnp.where(pick, -jnp.inf, scores[g])
    sel_b = [s.astype(BF16) for s in sels]

    kpos = lax.broadcasted_iota(jnp.int32, (KW, QB), 0)
    tq = q0 + lax.broadcasted_iota(jnp.int32, (KW, QB), 1)
    bounded_ok = bound_ref[0] <= SAFE_LOG2_BOUND
    shift = jnp.where(bounded_ok, bound_ref[0], 0.0)
    c_near = jnp.maximum(qi - 1, 0) // (KW // LANE)

    def scores_of(g, c0):
        chosen = _dot(ext_ref[pl.ds(c0, KW), :], sel_b[g])
        return (chosen - 1.0) * (-NEG) - shift, _dot(ks_ref[g, pl.ds(c0, KW), :], q_ts[g])

    def attend(bounded):
        for g in range(ng):
            _flash_init(*state[g][2:5])

        def qk_stage(c, buf):
            c0 = pl.multiple_of(c * KW, KW)
            ahead = []
            for g in range(ng):
                madd, s = scores_of(g, c0)
                s = s + _tile_lanes(madd, hpg)
                state[g][5 + buf][...] = s
                if bounded:
                    ahead.append(s[KW - 8:KW])
                else:
                    ahead.append(jnp.max(s, axis=0, keepdims=True))
                    state[g][7 + buf][...] = ahead[-1]
            return ahead

        def soft_stage(c, buf, ahead):
            c0 = pl.multiple_of(c * KW, KW)
            for g in range(ng):
                m_ref, l_ref, acc_ref = state[g][2:5]
                after = None if ahead is None else ahead[g]
                v_t = vst_ref[g, :, pl.ds(c0, KW)]
                if bounded:
                    _flash_accumulate(state[g][5 + buf][...], v_t, l_ref, acc_ref, after=after)
                else:
                    _flash_update(state[g][5 + buf][...], v_t, m_ref, l_ref, acc_ref,
                                  col_max=state[g][7 + buf][...], after=after)

        def near_step(c, x):
            c0 = pl.multiple_of(c * KW, KW)
            causal = jnp.where(c0 + kpos <= tq, 0.0, NEG)
            scores = []
            for g in range(ng):
                madd, s = scores_of(g, c0)
                scores.append(s + _tile_lanes(madd + causal, hpg)
                              + _near_bias(dt_ref, group_heads[g], qi, c * (KW // LANE), KW // LANE))
            for g in range(ng):
                m_ref, l_ref, acc_ref = state[g][2:5]
                if bounded:
                    _flash_accumulate(scores[g], vst_ref[g, :, pl.ds(c0, KW)], l_ref, acc_ref)
                else:
                    _flash_update(scores[g], vst_ref[g, :, pl.ds(c0, KW)], m_ref, l_ref, acc_ref)
            return x

        _pipelined_chunks(c_near, qk_stage, soft_stage)
        lax.fori_loop(c_near, qi // (KW // LANE) + 1, near_step, 0)
        for g in range(ng):
            m_ref, l_ref, acc_ref = state[g][2:5]
            acc_ref[...] = _sum_result(l_ref, acc_ref) if bounded else _flash_result(m_ref, l_ref, acc_ref)

    pl.when(bounded_ok)(lambda: attend(True))
    pl.when(jnp.logical_not(bounded_ok))(lambda: attend(False))

    for g in range(ng):
        os_t = state[g][4][...]
        gates = jax.nn.sigmoid(gt_ref[g])
        for h in range(hpg):
            sl = slice(h * QB, (h + 1) * QB)
            o_t = (gates[3 * h:3 * h + 1] * oc_t[g][:, sl] + gates[3 * h + 1:3 * h + 2] * os_t[:, sl]
                   + gates[3 * h + 2:3 * h + 3] * ow_t[g][:, sl])
            hh = group_heads[g][h]
            o_ref[:, hh * HEAD_DIM:(hh + 1) * HEAD_DIM] = o_t.T.astype(o_ref.dtype)


def nsa_attention(logit_bound, q_t, gates_t, kc, vc_t, k_sw, v_sw_t, dt, dc, bsz, seq):
    nq = seq // QB
    ncp = seq // CMP_STRIDE
    nc = ncp - 1
    ns = seq // SEL_BLOCK
    assert ns <= LANE and seq >= WINDOW + QB and seq % KW == 0
    assert CMP_BLOCK == 2 * CMP_STRIDE and SEL_BLOCK == 4 * CMP_STRIDE
    expand =((np.arange(seq)[:, None] // SEL_BLOCK) == np.arange(LANE)[None, :]).astype(np.float32)
    ng = NSA_GROUPS
    once = dict(pipeline_mode=pl.Buffered(1))
    ks_spec = pl.BlockSpec((ng, seq, HEAD_DIM), lambda b, i: (0, b, 0), **once)
    kw_spec = pl.BlockSpec((ng, seq, HEAD_DIM), lambda b, i: (1, b, 0), **once)
    vs_spec = pl.BlockSpec((ng, HEAD_DIM, seq), lambda b, i: (0, 0, b), **once)
    vw_spec = pl.BlockSpec((ng, HEAD_DIM, seq), lambda b, i: (1, 0, b), **once)
    lanes = NSA_HPG * QB
    group_state = [pltpu.VMEM((ncp + CWIN, lanes), F32), pltpu.VMEM((ncp + SUBLANE, QB), F32),
                   pltpu.VMEM((1, lanes), F32), pltpu.VMEM((1, lanes), F32), pltpu.VMEM((HEAD_DIM, lanes), F32),
                   pltpu.VMEM((KW, lanes), F32), pltpu.VMEM((KW, lanes), F32),
                   pltpu.VMEM((1, lanes), F32), pltpu.VMEM((1, lanes), F32)]
    assert len(group_state) == NSA_STATE
    return pl.pallas_call(
        functools.partial(_nsa_kernel, seq=seq, nc=nc),
        grid=(bsz, nq),
        in_specs=[pl.BlockSpec(memory_space=pltpu.SMEM),
                  pl.BlockSpec((NSA_HEADS, HEAD_DIM, QB), lambda b, i: (0, 0, b * nq + i)),
                  pl.BlockSpec((ng, GATE_ROWS, QB), lambda b, i: (0, 0, b * nq + i)),
                  pl.BlockSpec((ng, ncp, HEAD_DIM), lambda b, i: (0, b, 0)),
                  pl.BlockSpec((ng, HEAD_DIM, ncp), lambda b, i: (0, 0, b)),
                  ks_spec, vs_spec, kw_spec, vw_spec,
                  pl.BlockSpec((NSA_HEADS, 3, LANE, LANE), lambda b, i: (0, 0, 0, 0)),
                  pl.BlockSpec((NSA_HEADS, CWIN, LANE), lambda b, i: (0, 0, 0)),
                  pl.BlockSpec((seq, LANE), lambda b, i: (0, 0))],
        out_specs=pl.BlockSpec((QB, NSA_HEADS * HEAD_DIM), lambda b, i: (b * nq + i, 0)),
        out_shape=jax.ShapeDtypeStruct((bsz * seq, NSA_HEADS * HEAD_DIM), BF16),
        scratch_shapes=group_state * ng,
        compiler_params=_cparams(("arbitrary", "arbitrary")),
        name="nsa_attention",
    )(jnp.reshape(logit_bound, (1,)).astype(F32), q_t, gates_t, kc, vc_t, k_sw, v_sw_t, k_sw, v_sw_t, dt, dc,
      jnp.asarray(expand, BF16))


MLA_HPS = 2


def _mla_kernel(bound_ref, qt_ref, k_ref, vt_ref, o_ref, *scratch):
    qi = pl.program_id(2)
    chains = [scratch[3 * h:3 * h + 3] for h in range(MLA_HPS)]
    sbuf = [scratch[(3 + b) * MLA_HPS:(4 + b) * MLA_HPS] for b in range(2)]
    cbuf = [scratch[(5 + b) * MLA_HPS:(6 + b) * MLA_HPS] for b in range(2)]
    c_diag = pl.multiple_of(qi * KW, KW)
    kpos = lax.broadcasted_iota(jnp.int32, (KW, KW), 0)
    tq = lax.broadcasted_iota(jnp.int32, (KW, KW), 1)
    dv = vt_ref.shape[1]

    def attend(bounded):
        for ch in chains:
            _flash_init(*ch)

        def qk_stage(c, buf):
            c0 = pl.multiple_of(c * KW, KW)
            ahead = []
            for h in range(MLA_HPS):
                s = _dot(k_ref[h, pl.ds(c0, KW), :], qt_ref[h])
                sbuf[buf][h][...] = s
                if bounded:
                    ahead.append(s[KW - 8:KW])
                else:
                    ahead.append(jnp.max(s, axis=0, keepdims=True))
                    cbuf[buf][h][...] = ahead[-1]
            return ahead

        def soft_stage(c, buf, ahead):
            c0 = pl.multiple_of(c * KW, KW)
            for h, (m_ref, l_ref, acc_ref) in enumerate(chains):
                after = None if ahead is None else ahead[h]
                if bounded:
                    _flash_accumulate(sbuf[buf][h][...], vt_ref[h, :, pl.ds(c0, KW)], l_ref, acc_ref, after=after)
                else:
                    _flash_update(sbuf[buf][h][...], vt_ref[h, :, pl.ds(c0, KW)], m_ref, l_ref, acc_ref,
                                  col_max=cbuf[buf][h][...], after=after)

        _pipelined_chunks(qi, qk_stage, soft_stage)
        causal = jnp.where(kpos <= tq, 0.0, NEG)
        scores = [_dot(k_ref[h, pl.ds(c_diag, KW), :], qt_ref[h]) + causal for h in range(MLA_HPS)]
        for h, (m_ref, l_ref, acc_ref) in enumerate(chains):
            if bounded:
                _flash_accumulate(scores[h], vt_ref[h, :, pl.ds(c_diag, KW)], l_ref, acc_ref)
                o_t = _sum_result(l_ref, acc_ref)
            else:
                _flash_update(scores[h], vt_ref[h, :, pl.ds(c_diag, KW)], m_ref, l_ref, acc_ref)
                o_t = _flash_result(m_ref, l_ref, acc_ref)
            o_ref[:, h * dv:(h + 1) * dv] = o_t.T.astype(o_ref.dtype)

    bounded_ok = bound_ref[0] <= SAFE_LOG2_BOUND
    pl.when(bounded_ok)(lambda: attend(True))
    pl.when(jnp.logical_not(bounded_ok))(lambda: attend(False))


def mla_attention(logit_bound, q_t, k, v_t, bsz, seq):
    nh, dqk, _ = q_t.shape
    dv = v_t.shape[1]
    nq = seq // KW
    hps = MLA_HPS
    state = [pltpu.VMEM((1, KW), F32), pltpu.VMEM((1, KW), F32), pltpu.VMEM((dv, KW), F32)] * hps
    state += [pltpu.VMEM((KW, KW), F32)] * (2 * hps)
    state += [pltpu.VMEM((1, KW), F32)] * (2 * hps)
    return pl.pallas_call(
        _mla_kernel,
        grid=(bsz, nh // hps, nq),
        in_specs=[pl.BlockSpec(memory_space=pltpu.SMEM),
                  pl.BlockSpec((hps, dqk, KW), lambda b, h, i: (h, 0, b * nq + i)),
                  pl.BlockSpec((hps, seq, dqk), lambda b, h, i: (h, b, 0)),
                  pl.BlockSpec((hps, dv, seq), lambda b, h, i: (h, 0, b))],
        out_specs=pl.BlockSpec((KW, hps * dv), lambda b, h, i: (b * nq + i, h)),
        out_shape=jax.ShapeDtypeStruct((bsz * seq, nh * dv), BF16),
        scratch_shapes=state,
        compiler_params=_cparams(("arbitrary", "arbitrary", "arbitrary")),
        name="mla_attention",
    )(jnp.reshape(logit_bound, (1,)).astype(F32), q_t, k, v_t)


INT_MIN = -2 ** 31
NEG_KEY = int(np.array(NEG, np.float32).view(np.int32)) ^ 0x7FFFFFFF
KEY_BITS = 32
SURE_BITS = 22


def _sort_key(x):
    bits = pltpu.bitcast(x + 0.0, jnp.int32)
    return jnp.where(bits < 0, bits ^ 0x7FFFFFFF, bits)


def _dsa_kernel(bound_ref, iqt_ref, iwt_ref, ik_ref, qt_ref, k_ref, vt_ref, dt_ref, o_ref,
                key_ref, *state, seq, k_sel):
    qi = pl.program_id(1)
    q0 = qi * QB
    n_chunk = (q0 + QB + KW - 1) // KW
    n_rest = seq - n_chunk * KW
    kpos = lax.broadcasted_iota(jnp.int32, (KW, QB), 0)
    tq = q0 + lax.broadcasted_iota(jnp.int32, (KW, QB), 1)
    hpp = KW // QB

    def score_chunk(c, x):
        c0 = pl.multiple_of(c * KW, KW)
        ikc = ik_ref[pl.ds(c0, KW), :]
        acc = jnp.zeros((KW, QB), F32)
        for piece in range(IDX_HEADS // hpp):
            sl = slice(piece * KW, (piece + 1) * KW)
            s = jnp.maximum(_dot(ikc, iqt_ref[0, :, sl]), 0.0) * iwt_ref[0, :, sl]
            for j in range(hpp):
                acc = acc + s[:, j * QB:(j + 1) * QB]
        acc = jnp.where(c0 + kpos <= tq, acc, NEG)
        key_ref[pl.ds(c0, KW), :] = _sort_key(acc)
        return x

    lax.fori_loop(0, n_chunk, score_chunk, 0)

    def count(pred):
        def chunk_hits(c):
            c0 = pl.multiple_of(c * KW, KW)
            hit = jnp.where(pred(key_ref[pl.ds(c0, KW), :], c0), 1.0, 0.0)
            parts = [hit[SUBLANE * i:SUBLANE * (i + 1)] for i in range(KW // SUBLANE)]
            while len(parts) > 1:
                parts = [parts[i] + parts[i + 1] for i in range(0, len(parts), 2)]
            return parts[0]

        def body(j, acc):
            second = 2 * j + 1
            weight = jnp.where(second < n_chunk, 1.0, 0.0)
            return acc + chunk_hits(2 * j) + chunk_hits(jnp.minimum(second, n_chunk - 1)) * weight

        acc = lax.fori_loop(0, (n_chunk + 1) // 2, body, jnp.zeros((SUBLANE, QB), F32))
        return jnp.sum(acc, axis=0, keepdims=True)

    rest = n_rest.astype(F32)
    kf = float(k_sel)

    def bit_step(i, st):
        u, thr_s, settled = st
        bit = jnp.left_shift(jnp.int32(1), KEY_BITS - 1 - i)
        trial = (u | bit) ^ INT_MIN
        cnt = count(lambda keys, c0: keys >= trial) + jnp.where(NEG_KEY >= trial, rest, 0.0)
        new = (cnt == kf) & (settled < 0.5)
        return (jnp.where(cnt >= kf, u | bit, u), jnp.where(new, trial, thr_s), jnp.where(new, 1.0, settled))

    st = (jnp.zeros((1, QB), jnp.int32), jnp.zeros((1, QB), jnp.int32), jnp.zeros((1, QB), F32))
    st = lax.fori_loop(0, SURE_BITS, bit_step, st)
    _, (u, thr_s, settled) = lax.while_loop(
        lambda c: (c[0] < KEY_BITS) & (jnp.min(c[1][2]) < 0.5),
        lambda c: (c[0] + 1, bit_step(c[0], c[1])), (jnp.int32(SURE_BITS), st))
    is_settled = settled > 0.5
    thr = jnp.where(is_settled, thr_s, u ^ INT_MIN)

    def edge_counts():
        return (count(lambda keys, c0: keys > thr) + jnp.where(NEG_KEY > thr, rest, 0.0),
                count(lambda keys, c0: keys >= thr) + jnp.where(NEG_KEY >= thr, rest, 0.0))

    zero_cnt = jnp.zeros((1, QB), F32)
    cnt_gt, cnt_ge = lax.cond(jnp.min(settled) > 0.5, lambda: (zero_cnt, zero_cnt), edge_counts)
    need = kf - cnt_gt
    tie_q = (cnt_ge > kf) & (thr != NEG_KEY) & jnp.logical_not(is_settled)
    idx_bits = (seq - 1).bit_length()
    no_cut = 2 ** 30

    def tie_cut():
        def idx_step(i, x):
            bit = jnp.left_shift(jnp.int32(1), idx_bits - 1 - i)
            trial = x | bit
            f = count(lambda keys, c0: (keys == thr) & (c0 + kpos < trial))
            return jnp.where(f <= need - 1.0, trial, x)
        return lax.fori_loop(0, idx_bits, idx_step, jnp.zeros((1, QB), jnp.int32))

    any_tie = jnp.max(jnp.where(tie_q, 1.0, 0.0)) > 0.0
    x_cut = lax.cond(any_tie, tie_cut, lambda: jnp.full((1, QB), no_cut, jnp.int32))
    x_cut = jnp.where(tie_q, x_cut, no_cut)

    bounded_ok = bound_ref[0] <= SAFE_LOG2_BOUND
    shift = jnp.where(bounded_ok, bound_ref[0], 0.0)

    def mask_add(c0):
        keys = key_ref[pl.ds(c0, KW), :]
        pos = c0 + kpos
        chosen = (keys > thr) | ((keys == thr) & (pos <= x_cut))
        return _tile_lanes(jnp.where(chosen & (pos <= tq), -shift, NEG), DSA_HPG)

    c_near = jnp.maximum(qi - 1, 0) // (KW // LANE)
    ng = DSA_KV_HEADS
    chains = [state[3 * g:3 * g + 3] for g in range(ng)]
    sbuf = [state[(3 + b) * ng:(4 + b) * ng] for b in range(2)]
    cbuf = [state[(5 + b) * ng:(6 + b) * ng] for b in range(2)]
    group_heads = [[g * DSA_HPG + h for h in range(DSA_HPG)] for g in range(ng)]

    def raw_scores(c0, g):
        q_t = jnp.concatenate([qt_ref[h] for h in group_heads[g]], axis=1)
        return _dot(k_ref[g, pl.ds(c0, KW), :], q_t)

    def attend(bounded):
        for ch in chains:
            _flash_init(*ch)

        def qk_stage(c, buf):
            c0 = pl.multiple_of(c * KW, KW)
            madd = mask_add(c0)
            ahead = []
            for g in range(ng):
                s = raw_scores(c0, g) + madd
                sbuf[buf][g][...] = s
                if bounded:
                    ahead.append(s[KW - 8:KW])
                else:
                    ahead.append(jnp.max(s, axis=0, keepdims=True))
                    cbuf[buf][g][...] = ahead[-1]
            return ahead

        def soft_stage(c, buf, ahead):
            c0 = pl.multiple_of(c * KW, KW)
            for g, (m_ref, l_ref, acc_ref) in enumerate(chains):
                after = None if ahead is None else ahead[g]
                if bounded:
                    _flash_accumulate(sbuf[buf][g][...], vt_ref[g, :, pl.ds(c0, KW)], l_ref, acc_ref, after=after)
                else:
                    _flash_update(sbuf[buf][g][...], vt_ref[g, :, pl.ds(c0, KW)], m_ref, l_ref, acc_ref,
                                  col_max=cbuf[buf][g][...], after=after)

        _pipelined_chunks(c_near, qk_stage, soft_stage)

        def near_step(c, x):
            c0 = pl.multiple_of(c * KW, KW)
            madd = mask_add(c0)
            scores = [raw_scores(c0, g) + madd
                      + _near_bias(dt_ref, group_heads[g], qi, c * (KW // LANE), KW // LANE)
                      for g in range(ng)]
            for g, (m_ref, l_ref, acc_ref) in enumerate(chains):
                if bounded:
                    _flash_accumulate(scores[g], vt_ref[g, :, pl.ds(c0, KW)], l_ref, acc_ref)
                else:
                    _flash_update(scores[g], vt_ref[g, :, pl.ds(c0, KW)], m_ref, l_ref, acc_ref)
            return x

        lax.fori_loop(c_near, n_chunk, near_step, 0)
        for g, (m_ref, l_ref, acc_ref) in enumerate(chains):
            o_t = _sum_result(l_ref, acc_ref) if bounded else _flash_result(m_ref, l_ref, acc_ref)
            for h in range(DSA_HPG):
                hh = group_heads[g][h]
                o_ref[:, hh * HEAD_DIM:(hh + 1) * HEAD_DIM] = o_t[:, h * QB:(h + 1) * QB].T.astype(o_ref.dtype)

    pl.when(bounded_ok)(lambda: attend(True))
    pl.when(jnp.logical_not(bounded_ok))(lambda: attend(False))


def _idx_prep_kernel(p_ref, c_ref, sa_ref, sb_ref, iqt_ref, ik_ref, iwt_ref, *, ntile):
    nslab_q = IDX_HEADS * IDX_DIM // LANE
    per = LANE // IDX_DIM
    half = IDX_ROPE // 2
    zrows = jnp.zeros((LANE - IDX_DIM, QB), F32)

    def rope_slab(x, c, sa, sb):
        return x * c + pltpu.roll(x, LANE - half, axis=1) * sa + pltpu.roll(x, half, axis=1) * sb

    for t in range(ntile):
        rows = slice(t * QB, (t + 1) * QB)
        c, sa, sb = c_ref[rows, :], sa_ref[rows, :], sb_ref[rows, :]
        cols = []
        for s in range(nslab_q):
            x_t = (rope_slab(p_ref[s, rows, :], c, sa, sb) * IDX_DIM ** -0.5).T
            for j in range(per):
                cols.append(jnp.concatenate([x_t[j * IDX_DIM:(j + 1) * IDX_DIM], zrows], axis=0))
        iqt_ref[t] = jnp.concatenate(cols, axis=1).astype(iqt_ref.dtype)
        tail = p_ref[nslab_q, rows, :]
        lane = lax.broadcasted_iota(jnp.int32, (QB, LANE), 1)
        ik_ref[rows, :] = jnp.where(lane < IDX_DIM, rope_slab(tail, c, sa, sb), 0.0).astype(ik_ref.dtype)
        w_t = (tail * IDX_HEADS ** -0.5).T
        iwt_ref[t] = jnp.concatenate([w_t[IDX_DIM + h:IDX_DIM + h + 1, :] for h in range(IDX_HEADS)], axis=1)


def indexer_operands(proj, seq, tm=512):
    _, m, _ = proj.shape
    ntile = tm // QB
    tps = seq // tm
    cos, sin = _rope_tables(seq, IDX_ROPE)
    zero = jnp.zeros_like(sin)
    rest = IDX_DIM - IDX_ROPE
    per = LANE // IDX_DIM
    c_tab = jnp.tile(jnp.concatenate([cos, cos, jnp.ones((seq, rest), F32)], axis=1), (1, per))
    sa_tab = jnp.tile(jnp.concatenate([-sin, zero, jnp.zeros((seq, rest), F32)], axis=1), (1, per))
    sb_tab = jnp.tile(jnp.concatenate([zero, sin, jnp.zeros((seq, rest), F32)], axis=1), (1, per))
    lanes = IDX_HEADS * QB
    tab_spec = pl.BlockSpec((tm, LANE), lambda i: (i % tps, 0))
    return pl.pallas_call(
        functools.partial(_idx_prep_kernel, ntile=ntile),
        grid=(m // tm,),
        in_specs=[pl.BlockSpec((proj.shape[0], tm, LANE), lambda i: (0, i, 0)), tab_spec, tab_spec, tab_spec],
        out_specs=[pl.BlockSpec((ntile, LANE, lanes), lambda i: (i, 0, 0)),
                   pl.BlockSpec((tm, LANE), lambda i: (i, 0)),
                   pl.BlockSpec((ntile, 1, lanes), lambda i: (i, 0, 0))],
        out_shape=[jax.ShapeDtypeStruct((m // QB, LANE, lanes), BF16),
                   jax.ShapeDtypeStruct((m, LANE), BF16),
                   jax.ShapeDtypeStruct((m // QB, 1, lanes), F32)],
        compiler_params=_cparams(("arbitrary",)),
        name="dsa_indexer_operands",
    )(proj, c_tab, sa_tab, sb_tab)


def dsa_attention(logit_bound, iq_t, iw_t, ik, q_t, k, v_t, dt, bsz, seq):
    nq = seq // QB
    k_sel = min(DSA_TOPK_MAX, seq // 4)
    assert seq % KW == 0
    lanes = DSA_HPG * QB
    return pl.pallas_call(
        functools.partial(_dsa_kernel, seq=seq, k_sel=k_sel),
        grid=(bsz, nq),
        in_specs=[pl.BlockSpec(memory_space=pltpu.SMEM),
                  pl.BlockSpec((1, LANE, IDX_HEADS * QB), lambda b, i: (b * nq + i, 0, 0)),
                  pl.BlockSpec((1, 1, IDX_HEADS * QB), lambda b, i: (b * nq + i, 0, 0)),
                  pl.BlockSpec((seq, LANE), lambda b, i: (b, 0)),
                  pl.BlockSpec((DSA_HEADS, HEAD_DIM, QB), lambda b, i: (0, 0, b * nq + i)),
                  pl.BlockSpec((DSA_KV_HEADS, seq, HEAD_DIM), lambda b, i: (0, b, 0),
                               pipeline_mode=pl.Buffered(1)),
                  pl.BlockSpec((DSA_KV_HEADS, HEAD_DIM, seq), lambda b, i: (0, 0, b),
                               pipeline_mode=pl.Buffered(1)),
                  pl.BlockSpec((DSA_HEADS, 3, LANE, LANE), lambda b, i: (0, 0, 0, 0),
                               pipeline_mode=pl.Buffered(1))],
        out_specs=pl.BlockSpec((QB, DSA_HEADS * HEAD_DIM), lambda b, i: (b * nq + i, 0)),
        out_shape=jax.ShapeDtypeStruct((bsz * seq, DSA_HEADS * HEAD_DIM), BF16),
        scratch_shapes=[pltpu.VMEM((seq, QB), jnp.int32)]
        + [pltpu.VMEM((1, lanes), F32), pltpu.VMEM((1, lanes), F32),
           pltpu.VMEM((HEAD_DIM, lanes), F32)] * DSA_KV_HEADS
        + [pltpu.VMEM((KW, lanes), F32)] * (2 * DSA_KV_HEADS)
        + [pltpu.VMEM((1, lanes), F32)] * (2 * DSA_KV_HEADS),
        compiler_params=_cparams(("arbitrary", "arbitrary")),
        name="dsa_attention",
    )(jnp.reshape(logit_bound, (1,)).astype(F32), iq_t, iw_t, ik, q_t, k, v_t, dt)


def _rope_tables(seq, dim):
    half = dim // 2
    inv = ROPE_THETA ** (-jnp.arange(half, dtype=F32) / half)
    ang = jnp.arange(seq, dtype=F32)[:, None] * inv[None, :]
    return jnp.cos(ang), jnp.sin(ang)


def _logit_bound(gq, gk, dim, scale):
    return dim * scale * jnp.max(jnp.abs(gq)) * jnp.max(jnp.abs(gk)) * (1.0 + 2.0 ** -7)


def _pad_cols(w, n):
    return jnp.pad(w, ((0, 0), (0, n - w.shape[1])))


def _t(x):
    return jnp.swapaxes(x, -1, -2)


def _even_mixer(h, x2, gate, next_norm, dt, dc, bias_bound, bsz, seq, w_in, w_out, nsa_qk_g, cmp_pe, cmp_w1, cmp_b1,
                cmp_w2, cmp_b2, q_norm_g, kv_norm_g, w_uq, w_ukv, nope_g, rope_g):
    m = bsz * seq
    nq_cols = NSA_HEADS * HEAD_DIM
    nkv_cols = 6 * NSA_GROUPS * HEAD_DIM
    ngate = 3 * NSA_HEADS
    o_gate = nq_cols + nkv_cols
    o_cq = o_gate + ngate
    o_ckv = o_cq + MLA_Q_RANK
    o_kpe = o_ckv + MLA_KV_RANK
    gw = NSA_GROUPS * HEAD_DIM
    kvw = [w_in[:, nq_cols + i * gw:nq_cols + (i + 1) * gw] for i in range(6)]
    scale = HEAD_DIM ** -0.5 * LOG2E
    tail = jnp.concatenate([w_in[:, o_kpe:], w_in[:, o_gate:o_cq]], axis=1)
    w_r = jnp.concatenate([kvw[0], kvw[1], w_in[:, o_cq:o_kpe], _pad_cols(tail, LANE)], axis=1).astype(BF16)
    q_t, k_sw, v_sw_t, proj = proj_qkv_raw(
        h, w_in[:, :nq_cols].astype(BF16), nsa_qk_g[0] * scale,
        jnp.concatenate([kvw[2], kvw[4]], axis=1).astype(BF16), nsa_qk_g[1],
        jnp.concatenate([kvw[3], kvw[5]], axis=1).astype(BF16), w_r)
    s_cq = 2 * NSA_GROUPS
    s_ckv = s_cq + MLA_Q_RANK // LANE
    s_tail = s_ckv + MLA_KV_RANK // LANE
    kvc = compress_kv(proj, 0, bsz, seq, cmp_pe, cmp_w1, cmp_b1, cmp_w2, cmp_b2, nsa_qk_g[1])
    tail_v = proj[s_tail]
    gates = tail_v[:, MLA_ROPE:MLA_ROPE + ngate].reshape(m, NSA_GROUPS, 3 * NSA_HPG)
    gates_t = jnp.pad(jnp.transpose(gates, (1, 2, 0)), ((0, 0), (0, GATE_ROWS - 3 * NSA_HPG), (0, 0)))
    nsa_bound = _logit_bound(nsa_qk_g[0], nsa_qk_g[1], HEAD_DIM, scale) + bias_bound
    o_nsa = nsa_attention(nsa_bound, q_t, gates_t, kvc[0], _t(kvc[1]), k_sw, v_sw_t,
                          dt[:NSA_HEADS], dc[:NSA_HEADS], bsz, seq)

    dq = MLA_NOPE + MLA_ROPE
    wq = w_uq.reshape(MLA_Q_RANK, MLA_HEADS, dq)
    wq_r = jnp.concatenate([wq[:, :, :MLA_NOPE].reshape(MLA_Q_RANK, -1),
                            wq[:, :, MLA_NOPE:].reshape(MLA_Q_RANK, -1)], axis=1).astype(BF16)
    cos, sin = _rope_tables(seq, MLA_ROPE)
    mscale = dq ** -0.5 * LOG2E
    side = [jnp.sqrt(MLA_NOPE * jnp.max(jnp.abs(nope_g[i])) ** 2 + MLA_ROPE * jnp.max(jnp.abs(rope_g[i])) ** 2)
            for i in range(2)]
    mla_bound = mscale * side[0] * side[1] * (1.0 + 2.0 ** -7)
    mla_shift = jnp.where(mla_bound <= SAFE_LOG2_BOUND, mla_bound, 0.0)
    q_mla_t, k_mla, v_mla_t = mla_project(proj, s_cq, s_ckv, s_tail, seq, q_norm_g, kv_norm_g, wq_r,
                                          w_ukv.astype(BF16), nope_g, rope_g, cos, sin, mscale, mla_shift)
    o_mla = mla_attention(mla_bound, q_mla_t, k_mla, v_mla_t, bsz, seq)
    w_o = w_out.astype(BF16)
    return resproj([(o_nsa, w_o[:nq_cols]), (o_mla, w_o[nq_cols:])], x2, gate, seq, next_norm)


def _odd_mixer(h, x2, gate, next_norm, dt, bias_bound, bsz, seq, w_in, w_out, qk_g):
    nq = DSA_HEADS * HEAD_DIM
    nkv = DSA_KV_HEADS * HEAD_DIM
    niq = IDX_HEADS * IDX_DIM
    o_k, o_v, o_iq = nq, nq + nkv, nq + 2 * nkv
    w_idx = w_in[:, o_iq:]
    q_t, k, v_t, proj = proj_qkv_raw(
        h, w_in[:, :o_k].astype(BF16), qk_g[0] * (HEAD_DIM ** -0.5 * LOG2E), w_in[:, o_k:o_v].astype(BF16), qk_g[1],
        w_in[:, o_v:o_iq].astype(BF16), _pad_cols(w_idx, niq + LANE).astype(BF16))
    iq_t, ik, iw_t = indexer_operands(proj, seq)
    bound = _logit_bound(qk_g[0], qk_g[1], HEAD_DIM, HEAD_DIM ** -0.5 * LOG2E) + bias_bound
    o = dsa_attention(bound, iq_t, iw_t, ik, q_t, k, v_t, dt, bsz, seq)
    return resproj([(o, w_out.astype(BF16))], x2, gate, seq, next_norm)


def _conv_ffn(h, x2, gate, next_norm, seq, w_up_all, layer, conv_w, conv_b, w_down):
    a = ffn_up(h, w_up_all, layer, conv_w, conv_b, seq)
    return resproj([(a, w_down.astype(BF16))], x2, gate, seq, next_norm)


def kernel(x, c, rel_bias, ada_w, ada_b, norm_g, ev_w_in, ev_w_out, nsa_qk_g, cmp_pe, cmp_w1, cmp_b1, cmp_w2, cmp_b2, mla_q_norm_g, mla_kv_norm_g, mla_w_uq, mla_w_ukv, mla_nope_g, mla_rope_g, od_w_in, od_w_out, dsa_qk_g, ffn_w_up, ffn_conv_w, ffn_conv_b, ffn_w_down):
    bsz, seq, d = x.shape
    depth = ada_w.shape[0]
    x2 = x.reshape(bsz * seq, d)
    mods = ada_all(c, ada_w, ada_b)
    dt, dc = bias_tiles(rel_bias)
    bias_bound = 2.0 * LOG2E * jnp.max(jnp.abs(rel_bias))
    def norm_of(i, sub):
        if i >= depth:
            return None
        shift, scale, _ = jnp.split(mods[i, sub], 3, axis=-1)
        return norm_g[i, sub], scale, shift

    g0, scale0, shift0 = norm_of(0, 0)
    h = modnorm(x2, g0, scale0, shift0, seq)
    for i in range(depth):
        j = i // 2
        gate = jnp.split(mods[i, 0], 3, axis=-1)[2]
        if i % 2 == 0:
            x2, h = _even_mixer(h, x2, gate, norm_of(i, 1), dt, dc, bias_bound, bsz, seq, ev_w_in[j],
                                ev_w_out[j], nsa_qk_g[j], cmp_pe[j], cmp_w1[j], cmp_b1[j], cmp_w2[j],
                                cmp_b2[j], mla_q_norm_g[j], mla_kv_norm_g[j], mla_w_uq[j], mla_w_ukv[j],
                                mla_nope_g[j], mla_rope_g[j])
        else:
            x2, h = _odd_mixer(h, x2, gate, norm_of(i, 1), dt, bias_bound, bsz, seq, od_w_in[j], od_w_out[j],
                               dsa_qk_g[j])
        gate = jnp.split(mods[i, 1], 3, axis=-1)[2]
        x2, h = _conv_ffn(h, x2, gate, norm_of(i + 1, 0), seq, ffn_w_up, i, ffn_conv_w[i], ffn_conv_b[i],
                          ffn_w_down[i])
    return x2.reshape(bsz, seq, d)
```

```python
import functools
import math

import numpy as np
import jax
import jax.numpy as jnp
from jax import lax
from jax.experimental import pallas as pl
from jax.experimental.pallas import tpu as pltpu

HEAD_DIM = 128
NSA_HEADS = 8
NSA_GROUPS = 2
NSA_HPG = NSA_HEADS // NSA_GROUPS
CMP_BLOCK = 32
CMP_STRIDE = 16
CMP_HIDDEN = 256
SEL_BLOCK = 64
SEL_TOP_N = 16
WINDOW = 512
MLA_HEADS = 8
MLA_Q_RANK = 512
MLA_KV_RANK = 256
MLA_NOPE = 128
MLA_ROPE = 64
MLA_V = 128
DSA_HEADS = 16
DSA_KV_HEADS = 4
DSA_HPG = DSA_HEADS // DSA_KV_HEADS
IDX_HEADS = 16
IDX_DIM = 64
IDX_ROPE = 32
DSA_TOPK_MAX = 256
REL_BUCKETS = 32
REL_MAX_DIST = 128
CONV_WIDTH = 3
ROPE_THETA = 10000.0
EPS = 1e-6
NEG = -1e30
FORCE = 1e9

LANE = 128
SUBLANE = 8
QB = 128
VMEM_LIMIT = 56 * 1024 * 1024

F32 = jnp.float32
BF16 = jnp.bfloat16


def _t5_thresholds():
    d = np.arange(0, 4 * REL_MAX_DIST)
    half = REL_BUCKETS // 2
    val = np.log(np.maximum(d, 1) / half) / math.log(REL_MAX_DIST / half) * (REL_BUCKETS - half)
    large = np.minimum(half + np.floor(np.maximum(val, 0.0)).astype(np.int64), REL_BUCKETS - 1)
    bucket = np.where(d < half, d, large)
    return [int(np.argmax(bucket >= b)) for b in range(1, REL_BUCKETS)]


T5_THR = _t5_thresholds()
T5_FAR = T5_THR[-1]
assert T5_FAR <= LANE


def _cparams(sem):
    return pltpu.CompilerParams(dimension_semantics=sem, vmem_limit_bytes=VMEM_LIMIT)


def _dot(a, b):
    return jnp.dot(a, b, preferred_element_type=F32)


def _ada_kernel(ct_ref, w_ref, b_ref, o_ref, *, bsz):
    ct = ct_ref[...]
    a = ct * jax.nn.sigmoid(ct)
    w = w_ref[0]
    rows = [jnp.sum(a[:, b:b + 1] * w, axis=0, keepdims=True) for b in range(bsz)]
    rows.append(jnp.zeros((o_ref.shape[1] - bsz, w.shape[1]), F32))
    o_ref[0] = jnp.concatenate(rows, axis=0) + b_ref[0]


def ada_all(c, ada_w, ada_b):
    depth, two, d, n3 = ada_w.shape
    bsz = c.shape[0]
    rows = -(-bsz // SUBLANE) * SUBLANE
    assert bsz <= LANE
    ct = jnp.zeros((d, LANE), F32).at[:, :bsz].set(c.T)
    w = ada_w.reshape(depth * two, d, n3)
    b = ada_b.reshape(depth * two, 1, n3)
    tn = 512
    out = pl.pallas_call(
        functools.partial(_ada_kernel, bsz=bsz),
        grid=(depth * two, n3 // tn),
        in_specs=[pl.BlockSpec((d, LANE), lambda l, j: (0, 0)),
                  pl.BlockSpec((1, d, tn), lambda l, j: (l, 0, j)),
                  pl.BlockSpec((1, 1, tn), lambda l, j: (l, 0, j))],
        out_specs=pl.BlockSpec((1, rows, tn), lambda l, j: (l, 0, j)),
        out_shape=jax.ShapeDtypeStruct((depth * two, rows, n3), F32),
        compiler_params=_cparams(("arbitrary", "arbitrary")),
        name="ada_mod",
    )(ct, w, b)
    return out[:, :bsz].reshape(depth, two, bsz, n3)


def _modnorm_kernel(x_ref, g_ref, sc_ref, sh_ref, o_ref):
    x = x_ref[...]
    y = x * lax.rsqrt(jnp.mean(x * x, axis=-1, keepdims=True) + EPS)
    h = (y * g_ref[...]) * (1.0 + sc_ref[0]) + sh_ref[0]
    o_ref[...] = h.astype(o_ref.dtype)


def modnorm(x2, g, scale, shift, seq, tm=1024):
    m, d = x2.shape
    tpb = seq // tm
    return pl.pallas_call(
        _modnorm_kernel,
        grid=(m // tm,),
        in_specs=[pl.BlockSpec((tm, d), lambda i: (i, 0)),
                  pl.BlockSpec((1, d), lambda i: (0, 0)),
                  pl.BlockSpec((1, 1, d), lambda i: (i // tpb, 0, 0)),
                  pl.BlockSpec((1, 1, d), lambda i: (i // tpb, 0, 0))],
        out_specs=pl.BlockSpec((tm, d), lambda i: (i, 0)),
        out_shape=jax.ShapeDtypeStruct((m, d), BF16),
        compiler_params=_cparams(("arbitrary",)),
        name="modnorm",
    )(x2, g.reshape(1, d), scale.reshape(-1, 1, d), shift.reshape(-1, 1, d))


def _head_norm(y, g_ref):
    return y * lax.rsqrt(jnp.mean(y * y, axis=-1, keepdims=True) + EPS) * g_ref[...]


def _proj_qkvr_kernel(x_ref, wq_ref, wk_ref, wv_ref, wr_ref, gq_ref, gk_ref, oq_ref, ok_ref, ov_ref, or_ref):
    x = x_ref[...]
    acc = _dot(x, wq_ref[...])
    for s in range(oq_ref.shape[0]):
        oq_ref[s] = _head_norm(acc[:, s * LANE:(s + 1) * LANE], gq_ref).T.astype(oq_ref.dtype)
    acc = _dot(x, wk_ref[...])
    for s in range(ok_ref.shape[0]):
        ok_ref[s] = _head_norm(acc[:, s * LANE:(s + 1) * LANE], gk_ref).astype(ok_ref.dtype)
    acc = _dot(x, wv_ref[...])
    for s in range(ov_ref.shape[0]):
        ov_ref[s] = acc[:, s * LANE:(s + 1) * LANE].T.astype(ov_ref.dtype)
    acc = _dot(x, wr_ref[...])
    for s in range(or_ref.shape[0]):
        or_ref[s] = acc[:, s * LANE:(s + 1) * LANE]


def proj_qkv_raw(x, wq, g_q, wk, g_k, wv, wr, tm=512):
    m, k = x.shape
    nq, nk, nv, nr = (w.shape[1] // LANE for w in (wq, wk, wv, wr))
    whole = lambda w: pl.BlockSpec(w.shape, lambda i: (0, 0), pipeline_mode=pl.Buffered(1))
    gain = pl.BlockSpec((1, LANE), lambda i: (0, 0))
    return pl.pallas_call(
        _proj_qkvr_kernel,
        grid=(m // tm,),
        in_specs=[pl.BlockSpec((tm, k), lambda i: (i, 0)), whole(wq), whole(wk), whole(wv), whole(wr), gain, gain],
        out_specs=[pl.BlockSpec((nq, LANE, tm), lambda i: (0, 0, i)),
                   pl.BlockSpec((nk, tm, LANE), lambda i: (0, i, 0)),
                   pl.BlockSpec((nv, LANE, tm), lambda i: (0, 0, i)),
                   pl.BlockSpec((nr, tm, LANE), lambda i: (0, i, 0))],
        out_shape=[jax.ShapeDtypeStruct((nq, LANE, m), BF16),
                   jax.ShapeDtypeStruct((nk, m, LANE), BF16),
                   jax.ShapeDtypeStruct((nv, LANE, m), BF16),
                   jax.ShapeDtypeStruct((nr, m, LANE), F32)],
        compiler_params=_cparams(("arbitrary",)),
        name="proj_qkv_raw",
    )(x, wq, wk, wv, wr, g_q.reshape(1, LANE), g_k.reshape(1, LANE))


def _rms_rows(x, g):
    return x * lax.rsqrt(jnp.mean(x * x, axis=-1, keepdims=True) + EPS) * g


def _rope_rows(x, cos, sin):
    half = x.shape[-1] // 2
    x1, x2 = x[:, :half], x[:, half:]
    return jnp.concatenate([x1 * cos - x2 * sin, x1 * sin + x2 * cos], axis=1)


def _latent(x_ref, g_ref):
    x = jnp.concatenate([x_ref[s] for s in range(x_ref.shape[0])], axis=1)
    return _rms_rows(x, g_ref[...]).astype(BF16)


def _mla_q_kernel(shift_ref, x_ref, g_ref, w_ref, gn_ref, gr2_ref, c_ref, sa_ref, sb_ref, o_ref, *, scale):
    acc = _dot(_latent(x_ref, g_ref), w_ref[...])
    tm = acc.shape[0]
    for h in range(MLA_HEADS):
        nope = _rms_rows(acc[:, h * MLA_NOPE:(h + 1) * MLA_NOPE], gn_ref[...]) * scale
        o_ref[h, 0:MLA_NOPE, :] = nope.T.astype(o_ref.dtype)
    first = lax.broadcasted_iota(jnp.int32, (LANE - MLA_ROPE, tm), 0) == 0
    pad_rows = jnp.where(first, -shift_ref[0], 0.0)
    low = lax.broadcasted_iota(jnp.int32, (tm, LANE), 1) < MLA_ROPE
    c, sa, sb = c_ref[...], sa_ref[...], sb_ref[...]
    half = MLA_ROPE // 2
    per = LANE // MLA_ROPE
    for s in range(MLA_HEADS // per):
        x = acc[:, MLA_HEADS * MLA_NOPE + s * LANE:MLA_HEADS * MLA_NOPE + (s + 1) * LANE]
        sq = x * x
        s_low = jnp.sum(jnp.where(low, sq, 0.0), axis=-1, keepdims=True)
        s_all = jnp.sum(sq, axis=-1, keepdims=True)
        inv = jnp.where(low, lax.rsqrt(s_low / MLA_ROPE + EPS), lax.rsqrt((s_all - s_low) / MLA_ROPE + EPS))
        y = x * inv * gr2_ref[...]
        roped = y * c + pltpu.roll(y, LANE - half, axis=1) * sa + pltpu.roll(y, half, axis=1) * sb
        x_t = (roped * scale).T
        for j in range(per):
            o_ref[per * s + j, MLA_NOPE:MLA_NOPE + LANE, :] = jnp.concatenate(
                [x_t[j * MLA_ROPE:(j + 1) * MLA_ROPE], pad_rows], axis=0).astype(o_ref.dtype)


def _mla_kv_kernel(x_ref, g_ref, w_ref, tail_ref, gn_ref, gr_ref, cos_ref, sin_ref, ok_ref, ov_ref):
    acc = _dot(_latent(x_ref, g_ref), w_ref[...])
    tm = acc.shape[0]
    k_pe = _rope_rows(_rms_rows(tail_ref[0][:, :MLA_ROPE], gr_ref[...]), cos_ref[...], sin_ref[...])
    first = lax.broadcasted_iota(jnp.int32, (tm, LANE - MLA_ROPE), 1) == 0
    k_pe = jnp.concatenate([k_pe, jnp.where(first, 1.0, 0.0)], axis=1).astype(ok_ref.dtype)
    for h in range(MLA_HEADS):
        c0 = h * (MLA_NOPE + MLA_V)
        ok_ref[h, :, 0:MLA_NOPE] = _rms_rows(acc[:, c0:c0 + MLA_NOPE], gn_ref[...]).astype(ok_ref.dtype)
        ok_ref[h, :, MLA_NOPE:MLA_NOPE + LANE] = k_pe
        ov_ref[h] = acc[:, c0 + MLA_NOPE:c0 + MLA_NOPE + MLA_V].T.astype(ov_ref.dtype)


def mla_project(proj, s_cq, s_ckv, s_tail, seq, q_norm_g, kv_norm_g, wq_r, w_ukv, nope_g, rope_g, cos, sin,
                scale, shift, tm=512):
    _, m, _ = proj.shape
    kq, kkv = s_ckv - s_cq, s_tail - s_ckv
    tps = seq // tm
    dqk = MLA_NOPE + LANE
    half = MLA_ROPE // 2
    rope_specs = [pl.BlockSpec((tm, half), lambda i: (i % tps, 0))] * 2
    gain_specs = [pl.BlockSpec((1, MLA_NOPE), lambda i: (0, 0)), pl.BlockSpec((1, MLA_ROPE), lambda i: (0, 0))]
    per = LANE // MLA_ROPE
    zero = jnp.zeros_like(sin)
    c_tab = jnp.tile(jnp.concatenate([cos, cos], axis=1), (1, per))
    sa_tab = jnp.tile(jnp.concatenate([-sin, zero], axis=1), (1, per))
    sb_tab = jnp.tile(jnp.concatenate([zero, sin], axis=1), (1, per))
    tab_spec = pl.BlockSpec((tm, LANE), lambda i: (i % tps, 0))
    q_t = pl.pallas_call(
        functools.partial(_mla_q_kernel, scale=scale),
        grid=(m // tm,),
        in_specs=[pl.BlockSpec(memory_space=pltpu.SMEM),
                  pl.BlockSpec((kq, tm, LANE), lambda i: (s_cq // kq, i, 0)),
                  pl.BlockSpec((1, kq * LANE), lambda i: (0, 0)),
                  pl.BlockSpec(wq_r.shape, lambda i: (0, 0)),
                  pl.BlockSpec((1, MLA_NOPE), lambda i: (0, 0)), pl.BlockSpec((1, LANE), lambda i: (0, 0)),
                  tab_spec, tab_spec, tab_spec],
        out_specs=pl.BlockSpec((MLA_HEADS, dqk, tm), lambda i: (0, 0, i)),
        out_shape=jax.ShapeDtypeStruct((MLA_HEADS, dqk, m), BF16),
        compiler_params=_cparams(("arbitrary",)),
        name="mla_q_project",
    )(jnp.reshape(shift, (1,)).astype(F32), proj, q_norm_g.reshape(1, -1), wq_r, nope_g[0].reshape(1, -1),
      jnp.tile(rope_g[0].reshape(1, -1), (1, per)), c_tab, sa_tab, sb_tab)
    k, v_t = pl.pallas_call(
        _mla_kv_kernel,
        grid=(m // tm,),
        in_specs=[pl.BlockSpec((kkv, tm, LANE), lambda i: (s_ckv // kkv, i, 0)),
                  pl.BlockSpec((1, kkv * LANE), lambda i: (0, 0)),
                  pl.BlockSpec(w_ukv.shape, lambda i: (0, 0)),
                  pl.BlockSpec((1, tm, LANE), lambda i: (s_tail, i, 0))] + gain_specs + rope_specs,
        out_specs=[pl.BlockSpec((MLA_HEADS, tm, dqk), lambda i: (0, i, 0)),
                   pl.BlockSpec((MLA_HEADS, MLA_V, tm), lambda i: (0, 0, i))],
        out_shape=[jax.ShapeDtypeStruct((MLA_HEADS, m, dqk), BF16),
                   jax.ShapeDtypeStruct((MLA_HEADS, MLA_V, m), BF16)],
        compiler_params=_cparams(("arbitrary",)),
        name="mla_kv_project",
    )(proj, kv_norm_g.reshape(1, -1), w_ukv, proj, nope_g[1].reshape(1, -1), rope_g[1].reshape(1, -1), cos, sin)
    return q_t, k, v_t


RES_COLS = 512


def _resproj_kernel(*refs, npair, fuse_norm):
    xres_ref, gate_ref = refs[2 * npair], refs[2 * npair + 1]
    outs = refs[2 * npair + 2 + (3 if fuse_norm else 0):]
    o_ref = outs[0]
    n = o_ref.shape[1]
    for c0 in range(0, n, RES_COLS):
        cols = slice(c0, c0 + RES_COLS)
        acc = _dot(refs[0][...], refs[1][:, cols])
        for p in range(1, npair):
            acc = acc + _dot(refs[2 * p][...], refs[2 * p + 1][:, cols])
        o_ref[:, cols] = xres_ref[:, cols] + gate_ref[0][:, cols] * acc
    if fuse_norm:
        g_ref, sc_ref, sh_ref = refs[2 * npair + 2:2 * npair + 5]
        x = o_ref[...]
        y = x * lax.rsqrt(jnp.mean(x * x, axis=-1, keepdims=True) + EPS)
        outs[1][...] = ((y * g_ref[...]) * (1.0 + sc_ref[0]) + sh_ref[0]).astype(outs[1].dtype)


def resproj(pairs, xres, gate, seq, next_norm=None, tm=512):
    m, n = xres.shape
    tpb = seq // tm
    in_specs, args = [], []
    for x, w in pairs:
        k = x.shape[1]
        in_specs += [pl.BlockSpec((tm, k), lambda i: (i, 0)),
                     pl.BlockSpec((k, n), lambda i: (0, 0), pipeline_mode=pl.Buffered(1))]
        args += [x, w]
    per_batch = pl.BlockSpec((1, 1, n), lambda i: (i // tpb, 0, 0))
    in_specs += [pl.BlockSpec((tm, n), lambda i: (i, 0)), per_batch]
    args += [xres, gate.reshape(-1, 1, n)]
    out_specs = [pl.BlockSpec((tm, n), lambda i: (i, 0))]
    out_shape = [jax.ShapeDtypeStruct((m, n), F32)]
    if next_norm is not None:
        g, scale, shift = next_norm
        in_specs += [pl.BlockSpec((1, n), lambda i: (0, 0)), per_batch, per_batch]
        args += [g.reshape(1, n), scale.reshape(-1, 1, n), shift.reshape(-1, 1, n)]
        out_specs.append(pl.BlockSpec((tm, n), lambda i: (i, 0)))
        out_shape.append(jax.ShapeDtypeStruct((m, n), BF16))
    out = pl.pallas_call(
        functools.partial(_resproj_kernel, npair=len(pairs), fuse_norm=next_norm is not None),
        grid=(m // tm,),
        in_specs=in_specs,
        out_specs=out_specs,
        out_shape=out_shape,
        compiler_params=_cparams(("arbitrary",)),
        name="resproj",
    )(*args)
    return (out[0], out[1]) if next_norm is not None else (out[0], None)


HALO = 8


def _ffn_up_kernel(h_ref, wg32_ref, wv32_ref, cwg_ref, cwv_ref, cbg_ref, cbv_ref, o_ref,
                   ug_ref, uv_ref, wg_ref, wv_ref, *, tm, tiles_per_seq):
    i = pl.program_id(1)
    first = (i % tiles_per_seq) == 0

    @pl.when(i == 0)
    def _():
        wg_ref[...] = wg32_ref[...].astype(wg_ref.dtype)
        wv_ref[...] = wv32_ref[...].astype(wv_ref.dtype)

    @pl.when(first)
    def _():
        ug_ref[0:HALO, :] = jnp.zeros((HALO, ug_ref.shape[1]), F32)
        uv_ref[0:HALO, :] = jnp.zeros((HALO, uv_ref.shape[1]), F32)

    @pl.when(jnp.logical_not(first))
    def _():
        ug_ref[0:HALO, :] = ug_ref[tm:tm + HALO, :]
        uv_ref[0:HALO, :] = uv_ref[tm:tm + HALO, :]

    h = h_ref[...]
    ug_ref[HALO:HALO + tm, :] = _dot(h, wg_ref[...])
    uv_ref[HALO:HALO + tm, :] = _dot(h, wv_ref[...])

    def conv(u_ref, cw_ref, cb_ref):
        out = cb_ref[...]
        for j in range(CONV_WIDTH):
            off = HALO - (CONV_WIDTH - 1) + j
            out = out + cw_ref[j:j + 1, :] * u_ref[off:off + tm, :]
        return out

    g = conv(ug_ref, cwg_ref, cbg_ref)
    v = conv(uv_ref, cwv_ref, cbv_ref)
    o_ref[...] = (g * jax.nn.sigmoid(g) * v).astype(o_ref.dtype)


def ffn_up(h, w_up_all, layer, conv_w, conv_b, seq, tm=1024, tn=512):
    m, d = h.shape
    f = w_up_all.shape[2] // 2
    nj = f // tn
    tps = seq // tm
    cb = conv_b.reshape(1, 2 * f)
    return pl.pallas_call(
        functools.partial(_ffn_up_kernel, tm=tm, tiles_per_seq=tps),
        grid=(nj, m // tm),
        in_specs=[pl.BlockSpec((tm, d), lambda j, i: (i, 0)),
                  pl.BlockSpec((None, d, tn), lambda j, i: (layer, 0, j)),
                  pl.BlockSpec((None, d, tn), lambda j, i: (layer, 0, nj + j)),
                  pl.BlockSpec((CONV_WIDTH, tn), lambda j, i: (0, j)),
                  pl.BlockSpec((CONV_WIDTH, tn), lambda j, i: (0, nj + j)),
                  pl.BlockSpec((1, tn), lambda j, i: (0, j)),
                  pl.BlockSpec((1, tn), lambda j, i: (0, nj + j))],
        out_specs=pl.BlockSpec((tm, tn), lambda j, i: (i, j)),
        out_shape=jax.ShapeDtypeStruct((m, f), BF16),
        scratch_shapes=[pltpu.VMEM((tm + HALO, tn), F32), pltpu.VMEM((tm + HALO, tn), F32),
                        pltpu.VMEM((d, tn), BF16), pltpu.VMEM((d, tn), BF16)],
        compiler_params=_cparams(("arbitrary", "arbitrary")),
        name="ffn_up_conv",
    )(h, w_up_all, w_up_all, conv_w, conv_w, cb, cb)


LOG2E = 1.4426950408889634
CWIN = 16


def _t5_shifted(dist, tbl_ref, h):
    val = jnp.full(dist.shape, tbl_ref[0, h], F32)
    for b in range(1, REL_BUCKETS):
        val = jnp.where(dist >= T5_THR[b - 1], tbl_ref[b, h], val)
    return (val - tbl_ref[REL_BUCKETS - 1, h]) * LOG2E


def _bias_tiles_kernel(tbl_ref, dt_ref, dc_ref):
    h = pl.program_id(0)
    key = lax.broadcasted_iota(jnp.int32, (LANE, LANE), 0)
    q = lax.broadcasted_iota(jnp.int32, (LANE, LANE), 1)
    for rel in range(2):
        dt_ref[0, rel] = _t5_shifted(rel * LANE + q - key, tbl_ref, h)
    dt_ref[0, 2] = jnp.zeros((LANE, LANE), F32)
    u = lax.broadcasted_iota(jnp.int32, (CWIN, LANE), 0)
    qc = lax.broadcasted_iota(jnp.int32, (CWIN, LANE), 1)
    dc_ref[0] = _t5_shifted(qc - CMP_STRIDE * (u - CWIN // 2) - (CMP_BLOCK - 1), tbl_ref, h)


def bias_tiles(rel_bias):
    nh = rel_bias.shape[1]
    return pl.pallas_call(
        _bias_tiles_kernel,
        grid=(nh,),
        in_specs=[pl.BlockSpec(memory_space=pltpu.SMEM)],
        out_specs=[pl.BlockSpec((1, 3, LANE, LANE), lambda h: (h, 0, 0, 0)),
                   pl.BlockSpec((1, CWIN, LANE), lambda h: (h, 0, 0))],
        out_shape=[jax.ShapeDtypeStruct((nh, 3, LANE, LANE), F32),
                   jax.ShapeDtypeStruct((nh, CWIN, LANE), F32)],
        compiler_params=_cparams(("arbitrary",)),
        name="t5_bias_tiles",
    )(rel_bias)


def _compress_kernel(x_ref, pe_ref, w1_ref, b1_ref, w2_ref, b2_ref, g_ref, o_ref, *, half):
    kv = pl.program_id(0)
    nchunk = x_ref.shape[1] // CMP_STRIDE
    a = jnp.zeros((nchunk, CMP_HIDDEN), F32)
    b = jnp.zeros((nchunk, CMP_HIDDEN), F32)
    for p in range(CMP_STRIDE):
        xp = x_ref[0, pl.ds(p, nchunk, stride=CMP_STRIDE), :]
        rows = slice(p * HEAD_DIM, (p + 1) * HEAD_DIM)
        a = a + _dot((xp + pe_ref[0, p:p + 1, :]).astype(BF16), w1_ref[0, rows, :])
        q = CMP_STRIDE + p
        b = b + _dot((xp + pe_ref[0, q:q + 1, :]).astype(BF16),
                     w1_ref[0, half + p * HEAD_DIM:half + (p + 1) * HEAD_DIM, :])
    b_next = jnp.concatenate([b[1:], jnp.zeros((1, b.shape[1]), F32)], axis=0)
    hid = jax.nn.gelu(a + b_next + b1_ref[0])
    out = _dot(hid.astype(BF16), w2_ref[0]) + b2_ref[0]
    normed = out * lax.rsqrt(jnp.mean(out * out, axis=-1, keepdims=True) + EPS) * g_ref[...]
    out = jnp.where(kv == 0, normed, out)
    o_ref[0, 0] = out.astype(o_ref.dtype)


def compress_kv(proj, slab0, bsz, seq, cmp_pe, cmp_w1, cmp_b1, cmp_w2, cmp_b2, g_k):
    nslab, m, _ = proj.shape
    nchunk = seq // CMP_STRIDE
    half = CMP_STRIDE * HEAD_DIM
    del nslab, m
    return pl.pallas_call(
        functools.partial(_compress_kernel, half=half),
        grid=(2, bsz, NSA_GROUPS),
        in_specs=[pl.BlockSpec((1, seq, HEAD_DIM), lambda kv, b, g: (slab0 + 2 * kv + g, b, 0)),
                  pl.BlockSpec((1, CMP_BLOCK, HEAD_DIM), lambda kv, b, g: (kv, 0, 0)),
                  pl.BlockSpec((1, 2 * half, CMP_HIDDEN), lambda kv, b, g: (kv, 0, 0)),
                  pl.BlockSpec((1, 1, CMP_HIDDEN), lambda kv, b, g: (kv, 0, 0)),
                  pl.BlockSpec((1, CMP_HIDDEN, HEAD_DIM), lambda kv, b, g: (kv, 0, 0)),
                  pl.BlockSpec((1, 1, HEAD_DIM), lambda kv, b, g: (kv, 0, 0)),
                  pl.BlockSpec((1, HEAD_DIM), lambda kv, b, g: (0, 0))],
        out_specs=pl.BlockSpec((1, 1, nchunk, HEAD_DIM), lambda kv, b, g: (kv, g, b, 0)),
        out_shape=jax.ShapeDtypeStruct((2, NSA_GROUPS, bsz * nchunk, HEAD_DIM), BF16),
        compiler_params=_cparams(("arbitrary", "arbitrary", "arbitrary")),
        name="nsa_compress",
    )(proj, cmp_pe, cmp_w1.astype(BF16), cmp_b1.reshape(2, 1, CMP_HIDDEN), cmp_w2.astype(BF16),
      cmp_b2.reshape(2, 1, HEAD_DIM), g_k.reshape(1, HEAD_DIM))


KW = 512
PV_KEYS = 512


def _tile_lanes(x, n):
    return jnp.concatenate([x] * n, axis=1)


def _flash_init(m_ref, l_ref, acc_ref):
    m_ref[...] = jnp.full(m_ref.shape, NEG, F32)
    l_ref[...] = jnp.zeros(l_ref.shape, F32)
    acc_ref[...] = jnp.zeros(acc_ref.shape, F32)


def _zero_after(x):
    bits = pltpu.bitcast(x, jnp.int32)
    return lax.shift_right_logical(lax.shift_right_logical(bits, 16), 16).astype(F32)


def _flash_update(s, v_t, m_ref, l_ref, acc_ref, col_max=None, after=None):
    m_old = m_ref[...]
    if col_max is None:
        col_max = jnp.max(s, axis=0, keepdims=True)
    m_new = jnp.maximum(m_old, col_max)
    alpha = jnp.exp2(m_old - m_new)
    l_new = alpha * l_ref[...]
    acc = alpha * acc_ref[...]
    nk = s.shape[0]
    for k0 in range(0, nk, PV_KEYS):
        p = jnp.exp2(s[k0:k0 + PV_KEYS] - m_new)
        l_new = l_new + jnp.sum(p, axis=0, keepdims=True)
        acc = acc + _dot(v_t[:, k0:k0 + PV_KEYS], p.astype(BF16))
    l_ref[...] = l_new
    acc_ref[...] = acc
    m_ref[...] = m_new if after is None else m_new + _zero_after(after)


SAFE_LOG2_BOUND = 60.0


def _flash_accumulate(s, v_t, l_ref, acc_ref, after=None):
    l_new = l_ref[...]
    acc = acc_ref[...]
    for k0 in range(0, s.shape[0], PV_KEYS):
        p = jnp.exp2(s[k0:k0 + PV_KEYS])
        l_new = l_new + jnp.sum(p, axis=0, keepdims=True)
        acc = acc + _dot(v_t[:, k0:k0 + PV_KEYS], p.astype(BF16))
    if after is not None:
        l_new = l_new + jnp.max(_zero_after(after), axis=0, keepdims=True)
    l_ref[...] = l_new
    acc_ref[...] = acc


def _sum_result(l_ref, acc_ref):
    den = l_ref[...]
    ok = den > 0.0
    return acc_ref[...] * jnp.where(ok, 1.0 / jnp.where(ok, den, 1.0), 0.0)


def _inv_den(m, den):
    ok = m > 0.5 * NEG
    return jnp.where(ok, 1.0 / jnp.where(ok, den, 1.0), 0.0)


def _flash_result(m_ref, l_ref, acc_ref):
    return acc_ref[...] * _inv_den(m_ref[...], l_ref[...])


def _softmax_cols(s):
    m = jnp.max(s, axis=0, keepdims=True)
    p = jnp.exp2(s - m)
    return p * _inv_den(m, jnp.sum(p, axis=0, keepdims=True))


def _near_bias(dt_ref, heads, qi, kt0, ntile):
    rows = []
    for j in range(ntile):
        rel = jnp.clip(qi - (kt0 + j), 0, 2)
        rows.append(jnp.concatenate([dt_ref[h, rel] for h in heads], axis=1))
    return jnp.concatenate(rows, axis=0)


def _pipelined_chunks(n, qk_stage, soft_stage):
    @pl.when(n > 0)
    def _():
        qk_stage(0, 0)

    def pair(p, x):
        c = 2 * p
        ahead = qk_stage(c + 1, 1)
        soft_stage(c, 0, ahead)
        ahead = qk_stage(jnp.minimum(c + 2, n - 1), 0)
        soft_stage(c + 1, 1, ahead)
        return x

    lax.fori_loop(0, n // 2, pair, 0)

    @pl.when(n % 2 == 1)
    def _():
        soft_stage(n - 1, 0, None)


NSA_STATE = 9
GATE_ROWS = -(-3 * NSA_HPG // SUBLANE) * SUBLANE


def _nsa_kernel(bound_ref, qt_ref, gt_ref, kc_ref, vct_ref, ks_ref, vst_ref, kw_ref, vwt_ref,
                dt_ref, dc_ref, ext_ref, o_ref, *scratch, seq, nc):
    ng = NSA_GROUPS
    state = [scratch[NSA_STATE * g:NSA_STATE * (g + 1)] for g in range(ng)]
    qi = pl.program_id(1)
    q0 = qi * QB
    hpg = NSA_HPG
    ncp = kc_ref.shape[1]
    ns = seq // SEL_BLOCK
    group_heads = [[g * hpg + h for h in range(hpg)] for g in range(ng)]
    q_ts = [jnp.concatenate([qt_ref[h] for h in group_heads[g]], axis=1) for g in range(ng)]
    pad = CWIN // 2
    wkeys = WINDOW + QB
    start = pl.multiple_of(jnp.maximum(q0 - WINDOW, 0), LANE)
    r0 = pl.multiple_of(qi * (QB // CMP_STRIDE), 8)

    s_w = []
    for g in range(ng):
        sc_ref = state[g][0]
        sc_ref[0:pad, :] = jnp.zeros((pad, hpg * QB), F32)
        sc_ref[pad + ncp:2 * pad + ncp, :] = jnp.zeros((pad, hpg * QB), F32)
        sc_ref[pad:pad + ncp, :] = _dot(kc_ref[g], q_ts[g])
        s_w.append(_dot(kw_ref[g, pl.ds(start, wkeys), :], q_ts[g]))

    ci = lax.broadcasted_iota(jnp.int32, (ncp, QB), 0)
    tc = q0 + lax.broadcasted_iota(jnp.int32, (ncp, QB), 1)
    valid_c = (ci * CMP_STRIDE + CMP_BLOCK - 1 <= tc) & (ci < nc)
    madd_c = _tile_lanes(jnp.where(valid_c, 0.0, NEG), hpg)
    oc_t, p_sum = [], []
    for g in range(ng):
        sc_ref = state[g][0]
        sc_ref[pl.ds(r0, CWIN), :] = sc_ref[pl.ds(r0, CWIN), :] + jnp.concatenate(
            [dc_ref[h] for h in group_heads[g]], axis=1)
        p_c = _softmax_cols(sc_ref[pad:pad + ncp, :] + madd_c)
        oc_t.append(_dot(vct_ref[g], p_c.astype(BF16)))
        ps = p_c[:, 0:QB]
        for h in range(1, hpg):
            ps = ps + p_c[:, h * QB:(h + 1) * QB]
        p_sum.append(ps)

    dist_w = (q0 + lax.broadcasted_iota(jnp.int32, (wkeys, QB), 1)) - (
        start + lax.broadcasted_iota(jnp.int32, (wkeys, QB), 0))
    madd_w = _tile_lanes(jnp.where((dist_w >= 0) & (dist_w < WINDOW), 0.0, NEG), hpg)
    ow_t = []
    for g in range(ng):
        p_w = _softmax_cols(s_w[g] + _near_bias(dt_ref, group_heads[g], qi, start // LANE, wkeys // LANE) + madd_w)
        ow_t.append(_dot(vwt_ref[g, :, pl.ds(start, wkeys)], p_w.astype(BF16)))

    per = SEL_BLOCK // CMP_STRIDE
    blk = lax.broadcasted_iota(jnp.int32, (LANE, QB), 0)
    t = q0 + lax.broadcasted_iota(jnp.int32, (LANE, QB), 1)
    tb = t // SEL_BLOCK
    forced = (blk == 0) | (blk == tb) | (blk == tb - 1)
    blk_f = blk.astype(F32)
    scores = []
    for g in range(ng):
        ps_ref = state[g][1]
        ps_ref[0:SUBLANE, :] = jnp.zeros((SUBLANE, QB), F32)
        ps_ref[SUBLANE:SUBLANE + ncp, :] = p_sum[g]
        band = [ps_ref[pl.ds(SUBLANE + r, ns, stride=per), :] for r in range(-1, per)]
        imp = 0.5 * band[0] + band[1] + band[2] + band[3] + 0.5 * band[4]
        if ns < LANE:
            imp = jnp.concatenate([imp, jnp.zeros((LANE - ns, QB), F32)], axis=0)
        score = jnp.where(forced, FORCE, jnp.where(blk * SEL_BLOCK <= t, imp, NEG))
        scores.append(jnp.where(blk < ns, score, -jnp.inf))
    sels = [jnp.zeros((LANE, QB), F32) for _ in range(ng)]
    for _ in range(min(SEL_TOP_N, ns)):
        for g in range(ng):
            mx = jnp.max(scores[g], axis=0, keepdims=True)
            first = jnp.min(jnp.where(scores[g] == mx, blk_f, float(LANE)), axis=0, keepdims=True)
            pick = blk_f == first
            sels[g] = jnp.where(pick, 1.0, sels[g])
            scores[g] = jnp.where(pick, -jnp.inf, scores[g])
    sel_b = [s.astype(BF16) for s in sels]

    kpos = lax.broadcasted_iota(jnp.int32, (KW, QB), 0)
    tq = q0 + lax.broadcasted_iota(jnp.int32, (KW, QB), 1)
    bounded_ok = bound_ref[0] <= SAFE_LOG2_BOUND
    shift = jnp.where(bounded_ok, bound_ref[0], 0.0)
    c_near = jnp.maximum(qi - 1, 0) // (KW // LANE)

    def scores_of(g, c0):
        chosen = _dot(ext_ref[pl.ds(c0, KW), :], sel_b[g])
        return (chosen - 1.0) * (-NEG) - shift, _dot(ks_ref[g, pl.ds(c0, KW), :], q_ts[g])

    def attend(bounded):
        for g in range(ng):
            _flash_init(*state[g][2:5])

        def qk_stage(c, buf):
            c0 = pl.multiple_of(c * KW, KW)
            ahead = []
            for g in range(ng):
                madd, s = scores_of(g, c0)
                s = s + _tile_lanes(madd, hpg)
                state[g][5 + buf][...] = s
                if bounded:
                    ahead.append(s[KW - 8:KW])
                else:
                    ahead.append(jnp.max(s, axis=0, keepdims=True))
                    state[g][7 + buf][...] = ahead[-1]
            return ahead

        def soft_stage(c, buf, ahead):
            c0 = pl.multiple_of(c * KW, KW)
            for g in range(ng):
                m_ref, l_ref, acc_ref = state[g][2:5]
                after = None if ahead is None else ahead[g]
                v_t = vst_ref[g, :, pl.ds(c0, KW)]
                if bounded:
                    _flash_accumulate(state[g][5 + buf][...], v_t, l_ref, acc_ref, after=after)
                else:
                    _flash_update(state[g][5 + buf][...], v_t, m_ref, l_ref, acc_ref,
                                  col_max=state[g][7 + buf][...], after=after)

        def near_step(c, x):
            c0 = pl.multiple_of(c * KW, KW)
            causal = jnp.where(c0 + kpos <= tq, 0.0, NEG)
            scores = []
            for g in range(ng):
                madd, s = scores_of(g, c0)
                scores.append(s + _tile_lanes(madd + causal, hpg)
                              + _near_bias(dt_ref, group_heads[g], qi, c * (KW // LANE), KW // LANE))
            for g in range(ng):
                m_ref, l_ref, acc_ref = state[g][2:5]
                if bounded:
                    _flash_accumulate(scores[g], vst_ref[g, :, pl.ds(c0, KW)], l_ref, acc_ref)
                else:
                    _flash_update(scores[g], vst_ref[g, :, pl.ds(c0, KW)], m_ref, l_ref, acc_ref)
            return x

        _pipelined_chunks(c_near, qk_stage, soft_stage)
        lax.fori_loop(c_near, qi // (KW // LANE) + 1, near_step, 0)
        for g in range(ng):
            m_ref, l_ref, acc_ref = state[g][2:5]
            acc_ref[...] = _sum_result(l_ref, acc_ref) if bounded else _flash_result(m_ref, l_ref, acc_ref)

    pl.when(bounded_ok)(lambda: attend(True))
    pl.when(jnp.logical_not(bounded_ok))(lambda: attend(False))

    for g in range(ng):
        os_t = state[g][4][...]
        gates = jax.nn.sigmoid(gt_ref[g])
        for h in range(hpg):
            sl = slice(h * QB, (h + 1) * QB)
            o_t = (gates[3 * h:3 * h + 1] * oc_t[g][:, sl] + gates[3 * h + 1:3 * h + 2] * os_t[:, sl]
                   + gates[3 * h + 2:3 * h + 3] * ow_t[g][:, sl])
            hh = group_heads[g][h]
            o_ref[:, hh * HEAD_DIM:(hh + 1) * HEAD_DIM] = o_t.T.astype(o_ref.dtype)


def nsa_attention(logit_bound, q_t, gates_t, kc, vc_t, k_sw, v_sw_t, dt, dc, bsz, seq):
    nq = seq // QB
    ncp = seq // CMP_STRIDE
    nc = ncp - 1
    ns = seq // SEL_BLOCK
    assert ns <= LANE and seq >= WINDOW + QB and seq % KW == 0
    assert CMP_BLOCK == 2 * CMP_STRIDE and SEL_BLOCK == 4 * CMP_STRIDE
    expand =((np.arange(seq)[:, None] // SEL_BLOCK) == np.arange(LANE)[None, :]).astype(np.float32)
    ng = NSA_GROUPS
    once = dict(pipeline_mode=pl.Buffered(1))
    ks_spec = pl.BlockSpec((ng, seq, HEAD_DIM), lambda b, i: (0, b, 0), **once)
    kw_spec = pl.BlockSpec((ng, seq, HEAD_DIM), lambda b, i: (1, b, 0), **once)
    vs_spec = pl.BlockSpec((ng, HEAD_DIM, seq), lambda b, i: (0, 0, b), **once)
    vw_spec = pl.BlockSpec((ng, HEAD_DIM, seq), lambda b, i: (1, 0, b), **once)
    lanes = NSA_HPG * QB
    group_state = [pltpu.VMEM((ncp + CWIN, lanes), F32), pltpu.VMEM((ncp + SUBLANE, QB), F32),
                   pltpu.VMEM((1, lanes), F32), pltpu.VMEM((1, lanes), F32), pltpu.VMEM((HEAD_DIM, lanes), F32),
                   pltpu.VMEM((KW, lanes), F32), pltpu.VMEM((KW, lanes), F32),
                   pltpu.VMEM((1, lanes), F32), pltpu.VMEM((1, lanes), F32)]
    assert len(group_state) == NSA_STATE
    return pl.pallas_call(
        functools.partial(_nsa_kernel, seq=seq, nc=nc),
        grid=(bsz, nq),
        in_specs=[pl.BlockSpec(memory_space=pltpu.SMEM),
                  pl.BlockSpec((NSA_HEADS, HEAD_DIM, QB), lambda b, i: (0, 0, b * nq + i)),
                  pl.BlockSpec((ng, GATE_ROWS, QB), lambda b, i: (0, 0, b * nq + i)),
                  pl.BlockSpec((ng, ncp, HEAD_DIM), lambda b, i: (0, b, 0)),
                  pl.BlockSpec((ng, HEAD_DIM, ncp), lambda b, i: (0, 0, b)),
                  ks_spec, vs_spec, kw_spec, vw_spec,
                  pl.BlockSpec((NSA_HEADS, 3, LANE, LANE), lambda b, i: (0, 0, 0, 0)),
                  pl.BlockSpec((NSA_HEADS, CWIN, LANE), lambda b, i: (0, 0, 0)),
                  pl.BlockSpec((seq, LANE), lambda b, i: (0, 0))],
        out_specs=pl.BlockSpec((QB, NSA_HEADS * HEAD_DIM), lambda b, i: (b * nq + i, 0)),
        out_shape=jax.ShapeDtypeStruct((bsz * seq, NSA_HEADS * HEAD_DIM), BF16),
        scratch_shapes=group_state * ng,
        compiler_params=_cparams(("arbitrary", "arbitrary")),
        name="nsa_attention",
    )(jnp.reshape(logit_bound, (1,)).astype(F32), q_t, gates_t, kc, vc_t, k_sw, v_sw_t, k_sw, v_sw_t, dt, dc,
      jnp.asarray(expand, BF16))


MLA_HPS = 2


def _mla_kernel(bound_ref, qt_ref, k_ref, vt_ref, o_ref, *scratch):
    qi = pl.program_id(2)
    chains = [scratch[3 * h:3 * h + 3] for h in range(MLA_HPS)]
    sbuf = [scratch[(3 + b) * MLA_HPS:(4 + b) * MLA_HPS] for b in range(2)]
    cbuf = [scratch[(5 + b) * MLA_HPS:(6 + b) * MLA_HPS] for b in range(2)]
    c_diag = pl.multiple_of(qi * KW, KW)
    kpos = lax.broadcasted_iota(jnp.int32, (KW, KW), 0)
    tq = lax.broadcasted_iota(jnp.int32, (KW, KW), 1)
    dv = vt_ref.shape[1]

    def attend(bounded):
        for ch in chains:
            _flash_init(*ch)

        def qk_stage(c, buf):
            c0 = pl.multiple_of(c * KW, KW)
            ahead = []
            for h in range(MLA_HPS):
                s = _dot(k_ref[h, pl.ds(c0, KW), :], qt_ref[h])
                sbuf[buf][h][...] = s
                if bounded:
                    ahead.append(s[KW - 8:KW])
                else:
                    ahead.append(jnp.max(s, axis=0, keepdims=True))
                    cbuf[buf][h][...] = ahead[-1]
            return ahead

        def soft_stage(c, buf, ahead):
            c0 = pl.multiple_of(c * KW, KW)
            for h, (m_ref, l_ref, acc_ref) in enumerate(chains):
                after = None if ahead is None else ahead[h]
                if bounded:
                    _flash_accumulate(sbuf[buf][h][...], vt_ref[h, :, pl.ds(c0, KW)], l_ref, acc_ref, after=after)
                else:
                    _flash_update(sbuf[buf][h][...], vt_ref[h, :, pl.ds(c0, KW)], m_ref, l_ref, acc_ref,
                                  col_max=cbuf[buf][h][...], after=after)

        _pipelined_chunks(qi, qk_stage, soft_stage)
        causal = jnp.where(kpos <= tq, 0.0, NEG)
        scores = [_dot(k_ref[h, pl.ds(c_diag, KW), :], qt_ref[h]) + causal for h in range(MLA_HPS)]
        for h, (m_ref, l_ref, acc_ref) in enumerate(chains):
            if bounded:
                _flash_accumulate(scores[h], vt_ref[h, :, pl.ds(c_diag, KW)], l_ref, acc_ref)
                o_t = _sum_result(l_ref, acc_ref)
            else:
                _flash_update(scores[h], vt_ref[h, :, pl.ds(c_diag, KW)], m_ref, l_ref, acc_ref)
                o_t = _flash_result(m_ref, l_ref, acc_ref)
            o_ref[:, h * dv:(h + 1) * dv] = o_t.T.astype(o_ref.dtype)

    bounded_ok = bound_ref[0] <= SAFE_LOG2_BOUND
    pl.when(bounded_ok)(lambda: attend(True))
    pl.when(jnp.logical_not(bounded_ok))(lambda: attend(False))


def mla_attention(logit_bound, q_t, k, v_t, bsz, seq):
    nh, dqk, _ = q_t.shape
    dv = v_t.shape[1]
    nq = seq // KW
    hps = MLA_HPS
    state = [pltpu.VMEM((1, KW), F32), pltpu.VMEM((1, KW), F32), pltpu.VMEM((dv, KW), F32)] * hps
    state += [pltpu.VMEM((KW, KW), F32)] * (2 * hps)
    state += [pltpu.VMEM((1, KW), F32)] * (2 * hps)
    return pl.pallas_call(
        _mla_kernel,
        grid=(bsz, nh // hps, nq),
        in_specs=[pl.BlockSpec(memory_space=pltpu.SMEM),
                  pl.BlockSpec((hps, dqk, KW), lambda b, h, i: (h, 0, b * nq + i)),
                  pl.BlockSpec((hps, seq, dqk), lambda b, h, i: (h, b, 0)),
                  pl.BlockSpec((hps, dv, seq), lambda b, h, i: (h, 0, b))],
        out_specs=pl.BlockSpec((KW, hps * dv), lambda b, h, i: (b * nq + i, h)),
        out_shape=jax.ShapeDtypeStruct((bsz * seq, nh * dv), BF16),
        scratch_shapes=state,
        compiler_params=_cparams(("arbitrary", "arbitrary", "arbitrary")),
        name="mla_attention",
    )(jnp.reshape(logit_bound, (1,)).astype(F32), q_t, k, v_t)


INT_MIN = -2 ** 31
NEG_KEY = int(np.array(NEG, np.float32).view(np.int32)) ^ 0x7FFFFFFF
KEY_BITS = 32
SURE_BITS = 22


def _sort_key(x):
    bits = pltpu.bitcast(x + 0.0, jnp.int32)
    return jnp.where(bits < 0, bits ^ 0x7FFFFFFF, bits)


def _dsa_kernel(bound_ref, iqt_ref, iwt_ref, ik_ref, qt_ref, k_ref, vt_ref, dt_ref, o_ref,
                key_ref, *state, seq, k_sel):
    qi = pl.program_id(1)
    q0 = qi * QB
    n_chunk = (q0 + QB + KW - 1) // KW
    n_rest = seq - n_chunk * KW
    kpos = lax.broadcasted_iota(jnp.int32, (KW, QB), 0)
    tq = q0 + lax.broadcasted_iota(jnp.int32, (KW, QB), 1)
    hpp = KW // QB

    def score_chunk(c, x):
        c0 = pl.multiple_of(c * KW, KW)
        ikc = ik_ref[pl.ds(c0, KW), :]
        acc = jnp.zeros((KW, QB), F32)
        for piece in range(IDX_HEADS // hpp):
            sl = slice(piece * KW, (piece + 1) * KW)
            s = jnp.maximum(_dot(ikc, iqt_ref[0, :, sl]), 0.0) * iwt_ref[0, :, sl]
            for j in range(hpp):
                acc = acc + s[:, j * QB:(j + 1) * QB]
        acc = jnp.where(c0 + kpos <= tq, acc, NEG)
        key_ref[pl.ds(c0, KW), :] = _sort_key(acc)
        return x

    lax.fori_loop(0, n_chunk, score_chunk, 0)

    def count(pred):
        def chunk_hits(c):
            c0 = pl.multiple_of(c * KW, KW)
            hit = jnp.where(pred(key_ref[pl.ds(c0, KW), :], c0), 1.0, 0.0)
            parts = [hit[SUBLANE * i:SUBLANE * (i + 1)] for i in range(KW // SUBLANE)]
            while len(parts) > 1:
                parts = [parts[i] + parts[i + 1] for i in range(0, len(parts), 2)]
            return parts[0]

        def body(j, acc):
            second = 2 * j + 1
            weight = jnp.where(second < n_chunk, 1.0, 0.0)
            return acc + chunk_hits(2 * j) + chunk_hits(jnp.minimum(second, n_chunk - 1)) * weight

        acc = lax.fori_loop(0, (n_chunk + 1) // 2, body, jnp.zeros((SUBLANE, QB), F32))
        return jnp.sum(acc, axis=0, keepdims=True)

    rest = n_rest.astype(F32)
    kf = float(k_sel)

    def bit_step(i, st):
        u, thr_s, settled = st
        bit = jnp.left_shift(jnp.int32(1), KEY_BITS - 1 - i)
        trial = (u | bit) ^ INT_MIN
        cnt = count(lambda keys, c0: keys >= trial) + jnp.where(NEG_KEY >= trial, rest, 0.0)
        new = (cnt == kf) & (settled < 0.5)
        return (jnp.where(cnt >= kf, u | bit, u), jnp.where(new, trial, thr_s), jnp.where(new, 1.0, settled))

    st = (jnp.zeros((1, QB), jnp.int32), jnp.zeros((1, QB), jnp.int32), jnp.zeros((1, QB), F32))
    st = lax.fori_loop(0, SURE_BITS, bit_step, st)
    _, (u, thr_s, settled) = lax.while_loop(
        lambda c: (c[0] < KEY_BITS) & (jnp.min(c[1][2]) < 0.5),
        lambda c: (c[0] + 1, bit_step(c[0], c[1])), (jnp.int32(SURE_BITS), st))
    is_settled = settled > 0.5
    thr = jnp.where(is_settled, thr_s, u ^ INT_MIN)

    def edge_counts():
        return (count(lambda keys, c0: keys > thr) + jnp.where(NEG_KEY > thr, rest, 0.0),
                count(lambda keys, c0: keys >= thr) + jnp.where(NEG_KEY >= thr, rest, 0.0))

    zero_cnt = jnp.zeros((1, QB), F32)
    cnt_gt, cnt_ge = lax.cond(jnp.min(settled) > 0.5, lambda: (zero_cnt, zero_cnt), edge_counts)
    need = kf - cnt_gt
    tie_q = (cnt_ge > kf) & (thr != NEG_KEY) & jnp.logical_not(is_settled)
    idx_bits = (seq - 1).bit_length()
    no_cut = 2 ** 30

    def tie_cut():
        def idx_step(i, x):
            bit = jnp.left_shift(jnp.int32(1), idx_bits - 1 - i)
            trial = x | bit
            f = count(lambda keys, c0: (keys == thr) & (c0 + kpos < trial))
            return jnp.where(f <= need - 1.0, trial, x)
        return lax.fori_loop(0, idx_bits, idx_step, jnp.zeros((1, QB), jnp.int32))

    any_tie = jnp.max(jnp.where(tie_q, 1.0, 0.0)) > 0.0
    x_cut = lax.cond(any_tie, tie_cut, lambda: jnp.full((1, QB), no_cut, jnp.int32))
    x_cut = jnp.where(tie_q, x_cut, no_cut)

    bounded_ok = bound_ref[0] <= SAFE_LOG2_BOUND
    shift = jnp.where(bounded_ok, bound_ref[0], 0.0)

    def mask_add(c0):
        keys = key_ref[pl.ds(c0, KW), :]
        pos = c0 + kpos
        chosen = (keys > thr) | ((keys == thr) & (pos <= x_cut))
        return _tile_lanes(jnp.where(chosen & (pos <= tq), -shift, NEG), DSA_HPG)

    c_near = jnp.maximum(qi - 1, 0) // (KW // LANE)
    ng = DSA_KV_HEADS
    chains = [state[3 * g:3 * g + 3] for g in range(ng)]
    sbuf = [state[(3 + b) * ng:(4 + b) * ng] for b in range(2)]
    cbuf = [state[(5 + b) * ng:(6 + b) * ng] for b in range(2)]
    group_heads = [[g * DSA_HPG + h for h in range(DSA_HPG)] for g in range(ng)]

    def raw_scores(c0, g):
        q_t = jnp.concatenate([qt_ref[h] for h in group_heads[g]], axis=1)
        return _dot(k_ref[g, pl.ds(c0, KW), :], q_t)

    def attend(bounded):
        for ch in chains:
            _flash_init(*ch)

        def qk_stage(c, buf):
            c0 = pl.multiple_of(c * KW, KW)
            madd = mask_add(c0)
            ahead = []
            for g in range(ng):
                s = raw_scores(c0, g) + madd
                sbuf[buf][g][...] = s
                if bounded:
                    ahead.append(s[KW - 8:KW])
                else:
                    ahead.append(jnp.max(s, axis=0, keepdims=True))
                    cbuf[buf][g][...] = ahead[-1]
            return ahead

        def soft_stage(c, buf, ahead):
            c0 = pl.multiple_of(c * KW, KW)
            for g, (m_ref, l_ref, acc_ref) in enumerate(chains):
                after = None if ahead is None else ahead[g]
                if bounded:
                    _flash_accumulate(sbuf[buf][g][...], vt_ref[g, :, pl.ds(c0, KW)], l_ref, acc_ref, after=after)
                else:
                    _flash_update(sbuf[buf][g][...], vt_ref[g, :, pl.ds(c0, KW)], m_ref, l_ref, acc_ref,
                                  col_max=cbuf[buf][g][...], after=after)

        _pipelined_chunks(c_near, qk_stage, soft_stage)

        def near_step(c, x):
            c0 = pl.multiple_of(c * KW, KW)
            madd = mask_add(c0)
            scores = [raw_scores(c0, g) + madd
                      + _near_bias(dt_ref, group_heads[g], qi, c * (KW // LANE), KW // LANE)
                      for g in range(ng)]
            for g, (m_ref, l_ref, acc_ref) in enumerate(chains):
                if bounded:
                    _flash_accumulate(scores[g], vt_ref[g, :, pl.ds(c0, KW)], l_ref, acc_ref)
                else:
                    _flash_update(scores[g], vt_ref[g, :, pl.ds(c0, KW)], m_ref, l_ref, acc_ref)
            return x

        lax.fori_loop(c_near, n_chunk, near_step, 0)
        for g, (m_ref, l_ref, acc_ref) in enumerate(chains):
            o_t = _sum_result(l_ref, acc_ref) if bounded else _flash_result(m_ref, l_ref, acc_ref)
            for h in range(DSA_HPG):
                hh = group_heads[g][h]
                o_ref[:, hh * HEAD_DIM:(hh + 1) * HEAD_DIM] = o_t[:, h * QB:(h + 1) * QB].T.astype(o_ref.dtype)

    pl.when(bounded_ok)(lambda: attend(True))
    pl.when(jnp.logical_not(bounded_ok))(lambda: attend(False))


def _idx_prep_kernel(p_ref, c_ref, sa_ref, sb_ref, iqt_ref, ik_ref, iwt_ref, *, ntile):
    nslab_q = IDX_HEADS * IDX_DIM // LANE
    per = LANE // IDX_DIM
    half = IDX_ROPE // 2
    zrows = jnp.zeros((LANE - IDX_DIM, QB), F32)

    def rope_slab(x, c, sa, sb):
        return x * c + pltpu.roll(x, LANE - half, axis=1) * sa + pltpu.roll(x, half, axis=1) * sb

    for t in range(ntile):
        rows = slice(t * QB, (t + 1) * QB)
        c, sa, sb = c_ref[rows, :], sa_ref[rows, :], sb_ref[rows, :]
        cols = []
        for s in range(nslab_q):
            x_t = (rope_slab(p_ref[s, rows, :], c, sa, sb) * IDX_DIM ** -0.5).T
            for j in range(per):
                cols.append(jnp.concatenate([x_t[j * IDX_DIM:(j + 1) * IDX_DIM], zrows], axis=0))
        iqt_ref[t] = jnp.concatenate(cols, axis=1).astype(iqt_ref.dtype)
        tail = p_ref[nslab_q, rows, :]
        lane = lax.broadcasted_iota(jnp.int32, (QB, LANE), 1)
        ik_ref[rows, :] = jnp.where(lane < IDX_DIM, rope_slab(tail, c, sa, sb), 0.0).astype(ik_ref.dtype)
        w_t = (tail * IDX_HEADS ** -0.5).T
        iwt_ref[t] = jnp.concatenate([w_t[IDX_DIM + h:IDX_DIM + h + 1, :] for h in range(IDX_HEADS)], axis=1)


def indexer_operands(proj, seq, tm=512):
    _, m, _ = proj.shape
    ntile = tm // QB
    tps = seq // tm
    cos, sin = _rope_tables(seq, IDX_ROPE)
    zero = jnp.zeros_like(sin)
    rest = IDX_DIM - IDX_ROPE
    per = LANE // IDX_DIM
    c_tab = jnp.tile(jnp.concatenate([cos, cos, jnp.ones((seq, rest), F32)], axis=1), (1, per))
    sa_tab = jnp.tile(jnp.concatenate([-sin, zero, jnp.zeros((seq, rest), F32)], axis=1), (1, per))
    sb_tab = jnp.tile(jnp.concatenate([zero, sin, jnp.zeros((seq, rest), F32)], axis=1), (1, per))
    lanes = IDX_HEADS * QB
    tab_spec = pl.BlockSpec((tm, LANE), lambda i: (i % tps, 0))
    return pl.pallas_call(
        functools.partial(_idx_prep_kernel, ntile=ntile),
        grid=(m // tm,),
        in_specs=[pl.BlockSpec((proj.shape[0], tm, LANE), lambda i: (0, i, 0)), tab_spec, tab_spec, tab_spec],
        out_specs=[pl.BlockSpec((ntile, LANE, lanes), lambda i: (i, 0, 0)),
                   pl.BlockSpec((tm, LANE), lambda i: (i, 0)),
                   pl.BlockSpec((ntile, 1, lanes), lambda i: (i, 0, 0))],
        out_shape=[jax.ShapeDtypeStruct((m // QB, LANE, lanes), BF16),
                   jax.ShapeDtypeStruct((m, LANE), BF16),
                   jax.ShapeDtypeStruct((m // QB, 1, lanes), F32)],
        compiler_params=_cparams(("arbitrary",)),
        name="dsa_indexer_operands",
    )(proj, c_tab, sa_tab, sb_tab)


def dsa_attention(logit_bound, iq_t, iw_t, ik, q_t, k, v_t, dt, bsz, seq):
    nq = seq // QB
    k_sel = min(DSA_TOPK_MAX, seq // 4)
    assert seq % KW == 0
    lanes = DSA_HPG * QB
    return pl.pallas_call(
        functools.partial(_dsa_kernel, seq=seq, k_sel=k_sel),
        grid=(bsz, nq),
        in_specs=[pl.BlockSpec(memory_space=pltpu.SMEM),
                  pl.BlockSpec((1, LANE, IDX_HEADS * QB), lambda b, i: (b * nq + i, 0, 0)),
                  pl.BlockSpec((1, 1, IDX_HEADS * QB), lambda b, i: (b * nq + i, 0, 0)),
                  pl.BlockSpec((seq, LANE), lambda b, i: (b, 0)),
                  pl.BlockSpec((DSA_HEADS, HEAD_DIM, QB), lambda b, i: (0, 0, b * nq + i)),
                  pl.BlockSpec((DSA_KV_HEADS, seq, HEAD_DIM), lambda b, i: (0, b, 0),
                               pipeline_mode=pl.Buffered(1)),
                  pl.BlockSpec((DSA_KV_HEADS, HEAD_DIM, seq), lambda b, i: (0, 0, b),
                               pipeline_mode=pl.Buffered(1)),
                  pl.BlockSpec((DSA_HEADS, 3, LANE, LANE), lambda b, i: (0, 0, 0, 0),
                               pipeline_mode=pl.Buffered(1))],
        out_specs=pl.BlockSpec((QB, DSA_HEADS * HEAD_DIM), lambda b, i: (b * nq + i, 0)),
        out_shape=jax.ShapeDtypeStruct((bsz * seq, DSA_HEADS * HEAD_DIM), BF16),
        scratch_shapes=[pltpu.VMEM((seq, QB), jnp.int32)]
        + [pltpu.VMEM((1, lanes), F32), pltpu.VMEM((1, lanes), F32),
           pltpu.VMEM((HEAD_DIM, lanes), F32)] * DSA_KV_HEADS
        + [pltpu.VMEM((KW, lanes), F32)] * (2 * DSA_KV_HEADS)
        + [pltpu.VMEM((1, lanes), F32)] * (2 * DSA_KV_HEADS),
        compiler_params=_cparams(("arbitrary", "arbitrary")),
        name="dsa_attention",
    )(jnp.reshape(logit_bound, (1,)).astype(F32), iq_t, iw_t, ik, q_t, k, v_t, dt)


def _rope_tables(seq, dim):
    half = dim // 2
    inv = ROPE_THETA ** (-jnp.arange(half, dtype=F32) / half)
    ang = jnp.arange(seq, dtype=F32)[:, None] * inv[None, :]
    return jnp.cos(ang), jnp.sin(ang)


def _logit_bound(gq, gk, dim, scale):
    return dim * scale * jnp.max(jnp.abs(gq)) * jnp.max(jnp.abs(gk)) * (1.0 + 2.0 ** -7)


def _pad_cols(w, n):
    return jnp.pad(w, ((0, 0), (0, n - w.shape[1])))


def _t(x):
    return jnp.swapaxes(x, -1, -2)


def _even_mixer(h, x2, gate, next_norm, dt, dc, bias_bound, bsz, seq, w_in, w_out, nsa_qk_g, cmp_pe, cmp_w1, cmp_b1,
                cmp_w2, cmp_b2, q_norm_g, kv_norm_g, w_uq, w_ukv, nope_g, rope_g):
    m = bsz * seq
    nq_cols = NSA_HEADS * HEAD_DIM
    nkv_cols = 6 * NSA_GROUPS * HEAD_DIM
    ngate = 3 * NSA_HEADS
    o_gate = nq_cols + nkv_cols
    o_cq = o_gate + ngate
    o_ckv = o_cq + MLA_Q_RANK
    o_kpe = o_ckv + MLA_KV_RANK
    gw = NSA_GROUPS * HEAD_DIM
    kvw = [w_in[:, nq_cols + i * gw:nq_cols + (i + 1) * gw] for i in range(6)]
    scale = HEAD_DIM ** -0.5 * LOG2E
    tail = jnp.concatenate([w_in[:, o_kpe:], w_in[:, o_gate:o_cq]], axis=1)
    w_r = jnp.concatenate([kvw[0], kvw[1], w_in[:, o_cq:o_kpe], _pad_cols(tail, LANE)], axis=1).astype(BF16)
    q_t, k_sw, v_sw_t, proj = proj_qkv_raw(
        h, w_in[:, :nq_cols].astype(BF16), nsa_qk_g[0] * scale,
        jnp.concatenate([kvw[2], kvw[4]], axis=1).astype(BF16), nsa_qk_g[1],
        jnp.concatenate([kvw[3], kvw[5]], axis=1).astype(BF16), w_r)
    s_cq = 2 * NSA_GROUPS
    s_ckv = s_cq + MLA_Q_RANK // LANE
    s_tail = s_ckv + MLA_KV_RANK // LANE
    kvc = compress_kv(proj, 0, bsz, seq, cmp_pe, cmp_w1, cmp_b1, cmp_w2, cmp_b2, nsa_qk_g[1])
    tail_v = proj[s_tail]
    gates = tail_v[:, MLA_ROPE:MLA_ROPE + ngate].reshape(m, NSA_GROUPS, 3 * NSA_HPG)
    gates_t = jnp.pad(jnp.transpose(gates, (1, 2, 0)), ((0, 0), (0, GATE_ROWS - 3 * NSA_HPG), (0, 0)))
    nsa_bound = _logit_bound(nsa_qk_g[0], nsa_qk_g[1], HEAD_DIM, scale) + bias_bound
    o_nsa = nsa_attention(nsa_bound, q_t, gates_t, kvc[0], _t(kvc[1]), k_sw, v_sw_t,
                          dt[:NSA_HEADS], dc[:NSA_HEADS], bsz, seq)

    dq = MLA_NOPE + MLA_ROPE
    wq = w_uq.reshape(MLA_Q_RANK, MLA_HEADS, dq)
    wq_r = jnp.concatenate([wq[:, :, :MLA_NOPE].reshape(MLA_Q_RANK, -1),
                            wq[:, :, MLA_NOPE:].reshape(MLA_Q_RANK, -1)], axis=1).astype(BF16)
    cos, sin = _rope_tables(seq, MLA_ROPE)
    mscale = dq ** -0.5 * LOG2E
    side = [jnp.sqrt(MLA_NOPE * jnp.max(jnp.abs(nope_g[i])) ** 2 + MLA_ROPE * jnp.max(jnp.abs(rope_g[i])) ** 2)
            for i in range(2)]
    mla_bound = mscale * side[0] * side[1] * (1.0 + 2.0 ** -7)
    mla_shift = jnp.where(mla_bound <= SAFE_LOG2_BOUND, mla_bound, 0.0)
    q_mla_t, k_mla, v_mla_t = mla_project(proj, s_cq, s_ckv, s_tail, seq, q_norm_g, kv_norm_g, wq_r,
                                          w_ukv.astype(BF16), nope_g, rope_g, cos, sin, mscale, mla_shift)
    o_mla = mla_attention(mla_bound, q_mla_t, k_mla, v_mla_t, bsz, seq)
    w_o = w_out.astype(BF16)
    return resproj([(o_nsa, w_o[:nq_cols]), (o_mla, w_o[nq_cols:])], x2, gate, seq, next_norm)


def _odd_mixer(h, x2, gate, next_norm, dt, bias_bound, bsz, seq, w_in, w_out, qk_g):
    nq = DSA_HEADS * HEAD_DIM
    nkv = DSA_KV_HEADS * HEAD_DIM
    niq = IDX_HEADS * IDX_DIM
    o_k, o_v, o_iq = nq, nq + nkv, nq + 2 * nkv
    w_idx = w_in[:, o_iq:]
    q_t, k, v_t, proj = proj_qkv_raw(
        h, w_in[:, :o_k].astype(BF16), qk_g[0] * (HEAD_DIM ** -0.5 * LOG2E), w_in[:, o_k:o_v].astype(BF16), qk_g[1],
        w_in[:, o_v:o_iq].astype(BF16), _pad_cols(w_idx, niq + LANE).astype(BF16))
    iq_t, ik, iw_t = indexer_operands(proj, seq)
    bound = _logit_bound(qk_g[0], qk_g[1], HEAD_DIM, HEAD_DIM ** -0.5 * LOG2E) + bias_bound
    o = dsa_attention(bound, iq_t, iw_t, ik, q_t, k, v_t, dt, bsz, seq)
    return resproj([(o, w_out.astype(BF16))], x2, gate, seq, next_norm)


def _conv_ffn(h, x2, gate, next_norm, seq, w_up_all, layer, conv_w, conv_b, w_down):
    a = ffn_up(h, w_up_all, layer, conv_w, conv_b, seq)
    return resproj([(a, w_down.astype(BF16))], x2, gate, seq, next_norm)


def kernel(x, c, rel_bias, ada_w, ada_b, norm_g, ev_w_in, ev_w_out, nsa_qk_g, cmp_pe, cmp_w1, cmp_b1, cmp_w2, cmp_b2, mla_q_norm_g, mla_kv_norm_g, mla_w_uq, mla_w_ukv, mla_nope_g, mla_rope_g, od_w_in, od_w_out, dsa_qk_g, ffn_w_up, ffn_conv_w, ffn_conv_b, ffn_w_down):
    bsz, seq, d = x.shape
    depth = ada_w.shape[0]
    x2 = x.reshape(bsz * seq, d)
    mods = ada_all(c, ada_w, ada_b)
    dt, dc = bias_tiles(rel_bias)
    bias_bound = 2.0 * LOG2E * jnp.max(jnp.abs(rel_bias))
    def norm_of(i, sub):
        if i >= depth:
            return None
        shift, scale, _ = jnp.split(mods[i, sub], 3, axis=-1)
        return norm_g[i, sub], scale, shift

    g0, scale0, shift0 = norm_of(0, 0)
    h = modnorm(x2, g0, scale0, shift0, seq)
    for i in range(depth):
        j = i // 2
        gate = jnp.split(mods[i, 0], 3, axis=-1)[2]
        if i % 2 == 0:
            x2, h = _even_mixer(h, x2, gate, norm_of(i, 1), dt, dc, bias_bound, bsz, seq, ev_w_in[j],
                                ev_w_out[j], nsa_qk_g[j], cmp_pe[j], cmp_w1[j], cmp_b1[j], cmp_w2[j],
                                cmp_b2[j], mla_q_norm_g[j], mla_kv_norm_g[j], mla_w_uq[j], mla_w_ukv[j],
                                mla_nope_g[j], mla_rope_g[j])
        else:
            x2, h = _odd_mixer(h, x2, gate, norm_of(i, 1), dt, bias_bound, bsz, seq, od_w_in[j], od_w_out[j],
                               dsa_qk_g[j])
        gate = jnp.split(mods[i, 1], 3, axis=-1)[2]
        x2, h = _conv_ffn(h, x2, gate, norm_of(i + 1, 0), seq, ffn_w_up, i, ffn_conv_w[i], ffn_conv_b[i],
                          ffn_w_down[i])
    return x2.reshape(bsz, seq, d)
```

```python
import functools
import math

import numpy as np
import jax
import jax.numpy as jnp
from jax import lax
from jax.experimental import pallas as pl
from jax.experimental.pallas import tpu as pltpu

HEAD_DIM = 128
NSA_HEADS = 8
NSA_GROUPS = 2
NSA_HPG = NSA_HEADS // NSA_GROUPS
CMP_BLOCK = 32
CMP_STRIDE = 16
CMP_HIDDEN = 256
SEL_BLOCK = 64
SEL_TOP_N = 16
WINDOW = 512
MLA_HEADS = 8
MLA_Q_RANK = 512
MLA_KV_RANK = 256
MLA_NOPE = 128
MLA_ROPE = 64
MLA_V = 128
DSA_HEADS = 16
DSA_KV_HEADS = 4
DSA_HPG = DSA_HEADS // DSA_KV_HEADS
IDX_HEADS = 16
IDX_DIM = 64
IDX_ROPE = 32
DSA_TOPK_MAX = 256
REL_BUCKETS = 32
REL_MAX_DIST = 128
CONV_WIDTH = 3
ROPE_THETA = 10000.0
EPS = 1e-6
NEG = -1e30
FORCE = 1e9

LANE = 128
SUBLANE = 8
QB = 128
VMEM_LIMIT = 56 * 1024 * 1024

F32 = jnp.float32
BF16 = jnp.bfloat16


def _t5_thresholds():
    d = np.arange(0, 4 * REL_MAX_DIST)
    half = REL_BUCKETS // 2
    val = np.log(np.maximum(d, 1) / half) / math.log(REL_MAX_DIST / half) * (REL_BUCKETS - half)
    large = np.minimum(half + np.floor(np.maximum(val, 0.0)).astype(np.int64), REL_BUCKETS - 1)
    bucket = np.where(d < half, d, large)
    return [int(np.argmax(bucket >= b)) for b in range(1, REL_BUCKETS)]


T5_THR = _t5_thresholds()
T5_FAR = T5_THR[-1]
assert T5_FAR <= LANE


def _cparams(sem):
    return pltpu.CompilerParams(dimension_semantics=sem, vmem_limit_bytes=VMEM_LIMIT)


def _dot(a, b):
    return jnp.dot(a, b, preferred_element_type=F32)


def _ada_kernel(ct_ref, w_ref, b_ref, o_ref, *, bsz):
    ct = ct_ref[...]
    a = ct * jax.nn.sigmoid(ct)
    w = w_ref[0]
    rows = [jnp.sum(a[:, b:b + 1] * w, axis=0, keepdims=True) for b in range(bsz)]
    rows.append(jnp.zeros((o_ref.shape[1] - bsz, w.shape[1]), F32))
    o_ref[0] = jnp.concatenate(rows, axis=0) + b_ref[0]


def ada_all(c, ada_w, ada_b):
    depth, two, d, n3 = ada_w.shape
    bsz = c.shape[0]
    rows = -(-bsz // SUBLANE) * SUBLANE
    assert bsz <= LANE
    ct = jnp.zeros((d, LANE), F32).at[:, :bsz].set(c.T)
    w = ada_w.reshape(depth * two, d, n3)
    b = ada_b.reshape(depth * two, 1, n3)
    tn = 512
    out = pl.pallas_call(
        functools.partial(_ada_kernel, bsz=bsz),
        grid=(depth * two, n3 // tn),
        in_specs=[pl.BlockSpec((d, LANE), lambda l, j: (0, 0)),
                  pl.BlockSpec((1, d, tn), lambda l, j: (l, 0, j)),
                  pl.BlockSpec((1, 1, tn), lambda l, j: (l, 0, j))],
        out_specs=pl.BlockSpec((1, rows, tn), lambda l, j: (l, 0, j)),
        out_shape=jax.ShapeDtypeStruct((depth * two, rows, n3), F32),
        compiler_params=_cparams(("arbitrary", "arbitrary")),
        name="ada_mod",
    )(ct, w, b)
    return out[:, :bsz].reshape(depth, two, bsz, n3)


def _modnorm_kernel(x_ref, g_ref, sc_ref, sh_ref, o_ref):
    x = x_ref[...]
    y = x * lax.rsqrt(jnp.mean(x * x, axis=-1, keepdims=True) + EPS)
    h = (y * g_ref[...]) * (1.0 + sc_ref[0]) + sh_ref[0]
    o_ref[...] = h.astype(o_ref.dtype)


def modnorm(x2, g, scale, shift, seq, tm=1024):
    m, d = x2.shape
    tpb = seq // tm
    return pl.pallas_call(
        _modnorm_kernel,
        grid=(m // tm,),
        in_specs=[pl.BlockSpec((tm, d), lambda i: (i, 0)),
                  pl.BlockSpec((1, d), lambda i: (0, 0)),
                  pl.BlockSpec((1, 1, d), lambda i: (i // tpb, 0, 0)),
                  pl.BlockSpec((1, 1, d), lambda i: (i // tpb, 0, 0))],
        out_specs=pl.BlockSpec((tm, d), lambda i: (i, 0)),
        out_shape=jax.ShapeDtypeStruct((m, d), BF16),
        compiler_params=_cparams(("arbitrary",)),
        name="modnorm",
    )(x2, g.reshape(1, d), scale.reshape(-1, 1, d), shift.reshape(-1, 1, d))


def _head_norm(y, g_ref):
    return y * lax.rsqrt(jnp.mean(y * y, axis=-1, keepdims=True) + EPS) * g_ref[...]


def _proj_qkvr_kernel(x_ref, wq_ref, wk_ref, wv_ref, wr_ref, gq_ref, gk_ref, oq_ref, ok_ref, ov_ref, or_ref):
    x = x_ref[...]
    acc = _dot(x, wq_ref[...])
    for s in range(oq_ref.shape[0]):
        oq_ref[s] = _head_norm(acc[:, s * LANE:(s + 1) * LANE], gq_ref).T.astype(oq_ref.dtype)
    acc = _dot(x, wk_ref[...])
    for s in range(ok_ref.shape[0]):
        ok_ref[s] = _head_norm(acc[:, s * LANE:(s + 1) * LANE], gk_ref).astype(ok_ref.dtype)
    acc = _dot(x, wv_ref[...])
    for s in range(ov_ref.shape[0]):
        ov_ref[s] = acc[:, s * LANE:(s + 1) * LANE].T.astype(ov_ref.dtype)
    acc = _dot(x, wr_ref[...])
    for s in range(or_ref.shape[0]):
        or_ref[s] = acc[:, s * LANE:(s + 1) * LANE]


def proj_qkv_raw(x, wq, g_q, wk, g_k, wv, wr, tm=512):
    m, k = x.shape
    nq, nk, nv, nr = (w.shape[1] // LANE for w in (wq, wk, wv, wr))
    whole = lambda w: pl.BlockSpec(w.shape, lambda i: (0, 0), pipeline_mode=pl.Buffered(1))
    gain = pl.BlockSpec((1, LANE), lambda i: (0, 0))
    return pl.pallas_call(
        _proj_qkvr_kernel,
        grid=(m // tm,),
        in_specs=[pl.BlockSpec((tm, k), lambda i: (i, 0)), whole(wq), whole(wk), whole(wv), whole(wr), gain, gain],
        out_specs=[pl.BlockSpec((nq, LANE, tm), lambda i: (0, 0, i)),
                   pl.BlockSpec((nk, tm, LANE), lambda i: (0, i, 0)),
                   pl.BlockSpec((nv, LANE, tm), lambda i: (0, 0, i)),
                   pl.BlockSpec((nr, tm, LANE), lambda i: (0, i, 0))],
        out_shape=[jax.ShapeDtypeStruct((nq, LANE, m), BF16),
                   jax.ShapeDtypeStruct((nk, m, LANE), BF16),
                   jax.ShapeDtypeStruct((nv, LANE, m), BF16),
                   jax.ShapeDtypeStruct((nr, m, LANE), F32)],
        compiler_params=_cparams(("arbitrary",)),
        name="proj_qkv_raw",
    )(x, wq, wk, wv, wr, g_q.reshape(1, LANE), g_k.reshape(1, LANE))


def _rms_rows(x, g):
    return x * lax.rsqrt(jnp.mean(x * x, axis=-1, keepdims=True) + EPS) * g


def _rope_rows(x, cos, sin):
    half = x.shape[-1] // 2
    x1, x2 = x[:, :half], x[:, half:]
    return jnp.concatenate([x1 * cos - x2 * sin, x1 * sin + x2 * cos], axis=1)


def _latent(x_ref, g_ref):
    x = jnp.concatenate([x_ref[s] for s in range(x_ref.shape[0])], axis=1)
    return _rms_rows(x, g_ref[...]).astype(BF16)


def _mla_q_kernel(shift_ref, x_ref, g_ref, w_ref, gn_ref, gr2_ref, c_ref, sa_ref, sb_ref, o_ref, *, scale):
    acc = _dot(_latent(x_ref, g_ref), w_ref[...])
    tm = acc.shape[0]
    for h in range(MLA_HEADS):
        nope = _rms_rows(acc[:, h * MLA_NOPE:(h + 1) * MLA_NOPE], gn_ref[...]) * scale
        o_ref[h, 0:MLA_NOPE, :] = nope.T.astype(o_ref.dtype)
    first = lax.broadcasted_iota(jnp.int32, (LANE - MLA_ROPE, tm), 0) == 0
    pad_rows = jnp.where(first, -shift_ref[0], 0.0)
    low = lax.broadcasted_iota(jnp.int32, (tm, LANE), 1) < MLA_ROPE
    c, sa, sb = c_ref[...], sa_ref[...], sb_ref[...]
    half = MLA_ROPE // 2
    per = LANE // MLA_ROPE
    for s in range(MLA_HEADS // per):
        x = acc[:, MLA_HEADS * MLA_NOPE + s * LANE:MLA_HEADS * MLA_NOPE + (s + 1) * LANE]
        sq = x * x
        s_low = jnp.sum(jnp.where(low, sq, 0.0), axis=-1, keepdims=True)
        s_all = jnp.sum(sq, axis=-1, keepdims=True)
        inv = jnp.where(low, lax.rsqrt(s_low / MLA_ROPE + EPS), lax.rsqrt((s_all - s_low) / MLA_ROPE + EPS))
        y = x * inv * gr2_ref[...]
        roped = y * c + pltpu.roll(y, LANE - half, axis=1) * sa + pltpu.roll(y, half, axis=1) * sb
        x_t = (roped * scale).T
        for j in range(per):
            o_ref[per * s + j, MLA_NOPE:MLA_NOPE + LANE, :] = jnp.concatenate(
                [x_t[j * MLA_ROPE:(j + 1) * MLA_ROPE], pad_rows], axis=0).astype(o_ref.dtype)


def _mla_kv_kernel(x_ref, g_ref, w_ref, tail_ref, gn_ref, gr_ref, cos_ref, sin_ref, ok_ref, ov_ref):
    acc = _dot(_latent(x_ref, g_ref), w_ref[...])
    tm = acc.shape[0]
    k_pe = _rope_rows(_rms_rows(tail_ref[0][:, :MLA_ROPE], gr_ref[...]), cos_ref[...], sin_ref[...])
    first = lax.broadcasted_iota(jnp.int32, (tm, LANE - MLA_ROPE), 1) == 0
    k_pe = jnp.concatenate([k_pe, jnp.where(first, 1.0, 0.0)], axis=1).astype(ok_ref.dtype)
    for h in range(MLA_HEADS):
        c0 = h * (MLA_NOPE + MLA_V)
        ok_ref[h, :, 0:MLA_NOPE] = _rms_rows(acc[:, c0:c0 + MLA_NOPE], gn_ref[...]).astype(ok_ref.dtype)
        ok_ref[h, :, MLA_NOPE:MLA_NOPE + LANE] = k_pe
        ov_ref[h] = acc[:, c0 + MLA_NOPE:c0 + MLA_NOPE + MLA_V].T.astype(ov_ref.dtype)


def mla_project(proj, s_cq, s_ckv, s_tail, seq, q_norm_g, kv_norm_g, wq_r, w_ukv, nope_g, rope_g, cos, sin,
                scale, shift, tm=512):
    _, m, _ = proj.shape
    kq, kkv = s_ckv - s_cq, s_tail - s_ckv
    tps = seq // tm
    dqk = MLA_NOPE + LANE
    half = MLA_ROPE // 2
    rope_specs = [pl.BlockSpec((tm, half), lambda i: (i % tps, 0))] * 2
    gain_specs = [pl.BlockSpec((1, MLA_NOPE), lambda i: (0, 0)), pl.BlockSpec((1, MLA_ROPE), lambda i: (0, 0))]
    per = LANE // MLA_ROPE
    zero = jnp.zeros_like(sin)
    c_tab = jnp.tile(jnp.concatenate([cos, cos], axis=1), (1, per))
    sa_tab = jnp.tile(jnp.concatenate([-sin, zero], axis=1), (1, per))
    sb_tab = jnp.tile(jnp.concatenate([zero, sin], axis=1), (1, per))
    tab_spec = pl.BlockSpec((tm, LANE), lambda i: (i % tps, 0))
    q_t = pl.pallas_call(
        functools.partial(_mla_q_kernel, scale=scale),
        grid=(m // tm,),
        in_specs=[pl.BlockSpec(memory_space=pltpu.SMEM),
                  pl.BlockSpec((kq, tm, LANE), lambda i: (s_cq // kq, i, 0)),
                  pl.BlockSpec((1, kq * LANE), lambda i: (0, 0)),
                  pl.BlockSpec(wq_r.shape, lambda i: (0, 0)),
                  pl.BlockSpec((1, MLA_NOPE), lambda i: (0, 0)), pl.BlockSpec((1, LANE), lambda i: (0, 0)),
                  tab_spec, tab_spec, tab_spec],
        out_specs=pl.BlockSpec((MLA_HEADS, dqk, tm), lambda i: (0, 0, i)),
        out_shape=jax.ShapeDtypeStruct((MLA_HEADS, dqk, m), BF16),
        compiler_params=_cparams(("arbitrary",)),
        name="mla_q_project",
    )(jnp.reshape(shift, (1,)).astype(F32), proj, q_norm_g.reshape(1, -1), wq_r, nope_g[0].reshape(1, -1),
      jnp.tile(rope_g[0].reshape(1, -1), (1, per)), c_tab, sa_tab, sb_tab)
    k, v_t = pl.pallas_call(
        _mla_kv_kernel,
        grid=(m // tm,),
        in_specs=[pl.BlockSpec((kkv, tm, LANE), lambda i: (s_ckv // kkv, i, 0)),
                  pl.BlockSpec((1, kkv * LANE), lambda i: (0, 0)),
                  pl.BlockSpec(w_ukv.shape, lambda i: (0, 0)),
                  pl.BlockSpec((1, tm, LANE), lambda i: (s_tail, i, 0))] + gain_specs + rope_specs,
        out_specs=[pl.BlockSpec((MLA_HEADS, tm, dqk), lambda i: (0, i, 0)),
                   pl.BlockSpec((MLA_HEADS, MLA_V, tm), lambda i: (0, 0, i))],
        out_shape=[jax.ShapeDtypeStruct((MLA_HEADS, m, dqk), BF16),
                   jax.ShapeDtypeStruct((MLA_HEADS, MLA_V, m), BF16)],
        compiler_params=_cparams(("arbitrary",)),
        name="mla_kv_project",
    )(proj, kv_norm_g.reshape(1, -1), w_ukv, proj, nope_g[1].reshape(1, -1), rope_g[1].reshape(1, -1), cos, sin)
    return q_t, k, v_t


RES_COLS = 512


def _resproj_kernel(*refs, npair, fuse_norm):
    xres_ref, gate_ref = refs[2 * npair], refs[2 * npair + 1]
    outs = refs[2 * npair + 2 + (3 if fuse_norm else 0):]
    o_ref = outs[0]
    n = o_ref.shape[1]
    for c0 in range(0, n, RES_COLS):
        cols = slice(c0, c0 + RES_COLS)
        acc = _dot(refs[0][...], refs[1][:, cols])
        for p in range(1, npair):
            acc = acc + _dot(refs[2 * p][...], refs[2 * p + 1][:, cols])
        o_ref[:, cols] = xres_ref[:, cols] + gate_ref[0][:, cols] * acc
    if fuse_norm:
        g_ref, sc_ref, sh_ref = refs[2 * npair + 2:2 * npair + 5]
        x = o_ref[...]
        y = x * lax.rsqrt(jnp.mean(x * x, axis=-1, keepdims=True) + EPS)
        outs[1][...] = ((y * g_ref[...]) * (1.0 + sc_ref[0]) + sh_ref[0]).astype(outs[1].dtype)


def resproj(pairs, xres, gate, seq, next_norm=None, tm=512):
    m, n = xres.shape
    tpb = seq // tm
    in_specs, args = [], []
    for x, w in pairs:
        k = x.shape[1]
        in_specs += [pl.BlockSpec((tm, k), lambda i: (i, 0)),
                     pl.BlockSpec((k, n), lambda i: (0, 0), pipeline_mode=pl.Buffered(1))]
        args += [x, w]
    per_batch = pl.BlockSpec((1, 1, n), lambda i: (i // tpb, 0, 0))
    in_specs += [pl.BlockSpec((tm, n), lambda i: (i, 0)), per_batch]
    args += [xres, gate.reshape(-1, 1, n)]
    out_specs = [pl.BlockSpec((tm, n), lambda i: (i, 0))]
    out_shape = [jax.ShapeDtypeStruct((m, n), F32)]
    if next_norm is not None:
        g, scale, shift = next_norm
        in_specs += [pl.BlockSpec((1, n), lambda i: (0, 0)), per_batch, per_batch]
        args += [g.reshape(1, n), scale.reshape(-1, 1, n), shift.reshape(-1, 1, n)]
        out_specs.append(pl.BlockSpec((tm, n), lambda i: (i, 0)))
        out_shape.append(jax.ShapeDtypeStruct((m, n), BF16))
    out = pl.pallas_call(
        functools.partial(_resproj_kernel, npair=len(pairs), fuse_norm=next_norm is not None),
        grid=(m // tm,),
        in_specs=in_specs,
        out_specs=out_specs,
        out_shape=out_shape,
        compiler_params=_cparams(("arbitrary",)),
        name="resproj",
    )(*args)
    return (out[0], out[1]) if next_norm is not None else (out[0], None)


HALO = 8


def _ffn_up_kernel(h_ref, wg32_ref, wv32_ref, cwg_ref, cwv_ref, cbg_ref, cbv_ref, o_ref,
                   ug_ref, uv_ref, wg_ref, wv_ref, *, tm, tiles_per_seq):
    i = pl.program_id(1)
    first = (i % tiles_per_seq) == 0

    @pl.when(i == 0)
    def _():
        wg_ref[...] = wg32_ref[...].astype(wg_ref.dtype)
        wv_ref[...] = wv32_ref[...].astype(wv_ref.dtype)

    @pl.when(first)
    def _():
        ug_ref[0:HALO, :] = jnp.zeros((HALO, ug_ref.shape[1]), F32)
        uv_ref[0:HALO, :] = jnp.zeros((HALO, uv_ref.shape[1]), F32)

    @pl.when(jnp.logical_not(first))
    def _():
        ug_ref[0:HALO, :] = ug_ref[tm:tm + HALO, :]
        uv_ref[0:HALO, :] = uv_ref[tm:tm + HALO, :]

    h = h_ref[...]
    ug_ref[HALO:HALO + tm, :] = _dot(h, wg_ref[...])
    uv_ref[HALO:HALO + tm, :] = _dot(h, wv_ref[...])

    def conv(u_ref, cw_ref, cb_ref):
        out = cb_ref[...]
        for j in range(CONV_WIDTH):
            off = HALO - (CONV_WIDTH - 1) + j
            out = out + cw_ref[j:j + 1, :] * u_ref[off:off + tm, :]
        return out

    g = conv(ug_ref, cwg_ref, cbg_ref)
    v = conv(uv_ref, cwv_ref, cbv_ref)
    o_ref[...] = (g * jax.nn.sigmoid(g) * v).astype(o_ref.dtype)


def ffn_up(h, w_up_all, layer, conv_w, conv_b, seq, tm=1024, tn=512):
    m, d = h.shape
    f = w_up_all.shape[2] // 2
    nj = f // tn
    tps = seq // tm
    cb = conv_b.reshape(1, 2 * f)
    return pl.pallas_call(
        functools.partial(_ffn_up_kernel, tm=tm, tiles_per_seq=tps),
        grid=(nj, m // tm),
        in_specs=[pl.BlockSpec((tm, d), lambda j, i: (i, 0)),
                  pl.BlockSpec((None, d, tn), lambda j, i: (layer, 0, j)),
                  pl.BlockSpec((None, d, tn), lambda j, i: (layer, 0, nj + j)),
                  pl.BlockSpec((CONV_WIDTH, tn), lambda j, i: (0, j)),
                  pl.BlockSpec((CONV_WIDTH, tn), lambda j, i: (0, nj + j)),
                  pl.BlockSpec((1, tn), lambda j, i: (0, j)),
                  pl.BlockSpec((1, tn), lambda j, i: (0, nj + j))],
        out_specs=pl.BlockSpec((tm, tn), lambda j, i: (i, j)),
        out_shape=jax.ShapeDtypeStruct((m, f), BF16),
        scratch_shapes=[pltpu.VMEM((tm + HALO, tn), F32), pltpu.VMEM((tm + HALO, tn), F32),
                        pltpu.VMEM((d, tn), BF16), pltpu.VMEM((d, tn), BF16)],
        compiler_params=_cparams(("arbitrary", "arbitrary")),
        name="ffn_up_conv",
    )(h, w_up_all, w_up_all, conv_w, conv_w, cb, cb)


LOG2E = 1.4426950408889634
CWIN = 16


def _t5_shifted(dist, tbl_ref, h):
    val = jnp.full(dist.shape, tbl_ref[0, h], F32)
    for b in range(1, REL_BUCKETS):
        val = jnp.where(dist >= T5_THR[b - 1], tbl_ref[b, h], val)
    return (val - tbl_ref[REL_BUCKETS - 1, h]) * LOG2E


def _bias_tiles_kernel(tbl_ref, dt_ref, dc_ref):
    h = pl.program_id(0)
    key = lax.broadcasted_iota(jnp.int32, (LANE, LANE), 0)
    q = lax.broadcasted_iota(jnp.int32, (LANE, LANE), 1)
    for rel in range(2):
        dt_ref[0, rel] = _t5_shifted(rel * LANE + q - key, tbl_ref, h)
    dt_ref[0, 2] = jnp.zeros((LANE, LANE), F32)
    u = lax.broadcasted_iota(jnp.int32, (CWIN, LANE), 0)
    qc = lax.broadcasted_iota(jnp.int32, (CWIN, LANE), 1)
    dc_ref[0] = _t5_shifted(qc - CMP_STRIDE * (u - CWIN // 2) - (CMP_BLOCK - 1), tbl_ref, h)


def bias_tiles(rel_bias):
    nh = rel_bias.shape[1]
    return pl.pallas_call(
        _bias_tiles_kernel,
        grid=(nh,),
        in_specs=[pl.BlockSpec(memory_space=pltpu.SMEM)],
        out_specs=[pl.BlockSpec((1, 3, LANE, LANE), lambda h: (h, 0, 0, 0)),
                   pl.BlockSpec((1, CWIN, LANE), lambda h: (h, 0, 0))],
        out_shape=[jax.ShapeDtypeStruct((nh, 3, LANE, LANE), F32),
                   jax.ShapeDtypeStruct((nh, CWIN, LANE), F32)],
        compiler_params=_cparams(("arbitrary",)),
        name="t5_bias_tiles",
    )(rel_bias)


def _compress_kernel(x_ref, pe_ref, w1_ref, b1_ref, w2_ref, b2_ref, g_ref, o_ref, *, half):
    kv = pl.program_id(0)
    nchunk = x_ref.shape[1] // CMP_STRIDE
    a = jnp.zeros((nchunk, CMP_HIDDEN), F32)
    b = jnp.zeros((nchunk, CMP_HIDDEN), F32)
    for p in range(CMP_STRIDE):
        xp = x_ref[0, pl.ds(p, nchunk, stride=CMP_STRIDE), :]
        rows = slice(p * HEAD_DIM, (p + 1) * HEAD_DIM)
        a = a + _dot((xp + pe_ref[0, p:p + 1, :]).astype(BF16), w1_ref[0, rows, :])
        q = CMP_STRIDE + p
        b = b + _dot((xp + pe_ref[0, q:q + 1, :]).astype(BF16),
                     w1_ref[0, half + p * HEAD_DIM:half + (p + 1) * HEAD_DIM, :])
    b_next = jnp.concatenate([b[1:], jnp.zeros((1, b.shape[1]), F32)], axis=0)
    hid = jax.nn.gelu(a + b_next + b1_ref[0])
    out = _dot(hid.astype(BF16), w2_ref[0]) + b2_ref[0]
    normed = out * lax.rsqrt(jnp.mean(out * out, axis=-1, keepdims=True) + EPS) * g_ref[...]
    out = jnp.where(kv == 0, normed, out)
    o_ref[0, 0] = out.astype(o_ref.dtype)


def compress_kv(proj, slab0, bsz, seq, cmp_pe, cmp_w1, cmp_b1, cmp_w2, cmp_b2, g_k):
    nslab, m, _ = proj.shape
    nchunk = seq // CMP_STRIDE
    half = CMP_STRIDE * HEAD_DIM
    del nslab, m
    return pl.pallas_call(
        functools.partial(_compress_kernel, half=half),
        grid=(2, bsz, NSA_GROUPS),
        in_specs=[pl.BlockSpec((1, seq, HEAD_DIM), lambda kv, b, g: (slab0 + 2 * kv + g, b, 0)),
                  pl.BlockSpec((1, CMP_BLOCK, HEAD_DIM), lambda kv, b, g: (kv, 0, 0)),
                  pl.BlockSpec((1, 2 * half, CMP_HIDDEN), lambda kv, b, g: (kv, 0, 0)),
                  pl.BlockSpec((1, 1, CMP_HIDDEN), lambda kv, b, g: (kv, 0, 0)),
                  pl.BlockSpec((1, CMP_HIDDEN, HEAD_DIM), lambda kv, b, g: (kv, 0, 0)),
                  pl.BlockSpec((1, 1, HEAD_DIM), lambda kv, b, g: (kv, 0, 0)),
                  pl.BlockSpec((1, HEAD_DIM), lambda kv, b, g: (0, 0))],
        out_specs=pl.BlockSpec((1, 1, nchunk, HEAD_DIM), lambda kv, b, g: (kv, g, b, 0)),
        out_shape=jax.ShapeDtypeStruct((2, NSA_GROUPS, bsz * nchunk, HEAD_DIM), BF16),
        compiler_params=_cparams(("arbitrary", "arbitrary", "arbitrary")),
        name="nsa_compress",
    )(proj, cmp_pe, cmp_w1.astype(BF16), cmp_b1.reshape(2, 1, CMP_HIDDEN), cmp_w2.astype(BF16),
      cmp_b2.reshape(2, 1, HEAD_DIM), g_k.reshape(1, HEAD_DIM))


KW = 512
PV_KEYS = 512


def _tile_lanes(x, n):
    return jnp.concatenate([x] * n, axis=1)


def _flash_init(m_ref, l_ref, acc_ref):
    m_ref[...] = jnp.full(m_ref.shape, NEG, F32)
    l_ref[...] = jnp.zeros(l_ref.shape, F32)
    acc_ref[...] = jnp.zeros(acc_ref.shape, F32)


def _zero_after(x):
    bits = pltpu.bitcast(x, jnp.int32)
    return lax.shift_right_logical(lax.shift_right_logical(bits, 16), 16).astype(F32)


def _flash_update(s, v_t, m_ref, l_ref, acc_ref, col_max=None, after=None):
    m_old = m_ref[...]
    if col_max is None:
        col_max = jnp.max(s, axis=0, keepdims=True)
    m_new = jnp.maximum(m_old, col_max)
    alpha = jnp.exp2(m_old - m_new)
    l_new = alpha * l_ref[...]
    acc = alpha * acc_ref[...]
    nk = s.shape[0]
    for k0 in range(0, nk, PV_KEYS):
        p = jnp.exp2(s[k0:k0 + PV_KEYS] - m_new)
        l_new = l_new + jnp.sum(p, axis=0, keepdims=True)
        acc = acc + _dot(v_t[:, k0:k0 + PV_KEYS], p.astype(BF16))
    l_ref[...] = l_new
    acc_ref[...] = acc
    m_ref[...] = m_new if after is None else m_new + _zero_after(after)


SAFE_LOG2_BOUND = 60.0


def _flash_accumulate(s, v_t, l_ref, acc_ref, after=None):
    l_new = l_ref[...]
    acc = acc_ref[...]
    for k0 in range(0, s.shape[0], PV_KEYS):
        p = jnp.exp2(s[k0:k0 + PV_KEYS])
        l_new = l_new + jnp.sum(p, axis=0, keepdims=True)
        acc = acc + _dot(v_t[:, k0:k0 + PV_KEYS], p.astype(BF16))
    if after is not None:
        l_new = l_new + jnp.max(_zero_after(after), axis=0, keepdims=True)
    l_ref[...] = l_new
    acc_ref[...] = acc


def _sum_result(l_ref, acc_ref):
    den = l_ref[...]
    ok = den > 0.0
    return acc_ref[...] * jnp.where(ok, 1.0 / jnp.where(ok, den, 1.0), 0.0)


def _inv_den(m, den):
    ok = m > 0.5 * NEG
    return jnp.where(ok, 1.0 / jnp.where(ok, den, 1.0), 0.0)


def _flash_result(m_ref, l_ref, acc_ref):
    return acc_ref[...] * _inv_den(m_ref[...], l_ref[...])


def _softmax_cols(s):
    m = jnp.max(s, axis=0, keepdims=True)
    p = jnp.exp2(s - m)
    return p * _inv_den(m, jnp.sum(p, axis=0, keepdims=True))


def _near_bias(dt_ref, heads, qi, kt0, ntile):
    rows = []
    for j in range(ntile):
        rel = jnp.clip(qi - (kt0 + j), 0, 2)
        rows.append(jnp.concatenate([dt_ref[h, rel] for h in heads], axis=1))
    return jnp.concatenate(rows, axis=0)


def _pipelined_chunks(n, qk_stage, soft_stage):
    @pl.when(n > 0)
    def _():
        qk_stage(0, 0)

    def pair(p, x):
        c = 2 * p
        ahead = qk_stage(c + 1, 1)
        soft_stage(c, 0, ahead)
        ahead = qk_stage(jnp.minimum(c + 2, n - 1), 0)
        soft_stage(c + 1, 1, ahead)
        return x

    lax.fori_loop(0, n // 2, pair, 0)

    @pl.when(n % 2 == 1)
    def _():
        soft_stage(n - 1, 0, None)


NSA_STATE = 9
GATE_ROWS = -(-3 * NSA_HPG // SUBLANE) * SUBLANE


def _nsa_kernel(bound_ref, qt_ref, gt_ref, kc_ref, vct_ref, ks_ref, vst_ref, kw_ref, vwt_ref,
                dt_ref, dc_ref, ext_ref, o_ref, *scratch, seq, nc):
    ng = NSA_GROUPS
    state = [scratch[NSA_STATE * g:NSA_STATE * (g + 1)] for g in range(ng)]
    qi = pl.program_id(1)
    q0 = qi * QB
    hpg = NSA_HPG
    ncp = kc_ref.shape[1]
    ns = seq // SEL_BLOCK
    group_heads = [[g * hpg + h for h in range(hpg)] for g in range(ng)]
    q_ts = [jnp.concatenate([qt_ref[h] for h in group_heads[g]], axis=1) for g in range(ng)]
    pad = CWIN // 2
    wkeys = WINDOW + QB
    start = pl.multiple_of(jnp.maximum(q0 - WINDOW, 0), LANE)
    r0 = pl.multiple_of(qi * (QB // CMP_STRIDE), 8)

    s_w = []
    for g in range(ng):
        sc_ref = state[g][0]
        sc_ref[0:pad, :] = jnp.zeros((pad, hpg * QB), F32)
        sc_ref[pad + ncp:2 * pad + ncp, :] = jnp.zeros((pad, hpg * QB), F32)
        sc_ref[pad:pad + ncp, :] = _dot(kc_ref[g], q_ts[g])
        s_w.append(_dot(kw_ref[g, pl.ds(start, wkeys), :], q_ts[g]))

    ci = lax.broadcasted_iota(jnp.int32, (ncp, QB), 0)
    tc = q0 + lax.broadcasted_iota(jnp.int32, (ncp, QB), 1)
    valid_c = (ci * CMP_STRIDE + CMP_BLOCK - 1 <= tc) & (ci < nc)
    madd_c = _tile_lanes(jnp.where(valid_c, 0.0, NEG), hpg)
    oc_t, p_sum = [], []
    for g in range(ng):
        sc_ref = state[g][0]
        sc_ref[pl.ds(r0, CWIN), :] = sc_ref[pl.ds(r0, CWIN), :] + jnp.concatenate(
            [dc_ref[h] for h in group_heads[g]], axis=1)
        p_c = _softmax_cols(sc_ref[pad:pad + ncp, :] + madd_c)
        oc_t.append(_dot(vct_ref[g], p_c.astype(BF16)))
        ps = p_c[:, 0:QB]
        for h in range(1, hpg):
            ps = ps + p_c[:, h * QB:(h + 1) * QB]
        p_sum.append(ps)

    dist_w = (q0 + lax.broadcasted_iota(jnp.int32, (wkeys, QB), 1)) - (
        start + lax.broadcasted_iota(jnp.int32, (wkeys, QB), 0))
    madd_w = _tile_lanes(jnp.where((dist_w >= 0) & (dist_w < WINDOW), 0.0, NEG), hpg)
    ow_t = []
    for g in range(ng):
        p_w = _softmax_cols(s_w[g] + _near_bias(dt_ref, group_heads[g], qi, start // LANE, wkeys // LANE) + madd_w)
        ow_t.append(_dot(vwt_ref[g, :, pl.ds(start, wkeys)], p_w.astype(BF16)))

    per = SEL_BLOCK // CMP_STRIDE
    blk = lax.broadcasted_iota(jnp.int32, (LANE, QB), 0)
    t = q0 + lax.broadcasted_iota(jnp.int32, (LANE, QB), 1)
    tb = t // SEL_BLOCK
    forced = (blk == 0) | (blk == tb) | (blk == tb - 1)
    blk_f = blk.astype(F32)
    scores = []
    for g in range(ng):
        ps_ref = state[g][1]
        ps_ref[0:SUBLANE, :] = jnp.zeros((SUBLANE, QB), F32)
        ps_ref[SUBLANE:SUBLANE + ncp, :] = p_sum[g]
        band = [ps_ref[pl.ds(SUBLANE + r, ns, stride=per), :] for r in range(-1, per)]
        imp = 0.5 * band[0] + band[1] + band[2] + band[3] + 0.5 * band[4]
        if ns < LANE:
            imp = jnp.concatenate([imp, jnp.zeros((LANE - ns, QB), F32)], axis=0)
        score = jnp.where(forced, FORCE, jnp.where(blk * SEL_BLOCK <= t, imp, NEG))
        scores.append(jnp.where(blk < ns, score, -jnp.inf))
    sels = [jnp.zeros((LANE, QB), F32) for _ in range(ng)]
    for _ in range(min(SEL_TOP_N, ns)):
        for g in range(ng):
            mx = jnp.max(scores[g], axis=0, keepdims=True)
            first = jnp.min(jnp.where(scores[g] == mx, blk_f, float(LANE)), axis=0, keepdims=True)
            pick = blk_f == first
            sels[g] = jnp.where(pick, 1.0, sels[g])
            scores[g] = jnp.where(pick, -jnp.inf, scores[g])
    sel_b = [s.astype(BF16) for s in sels]

    kpos = lax.broadcasted_iota(jnp.int32, (KW, QB), 0)
    tq = q0 + lax.broadcasted_iota(jnp.int32, (KW, QB), 1)
    bounded_ok = bound_ref[0] <= SAFE_LOG2_BOUND
    shift = jnp.where(bounded_ok, bound_ref[0], 0.0)
    c_near = jnp.maximum(qi - 1, 0) // (KW // LANE)

    def scores_of(g, c0):
        chosen = _dot(ext_ref[pl.ds(c0, KW), :], sel_b[g])
        return (chosen - 1.0) * (-NEG) - shift, _dot(ks_ref[g, pl.ds(c0, KW), :], q_ts[g])

    def attend(bounded):
        for g in range(ng):
            _flash_init(*state[g][2:5])

        def qk_stage(c, buf):
            c0 = pl.multiple_of(c * KW, KW)
            ahead = []
            for g in range(ng):
                madd, s = scores_of(g, c0)
                s = s + _tile_lanes(madd, hpg)
                state[g][5 + buf][...] = s
                if bounded:
                    ahead.append(s[KW - 8:KW])
                else:
                    ahead.append(jnp.max(s, axis=0, keepdims=True))
                    state[g][7 + buf][...] = ahead[-1]
            return ahead

        def soft_stage(c, buf, ahead):
            c0 = pl.multiple_of(c * KW, KW)
            for g in range(ng):
                m_ref, l_ref, acc_ref = state[g][2:5]
                after = None if ahead is None else ahead[g]
                v_t = vst_ref[g, :, pl.ds(c0, KW)]
                if bounded:
                    _flash_accumulate(state[g][5 + buf][...], v_t, l_ref, acc_ref, after=after)
                else:
                    _flash_update(state[g][5 + buf][...], v_t, m_ref, l_ref, acc_ref,
                                  col_max=state[g][7 + buf][...], after=after)

        def near_step(c, x):
            c0 = pl.multiple_of(c * KW, KW)
            causal = jnp.where(c0 + kpos <= tq, 0.0, NEG)
            scores = []
            for g in range(ng):
                madd, s = scores_of(g, c0)
                scores.append(s + _tile_lanes(madd + causal, hpg)
                              + _near_bias(dt_ref, group_heads[g], qi, c * (KW // LANE), KW // LANE))
            for g in range(ng):
                m_ref, l_ref, acc_ref = state[g][2:5]
                if bounded:
                    _flash_accumulate(scores[g], vst_ref[g, :, pl.ds(c0, KW)], l_ref, acc_ref)
                else:
                    _flash_update(scores[g], vst_ref[g, :, pl.ds(c0, KW)], m_ref, l_ref, acc_ref)
            return x

        _pipelined_chunks(c_near, qk_stage, soft_stage)
        lax.fori_loop(c_near, qi // (KW // LANE) + 1, near_step, 0)
        for g in range(ng):
            m_ref, l_ref, acc_ref = state[g][2:5]
            acc_ref[...] = _sum_result(l_ref, acc_ref) if bounded else _flash_result(m_ref, l_ref, acc_ref)

    pl.when(bounded_ok)(lambda: attend(True))
    pl.when(jnp.logical_not(bounded_ok))(lambda: attend(False))

    for g in range(ng):
        os_t = state[g][4][...]
        gates = jax.nn.sigmoid(gt_ref[g])
        for h in range(hpg):
            sl = slice(h * QB, (h + 1) * QB)
            o_t = (gates[3 * h:3 * h + 1] * oc_t[g][:, sl] + gates[3 * h + 1:3 * h + 2] * os_t[:, sl]
                   + gates[3 * h + 2:3 * h + 3] * ow_t[g][:, sl])
            hh = group_heads[g][h]
            o_ref[:, hh * HEAD_DIM:(hh + 1) * HEAD_DIM] = o_t.T.astype(o_ref.dtype)


def nsa_attention(logit_bound, q_t, gates_t, kc, vc_t, k_sw, v_sw_t, dt, dc, bsz, seq):
    nq = seq // QB
    ncp = seq // CMP_STRIDE
    nc = ncp - 1
    ns = seq // SEL_BLOCK
    assert ns <= LANE and seq >= WINDOW + QB and seq % KW == 0
    assert CMP_BLOCK == 2 * CMP_STRIDE and SEL_BLOCK == 4 * CMP_STRIDE
    expand =((np.arange(seq)[:, None] // SEL_BLOCK) == np.arange(LANE)[None, :]).astype(np.float32)
    ng = NSA_GROUPS
    once = dict(pipeline_mode=pl.Buffered(1))
    ks_spec = pl.BlockSpec((ng, seq, HEAD_DIM), lambda b, i: (0, b, 0), **once)
    kw_spec = pl.BlockSpec((ng, seq, HEAD_DIM), lambda b, i: (1, b, 0), **once)
    vs_spec = pl.BlockSpec((ng, HEAD_DIM, seq), lambda b, i: (0, 0, b), **once)
    vw_spec = pl.BlockSpec((ng, HEAD_DIM, seq), lambda b, i: (1, 0, b), **once)
    lanes = NSA_HPG * QB
    group_state = [pltpu.VMEM((ncp + CWIN, lanes), F32), pltpu.VMEM((ncp + SUBLANE, QB), F32),
                   pltpu.VMEM((1, lanes), F32), pltpu.VMEM((1, lanes), F32), pltpu.VMEM((HEAD_DIM, lanes), F32),
                   pltpu.VMEM((KW, lanes), F32), pltpu.VMEM((KW, lanes), F32),
                   pltpu.VMEM((1, lanes), F32), pltpu.VMEM((1, lanes), F32)]
    assert len(group_state) == NSA_STATE
    return pl.pallas_call(
        functools.partial(_nsa_kernel, seq=seq, nc=nc),
        grid=(bsz, nq),
        in_specs=[pl.BlockSpec(memory_space=pltpu.SMEM),
                  pl.BlockSpec((NSA_HEADS, HEAD_DIM, QB), lambda b, i: (0, 0, b * nq + i)),
                  pl.BlockSpec((ng, GATE_ROWS, QB), lambda b, i: (0, 0, b * nq + i)),
                  pl.BlockSpec((ng, ncp, HEAD_DIM), lambda b, i: (0, b, 0)),
                  pl.BlockSpec((ng, HEAD_DIM, ncp), lambda b, i: (0, 0, b)),
                  ks_spec, vs_spec, kw_spec, vw_spec,
                  pl.BlockSpec((NSA_HEADS, 3, LANE, LANE), lambda b, i: (0, 0, 0, 0)),
                  pl.BlockSpec((NSA_HEADS, CWIN, LANE), lambda b, i: (0, 0, 0)),
                  pl.BlockSpec((seq, LANE), lambda b, i: (0, 0))],
        out_specs=pl.BlockSpec((QB, NSA_HEADS * HEAD_DIM), lambda b, i: (b * nq + i, 0)),
        out_shape=jax.ShapeDtypeStruct((bsz * seq, NSA_HEADS * HEAD_DIM), BF16),
        scratch_shapes=group_state * ng,
        compiler_params=_cparams(("arbitrary", "arbitrary")),
        name="nsa_attention",
    )(jnp.reshape(logit_bound, (1,)).astype(F32), q_t, gates_t, kc, vc_t, k_sw, v_sw_t, k_sw, v_sw_t, dt, dc,
      jnp.asarray(expand, BF16))


MLA_HPS = 2


def _mla_kernel(bound_ref, qt_ref, k_ref, vt_ref, o_ref, *scratch):
    qi = pl.program_id(2)
    chains = [scratch[3 * h:3 * h + 3] for h in range(MLA_HPS)]
    sbuf = [scratch[(3 + b) * MLA_HPS:(4 + b) * MLA_HPS] for b in range(2)]
    cbuf = [scratch[(5 + b) * MLA_HPS:(6 + b) * MLA_HPS] for b in range(2)]
    c_diag = pl.multiple_of(qi * KW, KW)
    kpos = lax.broadcasted_iota(jnp.int32, (KW, KW), 0)
    tq = lax.broadcasted_iota(jnp.int32, (KW, KW), 1)
    dv = vt_ref.shape[1]

    def attend(bounded):
        for ch in chains:
            _flash_init(*ch)

        def qk_stage(c, buf):
            c0 = pl.multiple_of(c * KW, KW)
            ahead = []
            for h in range(MLA_HPS):
                s = _dot(k_ref[h, pl.ds(c0, KW), :], qt_ref[h])
                sbuf[buf][h][...] = s
                if bounded:
                    ahead.append(s[KW - 8:KW])
                else:
                    ahead.append(jnp.max(s, axis=0, keepdims=True))
                    cbuf[buf][h][...] = ahead[-1]
            return ahead

        def soft_stage(c, buf, ahead):
            c0 = pl.multiple_of(c * KW, KW)
            for h, (m_ref, l_ref, acc_ref) in enumerate(chains):
                after = None if ahead is None else ahead[h]
                if bounded:
                    _flash_accumulate(sbuf[buf][h][...], vt_ref[h, :, pl.ds(c0, KW)], l_ref, acc_ref, after=after)
                else:
                    _flash_update(sbuf[buf][h][...], vt_ref[h, :, pl.ds(c0, KW)], m_ref, l_ref, acc_ref,
                                  col_max=cbuf[buf][h][...], after=after)

        _pipelined_chunks(qi, qk_stage, soft_stage)
        causal = jnp.where(kpos <= tq, 0.0, NEG)
        scores = [_dot(k_ref[h, pl.ds(c_diag, KW), :], qt_ref[h]) + causal for h in range(MLA_HPS)]
        for h, (m_ref, l_ref, acc_ref) in enumerate(chains):
            if bounded:
                _flash_accumulate(scores[h], vt_ref[h, :, pl.ds(c_diag, KW)], l_ref, acc_ref)
                o_t = _sum_result(l_ref, acc_ref)
            else:
                _flash_update(scores[h], vt_ref[h, :, pl.ds(c_diag, KW)], m_ref, l_ref, acc_ref)
                o_t = _flash_result(m_ref, l_ref, acc_ref)
            o_ref[:, h * dv:(h + 1) * dv] = o_t.T.astype(o_ref.dtype)

    bounded_ok = bound_ref[0] <= SAFE_LOG2_BOUND
    pl.when(bounded_ok)(lambda: attend(True))
    pl.when(jnp.logical_not(bounded_ok))(lambda: attend(False))


def mla_attention(logit_bound, q_t, k, v_t, bsz, seq):
    nh, dqk, _ = q_t.shape
    dv = v_t.shape[1]
    nq = seq // KW
    hps = MLA_HPS
    state = [pltpu.VMEM((1, KW), F32), pltpu.VMEM((1, KW), F32), pltpu.VMEM((dv, KW), F32)] * hps
    state += [pltpu.VMEM((KW, KW), F32)] * (2 * hps)
    state += [pltpu.VMEM((1, KW), F32)] * (2 * hps)
    return pl.pallas_call(
        _mla_kernel,
        grid=(bsz, nh // hps, nq),
        in_specs=[pl.BlockSpec(memory_space=pltpu.SMEM),
                  pl.BlockSpec((hps, dqk, KW), lambda b, h, i: (h, 0, b * nq + i)),
                  pl.BlockSpec((hps, seq, dqk), lambda b, h, i: (h, b, 0)),
                  pl.BlockSpec((hps, dv, seq), lambda b, h, i: (h, 0, b))],
        out_specs=pl.BlockSpec((KW, hps * dv), lambda b, h, i: (b * nq + i, h)),
        out_shape=jax.ShapeDtypeStruct((bsz * seq, nh * dv), BF16),
        scratch_shapes=state,
        compiler_params=_cparams(("arbitrary", "arbitrary", "arbitrary")),
        name="mla_attention",
    )(jnp.reshape(logit_bound, (1,)).astype(F32), q_t, k, v_t)


INT_MIN = -2 ** 31
NEG_KEY = int(np.array(NEG, np.float32).view(np.int32)) ^ 0x7FFFFFFF
KEY_BITS = 32
SURE_BITS = 22


def _sort_key(x):
    bits = pltpu.bitcast(x + 0.0, jnp.int32)
    return jnp.where(bits < 0, bits ^ 0x7FFFFFFF, bits)


def _dsa_kernel(bound_ref, iqt_ref, iwt_ref, ik_ref, qt_ref, k_ref, vt_ref, dt_ref, o_ref,
                key_ref, *state, seq, k_sel):
    qi = pl.program_id(1)
    q0 = qi * QB
    n_chunk = (q0 + QB + KW - 1) // KW
    n_rest = seq - n_chunk * KW
    kpos = lax.broadcasted_iota(jnp.int32, (KW, QB), 0)
    tq = q0 + lax.broadcasted_iota(jnp.int32, (KW, QB), 1)
    hpp = KW // QB

    def score_chunk(c):
        c0 = pl.multiple_of(c * KW, KW)
        ikc = ik_ref[pl.ds(c0, KW), :]
        acc = jnp.zeros((KW, QB), F32)
        for piece in range(IDX_HEADS // hpp):
            sl = slice(piece * KW, (piece + 1) * KW)
            s = jnp.maximum(_dot(ikc, iqt_ref[0, :, sl]), 0.0) * iwt_ref[0, :, sl]
            for j in range(hpp):
                acc = acc + s[:, j * QB:(j + 1) * QB]
        acc = jnp.where(c0 + kpos <= tq, acc, NEG)
        key_ref[pl.ds(c0, KW), :] = _sort_key(acc)

    def score_pair(j, x):
        score_chunk(2 * j)
        score_chunk(jnp.minimum(2 * j + 1, n_chunk - 1))
        return x

    lax.fori_loop(0, (n_chunk + 1) // 2, score_pair, 0)

    def count(pred):
        def chunk_hits(c):
            c0 = pl.multiple_of(c * KW, KW)
            hit = jnp.where(pred(key_ref[pl.ds(c0, KW), :], c0), 1.0, 0.0)
            parts = [hit[SUBLANE * i:SUBLANE * (i + 1)] for i in range(KW // SUBLANE)]
            while len(parts) > 1:
                parts = [parts[i] + parts[i + 1] for i in range(0, len(parts), 2)]
            return parts[0]

        def body(j, acc):
            second = 2 * j + 1
            weight = jnp.where(second < n_chunk, 1.0, 0.0)
            return acc + chunk_hits(2 * j) + chunk_hits(jnp.minimum(second, n_chunk - 1)) * weight

        acc = lax.fori_loop(0, (n_chunk + 1) // 2, body, jnp.zeros((SUBLANE, QB), F32))
        return jnp.sum(acc, axis=0, keepdims=True)

    rest = n_rest.astype(F32)
    kf = float(k_sel)

    def bit_step(i, st):
        u, thr_s, settled = st
        bit = jnp.left_shift(jnp.int32(1), KEY_BITS - 1 - i)
        trial = (u | bit) ^ INT_MIN
        cnt = count(lambda keys, c0: keys >= trial) + jnp.where(NEG_KEY >= trial, rest, 0.0)
        new = (cnt == kf) & (settled < 0.5)
        return (jnp.where(cnt >= kf, u | bit, u), jnp.where(new, trial, thr_s), jnp.where(new, 1.0, settled))

    st = (jnp.zeros((1, QB), jnp.int32), jnp.zeros((1, QB), jnp.int32), jnp.zeros((1, QB), F32))
    st = lax.fori_loop(0, SURE_BITS, bit_step, st)
    _, (u, thr_s, settled) = lax.while_loop(
        lambda c: (c[0] < KEY_BITS) & (jnp.min(c[1][2]) < 0.5),
        lambda c: (c[0] + 1, bit_step(c[0], c[1])), (jnp.int32(SURE_BITS), st))
    is_settled = settled > 0.5
    thr = jnp.where(is_settled, thr_s, u ^ INT_MIN)

    def edge_counts():
        return (count(lambda keys, c0: keys > thr) + jnp.where(NEG_KEY > thr, rest, 0.0),
                count(lambda keys, c0: keys >= thr) + jnp.where(NEG_KEY >= thr, rest, 0.0))

    zero_cnt = jnp.zeros((1, QB), F32)
    cnt_gt, cnt_ge = lax.cond(jnp.min(settled) > 0.5, lambda: (zero_cnt, zero_cnt), edge_counts)
    need = kf - cnt_gt
    tie_q = (cnt_ge > kf) & (thr != NEG_KEY) & jnp.logical_not(is_settled)
    idx_bits = (seq - 1).bit_length()
    no_cut = 2 ** 30

    def tie_cut():
        def idx_step(i, x):
            bit = jnp.left_shift(jnp.int32(1), idx_bits - 1 - i)
            trial = x | bit
            f = count(lambda keys, c0: (keys == thr) & (c0 + kpos < trial))
            return jnp.where(f <= need - 1.0, trial, x)
        return lax.fori_loop(0, idx_bits, idx_step, jnp.zeros((1, QB), jnp.int32))

    any_tie = jnp.max(jnp.where(tie_q, 1.0, 0.0)) > 0.0
    x_cut = lax.cond(any_tie, tie_cut, lambda: jnp.full((1, QB), no_cut, jnp.int32))
    x_cut = jnp.where(tie_q, x_cut, no_cut)

    bounded_ok = bound_ref[0] <= SAFE_LOG2_BOUND
    shift = jnp.where(bounded_ok, bound_ref[0], 0.0)

    def mask_add(c0):
        keys = key_ref[pl.ds(c0, KW), :]
        pos = c0 + kpos
        chosen = (keys > thr) | ((keys == thr) & (pos <= x_cut))
        return _tile_lanes(jnp.where(chosen & (pos <= tq), -shift, NEG), DSA_HPG)

    c_near = jnp.maximum(qi - 1, 0) // (KW // LANE)
    ng = DSA_KV_HEADS
    chains = [state[3 * g:3 * g + 3] for g in range(ng)]
    sbuf = [state[(3 + b) * ng:(4 + b) * ng] for b in range(2)]
    cbuf = [state[(5 + b) * ng:(6 + b) * ng] for b in range(2)]
    group_heads = [[g * DSA_HPG + h for h in range(DSA_HPG)] for g in range(ng)]

    def raw_scores(c0, g):
        q_t = jnp.concatenate([qt_ref[h] for h in group_heads[g]], axis=1)
        return _dot(k_ref[g, pl.ds(c0, KW), :], q_t)

    def attend(bounded):
        for ch in chains:
            _flash_init(*ch)

        def qk_stage(c, buf):
            c0 = pl.multiple_of(c * KW, KW)
            madd = mask_add(c0)
            ahead = []
            for g in range(ng):
                s = raw_scores(c0, g) + madd
                sbuf[buf][g][...] = s
                if bounded:
                    ahead.append(s[KW - 8:KW])
                else:
                    ahead.append(jnp.max(s, axis=0, keepdims=True))
                    cbuf[buf][g][...] = ahead[-1]
            return ahead

        def soft_stage(c, buf, ahead):
            c0 = pl.multiple_of(c * KW, KW)
            for g, (m_ref, l_ref, acc_ref) in enumerate(chains):
                after = None if ahead is None else ahead[g]
                if bounded:
                    _flash_accumulate(sbuf[buf][g][...], vt_ref[g, :, pl.ds(c0, KW)], l_ref, acc_ref, after=after)
                else:
                    _flash_update(sbuf[buf][g][...], vt_ref[g, :, pl.ds(c0, KW)], m_ref, l_ref, acc_ref,
                                  col_max=cbuf[buf][g][...], after=after)

        _pipelined_chunks(c_near, qk_stage, soft_stage)

        def near_step(c, x):
            c0 = pl.multiple_of(c * KW, KW)
            madd = mask_add(c0)
            scores = [raw_scores(c0, g) + madd
                      + _near_bias(dt_ref, group_heads[g], qi, c * (KW // LANE), KW // LANE)
                      for g in range(ng)]
            for g, (m_ref, l_ref, acc_ref) in enumerate(chains):
                if bounded:
                    _flash_accumulate(scores[g], vt_ref[g, :, pl.ds(c0, KW)], l_ref, acc_ref)
                else:
                    _flash_update(scores[g], vt_ref[g, :, pl.ds(c0, KW)], m_ref, l_ref, acc_ref)
            return x

        lax.fori_loop(c_near, n_chunk, near_step, 0)
        for g, (m_ref, l_ref, acc_ref) in enumerate(chains):
            o_t = _sum_result(l_ref, acc_ref) if bounded else _flash_result(m_ref, l_ref, acc_ref)
            for h in range(DSA_HPG):
                hh = group_heads[g][h]
                o_ref[:, hh * HEAD_DIM:(hh + 1) * HEAD_DIM] = o_t[:, h * QB:(h + 1) * QB].T.astype(o_ref.dtype)

    pl.when(bounded_ok)(lambda: attend(True))
    pl.when(jnp.logical_not(bounded_ok))(lambda: attend(False))


def _idx_prep_kernel(p_ref, c_ref, sa_ref, sb_ref, iqt_ref, ik_ref, iwt_ref, *, ntile):
    nslab_q = IDX_HEADS * IDX_DIM // LANE
    per = LANE // IDX_DIM
    half = IDX_ROPE // 2
    zrows = jnp.zeros((LANE - IDX_DIM, QB), F32)

    def rope_slab(x, c, sa, sb):
        return x * c + pltpu.roll(x, LANE - half, axis=1) * sa + pltpu.roll(x, half, axis=1) * sb

    for t in range(ntile):
        rows = slice(t * QB, (t + 1) * QB)
        c, sa, sb = c_ref[rows, :], sa_ref[rows, :], sb_ref[rows, :]
        cols = []
        for s in range(nslab_q):
            x_t = (rope_slab(p_ref[s, rows, :], c, sa, sb) * IDX_DIM ** -0.5).T
            for j in range(per):
                cols.append(jnp.concatenate([x_t[j * IDX_DIM:(j + 1) * IDX_DIM], zrows], axis=0))
        iqt_ref[t] = jnp.concatenate(cols, axis=1).astype(iqt_ref.dtype)
        tail = p_ref[nslab_q, rows, :]
        lane = lax.broadcasted_iota(jnp.int32, (QB, LANE), 1)
        ik_ref[rows, :] = jnp.where(lane < IDX_DIM, rope_slab(tail, c, sa, sb), 0.0).astype(ik_ref.dtype)
        w_t = (tail * IDX_HEADS ** -0.5).T
        iwt_ref[t] = jnp.concatenate([w_t[IDX_DIM + h:IDX_DIM + h + 1, :] for h in range(IDX_HEADS)], axis=1)


def indexer_operands(proj, seq, tm=512):
    _, m, _ = proj.shape
    ntile = tm // QB
    tps = seq // tm
    cos, sin = _rope_tables(seq, IDX_ROPE)
    zero = jnp.zeros_like(sin)
    rest = IDX_DIM - IDX_ROPE
    per = LANE // IDX_DIM
    c_tab = jnp.tile(jnp.concatenate([cos, cos, jnp.ones((seq, rest), F32)], axis=1), (1, per))
    sa_tab = jnp.tile(jnp.concatenate([-sin, zero, jnp.zeros((seq, rest), F32)], axis=1), (1, per))
    sb_tab = jnp.tile(jnp.concatenate([zero, sin, jnp.zeros((seq, rest), F32)], axis=1), (1, per))
    lanes = IDX_HEADS * QB
    tab_spec = pl.BlockSpec((tm, LANE), lambda i: (i % tps, 0))
    return pl.pallas_call(
        functools.partial(_idx_prep_kernel, ntile=ntile),
        grid=(m // tm,),
        in_specs=[pl.BlockSpec((proj.shape[0], tm, LANE), lambda i: (0, i, 0)), tab_spec, tab_spec, tab_spec],
        out_specs=[pl.BlockSpec((ntile, LANE, lanes), lambda i: (i, 0, 0)),
                   pl.BlockSpec((tm, LANE), lambda i: (i, 0)),
                   pl.BlockSpec((ntile, 1, lanes), lambda i: (i, 0, 0))],
        out_shape=[jax.ShapeDtypeStruct((m // QB, LANE, lanes), BF16),
                   jax.ShapeDtypeStruct((m, LANE), BF16),
                   jax.ShapeDtypeStruct((m // QB, 1, lanes), F32)],
        compiler_params=_cparams(("arbitrary",)),
        name="dsa_indexer_operands",
    )(proj, c_tab, sa_tab, sb_tab)


def dsa_attention(logit_bound, iq_t, iw_t, ik, q_t, k, v_t, dt, bsz, seq):
    nq = seq // QB
    k_sel = min(DSA_TOPK_MAX, seq // 4)
    assert seq % KW == 0
    lanes = DSA_HPG * QB
    return pl.pallas_call(
        functools.partial(_dsa_kernel, seq=seq, k_sel=k_sel),
        grid=(bsz, nq),
        in_specs=[pl.BlockSpec(memory_space=pltpu.SMEM),
                  pl.BlockSpec((1, LANE, IDX_HEADS * QB), lambda b, i: (b * nq + i, 0, 0)),
                  pl.BlockSpec((1, 1, IDX_HEADS * QB), lambda b, i: (b * nq + i, 0, 0)),
                  pl.BlockSpec((seq, LANE), lambda b, i: (b, 0)),
                  pl.BlockSpec((DSA_HEADS, HEAD_DIM, QB), lambda b, i: (0, 0, b * nq + i)),
                  pl.BlockSpec((DSA_KV_HEADS, seq, HEAD_DIM), lambda b, i: (0, b, 0),
                               pipeline_mode=pl.Buffered(1)),
                  pl.BlockSpec((DSA_KV_HEADS, HEAD_DIM, seq), lambda b, i: (0, 0, b),
                               pipeline_mode=pl.Buffered(1)),
                  pl.BlockSpec((DSA_HEADS, 3, LANE, LANE), lambda b, i: (0, 0, 0, 0),
                               pipeline_mode=pl.Buffered(1))],
        out_specs=pl.BlockSpec((QB, DSA_HEADS * HEAD_DIM), lambda b, i: (b * nq + i, 0)),
        out_shape=jax.ShapeDtypeStruct((bsz * seq, DSA_HEADS * HEAD_DIM), BF16),
        scratch_shapes=[pltpu.VMEM((seq, QB), jnp.int32)]
        + [pltpu.VMEM((1, lanes), F32), pltpu.VMEM((1, lanes), F32),
           pltpu.VMEM((HEAD_DIM, lanes), F32)] * DSA_KV_HEADS
        + [pltpu.VMEM((KW, lanes), F32)] * (2 * DSA_KV_HEADS)
        + [pltpu.VMEM((1, lanes), F32)] * (2 * DSA_KV_HEADS),
        compiler_params=_cparams(("arbitrary", "arbitrary")),
        name="dsa_attention",
    )(jnp.reshape(logit_bound, (1,)).astype(F32), iq_t, iw_t, ik, q_t, k, v_t, dt)


def _rope_tables(seq, dim):
    half = dim // 2
    inv = ROPE_THETA ** (-jnp.arange(half, dtype=F32) / half)
    ang = jnp.arange(seq, dtype=F32)[:, None] * inv[None, :]
    return jnp.cos(ang), jnp.sin(ang)


def _logit_bound(gq, gk, dim, scale):
    return dim * scale * jnp.max(jnp.abs(gq)) * jnp.max(jnp.abs(gk)) * (1.0 + 2.0 ** -7)


def _pad_cols(w, n):
    return jnp.pad(w, ((0, 0), (0, n - w.shape[1])))


def _t(x):
    return jnp.swapaxes(x, -1, -2)


def _even_mixer(h, x2, gate, next_norm, dt, dc, bias_bound, bsz, seq, w_in, w_out, nsa_qk_g, cmp_pe, cmp_w1, cmp_b1,
                cmp_w2, cmp_b2, q_norm_g, kv_norm_g, w_uq, w_ukv, nope_g, rope_g):
    m = bsz * seq
    nq_cols = NSA_HEADS * HEAD_DIM
    nkv_cols = 6 * NSA_GROUPS * HEAD_DIM
    ngate = 3 * NSA_HEADS
    o_gate = nq_cols + nkv_cols
    o_cq = o_gate + ngate
    o_ckv = o_cq + MLA_Q_RANK
    o_kpe = o_ckv + MLA_KV_RANK
    gw = NSA_GROUPS * HEAD_DIM
    kvw = [w_in[:, nq_cols + i * gw:nq_cols + (i + 1) * gw] for i in range(6)]
    scale = HEAD_DIM ** -0.5 * LOG2E
    tail = jnp.concatenate([w_in[:, o_kpe:], w_in[:, o_gate:o_cq]], axis=1)
    w_r = jnp.concatenate([kvw[0], kvw[1], w_in[:, o_cq:o_kpe], _pad_cols(tail, LANE)], axis=1).astype(BF16)
    q_t, k_sw, v_sw_t, proj = proj_qkv_raw(
        h, w_in[:, :nq_cols].astype(BF16), nsa_qk_g[0] * scale,
        jnp.concatenate([kvw[2], kvw[4]], axis=1).astype(BF16), nsa_qk_g[1],
        jnp.concatenate([kvw[3], kvw[5]], axis=1).astype(BF16), w_r)
    s_cq = 2 * NSA_GROUPS
    s_ckv = s_cq + MLA_Q_RANK // LANE
    s_tail = s_ckv + MLA_KV_RANK // LANE
    kvc = compress_kv(proj, 0, bsz, seq, cmp_pe, cmp_w1, cmp_b1, cmp_w2, cmp_b2, nsa_qk_g[1])
    tail_v = proj[s_tail]
    gates = tail_v[:, MLA_ROPE:MLA_ROPE + ngate].reshape(m, NSA_GROUPS, 3 * NSA_HPG)
    gates_t = jnp.pad(jnp.transpose(gates, (1, 2, 0)), ((0, 0), (0, GATE_ROWS - 3 * NSA_HPG), (0, 0)))
    nsa_bound = _logit_bound(nsa_qk_g[0], nsa_qk_g[1], HEAD_DIM, scale) + bias_bound
    o_nsa = nsa_attention(nsa_bound, q_t, gates_t, kvc[0], _t(kvc[1]), k_sw, v_sw_t,
                          dt[:NSA_HEADS], dc[:NSA_HEADS], bsz, seq)

    dq = MLA_NOPE + MLA_ROPE
    wq = w_uq.reshape(MLA_Q_RANK, MLA_HEADS, dq)
    wq_r = jnp.concatenate([wq[:, :, :MLA_NOPE].reshape(MLA_Q_RANK, -1),
                            wq[:, :, MLA_NOPE:].reshape(MLA_Q_RANK, -1)], axis=1).astype(BF16)
    cos, sin = _rope_tables(seq, MLA_ROPE)
    mscale = dq ** -0.5 * LOG2E
    side = [jnp.sqrt(MLA_NOPE * jnp.max(jnp.abs(nope_g[i])) ** 2 + MLA_ROPE * jnp.max(jnp.abs(rope_g[i])) ** 2)
            for i in range(2)]
    mla_bound = mscale * side[0] * side[1] * (1.0 + 2.0 ** -7)
    mla_shift = jnp.where(mla_bound <= SAFE_LOG2_BOUND, mla_bound, 0.0)
    q_mla_t, k_mla, v_mla_t = mla_project(proj, s_cq, s_ckv, s_tail, seq, q_norm_g, kv_norm_g, wq_r,
                                          w_ukv.astype(BF16), nope_g, rope_g, cos, sin, mscale, mla_shift)
    o_mla = mla_attention(mla_bound, q_mla_t, k_mla, v_mla_t, bsz, seq)
    w_o = w_out.astype(BF16)
    return resproj([(o_nsa, w_o[:nq_cols]), (o_mla, w_o[nq_cols:])], x2, gate, seq, next_norm)


def _odd_mixer(h, x2, gate, next_norm, dt, bias_bound, bsz, seq, w_in, w_out, qk_g):
    nq = DSA_HEADS * HEAD_DIM
    nkv = DSA_KV_HEADS * HEAD_DIM
    niq = IDX_HEADS * IDX_DIM
    o_k, o_v, o_iq = nq, nq + nkv, nq + 2 * nkv
    w_idx = w_in[:, o_iq:]
    q_t, k, v_t, proj = proj_qkv_raw(
        h, w_in[:, :o_k].astype(BF16), qk_g[0] * (HEAD_DIM ** -0.5 * LOG2E), w_in[:, o_k:o_v].astype(BF16), qk_g[1],
        w_in[:, o_v:o_iq].astype(BF16), _pad_cols(w_idx, niq + LANE).astype(BF16))
    iq_t, ik, iw_t = indexer_operands(proj, seq)
    bound = _logit_bound(qk_g[0], qk_g[1], HEAD_DIM, HEAD_DIM ** -0.5 * LOG2E) + bias_bound
    o = dsa_attention(bound, iq_t, iw_t, ik, q_t, k, v_t, dt, bsz, seq)
    return resproj([(o, w_out.astype(BF16))], x2, gate, seq, next_norm)


def _conv_ffn(h, x2, gate, next_norm, seq, w_up_all, layer, conv_w, conv_b, w_down):
    a = ffn_up(h, w_up_all, layer, conv_w, conv_b, seq)
    return resproj([(a, w_down.astype(BF16))], x2, gate, seq, next_norm)


def kernel(x, c, rel_bias, ada_w, ada_b, norm_g, ev_w_in, ev_w_out, nsa_qk_g, cmp_pe, cmp_w1, cmp_b1, cmp_w2, cmp_b2, mla_q_norm_g, mla_kv_norm_g, mla_w_uq, mla_w_ukv, mla_nope_g, mla_rope_g, od_w_in, od_w_out, dsa_qk_g, ffn_w_up, ffn_conv_w, ffn_conv_b, ffn_w_down):
    bsz, seq, d = x.shape
    depth = ada_w.shape[0]
    x2 = x.reshape(bsz * seq, d)
    mods = ada_all(c, ada_w, ada_b)
    dt, dc = bias_tiles(rel_bias)
    bias_bound = 2.0 * LOG2E * jnp.max(jnp.abs(rel_bias))
    def norm_of(i, sub):
        if i >= depth:
            return None
        shift, scale, _ = jnp.split(mods[i, sub], 3, axis=-1)
        return norm_g[i, sub], scale, shift

    g0, scale0, shift0 = norm_of(0, 0)
    h = modnorm(x2, g0, scale0, shift0, seq)
    for i in range(depth):
        j = i // 2
        gate = jnp.split(mods[i, 0], 3, axis=-1)[2]
        if i % 2 == 0:
            x2, h = _even_mixer(h, x2, gate, norm_of(i, 1), dt, dc, bias_bound, bsz, seq, ev_w_in[j],
                                ev_w_out[j], nsa_qk_g[j], cmp_pe[j], cmp_w1[j], cmp_b1[j], cmp_w2[j],
                                cmp_b2[j], mla_q_norm_g[j], mla_kv_norm_g[j], mla_w_uq[j], mla_w_ukv[j],
                                mla_nope_g[j], mla_rope_g[j])
        else:
            x2, h = _odd_mixer(h, x2, gate, norm_of(i, 1), dt, bias_bound, bsz, seq, od_w_in[j], od_w_out[j],
                               dsa_qk_g[j])
        gate = jnp.split(mods[i, 1], 3, axis=-1)[2]
        x2, h = _conv_ffn(h, x2, gate, norm_of(i + 1, 0), seq, ffn_w_up, i, ffn_conv_w[i], ffn_conv_b[i],
                          ffn_w_down[i])
    return x2.reshape(bsz, seq, d)
```

```python
import functools
import math

import numpy as np
import jax
import jax.numpy as jnp
from jax import lax
from jax.experimental import pallas as pl
from jax.experimental.pallas import tpu as pltpu

HEAD_DIM = 128
NSA_HEADS = 8
NSA_GROUPS = 2
NSA_HPG = NSA_HEADS // NSA_GROUPS
CMP_BLOCK = 32
CMP_STRIDE = 16
CMP_HIDDEN = 256
SEL_BLOCK = 64
SEL_TOP_N = 16
WINDOW = 512
MLA_HEADS = 8
MLA_Q_RANK = 512
MLA_KV_RANK = 256
MLA_NOPE = 128
MLA_ROPE = 64
MLA_V = 128
DSA_HEADS = 16
DSA_KV_HEADS = 4
DSA_HPG = DSA_HEADS // DSA_KV_HEADS
IDX_HEADS = 16
IDX_DIM = 64
IDX_ROPE = 32
DSA_TOPK_MAX = 256
REL_BUCKETS = 32
REL_MAX_DIST = 128
CONV_WIDTH = 3
ROPE_THETA = 10000.0
EPS = 1e-6
NEG = -1e30
FORCE = 1e9

LANE = 128
SUBLANE = 8
QB = 128
VMEM_LIMIT = 56 * 1024 * 1024

F32 = jnp.float32
BF16 = jnp.bfloat16


def _t5_thresholds():
    d = np.arange(0, 4 * REL_MAX_DIST)
    half = REL_BUCKETS // 2
    val = np.log(np.maximum(d, 1) / half) / math.log(REL_MAX_DIST / half) * (REL_BUCKETS - half)
    large = np.minimum(half + np.floor(np.maximum(val, 0.0)).astype(np.int64), REL_BUCKETS - 1)
    bucket = np.where(d < half, d, large)
    return [int(np.argmax(bucket >= b)) for b in range(1, REL_BUCKETS)]


T5_THR = _t5_thresholds()
T5_FAR = T5_THR[-1]
assert T5_FAR <= LANE


def _cparams(sem):
    return pltpu.CompilerParams(dimension_semantics=sem, vmem_limit_bytes=VMEM_LIMIT)


def _dot(a, b):
    return jnp.dot(a, b, preferred_element_type=F32)


def _ada_kernel(ct_ref, w_ref, b_ref, o_ref, *, bsz):
    ct = ct_ref[...]
    a = ct * jax.nn.sigmoid(ct)
    w = w_ref[0]
    rows = [jnp.sum(a[:, b:b + 1] * w, axis=0, keepdims=True) for b in range(bsz)]
    rows.append(jnp.zeros((o_ref.shape[1] - bsz, w.shape[1]), F32))
    o_ref[0] = jnp.concatenate(rows, axis=0) + b_ref[0]


def ada_all(c, ada_w, ada_b):
    depth, two, d, n3 = ada_w.shape
    bsz = c.shape[0]
    rows = -(-bsz // SUBLANE) * SUBLANE
    assert bsz <= LANE
    ct = jnp.zeros((d, LANE), F32).at[:, :bsz].set(c.T)
    w = ada_w.reshape(depth * two, d, n3)
    b = ada_b.reshape(depth * two, 1, n3)
    tn = 512
    out = pl.pallas_call(
        functools.partial(_ada_kernel, bsz=bsz),
        grid=(depth * two, n3 // tn),
        in_specs=[pl.BlockSpec((d, LANE), lambda l, j: (0, 0)),
                  pl.BlockSpec((1, d, tn), lambda l, j: (l, 0, j)),
                  pl.BlockSpec((1, 1, tn), lambda l, j: (l, 0, j))],
        out_specs=pl.BlockSpec((1, rows, tn), lambda l, j: (l, 0, j)),
        out_shape=jax.ShapeDtypeStruct((depth * two, rows, n3), F32),
        compiler_params=_cparams(("arbitrary", "arbitrary")),
        name="ada_mod",
    )(ct, w, b)
    return out[:, :bsz].reshape(depth, two, bsz, n3)


def _modnorm_kernel(x_ref, g_ref, sc_ref, sh_ref, o_ref):
    x = x_ref[...]
    y = x * lax.rsqrt(jnp.mean(x * x, axis=-1, keepdims=True) + EPS)
    h = (y * g_ref[...]) * (1.0 + sc_ref[0]) + sh_ref[0]
    o_ref[...] = h.astype(o_ref.dtype)


def modnorm(x2, g, scale, shift, seq, tm=1024):
    m, d = x2.shape
    tpb = seq // tm
    return pl.pallas_call(
        _modnorm_kernel,
        grid=(m // tm,),
        in_specs=[pl.BlockSpec((tm, d), lambda i: (i, 0)),
                  pl.BlockSpec((1, d), lambda i: (0, 0)),
                  pl.BlockSpec((1, 1, d), lambda i: (i // tpb, 0, 0)),
                  pl.BlockSpec((1, 1, d), lambda i: (i // tpb, 0, 0))],
        out_specs=pl.BlockSpec((tm, d), lambda i: (i, 0)),
        out_shape=jax.ShapeDtypeStruct((m, d), BF16),
        compiler_params=_cparams(("arbitrary",)),
        name="modnorm",
    )(x2, g.reshape(1, d), scale.reshape(-1, 1, d), shift.reshape(-1, 1, d))


def _head_norm(y, g_ref):
    return y * lax.rsqrt(jnp.mean(y * y, axis=-1, keepdims=True) + EPS) * g_ref[...]


def _qkv_heads(x, wq_ref, wk_ref, wv_ref, gq_ref, gk_ref, oq_ref, ok_ref, ov_ref):
    acc = _dot(x, wq_ref[...])
    for s in range(oq_ref.shape[0]):
        oq_ref[s] = _head_norm(acc[:, s * LANE:(s + 1) * LANE], gq_ref).T.astype(oq_ref.dtype)
    acc = _dot(x, wk_ref[...])
    for s in range(ok_ref.shape[0]):
        ok_ref[s] = _head_norm(acc[:, s * LANE:(s + 1) * LANE], gk_ref).astype(ok_ref.dtype)
    acc = _dot(x, wv_ref[...])
    for s in range(ov_ref.shape[0]):
        ov_ref[s] = acc[:, s * LANE:(s + 1) * LANE].T.astype(ov_ref.dtype)


def _proj_qkvr_kernel(x_ref, wq_ref, wk_ref, wv_ref, wr_ref, gq_ref, gk_ref, oq_ref, ok_ref, ov_ref, or_ref):
    x = x_ref[...]
    _qkv_heads(x, wq_ref, wk_ref, wv_ref, gq_ref, gk_ref, oq_ref, ok_ref, ov_ref)
    acc = _dot(x, wr_ref[...])
    for s in range(or_ref.shape[0]):
        or_ref[s] = acc[:, s * LANE:(s + 1) * LANE]


def _proj_qkvi_kernel(x_ref, wq_ref, wk_ref, wv_ref, wi_ref, gq_ref, gk_ref, c_ref, sa_ref, sb_ref,
                      oq_ref, ok_ref, ov_ref, iqt_ref, ik_ref, iwt_ref):
    x = x_ref[...]
    _qkv_heads(x, wq_ref, wk_ref, wv_ref, gq_ref, gk_ref, oq_ref, ok_ref, ov_ref)
    acc = _dot(x, wi_ref[...])
    _indexer_layouts(lambda s, rows: acc[rows, s * LANE:(s + 1) * LANE], c_ref, sa_ref, sb_ref,
                     iqt_ref, ik_ref, iwt_ref)


def _proj_specs(x, ws, tm):
    m, k = x.shape
    whole = lambda w: pl.BlockSpec(w.shape, lambda i: (0, 0), pipeline_mode=pl.Buffered(1))
    gain = pl.BlockSpec((1, LANE), lambda i: (0, 0))
    nq, nk, nv = (w.shape[1] // LANE for w in ws[:3])
    in_specs = [pl.BlockSpec((tm, k), lambda i: (i, 0))] + [whole(w) for w in ws] + [gain, gain]
    out_specs = [pl.BlockSpec((nq, LANE, tm), lambda i: (0, 0, i)),
                 pl.BlockSpec((nk, tm, LANE), lambda i: (0, i, 0)),
                 pl.BlockSpec((nv, LANE, tm), lambda i: (0, 0, i))]
    out_shape = [jax.ShapeDtypeStruct((nq, LANE, m), BF16),
                 jax.ShapeDtypeStruct((nk, m, LANE), BF16),
                 jax.ShapeDtypeStruct((nv, LANE, m), BF16)]
    return in_specs, out_specs, out_shape


def proj_qkv_raw(x, wq, g_q, wk, g_k, wv, wr, tm=512):
    m = x.shape[0]
    nr = wr.shape[1] // LANE
    in_specs, out_specs, out_shape = _proj_specs(x, (wq, wk, wv, wr), tm)
    return pl.pallas_call(
        _proj_qkvr_kernel,
        grid=(m // tm,),
        in_specs=in_specs,
        out_specs=out_specs + [pl.BlockSpec((nr, tm, LANE), lambda i: (0, i, 0))],
        out_shape=out_shape + [jax.ShapeDtypeStruct((nr, m, LANE), F32)],
        compiler_params=_cparams(("arbitrary",)),
        name="proj_qkv_raw",
    )(x, wq, wk, wv, wr, g_q.reshape(1, LANE), g_k.reshape(1, LANE))


def proj_qkv_indexer(x, wq, g_q, wk, g_k, wv, wi, seq, tm=512):
    m = x.shape[0]
    ntile = tm // QB
    tps = seq // tm
    cos, sin = _rope_tables(seq, IDX_ROPE)
    zero = jnp.zeros_like(sin)
    rest = IDX_DIM - IDX_ROPE
    per = LANE // IDX_DIM
    c_tab = jnp.tile(jnp.concatenate([cos, cos, jnp.ones((seq, rest), F32)], axis=1), (1, per))
    sa_tab = jnp.tile(jnp.concatenate([-sin, zero, jnp.zeros((seq, rest), F32)], axis=1), (1, per))
    sb_tab = jnp.tile(jnp.concatenate([zero, sin, jnp.zeros((seq, rest), F32)], axis=1), (1, per))
    lanes = IDX_HEADS * QB
    tab_spec = pl.BlockSpec((tm, LANE), lambda i: (i % tps, 0))
    in_specs, out_specs, out_shape = _proj_specs(x, (wq, wk, wv, wi), tm)
    return pl.pallas_call(
        _proj_qkvi_kernel,
        grid=(m // tm,),
        in_specs=in_specs + [tab_spec, tab_spec, tab_spec],
        out_specs=out_specs + [pl.BlockSpec((ntile, LANE, lanes), lambda i: (i, 0, 0)),
                               pl.BlockSpec((tm, LANE), lambda i: (i, 0)),
                               pl.BlockSpec((ntile, 1, lanes), lambda i: (i, 0, 0))],
        out_shape=out_shape + [jax.ShapeDtypeStruct((m // QB, LANE, lanes), BF16),
                               jax.ShapeDtypeStruct((m, LANE), BF16),
                               jax.ShapeDtypeStruct((m // QB, 1, lanes), F32)],
        compiler_params=_cparams(("arbitrary",)),
        name="proj_qkv_indexer",
    )(x, wq, wk, wv, wi, g_q.reshape(1, LANE), g_k.reshape(1, LANE), c_tab, sa_tab, sb_tab)


def _rms_rows(x, g):
    return x * lax.rsqrt(jnp.mean(x * x, axis=-1, keepdims=True) + EPS) * g


def _rope_rows(x, cos, sin):
    half = x.shape[-1] // 2
    x1, x2 = x[:, :half], x[:, half:]
    return jnp.concatenate([x1 * cos - x2 * sin, x1 * sin + x2 * cos], axis=1)


def _latent(x_ref, g_ref):
    x = jnp.concatenate([x_ref[s] for s in range(x_ref.shape[0])], axis=1)
    return _rms_rows(x, g_ref[...]).astype(BF16)


def _mla_q_kernel(shift_ref, x_ref, g_ref, w_ref, gn_ref, gr2_ref, c_ref, sa_ref, sb_ref, o_ref, *, scale):
    acc = _dot(_latent(x_ref, g_ref), w_ref[...])
    tm = acc.shape[0]
    for h in range(MLA_HEADS):
        nope = _rms_rows(acc[:, h * MLA_NOPE:(h + 1) * MLA_NOPE], gn_ref[...]) * scale
        o_ref[h, 0:MLA_NOPE, :] = nope.T.astype(o_ref.dtype)
    first = lax.broadcasted_iota(jnp.int32, (LANE - MLA_ROPE, tm), 0) == 0
    pad_rows = jnp.where(first, -shift_ref[0], 0.0)
    low = lax.broadcasted_iota(jnp.int32, (tm, LANE), 1) < MLA_ROPE
    c, sa, sb = c_ref[...], sa_ref[...], sb_ref[...]
    half = MLA_ROPE // 2
    per = LANE // MLA_ROPE
    for s in range(MLA_HEADS // per):
        x = acc[:, MLA_HEADS * MLA_NOPE + s * LANE:MLA_HEADS * MLA_NOPE + (s + 1) * LANE]
        sq = x * x
        s_low = jnp.sum(jnp.where(low, sq, 0.0), axis=-1, keepdims=True)
        s_all = jnp.sum(sq, axis=-1, keepdims=True)
        inv = jnp.where(low, lax.rsqrt(s_low / MLA_ROPE + EPS), lax.rsqrt((s_all - s_low) / MLA_ROPE + EPS))
        y = x * inv * gr2_ref[...]
        roped = y * c + pltpu.roll(y, LANE - half, axis=1) * sa + pltpu.roll(y, half, axis=1) * sb
        x_t = (roped * scale).T
        for j in range(per):
            o_ref[per * s + j, MLA_NOPE:MLA_NOPE + LANE, :] = jnp.concatenate(
                [x_t[j * MLA_ROPE:(j + 1) * MLA_ROPE], pad_rows], axis=0).astype(o_ref.dtype)


def _mla_kv_kernel(x_ref, g_ref, w_ref, tail_ref, gn_ref, gr_ref, cos_ref, sin_ref, ok_ref, ov_ref):
    acc = _dot(_latent(x_ref, g_ref), w_ref[...])
    tm = acc.shape[0]
    k_pe = _rope_rows(_rms_rows(tail_ref[0][:, :MLA_ROPE], gr_ref[...]), cos_ref[...], sin_ref[...])
    first = lax.broadcasted_iota(jnp.int32, (tm, LANE - MLA_ROPE), 1) == 0
    k_pe = jnp.concatenate([k_pe, jnp.where(first, 1.0, 0.0)], axis=1).astype(ok_ref.dtype)
    for h in range(MLA_HEADS):
        c0 = h * (MLA_NOPE + MLA_V)
        ok_ref[h, :, 0:MLA_NOPE] = _rms_rows(acc[:, c0:c0 + MLA_NOPE], gn_ref[...]).astype(ok_ref.dtype)
        ok_ref[h, :, MLA_NOPE:MLA_NOPE + LANE] = k_pe
        ov_ref[h] = acc[:, c0 + MLA_NOPE:c0 + MLA_NOPE + MLA_V].T.astype(ov_ref.dtype)


def mla_project(proj, s_cq, s_ckv, s_tail, seq, q_norm_g, kv_norm_g, wq_r, w_ukv, nope_g, rope_g, cos, sin,
                scale, shift, tm=512):
    _, m, _ = proj.shape
    kq, kkv = s_ckv - s_cq, s_tail - s_ckv
    tps = seq // tm
    dqk = MLA_NOPE + LANE
    half = MLA_ROPE // 2
    rope_specs = [pl.BlockSpec((tm, half), lambda i: (i % tps, 0))] * 2
    gain_specs = [pl.BlockSpec((1, MLA_NOPE), lambda i: (0, 0)), pl.BlockSpec((1, MLA_ROPE), lambda i: (0, 0))]
    per = LANE // MLA_ROPE
    zero = jnp.zeros_like(sin)
    c_tab = jnp.tile(jnp.concatenate([cos, cos], axis=1), (1, per))
    sa_tab = jnp.tile(jnp.concatenate([-sin, zero], axis=1), (1, per))
    sb_tab = jnp.tile(jnp.concatenate([zero, sin], axis=1), (1, per))
    tab_spec = pl.BlockSpec((tm, LANE), lambda i: (i % tps, 0))
    q_t = pl.pallas_call(
        functools.partial(_mla_q_kernel, scale=scale),
        grid=(m // tm,),
        in_specs=[pl.BlockSpec(memory_space=pltpu.SMEM),
                  pl.BlockSpec((kq, tm, LANE), lambda i: (s_cq // kq, i, 0)),
                  pl.BlockSpec((1, kq * LANE), lambda i: (0, 0)),
                  pl.BlockSpec(wq_r.shape, lambda i: (0, 0)),
                  pl.BlockSpec((1, MLA_NOPE), lambda i: (0, 0)), pl.BlockSpec((1, LANE), lambda i: (0, 0)),
                  tab_spec, tab_spec, tab_spec],
        out_specs=pl.BlockSpec((MLA_HEADS, dqk, tm), lambda i: (0, 0, i)),
        out_shape=jax.ShapeDtypeStruct((MLA_HEADS, dqk, m), BF16),
        compiler_params=_cparams(("arbitrary",)),
        name="mla_q_project",
    )(jnp.reshape(shift, (1,)).astype(F32), proj, q_norm_g.reshape(1, -1), wq_r, nope_g[0].reshape(1, -1),
      jnp.tile(rope_g[0].reshape(1, -1), (1, per)), c_tab, sa_tab, sb_tab)
    k, v_t = pl.pallas_call(
        _mla_kv_kernel,
        grid=(m // tm,),
        in_specs=[pl.BlockSpec((kkv, tm, LANE), lambda i: (s_ckv // kkv, i, 0)),
                  pl.BlockSpec((1, kkv * LANE), lambda i: (0, 0)),
                  pl.BlockSpec(w_ukv.shape, lambda i: (0, 0)),
                  pl.BlockSpec((1, tm, LANE), lambda i: (s_tail, i, 0))] + gain_specs + rope_specs,
        out_specs=[pl.BlockSpec((MLA_HEADS, tm, dqk), lambda i: (0, i, 0)),
                   pl.BlockSpec((MLA_HEADS, MLA_V, tm), lambda i: (0, 0, i))],
        out_shape=[jax.ShapeDtypeStruct((MLA_HEADS, m, dqk), BF16),
                   jax.ShapeDtypeStruct((MLA_HEADS, MLA_V, m), BF16)],
        compiler_params=_cparams(("arbitrary",)),
        name="mla_kv_project",
    )(proj, kv_norm_g.reshape(1, -1), w_ukv, proj, nope_g[1].reshape(1, -1), rope_g[1].reshape(1, -1), cos, sin)
    return q_t, k, v_t


RES_COLS = 512


def _resproj_kernel(*refs, npair, fuse_norm):
    xres_ref, gate_ref = refs[2 * npair], refs[2 * npair + 1]
    outs = refs[2 * npair + 2 + (3 if fuse_norm else 0):]
    o_ref = outs[0]
    n = o_ref.shape[1]
    for c0 in range(0, n, RES_COLS):
        cols = slice(c0, c0 + RES_COLS)
        acc = _dot(refs[0][...], refs[1][:, cols])
        for p in range(1, npair):
            acc = acc + _dot(refs[2 * p][...], refs[2 * p + 1][:, cols])
        o_ref[:, cols] = xres_ref[:, cols] + gate_ref[0][:, cols] * acc
    if fuse_norm:
        g_ref, sc_ref, sh_ref = refs[2 * npair + 2:2 * npair + 5]
        x = o_ref[...]
        y = x * lax.rsqrt(jnp.mean(x * x, axis=-1, keepdims=True) + EPS)
        outs[1][...] = ((y * g_ref[...]) * (1.0 + sc_ref[0]) + sh_ref[0]).astype(outs[1].dtype)


def resproj(pairs, xres, gate, seq, next_norm=None, tm=512):
    m, n = xres.shape
    tpb = seq // tm
    in_specs, args = [], []
    for x, w in pairs:
        k = x.shape[1]
        in_specs += [pl.BlockSpec((tm, k), lambda i: (i, 0)),
                     pl.BlockSpec((k, n), lambda i: (0, 0), pipeline_mode=pl.Buffered(1))]
        args += [x, w]
    per_batch = pl.BlockSpec((1, 1, n), lambda i: (i // tpb, 0, 0))
    in_specs += [pl.BlockSpec((tm, n), lambda i: (i, 0)), per_batch]
    args += [xres, gate.reshape(-1, 1, n)]
    out_specs = [pl.BlockSpec((tm, n), lambda i: (i, 0))]
    out_shape = [jax.ShapeDtypeStruct((m, n), F32)]
    if next_norm is not None:
        g, scale, shift = next_norm
        in_specs += [pl.BlockSpec((1, n), lambda i: (0, 0)), per_batch, per_batch]
        args += [g.reshape(1, n), scale.reshape(-1, 1, n), shift.reshape(-1, 1, n)]
        out_specs.append(pl.BlockSpec((tm, n), lambda i: (i, 0)))
        out_shape.append(jax.ShapeDtypeStruct((m, n), BF16))
    out = pl.pallas_call(
        functools.partial(_resproj_kernel, npair=len(pairs), fuse_norm=next_norm is not None),
        grid=(m // tm,),
        in_specs=in_specs,
        out_specs=out_specs,
        out_shape=out_shape,
        compiler_params=_cparams(("arbitrary",)),
        name="resproj",
    )(*args)
    return (out[0], out[1]) if next_norm is not None else (out[0], None)


HALO = 8


def _ffn_up_kernel(h_ref, wg32_ref, wv32_ref, cwg_ref, cwv_ref, cbg_ref, cbv_ref, o_ref,
                   ug_ref, uv_ref, wg_ref, wv_ref, *, tm, tiles_per_seq):
    i = pl.program_id(1)
    first = (i % tiles_per_seq) == 0

    @pl.when(i == 0)
    def _():
        wg_ref[...] = wg32_ref[...].astype(wg_ref.dtype)
        wv_ref[...] = wv32_ref[...].astype(wv_ref.dtype)

    @pl.when(first)
    def _():
        ug_ref[0:HALO, :] = jnp.zeros((HALO, ug_ref.shape[1]), F32)
        uv_ref[0:HALO, :] = jnp.zeros((HALO, uv_ref.shape[1]), F32)

    @pl.when(jnp.logical_not(first))
    def _():
        ug_ref[0:HALO, :] = ug_ref[tm:tm + HALO, :]
        uv_ref[0:HALO, :] = uv_ref[tm:tm + HALO, :]

    h = h_ref[...]
    ug_ref[HALO:HALO + tm, :] = _dot(h, wg_ref[...])
    uv_ref[HALO:HALO + tm, :] = _dot(h, wv_ref[...])

    def conv(u_ref, cw_ref, cb_ref):
        out = cb_ref[...]
        for j in range(CONV_WIDTH):
            off = HALO - (CONV_WIDTH - 1) + j
            out = out + cw_ref[j:j + 1, :] * u_ref[off:off + tm, :]
        return out

    g = conv(ug_ref, cwg_ref, cbg_ref)
    v = conv(uv_ref, cwv_ref, cbv_ref)
    o_ref[...] = (g * jax.nn.sigmoid(g) * v).astype(o_ref.dtype)


def ffn_up(h, w_up_all, layer, conv_w, conv_b, seq, tm=1024, tn=512):
    m, d = h.shape
    f = w_up_all.shape[2] // 2
    nj = f // tn
    tps = seq // tm
    cb = conv_b.reshape(1, 2 * f)
    return pl.pallas_call(
        functools.partial(_ffn_up_kernel, tm=tm, tiles_per_seq=tps),
        grid=(nj, m // tm),
        in_specs=[pl.BlockSpec((tm, d), lambda j, i: (i, 0)),
                  pl.BlockSpec((None, d, tn), lambda j, i: (layer, 0, j)),
                  pl.BlockSpec((None, d, tn), lambda j, i: (layer, 0, nj + j)),
                  pl.BlockSpec((CONV_WIDTH, tn), lambda j, i: (0, j)),
                  pl.BlockSpec((CONV_WIDTH, tn), lambda j, i: (0, nj + j)),
                  pl.BlockSpec((1, tn), lambda j, i: (0, j)),
                  pl.BlockSpec((1, tn), lambda j, i: (0, nj + j))],
        out_specs=pl.BlockSpec((tm, tn), lambda j, i: (i, j)),
        out_shape=jax.ShapeDtypeStruct((m, f), BF16),
        scratch_shapes=[pltpu.VMEM((tm + HALO, tn), F32), pltpu.VMEM((tm + HALO, tn), F32),
                        pltpu.VMEM((d, tn), BF16), pltpu.VMEM((d, tn), BF16)],
        compiler_params=_cparams(("arbitrary", "arbitrary")),
        name="ffn_up_conv",
    )(h, w_up_all, w_up_all, conv_w, conv_w, cb, cb)


LOG2E = 1.4426950408889634
CWIN = 16


def _t5_shifted(dist, tbl_ref, h):
    val = jnp.full(dist.shape, tbl_ref[0, h], F32)
    for b in range(1, REL_BUCKETS):
        val = jnp.where(dist >= T5_THR[b - 1], tbl_ref[b, h], val)
    return (val - tbl_ref[REL_BUCKETS - 1, h]) * LOG2E


def _bias_tiles_kernel(tbl_ref, dt_ref, dc_ref):
    h = pl.program_id(0)
    key = lax.broadcasted_iota(jnp.int32, (LANE, LANE), 0)
    q = lax.broadcasted_iota(jnp.int32, (LANE, LANE), 1)
    for rel in range(2):
        dt_ref[0, rel] = _t5_shifted(rel * LANE + q - key, tbl_ref, h)
    dt_ref[0, 2] = jnp.zeros((LANE, LANE), F32)
    u = lax.broadcasted_iota(jnp.int32, (CWIN, LANE), 0)
    qc = lax.broadcasted_iota(jnp.int32, (CWIN, LANE), 1)
    dc_ref[0] = _t5_shifted(qc - CMP_STRIDE * (u - CWIN // 2) - (CMP_BLOCK - 1), tbl_ref, h)


def bias_tiles(rel_bias):
    nh = rel_bias.shape[1]
    return pl.pallas_call(
        _bias_tiles_kernel,
        grid=(nh,),
        in_specs=[pl.BlockSpec(memory_space=pltpu.SMEM)],
        out_specs=[pl.BlockSpec((1, 3, LANE, LANE), lambda h: (h, 0, 0, 0)),
                   pl.BlockSpec((1, CWIN, LANE), lambda h: (h, 0, 0))],
        out_shape=[jax.ShapeDtypeStruct((nh, 3, LANE, LANE), F32),
                   jax.ShapeDtypeStruct((nh, CWIN, LANE), F32)],
        compiler_params=_cparams(("arbitrary",)),
        name="t5_bias_tiles",
    )(rel_bias)


def _compress_kernel(x_ref, pe_ref, w1_ref, b1_ref, w2_ref, b2_ref, g_ref, o_ref, *, half):
    kv = pl.program_id(0)
    nchunk = x_ref.shape[1] // CMP_STRIDE
    a = jnp.zeros((nchunk, CMP_HIDDEN), F32)
    b = jnp.zeros((nchunk, CMP_HIDDEN), F32)
    for p in range(CMP_STRIDE):
        xp = x_ref[0, pl.ds(p, nchunk, stride=CMP_STRIDE), :]
        rows = slice(p * HEAD_DIM, (p + 1) * HEAD_DIM)
        a = a + _dot((xp + pe_ref[0, p:p + 1, :]).astype(BF16), w1_ref[0, rows, :])
        q = CMP_STRIDE + p
        b = b + _dot((xp + pe_ref[0, q:q + 1, :]).astype(BF16),
                     w1_ref[0, half + p * HEAD_DIM:half + (p + 1) * HEAD_DIM, :])
    b_next = jnp.concatenate([b[1:], jnp.zeros((1, b.shape[1]), F32)], axis=0)
    hid = jax.nn.gelu(a + b_next + b1_ref[0])
    out = _dot(hid.astype(BF16), w2_ref[0]) + b2_ref[0]
    normed = out * lax.rsqrt(jnp.mean(out * out, axis=-1, keepdims=True) + EPS) * g_ref[...]
    out = jnp.where(kv == 0, normed, out)
    o_ref[0, 0] = out.astype(o_ref.dtype)


def compress_kv(proj, slab0, bsz, seq, cmp_pe, cmp_w1, cmp_b1, cmp_w2, cmp_b2, g_k):
    nslab, m, _ = proj.shape
    nchunk = seq // CMP_STRIDE
    half = CMP_STRIDE * HEAD_DIM
    del nslab, m
    return pl.pallas_call(
        functools.partial(_compress_kernel, half=half),
        grid=(2, bsz, NSA_GROUPS),
        in_specs=[pl.BlockSpec((1, seq, HEAD_DIM), lambda kv, b, g: (slab0 + 2 * kv + g, b, 0)),
                  pl.BlockSpec((1, CMP_BLOCK, HEAD_DIM), lambda kv, b, g: (kv, 0, 0)),
                  pl.BlockSpec((1, 2 * half, CMP_HIDDEN), lambda kv, b, g: (kv, 0, 0)),
                  pl.BlockSpec((1, 1, CMP_HIDDEN), lambda kv, b, g: (kv, 0, 0)),
                  pl.BlockSpec((1, CMP_HIDDEN, HEAD_DIM), lambda kv, b, g: (kv, 0, 0)),
                  pl.BlockSpec((1, 1, HEAD_DIM), lambda kv, b, g: (kv, 0, 0)),
                  pl.BlockSpec((1, HEAD_DIM), lambda kv, b, g: (0, 0))],
        out_specs=pl.BlockSpec((1, 1, nchunk, HEAD_DIM), lambda kv, b, g: (kv, g, b, 0)),
        out_shape=jax.ShapeDtypeStruct((2, NSA_GROUPS, bsz * nchunk, HEAD_DIM), BF16),
        compiler_params=_cparams(("arbitrary", "arbitrary", "arbitrary")),
        name="nsa_compress",
    )(proj, cmp_pe, cmp_w1.astype(BF16), cmp_b1.reshape(2, 1, CMP_HIDDEN), cmp_w2.astype(BF16),
      cmp_b2.reshape(2, 1, HEAD_DIM), g_k.reshape(1, HEAD_DIM))


KW = 512
PV_KEYS = 512


def _tile_lanes(x, n):
    return jnp.concatenate([x] * n, axis=1)


def _flash_init(m_ref, l_ref, acc_ref):
    m_ref[...] = jnp.full(m_ref.shape, NEG, F32)
    l_ref[...] = jnp.zeros(l_ref.shape, F32)
    acc_ref[...] = jnp.zeros(acc_ref.shape, F32)


def _zero_after(x):
    bits = pltpu.bitcast(x, jnp.int32)
    return lax.shift_right_logical(lax.shift_right_logical(bits, 16), 16).astype(F32)


def _flash_update(s, v_t, m_ref, l_ref, acc_ref, col_max=None, after=None):
    m_old = m_ref[...]
    if col_max is None:
        col_max = jnp.max(s, axis=0, keepdims=True)
    m_new = jnp.maximum(m_old, col_max)
    alpha = jnp.exp2(m_old - m_new)
    l_new = alpha * l_ref[...]
    acc = alpha * acc_ref[...]
    nk = s.shape[0]
    for k0 in range(0, nk, PV_KEYS):
        p = jnp.exp2(s[k0:k0 + PV_KEYS] - m_new)
        l_new = l_new + jnp.sum(p, axis=0, keepdims=True)
        acc = acc + _dot(v_t[:, k0:k0 + PV_KEYS], p.astype(BF16))
    l_ref[...] = l_new
    acc_ref[...] = acc
    m_ref[...] = m_new if after is None else m_new + _zero_after(after)


SAFE_LOG2_BOUND = 60.0


def _flash_accumulate(s, v_t, l_ref, acc_ref, after=None):
    l_new = l_ref[...]
    acc = acc_ref[...]
    for k0 in range(0, s.shape[0], PV_KEYS):
        p = jnp.exp2(s[k0:k0 + PV_KEYS])
        l_new = l_new + jnp.sum(p, axis=0, keepdims=True)
        acc = acc + _dot(v_t[:, k0:k0 + PV_KEYS], p.astype(BF16))
    if after is not None:
        l_new = l_new + jnp.max(_zero_after(after), axis=0, keepdims=True)
    l_ref[...] = l_new
    acc_ref[...] = acc


def _sum_result(l_ref, acc_ref):
    den = l_ref[...]
    ok = den > 0.0
    return acc_ref[...] * jnp.where(ok, 1.0 / jnp.where(ok, den, 1.0), 0.0)


def _inv_den(m, den):
    ok = m > 0.5 * NEG
    return jnp.where(ok, 1.0 / jnp.where(ok, den, 1.0), 0.0)


def _flash_result(m_ref, l_ref, acc_ref):
    return acc_ref[...] * _inv_den(m_ref[...], l_ref[...])


def _softmax_cols(s):
    m = jnp.max(s, axis=0, keepdims=True)
    p = jnp.exp2(s - m)
    return p * _inv_den(m, jnp.sum(p, axis=0, keepdims=True))


def _near_bias(dt_ref, heads, qi, kt0, ntile):
    rows = []
    for j in range(ntile):
        rel = jnp.clip(qi - (kt0 + j), 0, 2)
        rows.append(jnp.concatenate([dt_ref[h, rel] for h in heads], axis=1))
    return jnp.concatenate(rows, axis=0)


def _pipelined_chunks(n, qk_stage, soft_stage):
    @pl.when(n > 0)
    def _():
        qk_stage(0, 0)

    def pair(p, x):
        c = 2 * p
        ahead = qk_stage(c + 1, 1)
        soft_stage(c, 0, ahead)
        ahead = qk_stage(jnp.minimum(c + 2, n - 1), 0)
        soft_stage(c + 1, 1, ahead)
        return x

    lax.fori_loop(0, n // 2, pair, 0)

    @pl.when(n % 2 == 1)
    def _():
        soft_stage(n - 1, 0, None)


NSA_STATE = 9
GATE_ROWS = -(-3 * NSA_HPG // SUBLANE) * SUBLANE


def _nsa_kernel(bound_ref, qt_ref, gt_ref, kc_ref, vct_ref, ks_ref, vst_ref, kw_ref, vwt_ref,
                dt_ref, dc_ref, ext_ref, o_ref, *scratch, seq, nc):
    ng = NSA_GROUPS
    state = [scratch[NSA_STATE * g:NSA_STATE * (g + 1)] for g in range(ng)]
    qi = pl.program_id(1)
    q0 = qi * QB
    hpg = NSA_HPG
    ncp = kc_ref.shape[1]
    ns = seq // SEL_BLOCK
    group_heads = [[g * hpg + h for h in range(hpg)] for g in range(ng)]
    q_ts = [jnp.concatenate([qt_ref[h] for h in group_heads[g]], axis=1) for g in range(ng)]
    pad = CWIN // 2
    wkeys = WINDOW + QB
    start = pl.multiple_of(jnp.maximum(q0 - WINDOW, 0), LANE)
    r0 = pl.multiple_of(qi * (QB // CMP_STRIDE), 8)

    s_w = []
    for g in range(ng):
        sc_ref = state[g][0]
        sc_ref[0:pad, :] = jnp.zeros((pad, hpg * QB), F32)
        sc_ref[pad + ncp:2 * pad + ncp, :] = jnp.zeros((pad, hpg * QB), F32)
        sc_ref[pad:pad + ncp, :] = _dot(kc_ref[g], q_ts[g])
        s_w.append(_dot(kw_ref[g, pl.ds(start, wkeys), :], q_ts[g]))

    ci = lax.broadcasted_iota(jnp.int32, (ncp, QB), 0)
    tc = q0 + lax.broadcasted_iota(jnp.int32, (ncp, QB), 1)
    valid_c = (ci * CMP_STRIDE + CMP_BLOCK - 1 <= tc) & (ci < nc)
    madd_c = _tile_lanes(jnp.where(valid_c, 0.0, NEG), hpg)
    oc_t, p_sum = [], []
    for g in range(ng):
        sc_ref = state[g][0]
        sc_ref[pl.ds(r0, CWIN), :] = sc_ref[pl.ds(r0, CWIN), :] + jnp.concatenate(
            [dc_ref[h] for h in group_heads[g]], axis=1)
        p_c = _softmax_cols(sc_ref[pad:pad + ncp, :] + madd_c)
        oc_t.append(_dot(vct_ref[g], p_c.astype(BF16)))
        ps = p_c[:, 0:QB]
        for h in range(1, hpg):
            ps = ps + p_c[:, h * QB:(h + 1) * QB]
        p_sum.append(ps)

    dist_w = (q0 + lax.broadcasted_iota(jnp.int32, (wkeys, QB), 1)) - (
        start + lax.broadcasted_iota(jnp.int32, (wkeys, QB), 0))
    madd_w = _tile_lanes(jnp.where((dist_w >= 0) & (dist_w < WINDOW), 0.0, NEG), hpg)
    ow_t = []
    for g in range(ng):
        p_w = _softmax_cols(s_w[g] + _near_bias(dt_ref, group_heads[g], qi, start // LANE, wkeys // LANE) + madd_w)
        ow_t.append(_dot(vwt_ref[g, :, pl.ds(start, wkeys)], p_w.astype(BF16)))

    per = SEL_BLOCK // CMP_STRIDE
    blk = lax.broadcasted_iota(jnp.int32, (LANE, QB), 0)
    t = q0 + lax.broadcasted_iota(jnp.int32, (LANE, QB), 1)
    tb = t // SEL_BLOCK
    forced = (blk == 0) | (blk == tb) | (blk == tb - 1)
    blk_f = blk.astype(F32)
    scores = []
    for g in range(ng):
        ps_ref = state[g][1]
        ps_ref[0:SUBLANE, :] = jnp.zeros((SUBLANE, QB), F32)
        ps_ref[SUBLANE:SUBLANE + ncp, :] = p_sum[g]
        band = [ps_ref[pl.ds(SUBLANE + r, ns, stride=per), :] for r in range(-1, per)]
        imp = 0.5 * band[0] + band[1] + band[2] + band[3] + 0.5 * band[4]
        if ns < LANE:
            imp = jnp.concatenate([imp, jnp.zeros((LANE - ns, QB), F32)], axis=0)
        score = jnp.where(forced, FORCE, jnp.where(blk * SEL_BLOCK <= t, imp, NEG))
        scores.append(jnp.where(blk < ns, score, -jnp.inf))
    sels = [jnp.zeros((LANE, QB), F32) for _ in range(ng)]
    for _ in range(min(SEL_TOP_N, ns)):
        for g in range(ng):
            mx = jnp.max(scores[g], axis=0, keepdims=True)
            first = jnp.min(jnp.where(scores[g] == mx, blk_f, float(LANE)), axis=0, keepdims=True)
            pick = blk_f == first
            sels[g] = jnp.where(pick, 1.0, sels[g])
            scores[g] = jnp.where(pick, -jnp.inf, scores[g])
    sel_b = [s.astype(BF16) for s in sels]

    kpos = lax.broadcasted_iota(jnp.int32, (KW, QB), 0)
    tq = q0 + lax.broadcasted_iota(jnp.int32, (KW, QB), 1)
    bounded_ok = bound_ref[0] <= SAFE_LOG2_BOUND
    shift = jnp.where(bounded_ok, bound_ref[0], 0.0)
    c_near = jnp.maximum(qi - 1, 0) // (KW // LANE)

    def scores_of(g, c0):
        chosen = _dot(ext_ref[pl.ds(c0, KW), :], sel_b[g])
        return (chosen - 1.0) * (-NEG) - shift, _dot(ks_ref[g, pl.ds(c0, KW), :], q_ts[g])

    def attend(bounded):
        for g in range(ng):
            _flash_init(*state[g][2:5])

        def qk_stage(c, buf):
            c0 = pl.multiple_of(c * KW, KW)
            ahead = []
            for g in range(ng):
                madd, s = scores_of(g, c0)
                s = s + _tile_lanes(madd, hpg)
                state[g][5 + buf][...] = s
                if bounded:
                    ahead.append(s[KW - 8:KW])
                else:
                    ahead.append(jnp.max(s, axis=0, keepdims=True))
                    state[g][7 + buf][...] = ahead[-1]
            return ahead

        def soft_stage(c, buf, ahead):
            c0 = pl.multiple_of(c * KW, KW)
            for g in range(ng):
                m_ref, l_ref, acc_ref = state[g][2:5]
                after = None if ahead is None else ahead[g]
                v_t = vst_ref[g, :, pl.ds(c0, KW)]
                if bounded:
                    _flash_accumulate(state[g][5 + buf][...], v_t, l_ref, acc_ref, after=after)
                else:
                    _flash_update(state[g][5 + buf][...], v_t, m_ref, l_ref, acc_ref,
                                  col_max=state[g][7 + buf][...], after=after)

        def near_step(c, x):
            c0 = pl.multiple_of(c * KW, KW)
            causal = jnp.where(c0 + kpos <= tq, 0.0, NEG)
            scores = []
            for g in range(ng):
                madd, s = scores_of(g, c0)
                scores.append(s + _tile_lanes(madd + causal, hpg)
                              + _near_bias(dt_ref, group_heads[g], qi, c * (KW // LANE), KW // LANE))
            for g in range(ng):
                m_ref, l_ref, acc_ref = state[g][2:5]
                if bounded:
                    _flash_accumulate(scores[g], vst_ref[g, :, pl.ds(c0, KW)], l_ref, acc_ref)
                else:
                    _flash_update(scores[g], vst_ref[g, :, pl.ds(c0, KW)], m_ref, l_ref, acc_ref)
            return x

        _pipelined_chunks(c_near, qk_stage, soft_stage)
        lax.fori_loop(c_near, qi // (KW // LANE) + 1, near_step, 0)
        for g in range(ng):
            m_ref, l_ref, acc_ref = state[g][2:5]
            acc_ref[...] = _sum_result(l_ref, acc_ref) if bounded else _flash_result(m_ref, l_ref, acc_ref)

    pl.when(bounded_ok)(lambda: attend(True))
    pl.when(jnp.logical_not(bounded_ok))(lambda: attend(False))

    for g in range(ng):
        os_t = state[g][4][...]
        gates = jax.nn.sigmoid(gt_ref[g])
        for h in range(hpg):
            sl = slice(h * QB, (h + 1) * QB)
            o_t = (gates[3 * h:3 * h + 1] * oc_t[g][:, sl] + gates[3 * h + 1:3 * h + 2] * os_t[:, sl]
                   + gates[3 * h + 2:3 * h + 3] * ow_t[g][:, sl])
            hh = group_heads[g][h]
            o_ref[:, hh * HEAD_DIM:(hh + 1) * HEAD_DIM] = o_t.T.astype(o_ref.dtype)


def nsa_attention(logit_bound, q_t, gates_t, kc, vc_t, k_sw, v_sw_t, dt, dc, bsz, seq):
    nq = seq // QB
    ncp = seq // CMP_STRIDE
    nc = ncp - 1
    ns = seq // SEL_BLOCK
    assert ns <= LANE and seq >= WINDOW + QB and seq % KW == 0
    assert CMP_BLOCK == 2 * CMP_STRIDE and SEL_BLOCK == 4 * CMP_STRIDE
    expand =((np.arange(seq)[:, None] // SEL_BLOCK) == np.arange(LANE)[None, :]).astype(np.float32)
    ng = NSA_GROUPS
    once = dict(pipeline_mode=pl.Buffered(1))
    ks_spec = pl.BlockSpec((ng, seq, HEAD_DIM), lambda b, i: (0, b, 0), **once)
    kw_spec = pl.BlockSpec((ng, seq, HEAD_DIM), lambda b, i: (1, b, 0), **once)
    vs_spec = pl.BlockSpec((ng, HEAD_DIM, seq), lambda b, i: (0, 0, b), **once)
    vw_spec = pl.BlockSpec((ng, HEAD_DIM, seq), lambda b, i: (1, 0, b), **once)
    lanes = NSA_HPG * QB
    group_state = [pltpu.VMEM((ncp + CWIN, lanes), F32), pltpu.VMEM((ncp + SUBLANE, QB), F32),
                   pltpu.VMEM((1, lanes), F32), pltpu.VMEM((1, lanes), F32), pltpu.VMEM((HEAD_DIM, lanes), F32),
                   pltpu.VMEM((KW, lanes), F32), pltpu.VMEM((KW, lanes), F32),
                   pltpu.VMEM((1, lanes), F32), pltpu.VMEM((1, lanes), F32)]
    assert len(group_state) == NSA_STATE
    return pl.pallas_call(
        functools.partial(_nsa_kernel, seq=seq, nc=nc),
        grid=(bsz, nq),
        in_specs=[pl.BlockSpec(memory_space=pltpu.SMEM),
                  pl.BlockSpec((NSA_HEADS, HEAD_DIM, QB), lambda b, i: (0, 0, b * nq + i)),
                  pl.BlockSpec((ng, GATE_ROWS, QB), lambda b, i: (0, 0, b * nq + i)),
                  pl.BlockSpec((ng, ncp, HEAD_DIM), lambda b, i: (0, b, 0)),
                  pl.BlockSpec((ng, HEAD_DIM, ncp), lambda b, i: (0, 0, b)),
                  ks_spec, vs_spec, kw_spec, vw_spec,
                  pl.BlockSpec((NSA_HEADS, 3, LANE, LANE), lambda b, i: (0, 0, 0, 0)),
                  pl.BlockSpec((NSA_HEADS, CWIN, LANE), lambda b, i: (0, 0, 0)),
                  pl.BlockSpec((seq, LANE), lambda b, i: (0, 0))],
        out_specs=pl.BlockSpec((QB, NSA_HEADS * HEAD_DIM), lambda b, i: (b * nq + i, 0)),
        out_shape=jax.ShapeDtypeStruct((bsz * seq, NSA_HEADS * HEAD_DIM), BF16),
        scratch_shapes=group_state * ng,
        compiler_params=_cparams(("arbitrary", "arbitrary")),
        name="nsa_attention",
    )(jnp.reshape(logit_bound, (1,)).astype(F32), q_t, gates_t, kc, vc_t, k_sw, v_sw_t, k_sw, v_sw_t, dt, dc,
      jnp.asarray(expand, BF16))


MLA_HPS = 2


def _mla_kernel(bound_ref, qt_ref, k_ref, vt_ref, o_ref, *scratch):
    qi = pl.program_id(2)
    chains = [scratch[3 * h:3 * h + 3] for h in range(MLA_HPS)]
    sbuf = [scratch[(3 + b) * MLA_HPS:(4 + b) * MLA_HPS] for b in range(2)]
    cbuf = [scratch[(5 + b) * MLA_HPS:(6 + b) * MLA_HPS] for b in range(2)]
    c_diag = pl.multiple_of(qi * KW, KW)
    kpos = lax.broadcasted_iota(jnp.int32, (KW, KW), 0)
    tq = lax.broadcasted_iota(jnp.int32, (KW, KW), 1)
    dv = vt_ref.shape[1]

    def attend(bounded):
        for ch in chains:
            _flash_init(*ch)

        def qk_stage(c, buf):
            c0 = pl.multiple_of(c * KW, KW)
            ahead = []
            for h in range(MLA_HPS):
                s = _dot(k_ref[h, pl.ds(c0, KW), :], qt_ref[h])
                sbuf[buf][h][...] = s
                if bounded:
                    ahead.append(s[KW - 8:KW])
                else:
                    ahead.append(jnp.max(s, axis=0, keepdims=True))
                    cbuf[buf][h][...] = ahead[-1]
            return ahead

        def soft_stage(c, buf, ahead):
            c0 = pl.multiple_of(c * KW, KW)
            for h, (m_ref, l_ref, acc_ref) in enumerate(chains):
                after = None if ahead is None else ahead[h]
                if bounded:
                    _flash_accumulate(sbuf[buf][h][...], vt_ref[h, :, pl.ds(c0, KW)], l_ref, acc_ref, after=after)
                else:
                    _flash_update(sbuf[buf][h][...], vt_ref[h, :, pl.ds(c0, KW)], m_ref, l_ref, acc_ref,
                                  col_max=cbuf[buf][h][...], after=after)

        _pipelined_chunks(qi, qk_stage, soft_stage)
        causal = jnp.where(kpos <= tq, 0.0, NEG)
        scores = [_dot(k_ref[h, pl.ds(c_diag, KW), :], qt_ref[h]) + causal for h in range(MLA_HPS)]
        for h, (m_ref, l_ref, acc_ref) in enumerate(chains):
            if bounded:
                _flash_accumulate(scores[h], vt_ref[h, :, pl.ds(c_diag, KW)], l_ref, acc_ref)
                o_t = _sum_result(l_ref, acc_ref)
            else:
                _flash_update(scores[h], vt_ref[h, :, pl.ds(c_diag, KW)], m_ref, l_ref, acc_ref)
                o_t = _flash_result(m_ref, l_ref, acc_ref)
            o_ref[:, h * dv:(h + 1) * dv] = o_t.T.astype(o_ref.dtype)

    bounded_ok = bound_ref[0] <= SAFE_LOG2_BOUND
    pl.when(bounded_ok)(lambda: attend(True))
    pl.when(jnp.logical_not(bounded_ok))(lambda: attend(False))


def mla_attention(logit_bound, q_t, k, v_t, bsz, seq):
    nh, dqk, _ = q_t.shape
    dv = v_t.shape[1]
    nq = seq // KW
    hps = MLA_HPS
    state = [pltpu.VMEM((1, KW), F32), pltpu.VMEM((1, KW), F32), pltpu.VMEM((dv, KW), F32)] * hps
    state += [pltpu.VMEM((KW, KW), F32)] * (2 * hps)
    state += [pltpu.VMEM((1, KW), F32)] * (2 * hps)
    return pl.pallas_call(
        _mla_kernel,
        grid=(bsz, nh // hps, nq),
        in_specs=[pl.BlockSpec(memory_space=pltpu.SMEM),
                  pl.BlockSpec((hps, dqk, KW), lambda b, h, i: (h, 0, b * nq + i)),
                  pl.BlockSpec((hps, seq, dqk), lambda b, h, i: (h, b, 0)),
                  pl.BlockSpec((hps, dv, seq), lambda b, h, i: (h, 0, b))],
        out_specs=pl.BlockSpec((KW, hps * dv), lambda b, h, i: (b * nq + i, h)),
        out_shape=jax.ShapeDtypeStruct((bsz * seq, nh * dv), BF16),
        scratch_shapes=state,
        compiler_params=_cparams(("arbitrary", "arbitrary", "arbitrary")),
        name="mla_attention",
    )(jnp.reshape(logit_bound, (1,)).astype(F32), q_t, k, v_t)


INT_MIN = -2 ** 31
NEG_KEY = int(np.array(NEG, np.float32).view(np.int32)) ^ 0x7FFFFFFF
KEY_BITS = 32
SURE_BITS = 22


def _sort_key(x):
    bits = pltpu.bitcast(x + 0.0, jnp.int32)
    return jnp.where(bits < 0, bits ^ 0x7FFFFFFF, bits)


def _dsa_kernel(bound_ref, iqt_ref, iwt_ref, ik_ref, qt_ref, k_ref, vt_ref, dt_ref, o_ref,
                key_ref, *state, seq, k_sel):
    qi = pl.program_id(1)
    q0 = qi * QB
    n_chunk = (q0 + QB + KW - 1) // KW
    n_rest = seq - n_chunk * KW
    kpos = lax.broadcasted_iota(jnp.int32, (KW, QB), 0)
    tq = q0 + lax.broadcasted_iota(jnp.int32, (KW, QB), 1)
    hpp = KW // QB

    def score_chunk(c):
        c0 = pl.multiple_of(c * KW, KW)
        ikc = ik_ref[pl.ds(c0, KW), :]
        acc = jnp.zeros((KW, QB), F32)
        for piece in range(IDX_HEADS // hpp):
            sl = slice(piece * KW, (piece + 1) * KW)
            s = jnp.maximum(_dot(ikc, iqt_ref[0, :, sl]), 0.0) * iwt_ref[0, :, sl]
            for j in range(hpp):
                acc = acc + s[:, j * QB:(j + 1) * QB]
        acc = jnp.where(c0 + kpos <= tq, acc, NEG)
        key_ref[pl.ds(c0, KW), :] = _sort_key(acc)

    def score_pair(j, x):
        score_chunk(2 * j)
        score_chunk(jnp.minimum(2 * j + 1, n_chunk - 1))
        return x

    lax.fori_loop(0, (n_chunk + 1) // 2, score_pair, 0)

    def count(pred):
        def chunk_hits(c):
            c0 = pl.multiple_of(c * KW, KW)
            hit = jnp.where(pred(key_ref[pl.ds(c0, KW), :], c0), 1.0, 0.0)
            parts = [hit[SUBLANE * i:SUBLANE * (i + 1)] for i in range(KW // SUBLANE)]
            while len(parts) > 1:
                parts = [parts[i] + parts[i + 1] for i in range(0, len(parts), 2)]
            return parts[0]

        def body(j, acc):
            second = 2 * j + 1
            weight = jnp.where(second < n_chunk, 1.0, 0.0)
            return acc + chunk_hits(2 * j) + chunk_hits(jnp.minimum(second, n_chunk - 1)) * weight

        acc = lax.fori_loop(0, (n_chunk + 1) // 2, body, jnp.zeros((SUBLANE, QB), F32))
        return jnp.sum(acc, axis=0, keepdims=True)

    rest = n_rest.astype(F32)
    kf = float(k_sel)

    def bit_step(i, st):
        u, thr_s, settled = st
        bit = jnp.left_shift(jnp.int32(1), KEY_BITS - 1 - i)
        trial = (u | bit) ^ INT_MIN
        cnt = count(lambda keys, c0: keys >= trial) + jnp.where(NEG_KEY >= trial, rest, 0.0)
        new = (cnt == kf) & (settled < 0.5)
        return (jnp.where(cnt >= kf, u | bit, u), jnp.where(new, trial, thr_s), jnp.where(new, 1.0, settled))

    st = (jnp.zeros((1, QB), jnp.int32), jnp.zeros((1, QB), jnp.int32), jnp.zeros((1, QB), F32))
    st = lax.fori_loop(0, SURE_BITS, bit_step, st)
    _, (u, thr_s, settled) = lax.while_loop(
        lambda c: (c[0] < KEY_BITS) & (jnp.min(c[1][2]) < 0.5),
        lambda c: (c[0] + 1, bit_step(c[0], c[1])), (jnp.int32(SURE_BITS), st))
    is_settled = settled > 0.5
    thr = jnp.where(is_settled, thr_s, u ^ INT_MIN)

    def edge_counts():
        return (count(lambda keys, c0: keys > thr) + jnp.where(NEG_KEY > thr, rest, 0.0),
                count(lambda keys, c0: keys >= thr) + jnp.where(NEG_KEY >= thr, rest, 0.0))

    zero_cnt = jnp.zeros((1, QB), F32)
    cnt_gt, cnt_ge = lax.cond(jnp.min(settled) > 0.5, lambda: (zero_cnt, zero_cnt), edge_counts)
    need = kf - cnt_gt
    tie_q = (cnt_ge > kf) & (thr != NEG_KEY) & jnp.logical_not(is_settled)
    idx_bits = (seq - 1).bit_length()
    no_cut = 2 ** 30

    def tie_cut():
        def idx_step(i, x):
            bit = jnp.left_shift(jnp.int32(1), idx_bits - 1 - i)
            trial = x | bit
            f = count(lambda keys, c0: (keys == thr) & (c0 + kpos < trial))
            return jnp.where(f <= need - 1.0, trial, x)
        return lax.fori_loop(0, idx_bits, idx_step, jnp.zeros((1, QB), jnp.int32))

    any_tie = jnp.max(jnp.where(tie_q, 1.0, 0.0)) > 0.0
    x_cut = lax.cond(any_tie, tie_cut, lambda: jnp.full((1, QB), no_cut, jnp.int32))
    x_cut = jnp.where(tie_q, x_cut, no_cut)

    bounded_ok = bound_ref[0] <= SAFE_LOG2_BOUND
    shift = jnp.where(bounded_ok, bound_ref[0], 0.0)

    def mask_add(c0):
        keys = key_ref[pl.ds(c0, KW), :]
        pos = c0 + kpos
        chosen = (keys > thr) | ((keys == thr) & (pos <= x_cut))
        return _tile_lanes(jnp.where(chosen & (pos <= tq), -shift, NEG), DSA_HPG)

    c_near = jnp.maximum(qi - 1, 0) // (KW // LANE)
    ng = DSA_KV_HEADS
    chains = [state[3 * g:3 * g + 3] for g in range(ng)]
    sbuf = [state[(3 + b) * ng:(4 + b) * ng] for b in range(2)]
    cbuf = [state[(5 + b) * ng:(6 + b) * ng] for b in range(2)]
    group_heads = [[g * DSA_HPG + h for h in range(DSA_HPG)] for g in range(ng)]

    def raw_scores(c0, g):
        q_t = jnp.concatenate([qt_ref[h] for h in group_heads[g]], axis=1)
        return _dot(k_ref[g, pl.ds(c0, KW), :], q_t)

    def attend(bounded):
        for ch in chains:
            _flash_init(*ch)

        def qk_stage(c, buf):
            c0 = pl.multiple_of(c * KW, KW)
            madd = mask_add(c0)
            ahead = []
            for g in range(ng):
                s = raw_scores(c0, g) + madd
                sbuf[buf][g][...] = s
                if bounded:
                    ahead.append(s[KW - 8:KW])
                else:
                    ahead.append(jnp.max(s, axis=0, keepdims=True))
                    cbuf[buf][g][...] = ahead[-1]
            return ahead

        def soft_stage(c, buf, ahead):
            c0 = pl.multiple_of(c * KW, KW)
            for g, (m_ref, l_ref, acc_ref) in enumerate(chains):
                after = None if ahead is None else ahead[g]
                if bounded:
                    _flash_accumulate(sbuf[buf][g][...], vt_ref[g, :, pl.ds(c0, KW)], l_ref, acc_ref, after=after)
                else:
                    _flash_update(sbuf[buf][g][...], vt_ref[g, :, pl.ds(c0, KW)], m_ref, l_ref, acc_ref,
                                  col_max=cbuf[buf][g][...], after=after)

        _pipelined_chunks(c_near, qk_stage, soft_stage)

        def near_step(c, x):
            c0 = pl.multiple_of(c * KW, KW)
            madd = mask_add(c0)
            scores = [raw_scores(c0, g) + madd
                      + _near_bias(dt_ref, group_heads[g], qi, c * (KW // LANE), KW // LANE)
                      for g in range(ng)]
            for g, (m_ref, l_ref, acc_ref) in enumerate(chains):
                if bounded:
                    _flash_accumulate(scores[g], vt_ref[g, :, pl.ds(c0, KW)], l_ref, acc_ref)
                else:
                    _flash_update(scores[g], vt_ref[g, :, pl.ds(c0, KW)], m_ref, l_ref, acc_ref)
            return x

        lax.fori_loop(c_near, n_chunk, near_step, 0)
        for g, (m_ref, l_ref, acc_ref) in enumerate(chains):
            o_t = _sum_result(l_ref, acc_ref) if bounded else _flash_result(m_ref, l_ref, acc_ref)
            for h in range(DSA_HPG):
                hh = group_heads[g][h]
                o_ref[:, hh * HEAD_DIM:(hh + 1) * HEAD_DIM] = o_t[:, h * QB:(h + 1) * QB].T.astype(o_ref.dtype)

    pl.when(bounded_ok)(lambda: attend(True))
    pl.when(jnp.logical_not(bounded_ok))(lambda: attend(False))


def _indexer_layouts(get_slab, c_ref, sa_ref, sb_ref, iqt_ref, ik_ref, iwt_ref):
    nslab_q = IDX_HEADS * IDX_DIM // LANE
    per = LANE // IDX_DIM
    half = IDX_ROPE // 2
    zrows = jnp.zeros((LANE - IDX_DIM, QB), F32)

    def rope_slab(x, c, sa, sb):
        return x * c + pltpu.roll(x, LANE - half, axis=1) * sa + pltpu.roll(x, half, axis=1) * sb

    for t in range(iqt_ref.shape[0]):
        rows = slice(t * QB, (t + 1) * QB)
        c, sa, sb = c_ref[rows, :], sa_ref[rows, :], sb_ref[rows, :]
        cols = []
        for s in range(nslab_q):
            x_t = (rope_slab(get_slab(s, rows), c, sa, sb) * IDX_DIM ** -0.5).T
            for j in range(per):
                cols.append(jnp.concatenate([x_t[j * IDX_DIM:(j + 1) * IDX_DIM], zrows], axis=0))
        iqt_ref[t] = jnp.concatenate(cols, axis=1).astype(iqt_ref.dtype)
        tail = get_slab(nslab_q, rows)
        lane = lax.broadcasted_iota(jnp.int32, (QB, LANE), 1)
        ik_ref[rows, :] = jnp.where(lane < IDX_DIM, rope_slab(tail, c, sa, sb), 0.0).astype(ik_ref.dtype)
        w_t = (tail * IDX_HEADS ** -0.5).T
        iwt_ref[t] = jnp.concatenate([w_t[IDX_DIM + h:IDX_DIM + h + 1, :] for h in range(IDX_HEADS)], axis=1)


def dsa_attention(logit_bound, iq_t, iw_t, ik, q_t, k, v_t, dt, bsz, seq):
    nq = seq // QB
    k_sel = min(DSA_TOPK_MAX, seq // 4)
    assert seq % KW == 0
    lanes = DSA_HPG * QB
    return pl.pallas_call(
        functools.partial(_dsa_kernel, seq=seq, k_sel=k_sel),
        grid=(bsz, nq),
        in_specs=[pl.BlockSpec(memory_space=pltpu.SMEM),
                  pl.BlockSpec((1, LANE, IDX_HEADS * QB), lambda b, i: (b * nq + i, 0, 0)),
                  pl.BlockSpec((1, 1, IDX_HEADS * QB), lambda b, i: (b * nq + i, 0, 0)),
                  pl.BlockSpec((seq, LANE), lambda b, i: (b, 0)),
                  pl.BlockSpec((DSA_HEADS, HEAD_DIM, QB), lambda b, i: (0, 0, b * nq + i)),
                  pl.BlockSpec((DSA_KV_HEADS, seq, HEAD_DIM), lambda b, i: (0, b, 0),
                               pipeline_mode=pl.Buffered(1)),
                  pl.BlockSpec((DSA_KV_HEADS, HEAD_DIM, seq), lambda b, i: (0, 0, b),
                               pipeline_mode=pl.Buffered(1)),
                  pl.BlockSpec((DSA_HEADS, 3, LANE, LANE), lambda b, i: (0, 0, 0, 0),
                               pipeline_mode=pl.Buffered(1))],
        out_specs=pl.BlockSpec((QB, DSA_HEADS * HEAD_DIM), lambda b, i: (b * nq + i, 0)),
        out_shape=jax.ShapeDtypeStruct((bsz * seq, DSA_HEADS * HEAD_DIM), BF16),
        scratch_shapes=[pltpu.VMEM((seq, QB), jnp.int32)]
        + [pltpu.VMEM((1, lanes), F32), pltpu.VMEM((1, lanes), F32),
           pltpu.VMEM((HEAD_DIM, lanes), F32)] * DSA_KV_HEADS
        + [pltpu.VMEM((KW, lanes), F32)] * (2 * DSA_KV_HEADS)
        + [pltpu.VMEM((1, lanes), F32)] * (2 * DSA_KV_HEADS),
        compiler_params=_cparams(("arbitrary", "arbitrary")),
        name="dsa_attention",
    )(jnp.reshape(logit_bound, (1,)).astype(F32), iq_t, iw_t, ik, q_t, k, v_t, dt)


def _rope_tables(seq, dim):
    half = dim // 2
    inv = ROPE_THETA ** (-jnp.arange(half, dtype=F32) / half)
    ang = jnp.arange(seq, dtype=F32)[:, None] * inv[None, :]
    return jnp.cos(ang), jnp.sin(ang)


def _logit_bound(gq, gk, dim, scale):
    return dim * scale * jnp.max(jnp.abs(gq)) * jnp.max(jnp.abs(gk)) * (1.0 + 2.0 ** -7)


def _pad_cols(w, n):
    return jnp.pad(w, ((0, 0), (0, n - w.shape[1])))


def _t(x):
    return jnp.swapaxes(x, -1, -2)


def _even_mixer(h, x2, gate, next_norm, dt, dc, bias_bound, bsz, seq, w_in, w_out, nsa_qk_g, cmp_pe, cmp_w1, cmp_b1,
                cmp_w2, cmp_b2, q_norm_g, kv_norm_g, w_uq, w_ukv, nope_g, rope_g):
    m = bsz * seq
    nq_cols = NSA_HEADS * HEAD_DIM
    nkv_cols = 6 * NSA_GROUPS * HEAD_DIM
    ngate = 3 * NSA_HEADS
    o_gate = nq_cols + nkv_cols
    o_cq = o_gate + ngate
    o_ckv = o_cq + MLA_Q_RANK
    o_kpe = o_ckv + MLA_KV_RANK
    gw = NSA_GROUPS * HEAD_DIM
    kvw = [w_in[:, nq_cols + i * gw:nq_cols + (i + 1) * gw] for i in range(6)]
    scale = HEAD_DIM ** -0.5 * LOG2E
    tail = jnp.concatenate([w_in[:, o_kpe:], w_in[:, o_gate:o_cq]], axis=1)
    w_r = jnp.concatenate([kvw[0], kvw[1], w_in[:, o_cq:o_kpe], _pad_cols(tail, LANE)], axis=1).astype(BF16)
    q_t, k_sw, v_sw_t, proj = proj_qkv_raw(
        h, w_in[:, :nq_cols].astype(BF16), nsa_qk_g[0] * scale,
        jnp.concatenate([kvw[2], kvw[4]], axis=1).astype(BF16), nsa_qk_g[1],
        jnp.concatenate([kvw[3], kvw[5]], axis=1).astype(BF16), w_r)
    s_cq = 2 * NSA_GROUPS
    s_ckv = s_cq + MLA_Q_RANK // LANE
    s_tail = s_ckv + MLA_KV_RANK // LANE
    kvc = compress_kv(proj, 0, bsz, seq, cmp_pe, cmp_w1, cmp_b1, cmp_w2, cmp_b2, nsa_qk_g[1])
    tail_v = proj[s_tail]
    gates = tail_v[:, MLA_ROPE:MLA_ROPE + ngate].reshape(m, NSA_GROUPS, 3 * NSA_HPG)
    gates_t = jnp.pad(jnp.transpose(gates, (1, 2, 0)), ((0, 0), (0, GATE_ROWS - 3 * NSA_HPG), (0, 0)))
    nsa_bound = _logit_bound(nsa_qk_g[0], nsa_qk_g[1], HEAD_DIM, scale) + bias_bound
    o_nsa = nsa_attention(nsa_bound, q_t, gates_t, kvc[0], _t(kvc[1]), k_sw, v_sw_t,
                          dt[:NSA_HEADS], dc[:NSA_HEADS], bsz, seq)

    dq = MLA_NOPE + MLA_ROPE
    wq = w_uq.reshape(MLA_Q_RANK, MLA_HEADS, dq)
    wq_r = jnp.concatenate([wq[:, :, :MLA_NOPE].reshape(MLA_Q_RANK, -1),
                            wq[:, :, MLA_NOPE:].reshape(MLA_Q_RANK, -1)], axis=1).astype(BF16)
    cos, sin = _rope_tables(seq, MLA_ROPE)
    mscale = dq ** -0.5 * LOG2E
    side = [jnp.sqrt(MLA_NOPE * jnp.max(jnp.abs(nope_g[i])) ** 2 + MLA_ROPE * jnp.max(jnp.abs(rope_g[i])) ** 2)
            for i in range(2)]
    mla_bound = mscale * side[0] * side[1] * (1.0 + 2.0 ** -7)
    mla_shift = jnp.where(mla_bound <= SAFE_LOG2_BOUND, mla_bound, 0.0)
    q_mla_t, k_mla, v_mla_t = mla_project(proj, s_cq, s_ckv, s_tail, seq, q_norm_g, kv_norm_g, wq_r,
                                          w_ukv.astype(BF16), nope_g, rope_g, cos, sin, mscale, mla_shift)
    o_mla = mla_attention(mla_bound, q_mla_t, k_mla, v_mla_t, bsz, seq)
    w_o = w_out.astype(BF16)
    return resproj([(o_nsa, w_o[:nq_cols]), (o_mla, w_o[nq_cols:])], x2, gate, seq, next_norm)


def _odd_mixer(h, x2, gate, next_norm, dt, bias_bound, bsz, seq, w_in, w_out, qk_g):
    nq = DSA_HEADS * HEAD_DIM
    nkv = DSA_KV_HEADS * HEAD_DIM
    niq = IDX_HEADS * IDX_DIM
    o_k, o_v, o_iq = nq, nq + nkv, nq + 2 * nkv
    w_idx = w_in[:, o_iq:]
    q_t, k, v_t, iq_t, ik, iw_t = proj_qkv_indexer(
        h, w_in[:, :o_k].astype(BF16), qk_g[0] * (HEAD_DIM ** -0.5 * LOG2E), w_in[:, o_k:o_v].astype(BF16), qk_g[1],
        w_in[:, o_v:o_iq].astype(BF16), _pad_cols(w_idx, niq + LANE).astype(BF16), seq)
    bound = _logit_bound(qk_g[0], qk_g[1], HEAD_DIM, HEAD_DIM ** -0.5 * LOG2E) + bias_bound
    o = dsa_attention(bound, iq_t, iw_t, ik, q_t, k, v_t, dt, bsz, seq)
    return resproj([(o, w_out.astype(BF16))], x2, gate, seq, next_norm)


def _conv_ffn(h, x2, gate, next_norm, seq, w_up_all, layer, conv_w, conv_b, w_down):
    a = ffn_up(h, w_up_all, layer, conv_w, conv_b, seq)
    return resproj([(a, w_down.astype(BF16))], x2, gate, seq, next_norm)


def kernel(x, c, rel_bias, ada_w, ada_b, norm_g, ev_w_in, ev_w_out, nsa_qk_g, cmp_pe, cmp_w1, cmp_b1, cmp_w2, cmp_b2, mla_q_norm_g, mla_kv_norm_g, mla_w_uq, mla_w_ukv, mla_nope_g, mla_rope_g, od_w_in, od_w_out, dsa_qk_g, ffn_w_up, ffn_conv_w, ffn_conv_b, ffn_w_down):
    bsz, seq, d = x.shape
    depth = ada_w.shape[0]
    x2 = x.reshape(bsz * seq, d)
    mods = ada_all(c, ada_w, ada_b)
    dt, dc = bias_tiles(rel_bias)
    bias_bound = 2.0 * LOG2E * jnp.max(jnp.abs(rel_bias))
    def norm_of(i, sub):
        if i >= depth:
            return None
        shift, scale, _ = jnp.split(mods[i, sub], 3, axis=-1)
        return norm_g[i, sub], scale, shift

    g0, scale0, shift0 = norm_of(0, 0)
    h = modnorm(x2, g0, scale0, shift0, seq)
    for i in range(depth):
        j = i // 2
        gate = jnp.split(mods[i, 0], 3, axis=-1)[2]
        if i % 2 == 0:
            x2, h = _even_mixer(h, x2, gate, norm_of(i, 1), dt, dc, bias_bound, bsz, seq, ev_w_in[j],
                                ev_w_out[j], nsa_qk_g[j], cmp_pe[j], cmp_w1[j], cmp_b1[j], cmp_w2[j],
                                cmp_b2[j], mla_q_norm_g[j], mla_kv_norm_g[j], mla_w_uq[j], mla_w_ukv[j],
                                mla_nope_g[j], mla_rope_g[j])
        else:
            x2, h = _odd_mixer(h, x2, gate, norm_of(i, 1), dt, bias_bound, bsz, seq, od_w_in[j], od_w_out[j],
                               dsa_qk_g[j])
        gate = jnp.split(mods[i, 1], 3, axis=-1)[2]
        x2, h = _conv_ffn(h, x2, gate, norm_of(i + 1, 0), seq, ffn_w_up, i, ffn_conv_w[i], ffn_conv_b[i],
                          ffn_w_down[i])
    return x2.reshape(bsz, seq, d)
```

```python
import functools
import math

import numpy as np
import jax
import jax.numpy as jnp
from jax import lax
from jax.experimental import pallas as pl
from jax.experimental.pallas import tpu as pltpu

HEAD_DIM = 128
NSA_HEADS = 8
NSA_GROUPS = 2
NSA_HPG = NSA_HEADS // NSA_GROUPS
CMP_BLOCK = 32
CMP_STRIDE = 16
CMP_HIDDEN = 256
SEL_BLOCK = 64
SEL_TOP_N = 16
WINDOW = 512
MLA_HEADS = 8
MLA_Q_RANK = 512
MLA_KV_RANK = 256
MLA_NOPE = 128
MLA_ROPE = 64
MLA_V = 128
DSA_HEADS = 16
DSA_KV_HEADS = 4
DSA_HPG = DSA_HEADS // DSA_KV_HEADS
IDX_HEADS = 16
IDX_DIM = 64
IDX_ROPE = 32
DSA_TOPK_MAX = 256
REL_BUCKETS = 32
REL_MAX_DIST = 128
CONV_WIDTH = 3
ROPE_THETA = 10000.0
EPS = 1e-6
NEG = -1e30
FORCE = 1e9

LANE = 128
SUBLANE = 8
QB = 128
VMEM_LIMIT = 56 * 1024 * 1024

F32 = jnp.float32
BF16 = jnp.bfloat16


def _t5_thresholds():
    d = np.arange(0, 4 * REL_MAX_DIST)
    half = REL_BUCKETS // 2
    val = np.log(np.maximum(d, 1) / half) / math.log(REL_MAX_DIST / half) * (REL_BUCKETS - half)
    large = np.minimum(half + np.floor(np.maximum(val, 0.0)).astype(np.int64), REL_BUCKETS - 1)
    bucket = np.where(d < half, d, large)
    return [int(np.argmax(bucket >= b)) for b in range(1, REL_BUCKETS)]


T5_THR = _t5_thresholds()
T5_FAR = T5_THR[-1]
assert T5_FAR <= LANE


def _cparams(sem):
    return pltpu.CompilerParams(dimension_semantics=sem, vmem_limit_bytes=VMEM_LIMIT)


def _dot(a, b):
    return jnp.dot(a, b, preferred_element_type=F32)


def _ada_kernel(ct_ref, w_ref, b_ref, o_ref, *, bsz):
    ct = ct_ref[...]
    a = ct * jax.nn.sigmoid(ct)
    w = w_ref[0]
    rows = [jnp.sum(a[:, b:b + 1] * w, axis=0, keepdims=True) for b in range(bsz)]
    rows.append(jnp.zeros((o_ref.shape[1] - bsz, w.shape[1]), F32))
    o_ref[0] = jnp.concatenate(rows, axis=0) + b_ref[0]


def ada_all(c, ada_w, ada_b):
    depth, two, d, n3 = ada_w.shape
    bsz = c.shape[0]
    rows = -(-bsz // SUBLANE) * SUBLANE
    assert bsz <= LANE
    ct = jnp.zeros((d, LANE), F32).at[:, :bsz].set(c.T)
    w = ada_w.reshape(depth * two, d, n3)
    b = ada_b.reshape(depth * two, 1, n3)
    tn = 512
    out = pl.pallas_call(
        functools.partial(_ada_kernel, bsz=bsz),
        grid=(depth * two, n3 // tn),
        in_specs=[pl.BlockSpec((d, LANE), lambda l, j: (0, 0)),
                  pl.BlockSpec((1, d, tn), lambda l, j: (l, 0, j)),
                  pl.BlockSpec((1, 1, tn), lambda l, j: (l, 0, j))],
        out_specs=pl.BlockSpec((1, rows, tn), lambda l, j: (l, 0, j)),
        out_shape=jax.ShapeDtypeStruct((depth * two, rows, n3), F32),
        compiler_params=_cparams(("arbitrary", "arbitrary")),
        name="ada_mod",
    )(ct, w, b)
    return out[:, :bsz].reshape(depth, two, bsz, n3)


def _modnorm_kernel(x_ref, g_ref, sc_ref, sh_ref, o_ref):
    x = x_ref[...]
    y = x * lax.rsqrt(jnp.mean(x * x, axis=-1, keepdims=True) + EPS)
    h = (y * g_ref[...]) * (1.0 + sc_ref[0]) + sh_ref[0]
    o_ref[...] = h.astype(o_ref.dtype)


def modnorm(x2, g, scale, shift, seq, tm=1024):
    m, d = x2.shape
    tpb = seq // tm
    return pl.pallas_call(
        _modnorm_kernel,
        grid=(m // tm,),
        in_specs=[pl.BlockSpec((tm, d), lambda i: (i, 0)),
                  pl.BlockSpec((1, d), lambda i: (0, 0)),
                  pl.BlockSpec((1, 1, d), lambda i: (i // tpb, 0, 0)),
                  pl.BlockSpec((1, 1, d), lambda i: (i // tpb, 0, 0))],
        out_specs=pl.BlockSpec((tm, d), lambda i: (i, 0)),
        out_shape=jax.ShapeDtypeStruct((m, d), BF16),
        compiler_params=_cparams(("arbitrary",)),
        name="modnorm",
    )(x2, g.reshape(1, d), scale.reshape(-1, 1, d), shift.reshape(-1, 1, d))


def _head_norm(y, g_ref):
    return y * lax.rsqrt(jnp.mean(y * y, axis=-1, keepdims=True) + EPS) * g_ref[...]


def _qkv_heads(x, wq_ref, wk_ref, wv_ref, gq_ref, gk_ref, oq_ref, ok_ref, ov_ref):
    acc = _dot(x, wq_ref[...])
    for s in range(oq_ref.shape[0]):
        oq_ref[s] = _head_norm(acc[:, s * LANE:(s + 1) * LANE], gq_ref).T.astype(oq_ref.dtype)
    acc = _dot(x, wk_ref[...])
    for s in range(ok_ref.shape[0]):
        ok_ref[s] = _head_norm(acc[:, s * LANE:(s + 1) * LANE], gk_ref).astype(ok_ref.dtype)
    acc = _dot(x, wv_ref[...])
    for s in range(ov_ref.shape[0]):
        ov_ref[s] = acc[:, s * LANE:(s + 1) * LANE].T.astype(ov_ref.dtype)


def _proj_qkvi_kernel(x_ref, wq_ref, wk_ref, wv_ref, wi_ref, gq_ref, gk_ref, c_ref, sa_ref, sb_ref,
                      oq_ref, ok_ref, ov_ref, iqt_ref, ik_ref, iwt_ref):
    x = x_ref[...]
    _qkv_heads(x, wq_ref, wk_ref, wv_ref, gq_ref, gk_ref, oq_ref, ok_ref, ov_ref)
    acc = _dot(x, wi_ref[...])
    _indexer_layouts(lambda s, rows: acc[rows, s * LANE:(s + 1) * LANE], c_ref, sa_ref, sb_ref,
                     iqt_ref, ik_ref, iwt_ref)


def _proj_specs(x, ws, tm):
    m, k = x.shape
    whole = lambda w: pl.BlockSpec(w.shape, lambda i: (0, 0), pipeline_mode=pl.Buffered(1))
    gain = pl.BlockSpec((1, LANE), lambda i: (0, 0))
    nq, nk, nv = (w.shape[1] // LANE for w in ws[:3])
    in_specs = [pl.BlockSpec((tm, k), lambda i: (i, 0))] + [whole(w) for w in ws] + [gain, gain]
    out_specs = [pl.BlockSpec((nq, LANE, tm), lambda i: (0, 0, i)),
                 pl.BlockSpec((nk, tm, LANE), lambda i: (0, i, 0)),
                 pl.BlockSpec((nv, LANE, tm), lambda i: (0, 0, i))]
    out_shape = [jax.ShapeDtypeStruct((nq, LANE, m), BF16),
                 jax.ShapeDtypeStruct((nk, m, LANE), BF16),
                 jax.ShapeDtypeStruct((nv, LANE, m), BF16)]
    return in_specs, out_specs, out_shape


def proj_qkv_indexer(x, wq, g_q, wk, g_k, wv, wi, seq, tm=512):
    m = x.shape[0]
    ntile = tm // QB
    tps = seq // tm
    cos, sin = _rope_tables(seq, IDX_ROPE)
    zero = jnp.zeros_like(sin)
    rest = IDX_DIM - IDX_ROPE
    per = LANE // IDX_DIM
    c_tab = jnp.tile(jnp.concatenate([cos, cos, jnp.ones((seq, rest), F32)], axis=1), (1, per))
    sa_tab = jnp.tile(jnp.concatenate([-sin, zero, jnp.zeros((seq, rest), F32)], axis=1), (1, per))
    sb_tab = jnp.tile(jnp.concatenate([zero, sin, jnp.zeros((seq, rest), F32)], axis=1), (1, per))
    lanes = IDX_HEADS * QB
    tab_spec = pl.BlockSpec((tm, LANE), lambda i: (i % tps, 0))
    in_specs, out_specs, out_shape = _proj_specs(x, (wq, wk, wv, wi), tm)
    return pl.pallas_call(
        _proj_qkvi_kernel,
        grid=(m // tm,),
        in_specs=in_specs + [tab_spec, tab_spec, tab_spec],
        out_specs=out_specs + [pl.BlockSpec((ntile, LANE, lanes), lambda i: (i, 0, 0)),
                               pl.BlockSpec((tm, LANE), lambda i: (i, 0)),
                               pl.BlockSpec((ntile, 1, lanes), lambda i: (i, 0, 0))],
        out_shape=out_shape + [jax.ShapeDtypeStruct((m // QB, LANE, lanes), BF16),
                               jax.ShapeDtypeStruct((m, LANE), BF16),
                               jax.ShapeDtypeStruct((m // QB, 1, lanes), F32)],
        compiler_params=_cparams(("arbitrary",)),
        name="proj_qkv_indexer",
    )(x, wq, wk, wv, wi, g_q.reshape(1, LANE), g_k.reshape(1, LANE), c_tab, sa_tab, sb_tab)


def _rms_rows(x, g):
    return x * lax.rsqrt(jnp.mean(x * x, axis=-1, keepdims=True) + EPS) * g


def _rope_rows(x, cos, sin):
    half = x.shape[-1] // 2
    x1, x2 = x[:, :half], x[:, half:]
    return jnp.concatenate([x1 * cos - x2 * sin, x1 * sin + x2 * cos], axis=1)


def _mla_q_heads(acc, shift_ref, gn_ref, gr2_ref, c_ref, sa_ref, sb_ref, o_ref, scale):
    tm = acc.shape[0]
    for h in range(MLA_HEADS):
        nope = _rms_rows(acc[:, h * MLA_NOPE:(h + 1) * MLA_NOPE], gn_ref[...]) * scale
        o_ref[h, 0:MLA_NOPE, :] = nope.T.astype(o_ref.dtype)
    first = lax.broadcasted_iota(jnp.int32, (LANE - MLA_ROPE, tm), 0) == 0
    pad_rows = jnp.where(first, -shift_ref[0], 0.0)
    low = lax.broadcasted_iota(jnp.int32, (tm, LANE), 1) < MLA_ROPE
    c, sa, sb = c_ref[...], sa_ref[...], sb_ref[...]
    half = MLA_ROPE // 2
    per = LANE // MLA_ROPE
    for s in range(MLA_HEADS // per):
        x = acc[:, MLA_HEADS * MLA_NOPE + s * LANE:MLA_HEADS * MLA_NOPE + (s + 1) * LANE]
        sq = x * x
        s_low = jnp.sum(jnp.where(low, sq, 0.0), axis=-1, keepdims=True)
        s_all = jnp.sum(sq, axis=-1, keepdims=True)
        inv = jnp.where(low, lax.rsqrt(s_low / MLA_ROPE + EPS), lax.rsqrt((s_all - s_low) / MLA_ROPE + EPS))
        y = x * inv * gr2_ref[...]
        roped = y * c + pltpu.roll(y, LANE - half, axis=1) * sa + pltpu.roll(y, half, axis=1) * sb
        x_t = (roped * scale).T
        for j in range(per):
            o_ref[per * s + j, MLA_NOPE:MLA_NOPE + LANE, :] = jnp.concatenate(
                [x_t[j * MLA_ROPE:(j + 1) * MLA_ROPE], pad_rows], axis=0).astype(o_ref.dtype)


def _mla_kv_heads(acc, tail, gn_ref, gr_ref, cos_ref, sin_ref, ok_ref, ov_ref):
    tm = acc.shape[0]
    k_pe = _rope_rows(_rms_rows(tail[:, :MLA_ROPE], gr_ref[...]), cos_ref[...], sin_ref[...])
    first = lax.broadcasted_iota(jnp.int32, (tm, LANE - MLA_ROPE), 1) == 0
    k_pe = jnp.concatenate([k_pe, jnp.where(first, 1.0, 0.0)], axis=1).astype(ok_ref.dtype)
    for h in range(MLA_HEADS):
        c0 = h * (MLA_NOPE + MLA_V)
        ok_ref[h, :, 0:MLA_NOPE] = _rms_rows(acc[:, c0:c0 + MLA_NOPE], gn_ref[...]).astype(ok_ref.dtype)
        ok_ref[h, :, MLA_NOPE:MLA_NOPE + LANE] = k_pe
        ov_ref[h] = acc[:, c0 + MLA_NOPE:c0 + MLA_NOPE + MLA_V].T.astype(ov_ref.dtype)


def _proj_even_kernel(shift_ref, x_ref, wq_ref, wk_ref, wv_ref, wr_ref, gq_ref, gk_ref, glq_ref, glkv_ref,
                      wuq_ref, wukv_ref, gnq_ref, grq2_ref, gnk_ref, grk_ref, c_ref, sa_ref, sb_ref,
                      cos_ref, sin_ref, oq_ref, ok_ref, ov_ref, or_ref, oqm_ref, okm_ref, ovm_ref, *, scale):
    x = x_ref[...]
    _qkv_heads(x, wq_ref, wk_ref, wv_ref, gq_ref, gk_ref, oq_ref, ok_ref, ov_ref)
    acc = _dot(x, wr_ref[...])
    nraw = or_ref.shape[0] - 1
    for s in range(nraw):
        or_ref[s] = acc[:, s * LANE:(s + 1) * LANE]
    c0 = nraw * LANE
    c1 = c0 + MLA_Q_RANK
    c2 = c1 + MLA_KV_RANK
    tail = acc[:, c2:c2 + LANE]
    or_ref[nraw] = tail
    lat_q = _rms_rows(acc[:, c0:c1], glq_ref[...]).astype(BF16)
    _mla_q_heads(_dot(lat_q, wuq_ref[...]), shift_ref, gnq_ref, grq2_ref, c_ref, sa_ref, sb_ref, oqm_ref, scale)
    lat_kv = _rms_rows(acc[:, c1:c2], glkv_ref[...]).astype(BF16)
    _mla_kv_heads(_dot(lat_kv, wukv_ref[...]), tail, gnk_ref, grk_ref, cos_ref, sin_ref, okm_ref, ovm_ref)


def proj_even(x, wq, g_q, wk, g_k, wv, wr, nraw, seq, q_norm_g, kv_norm_g, wq_r, w_ukv, nope_g, rope_g,
              scale, shift, tm=512):
    m = x.shape[0]
    tps = seq // tm
    dqk = MLA_NOPE + LANE
    half = MLA_ROPE // 2
    per = LANE // MLA_ROPE
    cos, sin = _rope_tables(seq, MLA_ROPE)
    zero = jnp.zeros_like(sin)
    c_tab = jnp.tile(jnp.concatenate([cos, cos], axis=1), (1, per))
    sa_tab = jnp.tile(jnp.concatenate([-sin, zero], axis=1), (1, per))
    sb_tab = jnp.tile(jnp.concatenate([zero, sin], axis=1), (1, per))
    tab_spec = pl.BlockSpec((tm, LANE), lambda i: (i % tps, 0))
    rope_spec = pl.BlockSpec((tm, half), lambda i: (i % tps, 0))
    row = lambda n: pl.BlockSpec((1, n), lambda i: (0, 0))
    whole = lambda w: pl.BlockSpec(w.shape, lambda i: (0, 0), pipeline_mode=pl.Buffered(1))
    in_specs, out_specs, out_shape = _proj_specs(x, (wq, wk, wv, wr), tm)
    in_specs = ([pl.BlockSpec(memory_space=pltpu.SMEM)] + in_specs
                + [row(MLA_Q_RANK), row(MLA_KV_RANK), whole(wq_r), whole(w_ukv),
                   row(MLA_NOPE), row(LANE), row(MLA_NOPE), row(MLA_ROPE),
                   tab_spec, tab_spec, tab_spec, rope_spec, rope_spec])
    out_specs += [pl.BlockSpec((nraw + 1, tm, LANE), lambda i: (0, i, 0)),
                  pl.BlockSpec((MLA_HEADS, dqk, tm), lambda i: (0, 0, i)),
                  pl.BlockSpec((MLA_HEADS, tm, dqk), lambda i: (0, i, 0)),
                  pl.BlockSpec((MLA_HEADS, MLA_V, tm), lambda i: (0, 0, i))]
    out_shape += [jax.ShapeDtypeStruct((nraw + 1, m, LANE), F32),
                  jax.ShapeDtypeStruct((MLA_HEADS, dqk, m), BF16),
                  jax.ShapeDtypeStruct((MLA_HEADS, m, dqk), BF16),
                  jax.ShapeDtypeStruct((MLA_HEADS, MLA_V, m), BF16)]
    return pl.pallas_call(
        functools.partial(_proj_even_kernel, scale=scale),
        grid=(m // tm,),
        in_specs=in_specs,
        out_specs=out_specs,
        out_shape=out_shape,
        compiler_params=_cparams(("arbitrary",)),
        name="proj_nsa_mla",
    )(jnp.reshape(shift, (1,)).astype(F32), x, wq, wk, wv, wr, g_q.reshape(1, LANE), g_k.reshape(1, LANE),
      q_norm_g.reshape(1, -1), kv_norm_g.reshape(1, -1), wq_r, w_ukv, nope_g[0].reshape(1, -1),
      jnp.tile(rope_g[0].reshape(1, -1), (1, per)), nope_g[1].reshape(1, -1), rope_g[1].reshape(1, -1),
      c_tab, sa_tab, sb_tab, cos, sin)


RES_COLS = 512


def _resproj_kernel(*refs, npair, fuse_norm):
    xres_ref, gate_ref = refs[2 * npair], refs[2 * npair + 1]
    outs = refs[2 * npair + 2 + (3 if fuse_norm else 0):]
    o_ref = outs[0]
    n = o_ref.shape[1]
    for c0 in range(0, n, RES_COLS):
        cols = slice(c0, c0 + RES_COLS)
        acc = _dot(refs[0][...], refs[1][:, cols])
        for p in range(1, npair):
            acc = acc + _dot(refs[2 * p][...], refs[2 * p + 1][:, cols])
        o_ref[:, cols] = xres_ref[:, cols] + gate_ref[0][:, cols] * acc
    if fuse_norm:
        g_ref, sc_ref, sh_ref = refs[2 * npair + 2:2 * npair + 5]
        x = o_ref[...]
        y = x * lax.rsqrt(jnp.mean(x * x, axis=-1, keepdims=True) + EPS)
        outs[1][...] = ((y * g_ref[...]) * (1.0 + sc_ref[0]) + sh_ref[0]).astype(outs[1].dtype)


def resproj(pairs, xres, gate, seq, next_norm=None, tm=512):
    m, n = xres.shape
    tpb = seq // tm
    in_specs, args = [], []
    for x, w in pairs:
        k = x.shape[1]
        in_specs += [pl.BlockSpec((tm, k), lambda i: (i, 0)),
                     pl.BlockSpec((k, n), lambda i: (0, 0), pipeline_mode=pl.Buffered(1))]
        args += [x, w]
    per_batch = pl.BlockSpec((1, 1, n), lambda i: (i // tpb, 0, 0))
    in_specs += [pl.BlockSpec((tm, n), lambda i: (i, 0)), per_batch]
    args += [xres, gate.reshape(-1, 1, n)]
    out_specs = [pl.BlockSpec((tm, n), lambda i: (i, 0))]
    out_shape = [jax.ShapeDtypeStruct((m, n), F32)]
    if next_norm is not None:
        g, scale, shift = next_norm
        in_specs += [pl.BlockSpec((1, n), lambda i: (0, 0)), per_batch, per_batch]
        args += [g.reshape(1, n), scale.reshape(-1, 1, n), shift.reshape(-1, 1, n)]
        out_specs.append(pl.BlockSpec((tm, n), lambda i: (i, 0)))
        out_shape.append(jax.ShapeDtypeStruct((m, n), BF16))
    out = pl.pallas_call(
        functools.partial(_resproj_kernel, npair=len(pairs), fuse_norm=next_norm is not None),
        grid=(m // tm,),
        in_specs=in_specs,
        out_specs=out_specs,
        out_shape=out_shape,
        compiler_params=_cparams(("arbitrary",)),
        name="resproj",
    )(*args)
    return (out[0], out[1]) if next_norm is not None else (out[0], None)


HALO = 8


def _ffn_up_kernel(h_ref, wg32_ref, wv32_ref, cwg_ref, cwv_ref, cbg_ref, cbv_ref, o_ref,
                   ug_ref, uv_ref, wg_ref, wv_ref, *, tm, tiles_per_seq):
    i = pl.program_id(1)
    first = (i % tiles_per_seq) == 0

    @pl.when(i == 0)
    def _():
        wg_ref[...] = wg32_ref[...].astype(wg_ref.dtype)
        wv_ref[...] = wv32_ref[...].astype(wv_ref.dtype)

    @pl.when(first)
    def _():
        ug_ref[0:HALO, :] = jnp.zeros((HALO, ug_ref.shape[1]), F32)
        uv_ref[0:HALO, :] = jnp.zeros((HALO, uv_ref.shape[1]), F32)

    @pl.when(jnp.logical_not(first))
    def _():
        ug_ref[0:HALO, :] = ug_ref[tm:tm + HALO, :]
        uv_ref[0:HALO, :] = uv_ref[tm:tm + HALO, :]

    h = h_ref[...]
    ug_ref[HALO:HALO + tm, :] = _dot(h, wg_ref[...])
    uv_ref[HALO:HALO + tm, :] = _dot(h, wv_ref[...])

    def conv(u_ref, cw_ref, cb_ref):
        out = cb_ref[...]
        for j in range(CONV_WIDTH):
            off = HALO - (CONV_WIDTH - 1) + j
            out = out + cw_ref[j:j + 1, :] * u_ref[off:off + tm, :]
        return out

    g = conv(ug_ref, cwg_ref, cbg_ref)
    v = conv(uv_ref, cwv_ref, cbv_ref)
    hg = 0.5 * g
    o_ref[...] = ((hg + hg * jnp.tanh(hg)) * v).astype(o_ref.dtype)


def ffn_up(h, w_up_all, layer, conv_w, conv_b, seq, tm=1024, tn=512):
    m, d = h.shape
    f = w_up_all.shape[2] // 2
    nj = f // tn
    tps = seq // tm
    cb = conv_b.reshape(1, 2 * f)
    return pl.pallas_call(
        functools.partial(_ffn_up_kernel, tm=tm, tiles_per_seq=tps),
        grid=(nj, m // tm),
        in_specs=[pl.BlockSpec((tm, d), lambda j, i: (i, 0)),
                  pl.BlockSpec((None, d, tn), lambda j, i: (layer, 0, j)),
                  pl.BlockSpec((None, d, tn), lambda j, i: (layer, 0, nj + j)),
                  pl.BlockSpec((CONV_WIDTH, tn), lambda j, i: (0, j)),
                  pl.BlockSpec((CONV_WIDTH, tn), lambda j, i: (0, nj + j)),
                  pl.BlockSpec((1, tn), lambda j, i: (0, j)),
                  pl.BlockSpec((1, tn), lambda j, i: (0, nj + j))],
        out_specs=pl.BlockSpec((tm, tn), lambda j, i: (i, j)),
        out_shape=jax.ShapeDtypeStruct((m, f), BF16),
        scratch_shapes=[pltpu.VMEM((tm + HALO, tn), F32), pltpu.VMEM((tm + HALO, tn), F32),
                        pltpu.VMEM((d, tn), BF16), pltpu.VMEM((d, tn), BF16)],
        compiler_params=_cparams(("arbitrary", "arbitrary")),
        name="ffn_up_conv",
    )(h, w_up_all, w_up_all, conv_w, conv_w, cb, cb)


LOG2E = 1.4426950408889634
CWIN = 16


def _t5_shifted(dist, tbl_ref, h):
    val = jnp.full(dist.shape, tbl_ref[0, h], F32)
    for b in range(1, REL_BUCKETS):
        val = jnp.where(dist >= T5_THR[b - 1], tbl_ref[b, h], val)
    return (val - tbl_ref[REL_BUCKETS - 1, h]) * LOG2E


def _bias_tiles_kernel(tbl_ref, dt_ref, dc_ref):
    h = pl.program_id(0)
    key = lax.broadcasted_iota(jnp.int32, (LANE, LANE), 0)
    q = lax.broadcasted_iota(jnp.int32, (LANE, LANE), 1)
    for rel in range(2):
        dt_ref[0, rel] = _t5_shifted(rel * LANE + q - key, tbl_ref, h)
    dt_ref[0, 2] = jnp.zeros((LANE, LANE), F32)
    u = lax.broadcasted_iota(jnp.int32, (CWIN, LANE), 0)
    qc = lax.broadcasted_iota(jnp.int32, (CWIN, LANE), 1)
    dc_ref[0] = _t5_shifted(qc - CMP_STRIDE * (u - CWIN // 2) - (CMP_BLOCK - 1), tbl_ref, h)


def bias_tiles(rel_bias):
    nh = rel_bias.shape[1]
    return pl.pallas_call(
        _bias_tiles_kernel,
        grid=(nh,),
        in_specs=[pl.BlockSpec(memory_space=pltpu.SMEM)],
        out_specs=[pl.BlockSpec((1, 3, LANE, LANE), lambda h: (h, 0, 0, 0)),
                   pl.BlockSpec((1, CWIN, LANE), lambda h: (h, 0, 0))],
        out_shape=[jax.ShapeDtypeStruct((nh, 3, LANE, LANE), F32),
                   jax.ShapeDtypeStruct((nh, CWIN, LANE), F32)],
        compiler_params=_cparams(("arbitrary",)),
        name="t5_bias_tiles",
    )(rel_bias)


def _compress_kernel(x_ref, pe_ref, w1_ref, b1_ref, w2_ref, b2_ref, g_ref, o_ref, *, half):
    kv = pl.program_id(0)
    nchunk = x_ref.shape[1] // CMP_STRIDE
    a = jnp.zeros((nchunk, CMP_HIDDEN), F32)
    b = jnp.zeros((nchunk, CMP_HIDDEN), F32)
    for p in range(CMP_STRIDE):
        xp = x_ref[0, pl.ds(p, nchunk, stride=CMP_STRIDE), :]
        rows = slice(p * HEAD_DIM, (p + 1) * HEAD_DIM)
        a = a + _dot((xp + pe_ref[0, p:p + 1, :]).astype(BF16), w1_ref[0, rows, :])
        q = CMP_STRIDE + p
        b = b + _dot((xp + pe_ref[0, q:q + 1, :]).astype(BF16),
                     w1_ref[0, half + p * HEAD_DIM:half + (p + 1) * HEAD_DIM, :])
    b_next = jnp.concatenate([b[1:], jnp.zeros((1, b.shape[1]), F32)], axis=0)
    hid = jax.nn.gelu(a + b_next + b1_ref[0])
    out = _dot(hid.astype(BF16), w2_ref[0]) + b2_ref[0]
    normed = out * lax.rsqrt(jnp.mean(out * out, axis=-1, keepdims=True) + EPS) * g_ref[...]
    out = jnp.where(kv == 0, normed, out)
    o_ref[0, 0] = out.astype(o_ref.dtype)


def compress_kv(proj, slab0, bsz, seq, cmp_pe, cmp_w1, cmp_b1, cmp_w2, cmp_b2, g_k):
    nslab, m, _ = proj.shape
    nchunk = seq // CMP_STRIDE
    half = CMP_STRIDE * HEAD_DIM
    del nslab, m
    return pl.pallas_call(
        functools.partial(_compress_kernel, half=half),
        grid=(2, bsz, NSA_GROUPS),
        in_specs=[pl.BlockSpec((1, seq, HEAD_DIM), lambda kv, b, g: (slab0 + 2 * kv + g, b, 0)),
                  pl.BlockSpec((1, CMP_BLOCK, HEAD_DIM), lambda kv, b, g: (kv, 0, 0)),
                  pl.BlockSpec((1, 2 * half, CMP_HIDDEN), lambda kv, b, g: (kv, 0, 0)),
                  pl.BlockSpec((1, 1, CMP_HIDDEN), lambda kv, b, g: (kv, 0, 0)),
                  pl.BlockSpec((1, CMP_HIDDEN, HEAD_DIM), lambda kv, b, g: (kv, 0, 0)),
                  pl.BlockSpec((1, 1, HEAD_DIM), lambda kv, b, g: (kv, 0, 0)),
                  pl.BlockSpec((1, HEAD_DIM), lambda kv, b, g: (0, 0))],
        out_specs=pl.BlockSpec((1, 1, nchunk, HEAD_DIM), lambda kv, b, g: (kv, g, b, 0)),
        out_shape=jax.ShapeDtypeStruct((2, NSA_GROUPS, bsz * nchunk, HEAD_DIM), BF16),
        compiler_params=_cparams(("arbitrary", "arbitrary", "arbitrary")),
        name="nsa_compress",
    )(proj, cmp_pe, cmp_w1.astype(BF16), cmp_b1.reshape(2, 1, CMP_HIDDEN), cmp_w2.astype(BF16),
      cmp_b2.reshape(2, 1, HEAD_DIM), g_k.reshape(1, HEAD_DIM))


KW = 512
PV_KEYS = 512


def _tile_lanes(x, n):
    return jnp.concatenate([x] * n, axis=1)


def _flash_init(m_ref, l_ref, acc_ref):
    m_ref[...] = jnp.full(m_ref.shape, NEG, F32)
    l_ref[...] = jnp.zeros(l_ref.shape, F32)
    acc_ref[...] = jnp.zeros(acc_ref.shape, F32)


def _zero_after(x):
    bits = pltpu.bitcast(x, jnp.int32)
    return lax.shift_right_logical(lax.shift_right_logical(bits, 16), 16).astype(F32)


def _flash_update(s, v_t, m_ref, l_ref, acc_ref, col_max=None, after=None):
    m_old = m_ref[...]
    if col_max is None:
        col_max = jnp.max(s, axis=0, keepdims=True)
    m_new = jnp.maximum(m_old, col_max)
    alpha = jnp.exp2(m_old - m_new)
    l_new = alpha * l_ref[...]
    acc = alpha * acc_ref[...]
    nk = s.shape[0]
    for k0 in range(0, nk, PV_KEYS):
        p = jnp.exp2(s[k0:k0 + PV_KEYS] - m_new)
        l_new = l_new + jnp.sum(p, axis=0, keepdims=True)
        acc = acc + _dot(v_t[:, k0:k0 + PV_KEYS], p.astype(BF16))
    l_ref[...] = l_new
    acc_ref[...] = acc
    m_ref[...] = m_new if after is None else m_new + _zero_after(after)


SAFE_LOG2_BOUND = 60.0


def _flash_accumulate(s, v_t, l_ref, acc_ref, after=None):
    l_new = l_ref[...]
    acc = acc_ref[...]
    for k0 in range(0, s.shape[0], PV_KEYS):
        p = jnp.exp2(s[k0:k0 + PV_KEYS])
        l_new = l_new + jnp.sum(p, axis=0, keepdims=True)
        acc = acc + _dot(v_t[:, k0:k0 + PV_KEYS], p.astype(BF16))
    if after is not None:
        l_new = l_new + jnp.max(_zero_after(after), axis=0, keepdims=True)
    l_ref[...] = l_new
    acc_ref[...] = acc


def _sum_result(l_ref, acc_ref):
    den = l_ref[...]
    ok = den > 0.0
    return acc_ref[...] * jnp.where(ok, 1.0 / jnp.where(ok, den, 1.0), 0.0)


def _inv_den(m, den):
    ok = m > 0.5 * NEG
    return jnp.where(ok, 1.0 / jnp.where(ok, den, 1.0), 0.0)


def _flash_result(m_ref, l_ref, acc_ref):
    return acc_ref[...] * _inv_den(m_ref[...], l_ref[...])


def _softmax_cols(s):
    m = jnp.max(s, axis=0, keepdims=True)
    p = jnp.exp2(s - m)
    return p * _inv_den(m, jnp.sum(p, axis=0, keepdims=True))


def _near_bias(dt_ref, heads, qi, kt0, ntile):
    rows = []
    for j in range(ntile):
        rel = jnp.clip(qi - (kt0 + j), 0, 2)
        rows.append(jnp.concatenate([dt_ref[h, rel] for h in heads], axis=1))
    return jnp.concatenate(rows, axis=0)


def _pipelined_chunks(n, qk_stage, soft_stage):
    @pl.when(n > 0)
    def _():
        qk_stage(0, 0)

    def pair(p, x):
        c = 2 * p
        ahead = qk_stage(c + 1, 1)
        soft_stage(c, 0, ahead)
        ahead = qk_stage(jnp.minimum(c + 2, n - 1), 0)
        soft_stage(c + 1, 1, ahead)
        return x

    lax.fori_loop(0, n // 2, pair, 0)

    @pl.when(n % 2 == 1)
    def _():
        soft_stage(n - 1, 0, None)


NSA_STATE = 9
GATE_ROWS = -(-3 * NSA_HPG // SUBLANE) * SUBLANE


def _nsa_kernel(bound_ref, qt_ref, gt_ref, kc_ref, vct_ref, ks_ref, vst_ref, kw_ref, vwt_ref,
                dt_ref, dc_ref, ext_ref, o_ref, *scratch, seq, nc):
    ng = NSA_GROUPS
    state = [scratch[NSA_STATE * g:NSA_STATE * (g + 1)] for g in range(ng)]
    qi = pl.program_id(1)
    q0 = qi * QB
    hpg = NSA_HPG
    ncp = kc_ref.shape[1]
    ns = seq // SEL_BLOCK
    group_heads = [[g * hpg + h for h in range(hpg)] for g in range(ng)]
    q_ts = [jnp.concatenate([qt_ref[h] for h in group_heads[g]], axis=1) for g in range(ng)]
    pad = CWIN // 2
    wkeys = WINDOW + QB
    start = pl.multiple_of(jnp.maximum(q0 - WINDOW, 0), LANE)
    r0 = pl.multiple_of(qi * (QB // CMP_STRIDE), 8)

    s_w = []
    for g in range(ng):
        sc_ref = state[g][0]
        sc_ref[0:pad, :] = jnp.zeros((pad, hpg * QB), F32)
        sc_ref[pad + ncp:2 * pad + ncp, :] = jnp.zeros((pad, hpg * QB), F32)
        sc_ref[pad:pad + ncp, :] = _dot(kc_ref[g], q_ts[g])
        s_w.append(_dot(kw_ref[g, pl.ds(start, wkeys), :], q_ts[g]))

    ci = lax.broadcasted_iota(jnp.int32, (ncp, QB), 0)
    tc = q0 + lax.broadcasted_iota(jnp.int32, (ncp, QB), 1)
    valid_c = (ci * CMP_STRIDE + CMP_BLOCK - 1 <= tc) & (ci < nc)
    madd_c = _tile_lanes(jnp.where(valid_c, 0.0, NEG), hpg)
    oc_t, p_sum = [], []
    for g in range(ng):
        sc_ref = state[g][0]
        sc_ref[pl.ds(r0, CWIN), :] = sc_ref[pl.ds(r0, CWIN), :] + jnp.concatenate(
            [dc_ref[h] for h in group_heads[g]], axis=1)
        p_c = _softmax_cols(sc_ref[pad:pad + ncp, :] + madd_c)
        oc_t.append(_dot(vct_ref[g], p_c.astype(BF16)))
        ps = p_c[:, 0:QB]
        for h in range(1, hpg):
            ps = ps + p_c[:, h * QB:(h + 1) * QB]
        p_sum.append(ps)

    dist_w = (q0 + lax.broadcasted_iota(jnp.int32, (wkeys, QB), 1)) - (
        start + lax.broadcasted_iota(jnp.int32, (wkeys, QB), 0))
    madd_w = _tile_lanes(jnp.where((dist_w >= 0) & (dist_w < WINDOW), 0.0, NEG), hpg)
    ow_t = []
    for g in range(ng):
        p_w = _softmax_cols(s_w[g] + _near_bias(dt_ref, group_heads[g], qi, start // LANE, wkeys // LANE) + madd_w)
        ow_t.append(_dot(vwt_ref[g, :, pl.ds(start, wkeys)], p_w.astype(BF16)))

    per = SEL_BLOCK // CMP_STRIDE
    blk = lax.broadcasted_iota(jnp.int32, (LANE, QB), 0)
    t = q0 + lax.broadcasted_iota(jnp.int32, (LANE, QB), 1)
    tb = t // SEL_BLOCK
    forced = (blk == 0) | (blk == tb) | (blk == tb - 1)
    blk_f = blk.astype(F32)
    scores = []
    for g in range(ng):
        ps_ref = state[g][1]
        ps_ref[0:SUBLANE, :] = jnp.zeros((SUBLANE, QB), F32)
        ps_ref[SUBLANE:SUBLANE + ncp, :] = p_sum[g]
        band = [ps_ref[pl.ds(SUBLANE + r, ns, stride=per), :] for r in range(-1, per)]
        imp = 0.5 * band[0] + band[1] + band[2] + band[3] + 0.5 * band[4]
        if ns < LANE:
            imp = jnp.concatenate([imp, jnp.zeros((LANE - ns, QB), F32)], axis=0)
        score = jnp.where(forced, FORCE, jnp.where(blk * SEL_BLOCK <= t, imp, NEG))
        scores.append(jnp.where(blk < ns, score, -jnp.inf))
    sels = [jnp.zeros((LANE, QB), F32) for _ in range(ng)]
    for _ in range(min(SEL_TOP_N, ns)):
        for g in range(ng):
            mx = jnp.max(scores[g], axis=0, keepdims=True)
            first = jnp.min(jnp.where(scores[g] == mx, blk_f, float(LANE)), axis=0, keepdims=True)
            pick = blk_f == first
            sels[g] = jnp.where(pick, 1.0, sels[g])
            scores[g] = jnp.where(pick, -jnp.inf, scores[g])
    sel_b = [s.astype(BF16) for s in sels]

    kpos = lax.broadcasted_iota(jnp.int32, (KW, QB), 0)
    tq = q0 + lax.broadcasted_iota(jnp.int32, (KW, QB), 1)
    bounded_ok = bound_ref[0] <= SAFE_LOG2_BOUND
    shift = jnp.where(bounded_ok, bound_ref[0], 0.0)
    c_near = jnp.maximum(qi - 1, 0) // (KW // LANE)

    def scores_of(g, c0):
        chosen = _dot(ext_ref[pl.ds(c0, KW), :], sel_b[g])
        return (chosen - 1.0) * (-NEG) - shift, _dot(ks_ref[g, pl.ds(c0, KW), :], q_ts[g])

    def attend(bounded):
        for g in range(ng):
            _flash_init(*state[g][2:5])

        def qk_stage(c, buf):
            c0 = pl.multiple_of(c * KW, KW)
            ahead = []
            for g in range(ng):
                madd, s = scores_of(g, c0)
                s = s + _tile_lanes(madd, hpg)
                state[g][5 + buf][...] = s
                if bounded:
                    ahead.append(s[KW - 8:KW])
                else:
                    ahead.append(jnp.max(s, axis=0, keepdims=True))
                    state[g][7 + buf][...] = ahead[-1]
            return ahead

        def soft_stage(c, buf, ahead):
            c0 = pl.multiple_of(c * KW, KW)
            for g in range(ng):
                m_ref, l_ref, acc_ref = state[g][2:5]
                after = None if ahead is None else ahead[g]
                v_t = vst_ref[g, :, pl.ds(c0, KW)]
                if bounded:
                    _flash_accumulate(state[g][5 + buf][...], v_t, l_ref, acc_ref, after=after)
                else:
                    _flash_update(state[g][5 + buf][...], v_t, m_ref, l_ref, acc_ref,
                                  col_max=state[g][7 + buf][...], after=after)

        def near_step(c, x):
            c0 = pl.multiple_of(c * KW, KW)
            causal = jnp.where(c0 + kpos <= tq, 0.0, NEG)
            scores = []
            for g in range(ng):
                madd, s = scores_of(g, c0)
                scores.append(s + _tile_lanes(madd + causal, hpg)
                              + _near_bias(dt_ref, group_heads[g], qi, c * (KW // LANE), KW // LANE))
            for g in range(ng):
                m_ref, l_ref, acc_ref = state[g][2:5]
                if bounded:
                    _flash_accumulate(scores[g], vst_ref[g, :, pl.ds(c0, KW)], l_ref, acc_ref)
                else:
                    _flash_update(scores[g], vst_ref[g, :, pl.ds(c0, KW)], m_ref, l_ref, acc_ref)
            return x

        _pipelined_chunks(c_near, qk_stage, soft_stage)
        lax.fori_loop(c_near, qi // (KW // LANE) + 1, near_step, 0)
        for g in range(ng):
            m_ref, l_ref, acc_ref = state[g][2:5]
            acc_ref[...] = _sum_result(l_ref, acc_ref) if bounded else _flash_result(m_ref, l_ref, acc_ref)

    pl.when(bounded_ok)(lambda: attend(True))
    pl.when(jnp.logical_not(bounded_ok))(lambda: attend(False))

    for g in range(ng):
        os_t = state[g][4][...]
        gates = jax.nn.sigmoid(gt_ref[g])
        for h in range(hpg):
            sl = slice(h * QB, (h + 1) * QB)
            o_t = (gates[3 * h:3 * h + 1] * oc_t[g][:, sl] + gates[3 * h + 1:3 * h + 2] * os_t[:, sl]
                   + gates[3 * h + 2:3 * h + 3] * ow_t[g][:, sl])
            hh = group_heads[g][h]
            o_ref[:, hh * HEAD_DIM:(hh + 1) * HEAD_DIM] = o_t.T.astype(o_ref.dtype)


def nsa_attention(logit_bound, q_t, gates_t, kc, vc_t, k_sw, v_sw_t, dt, dc, bsz, seq):
    nq = seq // QB
    ncp = seq // CMP_STRIDE
    nc = ncp - 1
    ns = seq // SEL_BLOCK
    assert ns <= LANE and seq >= WINDOW + QB and seq % KW == 0
    assert CMP_BLOCK == 2 * CMP_STRIDE and SEL_BLOCK == 4 * CMP_STRIDE
    expand =((np.arange(seq)[:, None] // SEL_BLOCK) == np.arange(LANE)[None, :]).astype(np.float32)
    ng = NSA_GROUPS
    once = dict(pipeline_mode=pl.Buffered(1))
    ks_spec = pl.BlockSpec((ng, seq, HEAD_DIM), lambda b, i: (0, b, 0), **once)
    kw_spec = pl.BlockSpec((ng, seq, HEAD_DIM), lambda b, i: (1, b, 0), **once)
    vs_spec = pl.BlockSpec((ng, HEAD_DIM, seq), lambda b, i: (0, 0, b), **once)
    vw_spec = pl.BlockSpec((ng, HEAD_DIM, seq), lambda b, i: (1, 0, b), **once)
    lanes = NSA_HPG * QB
    group_state = [pltpu.VMEM((ncp + CWIN, lanes), F32), pltpu.VMEM((ncp + SUBLANE, QB), F32),
                   pltpu.VMEM((1, lanes), F32), pltpu.VMEM((1, lanes), F32), pltpu.VMEM((HEAD_DIM, lanes), F32),
                   pltpu.VMEM((KW, lanes), F32), pltpu.VMEM((KW, lanes), F32),
                   pltpu.VMEM((1, lanes), F32), pltpu.VMEM((1, lanes), F32)]
    assert len(group_state) == NSA_STATE
    return pl.pallas_call(
        functools.partial(_nsa_kernel, seq=seq, nc=nc),
        grid=(bsz, nq),
        in_specs=[pl.BlockSpec(memory_space=pltpu.SMEM),
                  pl.BlockSpec((NSA_HEADS, HEAD_DIM, QB), lambda b, i: (0, 0, b * nq + i)),
                  pl.BlockSpec((ng, GATE_ROWS, QB), lambda b, i: (0, 0, b * nq + i)),
                  pl.BlockSpec((ng, ncp, HEAD_DIM), lambda b, i: (0, b, 0)),
                  pl.BlockSpec((ng, HEAD_DIM, ncp), lambda b, i: (0, 0, b)),
                  ks_spec, vs_spec, kw_spec, vw_spec,
                  pl.BlockSpec((NSA_HEADS, 3, LANE, LANE), lambda b, i: (0, 0, 0, 0)),
                  pl.BlockSpec((NSA_HEADS, CWIN, LANE), lambda b, i: (0, 0, 0)),
                  pl.BlockSpec((seq, LANE), lambda b, i: (0, 0))],
        out_specs=pl.BlockSpec((QB, NSA_HEADS * HEAD_DIM), lambda b, i: (b * nq + i, 0)),
        out_shape=jax.ShapeDtypeStruct((bsz * seq, NSA_HEADS * HEAD_DIM), BF16),
        scratch_shapes=group_state * ng,
        compiler_params=_cparams(("arbitrary", "arbitrary")),
        name="nsa_attention",
    )(jnp.reshape(logit_bound, (1,)).astype(F32), q_t, gates_t, kc, vc_t, k_sw, v_sw_t, k_sw, v_sw_t, dt, dc,
      jnp.asarray(expand, BF16))


MLA_HPS = 2


def _mla_kernel(bound_ref, qt_ref, k_ref, vt_ref, o_ref, *scratch):
    qi = pl.program_id(2)
    chains = [scratch[3 * h:3 * h + 3] for h in range(MLA_HPS)]
    sbuf = [scratch[(3 + b) * MLA_HPS:(4 + b) * MLA_HPS] for b in range(2)]
    cbuf = [scratch[(5 + b) * MLA_HPS:(6 + b) * MLA_HPS] for b in range(2)]
    c_diag = pl.multiple_of(qi * KW, KW)
    kpos = lax.broadcasted_iota(jnp.int32, (KW, KW), 0)
    tq = lax.broadcasted_iota(jnp.int32, (KW, KW), 1)
    dv = vt_ref.shape[1]

    def attend(bounded):
        for ch in chains:
            _flash_init(*ch)

        def qk_stage(c, buf):
            c0 = pl.multiple_of(c * KW, KW)
            ahead = []
            for h in range(MLA_HPS):
                s = _dot(k_ref[h, pl.ds(c0, KW), :], qt_ref[h])
                sbuf[buf][h][...] = s
                if bounded:
                    ahead.append(s[KW - 8:KW])
                else:
                    ahead.append(jnp.max(s, axis=0, keepdims=True))
                    cbuf[buf][h][...] = ahead[-1]
            return ahead

        def soft_stage(c, buf, ahead):
            c0 = pl.multiple_of(c * KW, KW)
            for h, (m_ref, l_ref, acc_ref) in enumerate(chains):
                after = None if ahead is None else ahead[h]
                if bounded:
                    _flash_accumulate(sbuf[buf][h][...], vt_ref[h, :, pl.ds(c0, KW)], l_ref, acc_ref, after=after)
                else:
                    _flash_update(sbuf[buf][h][...], vt_ref[h, :, pl.ds(c0, KW)], m_ref, l_ref, acc_ref,
                                  col_max=cbuf[buf][h][...], after=after)

        _pipelined_chunks(qi, qk_stage, soft_stage)
        causal = jnp.where(kpos <= tq, 0.0, NEG)
        scores = [_dot(k_ref[h, pl.ds(c_diag, KW), :], qt_ref[h]) + causal for h in range(MLA_HPS)]
        for h, (m_ref, l_ref, acc_ref) in enumerate(chains):
            if bounded:
                _flash_accumulate(scores[h], vt_ref[h, :, pl.ds(c_diag, KW)], l_ref, acc_ref)
                o_t = _sum_result(l_ref, acc_ref)
            else:
                _flash_update(scores[h], vt_ref[h, :, pl.ds(c_diag, KW)], m_ref, l_ref, acc_ref)
                o_t = _flash_result(m_ref, l_ref, acc_ref)
            o_ref[:, h * dv:(h + 1) * dv] = o_t.T.astype(o_ref.dtype)

    bounded_ok = bound_ref[0] <= SAFE_LOG2_BOUND
    pl.when(bounded_ok)(lambda: attend(True))
    pl.when(jnp.logical_not(bounded_ok))(lambda: attend(False))


def mla_attention(logit_bound, q_t, k, v_t, bsz, seq):
    nh, dqk, _ = q_t.shape
    dv = v_t.shape[1]
    nq = seq // KW
    hps = MLA_HPS
    state = [pltpu.VMEM((1, KW), F32), pltpu.VMEM((1, KW), F32), pltpu.VMEM((dv, KW), F32)] * hps
    state += [pltpu.VMEM((KW, KW), F32)] * (2 * hps)
    state += [pltpu.VMEM((1, KW), F32)] * (2 * hps)
    return pl.pallas_call(
        _mla_kernel,
        grid=(bsz, nh // hps, nq),
        in_specs=[pl.BlockSpec(memory_space=pltpu.SMEM),
                  pl.BlockSpec((hps, dqk, KW), lambda b, h, i: (h, 0, b * nq + i)),
                  pl.BlockSpec((hps, seq, dqk), lambda b, h, i: (h, b, 0)),
                  pl.BlockSpec((hps, dv, seq), lambda b, h, i: (h, 0, b))],
        out_specs=pl.BlockSpec((KW, hps * dv), lambda b, h, i: (b * nq + i, h)),
        out_shape=jax.ShapeDtypeStruct((bsz * seq, nh * dv), BF16),
        scratch_shapes=state,
        compiler_params=_cparams(("arbitrary", "arbitrary", "arbitrary")),
        name="mla_attention",
    )(jnp.reshape(logit_bound, (1,)).astype(F32), q_t, k, v_t)


INT_MIN = -2 ** 31
NEG_KEY = int(np.array(NEG, np.float32).view(np.int32)) ^ 0x7FFFFFFF
KEY_BITS = 32
SURE_BITS = 22


def _sort_key(x):
    bits = pltpu.bitcast(x + 0.0, jnp.int32)
    return jnp.where(bits < 0, bits ^ 0x7FFFFFFF, bits)


def _dsa_kernel(bound_ref, iqt_ref, iwt_ref, ik_ref, qt_ref, k_ref, vt_ref, dt_ref, o_ref,
                key_ref, *state, seq, k_sel):
    qi = pl.program_id(1)
    q0 = qi * QB
    n_chunk = (q0 + QB + KW - 1) // KW
    n_rest = seq - n_chunk * KW
    kpos = lax.broadcasted_iota(jnp.int32, (KW, QB), 0)
    tq = q0 + lax.broadcasted_iota(jnp.int32, (KW, QB), 1)
    hpp = KW // QB

    def score_chunk(c):
        c0 = pl.multiple_of(c * KW, KW)
        ikc = ik_ref[pl.ds(c0, KW), :]
        acc = jnp.zeros((KW, QB), F32)
        for piece in range(IDX_HEADS // hpp):
            sl = slice(piece * KW, (piece + 1) * KW)
            s = jnp.maximum(_dot(ikc, iqt_ref[0, :, sl]), 0.0) * iwt_ref[0, :, sl]
            for j in range(hpp):
                acc = acc + s[:, j * QB:(j + 1) * QB]
        acc = jnp.where(c0 + kpos <= tq, acc, NEG)
        key_ref[pl.ds(c0, KW), :] = _sort_key(acc)

    def score_pair(j, x):
        score_chunk(2 * j)
        score_chunk(jnp.minimum(2 * j + 1, n_chunk - 1))
        return x

    lax.fori_loop(0, (n_chunk + 1) // 2, score_pair, 0)

    def count(pred):
        def chunk_hits(c):
            c0 = pl.multiple_of(c * KW, KW)
            hit = jnp.where(pred(key_ref[pl.ds(c0, KW), :], c0), 1.0, 0.0)
            parts = [hit[SUBLANE * i:SUBLANE * (i + 1)] for i in range(KW // SUBLANE)]
            while len(parts) > 1:
                parts = [parts[i] + parts[i + 1] for i in range(0, len(parts), 2)]
            return parts[0]

        def body(j, acc):
            second = 2 * j + 1
            weight = jnp.where(second < n_chunk, 1.0, 0.0)
            return acc + chunk_hits(2 * j) + chunk_hits(jnp.minimum(second, n_chunk - 1)) * weight

        acc = lax.fori_loop(0, (n_chunk + 1) // 2, body, jnp.zeros((SUBLANE, QB), F32))
        return jnp.sum(acc, axis=0, keepdims=True)

    rest = n_rest.astype(F32)
    kf = float(k_sel)

    def bit_step(i, st):
        u, thr_s, settled = st
        bit = jnp.left_shift(jnp.int32(1), KEY_BITS - 1 - i)
        trial = (u | bit) ^ INT_MIN
        cnt = count(lambda keys, c0: keys >= trial) + jnp.where(NEG_KEY >= trial, rest, 0.0)
        new = (cnt == kf) & (settled < 0.5)
        return (jnp.where(cnt >= kf, u | bit, u), jnp.where(new, trial, thr_s), jnp.where(new, 1.0, settled))

    st = (jnp.zeros((1, QB), jnp.int32), jnp.zeros((1, QB), jnp.int32), jnp.zeros((1, QB), F32))
    st = lax.fori_loop(0, SURE_BITS, bit_step, st)
    _, (u, thr_s, settled) = lax.while_loop(
        lambda c: (c[0] < KEY_BITS) & (jnp.min(c[1][2]) < 0.5),
        lambda c: (c[0] + 1, bit_step(c[0], c[1])), (jnp.int32(SURE_BITS), st))
    is_settled = settled > 0.5
    thr = jnp.where(is_settled, thr_s, u ^ INT_MIN)

    def edge_counts():
        return (count(lambda keys, c0: keys > thr) + jnp.where(NEG_KEY > thr, rest, 0.0),
                count(lambda keys, c0: keys >= thr) + jnp.where(NEG_KEY >= thr, rest, 0.0))

    zero_cnt = jnp.zeros((1, QB), F32)
    cnt_gt, cnt_ge = lax.cond(jnp.min(settled) > 0.5, lambda: (zero_cnt, zero_cnt), edge_counts)
    need = kf - cnt_gt
    tie_q = (cnt_ge > kf) & (thr != NEG_KEY) & jnp.logical_not(is_settled)
    idx_bits = (seq - 1).bit_length()
    no_cut = 2 ** 30

    def tie_cut():
        def idx_step(i, x):
            bit = jnp.left_shift(jnp.int32(1), idx_bits - 1 - i)
            trial = x | bit
            f = count(lambda keys, c0: (keys == thr) & (c0 + kpos < trial))
            return jnp.where(f <= need - 1.0, trial, x)
        return lax.fori_loop(0, idx_bits, idx_step, jnp.zeros((1, QB), jnp.int32))

    any_tie = jnp.max(jnp.where(tie_q, 1.0, 0.0)) > 0.0
    x_cut = lax.cond(any_tie, tie_cut, lambda: jnp.full((1, QB), no_cut, jnp.int32))
    x_cut = jnp.where(tie_q, x_cut, no_cut)

    bounded_ok = bound_ref[0] <= SAFE_LOG2_BOUND
    shift = jnp.where(bounded_ok, bound_ref[0], 0.0)

    def mask_add(c0):
        keys = key_ref[pl.ds(c0, KW), :]
        pos = c0 + kpos
        chosen = (keys > thr) | ((keys == thr) & (pos <= x_cut))
        return _tile_lanes(jnp.where(chosen & (pos <= tq), -shift, NEG), DSA_HPG)

    c_near = jnp.maximum(qi - 1, 0) // (KW // LANE)
    ng = DSA_KV_HEADS
    chains = [state[3 * g:3 * g + 3] for g in range(ng)]
    sbuf = [state[(3 + b) * ng:(4 + b) * ng] for b in range(2)]
    cbuf = [state[(5 + b) * ng:(6 + b) * ng] for b in range(2)]
    group_heads = [[g * DSA_HPG + h for h in range(DSA_HPG)] for g in range(ng)]

    def raw_scores(c0, g):
        q_t = jnp.concatenate([qt_ref[h] for h in group_heads[g]], axis=1)
        return _dot(k_ref[g, pl.ds(c0, KW), :], q_t)

    def attend(bounded):
        for ch in chains:
            _flash_init(*ch)

        def qk_stage(c, buf):
            c0 = pl.multiple_of(c * KW, KW)
            madd = mask_add(c0)
            ahead = []
            for g in range(ng):
                s = raw_scores(c0, g) + madd
                sbuf[buf][g][...] = s
                if bounded:
                    ahead.append(s[KW - 8:KW])
                else:
                    ahead.append(jnp.max(s, axis=0, keepdims=True))
                    cbuf[buf][g][...] = ahead[-1]
            return ahead

        def soft_stage(c, buf, ahead):
            c0 = pl.multiple_of(c * KW, KW)
            for g, (m_ref, l_ref, acc_ref) in enumerate(chains):
                after = None if ahead is None else ahead[g]
                if bounded:
                    _flash_accumulate(sbuf[buf][g][...], vt_ref[g, :, pl.ds(c0, KW)], l_ref, acc_ref, after=after)
                else:
                    _flash_update(sbuf[buf][g][...], vt_ref[g, :, pl.ds(c0, KW)], m_ref, l_ref, acc_ref,
                                  col_max=cbuf[buf][g][...], after=after)

        _pipelined_chunks(c_near, qk_stage, soft_stage)

        def near_step(c, x):
            c0 = pl.multiple_of(c * KW, KW)
            madd = mask_add(c0)
            scores = [raw_scores(c0, g) + madd
                      + _near_bias(dt_ref, group_heads[g], qi, c * (KW // LANE), KW // LANE)
                      for g in range(ng)]
            for g, (m_ref, l_ref, acc_ref) in enumerate(chains):
                if bounded:
                    _flash_accumulate(scores[g], vt_ref[g, :, pl.ds(c0, KW)], l_ref, acc_ref)
                else:
                    _flash_update(scores[g], vt_ref[g, :, pl.ds(c0, KW)], m_ref, l_ref, acc_ref)
            return x

        lax.fori_loop(c_near, n_chunk, near_step, 0)
        for g, (m_ref, l_ref, acc_ref) in enumerate(chains):
            o_t = _sum_result(l_ref, acc_ref) if bounded else _flash_result(m_ref, l_ref, acc_ref)
            for h in range(DSA_HPG):
                hh = group_heads[g][h]
                o_ref[:, hh * HEAD_DIM:(hh + 1) * HEAD_DIM] = o_t[:, h * QB:(h + 1) * QB].T.astype(o_ref.dtype)

    pl.when(bounded_ok)(lambda: attend(True))
    pl.when(jnp.logical_not(bounded_ok))(lambda: attend(False))


def _indexer_layouts(get_slab, c_ref, sa_ref, sb_ref, iqt_ref, ik_ref, iwt_ref):
    nslab_q = IDX_HEADS * IDX_DIM // LANE
    per = LANE // IDX_DIM
    half = IDX_ROPE // 2
    zrows = jnp.zeros((LANE - IDX_DIM, QB), F32)

    def rope_slab(x, c, sa, sb):
        return x * c + pltpu.roll(x, LANE - half, axis=1) * sa + pltpu.roll(x, half, axis=1) * sb

    for t in range(iqt_ref.shape[0]):
        rows = slice(t * QB, (t + 1) * QB)
        c, sa, sb = c_ref[rows, :], sa_ref[rows, :], sb_ref[rows, :]
        cols = []
        for s in range(nslab_q):
            x_t = (rope_slab(get_slab(s, rows), c, sa, sb) * IDX_DIM ** -0.5).T
            for j in range(per):
                cols.append(jnp.concatenate([x_t[j * IDX_DIM:(j + 1) * IDX_DIM], zrows], axis=0))
        iqt_ref[t] = jnp.concatenate(cols, axis=1).astype(iqt_ref.dtype)
        tail = get_slab(nslab_q, rows)
        lane = lax.broadcasted_iota(jnp.int32, (QB, LANE), 1)
        ik_ref[rows, :] = jnp.where(lane < IDX_DIM, rope_slab(tail, c, sa, sb), 0.0).astype(ik_ref.dtype)
        w_t = (tail * IDX_HEADS ** -0.5).T
        iwt_ref[t] = jnp.concatenate([w_t[IDX_DIM + h:IDX_DIM + h + 1, :] for h in range(IDX_HEADS)], axis=1)


def dsa_attention(logit_bound, iq_t, iw_t, ik, q_t, k, v_t, dt, bsz, seq):
    nq = seq // QB
    k_sel = min(DSA_TOPK_MAX, seq // 4)
    assert seq % KW == 0
    lanes = DSA_HPG * QB
    return pl.pallas_call(
        functools.partial(_dsa_kernel, seq=seq, k_sel=k_sel),
        grid=(bsz, nq),
        in_specs=[pl.BlockSpec(memory_space=pltpu.SMEM),
                  pl.BlockSpec((1, LANE, IDX_HEADS * QB), lambda b, i: (b * nq + i, 0, 0)),
                  pl.BlockSpec((1, 1, IDX_HEADS * QB), lambda b, i: (b * nq + i, 0, 0)),
                  pl.BlockSpec((seq, LANE), lambda b, i: (b, 0)),
                  pl.BlockSpec((DSA_HEADS, HEAD_DIM, QB), lambda b, i: (0, 0, b * nq + i)),
                  pl.BlockSpec((DSA_KV_HEADS, seq, HEAD_DIM), lambda b, i: (0, b, 0),
                               pipeline_mode=pl.Buffered(1)),
                  pl.BlockSpec((DSA_KV_HEADS, HEAD_DIM, seq), lambda b, i: (0, 0, b),
                               pipeline_mode=pl.Buffered(1)),
                  pl.BlockSpec((DSA_HEADS, 3, LANE, LANE), lambda b, i: (0, 0, 0, 0),
                               pipeline_mode=pl.Buffered(1))],
        out_specs=pl.BlockSpec((QB, DSA_HEADS * HEAD_DIM), lambda b, i: (b * nq + i, 0)),
        out_shape=jax.ShapeDtypeStruct((bsz * seq, DSA_HEADS * HEAD_DIM), BF16),
        scratch_shapes=[pltpu.VMEM((seq, QB), jnp.int32)]
        + [pltpu.VMEM((1, lanes), F32), pltpu.VMEM((1, lanes), F32),
           pltpu.VMEM((HEAD_DIM, lanes), F32)] * DSA_KV_HEADS
        + [pltpu.VMEM((KW, lanes), F32)] * (2 * DSA_KV_HEADS)
        + [pltpu.VMEM((1, lanes), F32)] * (2 * DSA_KV_HEADS),
        compiler_params=_cparams(("arbitrary", "arbitrary")),
        name="dsa_attention",
    )(jnp.reshape(logit_bound, (1,)).astype(F32), iq_t, iw_t, ik, q_t, k, v_t, dt)


def _rope_tables(seq, dim):
    half = dim // 2
    inv = ROPE_THETA ** (-jnp.arange(half, dtype=F32) / half)
    ang = jnp.arange(seq, dtype=F32)[:, None] * inv[None, :]
    return jnp.cos(ang), jnp.sin(ang)


def _logit_bound(gq, gk, dim, scale):
    return dim * scale * jnp.max(jnp.abs(gq)) * jnp.max(jnp.abs(gk)) * (1.0 + 2.0 ** -7)


def _pad_cols(w, n):
    return jnp.pad(w, ((0, 0), (0, n - w.shape[1])))


def _t(x):
    return jnp.swapaxes(x, -1, -2)


def _even_mixer(h, x2, gate, next_norm, dt, dc, bias_bound, bsz, seq, w_in, w_out, nsa_qk_g, cmp_pe, cmp_w1, cmp_b1,
                cmp_w2, cmp_b2, q_norm_g, kv_norm_g, w_uq, w_ukv, nope_g, rope_g):
    m = bsz * seq
    nq_cols = NSA_HEADS * HEAD_DIM
    nkv_cols = 6 * NSA_GROUPS * HEAD_DIM
    ngate = 3 * NSA_HEADS
    o_gate = nq_cols + nkv_cols
    o_cq = o_gate + ngate
    o_ckv = o_cq + MLA_Q_RANK
    o_kpe = o_ckv + MLA_KV_RANK
    gw = NSA_GROUPS * HEAD_DIM
    kvw = [w_in[:, nq_cols + i * gw:nq_cols + (i + 1) * gw] for i in range(6)]
    scale = HEAD_DIM ** -0.5 * LOG2E
    tail = jnp.concatenate([w_in[:, o_kpe:], w_in[:, o_gate:o_cq]], axis=1)
    w_r = jnp.concatenate([kvw[0], kvw[1], w_in[:, o_cq:o_kpe], _pad_cols(tail, LANE)], axis=1).astype(BF16)
    dq = MLA_NOPE + MLA_ROPE
    wq = w_uq.reshape(MLA_Q_RANK, MLA_HEADS, dq)
    wq_r = jnp.concatenate([wq[:, :, :MLA_NOPE].reshape(MLA_Q_RANK, -1),
                            wq[:, :, MLA_NOPE:].reshape(MLA_Q_RANK, -1)], axis=1).astype(BF16)
    mscale = dq ** -0.5 * LOG2E
    side = [jnp.sqrt(MLA_NOPE * jnp.max(jnp.abs(nope_g[i])) ** 2 + MLA_ROPE * jnp.max(jnp.abs(rope_g[i])) ** 2)
            for i in range(2)]
    mla_bound = mscale * side[0] * side[1] * (1.0 + 2.0 ** -7)
    mla_shift = jnp.where(mla_bound <= SAFE_LOG2_BOUND, mla_bound, 0.0)
    nraw = 2 * NSA_GROUPS
    q_t, k_sw, v_sw_t, proj, q_mla_t, k_mla, v_mla_t = proj_even(
        h, w_in[:, :nq_cols].astype(BF16), nsa_qk_g[0] * scale,
        jnp.concatenate([kvw[2], kvw[4]], axis=1).astype(BF16), nsa_qk_g[1],
        jnp.concatenate([kvw[3], kvw[5]], axis=1).astype(BF16), w_r, nraw, seq, q_norm_g, kv_norm_g, wq_r,
        w_ukv.astype(BF16), nope_g, rope_g, mscale, mla_shift)
    kvc = compress_kv(proj, 0, bsz, seq, cmp_pe, cmp_w1, cmp_b1, cmp_w2, cmp_b2, nsa_qk_g[1])
    tail_v = proj[nraw]
    gates = tail_v[:, MLA_ROPE:MLA_ROPE + ngate].reshape(m, NSA_GROUPS, 3 * NSA_HPG)
    gates_t = jnp.pad(jnp.transpose(gates, (1, 2, 0)), ((0, 0), (0, GATE_ROWS - 3 * NSA_HPG), (0, 0)))
    nsa_bound = _logit_bound(nsa_qk_g[0], nsa_qk_g[1], HEAD_DIM, scale) + bias_bound
    o_nsa = nsa_attention(nsa_bound, q_t, gates_t, kvc[0], _t(kvc[1]), k_sw, v_sw_t,
                          dt[:NSA_HEADS], dc[:NSA_HEADS], bsz, seq)

    o_mla = mla_attention(mla_bound, q_mla_t, k_mla, v_mla_t, bsz, seq)
    w_o = w_out.astype(BF16)
    return resproj([(o_nsa, w_o[:nq_cols]), (o_mla, w_o[nq_cols:])], x2, gate, seq, next_norm)


def _odd_mixer(h, x2, gate, next_norm, dt, bias_bound, bsz, seq, w_in, w_out, qk_g):
    nq = DSA_HEADS * HEAD_DIM
    nkv = DSA_KV_HEADS * HEAD_DIM
    niq = IDX_HEADS * IDX_DIM
    o_k, o_v, o_iq = nq, nq + nkv, nq + 2 * nkv
    w_idx = w_in[:, o_iq:]
    q_t, k, v_t, iq_t, ik, iw_t = proj_qkv_indexer(
        h, w_in[:, :o_k].astype(BF16), qk_g[0] * (HEAD_DIM ** -0.5 * LOG2E), w_in[:, o_k:o_v].astype(BF16), qk_g[1],
        w_in[:, o_v:o_iq].astype(BF16), _pad_cols(w_idx, niq + LANE).astype(BF16), seq)
    bound = _logit_bound(qk_g[0], qk_g[1], HEAD_DIM, HEAD_DIM ** -0.5 * LOG2E) + bias_bound
    o = dsa_attention(bound, iq_t, iw_t, ik, q_t, k, v_t, dt, bsz, seq)
    return resproj([(o, w_out.astype(BF16))], x2, gate, seq, next_norm)


def _conv_ffn(h, x2, gate, next_norm, seq, w_up_all, layer, conv_w, conv_b, w_down):
    a = ffn_up(h, w_up_all, layer, conv_w, conv_b, seq)
    return resproj([(a, w_down.astype(BF16))], x2, gate, seq, next_norm)


def kernel(x, c, rel_bias, ada_w, ada_b, norm_g, ev_w_in, ev_w_out, nsa_qk_g, cmp_pe, cmp_w1, cmp_b1, cmp_w2, cmp_b2, mla_q_norm_g, mla_kv_norm_g, mla_w_uq, mla_w_ukv, mla_nope_g, mla_rope_g, od_w_in, od_w_out, dsa_qk_g, ffn_w_up, ffn_conv_w, ffn_conv_b, ffn_w_down):
    bsz, seq, d = x.shape
    depth = ada_w.shape[0]
    x2 = x.reshape(bsz * seq, d)
    mods = ada_all(c, ada_w, ada_b)
    dt, dc = bias_tiles(rel_bias)
    bias_bound = 2.0 * LOG2E * jnp.max(jnp.abs(rel_bias))
    def norm_of(i, sub):
        if i >= depth:
            return None
        shift, scale, _ = jnp.split(mods[i, sub], 3, axis=-1)
        return norm_g[i, sub], scale, shift

    g0, scale0, shift0 = norm_of(0, 0)
    h = modnorm(x2, g0, scale0, shift0, seq)
    for i in range(depth):
        j = i // 2
        gate = jnp.split(mods[i, 0], 3, axis=-1)[2]
        if i % 2 == 0:
            x2, h = _even_mixer(h, x2, gate, norm_of(i, 1), dt, dc, bias_bound, bsz, seq, ev_w_in[j],
                                ev_w_out[j], nsa_qk_g[j], cmp_pe[j], cmp_w1[j], cmp_b1[j], cmp_w2[j],
                                cmp_b2[j], mla_q_norm_g[j], mla_kv_norm_g[j], mla_w_uq[j], mla_w_ukv[j],
                                mla_nope_g[j], mla_rope_g[j])
        else:
            x2, h = _odd_mixer(h, x2, gate, norm_of(i, 1), dt, bias_bound, bsz, seq, od_w_in[j], od_w_out[j],
                               dsa_qk_g[j])
        gate = jnp.split(mods[i, 1], 3, axis=-1)[2]
        x2, h = _conv_ffn(h, x2, gate, norm_of(i + 1, 0), seq, ffn_w_up, i, ffn_conv_w[i], ffn_conv_b[i],
                          ffn_w_down[i])
    return x2.reshape(bsz, seq, d)
```

```python
import functools
import math

import numpy as np
import jax
import jax.numpy as jnp
from jax import lax
from jax.experimental import pallas as pl
from jax.experimental.pallas import tpu as pltpu

HEAD_DIM = 128
NSA_HEADS = 8
NSA_GROUPS = 2
NSA_HPG = NSA_HEADS // NSA_GROUPS
CMP_BLOCK = 32
CMP_STRIDE = 16
CMP_HIDDEN = 256
SEL_BLOCK = 64
SEL_TOP_N = 16
WINDOW = 512
MLA_HEADS = 8
MLA_Q_RANK = 512
MLA_KV_RANK = 256
MLA_NOPE = 128
MLA_ROPE = 64
MLA_V = 128
DSA_HEADS = 16
DSA_KV_HEADS = 4
DSA_HPG = DSA_HEADS // DSA_KV_HEADS
IDX_HEADS = 16
IDX_DIM = 64
IDX_ROPE = 32
DSA_TOPK_MAX = 256
REL_BUCKETS = 32
REL_MAX_DIST = 128
CONV_WIDTH = 3
ROPE_THETA = 10000.0
EPS = 1e-6
NEG = -1e30
FORCE = 1e9

LANE = 128
SUBLANE = 8
QB = 128
VMEM_LIMIT = 56 * 1024 * 1024

F32 = jnp.float32
BF16 = jnp.bfloat16


def _t5_thresholds():
    d = np.arange(0, 4 * REL_MAX_DIST)
    half = REL_BUCKETS // 2
    val = np.log(np.maximum(d, 1) / half) / math.log(REL_MAX_DIST / half) * (REL_BUCKETS - half)
    large = np.minimum(half + np.floor(np.maximum(val, 0.0)).astype(np.int64), REL_BUCKETS - 1)
    bucket = np.where(d < half, d, large)
    return [int(np.argmax(bucket >= b)) for b in range(1, REL_BUCKETS)]


T5_THR = _t5_thresholds()
T5_FAR = T5_THR[-1]
assert T5_FAR <= LANE


def _cparams(sem):
    return pltpu.CompilerParams(dimension_semantics=sem, vmem_limit_bytes=VMEM_LIMIT)


def _dot(a, b):
    return jnp.dot(a, b, preferred_element_type=F32)


def _ada_kernel(ct_ref, w_ref, b_ref, o_ref, *, bsz):
    ct = ct_ref[...]
    a = ct * jax.nn.sigmoid(ct)
    w = w_ref[0]
    rows = [jnp.sum(a[:, b:b + 1] * w, axis=0, keepdims=True) for b in range(bsz)]
    rows.append(jnp.zeros((o_ref.shape[1] - bsz, w.shape[1]), F32))
    o_ref[0] = jnp.concatenate(rows, axis=0) + b_ref[0]


def ada_all(c, ada_w, ada_b):
    depth, two, d, n3 = ada_w.shape
    bsz = c.shape[0]
    rows = -(-bsz // SUBLANE) * SUBLANE
    assert bsz <= LANE
    ct = jnp.zeros((d, LANE), F32).at[:, :bsz].set(c.T)
    w = ada_w.reshape(depth * two, d, n3)
    b = ada_b.reshape(depth * two, 1, n3)
    tn = 512
    out = pl.pallas_call(
        functools.partial(_ada_kernel, bsz=bsz),
        grid=(depth * two, n3 // tn),
        in_specs=[pl.BlockSpec((d, LANE), lambda l, j: (0, 0)),
                  pl.BlockSpec((1, d, tn), lambda l, j: (l, 0, j)),
                  pl.BlockSpec((1, 1, tn), lambda l, j: (l, 0, j))],
        out_specs=pl.BlockSpec((1, rows, tn), lambda l, j: (l, 0, j)),
        out_shape=jax.ShapeDtypeStruct((depth * two, rows, n3), F32),
        compiler_params=_cparams(("arbitrary", "arbitrary")),
        name="ada_mod",
    )(ct, w, b)
    return out[:, :bsz].reshape(depth, two, bsz, n3)


def _modnorm_kernel(x_ref, g_ref, sc_ref, sh_ref, o_ref):
    x = x_ref[...]
    y = x * lax.rsqrt(jnp.mean(x * x, axis=-1, keepdims=True) + EPS)
    h = (y * g_ref[...]) * (1.0 + sc_ref[0]) + sh_ref[0]
    o_ref[...] = h.astype(o_ref.dtype)


def modnorm(x2, g, scale, shift, seq, tm=1024):
    m, d = x2.shape
    tpb = seq // tm
    return pl.pallas_call(
        _modnorm_kernel,
        grid=(m // tm,),
        in_specs=[pl.BlockSpec((tm, d), lambda i: (i, 0)),
                  pl.BlockSpec((1, d), lambda i: (0, 0)),
                  pl.BlockSpec((1, 1, d), lambda i: (i // tpb, 0, 0)),
                  pl.BlockSpec((1, 1, d), lambda i: (i // tpb, 0, 0))],
        out_specs=pl.BlockSpec((tm, d), lambda i: (i, 0)),
        out_shape=jax.ShapeDtypeStruct((m, d), BF16),
        compiler_params=_cparams(("arbitrary",)),
        name="modnorm",
    )(x2, g.reshape(1, d), scale.reshape(-1, 1, d), shift.reshape(-1, 1, d))


def _head_norm(y, g_ref):
    return y * lax.rsqrt(jnp.mean(y * y, axis=-1, keepdims=True) + EPS) * g_ref[...]


def _qkv_heads(x, wq_ref, wk_ref, wv_ref, gq_ref, gk_ref, oq_ref, ok_ref, ov_ref):
    acc = _dot(x, wq_ref[...])
    for s in range(oq_ref.shape[0]):
        oq_ref[s] = _head_norm(acc[:, s * LANE:(s + 1) * LANE], gq_ref).T.astype(oq_ref.dtype)
    acc = _dot(x, wk_ref[...])
    for s in range(ok_ref.shape[0]):
        ok_ref[s] = _head_norm(acc[:, s * LANE:(s + 1) * LANE], gk_ref).astype(ok_ref.dtype)
    acc = _dot(x, wv_ref[...])
    for s in range(ov_ref.shape[0]):
        ov_ref[s] = acc[:, s * LANE:(s + 1) * LANE].T.astype(ov_ref.dtype)


def _proj_qkvi_kernel(x_ref, wq_ref, wk_ref, wv_ref, wi_ref, gq_ref, gk_ref, c_ref, sa_ref, sb_ref,
                      oq_ref, ok_ref, ov_ref, iqt_ref, ik_ref, iwt_ref):
    x = x_ref[...]
    _qkv_heads(x, wq_ref, wk_ref, wv_ref, gq_ref, gk_ref, oq_ref, ok_ref, ov_ref)
    acc = _dot(x, wi_ref[...])
    _indexer_layouts(lambda s, rows: acc[rows, s * LANE:(s + 1) * LANE], c_ref, sa_ref, sb_ref,
                     iqt_ref, ik_ref, iwt_ref)


def _proj_specs(x, ws, tm):
    m, k = x.shape
    whole = lambda w: pl.BlockSpec(w.shape, lambda i: (0, 0), pipeline_mode=pl.Buffered(1))
    gain = pl.BlockSpec((1, LANE), lambda i: (0, 0))
    nq, nk, nv = (w.shape[1] // LANE for w in ws[:3])
    in_specs = [pl.BlockSpec((tm, k), lambda i: (i, 0))] + [whole(w) for w in ws] + [gain, gain]
    out_specs = [pl.BlockSpec((nq, LANE, tm), lambda i: (0, 0, i)),
                 pl.BlockSpec((nk, tm, LANE), lambda i: (0, i, 0)),
                 pl.BlockSpec((nv, LANE, tm), lambda i: (0, 0, i))]
    out_shape = [jax.ShapeDtypeStruct((nq, LANE, m), BF16),
                 jax.ShapeDtypeStruct((nk, m, LANE), BF16),
                 jax.ShapeDtypeStruct((nv, LANE, m), BF16)]
    return in_specs, out_specs, out_shape


def proj_qkv_indexer(x, wq, g_q, wk, g_k, wv, wi, seq, tm=512):
    m = x.shape[0]
    ntile = tm // QB
    tps = seq // tm
    cos, sin = _rope_tables(seq, IDX_ROPE)
    zero = jnp.zeros_like(sin)
    rest = IDX_DIM - IDX_ROPE
    per = LANE // IDX_DIM
    c_tab = jnp.tile(jnp.concatenate([cos, cos, jnp.ones((seq, rest), F32)], axis=1), (1, per))
    sa_tab = jnp.tile(jnp.concatenate([-sin, zero, jnp.zeros((seq, rest), F32)], axis=1), (1, per))
    sb_tab = jnp.tile(jnp.concatenate([zero, sin, jnp.zeros((seq, rest), F32)], axis=1), (1, per))
    lanes = IDX_HEADS * QB
    tab_spec = pl.BlockSpec((tm, LANE), lambda i: (i % tps, 0))
    in_specs, out_specs, out_shape = _proj_specs(x, (wq, wk, wv, wi), tm)
    return pl.pallas_call(
        _proj_qkvi_kernel,
        grid=(m // tm,),
        in_specs=in_specs + [tab_spec, tab_spec, tab_spec],
        out_specs=out_specs + [pl.BlockSpec((ntile, LANE, lanes), lambda i: (i, 0, 0)),
                               pl.BlockSpec((tm, LANE), lambda i: (i, 0)),
                               pl.BlockSpec((ntile, 1, lanes), lambda i: (i, 0, 0))],
        out_shape=out_shape + [jax.ShapeDtypeStruct((m // QB, LANE, lanes), BF16),
                               jax.ShapeDtypeStruct((m, LANE), BF16),
                               jax.ShapeDtypeStruct((m // QB, 1, lanes), F32)],
        compiler_params=_cparams(("arbitrary",)),
        name="proj_qkv_indexer",
    )(x, wq, wk, wv, wi, g_q.reshape(1, LANE), g_k.reshape(1, LANE), c_tab, sa_tab, sb_tab)


def _rms_rows(x, g):
    return x * lax.rsqrt(jnp.mean(x * x, axis=-1, keepdims=True) + EPS) * g


def _rope_rows(x, cos, sin):
    half = x.shape[-1] // 2
    x1, x2 = x[:, :half], x[:, half:]
    return jnp.concatenate([x1 * cos - x2 * sin, x1 * sin + x2 * cos], axis=1)


def _mla_q_heads(acc, shift_ref, gn_ref, gr2_ref, c_ref, sa_ref, sb_ref, o_ref, scale):
    tm = acc.shape[0]
    for h in range(MLA_HEADS):
        nope = _rms_rows(acc[:, h * MLA_NOPE:(h + 1) * MLA_NOPE], gn_ref[...]) * scale
        o_ref[h, 0:MLA_NOPE, :] = nope.T.astype(o_ref.dtype)
    first = lax.broadcasted_iota(jnp.int32, (LANE - MLA_ROPE, tm), 0) == 0
    pad_rows = jnp.where(first, -shift_ref[0], 0.0)
    low = lax.broadcasted_iota(jnp.int32, (tm, LANE), 1) < MLA_ROPE
    c, sa, sb = c_ref[...], sa_ref[...], sb_ref[...]
    half = MLA_ROPE // 2
    per = LANE // MLA_ROPE
    for s in range(MLA_HEADS // per):
        x = acc[:, MLA_HEADS * MLA_NOPE + s * LANE:MLA_HEADS * MLA_NOPE + (s + 1) * LANE]
        sq = x * x
        s_low = jnp.sum(jnp.where(low, sq, 0.0), axis=-1, keepdims=True)
        s_all = jnp.sum(sq, axis=-1, keepdims=True)
        inv = jnp.where(low, lax.rsqrt(s_low / MLA_ROPE + EPS), lax.rsqrt((s_all - s_low) / MLA_ROPE + EPS))
        y = x * inv * gr2_ref[...]
        roped = y * c + pltpu.roll(y, LANE - half, axis=1) * sa + pltpu.roll(y, half, axis=1) * sb
        x_t = (roped * scale).T
        for j in range(per):
            o_ref[per * s + j, MLA_NOPE:MLA_NOPE + LANE, :] = jnp.concatenate(
                [x_t[j * MLA_ROPE:(j + 1) * MLA_ROPE], pad_rows], axis=0).astype(o_ref.dtype)


def _mla_kv_heads(acc, tail, gn_ref, gr_ref, cos_ref, sin_ref, ok_ref, ov_ref):
    tm = acc.shape[0]
    k_pe = _rope_rows(_rms_rows(tail[:, :MLA_ROPE], gr_ref[...]), cos_ref[...], sin_ref[...])
    first = lax.broadcasted_iota(jnp.int32, (tm, LANE - MLA_ROPE), 1) == 0
    k_pe = jnp.concatenate([k_pe, jnp.where(first, 1.0, 0.0)], axis=1).astype(ok_ref.dtype)
    for h in range(MLA_HEADS):
        c0 = h * (MLA_NOPE + MLA_V)
        ok_ref[h, :, 0:MLA_NOPE] = _rms_rows(acc[:, c0:c0 + MLA_NOPE], gn_ref[...]).astype(ok_ref.dtype)
        ok_ref[h, :, MLA_NOPE:MLA_NOPE + LANE] = k_pe
        ov_ref[h] = acc[:, c0 + MLA_NOPE:c0 + MLA_NOPE + MLA_V].T.astype(ov_ref.dtype)


def _proj_even_kernel(shift_ref, x_ref, wq_ref, wk_ref, wv_ref, wr_ref, gq_ref, gk_ref, glq_ref, glkv_ref,
                      wuq_ref, wukv_ref, gnq_ref, grq2_ref, gnk_ref, grk_ref, c_ref, sa_ref, sb_ref,
                      cos_ref, sin_ref, oq_ref, ok_ref, ov_ref, or_ref, oqm_ref, okm_ref, ovm_ref, *, scale):
    x = x_ref[...]
    _qkv_heads(x, wq_ref, wk_ref, wv_ref, gq_ref, gk_ref, oq_ref, ok_ref, ov_ref)
    acc = _dot(x, wr_ref[...])
    nraw = or_ref.shape[0] - 1
    for s in range(nraw):
        or_ref[s] = acc[:, s * LANE:(s + 1) * LANE]
    c0 = nraw * LANE
    c1 = c0 + MLA_Q_RANK
    c2 = c1 + MLA_KV_RANK
    tail = acc[:, c2:c2 + LANE]
    or_ref[nraw] = tail
    lat_q = _rms_rows(acc[:, c0:c1], glq_ref[...]).astype(BF16)
    _mla_q_heads(_dot(lat_q, wuq_ref[...]), shift_ref, gnq_ref, grq2_ref, c_ref, sa_ref, sb_ref, oqm_ref, scale)
    lat_kv = _rms_rows(acc[:, c1:c2], glkv_ref[...]).astype(BF16)
    _mla_kv_heads(_dot(lat_kv, wukv_ref[...]), tail, gnk_ref, grk_ref, cos_ref, sin_ref, okm_ref, ovm_ref)


def proj_even(x, wq, g_q, wk, g_k, wv, wr, nraw, seq, q_norm_g, kv_norm_g, wq_r, w_ukv, nope_g, rope_g,
              scale, shift, tm=512):
    m = x.shape[0]
    tps = seq // tm
    dqk = MLA_NOPE + LANE
    half = MLA_ROPE // 2
    per = LANE // MLA_ROPE
    cos, sin = _rope_tables(seq, MLA_ROPE)
    zero = jnp.zeros_like(sin)
    c_tab = jnp.tile(jnp.concatenate([cos, cos], axis=1), (1, per))
    sa_tab = jnp.tile(jnp.concatenate([-sin, zero], axis=1), (1, per))
    sb_tab = jnp.tile(jnp.concatenate([zero, sin], axis=1), (1, per))
    tab_spec = pl.BlockSpec((tm, LANE), lambda i: (i % tps, 0))
    rope_spec = pl.BlockSpec((tm, half), lambda i: (i % tps, 0))
    row = lambda n: pl.BlockSpec((1, n), lambda i: (0, 0))
    whole = lambda w: pl.BlockSpec(w.shape, lambda i: (0, 0), pipeline_mode=pl.Buffered(1))
    in_specs, out_specs, out_shape = _proj_specs(x, (wq, wk, wv, wr), tm)
    in_specs = ([pl.BlockSpec(memory_space=pltpu.SMEM)] + in_specs
                + [row(MLA_Q_RANK), row(MLA_KV_RANK), whole(wq_r), whole(w_ukv),
                   row(MLA_NOPE), row(LANE), row(MLA_NOPE), row(MLA_ROPE),
                   tab_spec, tab_spec, tab_spec, rope_spec, rope_spec])
    out_specs += [pl.BlockSpec((nraw + 1, tm, LANE), lambda i: (0, i, 0)),
                  pl.BlockSpec((MLA_HEADS, dqk, tm), lambda i: (0, 0, i)),
                  pl.BlockSpec((MLA_HEADS, tm, dqk), lambda i: (0, i, 0)),
                  pl.BlockSpec((MLA_HEADS, MLA_V, tm), lambda i: (0, 0, i))]
    out_shape += [jax.ShapeDtypeStruct((nraw + 1, m, LANE), F32),
                  jax.ShapeDtypeStruct((MLA_HEADS, dqk, m), BF16),
                  jax.ShapeDtypeStruct((MLA_HEADS, m, dqk), BF16),
                  jax.ShapeDtypeStruct((MLA_HEADS, MLA_V, m), BF16)]
    return pl.pallas_call(
        functools.partial(_proj_even_kernel, scale=scale),
        grid=(m // tm,),
        in_specs=in_specs,
        out_specs=out_specs,
        out_shape=out_shape,
        compiler_params=_cparams(("arbitrary",)),
        name="proj_nsa_mla",
    )(jnp.reshape(shift, (1,)).astype(F32), x, wq, wk, wv, wr, g_q.reshape(1, LANE), g_k.reshape(1, LANE),
      q_norm_g.reshape(1, -1), kv_norm_g.reshape(1, -1), wq_r, w_ukv, nope_g[0].reshape(1, -1),
      jnp.tile(rope_g[0].reshape(1, -1), (1, per)), nope_g[1].reshape(1, -1), rope_g[1].reshape(1, -1),
      c_tab, sa_tab, sb_tab, cos, sin)


RES_COLS = 512


def _resproj_kernel(*refs, npair, fuse_norm):
    xres_ref, gate_ref = refs[2 * npair], refs[2 * npair + 1]
    outs = refs[2 * npair + 2 + (3 if fuse_norm else 0):]
    o_ref = outs[0]
    n = o_ref.shape[1]
    for c0 in range(0, n, RES_COLS):
        cols = slice(c0, c0 + RES_COLS)
        acc = _dot(refs[0][...], refs[1][:, cols])
        for p in range(1, npair):
            acc = acc + _dot(refs[2 * p][...], refs[2 * p + 1][:, cols])
        o_ref[:, cols] = xres_ref[:, cols] + gate_ref[0][:, cols] * acc
    if fuse_norm:
        g_ref, sc_ref, sh_ref = refs[2 * npair + 2:2 * npair + 5]
        x = o_ref[...]
        y = x * lax.rsqrt(jnp.mean(x * x, axis=-1, keepdims=True) + EPS)
        outs[1][...] = ((y * g_ref[...]) * (1.0 + sc_ref[0]) + sh_ref[0]).astype(outs[1].dtype)


def resproj(pairs, xres, gate, seq, next_norm=None, tm=512):
    m, n = xres.shape
    tpb = seq // tm
    in_specs, args = [], []
    for x, w in pairs:
        k = x.shape[1]
        in_specs += [pl.BlockSpec((tm, k), lambda i: (i, 0)),
                     pl.BlockSpec((k, n), lambda i: (0, 0), pipeline_mode=pl.Buffered(1))]
        args += [x, w]
    per_batch = pl.BlockSpec((1, 1, n), lambda i: (i // tpb, 0, 0))
    in_specs += [pl.BlockSpec((tm, n), lambda i: (i, 0)), per_batch]
    args += [xres, gate.reshape(-1, 1, n)]
    out_specs = [pl.BlockSpec((tm, n), lambda i: (i, 0))]
    out_shape = [jax.ShapeDtypeStruct((m, n), F32)]
    if next_norm is not None:
        g, scale, shift = next_norm
        in_specs += [pl.BlockSpec((1, n), lambda i: (0, 0)), per_batch, per_batch]
        args += [g.reshape(1, n), scale.reshape(-1, 1, n), shift.reshape(-1, 1, n)]
        out_specs.append(pl.BlockSpec((tm, n), lambda i: (i, 0)))
        out_shape.append(jax.ShapeDtypeStruct((m, n), BF16))
    out = pl.pallas_call(
        functools.partial(_resproj_kernel, npair=len(pairs), fuse_norm=next_norm is not None),
        grid=(m // tm,),
        in_specs=in_specs,
        out_specs=out_specs,
        out_shape=out_shape,
        compiler_params=_cparams(("arbitrary",)),
        name="resproj",
    )(*args)
    return (out[0], out[1]) if next_norm is not None else (out[0], None)


HALO = 8
EPI_ROWS = 64


def _ffn_up_kernel(h_ref, wg32_ref, wv32_ref, cwg_ref, cwv_ref, cbg_ref, cbv_ref, o_ref,
                   ug_ref, uv_ref, wg_ref, wv_ref, *, tm, tiles_per_seq):
    i = pl.program_id(1)
    first = (i % tiles_per_seq) == 0

    @pl.when(i == 0)
    def _():
        wg_ref[...] = wg32_ref[...].astype(wg_ref.dtype)
        wv_ref[...] = wv32_ref[...].astype(wv_ref.dtype)

    @pl.when(first)
    def _():
        ug_ref[0:HALO, :] = jnp.zeros((HALO, ug_ref.shape[1]), F32)
        uv_ref[0:HALO, :] = jnp.zeros((HALO, uv_ref.shape[1]), F32)

    @pl.when(jnp.logical_not(first))
    def _():
        ug_ref[0:HALO, :] = ug_ref[tm:tm + HALO, :]
        uv_ref[0:HALO, :] = uv_ref[tm:tm + HALO, :]

    h = h_ref[...]
    ug_ref[HALO:HALO + tm, :] = _dot(h, wg_ref[...])
    uv_ref[HALO:HALO + tm, :] = _dot(h, wv_ref[...])

    def conv(u_ref, cw_ref, cb_ref, r0):
        out = cb_ref[...]
        for j in range(CONV_WIDTH):
            off = r0 + HALO - (CONV_WIDTH - 1) + j
            out = out + cw_ref[j:j + 1, :] * u_ref[off:off + EPI_ROWS, :]
        return out

    for r0 in range(0, tm, EPI_ROWS):
        g = conv(ug_ref, cwg_ref, cbg_ref, r0)
        v = conv(uv_ref, cwv_ref, cbv_ref, r0)
        o_ref[r0:r0 + EPI_ROWS, :] = (g * jax.nn.sigmoid(g) * v).astype(o_ref.dtype)


def ffn_up(h, w_up_all, layer, conv_w, conv_b, seq, tm=1024, tn=512):
    m, d = h.shape
    f = w_up_all.shape[2] // 2
    nj = f // tn
    tps = seq // tm
    cb = conv_b.reshape(1, 2 * f)
    return pl.pallas_call(
        functools.partial(_ffn_up_kernel, tm=tm, tiles_per_seq=tps),
        grid=(nj, m // tm),
        in_specs=[pl.BlockSpec((tm, d), lambda j, i: (i, 0)),
                  pl.BlockSpec((None, d, tn), lambda j, i: (layer, 0, j)),
                  pl.BlockSpec((None, d, tn), lambda j, i: (layer, 0, nj + j)),
                  pl.BlockSpec((CONV_WIDTH, tn), lambda j, i: (0, j)),
                  pl.BlockSpec((CONV_WIDTH, tn), lambda j, i: (0, nj + j)),
                  pl.BlockSpec((1, tn), lambda j, i: (0, j)),
                  pl.BlockSpec((1, tn), lambda j, i: (0, nj + j))],
        out_specs=pl.BlockSpec((tm, tn), lambda j, i: (i, j)),
        out_shape=jax.ShapeDtypeStruct((m, f), BF16),
        scratch_shapes=[pltpu.VMEM((tm + HALO, tn), F32), pltpu.VMEM((tm + HALO, tn), F32),
                        pltpu.VMEM((d, tn), BF16), pltpu.VMEM((d, tn), BF16)],
        compiler_params=_cparams(("arbitrary", "arbitrary")),
        name="ffn_up_conv",
    )(h, w_up_all, w_up_all, conv_w, conv_w, cb, cb)


LOG2E = 1.4426950408889634
CWIN = 16


def _t5_shifted(dist, tbl_ref, h):
    val = jnp.full(dist.shape, tbl_ref[0, h], F32)
    for b in range(1, REL_BUCKETS):
        val = jnp.where(dist >= T5_THR[b - 1], tbl_ref[b, h], val)
    return (val - tbl_ref[REL_BUCKETS - 1, h]) * LOG2E


def _bias_tiles_kernel(tbl_ref, dt_ref, dc_ref):
    h = pl.program_id(0)
    key = lax.broadcasted_iota(jnp.int32, (LANE, LANE), 0)
    q = lax.broadcasted_iota(jnp.int32, (LANE, LANE), 1)
    for rel in range(2):
        dt_ref[0, rel] = _t5_shifted(rel * LANE + q - key, tbl_ref, h)
    dt_ref[0, 2] = jnp.zeros((LANE, LANE), F32)
    u = lax.broadcasted_iota(jnp.int32, (CWIN, LANE), 0)
    qc = lax.broadcasted_iota(jnp.int32, (CWIN, LANE), 1)
    dc_ref[0] = _t5_shifted(qc - CMP_STRIDE * (u - CWIN // 2) - (CMP_BLOCK - 1), tbl_ref, h)


def bias_tiles(rel_bias):
    nh = rel_bias.shape[1]
    return pl.pallas_call(
        _bias_tiles_kernel,
        grid=(nh,),
        in_specs=[pl.BlockSpec(memory_space=pltpu.SMEM)],
        out_specs=[pl.BlockSpec((1, 3, LANE, LANE), lambda h: (h, 0, 0, 0)),
                   pl.BlockSpec((1, CWIN, LANE), lambda h: (h, 0, 0))],
        out_shape=[jax.ShapeDtypeStruct((nh, 3, LANE, LANE), F32),
                   jax.ShapeDtypeStruct((nh, CWIN, LANE), F32)],
        compiler_params=_cparams(("arbitrary",)),
        name="t5_bias_tiles",
    )(rel_bias)


def _compress_kernel(x_ref, pe_ref, w1_ref, b1_ref, w2_ref, b2_ref, g_ref, o_ref, *, half):
    kv = pl.program_id(0)
    nchunk = x_ref.shape[1] // CMP_STRIDE
    a = jnp.zeros((nchunk, CMP_HIDDEN), F32)
    b = jnp.zeros((nchunk, CMP_HIDDEN), F32)
    for p in range(CMP_STRIDE):
        xp = x_ref[0, pl.ds(p, nchunk, stride=CMP_STRIDE), :]
        rows = slice(p * HEAD_DIM, (p + 1) * HEAD_DIM)
        a = a + _dot((xp + pe_ref[0, p:p + 1, :]).astype(BF16), w1_ref[0, rows, :])
        q = CMP_STRIDE + p
        b = b + _dot((xp + pe_ref[0, q:q + 1, :]).astype(BF16),
                     w1_ref[0, half + p * HEAD_DIM:half + (p + 1) * HEAD_DIM, :])
    b_next = jnp.concatenate([b[1:], jnp.zeros((1, b.shape[1]), F32)], axis=0)
    hid = jax.nn.gelu(a + b_next + b1_ref[0])
    out = _dot(hid.astype(BF16), w2_ref[0]) + b2_ref[0]
    normed = out * lax.rsqrt(jnp.mean(out * out, axis=-1, keepdims=True) + EPS) * g_ref[...]
    out = jnp.where(kv == 0, normed, out)
    o_ref[0, 0] = out.astype(o_ref.dtype)


def compress_kv(proj, slab0, bsz, seq, cmp_pe, cmp_w1, cmp_b1, cmp_w2, cmp_b2, g_k):
    nslab, m, _ = proj.shape
    nchunk = seq // CMP_STRIDE
    half = CMP_STRIDE * HEAD_DIM
    del nslab, m
    return pl.pallas_call(
        functools.partial(_compress_kernel, half=half),
        grid=(2, bsz, NSA_GROUPS),
        in_specs=[pl.BlockSpec((1, seq, HEAD_DIM), lambda kv, b, g: (slab0 + 2 * kv + g, b, 0)),
                  pl.BlockSpec((1, CMP_BLOCK, HEAD_DIM), lambda kv, b, g: (kv, 0, 0)),
                  pl.BlockSpec((1, 2 * half, CMP_HIDDEN), lambda kv, b, g: (kv, 0, 0)),
                  pl.BlockSpec((1, 1, CMP_HIDDEN), lambda kv, b, g: (kv, 0, 0)),
                  pl.BlockSpec((1, CMP_HIDDEN, HEAD_DIM), lambda kv, b, g: (kv, 0, 0)),
                  pl.BlockSpec((1, 1, HEAD_DIM), lambda kv, b, g: (kv, 0, 0)),
                  pl.BlockSpec((1, HEAD_DIM), lambda kv, b, g: (0, 0))],
        out_specs=pl.BlockSpec((1, 1, nchunk, HEAD_DIM), lambda kv, b, g: (kv, g, b, 0)),
        out_shape=jax.ShapeDtypeStruct((2, NSA_GROUPS, bsz * nchunk, HEAD_DIM), BF16),
        compiler_params=_cparams(("arbitrary", "arbitrary", "arbitrary")),
        name="nsa_compress",
    )(proj, cmp_pe, cmp_w1.astype(BF16), cmp_b1.reshape(2, 1, CMP_HIDDEN), cmp_w2.astype(BF16),
      cmp_b2.reshape(2, 1, HEAD_DIM), g_k.reshape(1, HEAD_DIM))


KW = 512
PV_KEYS = 512


def _tile_lanes(x, n):
    return jnp.concatenate([x] * n, axis=1)


def _flash_init(m_ref, l_ref, acc_ref):
    m_ref[...] = jnp.full(m_ref.shape, NEG, F32)
    l_ref[...] = jnp.zeros(l_ref.shape, F32)
    acc_ref[...] = jnp.zeros(acc_ref.shape, F32)


def _zero_after(x):
    bits = pltpu.bitcast(x, jnp.int32)
    return lax.shift_right_logical(lax.shift_right_logical(bits, 16), 16).astype(F32)


def _flash_update(s, v_t, m_ref, l_ref, acc_ref, col_max=None, after=None):
    m_old = m_ref[...]
    if col_max is None:
        col_max = jnp.max(s, axis=0, keepdims=True)
    m_new = jnp.maximum(m_old, col_max)
    alpha = jnp.exp2(m_old - m_new)
    l_new = alpha * l_ref[...]
    acc = alpha * acc_ref[...]
    nk = s.shape[0]
    for k0 in range(0, nk, PV_KEYS):
        p = jnp.exp2(s[k0:k0 + PV_KEYS] - m_new)
        l_new = l_new + jnp.sum(p, axis=0, keepdims=True)
        acc = acc + _dot(v_t[:, k0:k0 + PV_KEYS], p.astype(BF16))
    l_ref[...] = l_new
    acc_ref[...] = acc
    m_ref[...] = m_new if after is None else m_new + _zero_after(after)


SAFE_LOG2_BOUND = 60.0


def _flash_accumulate(s, v_t, l_ref, acc_ref, after=None):
    l_new = l_ref[...]
    acc = acc_ref[...]
    for k0 in range(0, s.shape[0], PV_KEYS):
        p = jnp.exp2(s[k0:k0 + PV_KEYS])
        l_new = l_new + jnp.sum(p, axis=0, keepdims=True)
        acc = acc + _dot(v_t[:, k0:k0 + PV_KEYS], p.astype(BF16))
    if after is not None:
        l_new = l_new + jnp.max(_zero_after(after), axis=0, keepdims=True)
    l_ref[...] = l_new
    acc_ref[...] = acc


def _sum_result(l_ref, acc_ref):
    den = l_ref[...]
    ok = den > 0.0
    return acc_ref[...] * jnp.where(ok, 1.0 / jnp.where(ok, den, 1.0), 0.0)


def _inv_den(m, den):
    ok = m > 0.5 * NEG
    return jnp.where(ok, 1.0 / jnp.where(ok, den, 1.0), 0.0)


def _flash_result(m_ref, l_ref, acc_ref):
    return acc_ref[...] * _inv_den(m_ref[...], l_ref[...])


def _softmax_cols(s):
    m = jnp.max(s, axis=0, keepdims=True)
    p = jnp.exp2(s - m)
    return p * _inv_den(m, jnp.sum(p, axis=0, keepdims=True))


def _near_bias(dt_ref, heads, qi, kt0, ntile):
    rows = []
    for j in range(ntile):
        rel = jnp.clip(qi - (kt0 + j), 0, 2)
        rows.append(jnp.concatenate([dt_ref[h, rel] for h in heads], axis=1))
    return jnp.concatenate(rows, axis=0)


def _pipelined_chunks(n, qk_stage, soft_stage):
    @pl.when(n > 0)
    def _():
        qk_stage(0, 0)

    def pair(p, x):
        c = 2 * p
        ahead = qk_stage(c + 1, 1)
        soft_stage(c, 0, ahead)
        ahead = qk_stage(jnp.minimum(c + 2, n - 1), 0)
        soft_stage(c + 1, 1, ahead)
        return x

    lax.fori_loop(0, n // 2, pair, 0)

    @pl.when(n % 2 == 1)
    def _():
        soft_stage(n - 1, 0, None)


NSA_STATE = 9
GATE_ROWS = -(-3 * NSA_HPG // SUBLANE) * SUBLANE


def _nsa_kernel(bound_ref, qt_ref, gt_ref, kc_ref, vct_ref, ks_ref, vst_ref, kw_ref, vwt_ref,
                dt_ref, dc_ref, ext_ref, o_ref, *scratch, seq, nc):
    ng = NSA_GROUPS
    state = [scratch[NSA_STATE * g:NSA_STATE * (g + 1)] for g in range(ng)]
    qi = pl.program_id(1)
    q0 = qi * QB
    hpg = NSA_HPG
    ncp = kc_ref.shape[1]
    ns = seq // SEL_BLOCK
    group_heads = [[g * hpg + h for h in range(hpg)] for g in range(ng)]
    q_ts = [jnp.concatenate([qt_ref[h] for h in group_heads[g]], axis=1) for g in range(ng)]
    pad = CWIN // 2
    wkeys = WINDOW + QB
    start = pl.multiple_of(jnp.maximum(q0 - WINDOW, 0), LANE)
    r0 = pl.multiple_of(qi * (QB // CMP_STRIDE), 8)

    s_w = []
    for g in range(ng):
        sc_ref = state[g][0]
        sc_ref[0:pad, :] = jnp.zeros((pad, hpg * QB), F32)
        sc_ref[pad + ncp:2 * pad + ncp, :] = jnp.zeros((pad, hpg * QB), F32)
        sc_ref[pad:pad + ncp, :] = _dot(kc_ref[g], q_ts[g])
        s_w.append(_dot(kw_ref[g, pl.ds(start, wkeys), :], q_ts[g]))

    ci = lax.broadcasted_iota(jnp.int32, (ncp, QB), 0)
    tc = q0 + lax.broadcasted_iota(jnp.int32, (ncp, QB), 1)
    valid_c = (ci * CMP_STRIDE + CMP_BLOCK - 1 <= tc) & (ci < nc)
    madd_c = _tile_lanes(jnp.where(valid_c, 0.0, NEG), hpg)
    oc_t, p_sum = [], []
    for g in range(ng):
        sc_ref = state[g][0]
        sc_ref[pl.ds(r0, CWIN), :] = sc_ref[pl.ds(r0, CWIN), :] + jnp.concatenate(
            [dc_ref[h] for h in group_heads[g]], axis=1)
        p_c = _softmax_cols(sc_ref[pad:pad + ncp, :] + madd_c)
        oc_t.append(_dot(vct_ref[g], p_c.astype(BF16)))
        ps = p_c[:, 0:QB]
        for h in range(1, hpg):
            ps = ps + p_c[:, h * QB:(h + 1) * QB]
        p_sum.append(ps)

    dist_w = (q0 + lax.broadcasted_iota(jnp.int32, (wkeys, QB), 1)) - (
        start + lax.broadcasted_iota(jnp.int32, (wkeys, QB), 0))
    madd_w = _tile_lanes(jnp.where((dist_w >= 0) & (dist_w < WINDOW), 0.0, NEG), hpg)
    ow_t = []
    for g in range(ng):
        p_w = _softmax_cols(s_w[g] + _near_bias(dt_ref, group_heads[g], qi, start // LANE, wkeys // LANE) + madd_w)
        ow_t.append(_dot(vwt_ref[g, :, pl.ds(start, wkeys)], p_w.astype(BF16)))

    per = SEL_BLOCK // CMP_STRIDE
    blk = lax.broadcasted_iota(jnp.int32, (LANE, QB), 0)
    t = q0 + lax.broadcasted_iota(jnp.int32, (LANE, QB), 1)
    tb = t // SEL_BLOCK
    forced = (blk == 0) | (blk == tb) | (blk == tb - 1)
    blk_f = blk.astype(F32)
    scores = []
    for g in range(ng):
        ps_ref = state[g][1]
        ps_ref[0:SUBLANE, :] = jnp.zeros((SUBLANE, QB), F32)
        ps_ref[SUBLANE:SUBLANE + ncp, :] = p_sum[g]
        band = [ps_ref[pl.ds(SUBLANE + r, ns, stride=per), :] for r in range(-1, per)]
        imp = 0.5 * band[0] + band[1] + band[2] + band[3] + 0.5 * band[4]
        if ns < LANE:
            imp = jnp.concatenate([imp, jnp.zeros((LANE - ns, QB), F32)], axis=0)
        score = jnp.where(forced, FORCE, jnp.where(blk * SEL_BLOCK <= t, imp, NEG))
        scores.append(jnp.where(blk < ns, score, -jnp.inf))
    sels = [jnp.zeros((LANE, QB), F32) for _ in range(ng)]
    for _ in range(min(SEL_TOP_N, ns)):
        for g in range(ng):
            mx = jnp.max(scores[g], axis=0, keepdims=True)
            first = jnp.min(jnp.where(scores[g] == mx, blk_f, float(LANE)), axis=0, keepdims=True)
            pick = blk_f == first
            sels[g] = jnp.where(pick, 1.0, sels[g])
            scores[g] = jnp.where(pick, -jnp.inf, scores[g])
    sel_b = [s.astype(BF16) for s in sels]

    kpos = lax.broadcasted_iota(jnp.int32, (KW, QB), 0)
    tq = q0 + lax.broadcasted_iota(jnp.int32, (KW, QB), 1)
    bounded_ok = bound_ref[0] <= SAFE_LOG2_BOUND
    shift = jnp.where(bounded_ok, bound_ref[0], 0.0)
    c_near = jnp.maximum(qi - 1, 0) // (KW // LANE)

    def scores_of(g, c0):
        chosen = _dot(ext_ref[pl.ds(c0, KW), :], sel_b[g])
        return (chosen - 1.0) * (-NEG) - shift, _dot(ks_ref[g, pl.ds(c0, KW), :], q_ts[g])

    def attend(bounded):
        for g in range(ng):
            _flash_init(*state[g][2:5])

        def qk_stage(c, buf):
            c0 = pl.multiple_of(c * KW, KW)
            ahead = []
            for g in range(ng):
                madd, s = scores_of(g, c0)
                s = s + _tile_lanes(madd, hpg)
                state[g][5 + buf][...] = s
                if bounded:
                    ahead.append(s[KW - 8:KW])
                else:
                    ahead.append(jnp.max(s, axis=0, keepdims=True))
                    state[g][7 + buf][...] = ahead[-1]
            return ahead

        def soft_stage(c, buf, ahead):
            c0 = pl.multiple_of(c * KW, KW)
            for g in range(ng):
                m_ref, l_ref, acc_ref = state[g][2:5]
                after = None if ahead is None else ahead[g]
                v_t = vst_ref[g, :, pl.ds(c0, KW)]
                if bounded:
                    _flash_accumulate(state[g][5 + buf][...], v_t, l_ref, acc_ref, after=after)
                else:
                    _flash_update(state[g][5 + buf][...], v_t, m_ref, l_ref, acc_ref,
                                  col_max=state[g][7 + buf][...], after=after)

        def near_step(c, x):
            c0 = pl.multiple_of(c * KW, KW)
            causal = jnp.where(c0 + kpos <= tq, 0.0, NEG)
            scores = []
            for g in range(ng):
                madd, s = scores_of(g, c0)
                scores.append(s + _tile_lanes(madd + causal, hpg)
                              + _near_bias(dt_ref, group_heads[g], qi, c * (KW // LANE), KW // LANE))
            for g in range(ng):
                m_ref, l_ref, acc_ref = state[g][2:5]
                if bounded:
                    _flash_accumulate(scores[g], vst_ref[g, :, pl.ds(c0, KW)], l_ref, acc_ref)
                else:
                    _flash_update(scores[g], vst_ref[g, :, pl.ds(c0, KW)], m_ref, l_ref, acc_ref)
            return x

        _pipelined_chunks(c_near, qk_stage, soft_stage)
        lax.fori_loop(c_near, qi // (KW // LANE) + 1, near_step, 0)
        for g in range(ng):
            m_ref, l_ref, acc_ref = state[g][2:5]
            acc_ref[...] = _sum_result(l_ref, acc_ref) if bounded else _flash_result(m_ref, l_ref, acc_ref)

    pl.when(bounded_ok)(lambda: attend(True))
    pl.when(jnp.logical_not(bounded_ok))(lambda: attend(False))

    for g in range(ng):
        os_t = state[g][4][...]
        gates = jax.nn.sigmoid(gt_ref[g])
        for h in range(hpg):
            sl = slice(h * QB, (h + 1) * QB)
            o_t = (gates[3 * h:3 * h + 1] * oc_t[g][:, sl] + gates[3 * h + 1:3 * h + 2] * os_t[:, sl]
                   + gates[3 * h + 2:3 * h + 3] * ow_t[g][:, sl])
            hh = group_heads[g][h]
            o_ref[:, hh * HEAD_DIM:(hh + 1) * HEAD_DIM] = o_t.T.astype(o_ref.dtype)


def nsa_attention(logit_bound, q_t, gates_t, kc, vc_t, k_sw, v_sw_t, dt, dc, bsz, seq):
    nq = seq // QB
    ncp = seq // CMP_STRIDE
    nc = ncp - 1
    ns = seq // SEL_BLOCK
    assert ns <= LANE and seq >= WINDOW + QB and seq % KW == 0
    assert CMP_BLOCK == 2 * CMP_STRIDE and SEL_BLOCK == 4 * CMP_STRIDE
    expand =((np.arange(seq)[:, None] // SEL_BLOCK) == np.arange(LANE)[None, :]).astype(np.float32)
    ng = NSA_GROUPS
    once = dict(pipeline_mode=pl.Buffered(1))
    ks_spec = pl.BlockSpec((ng, seq, HEAD_DIM), lambda b, i: (0, b, 0), **once)
    kw_spec = pl.BlockSpec((ng, seq, HEAD_DIM), lambda b, i: (1, b, 0), **once)
    vs_spec = pl.BlockSpec((ng, HEAD_DIM, seq), lambda b, i: (0, 0, b), **once)
    vw_spec = pl.BlockSpec((ng, HEAD_DIM, seq), lambda b, i: (1, 0, b), **once)
    lanes = NSA_HPG * QB
    group_state = [pltpu.VMEM((ncp + CWIN, lanes), F32), pltpu.VMEM((ncp + SUBLANE, QB), F32),
                   pltpu.VMEM((1, lanes), F32), pltpu.VMEM((1, lanes), F32), pltpu.VMEM((HEAD_DIM, lanes), F32),
                   pltpu.VMEM((KW, lanes), F32), pltpu.VMEM((KW, lanes), F32),
                   pltpu.VMEM((1, lanes), F32), pltpu.VMEM((1, lanes), F32)]
    assert len(group_state) == NSA_STATE
    return pl.pallas_call(
        functools.partial(_nsa_kernel, seq=seq, nc=nc),
        grid=(bsz, nq),
        in_specs=[pl.BlockSpec(memory_space=pltpu.SMEM),
                  pl.BlockSpec((NSA_HEADS, HEAD_DIM, QB), lambda b, i: (0, 0, b * nq + i)),
                  pl.BlockSpec((ng, GATE_ROWS, QB), lambda b, i: (0, 0, b * nq + i)),
                  pl.BlockSpec((ng, ncp, HEAD_DIM), lambda b, i: (0, b, 0)),
                  pl.BlockSpec((ng, HEAD_DIM, ncp), lambda b, i: (0, 0, b)),
                  ks_spec, vs_spec, kw_spec, vw_spec,
                  pl.BlockSpec((NSA_HEADS, 3, LANE, LANE), lambda b, i: (0, 0, 0, 0)),
                  pl.BlockSpec((NSA_HEADS, CWIN, LANE), lambda b, i: (0, 0, 0)),
                  pl.BlockSpec((seq, LANE), lambda b, i: (0, 0))],
        out_specs=pl.BlockSpec((QB, NSA_HEADS * HEAD_DIM), lambda b, i: (b * nq + i, 0)),
        out_shape=jax.ShapeDtypeStruct((bsz * seq, NSA_HEADS * HEAD_DIM), BF16),
        scratch_shapes=group_state * ng,
        compiler_params=_cparams(("arbitrary", "arbitrary")),
        name="nsa_attention",
    )(jnp.reshape(logit_bound, (1,)).astype(F32), q_t, gates_t, kc, vc_t, k_sw, v_sw_t, k_sw, v_sw_t, dt, dc,
      jnp.asarray(expand, BF16))


MLA_HPS = 2


def _mla_kernel(bound_ref, qt_ref, k_ref, vt_ref, o_ref, *scratch):
    qi = pl.program_id(2)
    chains = [scratch[3 * h:3 * h + 3] for h in range(MLA_HPS)]
    sbuf = [scratch[(3 + b) * MLA_HPS:(4 + b) * MLA_HPS] for b in range(2)]
    cbuf = [scratch[(5 + b) * MLA_HPS:(6 + b) * MLA_HPS] for b in range(2)]
    c_diag = pl.multiple_of(qi * KW, KW)
    kpos = lax.broadcasted_iota(jnp.int32, (KW, KW), 0)
    tq = lax.broadcasted_iota(jnp.int32, (KW, KW), 1)
    dv = vt_ref.shape[1]

    def attend(bounded):
        for ch in chains:
            _flash_init(*ch)

        def qk_stage(c, buf):
            c0 = pl.multiple_of(c * KW, KW)
            ahead = []
            for h in range(MLA_HPS):
                s = _dot(k_ref[h, pl.ds(c0, KW), :], qt_ref[h])
                sbuf[buf][h][...] = s
                if bounded:
                    ahead.append(s[KW - 8:KW])
                else:
                    ahead.append(jnp.max(s, axis=0, keepdims=True))
                    cbuf[buf][h][...] = ahead[-1]
            return ahead

        def soft_stage(c, buf, ahead):
            c0 = pl.multiple_of(c * KW, KW)
            for h, (m_ref, l_ref, acc_ref) in enumerate(chains):
                after = None if ahead is None else ahead[h]
                if bounded:
                    _flash_accumulate(sbuf[buf][h][...], vt_ref[h, :, pl.ds(c0, KW)], l_ref, acc_ref, after=after)
                else:
                    _flash_update(sbuf[buf][h][...], vt_ref[h, :, pl.ds(c0, KW)], m_ref, l_ref, acc_ref,
                                  col_max=cbuf[buf][h][...], after=after)

        _pipelined_chunks(qi, qk_stage, soft_stage)
        causal = jnp.where(kpos <= tq, 0.0, NEG)
        scores = [_dot(k_ref[h, pl.ds(c_diag, KW), :], qt_ref[h]) + causal for h in range(MLA_HPS)]
        for h, (m_ref, l_ref, acc_ref) in enumerate(chains):
            if bounded:
                _flash_accumulate(scores[h], vt_ref[h, :, pl.ds(c_diag, KW)], l_ref, acc_ref)
                o_t = _sum_result(l_ref, acc_ref)
            else:
                _flash_update(scores[h], vt_ref[h, :, pl.ds(c_diag, KW)], m_ref, l_ref, acc_ref)
                o_t = _flash_result(m_ref, l_ref, acc_ref)
            o_ref[:, h * dv:(h + 1) * dv] = o_t.T.astype(o_ref.dtype)

    bounded_ok = bound_ref[0] <= SAFE_LOG2_BOUND
    pl.when(bounded_ok)(lambda: attend(True))
    pl.when(jnp.logical_not(bounded_ok))(lambda: attend(False))


def mla_attention(logit_bound, q_t, k, v_t, bsz, seq):
    nh, dqk, _ = q_t.shape
    dv = v_t.shape[1]
    nq = seq // KW
    hps = MLA_HPS
    state = [pltpu.VMEM((1, KW), F32), pltpu.VMEM((1, KW), F32), pltpu.VMEM((dv, KW), F32)] * hps
    state += [pltpu.VMEM((KW, KW), F32)] * (2 * hps)
    state += [pltpu.VMEM((1, KW), F32)] * (2 * hps)
    return pl.pallas_call(
        _mla_kernel,
        grid=(bsz, nh // hps, nq),
        in_specs=[pl.BlockSpec(memory_space=pltpu.SMEM),
                  pl.BlockSpec((hps, dqk, KW), lambda b, h, i: (h, 0, b * nq + i)),
                  pl.BlockSpec((hps, seq, dqk), lambda b, h, i: (h, b, 0)),
                  pl.BlockSpec((hps, dv, seq), lambda b, h, i: (h, 0, b))],
        out_specs=pl.BlockSpec((KW, hps * dv), lambda b, h, i: (b * nq + i, h)),
        out_shape=jax.ShapeDtypeStruct((bsz * seq, nh * dv), BF16),
        scratch_shapes=state,
        compiler_params=_cparams(("arbitrary", "arbitrary", "arbitrary")),
        name="mla_attention",
    )(jnp.reshape(logit_bound, (1,)).astype(F32), q_t, k, v_t)


INT_MIN = -2 ** 31
NEG_KEY = int(np.array(NEG, np.float32).view(np.int32)) ^ 0x7FFFFFFF
KEY_BITS = 32
SURE_BITS = 22


def _sort_key(x):
    bits = pltpu.bitcast(x + 0.0, jnp.int32)
    return jnp.where(bits < 0, bits ^ 0x7FFFFFFF, bits)


def _dsa_kernel(bound_ref, iqt_ref, iwt_ref, ik_ref, qt_ref, k_ref, vt_ref, dt_ref, o_ref,
                key_ref, *state, seq, k_sel):
    qi = pl.program_id(1)
    q0 = qi * QB
    n_chunk = (q0 + QB + KW - 1) // KW
    n_rest = seq - n_chunk * KW
    kpos = lax.broadcasted_iota(jnp.int32, (KW, QB), 0)
    tq = q0 + lax.broadcasted_iota(jnp.int32, (KW, QB), 1)
    hpp = KW // QB

    def score_chunk(c):
        c0 = pl.multiple_of(c * KW, KW)
        ikc = ik_ref[pl.ds(c0, KW), :]
        acc = jnp.zeros((KW, QB), F32)
        for piece in range(IDX_HEADS // hpp):
            sl = slice(piece * KW, (piece + 1) * KW)
            s = jnp.maximum(_dot(ikc, iqt_ref[0, :, sl]), 0.0) * iwt_ref[0, :, sl]
            for j in range(hpp):
                acc = acc + s[:, j * QB:(j + 1) * QB]
        acc = jnp.where(c0 + kpos <= tq, acc, NEG)
        key_ref[pl.ds(c0, KW), :] = _sort_key(acc)

    def score_pair(j, x):
        score_chunk(2 * j)
        score_chunk(jnp.minimum(2 * j + 1, n_chunk - 1))
        return x

    lax.fori_loop(0, (n_chunk + 1) // 2, score_pair, 0)

    def count(pred):
        def chunk_hits(c):
            c0 = pl.multiple_of(c * KW, KW)
            hit = jnp.where(pred(key_ref[pl.ds(c0, KW), :], c0), 1.0, 0.0)
            parts = [hit[SUBLANE * i:SUBLANE * (i + 1)] for i in range(KW // SUBLANE)]
            while len(parts) > 1:
                parts = [parts[i] + parts[i + 1] for i in range(0, len(parts), 2)]
            return parts[0]

        def body(j, acc):
            second = 2 * j + 1
            weight = jnp.where(second < n_chunk, 1.0, 0.0)
            return acc + chunk_hits(2 * j) + chunk_hits(jnp.minimum(second, n_chunk - 1)) * weight

        acc = lax.fori_loop(0, (n_chunk + 1) // 2, body, jnp.zeros((SUBLANE, QB), F32))
        return jnp.sum(acc, axis=0, keepdims=True)

    rest = n_rest.astype(F32)
    kf = float(k_sel)

    def bit_step(i, st):
        u, thr_s, settled = st
        bit = jnp.left_shift(jnp.int32(1), KEY_BITS - 1 - i)
        trial = (u | bit) ^ INT_MIN
        cnt = count(lambda keys, c0: keys >= trial) + jnp.where(NEG_KEY >= trial, rest, 0.0)
        new = (cnt == kf) & (settled < 0.5)
        return (jnp.where(cnt >= kf, u | bit, u), jnp.where(new, trial, thr_s), jnp.where(new, 1.0, settled))

    st = (jnp.zeros((1, QB), jnp.int32), jnp.zeros((1, QB), jnp.int32), jnp.zeros((1, QB), F32))
    st = lax.fori_loop(0, SURE_BITS, bit_step, st)
    _, (u, thr_s, settled) = lax.while_loop(
        lambda c: (c[0] < KEY_BITS) & (jnp.min(c[1][2]) < 0.5),
        lambda c: (c[0] + 1, bit_step(c[0], c[1])), (jnp.int32(SURE_BITS), st))
    is_settled = settled > 0.5
    thr = jnp.where(is_settled, thr_s, u ^ INT_MIN)

    def edge_counts():
        return (count(lambda keys, c0: keys > thr) + jnp.where(NEG_KEY > thr, rest, 0.0),
                count(lambda keys, c0: keys >= thr) + jnp.where(NEG_KEY >= thr, rest, 0.0))

    zero_cnt = jnp.zeros((1, QB), F32)
    cnt_gt, cnt_ge = lax.cond(jnp.min(settled) > 0.5, lambda: (zero_cnt, zero_cnt), edge_counts)
    need = kf - cnt_gt
    tie_q = (cnt_ge > kf) & (thr != NEG_KEY) & jnp.logical_not(is_settled)
    idx_bits = (seq - 1).bit_length()
    no_cut = 2 ** 30

    def tie_cut():
        def idx_step(i, x):
            bit = jnp.left_shift(jnp.int32(1), idx_bits - 1 - i)
            trial = x | bit
            f = count(lambda keys, c0: (keys == thr) & (c0 + kpos < trial))
            return jnp.where(f <= need - 1.0, trial, x)
        return lax.fori_loop(0, idx_bits, idx_step, jnp.zeros((1, QB), jnp.int32))

    any_tie = jnp.max(jnp.where(tie_q, 1.0, 0.0)) > 0.0
    x_cut = lax.cond(any_tie, tie_cut, lambda: jnp.full((1, QB), no_cut, jnp.int32))
    x_cut = jnp.where(tie_q, x_cut, no_cut)

    bounded_ok = bound_ref[0] <= SAFE_LOG2_BOUND
    shift = jnp.where(bounded_ok, bound_ref[0], 0.0)

    def mask_add(c0):
        keys = key_ref[pl.ds(c0, KW), :]
        pos = c0 + kpos
        chosen = (keys > thr) | ((keys == thr) & (pos <= x_cut))
        return _tile_lanes(jnp.where(chosen & (pos <= tq), -shift, NEG), DSA_HPG)

    c_near = jnp.maximum(qi - 1, 0) // (KW // LANE)
    ng = DSA_KV_HEADS
    chains = [state[3 * g:3 * g + 3] for g in range(ng)]
    sbuf = [state[(3 + b) * ng:(4 + b) * ng] for b in range(2)]
    cbuf = [state[(5 + b) * ng:(6 + b) * ng] for b in range(2)]
    group_heads = [[g * DSA_HPG + h for h in range(DSA_HPG)] for g in range(ng)]

    def raw_scores(c0, g):
        q_t = jnp.concatenate([qt_ref[h] for h in group_heads[g]], axis=1)
        return _dot(k_ref[g, pl.ds(c0, KW), :], q_t)

    def attend(bounded):
        for ch in chains:
            _flash_init(*ch)

        def qk_stage(c, buf):
            c0 = pl.multiple_of(c * KW, KW)
            madd = mask_add(c0)
            ahead = []
            for g in range(ng):
                s = raw_scores(c0, g) + madd
                sbuf[buf][g][...] = s
                if bounded:
                    ahead.append(s[KW - 8:KW])
                else:
                    ahead.append(jnp.max(s, axis=0, keepdims=True))
                    cbuf[buf][g][...] = ahead[-1]
            return ahead

        def soft_stage(c, buf, ahead):
            c0 = pl.multiple_of(c * KW, KW)
            for g, (m_ref, l_ref, acc_ref) in enumerate(chains):
                after = None if ahead is None else ahead[g]
                if bounded:
                    _flash_accumulate(sbuf[buf][g][...], vt_ref[g, :, pl.ds(c0, KW)], l_ref, acc_ref, after=after)
                else:
                    _flash_update(sbuf[buf][g][...], vt_ref[g, :, pl.ds(c0, KW)], m_ref, l_ref, acc_ref,
                                  col_max=cbuf[buf][g][...], after=after)

        _pipelined_chunks(c_near, qk_stage, soft_stage)

        def near_step(c, x):
            c0 = pl.multiple_of(c * KW, KW)
            madd = mask_add(c0)
            scores = [raw_scores(c0, g) + madd
                      + _near_bias(dt_ref, group_heads[g], qi, c * (KW // LANE), KW // LANE)
                      for g in range(ng)]
            for g, (m_ref, l_ref, acc_ref) in enumerate(chains):
                if bounded:
                    _flash_accumulate(scores[g], vt_ref[g, :, pl.ds(c0, KW)], l_ref, acc_ref)
                else:
                    _flash_update(scores[g], vt_ref[g, :, pl.ds(c0, KW)], m_ref, l_ref, acc_ref)
            return x

        lax.fori_loop(c_near, n_chunk, near_step, 0)
        for g, (m_ref, l_ref, acc_ref) in enumerate(chains):
            o_t = _sum_result(l_ref, acc_ref) if bounded else _flash_result(m_ref, l_ref, acc_ref)
            for h in range(DSA_HPG):
                hh = group_heads[g][h]
                o_ref[:, hh * HEAD_DIM:(hh + 1) * HEAD_DIM] = o_t[:, h * QB:(h + 1) * QB].T.astype(o_ref.dtype)

    pl.when(bounded_ok)(lambda: attend(True))
    pl.when(jnp.logical_not(bounded_ok))(lambda: attend(False))


def _indexer_layouts(get_slab, c_ref, sa_ref, sb_ref, iqt_ref, ik_ref, iwt_ref):
    nslab_q = IDX_HEADS * IDX_DIM // LANE
    per = LANE // IDX_DIM
    half = IDX_ROPE // 2
    zrows = jnp.zeros((LANE - IDX_DIM, QB), F32)

    def rope_slab(x, c, sa, sb):
        return x * c + pltpu.roll(x, LANE - half, axis=1) * sa + pltpu.roll(x, half, axis=1) * sb

    for t in range(iqt_ref.shape[0]):
        rows = slice(t * QB, (t + 1) * QB)
        c, sa, sb = c_ref[rows, :], sa_ref[rows, :], sb_ref[rows, :]
        cols = []
        for s in range(nslab_q):
            x_t = (rope_slab(get_slab(s, rows), c, sa, sb) * IDX_DIM ** -0.5).T
            for j in range(per):
                cols.append(jnp.concatenate([x_t[j * IDX_DIM:(j + 1) * IDX_DIM], zrows], axis=0))
        iqt_ref[t] = jnp.concatenate(cols, axis=1).astype(iqt_ref.dtype)
        tail = get_slab(nslab_q, rows)
        lane = lax.broadcasted_iota(jnp.int32, (QB, LANE), 1)
        ik_ref[rows, :] = jnp.where(lane < IDX_DIM, rope_slab(tail, c, sa, sb), 0.0).astype(ik_ref.dtype)
        w_t = (tail * IDX_HEADS ** -0.5).T
        iwt_ref[t] = jnp.concatenate([w_t[IDX_DIM + h:IDX_DIM + h + 1, :] for h in range(IDX_HEADS)], axis=1)


def dsa_attention(logit_bound, iq_t, iw_t, ik, q_t, k, v_t, dt, bsz, seq):
    nq = seq // QB
    k_sel = min(DSA_TOPK_MAX, seq // 4)
    assert seq % KW == 0
    lanes = DSA_HPG * QB
    return pl.pallas_call(
        functools.partial(_dsa_kernel, seq=seq, k_sel=k_sel),
        grid=(bsz, nq),
        in_specs=[pl.BlockSpec(memory_space=pltpu.SMEM),
                  pl.BlockSpec((1, LANE, IDX_HEADS * QB), lambda b, i: (b * nq + i, 0, 0)),
                  pl.BlockSpec((1, 1, IDX_HEADS * QB), lambda b, i: (b * nq + i, 0, 0)),
                  pl.BlockSpec((seq, LANE), lambda b, i: (b, 0)),
                  pl.BlockSpec((DSA_HEADS, HEAD_DIM, QB), lambda b, i: (0, 0, b * nq + i)),
                  pl.BlockSpec((DSA_KV_HEADS, seq, HEAD_DIM), lambda b, i: (0, b, 0),
                               pipeline_mode=pl.Buffered(1)),
                  pl.BlockSpec((DSA_KV_HEADS, HEAD_DIM, seq), lambda b, i: (0, 0, b),
                               pipeline_mode=pl.Buffered(1)),
                  pl.BlockSpec((DSA_HEADS, 3, LANE, LANE), lambda b, i: (0, 0, 0, 0),
                               pipeline_mode=pl.Buffered(1))],
        out_specs=pl.BlockSpec((QB, DSA_HEADS * HEAD_DIM), lambda b, i: (b * nq + i, 0)),
        out_shape=jax.ShapeDtypeStruct((bsz * seq, DSA_HEADS * HEAD_DIM), BF16),
        scratch_shapes=[pltpu.VMEM((seq, QB), jnp.int32)]
        + [pltpu.VMEM((1, lanes), F32), pltpu.VMEM((1, lanes), F32),
           pltpu.VMEM((HEAD_DIM, lanes), F32)] * DSA_KV_HEADS
        + [pltpu.VMEM((KW, lanes), F32)] * (2 * DSA_KV_HEADS)
        + [pltpu.VMEM((1, lanes), F32)] * (2 * DSA_KV_HEADS),
        compiler_params=_cparams(("arbitrary", "arbitrary")),
        name="dsa_attention",
    )(jnp.reshape(logit_bound, (1,)).astype(F32), iq_t, iw_t, ik, q_t, k, v_t, dt)


def _rope_tables(seq, dim):
    half = dim // 2
    inv = ROPE_THETA ** (-jnp.arange(half, dtype=F32) / half)
    ang = jnp.arange(seq, dtype=F32)[:, None] * inv[None, :]
    return jnp.cos(ang), jnp.sin(ang)


def _logit_bound(gq, gk, dim, scale):
    return dim * scale * jnp.max(jnp.abs(gq)) * jnp.max(jnp.abs(gk)) * (1.0 + 2.0 ** -7)


def _pad_cols(w, n):
    return jnp.pad(w, ((0, 0), (0, n - w.shape[1])))


def _t(x):
    return jnp.swapaxes(x, -1, -2)


def _even_mixer(h, x2, gate, next_norm, dt, dc, bias_bound, bsz, seq, w_in, w_out, nsa_qk_g, cmp_pe, cmp_w1, cmp_b1,
                cmp_w2, cmp_b2, q_norm_g, kv_norm_g, w_uq, w_ukv, nope_g, rope_g):
    m = bsz * seq
    nq_cols = NSA_HEADS * HEAD_DIM
    nkv_cols = 6 * NSA_GROUPS * HEAD_DIM
    ngate = 3 * NSA_HEADS
    o_gate = nq_cols + nkv_cols
    o_cq = o_gate + ngate
    o_ckv = o_cq + MLA_Q_RANK
    o_kpe = o_ckv + MLA_KV_RANK
    gw = NSA_GROUPS * HEAD_DIM
    kvw = [w_in[:, nq_cols + i * gw:nq_cols + (i + 1) * gw] for i in range(6)]
    scale = HEAD_DIM ** -0.5 * LOG2E
    tail = jnp.concatenate([w_in[:, o_kpe:], w_in[:, o_gate:o_cq]], axis=1)
    w_r = jnp.concatenate([kvw[0], kvw[1], w_in[:, o_cq:o_kpe], _pad_cols(tail, LANE)], axis=1).astype(BF16)
    dq = MLA_NOPE + MLA_ROPE
    wq = w_uq.reshape(MLA_Q_RANK, MLA_HEADS, dq)
    wq_r = jnp.concatenate([wq[:, :, :MLA_NOPE].reshape(MLA_Q_RANK, -1),
                            wq[:, :, MLA_NOPE:].reshape(MLA_Q_RANK, -1)], axis=1).astype(BF16)
    mscale = dq ** -0.5 * LOG2E
    side = [jnp.sqrt(MLA_NOPE * jnp.max(jnp.abs(nope_g[i])) ** 2 + MLA_ROPE * jnp.max(jnp.abs(rope_g[i])) ** 2)
            for i in range(2)]
    mla_bound = mscale * side[0] * side[1] * (1.0 + 2.0 ** -7)
    mla_shift = jnp.where(mla_bound <= SAFE_LOG2_BOUND, mla_bound, 0.0)
    nraw = 2 * NSA_GROUPS
    q_t, k_sw, v_sw_t, proj, q_mla_t, k_mla, v_mla_t = proj_even(
        h, w_in[:, :nq_cols].astype(BF16), nsa_qk_g[0] * scale,
        jnp.concatenate([kvw[2], kvw[4]], axis=1).astype(BF16), nsa_qk_g[1],
        jnp.concatenate([kvw[3], kvw[5]], axis=1).astype(BF16), w_r, nraw, seq, q_norm_g, kv_norm_g, wq_r,
        w_ukv.astype(BF16), nope_g, rope_g, mscale, mla_shift)
    kvc = compress_kv(proj, 0, bsz, seq, cmp_pe, cmp_w1, cmp_b1, cmp_w2, cmp_b2, nsa_qk_g[1])
    tail_v = proj[nraw]
    gates = tail_v[:, MLA_ROPE:MLA_ROPE + ngate].reshape(m, NSA_GROUPS, 3 * NSA_HPG)
    gates_t = jnp.pad(jnp.transpose(gates, (1, 2, 0)), ((0, 0), (0, GATE_ROWS - 3 * NSA_HPG), (0, 0)))
    nsa_bound = _logit_bound(nsa_qk_g[0], nsa_qk_g[1], HEAD_DIM, scale) + bias_bound
    o_nsa = nsa_attention(nsa_bound, q_t, gates_t, kvc[0], _t(kvc[1]), k_sw, v_sw_t,
                          dt[:NSA_HEADS], dc[:NSA_HEADS], bsz, seq)

    o_mla = mla_attention(mla_bound, q_mla_t, k_mla, v_mla_t, bsz, seq)
    w_o = w_out.astype(BF16)
    return resproj([(o_nsa, w_o[:nq_cols]), (o_mla, w_o[nq_cols:])], x2, gate, seq, next_norm)


def _odd_mixer(h, x2, gate, next_norm, dt, bias_bound, bsz, seq, w_in, w_out, qk_g):
    nq = DSA_HEADS * HEAD_DIM
    nkv = DSA_KV_HEADS * HEAD_DIM
    niq = IDX_HEADS * IDX_DIM
    o_k, o_v, o_iq = nq, nq + nkv, nq + 2 * nkv
    w_idx = w_in[:, o_iq:]
    q_t, k, v_t, iq_t, ik, iw_t = proj_qkv_indexer(
        h, w_in[:, :o_k].astype(BF16), qk_g[0] * (HEAD_DIM ** -0.5 * LOG2E), w_in[:, o_k:o_v].astype(BF16), qk_g[1],
        w_in[:, o_v:o_iq].astype(BF16), _pad_cols(w_idx, niq + LANE).astype(BF16), seq)
    bound = _logit_bound(qk_g[0], qk_g[1], HEAD_DIM, HEAD_DIM ** -0.5 * LOG2E) + bias_bound
    o = dsa_attention(bound, iq_t, iw_t, ik, q_t, k, v_t, dt, bsz, seq)
    return resproj([(o, w_out.astype(BF16))], x2, gate, seq, next_norm)


def _conv_ffn(h, x2, gate, next_norm, seq, w_up_all, layer, conv_w, conv_b, w_down):
    a = ffn_up(h, w_up_all, layer, conv_w, conv_b, seq)
    return resproj([(a, w_down.astype(BF16))], x2, gate, seq, next_norm)


def kernel(x, c, rel_bias, ada_w, ada_b, norm_g, ev_w_in, ev_w_out, nsa_qk_g, cmp_pe, cmp_w1, cmp_b1, cmp_w2, cmp_b2, mla_q_norm_g, mla_kv_norm_g, mla_w_uq, mla_w_ukv, mla_nope_g, mla_rope_g, od_w_in, od_w_out, dsa_qk_g, ffn_w_up, ffn_conv_w, ffn_conv_b, ffn_w_down):
    bsz, seq, d = x.shape
    depth = ada_w.shape[0]
    x2 = x.reshape(bsz * seq, d)
    mods = ada_all(c, ada_w, ada_b)
    dt, dc = bias_tiles(rel_bias)
    bias_bound = 2.0 * LOG2E * jnp.max(jnp.abs(rel_bias))
    def norm_of(i, sub):
        if i >= depth:
            return None
        shift, scale, _ = jnp.split(mods[i, sub], 3, axis=-1)
        return norm_g[i, sub], scale, shift

    g0, scale0, shift0 = norm_of(0, 0)
    h = modnorm(x2, g0, scale0, shift0, seq)
    for i in range(depth):
        j = i // 2
        gate = jnp.split(mods[i, 0], 3, axis=-1)[2]
        if i % 2 == 0:
            x2, h = _even_mixer(h, x2, gate, norm_of(i, 1), dt, dc, bias_bound, bsz, seq, ev_w_in[j],
                                ev_w_out[j], nsa_qk_g[j], cmp_pe[j], cmp_w1[j], cmp_b1[j], cmp_w2[j],
                                cmp_b2[j], mla_q_norm_g[j], mla_kv_norm_g[j], mla_w_uq[j], mla_w_ukv[j],
                                mla_nope_g[j], mla_rope_g[j])
        else:
            x2, h = _odd_mixer(h, x2, gate, norm_of(i, 1), dt, bias_bound, bsz, seq, od_w_in[j], od_w_out[j],
                               dsa_qk_g[j])
        gate = jnp.split(mods[i, 1], 3, axis=-1)[2]
        x2, h = _conv_ffn(h, x2, gate, norm_of(i + 1, 0), seq, ffn_w_up, i, ffn_conv_w[i], ffn_conv_b[i],
                          ffn_w_down[i])
    return x2.reshape(bsz, seq, d)
```
